```python
import jax, jax.numpy as jnp
from jax import lax
import numpy as np

D_MODEL = 2048
BATCH = 8
SEQ = 8192
DEPTH = 1

HEAD_DIM = 128
ATTN_WIDTH = D_MODEL // 2
ATTN_HEADS = ATTN_WIDTH // HEAD_DIM
POOL_WIDTH = D_MODEL // 2
POOL_WINDOWS = (2, 4, 8, 16)
POOL_GROUPS = len(POOL_WINDOWS)
POOL_GROUP_WIDTH = POOL_WIDTH // POOL_GROUPS
DILATED_PATTERNS = ((128, 1), (512, 4), (2048, 16))
SUB_BLOCK = 128
D_FF = 4 * D_MODEL
ROPE_THETA = 10000.0
LN_EPS = 1e-5
DEEPNORM_ALPHA = (2.0 * DEPTH) ** 0.25
DEEPNORM_BETA = (8.0 * DEPTH) ** -0.25
IN_SPLITS = (ATTN_WIDTH, 2 * ATTN_WIDTH, 3 * ATTN_WIDTH, 3 * ATTN_WIDTH + POOL_WIDTH,
             3 * ATTN_WIDTH + POOL_WIDTH + D_MODEL)
IN_WIDTH = 3 * ATTN_WIDTH + POOL_WIDTH + 2 * D_MODEL

kernel_name = "dilated_attn_pool_gated_hybrid_deepnorm"


def layer_norm(x, g, b):
    xf = x.astype(jnp.float32)
    mu = jnp.mean(xf, axis=-1, keepdims=True)
    var = jnp.mean(jnp.square(xf - mu), axis=-1, keepdims=True)
    return ((xf - mu) * lax.rsqrt(var + LN_EPS) * g.astype(jnp.float32) + b.astype(jnp.float32)).astype(x.dtype)


def rope(t, positions):
    half = HEAD_DIM // 2
    inv_freq = ROPE_THETA ** (-jnp.arange(half, dtype=jnp.float32) / half)
    ang = positions.astype(jnp.float32)[..., None] * inv_freq
    cos = jnp.cos(ang)[:, :, None, :]
    sin = jnp.sin(ang)[:, :, None, :]
    t1 = t[..., :half].astype(jnp.float32)
    t2 = t[..., half:].astype(jnp.float32)
    return jnp.concatenate([t1 * cos - t2 * sin, t2 * cos + t1 * sin], axis=-1).astype(t.dtype)


def dilated_window_attention(q, k, v, window, dilation):
    B, S, H, Dh = q.shape
    span = window // dilation
    blk = SUB_BLOCK
    assert span <= blk
    unit = dilation * blk
    s_pad = -(-S // unit) * unit
    m_len = s_pad // dilation
    nb = m_len // blk

    def to_blocks(t):
        t = jnp.pad(t, ((0, 0), (0, s_pad - S), (0, 0), (0, 0)))
        t = t.reshape(B, m_len, dilation, H, Dh).transpose(0, 2, 1, 3, 4)
        return t.reshape(B, dilation, nb, blk, H, Dh)

    def with_prev(t):
        prev = jnp.pad(t, ((0, 0), (0, 0), (1, 0), (0, 0), (0, 0), (0, 0)))[:, :, :-1]
        return jnp.concatenate([prev, t], axis=3)

    qb = to_blocks(q)
    kw = with_prev(to_blocks(k))
    vw = with_prev(to_blocks(v))
    s = jnp.einsum('brnqhd,brnkhd->brnhqk', qb, kw,
                   preferred_element_type=jnp.float32) * (HEAD_DIM ** -0.5)
    qi = jnp.arange(blk)[:, None]
    kj = jnp.arange(2 * blk)[None, :]
    dist = qi + blk - kj
    band = (dist >= 0) & (dist <= span)
    valid = band[None] & ((jnp.arange(nb)[:, None, None] > 0) | (kj >= blk)[None])
    s = jnp.where(valid[None, None, :, None], s, -jnp.inf)
    mx = jnp.max(s, axis=-1, keepdims=True)
    p = jnp.exp(s - mx)
    l = jnp.sum(p, axis=-1)
    o = jnp.einsum('brnhqk,brnkhd->brnqhd', p, vw.astype(jnp.float32))
    o = o / jnp.swapaxes(l, 3, 4)[..., None]
    lse = jnp.swapaxes(mx[..., 0] + jnp.log(l), 3, 4)
    o = o.reshape(B, dilation, m_len, H, Dh).transpose(0, 2, 1, 3, 4).reshape(B, s_pad, H, Dh)[:, :S]
    lse = lse.reshape(B, dilation, m_len, H).transpose(0, 2, 1, 3).reshape(B, s_pad, H)[:, :S]
    return o, lse


def pool_mixer(u, w_pool, pool_scale):
    B, S, _ = u.shape
    ug = u.reshape(B, S, POOL_GROUPS, POOL_GROUP_WIDTH)
    pooled = []
    for g, w in enumerate(POOL_WINDOWS):
        xg = ug[:, :, g].astype(jnp.float32)
        c = jnp.cumsum(xg, axis=1)
        c_lag = jnp.pad(c, ((0, 0), (w, 0), (0, 0)))[:, :S]
        count = jnp.minimum(jnp.arange(1, S + 1), w).astype(jnp.float32)[None, :, None]
        pooled.append((c - c_lag) / count - xg)
    p = jnp.stack(pooled, axis=2).astype(u.dtype)
    y = jnp.einsum('bsgc,gcd->bsgd', p, w_pool).reshape(B, S, POOL_WIDTH)
    return y * pool_scale


def _fwd_setup_inputs(seed: int = 0) -> dict:
    key = jax.random.key(seed)
    ks = jax.random.split(key, 20)
    f32 = jnp.float32

    def nrm(k, shape, fan_in, gain=1.0):
        return jax.random.normal(k, shape, f32) * (gain * fan_in ** -0.5)

    x = jax.random.normal(ks[0], (BATCH, SEQ, D_MODEL), f32)
    positions = (jnp.arange(SEQ, dtype=jnp.int32)[None, :]
                 + jax.random.randint(ks[1], (BATCH, 1), 0, 1024, dtype=jnp.int32))
    w_in = jnp.concatenate([
        nrm(ks[2], (DEPTH, D_MODEL, 2 * ATTN_WIDTH), D_MODEL),
        nrm(ks[3], (DEPTH, D_MODEL, ATTN_WIDTH), D_MODEL, DEEPNORM_BETA),
        nrm(ks[4], (DEPTH, D_MODEL, POOL_WIDTH), D_MODEL),
        nrm(ks[5], (DEPTH, D_MODEL, 2 * D_MODEL), D_MODEL),
    ], axis=-1)
    w_pool = nrm(ks[6], (DEPTH, POOL_GROUPS, POOL_GROUP_WIDTH, POOL_GROUP_WIDTH), POOL_GROUP_WIDTH)
    pool_scale = 1.0 + 0.1 * jax.random.normal(ks[7], (DEPTH, POOL_WIDTH), f32)
    w_branch_attn = nrm(ks[8], (DEPTH, ATTN_WIDTH, D_MODEL), ATTN_WIDTH, DEEPNORM_BETA)
    w_branch_pool = nrm(ks[9], (DEPTH, POOL_WIDTH, D_MODEL), POOL_WIDTH, DEEPNORM_BETA)
    w_out = nrm(ks[10], (DEPTH, D_MODEL, D_MODEL), D_MODEL, DEEPNORM_BETA)
    ln_mix_g = 1.0 + 0.05 * jax.random.normal(ks[11], (DEPTH, D_MODEL), f32)
    ln_mix_b = 0.02 * jax.random.normal(ks[12], (DEPTH, D_MODEL), f32)
    w_ff1 = nrm(ks[13], (DEPTH, D_MODEL, D_FF), D_MODEL, DEEPNORM_BETA)
    w_ff2 = nrm(ks[14], (DEPTH, D_FF, D_MODEL), D_FF, DEEPNORM_BETA)
    ln_ff_g = 1.0 + 0.05 * jax.random.normal(ks[15], (DEPTH, D_MODEL), f32)
    ln_ff_b = 0.02 * jax.random.normal(ks[16], (DEPTH, D_MODEL), f32)
    return {"x": x, "positions": positions, "w_in": w_in, "w_pool": w_pool,
            "pool_scale": pool_scale, "w_branch_attn": w_branch_attn,
            "w_branch_pool": w_branch_pool, "w_out": w_out, "ln_mix_g": ln_mix_g,
            "ln_mix_b": ln_mix_b, "w_ff1": w_ff1, "w_ff2": w_ff2,
            "ln_ff_g": ln_ff_g, "ln_ff_b": ln_ff_b}


def _fwd_reference(x, positions, w_in, w_pool, pool_scale, w_branch_attn, w_branch_pool, w_out,
              ln_mix_g, ln_mix_b, w_ff1, w_ff2, ln_ff_g, ln_ff_b):
    B, S, _ = x.shape
    for layer in range(DEPTH):
        h = x @ w_in[layer]
        q, k, v, u, gate_attn, gate_pool = jnp.split(h, IN_SPLITS, axis=-1)
        q = rope(q.reshape(B, S, ATTN_HEADS, HEAD_DIM), positions)
        k = rope(k.reshape(B, S, ATTN_HEADS, HEAD_DIM), positions)
        v = v.reshape(B, S, ATTN_HEADS, HEAD_DIM)
        outs, lses = [], []
        for window, dilation in DILATED_PATTERNS:
            o_g, lse_g = dilated_window_attention(q, k, v, window, dilation)
            outs.append(o_g)
            lses.append(lse_g)
        mix_w = jax.nn.softmax(jnp.stack(lses, axis=0), axis=0)
        o_attn = jnp.einsum('pbsh,pbshd->bshd', mix_w, jnp.stack(outs, axis=0))
        y_attn = o_attn.reshape(B, S, ATTN_WIDTH).astype(x.dtype) @ w_branch_attn[layer]
        y_pool = pool_mixer(u, w_pool[layer], pool_scale[layer]) @ w_branch_pool[layer]
        merged = jax.nn.sigmoid(gate_attn) * y_attn + jax.nn.sigmoid(gate_pool) * y_pool
        mix = merged @ w_out[layer]
        x = layer_norm(DEEPNORM_ALPHA * x + mix, ln_mix_g[layer], ln_mix_b[layer])
        f = jnp.square(jax.nn.relu(x @ w_ff1[layer])) @ w_ff2[layer]
        x = layer_norm(DEEPNORM_ALPHA * x + f, ln_ff_g[layer], ln_ff_b[layer])
    return x


import jax as _jax
import jax.numpy as _jnp

TWIN_FORMAT = 'train_step'
FWD_PARAMS = ['x', 'positions', 'w_in', 'w_pool', 'pool_scale', 'w_branch_attn', 'w_branch_pool', 'w_out', 'ln_mix_g', 'ln_mix_b', 'w_ff1', 'w_ff2', 'ln_ff_g', 'ln_ff_b']
TWIN_WEIGHTS = ['w_in', 'w_pool', 'pool_scale', 'w_branch_attn', 'w_branch_pool', 'w_out', 'ln_mix_g', 'ln_mix_b', 'w_ff1', 'w_ff2', 'ln_ff_g', 'ln_ff_b']
TWIN_DIFF_INPUT = 'x'
TWIN_INPUTS = ['x', 'positions', 'w_in', 'w_pool', 'pool_scale', 'w_branch_attn', 'w_branch_pool', 'w_out', 'ln_mix_g', 'ln_mix_b', 'w_ff1', 'w_ff2', 'ln_ff_g', 'ln_ff_b', 'loss_target', 'm_w_in', 'm_w_pool', 'm_pool_scale', 'm_w_branch_attn', 'm_w_branch_pool', 'm_w_out', 'm_ln_mix_g', 'm_ln_mix_b', 'm_w_ff1', 'm_w_ff2', 'm_ln_ff_g', 'm_ln_ff_b', 'v_w_in', 'v_w_pool', 'v_pool_scale', 'v_w_branch_attn', 'v_w_branch_pool', 'v_w_out', 'v_ln_mix_g', 'v_ln_mix_b', 'v_w_ff1', 'v_w_ff2', 'v_ln_ff_g', 'v_ln_ff_b']
TWIN_OUTPUTS = ['loss', 'grad_x', 'grad_w_in', 'grad_w_pool', 'grad_pool_scale', 'grad_w_branch_attn', 'grad_w_branch_pool', 'grad_w_out', 'grad_ln_mix_g', 'grad_ln_mix_b', 'grad_w_ff1', 'grad_w_ff2', 'grad_ln_ff_g', 'grad_ln_ff_b', 'delta_w_in', 'delta_w_pool', 'delta_pool_scale', 'delta_w_branch_attn', 'delta_w_branch_pool', 'delta_w_out', 'delta_ln_mix_g', 'delta_ln_mix_b', 'delta_w_ff1', 'delta_w_ff2', 'delta_ln_ff_g', 'delta_ln_ff_b', 'new_m_w_in', 'new_m_w_pool', 'new_m_pool_scale', 'new_m_w_branch_attn', 'new_m_w_branch_pool', 'new_m_w_out', 'new_m_ln_mix_g', 'new_m_ln_mix_b', 'new_m_w_ff1', 'new_m_w_ff2', 'new_m_ln_ff_g', 'new_m_ln_ff_b', 'new_v_w_in', 'new_v_w_pool', 'new_v_pool_scale', 'new_v_w_branch_attn', 'new_v_w_branch_pool', 'new_v_w_out', 'new_v_ln_mix_g', 'new_v_ln_mix_b', 'new_v_w_ff1', 'new_v_w_ff2', 'new_v_ln_ff_g', 'new_v_ln_ff_b']
TWIN_LEAF_KINDS = {'loss': 'loss', 'grad_x': 'grad_x', 'grad_w_in': 'grad_w', 'grad_w_pool': 'grad_w', 'grad_pool_scale': 'grad_w', 'grad_w_branch_attn': 'grad_w', 'grad_w_branch_pool': 'grad_w', 'grad_w_out': 'grad_w', 'grad_ln_mix_g': 'grad_w', 'grad_ln_mix_b': 'grad_w', 'grad_w_ff1': 'grad_w', 'grad_w_ff2': 'grad_w', 'grad_ln_ff_g': 'grad_w', 'grad_ln_ff_b': 'grad_w', 'delta_w_in': 'delta_w', 'delta_w_pool': 'delta_w', 'delta_pool_scale': 'delta_w', 'delta_w_branch_attn': 'delta_w', 'delta_w_branch_pool': 'delta_w', 'delta_w_out': 'delta_w', 'delta_ln_mix_g': 'delta_w', 'delta_ln_mix_b': 'delta_w', 'delta_w_ff1': 'delta_w', 'delta_w_ff2': 'delta_w', 'delta_ln_ff_g': 'delta_w', 'delta_ln_ff_b': 'delta_w', 'new_m_w_in': 'new_m', 'new_m_w_pool': 'new_m', 'new_m_pool_scale': 'new_m', 'new_m_w_branch_attn': 'new_m', 'new_m_w_branch_pool': 'new_m', 'new_m_w_out': 'new_m', 'new_m_ln_mix_g': 'new_m', 'new_m_ln_mix_b': 'new_m', 'new_m_w_ff1': 'new_m', 'new_m_w_ff2': 'new_m', 'new_m_ln_ff_g': 'new_m', 'new_m_ln_ff_b': 'new_m', 'new_v_w_in': 'new_v', 'new_v_w_pool': 'new_v', 'new_v_pool_scale': 'new_v', 'new_v_w_branch_attn': 'new_v', 'new_v_w_branch_pool': 'new_v', 'new_v_w_out': 'new_v', 'new_v_ln_mix_g': 'new_v', 'new_v_ln_mix_b': 'new_v', 'new_v_w_ff1': 'new_v', 'new_v_w_ff2': 'new_v', 'new_v_ln_ff_g': 'new_v', 'new_v_ln_ff_b': 'new_v'}


def _forward(args):
    return _fwd_reference(*[args[k] for k in FWD_PARAMS])


def _output_shape():
    def fwd():
        inp = _fwd_setup_inputs(0)
        return _fwd_reference(*[inp[k] for k in FWD_PARAMS])
    out = _jax.eval_shape(fwd)
    return out.shape, out.dtype

N_MICROBATCH = 1
ADAM_LR = 0.001
ADAM_B1 = 0.9
ADAM_B2 = 0.999
ADAM_EPS = 1e-08
ADAM_WD = 0.01
ADAM_STEP = 10
PER_EXAMPLE_BATCH_AXIS = {'x': 0, 'positions': 0, 'loss_target': 0}
SHARED_INPUTS = []
_WEIGHT_DTYPES = {'w_in': _jnp.float32, 'w_pool': _jnp.float32, 'pool_scale': _jnp.float32, 'w_branch_attn': _jnp.float32, 'w_branch_pool': _jnp.float32, 'w_out': _jnp.float32, 'ln_mix_g': _jnp.float32, 'ln_mix_b': _jnp.float32, 'w_ff1': _jnp.float32, 'w_ff2': _jnp.float32, 'ln_ff_g': _jnp.float32, 'ln_ff_b': _jnp.float32}
MOMENT_SCALE = {'w_in': 9.911667e-03, 'w_pool': 2.598732e-02, 'pool_scale': 2.735046e-02, 'w_branch_attn': 4.696953e-03, 'w_branch_pool': 3.082945e-02, 'w_out': 3.053631e-02, 'ln_mix_g': 3.014311e+00, 'ln_mix_b': 5.519813e-01, 'w_ff1': 2.622404e-02, 'w_ff2': 7.609299e-02, 'ln_ff_g': 3.209357e+01, 'ln_ff_b': 2.744715e+00}


def _to_microbatches(a, axis):
    t = _jnp.moveaxis(a, axis, 0)
    t = t.reshape((N_MICROBATCH, t.shape[0] // N_MICROBATCH) + t.shape[1:])
    return _jnp.moveaxis(t, 1, axis + 1)


def setup_inputs(seed: int = 0) -> dict:
    inp = _fwd_setup_inputs(seed)
    key = _jax.random.fold_in(_jax.random.key(seed), 7919)
    shape, _ = _output_shape()
    out = dict(inp)
    out["loss_target"] = _jax.random.normal(_jax.random.fold_in(key, 0), shape, _jnp.float32)
    for i, name in enumerate(TWIN_WEIGHTS):
        w = inp[name].astype(_jnp.float32)
        if MOMENT_SCALE is None:
            s = _jnp.sqrt(_jnp.mean(_jnp.square(w)) + 1e-30)
        else:
            s = MOMENT_SCALE[name]
        km, kv = _jax.random.split(_jax.random.fold_in(key, i + 1))
        out[name] = w
        out["m_" + name] = s * _jax.random.normal(km, w.shape, _jnp.float32)
        out["v_" + name] = (s * s) * _jax.random.uniform(kv, w.shape, _jnp.float32, 0.5, 1.5)
    if N_MICROBATCH > 1:
        for name, axis in PER_EXAMPLE_BATCH_AXIS.items():
            out[name] = _to_microbatches(out[name], axis)
    return {'x': out['x'], 'positions': out['positions'], 'w_in': out['w_in'], 'w_pool': out['w_pool'], 'pool_scale': out['pool_scale'], 'w_branch_attn': out['w_branch_attn'], 'w_branch_pool': out['w_branch_pool'], 'w_out': out['w_out'], 'ln_mix_g': out['ln_mix_g'], 'ln_mix_b': out['ln_mix_b'], 'w_ff1': out['w_ff1'], 'w_ff2': out['w_ff2'], 'ln_ff_g': out['ln_ff_g'], 'ln_ff_b': out['ln_ff_b'], 'loss_target': out['loss_target'], 'm_w_in': out['m_w_in'], 'm_w_pool': out['m_w_pool'], 'm_pool_scale': out['m_pool_scale'], 'm_w_branch_attn': out['m_w_branch_attn'], 'm_w_branch_pool': out['m_w_branch_pool'], 'm_w_out': out['m_w_out'], 'm_ln_mix_g': out['m_ln_mix_g'], 'm_ln_mix_b': out['m_ln_mix_b'], 'm_w_ff1': out['m_w_ff1'], 'm_w_ff2': out['m_w_ff2'], 'm_ln_ff_g': out['m_ln_ff_g'], 'm_ln_ff_b': out['m_ln_ff_b'], 'v_w_in': out['v_w_in'], 'v_w_pool': out['v_w_pool'], 'v_pool_scale': out['v_pool_scale'], 'v_w_branch_attn': out['v_w_branch_attn'], 'v_w_branch_pool': out['v_w_branch_pool'], 'v_w_out': out['v_w_out'], 'v_ln_mix_g': out['v_ln_mix_g'], 'v_ln_mix_b': out['v_ln_mix_b'], 'v_w_ff1': out['v_w_ff1'], 'v_w_ff2': out['v_w_ff2'], 'v_ln_ff_g': out['v_ln_ff_g'], 'v_ln_ff_b': out['v_ln_ff_b']}


def _loss(weights, diff, rest, loss_target):
    with _jax.named_scope("forward"):
        args = {**rest, TWIN_DIFF_INPUT: diff, **{k: w.astype(_WEIGHT_DTYPES[k]) for k, w in weights.items()}}
        y = _forward(args)
    with _jax.named_scope("loss_head"):
        err = _jnp.square(y.astype(_jnp.float32) - loss_target)
        return 0.5 * _jnp.sum(_jnp.mean(err, axis=-1)) if err.ndim else 0.5 * err


def _adamw(w, g, m, v):
    m = ADAM_B1 * m + (1.0 - ADAM_B1) * g
    v = ADAM_B2 * v + (1.0 - ADAM_B2) * _jnp.square(g)
    m_hat = m / (1.0 - ADAM_B1 ** ADAM_STEP)
    v_hat = v / (1.0 - ADAM_B2 ** ADAM_STEP)
    delta = -ADAM_LR * (m_hat / (_jnp.sqrt(v_hat) + ADAM_EPS) + ADAM_WD * w)
    return delta, m, v


def reference(x, positions, w_in, w_pool, pool_scale, w_branch_attn, w_branch_pool, w_out, ln_mix_g, ln_mix_b, w_ff1, w_ff2, ln_ff_g, ln_ff_b, loss_target, m_w_in, m_w_pool, m_pool_scale, m_w_branch_attn, m_w_branch_pool, m_w_out, m_ln_mix_g, m_ln_mix_b, m_w_ff1, m_w_ff2, m_ln_ff_g, m_ln_ff_b, v_w_in, v_w_pool, v_pool_scale, v_w_branch_attn, v_w_branch_pool, v_w_out, v_ln_mix_g, v_ln_mix_b, v_w_ff1, v_w_ff2, v_ln_ff_g, v_ln_ff_b):
    given = dict(x=x, positions=positions, w_in=w_in, w_pool=w_pool, pool_scale=pool_scale, w_branch_attn=w_branch_attn, w_branch_pool=w_branch_pool, w_out=w_out, ln_mix_g=ln_mix_g, ln_mix_b=ln_mix_b, w_ff1=w_ff1, w_ff2=w_ff2, ln_ff_g=ln_ff_g, ln_ff_b=ln_ff_b, loss_target=loss_target, m_w_in=m_w_in, m_w_pool=m_w_pool, m_pool_scale=m_pool_scale, m_w_branch_attn=m_w_branch_attn, m_w_branch_pool=m_w_branch_pool, m_w_out=m_w_out, m_ln_mix_g=m_ln_mix_g, m_ln_mix_b=m_ln_mix_b, m_w_ff1=m_w_ff1, m_w_ff2=m_w_ff2, m_ln_ff_g=m_ln_ff_g, m_ln_ff_b=m_ln_ff_b, v_w_in=v_w_in, v_w_pool=v_w_pool, v_pool_scale=v_pool_scale, v_w_branch_attn=v_w_branch_attn, v_w_branch_pool=v_w_branch_pool, v_w_out=v_w_out, v_ln_mix_g=v_ln_mix_g, v_ln_mix_b=v_ln_mix_b, v_w_ff1=v_w_ff1, v_w_ff2=v_w_ff2, v_ln_ff_g=v_ln_ff_g, v_ln_ff_b=v_ln_ff_b)
    weights = {n: given[n] for n in TWIN_WEIGHTS}
    shared = {n: given[n] for n in SHARED_INPUTS}
    per_example = {n: given[n] for n in ['x', 'positions']}
    grad_fn = _jax.value_and_grad(_loss, argnums=(0, 1))

    def one_microbatch(ex, loss_target):
        ex = dict(ex)
        diff = ex.pop(TWIN_DIFF_INPUT)
        return grad_fn(weights, diff, {**shared, **ex}, loss_target)

    if N_MICROBATCH == 1:
        loss, (grad_w, grad_x) = one_microbatch(per_example, given["loss_target"])
    else:
        def body(carry, xs):
            loss_sum, grad_sum = carry
            l_k, (gw_k, gx_k) = one_microbatch(xs[0], xs[1])
            with _jax.named_scope("update"):
                return (loss_sum + l_k, _jax.tree.map(_jnp.add, grad_sum, gw_k)), gx_k

        init = (_jnp.zeros((), _jnp.float32), _jax.tree.map(_jnp.zeros_like, weights))
        (loss, grad_w), grad_x = _jax.lax.scan(body, init, (per_example, given["loss_target"]))
    with _jax.named_scope("update"):
        delta_w, new_m, new_v = {}, {}, {}
        for n in TWIN_WEIGHTS:
            delta_w[n], new_m[n], new_v[n] = _adamw(weights[n], grad_w[n], given["m_" + n], given["v_" + n])
    return (loss, grad_x, *[grad_w[n] for n in TWIN_WEIGHTS], *[delta_w[n] for n in TWIN_WEIGHTS],
            *[new_m[n] for n in TWIN_WEIGHTS], *[new_v[n] for n in TWIN_WEIGHTS])
```

```python
import functools

import jax
import jax.numpy as jnp
from jax import lax
from jax.experimental import pallas as pl
from jax.experimental.pallas import tpu as pltpu

F32 = jnp.float32
BF16 = jnp.bfloat16

N_DEV = 8
HEAD_DIM = 128
SUB_BLOCK = 128
DILATIONS = (1, 4, 16)
POOL_WINDOWS = (2, 4, 8, 16)
MAX_POOL_WINDOW = 16
POOL_HALO = 128
LN_EPS = 1e-5
DEEPNORM_ALPHA = 2.0 ** 0.25
ROPE_THETA = 10000.0
ATTN_SCALE = HEAD_DIM ** -0.5
ADAM_LR, ADAM_B1, ADAM_B2, ADAM_EPS, ADAM_WD, ADAM_STEP = 0.001, 0.9, 0.999, 1e-08, 0.01, 10
NEG_BIG = -1e30
VMEM_CAP_V7X = 64 * 1024 * 1024
MESH = pl.DeviceIdType.MESH


def _vmem_limit(est_bytes):
    return int(min(max(est_bytes * 5 // 4 + (4 << 20), 16 << 20), VMEM_CAP_V7X - (6 << 20)))


def _nbytes(shape, dtype):
    n = 1
    for s in shape:
        n *= s
    return n * jnp.dtype(dtype).itemsize


def _mm(name, a, b, form, tiles, outs, epi, extras=(), sequential=False):
    tm, tn, tk = tiles
    if form == "nn":
        (M, K), (K2, N) = a.shape, b.shape
    elif form == "nt":
        (M, K), (N, K2) = a.shape, b.shape
    else:
        (K, M), (K2, N) = a.shape, b.shape
    assert K == K2, (name, a.shape, b.shape)
    tm, tn, tk = min(tm, M), min(tn, N), min(tk, K)
    assert M % tm == 0 and N % tn == 0 and K % tk == 0, (name, M, N, K, tm, tn, tk)
    grid = (M // tm, N // tn, K // tk)
    nk = grid[2]
    if form == "nn":
        a_spec = pl.BlockSpec((tm, tk), lambda i, j, k: (i, k))
        b_spec = pl.BlockSpec((tk, tn), lambda i, j, k: (k, j))
        contract = ((1,), (0,))
    elif form == "nt":
        a_spec = pl.BlockSpec((tm, tk), lambda i, j, k: (i, k))
        b_spec = pl.BlockSpec((tn, tk), lambda i, j, k: (j, k))
        contract = ((1,), (1,))
    else:
        a_spec = pl.BlockSpec((tk, tm), lambda i, j, k: (k, i))
        b_spec = pl.BlockSpec((tk, tn), lambda i, j, k: (k, j))
        contract = ((0,), (0,))
    n_ex, n_out = len(extras), len(outs)

    def body(a_ref, b_ref, *rest):
        ex_refs = rest[:n_ex]
        out_refs = rest[n_ex:n_ex + n_out]
        i, j, k = pl.program_id(0), pl.program_id(1), pl.program_id(2)

        def prod():
            return lax.dot_general(a_ref[...].astype(BF16), b_ref[...].astype(BF16),
                                   (contract, ((), ())), preferred_element_type=F32)

        if nk == 1:
            epi(prod(), ex_refs, out_refs, i, j)
        else:
            acc = rest[n_ex + n_out]

            @pl.when(k == 0)
            def _():
                acc[...] = prod()

            @pl.when(k > 0)
            def _():
                acc[...] += prod()

            @pl.when(k == nk - 1)
            def _():
                epi(acc[...], ex_refs, out_refs, i, j)

    est = 2 * (_nbytes(a_spec.block_shape, a.dtype) + _nbytes(b_spec.block_shape, b.dtype))
    est += sum(2 * _nbytes(bs, arr.dtype) for arr, bs, _ in extras)
    est += sum(2 * _nbytes(bs, dt) for _, dt, bs, _ in outs)
    est += 4 * tm * tn * 4
    sem = ("arbitrary",) * 3 if sequential else ("parallel", "parallel", "arbitrary")
    return pl.pallas_call(
        body, name=name, grid=grid,
        in_specs=[a_spec, b_spec] + [pl.BlockSpec(bs, im) for _, bs, im in extras],
        out_specs=[pl.BlockSpec(bs, im) for _, _, bs, im in outs],
        out_shape=[jax.ShapeDtypeStruct(sh, dt) for sh, dt, _, _ in outs],
        scratch_shapes=[pltpu.VMEM((tm, tn), F32)] if nk > 1 else [],
        compiler_params=pltpu.CompilerParams(dimension_semantics=sem, vmem_limit_bytes=_vmem_limit(est)),
    )(a, b, *[arr for arr, _, _ in extras])


def _tile_out(shape, dtype, tm, tn):
    return (shape, dtype, (tm, tn), lambda i, j, k: (i, j))


def _row_sum_out(width):
    return ((1, width), F32, (1, width), lambda i, j, k: (0, 0))


def _accumulate_rows(ref, value, i):
    @pl.when(i == 0)
    def _():
        ref[...] = value

    @pl.when(i > 0)
    def _():
        ref[...] += value


def _layer_norm_bwd(dy, xhat, rstd, g):
    dxh = dy * g
    m1 = jnp.mean(dxh, axis=-1, keepdims=True)
    m2 = jnp.mean(dxh * xhat, axis=-1, keepdims=True)
    return rstd * (dxh - m1 - xhat * m2)


def _rope_apply(t, cos2, sin_signed):
    return t * cos2 + pltpu.roll(t, HEAD_DIM // 2, axis=1) * sin_signed


def _in_proj(xb, w_in, cos2, sin_fwd, n_rope_cols):
    S, D = xb.shape
    W = w_in.shape[1]
    tm, tn = min(1024, S), min(1024, W)
    assert n_rope_cols % tn == 0
    n_rope_tiles = n_rope_cols // tn

    def epi(acc, ex, out, i, j):
        cos_ref, sin_ref = ex
        (h_ref,) = out

        @pl.when(j < n_rope_tiles)
        def _():
            c, s = cos_ref[...], sin_ref[...]
            for hd in range(tn // HEAD_DIM):
                sl = slice(hd * HEAD_DIM, (hd + 1) * HEAD_DIM)
                h_ref[:, sl] = _rope_apply(acc[:, sl], c, s).astype(BF16)

        @pl.when(j >= n_rope_tiles)
        def _():
            h_ref[...] = acc.astype(BF16)

    row = lambda i, j, k: (i, 0)
    (h,) = _mm("in_proj", xb, w_in, "nn", (tm, tn, 512),
               [_tile_out((S, W), BF16, tm, tn)], epi,
               extras=[(cos2, (tm, HEAD_DIM), row), (sin_fwd, (tm, HEAD_DIM), row)])
    return h


def _attn_masks(mb):
    qi = lax.broadcasted_iota(jnp.int32, (SUB_BLOCK, SUB_BLOCK), 0)
    kj = lax.broadcasted_iota(jnp.int32, (SUB_BLOCK, SUB_BLOCK), 1)
    mask_cur = kj <= qi
    mask_prev = jnp.logical_and(kj >= qi, mb > 0)
    return mask_cur, mask_prev


def _dot_nt(a, b):
    return lax.dot_general(a, b, (((1,), (1,)), ((), ())), preferred_element_type=F32)


def _dot_tn(a, b):
    return lax.dot_general(a, b, (((0,), (0,)), ((), ())), preferred_element_type=F32)


def _dot_nn(a, b):
    return lax.dot_general(a, b, (((1,), (0,)), ((), ())), preferred_element_type=F32)


def _dilated_view_specs(h_width, aw, d):
    cpb = h_width // aw

    def spec(col, prev):
        if prev:
            return pl.BlockSpec((SUB_BLOCK, aw), lambda r, mb: (jnp.maximum(mb - 1, 0), r * cpb + col))
        return pl.BlockSpec((SUB_BLOCK, aw), lambda r, mb: (mb, r * cpb + col))

    return [spec(0, False), spec(1, True), spec(1, False), spec(2, True), spec(2, False)]


def _attn_fwd(h, d, aw):
    S, W = h.shape
    n_heads = aw // HEAD_DIM
    nb = S // d // SUB_BLOCK
    hv = h.reshape(S // d, d * W)

    def body(q_ref, kp_ref, kc_ref, vp_ref, vc_ref, o_ref, lse_ref):
        mask_cur, mask_prev = _attn_masks(pl.program_id(1))
        for hd in range(n_heads):
            sl = slice(hd * HEAD_DIM, (hd + 1) * HEAD_DIM)
            q = q_ref[:, sl]
            sc = jnp.where(mask_cur, _dot_nt(q, kc_ref[:, sl]) * ATTN_SCALE, NEG_BIG)
            sp = jnp.where(mask_prev, _dot_nt(q, kp_ref[:, sl]) * ATTN_SCALE, NEG_BIG)
            m = jnp.maximum(jnp.max(sc, axis=-1, keepdims=True), jnp.max(sp, axis=-1, keepdims=True))
            pc = jnp.exp(sc - m)
            pp = jnp.exp(sp - m)
            l = jnp.sum(pc, axis=-1, keepdims=True) + jnp.sum(pp, axis=-1, keepdims=True)
            o = _dot_nn(pc.astype(BF16), vc_ref[:, sl]) + _dot_nn(pp.astype(BF16), vp_ref[:, sl])
            o_ref[:, sl] = o / l
            lse_ref[:, sl] = jnp.broadcast_to(m + jnp.log(l), (SUB_BLOCK, HEAD_DIM))

    out_spec = pl.BlockSpec((SUB_BLOCK, aw), lambda r, mb: (mb, r))
    o, lse = pl.pallas_call(
        body, name=f"attn_fwd_d{d}", grid=(d, nb),
        in_specs=_dilated_view_specs(W, aw, d),
        out_specs=[out_spec, out_spec],
        out_shape=[jax.ShapeDtypeStruct((S // d, d * aw), F32)] * 2,
        compiler_params=pltpu.CompilerParams(dimension_semantics=("parallel", "parallel"),
                                             vmem_limit_bytes=_vmem_limit(16 << 20)),
    )(hv, hv, hv, hv, hv)
    return o.reshape(S, aw), lse.reshape(S, aw)


def _attn_combine(outs, lses):
    S, aw = outs[0].shape
    tm = min(512, S)

    def body(o1, o2, o3, l1, l2, l3, o_ref, lt_ref):
        a1, a2, a3 = l1[...], l2[...], l3[...]
        mx = jnp.maximum(jnp.maximum(a1, a2), a3)
        e1, e2, e3 = jnp.exp(a1 - mx), jnp.exp(a2 - mx), jnp.exp(a3 - mx)
        den = e1 + e2 + e3
        o_ref[...] = ((e1 * o1[...] + e2 * o2[...] + e3 * o3[...]) / den).astype(BF16)
        lt_ref[...] = mx + jnp.log(den)

    spec = pl.BlockSpec((tm, aw), lambda i: (i, 0))
    return pl.pallas_call(
        body, name="attn_combine", grid=(S // tm,),
        in_specs=[spec] * 6, out_specs=[spec, spec],
        out_shape=[jax.ShapeDtypeStruct((S, aw), BF16), jax.ShapeDtypeStruct((S, aw), F32)],
        compiler_params=pltpu.CompilerParams(dimension_semantics=("parallel",),
                                             vmem_limit_bytes=_vmem_limit(40 << 20)),
    )(*outs, *lses)


def _band(tm, width, w, row_offset, transpose):
    t = lax.broadcasted_iota(jnp.int32, (tm, width), 0)
    u = lax.broadcasted_iota(jnp.int32, (tm, width), 1)
    dist = (u - t - row_offset) if transpose else (t + row_offset - u)
    return jnp.logical_and(dist >= 0, dist < w).astype(BF16)


def _pool_fwd(h, w_pool, pool_scale, pw, u_col_block):
    S, W = h.shape
    n_groups = len(POOL_WINDOWS)
    gw = pw // n_groups
    tm = min(512, S)
    halo_per_tile = tm // POOL_HALO

    def body(uc_ref, uh_ref, w_ref, sc_ref, p_ref, y_ref, pm_ref):
        i = pl.program_id(0)
        t_abs = i * tm + lax.broadcasted_iota(jnp.int32, (tm, 1), 0)
        for g, w in enumerate(POOL_WINDOWS):
            sl = slice(g * gw, (g + 1) * gw)
            uc = uc_ref[:, sl]
            uh = jnp.where(i > 0, uh_ref[:, sl], jnp.zeros((POOL_HALO, gw), BF16))
            ssum = _dot_nn(_band(tm, tm, w, 0, False), uc) + _dot_nn(_band(tm, POOL_HALO, w, POOL_HALO, False), uh)
            cnt = jnp.minimum(t_abs + 1, w).astype(F32)
            p = (ssum / cnt - uc.astype(F32)).astype(BF16)
            y = _dot_nn(p, w_ref[g])
            p_ref[:, sl] = p
            y_ref[:, sl] = y.astype(BF16)
            pm_ref[:, sl] = (y * sc_ref[:, sl]).astype(BF16)

    row = pl.BlockSpec((tm, pw), lambda i: (i, 0))
    return pl.pallas_call(
        body, name="pool_fwd", grid=(S // tm,),
        in_specs=[pl.BlockSpec((tm, pw), lambda i: (i, u_col_block)),
                  pl.BlockSpec((POOL_HALO, pw), lambda i: (jnp.maximum(i * halo_per_tile - 1, 0), u_col_block)),
                  pl.BlockSpec((n_groups, gw, gw), lambda i: (0, 0, 0)),
                  pl.BlockSpec((1, pw), lambda i: (0, 0))],
        out_specs=[row, row, row],
        out_shape=[jax.ShapeDtypeStruct((S, pw), BF16)] * 3,
        compiler_params=pltpu.CompilerParams(dimension_semantics=("parallel",),
                                             vmem_limit_bytes=_vmem_limit(24 << 20)),
    )(h, h, w_pool, pool_scale)


def _branch_attn(o_attn, w_ba):
    S, _ = o_attn.shape
    D = w_ba.shape[1]
    tm, tn = min(1024, S), min(1024, D)

    def epi(acc, ex, out, i, j):
        out[0][...] = acc.astype(BF16)

    (y,) = _mm("branch_attn", o_attn, w_ba, "nn", (tm, tn, 1024), [_tile_out((S, D), BF16, tm, tn)], epi)
    return y


def _branch_pool_merge(pm, w_bp, h, y_attn, gate_col0):
    S, _ = pm.shape
    D = w_bp.shape[1]
    tm, tn = min(1024, S), min(1024, D)
    ga0, gp0 = gate_col0 // tn, (gate_col0 + D) // tn

    def epi(acc, ex, out, i, j):
        ga_ref, gp_ref, ya_ref = ex
        yp_ref, mg_ref = out
        yp = acc.astype(BF16)
        yp_ref[...] = yp
        mg = (jax.nn.sigmoid(ga_ref[...].astype(F32)) * ya_ref[...].astype(F32)
              + jax.nn.sigmoid(gp_ref[...].astype(F32)) * acc)
        mg_ref[...] = mg.astype(BF16)

    y_pool, merged = _mm(
        "branch_pool_merge", pm, w_bp, "nn", (tm, tn, 1024),
        [_tile_out((S, D), BF16, tm, tn), _tile_out((S, D), BF16, tm, tn)], epi,
        extras=[(h, (tm, tn), lambda i, j, k: (i, ga0 + j)), (h, (tm, tn), lambda i, j, k: (i, gp0 + j)),
                (y_attn, (tm, tn), lambda i, j, k: (i, j))])
    return y_pool, merged


def _layer_norm_rows(z, g, b):
    mu = jnp.mean(z, axis=-1, keepdims=True)
    zc = z - mu
    var = jnp.mean(zc * zc, axis=-1, keepdims=True)
    rstd = lax.rsqrt(var + LN_EPS)
    xhat = zc * rstd
    return xhat * g + b, xhat, rstd


def _out_proj_ln(merged, w_out, x, g, b):
    S, D = x.shape
    tm = min(512, S)

    def epi(acc, ex, out, i, j):
        x_ref, g_ref, b_ref = ex
        x1_ref, x1b_ref, xh_ref, rs_ref = out
        y, xhat, rstd = _layer_norm_rows(DEEPNORM_ALPHA * x_ref[...] + acc, g_ref[...], b_ref[...])
        x1_ref[...] = y
        x1b_ref[...] = y.astype(BF16)
        xh_ref[...] = xhat
        rs_ref[...] = jnp.broadcast_to(rstd, (tm, HEAD_DIM))

    row = lambda i, j, k: (i, 0)
    vec = lambda i, j, k: (0, 0)
    return _mm("out_proj_ln", merged, w_out, "nn", (tm, D, 512),
               [((S, D), F32, (tm, D), row), ((S, D), BF16, (tm, D), row), ((S, D), F32, (tm, D), row),
                ((S, HEAD_DIM), F32, (tm, HEAD_DIM), row)], epi,
               extras=[(x, (tm, D), row), (g, (1, D), vec), (b, (1, D), vec)])


def _ffn_up(x1b, w1):
    S, D = x1b.shape
    F = w1.shape[1]
    tm, tn = min(1024, S), min(2048, F)

    def epi(acc, ex, out, i, j):
        r = jnp.maximum(acc, 0.0)
        out[0][...] = (r * r).astype(BF16)

    (a,) = _mm("ffn_up", x1b, w1, "nn", (tm, tn, 512), [_tile_out((S, F), BF16, tm, tn)], epi)
    return a


def _ffn_down_loss(a, w2, x1, g, b, target):
    S, D = x1.shape
    tm = min(512, S)

    def epi(acc, ex, out, i, j):
        x1_ref, g_ref, b_ref, t_ref = ex
        dz_ref, dzb_ref, dg_ref, db_ref, loss_ref = out
        gv = g_ref[...]
        y, xhat, rstd = _layer_norm_rows(DEEPNORM_ALPHA * x1_ref[...] + acc, gv, b_ref[...])
        err = y - t_ref[...]
        loss = 0.5 * jnp.sum(jnp.mean(err * err, axis=-1, keepdims=True), axis=0, keepdims=True)
        dy = err * (1.0 / D)
        dz = _layer_norm_bwd(dy, xhat, rstd, gv)
        dz_ref[...] = dz
        dzb_ref[...] = dz.astype(BF16)
        _accumulate_rows(dg_ref, jnp.sum(dy * xhat, axis=0, keepdims=True), i)
        _accumulate_rows(db_ref, jnp.sum(dy, axis=0, keepdims=True), i)
        _accumulate_rows(loss_ref, jnp.broadcast_to(loss, (1, HEAD_DIM)), i)

    row = lambda i, j, k: (i, 0)
    vec = lambda i, j, k: (0, 0)
    return _mm("ffn_down_loss", a, w2, "nn", (tm, D, 512),
               [((S, D), F32, (tm, D), row), ((S, D), BF16, (tm, D), row),
                _row_sum_out(D), _row_sum_out(D), _row_sum_out(HEAD_DIM)], epi,
               extras=[(x1, (tm, D), row), (g, (1, D), vec), (b, (1, D), vec), (target, (tm, D), row)],
               sequential=True)


def _grad_weight(name, act, cot):
    M, N = act.shape[1], cot.shape[1]
    tm, tn = min(1024, M), min(2048, N)

    def epi(acc, ex, out, i, j):
        out[0][...] = acc.astype(BF16)

    (g,) = _mm(name, act, cot, "tn", (tm, tn, 512), [_tile_out((M, N), BF16, tm, tn)], epi)
    return g


def _ffn_down_bwd(dz2b, w2, a):
    S, D = dz2b.shape
    F = w2.shape[0]
    tm, tn = min(1024, S), min(2048, F)

    def epi(acc, ex, out, i, j):
        out[0][...] = (acc * (2.0 * jnp.sqrt(ex[0][...].astype(F32)))).astype(BF16)

    (dh1,) = _mm("ffn_down_bwd", dz2b, w2, "nt", (tm, tn, 512), [_tile_out((S, F), BF16, tm, tn)], epi,
                 extras=[(a, (tm, tn), lambda i, j, k: (i, j))])
    return dh1


def _ffn_up_bwd_ln(dh1, w1, dz2, xhat1, rstd1, g1):
    S, D = dz2.shape
    tm = min(512, S)

    def epi(acc, ex, out, i, j):
        dz2_ref, xh_ref, rs_ref, g_ref = ex
        dz_ref, dzb_ref, dg_ref, db_ref = out
        dy = DEEPNORM_ALPHA * dz2_ref[...] + acc
        xhat = xh_ref[...]
        dz = _layer_norm_bwd(dy, xhat, rs_ref[:, :1], g_ref[...])
        dz_ref[...] = dz
        dzb_ref[...] = dz.astype(BF16)
        _accumulate_rows(dg_ref, jnp.sum(dy * xhat, axis=0, keepdims=True), i)
        _accumulate_rows(db_ref, jnp.sum(dy, axis=0, keepdims=True), i)

    row = lambda i, j, k: (i, 0)
    vec = lambda i, j, k: (0, 0)
    return _mm("ffn_up_bwd_ln", dh1, w1, "nt", (tm, D, 512),
               [((S, D), F32, (tm, D), row), ((S, D), BF16, (tm, D), row), _row_sum_out(D), _row_sum_out(D)], epi,
               extras=[(dz2, (tm, D), row), (xhat1, (tm, D), row), (rstd1, (tm, HEAD_DIM), row), (g1, (1, D), vec)],
               sequential=True)


def _out_proj_bwd(dz1b, w_out, h, y_attn, y_pool, gate_col0):
    S, D = dz1b.shape
    W = h.shape[1]
    tm = min(256, S)
    assert gate_col0 == 2 * D and W == 4 * D

    def epi(acc, ex, out, i, j):
        gates_ref, ya_ref, yp_ref = ex
        dya_ref, dyp_ref, dh_ref = out
        sa = jax.nn.sigmoid(gates_ref[:, :D].astype(F32))
        sp = jax.nn.sigmoid(gates_ref[:, D:].astype(F32))
        dya_ref[...] = (acc * sa).astype(BF16)
        dyp_ref[...] = (acc * sp).astype(BF16)
        dh_ref[:, :D] = (acc * ya_ref[...].astype(F32) * (sa * (1.0 - sa))).astype(BF16)
        dh_ref[:, D:] = (acc * yp_ref[...].astype(F32) * (sp * (1.0 - sp))).astype(BF16)

    row = lambda i, j, k: (i, 0)
    return _mm("out_proj_bwd", dz1b, w_out, "nt", (tm, D, 512),
               [((S, D), BF16, (tm, D), row), ((S, D), BF16, (tm, D), row),
                ((S, W), BF16, (tm, 2 * D), lambda i, j, k: (i, 1))], epi,
               extras=[(h, (tm, 2 * D), lambda i, j, k: (i, 1)), (y_attn, (tm, D), row), (y_pool, (tm, D), row)])


def _branch_attn_bwd(dy_attn, w_ba, o_attn):
    S, D = dy_attn.shape
    aw = w_ba.shape[0]
    tm = min(512, S)

    def epi(acc, ex, out, i, j):
        do_ref, dl_ref = out
        do_ref[...] = acc.astype(BF16)
        o = ex[0][...].astype(F32)
        for hd in range(aw // HEAD_DIM):
            sl = slice(hd * HEAD_DIM, (hd + 1) * HEAD_DIM)
            dl = jnp.sum(acc[:, sl] * o[:, sl], axis=-1, keepdims=True)
            dl_ref[:, sl] = jnp.broadcast_to(dl, (tm, HEAD_DIM))

    row = lambda i, j, k: (i, 0)
    return _mm("branch_attn_bwd", dy_attn, w_ba, "nt", (tm, aw, 512),
               [((S, aw), BF16, (tm, aw), row), ((S, aw), F32, (tm, aw), row)], epi,
               extras=[(o_attn, (tm, aw), row)])


def _branch_pool_bwd(dy_pool, w_bp, y_pre, pool_scale):
    S, D = dy_pool.shape
    pw = w_bp.shape[0]
    tm = min(512, S)

    def epi(acc, ex, out, i, j):
        y_ref, sc_ref = ex
        dyp_ref, dsc_ref = out
        dyp_ref[...] = (acc * sc_ref[...]).astype(BF16)
        _accumulate_rows(dsc_ref, jnp.sum(acc * y_ref[...].astype(F32), axis=0, keepdims=True), i)

    row = lambda i, j, k: (i, 0)
    return _mm("branch_pool_bwd", dy_pool, w_bp, "nt", (tm, pw, 512),
               [((S, pw), BF16, (tm, pw), row), _row_sum_out(pw)], epi,
               extras=[(y_pre, (tm, pw), row), (pool_scale, (1, pw), lambda i, j, k: (0, 0))],
               sequential=True)


def _pool_bwd(dh, dy_pre, p, w_pool, pw, u_col_block):
    S, W = dh.shape
    n_groups = len(POOL_WINDOWS)
    gw = pw // n_groups
    tm = min(512, S)
    n_tiles = S // tm
    halo_per_tile = tm // POOL_HALO
    n_halo_blocks = S // POOL_HALO

    def body(dh_in_ref, dyc_ref, dyh_ref, p_ref, w_ref, dh_ref, dw_ref):
        del dh_in_ref
        i = pl.program_id(0)
        t_cur = i * tm + lax.broadcasted_iota(jnp.int32, (tm, 1), 0)
        t_halo = (i + 1) * tm + lax.broadcasted_iota(jnp.int32, (POOL_HALO, 1), 0)
        for g, w in enumerate(POOL_WINDOWS):
            sl = slice(g * gw, (g + 1) * gw)
            wg = w_ref[g]
            dyc = dyc_ref[:, sl]
            dyh = jnp.where(i < n_tiles - 1, dyh_ref[:, sl], jnp.zeros((POOL_HALO, gw), BF16))
            dp_cur = _dot_nt(dyc, wg)
            dp_halo = _dot_nt(dyh, wg)
            dpc_cur = (dp_cur / jnp.minimum(t_cur + 1, w).astype(F32)).astype(BF16)
            dpc_halo = (dp_halo / jnp.minimum(t_halo + 1, w).astype(F32)).astype(BF16)
            du = (_dot_nn(_band(tm, tm, w, 0, True), dpc_cur)
                  + _dot_nn(_band(tm, POOL_HALO, w, -tm, True), dpc_halo) - dp_cur)
            dh_ref[:, sl] = du.astype(BF16)
            dw = _dot_tn(p_ref[:, sl], dyc)

            @pl.when(i == 0)
            def _():
                dw_ref[g] = dw

            @pl.when(i > 0)
            def _():
                dw_ref[g] += dw

    row = pl.BlockSpec((tm, pw), lambda i: (i, 0))
    dh_new, dw_pool = pl.pallas_call(
        body, name="pool_bwd", grid=(n_tiles,),
        in_specs=[pl.BlockSpec(memory_space=pl.ANY), row,
                  pl.BlockSpec((POOL_HALO, pw), lambda i: (jnp.minimum((i + 1) * halo_per_tile, n_halo_blocks - 1), 0)),
                  row, pl.BlockSpec((n_groups, gw, gw), lambda i: (0, 0, 0))],
        out_specs=[pl.BlockSpec((tm, pw), lambda i: (i, u_col_block)),
                   pl.BlockSpec((n_groups, gw, gw), lambda i: (0, 0, 0))],
        out_shape=[jax.ShapeDtypeStruct((S, W), BF16), jax.ShapeDtypeStruct((n_groups, gw, gw), F32)],
        input_output_aliases={0: 0},
        compiler_params=pltpu.CompilerParams(dimension_semantics=("arbitrary",),
                                             vmem_limit_bytes=_vmem_limit(24 << 20)),
    )(dh, dy_pre, dy_pre, p, w_pool)
    return dh_new, dw_pool


def _attn_bwd(h, d_out, l_tot, delta, d, aw):
    S, W = h.shape
    n_heads = aw // HEAD_DIM
    nb = S // d // SUB_BLOCK
    hv = h.reshape(S // d, d * W)
    view = lambda t: t.reshape(S // d, d * aw)

    def body(q_ref, kp_ref, kc_ref, vp_ref, vc_ref, do_ref, l_ref, dl_ref,
             dq_ref, dkc_ref, dkp_ref, dvc_ref, dvp_ref):
        mask_cur, mask_prev = _attn_masks(pl.program_id(1))
        for hd in range(n_heads):
            sl = slice(hd * HEAD_DIM, (hd + 1) * HEAD_DIM)
            q, kc, kp, vc, vp, do = q_ref[:, sl], kc_ref[:, sl], kp_ref[:, sl], vc_ref[:, sl], vp_ref[:, sl], do_ref[:, sl]
            lt, dl = l_ref[:, sl], dl_ref[:, sl]
            pc = jnp.where(mask_cur, jnp.exp(jnp.where(mask_cur, _dot_nt(q, kc) * ATTN_SCALE - lt, NEG_BIG)), 0.0)
            pp = jnp.where(mask_prev, jnp.exp(jnp.where(mask_prev, _dot_nt(q, kp) * ATTN_SCALE - lt, NEG_BIG)), 0.0)
            dsc = (pc * (_dot_nt(do, vc) - dl) * ATTN_SCALE).astype(BF16)
            dsp = (pp * (_dot_nt(do, vp) - dl) * ATTN_SCALE).astype(BF16)
            dq_ref[:, sl] = _dot_nn(dsc, kc) + _dot_nn(dsp, kp)
            dkc_ref[:, sl] = _dot_tn(dsc, q)
            dkp_ref[:, sl] = _dot_tn(dsp, q)
            dvc_ref[:, sl] = _dot_tn(pc.astype(BF16), do)
            dvp_ref[:, sl] = _dot_tn(pp.astype(BF16), do)

    blk = pl.BlockSpec((SUB_BLOCK, aw), lambda r, mb: (mb, r))
    outs = pl.pallas_call(
        body, name=f"attn_bwd_d{d}", grid=(d, nb),
        in_specs=_dilated_view_specs(W, aw, d) + [blk, blk, blk],
        out_specs=[blk] * 5,
        out_shape=[jax.ShapeDtypeStruct((S // d, d * aw), F32)] * 5,
        compiler_params=pltpu.CompilerParams(dimension_semantics=("parallel", "parallel"),
                                             vmem_limit_bytes=_vmem_limit(24 << 20)),
    )(hv, hv, hv, hv, hv, view(d_out), view(l_tot), view(delta))
    return [t.reshape(S, aw) for t in outs]


def _attn_bwd_finish(dh, per_pattern, cos2, sin_bwd, aw):
    S, W = dh.shape
    n_heads = aw // HEAD_DIM
    tb = SUB_BLOCK
    nblk = S // tb
    n_pat = len(DILATIONS)

    def body(*refs):
        dq_refs = refs[1:1 + n_pat]
        dkc_refs = refs[1 + n_pat:1 + 2 * n_pat]
        dkp_refs = refs[1 + 2 * n_pat:1 + 3 * n_pat]
        dvc_refs = refs[1 + 3 * n_pat:1 + 4 * n_pat]
        dvp_refs = refs[1 + 4 * n_pat:1 + 5 * n_pat]
        cos_ref, sin_ref = refs[1 + 5 * n_pat], refs[2 + 5 * n_pat]
        out_ref = refs[3 + 5 * n_pat]
        i = pl.program_id(0)
        dq = dq_refs[0][...]
        dk = dkc_refs[0][...]
        dv = dvc_refs[0][...]
        for pi in range(1, n_pat):
            dq = dq + dq_refs[pi][...]
            dk = dk + dkc_refs[pi][...]
            dv = dv + dvc_refs[pi][...]
        for pi, d in enumerate(DILATIONS):
            has_next = i + d < nblk
            dk = dk + jnp.where(has_next, dkp_refs[pi][...], 0.0)
            dv = dv + jnp.where(has_next, dvp_refs[pi][...], 0.0)
        c, s = cos_ref[...], sin_ref[...]
        for hd in range(n_heads):
            sl = slice(hd * HEAD_DIM, (hd + 1) * HEAD_DIM)
            out_ref[:, sl] = _rope_apply(dq[:, sl], c, s).astype(BF16)
            out_ref[:, aw + hd * HEAD_DIM:aw + (hd + 1) * HEAD_DIM] = _rope_apply(dk[:, sl], c, s).astype(BF16)
        out_ref[:, 2 * aw:] = dv.astype(BF16)

    cur = pl.BlockSpec((tb, aw), lambda i: (i, 0))

    def shifted(d):
        return pl.BlockSpec((tb, aw), lambda i: (jnp.minimum(i + d, nblk - 1), 0))

    dqs = [pp[0] for pp in per_pattern]
    dkcs = [pp[1] for pp in per_pattern]
    dkps = [pp[2] for pp in per_pattern]
    dvcs = [pp[3] for pp in per_pattern]
    dvps = [pp[4] for pp in per_pattern]
    rope_spec = pl.BlockSpec((tb, HEAD_DIM), lambda i: (i, 0))
    return pl.pallas_call(
        body, name="attn_bwd_finish", grid=(nblk,),
        in_specs=([pl.BlockSpec(memory_space=pl.ANY)] + [cur] * (2 * n_pat) + [shifted(d) for d in DILATIONS]
                  + [cur] * n_pat + [shifted(d) for d in DILATIONS] + [rope_spec, rope_spec]),
        out_specs=pl.BlockSpec((tb, 3 * aw), lambda i: (i, 0)),
        out_shape=jax.ShapeDtypeStruct((S, W), BF16),
        input_output_aliases={0: 0},
        compiler_params=pltpu.CompilerParams(dimension_semantics=("parallel",),
                                             vmem_limit_bytes=_vmem_limit(24 << 20)),
    )(dh, *dqs, *dkcs, *dkps, *dvcs, *dvps, cos2, sin_bwd)


def _in_proj_bwd(dh, w_in, dz1):
    S, D = dz1.shape
    tm, tn = min(1024, S), min(1024, D)

    def epi(acc, ex, out, i, j):
        out[0][...] = DEEPNORM_ALPHA * ex[0][...] + acc

    (gx,) = _mm("in_proj_bwd", dh, w_in, "nt", (tm, tn, 512), [_tile_out((S, D), F32, tm, tn)], epi,
                extras=[(dz1, (tm, tn), lambda i, j, k: (i, j))])
    return gx


def _my_place():
    x, y, c = lax.axis_index("x"), lax.axis_index("y"), lax.axis_index("c")
    return x, y, c


def _flat(px, py, pc):
    return 4 * px + 2 * py + pc


def _shard_slice(ref, axis, idx, size):
    start = pl.multiple_of(idx * size, size)
    ix = [slice(None)] * len(ref.shape)
    ix[axis] = pl.ds(start, size)
    return ref.at[tuple(ix)]


def _all_gather_weights(shards, axes):
    n_w = len(shards)
    full_shapes = []
    for s, ax in zip(shards, axes):
        sh = list(s.shape)
        sh[ax] *= N_DEV
        full_shapes.append(tuple(sh))

    def body(*refs):
        src = refs[:n_w]
        dst = refs[n_w:2 * n_w]
        send_sems, recv_sems, local_sems = refs[2 * n_w:]
        x, y, c = _my_place()
        me, sibling = (x, y, c), (x, y, 1 - c)
        chips = [(1 - x, y), (x, 1 - y), (1 - x, 1 - y)]

        def place(w, dev):
            return _shard_slice(dst[w], axes[w], _flat(*dev), src[w].shape[axes[w]])

        def copy(w, k, block, to, from_src=False):
            return pltpu.make_async_remote_copy(
                src_ref=src[w] if from_src else place(w, block), dst_ref=place(w, block),
                send_sem=send_sems.at[w * 7 + k], recv_sem=recv_sems.at[w * 7 + k], device_id=to, device_id_type=MESH)

        mine, first, passed = [], [], []
        for w in range(n_w):
            cp = pltpu.make_async_copy(src[w], place(w, me), local_sems.at[w])
            cp.start()
            mine.append(cp)
            fw = [copy(w, 0, me, sibling, True)] + [copy(w, 1 + j, me, (*chip, c), True) for j, chip in enumerate(chips)]
            for cp in fw:
                cp.start()
            first += fw
        for w in range(n_w):
            for j, chip in enumerate(chips):
                copy(w, 1 + j, (*chip, c), me).wait_recv()
                cp = copy(w, 4 + j, (*chip, c), sibling)
                cp.start()
                passed.append(cp)
        for w in range(n_w):
            copy(w, 0, sibling, me).wait_recv()
            for j, chip in enumerate(chips):
                copy(w, 4 + j, (*chip, 1 - c), me).wait_recv()
        for cp in first + passed:
            cp.wait_send()
        for cp in mine:
            cp.wait()

    any_spec = pl.BlockSpec(memory_space=pl.ANY)
    return pl.pallas_call(
        body, name="all_gather_weights",
        in_specs=[any_spec] * n_w, out_specs=[any_spec] * n_w,
        out_shape=[jax.ShapeDtypeStruct(sh, s.dtype) for sh, s in zip(full_shapes, shards)],
        scratch_shapes=[pltpu.SemaphoreType.DMA((n_w * 7,)), pltpu.SemaphoreType.DMA((n_w * 7,)),
                        pltpu.SemaphoreType.DMA((n_w,))],
    )(*shards)


def _reduce_scatter_grads(grads, axes):
    n_w = len(grads)
    n_peer = N_DEV - 1
    shard_shapes = []
    for g, ax in zip(grads, axes):
        sh = list(g.shape)
        if ax is not None:
            sh[ax] //= N_DEV
        shard_shapes.append(tuple(sh))

    def body(*refs):
        src = refs[:n_w]
        dst = refs[n_w:2 * n_w]
        send_sems, recv_sems, local_sems = refs[2 * n_w:]
        x, y, c = _my_place()
        me = _flat(x, y, c)

        def part_for(w, dev):
            if axes[w] is None:
                return src[w]
            return _shard_slice(src[w], axes[w], dev, shard_shapes[w][axes[w]])

        def peer_of(r):
            return (x ^ ((r >> 2) & 1), y ^ ((r >> 1) & 1), c ^ (r & 1))

        def outgoing(w, r):
            peer = peer_of(r)
            return pltpu.make_async_remote_copy(
                src_ref=part_for(w, _flat(*peer)), dst_ref=dst[w].at[me],
                send_sem=send_sems.at[w * n_peer + r - 1], recv_sem=recv_sems.at[w * n_peer + r - 1],
                device_id=peer, device_id_type=MESH)

        def incoming(w, r):
            slot = dst[w].at[_flat(*peer_of(r))]
            return pltpu.make_async_remote_copy(
                src_ref=slot, dst_ref=slot,
                send_sem=send_sems.at[w * n_peer + r - 1], recv_sem=recv_sems.at[w * n_peer + r - 1],
                device_id=peer_of(r), device_id_type=MESH)

        own = [pltpu.make_async_copy(part_for(w, me), dst[w].at[me], local_sems.at[w]) for w in range(n_w)]
        for w in range(n_w):
            own[w].start()
            for r in range(1, N_DEV):
                outgoing(w, r).start()
        for w in range(n_w):
            for r in range(1, N_DEV):
                incoming(w, r).wait_recv()
        for w in range(n_w):
            for r in range(1, N_DEV):
                outgoing(w, r).wait_send()
            own[w].wait()

    any_spec = pl.BlockSpec(memory_space=pl.ANY)
    return pl.pallas_call(
        body, name="reduce_scatter_grads",
        in_specs=[any_spec] * n_w, out_specs=[any_spec] * n_w,
        out_shape=[jax.ShapeDtypeStruct((N_DEV,) + sh, g.dtype) for sh, g in zip(shard_shapes, grads)],
        scratch_shapes=[pltpu.SemaphoreType.DMA((n_w * n_peer,)), pltpu.SemaphoreType.DMA((n_w * n_peer,)),
                        pltpu.SemaphoreType.DMA((n_w,))],
    )(*grads)


def _adamw(name, partials, w, m, v):
    R, C = w.shape
    tr = R
    while tr * C * 4 > (1 << 20) and tr % 16 == 0:
        tr //= 2

    def body(p_ref, w_ref, m_ref, v_ref, g_ref, d_ref, nm_ref, nv_ref):
        g = p_ref[0].astype(F32)
        for jdev in range(1, N_DEV):
            g = g + p_ref[jdev].astype(F32)
        nm = ADAM_B1 * m_ref[...] + (1.0 - ADAM_B1) * g
        nv = ADAM_B2 * v_ref[...] + (1.0 - ADAM_B2) * (g * g)
        m_hat = nm / (1.0 - ADAM_B1 ** ADAM_STEP)
        v_hat = nv / (1.0 - ADAM_B2 ** ADAM_STEP)
        g_ref[...] = g
        d_ref[...] = -ADAM_LR * (m_hat / (jnp.sqrt(v_hat) + ADAM_EPS) + ADAM_WD * w_ref[...])
        nm_ref[...] = nm
        nv_ref[...] = nv

    spec = pl.BlockSpec((tr, C), lambda i: (i, 0))
    return pl.pallas_call(
        body, name=name, grid=(R // tr,),
        in_specs=[pl.BlockSpec((N_DEV, tr, C), lambda i: (0, i, 0)), spec, spec, spec],
        out_specs=[spec] * 4,
        out_shape=[jax.ShapeDtypeStruct((R, C), F32)] * 4,
        compiler_params=pltpu.CompilerParams(dimension_semantics=("parallel",),
                                             vmem_limit_bytes=_vmem_limit(24 << 20)),
    )(partials, w, m, v)


def _local_step(x, cos2, sin_fwd, sin_bwd, w_in, w_pool, pool_scale, w_ba, w_bp, w_out, g_mix, b_mix,
                w1, w2, g_ff, b_ff, target):
    S, D = x.shape
    aw = pw = D // 2
    u_col_block = 3
    gate_col0 = 4 * aw

    xb = x.astype(BF16)
    h = _in_proj(xb, w_in, cos2, sin_fwd, 2 * aw)
    fwd = [_attn_fwd(h, d, aw) for d in DILATIONS]
    o_attn, l_tot = _attn_combine([f[0] for f in fwd], [f[1] for f in fwd])
    p, y_pre, pm = _pool_fwd(h, w_pool, pool_scale, pw, u_col_block)
    y_attn = _branch_attn(o_attn, w_ba)
    y_pool, merged = _branch_pool_merge(pm, w_bp, h, y_attn, gate_col0)
    x1, x1b, xhat1, rstd1 = _out_proj_ln(merged, w_out, x, g_mix, b_mix)
    a = _ffn_up(x1b, w1)
    dz2, dz2b, dg_ff, db_ff, loss = _ffn_down_loss(a, w2, x1, g_ff, b_ff, target)

    dw2 = _grad_weight("grad_w_ff2", a, dz2b)
    dh1 = _ffn_down_bwd(dz2b, w2, a)
    dw1 = _grad_weight("grad_w_ff1", x1b, dh1)
    dz1, dz1b, dg_mix, db_mix = _ffn_up_bwd_ln(dh1, w1, dz2, xhat1, rstd1, g_mix)
    dw_out = _grad_weight("grad_w_out", merged, dz1b)
    dy_attn, dy_pool, dh = _out_proj_bwd(dz1b, w_out, h, y_attn, y_pool, gate_col0)
    dw_ba = _grad_weight("grad_w_branch_attn", o_attn, dy_attn)
    dw_bp = _grad_weight("grad_w_branch_pool", pm, dy_pool)
    d_out, delta = _branch_attn_bwd(dy_attn, w_ba, o_attn)
    dy_pre, d_scale = _branch_pool_bwd(dy_pool, w_bp, y_pre, pool_scale)
    dh, dw_pool = _pool_bwd(dh, dy_pre, p, w_pool, pw, u_col_block)
    per_pattern = [_attn_bwd(h, d_out, l_tot, delta, d, aw) for d in DILATIONS]
    dh = _attn_bwd_finish(dh, per_pattern, cos2, sin_bwd, aw)
    dw_in = _grad_weight("grad_w_in", xb, dh)
    grad_x = _in_proj_bwd(dh, w_in, dz1)
    big = (dw_in, dw_pool, dw_ba, dw_bp, dw_out, dw1, dw2)
    small = (d_scale, dg_mix, db_mix, dg_ff, db_ff)
    return loss, grad_x, big, small


def _rope_tables(positions):
    half = HEAD_DIM // 2
    inv_freq = ROPE_THETA ** (-jnp.arange(half, dtype=F32) / half)
    ang = positions.astype(F32)[:, None] * inv_freq
    cos, sin = jnp.cos(ang), jnp.sin(ang)
    cos2 = jnp.concatenate([cos, cos], axis=-1)
    sin_fwd = jnp.concatenate([-sin, sin], axis=-1)
    return cos2, sin_fwd, -sin_fwd


def kernel(x, positions, w_in, w_pool, pool_scale, w_branch_attn, w_branch_pool, w_out, ln_mix_g, ln_mix_b, w_ff1, w_ff2, ln_ff_g, ln_ff_b, loss_target, m_w_in, m_w_pool, m_pool_scale, m_w_branch_attn, m_w_branch_pool, m_w_out, m_ln_mix_g, m_ln_mix_b, m_w_ff1, m_w_ff2, m_ln_ff_g, m_ln_ff_b, v_w_in, v_w_pool, v_pool_scale, v_w_branch_attn, v_w_branch_pool, v_w_out, v_ln_mix_g, v_ln_mix_b, v_w_ff1, v_w_ff2, v_ln_ff_g, v_ln_ff_b):
    big_w = (w_in[0], w_pool[0], w_branch_attn[0], w_branch_pool[0], w_out[0], w_ff1[0], w_ff2[0])
    big_m = (m_w_in[0], m_w_pool[0], m_w_branch_attn[0], m_w_branch_pool[0], m_w_out[0], m_w_ff1[0], m_w_ff2[0])
    big_v = (v_w_in[0], v_w_pool[0], v_w_branch_attn[0], v_w_branch_pool[0], v_w_out[0], v_w_ff1[0], v_w_ff2[0])
    shard_axes = (1, 1, 1, 1, 0, 1, 0)
    small_w = (pool_scale, ln_mix_g, ln_mix_b, ln_ff_g, ln_ff_b)
    small_m = (m_pool_scale, m_ln_mix_g, m_ln_mix_b, m_ln_ff_g, m_ln_ff_b)
    small_v = (v_pool_scale, v_ln_mix_g, v_ln_mix_b, v_ln_ff_g, v_ln_ff_b)

    gathered = _all_gather_weights([w.astype(BF16) for w in big_w], shard_axes)
    gw_in, gw_pool, gw_ba, gw_bp, gw_out, gw1, gw2 = gathered

    cos2, sin_fwd, sin_bwd = _rope_tables(positions[0])
    loss, grad_x, big_g, small_g = _local_step(
        x[0], cos2, sin_fwd, sin_bwd, gw_in, gw_pool, pool_scale, gw_ba, gw_bp, gw_out, ln_mix_g, ln_mix_b,
        gw1, gw2, ln_ff_g, ln_ff_b, loss_target[0])

    small_cat = jnp.concatenate(small_g, axis=-1)
    n_small = small_cat.shape[-1]
    small_2d = (n_small // HEAD_DIM, HEAD_DIM)
    send = [g.astype(BF16) for g in big_g] + [small_cat.reshape(small_2d)]
    parts = _reduce_scatter_grads(send, shard_axes + (None,))

    names = ("w_in", "w_pool", "w_branch_attn", "w_branch_pool", "w_out", "w_ff1", "w_ff2")
    grads, deltas, new_ms, new_vs = {}, {}, {}, {}
    for n, part, w, m, v in zip(names, parts[:7], big_w, big_m, big_v):
        shp = w.shape
        r2 = (-1, shp[-1])
        g, dlt, nm, nv = _adamw("adamw_" + n, part.reshape((N_DEV,) + w.reshape(r2).shape),
                                w.reshape(r2), m.reshape(r2), v.reshape(r2))
        full = (1,) + shp
        grads[n], deltas[n], new_ms[n], new_vs[n] = g.reshape(full), dlt.reshape(full), nm.reshape(full), nv.reshape(full)

    sw, sm, sv = (jnp.concatenate(t, axis=-1).reshape(small_2d) for t in (small_w, small_m, small_v))
    small_out = [t.reshape(1, n_small) for t in _adamw("adamw_small", parts[7], sw, sm, sv)]
    small_names = ("pool_scale", "ln_mix_g", "ln_mix_b", "ln_ff_g", "ln_ff_b")
    off = 0
    for n, w in zip(small_names, small_w):
        width = w.shape[-1]
        grads[n], deltas[n], new_ms[n], new_vs[n] = (t[:, off:off + width] for t in small_out)
        off += width

    order = ("w_in", "w_pool", "pool_scale", "w_branch_attn", "w_branch_pool", "w_out", "ln_mix_g", "ln_mix_b",
             "w_ff1", "w_ff2", "ln_ff_g", "ln_ff_b")
    total_loss = lax.psum(loss[0, 0], ("x", "y", "c"))
    return (total_loss, grad_x[None], *[grads[n] for n in order], *[deltas[n] for n in order],
            *[new_ms[n] for n in order], *[new_vs[n] for n in order])
```

```python
import functools

import jax
import jax.numpy as jnp
from jax import lax
from jax.experimental import pallas as pl
from jax.experimental.pallas import tpu as pltpu

F32 = jnp.float32
BF16 = jnp.bfloat16

N_DEV = 8
HEAD_DIM = 128
SUB_BLOCK = 128
DILATIONS = (1, 4, 16)
POOL_WINDOWS = (2, 4, 8, 16)
MAX_POOL_WINDOW = 16
POOL_HALO = 128
LN_EPS = 1e-5
DEEPNORM_ALPHA = 2.0 ** 0.25
ROPE_THETA = 10000.0
ATTN_SCALE = HEAD_DIM ** -0.5
ADAM_LR, ADAM_B1, ADAM_B2, ADAM_EPS, ADAM_WD, ADAM_STEP = 0.001, 0.9, 0.999, 1e-08, 0.01, 10
NEG_BIG = -1e30
VMEM_CAP_V7X = 64 * 1024 * 1024
MESH = pl.DeviceIdType.MESH


def _vmem_limit(est_bytes):
    return int(min(max(est_bytes * 5 // 4 + (4 << 20), 16 << 20), VMEM_CAP_V7X - (6 << 20)))


def _nbytes(shape, dtype):
    n = 1
    for s in shape:
        n *= s
    return n * jnp.dtype(dtype).itemsize


def _mm(name, a, b, form, tiles, outs, epi, extras=(), sequential=False, after=None):
    tm, tn, tk = tiles
    if form == "nn":
        (M, K), (K2, N) = a.shape, b.shape
    elif form == "nt":
        (M, K), (N, K2) = a.shape, b.shape
    else:
        (K, M), (K2, N) = a.shape, b.shape
    assert K == K2, (name, a.shape, b.shape)
    tm, tn, tk = min(tm, M), min(tn, N), min(tk, K)
    assert M % tm == 0 and N % tn == 0 and K % tk == 0, (name, M, N, K, tm, tn, tk)
    grid = (M // tm, N // tn, K // tk)
    nk = grid[2]
    if form == "nn":
        a_spec = pl.BlockSpec((tm, tk), lambda i, j, k: (i, k))
        b_spec = pl.BlockSpec((tk, tn), lambda i, j, k: (k, j))
        contract = ((1,), (0,))
    elif form == "nt":
        a_spec = pl.BlockSpec((tm, tk), lambda i, j, k: (i, k))
        b_spec = pl.BlockSpec((tn, tk), lambda i, j, k: (j, k))
        contract = ((1,), (1,))
    else:
        a_spec = pl.BlockSpec((tk, tm), lambda i, j, k: (k, i))
        b_spec = pl.BlockSpec((tk, tn), lambda i, j, k: (k, j))
        contract = ((0,), (0,))
    n_ex, n_out = len(extras), len(outs)
    n_after = 0 if after is None else 1

    def body(a_ref, b_ref, *rest):
        ex_refs = rest[:n_ex]
        rest = rest[n_ex + n_after:]
        out_refs = rest[:n_out]
        i, j, k = pl.program_id(0), pl.program_id(1), pl.program_id(2)

        def prod():
            return lax.dot_general(a_ref[...].astype(BF16), b_ref[...].astype(BF16),
                                   (contract, ((), ())), preferred_element_type=F32)

        if nk == 1:
            epi(prod(), ex_refs, out_refs, i, j)
        else:
            acc = rest[n_out]

            @pl.when(k == 0)
            def _():
                acc[...] = prod()

            @pl.when(k > 0)
            def _():
                acc[...] += prod()

            @pl.when(k == nk - 1)
            def _():
                epi(acc[...], ex_refs, out_refs, i, j)

    est = 2 * (_nbytes(a_spec.block_shape, a.dtype) + _nbytes(b_spec.block_shape, b.dtype))
    est += sum(2 * _nbytes(bs, arr.dtype) for arr, bs, _ in extras)
    est += sum(2 * _nbytes(bs, dt) for _, dt, bs, _ in outs)
    est += 4 * tm * tn * 4
    sem = ("arbitrary",) * 3 if sequential else ("parallel", "parallel", "arbitrary")
    return pl.pallas_call(
        body, name=name, grid=grid,
        in_specs=([a_spec, b_spec] + [pl.BlockSpec(bs, im) for _, bs, im in extras]
                  + [pl.BlockSpec(memory_space=pl.ANY)] * n_after),
        out_specs=[pl.BlockSpec(bs, im) for _, _, bs, im in outs],
        out_shape=[jax.ShapeDtypeStruct(sh, dt) for sh, dt, _, _ in outs],
        scratch_shapes=[pltpu.VMEM((tm, tn), F32)] if nk > 1 else [],
        compiler_params=pltpu.CompilerParams(dimension_semantics=sem, vmem_limit_bytes=_vmem_limit(est)),
    )(a, b, *[arr for arr, _, _ in extras], *([after] if n_after else []))


def _tile_out(shape, dtype, tm, tn):
    return (shape, dtype, (tm, tn), lambda i, j, k: (i, j))


def _row_sum_out(width):
    return ((1, width), F32, (1, width), lambda i, j, k: (0, 0))


def _accumulate_rows(ref, value, i):
    @pl.when(i == 0)
    def _():
        ref[...] = value

    @pl.when(i > 0)
    def _():
        ref[...] += value


def _layer_norm_bwd(dy, xhat, rstd, g):
    dxh = dy * g
    m1 = jnp.mean(dxh, axis=-1, keepdims=True)
    m2 = jnp.mean(dxh * xhat, axis=-1, keepdims=True)
    return rstd * (dxh - m1 - xhat * m2)


def _rope_apply(t, cos2, sin_signed):
    return t * cos2 + pltpu.roll(t, HEAD_DIM // 2, axis=1) * sin_signed


def _in_proj(xb, w_in, cos2, sin_fwd, n_rope_cols, after=None):
    S, D = xb.shape
    W = w_in.shape[1]
    tm, tn = min(1024, S), min(1024, W)
    assert n_rope_cols % tn == 0
    n_rope_tiles = n_rope_cols // tn

    def epi(acc, ex, out, i, j):
        cos_ref, sin_ref = ex
        (h_ref,) = out

        @pl.when(j < n_rope_tiles)
        def _():
            c, s = cos_ref[...], sin_ref[...]
            for hd in range(tn // HEAD_DIM):
                sl = slice(hd * HEAD_DIM, (hd + 1) * HEAD_DIM)
                h_ref[:, sl] = _rope_apply(acc[:, sl], c, s).astype(BF16)

        @pl.when(j >= n_rope_tiles)
        def _():
            h_ref[...] = acc.astype(BF16)

    row = lambda i, j, k: (i, 0)
    (h,) = _mm("in_proj", xb, w_in, "nn", (tm, tn, 512),
               [_tile_out((S, W), BF16, tm, tn)], epi,
               extras=[(cos2, (tm, HEAD_DIM), row), (sin_fwd, (tm, HEAD_DIM), row)], after=after)
    return h


def _attn_masks(mb):
    qi = lax.broadcasted_iota(jnp.int32, (SUB_BLOCK, SUB_BLOCK), 0)
    kj = lax.broadcasted_iota(jnp.int32, (SUB_BLOCK, SUB_BLOCK), 1)
    mask_cur = kj <= qi
    mask_prev = jnp.logical_and(kj >= qi, mb > 0)
    return mask_cur, mask_prev


def _dot_nt(a, b):
    return lax.dot_general(a, b, (((1,), (1,)), ((), ())), preferred_element_type=F32)


def _dot_tn(a, b):
    return lax.dot_general(a, b, (((0,), (0,)), ((), ())), preferred_element_type=F32)


def _dot_nn(a, b):
    return lax.dot_general(a, b, (((1,), (0,)), ((), ())), preferred_element_type=F32)


def _dilated_view_specs(h_width, aw, d):
    cpb = h_width // aw

    def spec(col, prev):
        if prev:
            return pl.BlockSpec((SUB_BLOCK, aw), lambda r, mb: (jnp.maximum(mb - 1, 0), r * cpb + col))
        return pl.BlockSpec((SUB_BLOCK, aw), lambda r, mb: (mb, r * cpb + col))

    return [spec(0, False), spec(1, True), spec(1, False), spec(2, True), spec(2, False)]


def _attn_fwd(h, d, aw):
    S, W = h.shape
    n_heads = aw // HEAD_DIM
    nb = S // d // SUB_BLOCK
    hv = h.reshape(S // d, d * W)

    def body(q_ref, kp_ref, kc_ref, vp_ref, vc_ref, o_ref, lse_ref):
        mask_cur, mask_prev = _attn_masks(pl.program_id(1))
        for hd in range(n_heads):
            sl = slice(hd * HEAD_DIM, (hd + 1) * HEAD_DIM)
            q = q_ref[:, sl]
            sc = jnp.where(mask_cur, _dot_nt(q, kc_ref[:, sl]) * ATTN_SCALE, NEG_BIG)
            sp = jnp.where(mask_prev, _dot_nt(q, kp_ref[:, sl]) * ATTN_SCALE, NEG_BIG)
            m = jnp.maximum(jnp.max(sc, axis=-1, keepdims=True), jnp.max(sp, axis=-1, keepdims=True))
            pc = jnp.exp(sc - m)
            pp = jnp.exp(sp - m)
            l = jnp.sum(pc, axis=-1, keepdims=True) + jnp.sum(pp, axis=-1, keepdims=True)
            o = _dot_nn(pc.astype(BF16), vc_ref[:, sl]) + _dot_nn(pp.astype(BF16), vp_ref[:, sl])
            o_ref[:, sl] = o / l
            lse_ref[:, sl] = jnp.broadcast_to(m + jnp.log(l), (SUB_BLOCK, HEAD_DIM))

    out_spec = pl.BlockSpec((SUB_BLOCK, aw), lambda r, mb: (mb, r))
    o, lse = pl.pallas_call(
        body, name=f"attn_fwd_d{d}", grid=(d, nb),
        in_specs=_dilated_view_specs(W, aw, d),
        out_specs=[out_spec, out_spec],
        out_shape=[jax.ShapeDtypeStruct((S // d, d * aw), F32)] * 2,
        compiler_params=pltpu.CompilerParams(dimension_semantics=("parallel", "parallel"),
                                             vmem_limit_bytes=_vmem_limit(16 << 20)),
    )(hv, hv, hv, hv, hv)
    return o.reshape(S, aw), lse.reshape(S, aw)


def _attn_combine(outs, lses):
    S, aw = outs[0].shape
    tm = min(512, S)

    def body(o1, o2, o3, l1, l2, l3, o_ref, lt_ref):
        a1, a2, a3 = l1[...], l2[...], l3[...]
        mx = jnp.maximum(jnp.maximum(a1, a2), a3)
        e1, e2, e3 = jnp.exp(a1 - mx), jnp.exp(a2 - mx), jnp.exp(a3 - mx)
        den = e1 + e2 + e3
        o_ref[...] = ((e1 * o1[...] + e2 * o2[...] + e3 * o3[...]) / den).astype(BF16)
        lt_ref[...] = mx + jnp.log(den)

    spec = pl.BlockSpec((tm, aw), lambda i: (i, 0))
    return pl.pallas_call(
        body, name="attn_combine", grid=(S // tm,),
        in_specs=[spec] * 6, out_specs=[spec, spec],
        out_shape=[jax.ShapeDtypeStruct((S, aw), BF16), jax.ShapeDtypeStruct((S, aw), F32)],
        compiler_params=pltpu.CompilerParams(dimension_semantics=("parallel",),
                                             vmem_limit_bytes=_vmem_limit(40 << 20)),
    )(*outs, *lses)


def _band(tm, width, w, row_offset, transpose):
    t = lax.broadcasted_iota(jnp.int32, (tm, width), 0)
    u = lax.broadcasted_iota(jnp.int32, (tm, width), 1)
    dist = (u - t - row_offset) if transpose else (t + row_offset - u)
    return jnp.logical_and(dist >= 0, dist < w).astype(BF16)


def _pool_fwd(h, w_pool, pool_scale, pw, u_col_block):
    S, W = h.shape
    n_groups = len(POOL_WINDOWS)
    gw = pw // n_groups
    tm = min(512, S)
    halo_per_tile = tm // POOL_HALO

    def body(uc_ref, uh_ref, w_ref, sc_ref, p_ref, y_ref, pm_ref):
        i = pl.program_id(0)
        t_abs = i * tm + lax.broadcasted_iota(jnp.int32, (tm, 1), 0)
        for g, w in enumerate(POOL_WINDOWS):
            sl = slice(g * gw, (g + 1) * gw)
            uc = uc_ref[:, sl]
            uh = jnp.where(i > 0, uh_ref[:, sl], jnp.zeros((POOL_HALO, gw), BF16))
            ssum = _dot_nn(_band(tm, tm, w, 0, False), uc) + _dot_nn(_band(tm, POOL_HALO, w, POOL_HALO, False), uh)
            cnt = jnp.minimum(t_abs + 1, w).astype(F32)
            p = (ssum / cnt - uc.astype(F32)).astype(BF16)
            y = _dot_nn(p, w_ref[g])
            p_ref[:, sl] = p
            y_ref[:, sl] = y.astype(BF16)
            pm_ref[:, sl] = (y * sc_ref[:, sl]).astype(BF16)

    row = pl.BlockSpec((tm, pw), lambda i: (i, 0))
    return pl.pallas_call(
        body, name="pool_fwd", grid=(S // tm,),
        in_specs=[pl.BlockSpec((tm, pw), lambda i: (i, u_col_block)),
                  pl.BlockSpec((POOL_HALO, pw), lambda i: (jnp.maximum(i * halo_per_tile - 1, 0), u_col_block)),
                  pl.BlockSpec((n_groups, gw, gw), lambda i: (0, 0, 0)),
                  pl.BlockSpec((1, pw), lambda i: (0, 0))],
        out_specs=[row, row, row],
        out_shape=[jax.ShapeDtypeStruct((S, pw), BF16)] * 3,
        compiler_params=pltpu.CompilerParams(dimension_semantics=("parallel",),
                                             vmem_limit_bytes=_vmem_limit(24 << 20)),
    )(h, h, w_pool, pool_scale)


def _branch_attn(o_attn, w_ba):
    S, _ = o_attn.shape
    D = w_ba.shape[1]
    tm, tn = min(1024, S), min(1024, D)

    def epi(acc, ex, out, i, j):
        out[0][...] = acc.astype(BF16)

    (y,) = _mm("branch_attn", o_attn, w_ba, "nn", (tm, tn, 1024), [_tile_out((S, D), BF16, tm, tn)], epi)
    return y


def _branch_pool_merge(pm, w_bp, h, y_attn, gate_col0):
    S, _ = pm.shape
    D = w_bp.shape[1]
    tm, tn = min(1024, S), min(1024, D)
    ga0, gp0 = gate_col0 // tn, (gate_col0 + D) // tn

    def epi(acc, ex, out, i, j):
        ga_ref, gp_ref, ya_ref = ex
        yp_ref, mg_ref = out
        yp = acc.astype(BF16)
        yp_ref[...] = yp
        mg = (jax.nn.sigmoid(ga_ref[...].astype(F32)) * ya_ref[...].astype(F32)
              + jax.nn.sigmoid(gp_ref[...].astype(F32)) * acc)
        mg_ref[...] = mg.astype(BF16)

    y_pool, merged = _mm(
        "branch_pool_merge", pm, w_bp, "nn", (tm, tn, 1024),
        [_tile_out((S, D), BF16, tm, tn), _tile_out((S, D), BF16, tm, tn)], epi,
        extras=[(h, (tm, tn), lambda i, j, k: (i, ga0 + j)), (h, (tm, tn), lambda i, j, k: (i, gp0 + j)),
                (y_attn, (tm, tn), lambda i, j, k: (i, j))])
    return y_pool, merged


def _layer_norm_rows(z, g, b):
    mu = jnp.mean(z, axis=-1, keepdims=True)
    zc = z - mu
    var = jnp.mean(zc * zc, axis=-1, keepdims=True)
    rstd = lax.rsqrt(var + LN_EPS)
    xhat = zc * rstd
    return xhat * g + b, xhat, rstd


def _out_proj_ln(merged, w_out, x, g, b):
    S, D = x.shape
    tm = min(512, S)

    def epi(acc, ex, out, i, j):
        x_ref, g_ref, b_ref = ex
        x1_ref, x1b_ref, xh_ref, rs_ref = out
        y, xhat, rstd = _layer_norm_rows(DEEPNORM_ALPHA * x_ref[...] + acc, g_ref[...], b_ref[...])
        x1_ref[...] = y
        x1b_ref[...] = y.astype(BF16)
        xh_ref[...] = xhat
        rs_ref[...] = jnp.broadcast_to(rstd, (tm, HEAD_DIM))

    row = lambda i, j, k: (i, 0)
    vec = lambda i, j, k: (0, 0)
    return _mm("out_proj_ln", merged, w_out, "nn", (tm, D, 512),
               [((S, D), F32, (tm, D), row), ((S, D), BF16, (tm, D), row), ((S, D), F32, (tm, D), row),
                ((S, HEAD_DIM), F32, (tm, HEAD_DIM), row)], epi,
               extras=[(x, (tm, D), row), (g, (1, D), vec), (b, (1, D), vec)])


def _ffn_up(x1b, w1):
    S, D = x1b.shape
    F = w1.shape[1]
    tm, tn = min(1024, S), min(2048, F)

    def epi(acc, ex, out, i, j):
        r = jnp.maximum(acc, 0.0)
        out[0][...] = (r * r).astype(BF16)

    (a,) = _mm("ffn_up", x1b, w1, "nn", (tm, tn, 512), [_tile_out((S, F), BF16, tm, tn)], epi)
    return a


def _ffn_down_loss(a, w2, x1, g, b, target):
    S, D = x1.shape
    tm = min(512, S)

    def epi(acc, ex, out, i, j):
        x1_ref, g_ref, b_ref, t_ref = ex
        dz_ref, dzb_ref, dg_ref, db_ref, loss_ref = out
        gv = g_ref[...]
        y, xhat, rstd = _layer_norm_rows(DEEPNORM_ALPHA * x1_ref[...] + acc, gv, b_ref[...])
        err = y - t_ref[...]
        loss = 0.5 * jnp.sum(jnp.mean(err * err, axis=-1, keepdims=True), axis=0, keepdims=True)
        dy = err * (1.0 / D)
        dz = _layer_norm_bwd(dy, xhat, rstd, gv)
        dz_ref[...] = dz
        dzb_ref[...] = dz.astype(BF16)
        _accumulate_rows(dg_ref, jnp.sum(dy * xhat, axis=0, keepdims=True), i)
        _accumulate_rows(db_ref, jnp.sum(dy, axis=0, keepdims=True), i)
        _accumulate_rows(loss_ref, jnp.broadcast_to(loss, (1, HEAD_DIM)), i)

    row = lambda i, j, k: (i, 0)
    vec = lambda i, j, k: (0, 0)
    return _mm("ffn_down_loss", a, w2, "nn", (tm, D, 512),
               [((S, D), F32, (tm, D), row), ((S, D), BF16, (tm, D), row),
                _row_sum_out(D), _row_sum_out(D), _row_sum_out(HEAD_DIM)], epi,
               extras=[(x1, (tm, D), row), (g, (1, D), vec), (b, (1, D), vec), (target, (tm, D), row)],
               sequential=True)


def _grad_weight(name, act, cot):
    M, N = act.shape[1], cot.shape[1]
    tm, tn = min(1024, M), min(2048, N)

    def epi(acc, ex, out, i, j):
        out[0][...] = acc.astype(BF16)

    (g,) = _mm(name, act, cot, "tn", (tm, tn, 512), [_tile_out((M, N), BF16, tm, tn)], epi)
    return g


def _ffn_down_bwd(dz2b, w2, a, after=None):
    S, D = dz2b.shape
    F = w2.shape[0]
    tm, tn = min(1024, S), min(2048, F)

    def epi(acc, ex, out, i, j):
        out[0][...] = (acc * (2.0 * jnp.sqrt(ex[0][...].astype(F32)))).astype(BF16)

    (dh1,) = _mm("ffn_down_bwd", dz2b, w2, "nt", (tm, tn, 512), [_tile_out((S, F), BF16, tm, tn)], epi,
                 extras=[(a, (tm, tn), lambda i, j, k: (i, j))], after=after)
    return dh1


def _ffn_up_bwd_ln(dh1, w1, dz2, xhat1, rstd1, g1, after=None):
    S, D = dz2.shape
    tm = min(512, S)

    def epi(acc, ex, out, i, j):
        dz2_ref, xh_ref, rs_ref, g_ref = ex
        dz_ref, dzb_ref, dg_ref, db_ref = out
        dy = DEEPNORM_ALPHA * dz2_ref[...] + acc
        xhat = xh_ref[...]
        dz = _layer_norm_bwd(dy, xhat, rs_ref[:, :1], g_ref[...])
        dz_ref[...] = dz
        dzb_ref[...] = dz.astype(BF16)
        _accumulate_rows(dg_ref, jnp.sum(dy * xhat, axis=0, keepdims=True), i)
        _accumulate_rows(db_ref, jnp.sum(dy, axis=0, keepdims=True), i)

    row = lambda i, j, k: (i, 0)
    vec = lambda i, j, k: (0, 0)
    return _mm("ffn_up_bwd_ln", dh1, w1, "nt", (tm, D, 512),
               [((S, D), F32, (tm, D), row), ((S, D), BF16, (tm, D), row), _row_sum_out(D), _row_sum_out(D)], epi,
               extras=[(dz2, (tm, D), row), (xhat1, (tm, D), row), (rstd1, (tm, HEAD_DIM), row), (g1, (1, D), vec)],
               sequential=True, after=after)


def _out_proj_bwd(dz1b, w_out, h, y_attn, y_pool, gate_col0, after=None):
    S, D = dz1b.shape
    W = h.shape[1]
    tm = min(256, S)
    assert gate_col0 == 2 * D and W == 4 * D

    def epi(acc, ex, out, i, j):
        gates_ref, ya_ref, yp_ref = ex
        dya_ref, dyp_ref, dh_ref = out
        sa = jax.nn.sigmoid(gates_ref[:, :D].astype(F32))
        sp = jax.nn.sigmoid(gates_ref[:, D:].astype(F32))
        dya_ref[...] = (acc * sa).astype(BF16)
        dyp_ref[...] = (acc * sp).astype(BF16)
        dh_ref[:, :D] = (acc * ya_ref[...].astype(F32) * (sa * (1.0 - sa))).astype(BF16)
        dh_ref[:, D:] = (acc * yp_ref[...].astype(F32) * (sp * (1.0 - sp))).astype(BF16)

    row = lambda i, j, k: (i, 0)
    return _mm("out_proj_bwd", dz1b, w_out, "nt", (tm, D, 512),
               [((S, D), BF16, (tm, D), row), ((S, D), BF16, (tm, D), row),
                ((S, W), BF16, (tm, 2 * D), lambda i, j, k: (i, 1))], epi,
               extras=[(h, (tm, 2 * D), lambda i, j, k: (i, 1)), (y_attn, (tm, D), row), (y_pool, (tm, D), row)],
               after=after)


def _branch_attn_bwd(dy_attn, w_ba, o_attn, after=None):
    S, D = dy_attn.shape
    aw = w_ba.shape[0]
    tm = min(512, S)

    def epi(acc, ex, out, i, j):
        do_ref, dl_ref = out
        do_ref[...] = acc.astype(BF16)
        o = ex[0][...].astype(F32)
        for hd in range(aw // HEAD_DIM):
            sl = slice(hd * HEAD_DIM, (hd + 1) * HEAD_DIM)
            dl = jnp.sum(acc[:, sl] * o[:, sl], axis=-1, keepdims=True)
            dl_ref[:, sl] = jnp.broadcast_to(dl, (tm, HEAD_DIM))

    row = lambda i, j, k: (i, 0)
    return _mm("branch_attn_bwd", dy_attn, w_ba, "nt", (tm, aw, 512),
               [((S, aw), BF16, (tm, aw), row), ((S, aw), F32, (tm, aw), row)], epi,
               extras=[(o_attn, (tm, aw), row)], after=after)


def _branch_pool_bwd(dy_pool, w_bp, y_pre, pool_scale):
    S, D = dy_pool.shape
    pw = w_bp.shape[0]
    tm = min(512, S)

    def epi(acc, ex, out, i, j):
        y_ref, sc_ref = ex
        dyp_ref, dsc_ref = out
        dyp_ref[...] = (acc * sc_ref[...]).astype(BF16)
        _accumulate_rows(dsc_ref, jnp.sum(acc * y_ref[...].astype(F32), axis=0, keepdims=True), i)

    row = lambda i, j, k: (i, 0)
    return _mm("branch_pool_bwd", dy_pool, w_bp, "nt", (tm, pw, 512),
               [((S, pw), BF16, (tm, pw), row), _row_sum_out(pw)], epi,
               extras=[(y_pre, (tm, pw), row), (pool_scale, (1, pw), lambda i, j, k: (0, 0))],
               sequential=True)


def _pool_bwd(dh, dy_pre, p, w_pool, pw, u_col_block):
    S, W = dh.shape
    n_groups = len(POOL_WINDOWS)
    gw = pw // n_groups
    tm = min(512, S)
    n_tiles = S // tm
    halo_per_tile = tm // POOL_HALO
    n_halo_blocks = S // POOL_HALO

    def body(dh_in_ref, dyc_ref, dyh_ref, p_ref, w_ref, dh_ref, dw_ref):
        del dh_in_ref
        i = pl.program_id(0)
        t_cur = i * tm + lax.broadcasted_iota(jnp.int32, (tm, 1), 0)
        t_halo = (i + 1) * tm + lax.broadcasted_iota(jnp.int32, (POOL_HALO, 1), 0)
        for g, w in enumerate(POOL_WINDOWS):
            sl = slice(g * gw, (g + 1) * gw)
            wg = w_ref[g]
            dyc = dyc_ref[:, sl]
            dyh = jnp.where(i < n_tiles - 1, dyh_ref[:, sl], jnp.zeros((POOL_HALO, gw), BF16))
            dp_cur = _dot_nt(dyc, wg)
            dp_halo = _dot_nt(dyh, wg)
            dpc_cur = (dp_cur / jnp.minimum(t_cur + 1, w).astype(F32)).astype(BF16)
            dpc_halo = (dp_halo / jnp.minimum(t_halo + 1, w).astype(F32)).astype(BF16)
            du = (_dot_nn(_band(tm, tm, w, 0, True), dpc_cur)
                  + _dot_nn(_band(tm, POOL_HALO, w, -tm, True), dpc_halo) - dp_cur)
            dh_ref[:, sl] = du.astype(BF16)
            dw = _dot_tn(p_ref[:, sl], dyc)

            @pl.when(i == 0)
            def _():
                dw_ref[g] = dw

            @pl.when(i > 0)
            def _():
                dw_ref[g] += dw

    row = pl.BlockSpec((tm, pw), lambda i: (i, 0))
    dh_new, dw_pool = pl.pallas_call(
        body, name="pool_bwd", grid=(n_tiles,),
        in_specs=[pl.BlockSpec(memory_space=pl.ANY), row,
                  pl.BlockSpec((POOL_HALO, pw), lambda i: (jnp.minimum((i + 1) * halo_per_tile, n_halo_blocks - 1), 0)),
                  row, pl.BlockSpec((n_groups, gw, gw), lambda i: (0, 0, 0))],
        out_specs=[pl.BlockSpec((tm, pw), lambda i: (i, u_col_block)),
                   pl.BlockSpec((n_groups, gw, gw), lambda i: (0, 0, 0))],
        out_shape=[jax.ShapeDtypeStruct((S, W), BF16), jax.ShapeDtypeStruct((n_groups, gw, gw), F32)],
        input_output_aliases={0: 0},
        compiler_params=pltpu.CompilerParams(dimension_semantics=("arbitrary",),
                                             vmem_limit_bytes=_vmem_limit(24 << 20)),
    )(dh, dy_pre, dy_pre, p, w_pool)
    return dh_new, dw_pool


def _attn_bwd(h, d_out, l_tot, delta, d, aw):
    S, W = h.shape
    n_heads = aw // HEAD_DIM
    nb = S // d // SUB_BLOCK
    hv = h.reshape(S // d, d * W)
    view = lambda t: t.reshape(S // d, d * aw)

    def body(q_ref, kp_ref, kc_ref, vp_ref, vc_ref, do_ref, l_ref, dl_ref,
             dq_ref, dkc_ref, dkp_ref, dvc_ref, dvp_ref):
        mask_cur, mask_prev = _attn_masks(pl.program_id(1))
        for hd in range(n_heads):
            sl = slice(hd * HEAD_DIM, (hd + 1) * HEAD_DIM)
            q, kc, kp, vc, vp, do = q_ref[:, sl], kc_ref[:, sl], kp_ref[:, sl], vc_ref[:, sl], vp_ref[:, sl], do_ref[:, sl]
            lt, dl = l_ref[:, sl], dl_ref[:, sl]
            pc = jnp.where(mask_cur, jnp.exp(jnp.where(mask_cur, _dot_nt(q, kc) * ATTN_SCALE - lt, NEG_BIG)), 0.0)
            pp = jnp.where(mask_prev, jnp.exp(jnp.where(mask_prev, _dot_nt(q, kp) * ATTN_SCALE - lt, NEG_BIG)), 0.0)
            dsc = (pc * (_dot_nt(do, vc) - dl) * ATTN_SCALE).astype(BF16)
            dsp = (pp * (_dot_nt(do, vp) - dl) * ATTN_SCALE).astype(BF16)
            dq_ref[:, sl] = _dot_nn(dsc, kc) + _dot_nn(dsp, kp)
            dkc_ref[:, sl] = _dot_tn(dsc, q)
            dkp_ref[:, sl] = _dot_tn(dsp, q)
            dvc_ref[:, sl] = _dot_tn(pc.astype(BF16), do)
            dvp_ref[:, sl] = _dot_tn(pp.astype(BF16), do)

    blk = pl.BlockSpec((SUB_BLOCK, aw), lambda r, mb: (mb, r))
    outs = pl.pallas_call(
        body, name=f"attn_bwd_d{d}", grid=(d, nb),
        in_specs=_dilated_view_specs(W, aw, d) + [blk, blk, blk],
        out_specs=[blk] * 5,
        out_shape=[jax.ShapeDtypeStruct((S // d, d * aw), F32)] * 5,
        compiler_params=pltpu.CompilerParams(dimension_semantics=("parallel", "parallel"),
                                             vmem_limit_bytes=_vmem_limit(24 << 20)),
    )(hv, hv, hv, hv, hv, view(d_out), view(l_tot), view(delta))
    return [t.reshape(S, aw) for t in outs]


def _attn_bwd_finish(dh, per_pattern, cos2, sin_bwd, aw):
    S, W = dh.shape
    n_heads = aw // HEAD_DIM
    tb = SUB_BLOCK
    nblk = S // tb
    n_pat = len(DILATIONS)

    def body(*refs):
        dq_refs = refs[1:1 + n_pat]
        dkc_refs = refs[1 + n_pat:1 + 2 * n_pat]
        dkp_refs = refs[1 + 2 * n_pat:1 + 3 * n_pat]
        dvc_refs = refs[1 + 3 * n_pat:1 + 4 * n_pat]
        dvp_refs = refs[1 + 4 * n_pat:1 + 5 * n_pat]
        cos_ref, sin_ref = refs[1 + 5 * n_pat], refs[2 + 5 * n_pat]
        out_ref = refs[3 + 5 * n_pat]
        i = pl.program_id(0)
        dq = dq_refs[0][...]
        dk = dkc_refs[0][...]
        dv = dvc_refs[0][...]
        for pi in range(1, n_pat):
            dq = dq + dq_refs[pi][...]
            dk = dk + dkc_refs[pi][...]
            dv = dv + dvc_refs[pi][...]
        for pi, d in enumerate(DILATIONS):
            has_next = i + d < nblk
            dk = dk + jnp.where(has_next, dkp_refs[pi][...], 0.0)
            dv = dv + jnp.where(has_next, dvp_refs[pi][...], 0.0)
        c, s = cos_ref[...], sin_ref[...]
        for hd in range(n_heads):
            sl = slice(hd * HEAD_DIM, (hd + 1) * HEAD_DIM)
            out_ref[:, sl] = _rope_apply(dq[:, sl], c, s).astype(BF16)
            out_ref[:, aw + hd * HEAD_DIM:aw + (hd + 1) * HEAD_DIM] = _rope_apply(dk[:, sl], c, s).astype(BF16)
        out_ref[:, 2 * aw:] = dv.astype(BF16)

    cur = pl.BlockSpec((tb, aw), lambda i: (i, 0))

    def shifted(d):
        return pl.BlockSpec((tb, aw), lambda i: (jnp.minimum(i + d, nblk - 1), 0))

    dqs = [pp[0] for pp in per_pattern]
    dkcs = [pp[1] for pp in per_pattern]
    dkps = [pp[2] for pp in per_pattern]
    dvcs = [pp[3] for pp in per_pattern]
    dvps = [pp[4] for pp in per_pattern]
    rope_spec = pl.BlockSpec((tb, HEAD_DIM), lambda i: (i, 0))
    return pl.pallas_call(
        body, name="attn_bwd_finish", grid=(nblk,),
        in_specs=([pl.BlockSpec(memory_space=pl.ANY)] + [cur] * (2 * n_pat) + [shifted(d) for d in DILATIONS]
                  + [cur] * n_pat + [shifted(d) for d in DILATIONS] + [rope_spec, rope_spec]),
        out_specs=pl.BlockSpec((tb, 3 * aw), lambda i: (i, 0)),
        out_shape=jax.ShapeDtypeStruct((S, W), BF16),
        input_output_aliases={0: 0},
        compiler_params=pltpu.CompilerParams(dimension_semantics=("parallel",),
                                             vmem_limit_bytes=_vmem_limit(24 << 20)),
    )(dh, *dqs, *dkcs, *dkps, *dvcs, *dvps, cos2, sin_bwd)


def _in_proj_bwd(dh, w_in, dz1, after=None):
    S, D = dz1.shape
    tm, tn = min(1024, S), min(1024, D)

    def epi(acc, ex, out, i, j):
        out[0][...] = DEEPNORM_ALPHA * ex[0][...] + acc

    (gx,) = _mm("in_proj_bwd", dh, w_in, "nt", (tm, tn, 512), [_tile_out((S, D), F32, tm, tn)], epi,
                extras=[(dz1, (tm, tn), lambda i, j, k: (i, j))], after=after)
    return gx


def _my_place():
    x, y, c = lax.axis_index("x"), lax.axis_index("y"), lax.axis_index("c")
    return x, y, c


def _flat(px, py, pc):
    return 4 * px + 2 * py + pc


def _shard_slice(ref, axis, idx, size):
    start = pl.multiple_of(idx * size, size)
    ix = [slice(None)] * len(ref.shape)
    ix[axis] = pl.ds(start, size)
    return ref.at[tuple(ix)]


def _all_gather_weights(shards, axes):
    n_w = len(shards)
    full_shapes = []
    for s, ax in zip(shards, axes):
        sh = list(s.shape)
        sh[ax] *= N_DEV
        full_shapes.append(tuple(sh))

    def body(*refs):
        src = refs[:n_w]
        dst = refs[n_w:2 * n_w]
        send_sems, recv_sems, local_sems = refs[2 * n_w:]
        x, y, c = _my_place()
        me, sibling = (x, y, c), (x, y, 1 - c)
        chips = [(1 - x, y), (x, 1 - y), (1 - x, 1 - y)]

        def place(w, dev):
            return _shard_slice(dst[w], axes[w], _flat(*dev), src[w].shape[axes[w]])

        def copy(w, k, block, to, from_src=False):
            return pltpu.make_async_remote_copy(
                src_ref=src[w] if from_src else place(w, block), dst_ref=place(w, block),
                send_sem=send_sems.at[w * 7 + k], recv_sem=recv_sems.at[w * 7 + k], device_id=to, device_id_type=MESH)

        mine, first, passed = [], [], []
        for w in range(n_w):
            cp = pltpu.make_async_copy(src[w], place(w, me), local_sems.at[w])
            cp.start()
            mine.append(cp)
            fw = [copy(w, 0, me, sibling, True)] + [copy(w, 1 + j, me, (*chip, c), True) for j, chip in enumerate(chips)]
            for cp in fw:
                cp.start()
            first += fw
        for w in range(n_w):
            for j, chip in enumerate(chips):
                copy(w, 1 + j, (*chip, c), me).wait_recv()
                cp = copy(w, 4 + j, (*chip, c), sibling)
                cp.start()
                passed.append(cp)
        for w in range(n_w):
            copy(w, 0, sibling, me).wait_recv()
            for j, chip in enumerate(chips):
                copy(w, 4 + j, (*chip, 1 - c), me).wait_recv()
        for cp in first + passed:
            cp.wait_send()
        for cp in mine:
            cp.wait()

    any_spec = pl.BlockSpec(memory_space=pl.ANY)
    return pl.pallas_call(
        body, name="all_gather_weights",
        in_specs=[any_spec] * n_w, out_specs=[any_spec] * n_w,
        out_shape=[jax.ShapeDtypeStruct(sh, s.dtype) for sh, s in zip(full_shapes, shards)],
        scratch_shapes=[pltpu.SemaphoreType.DMA((n_w * 7,)), pltpu.SemaphoreType.DMA((n_w * 7,)),
                        pltpu.SemaphoreType.DMA((n_w,))],
    )(*shards)


_HBM_SPEC = pl.BlockSpec(memory_space=pltpu.HBM)
_SEM_SPEC = pl.BlockSpec(memory_space=pltpu.SEMAPHORE)
_ANY_SPEC = pl.BlockSpec(memory_space=pl.ANY)
_N_PEER = N_DEV - 1


def _peer_of(x, y, c, r):
    return (x ^ ((r >> 2) & 1), y ^ ((r >> 1) & 1), c ^ (r & 1))


class _Exchange:
    def __init__(self, name, part, slot):
        self.name, self.part, self.slot = name, part, slot

    def _copy(self, w, r, src, land, send_sems, recv_sems, sending):
        x, y, c = _my_place()
        peer = _peer_of(x, y, c, r)
        return pltpu.make_async_remote_copy(
            src_ref=self.part(w, src, _flat(*peer)),
            dst_ref=self.slot(w, land, _flat(x, y, c) if sending else _flat(*peer)),
            send_sem=send_sems.at[w * _N_PEER + r - 1], recv_sem=recv_sems.at[w * _N_PEER + r - 1],
            device_id=peer, device_id_type=MESH)

    def start(self, srcs, lands, after):
        n = len(srcs)

        def body(*refs):
            src, land = refs[:n], refs[n:2 * n]
            send_sems, recv_sems = refs[2 * n + 1], refs[2 * n + 2]
            token = refs[2 * n + 3 + 2 * n]
            for w in range(n):
                for r in range(1, N_DEV):
                    self._copy(w, r, src[w], land[w], send_sems, recv_sems, True).start()
            token[...] = jnp.zeros_like(token)

        sems = pltpu.SemaphoreType.DMA((n * _N_PEER,))
        outs = pl.pallas_call(
            body, name=self.name + "_start",
            out_shape=(sems, sems, *[pltpu.HBM(t.shape, t.dtype) for t in list(srcs) + list(lands)],
                       jax.ShapeDtypeStruct((8, 128), F32)),
            in_specs=[_HBM_SPEC] * (2 * n) + [_ANY_SPEC],
            out_specs=(_SEM_SPEC, _SEM_SPEC, *[_HBM_SPEC] * (2 * n), pl.BlockSpec(memory_space=pltpu.VMEM)),
            input_output_aliases={i: 2 + i for i in range(2 * n)},
            compiler_params=pltpu.CompilerParams(has_side_effects=pltpu.SideEffectType.DATAFLOW_SIDE_EFFECTING),
        )(*[pltpu.with_memory_space_constraint(t, pltpu.HBM) for t in list(srcs) + list(lands)], after)
        return outs[0], outs[1], outs[2:2 + n], outs[2 + n:2 + 2 * n], outs[2 + 2 * n]

    def wait(self, started, after):
        send_sems, recv_sems, srcs, lands, _ = started
        n = len(srcs)

        def body(*refs):
            src, land = refs[:n], refs[n:2 * n]
            s_sems, r_sems = refs[2 * n], refs[2 * n + 1]
            for w in range(n):
                for r in range(1, N_DEV):
                    cp = self._copy(w, r, src[w], land[w], s_sems, r_sems, False)
                    cp.wait_send()
                    cp.wait_recv()

        outs = pl.pallas_call(
            body, name=self.name + "_wait",
            out_shape=[pltpu.HBM(t.shape, t.dtype) for t in list(srcs) + list(lands)],
            in_specs=[_HBM_SPEC] * (2 * n) + [_SEM_SPEC, _SEM_SPEC, _ANY_SPEC],
            out_specs=[_HBM_SPEC] * (2 * n),
            input_output_aliases={i: i for i in range(2 * n)},
            compiler_params=pltpu.CompilerParams(has_side_effects=pltpu.SideEffectType.DATAFLOW_SIDE_EFFECTING),
        )(*srcs, *lands, send_sems, recv_sems, after)
        return outs[n:]


def _gather_exchange(name, axes, shard_sizes):
    return _Exchange(name, lambda w, src, dev: src,
                     lambda w, land, dev: _shard_slice(land, axes[w], dev, shard_sizes[w]))


def _scatter_exchange(name, axes, shard_sizes):
    def part(w, src, dev):
        return src if axes[w] is None else _shard_slice(src, axes[w], dev, shard_sizes[w])
    return _Exchange(name, part, lambda w, land, dev: land.at[dev])


def _adamw(name, partials, w, m, v):
    R, C = w.shape
    tr = R
    while tr * C * 4 > (1 << 20) and tr % 16 == 0:
        tr //= 2

    def body(p_ref, w_ref, m_ref, v_ref, g_ref, d_ref, nm_ref, nv_ref):
        g = p_ref[0].astype(F32)
        for jdev in range(1, N_DEV):
            g = g + p_ref[jdev].astype(F32)
        nm = ADAM_B1 * m_ref[...] + (1.0 - ADAM_B1) * g
        nv = ADAM_B2 * v_ref[...] + (1.0 - ADAM_B2) * (g * g)
        m_hat = nm / (1.0 - ADAM_B1 ** ADAM_STEP)
        v_hat = nv / (1.0 - ADAM_B2 ** ADAM_STEP)
        g_ref[...] = g
        d_ref[...] = -ADAM_LR * (m_hat / (jnp.sqrt(v_hat) + ADAM_EPS) + ADAM_WD * w_ref[...])
        nm_ref[...] = nm
        nv_ref[...] = nv

    spec = pl.BlockSpec((tr, C), lambda i: (i, 0))
    return pl.pallas_call(
        body, name=name, grid=(R // tr,),
        in_specs=[pl.BlockSpec((N_DEV, tr, C), lambda i: (0, i, 0)), spec, spec, spec],
        out_specs=[spec] * 4,
        out_shape=[jax.ShapeDtypeStruct((R, C), F32)] * 4,
        compiler_params=pltpu.CompilerParams(dimension_semantics=("parallel",),
                                             vmem_limit_bytes=_vmem_limit(24 << 20)),
    )(partials, w, m, v)


def _local_step(x, cos2, sin_fwd, sin_bwd, w_in, mix_weights, ffn_weights, pool_scale, g_mix, b_mix, g_ff, b_ff,
                target, send, start_after=None):
    S, D = x.shape
    aw = pw = D // 2
    u_col_block = 3
    gate_col0 = 4 * aw

    xb = x.astype(BF16)
    h = _in_proj(xb, w_in, cos2, sin_fwd, 2 * aw, after=start_after)
    fwd = [_attn_fwd(h, d, aw) for d in DILATIONS]
    o_attn, l_tot = _attn_combine([f[0] for f in fwd], [f[1] for f in fwd])
    w_pool, w_ba, w_bp, w_out = mix_weights(o_attn)
    p, y_pre, pm = _pool_fwd(h, w_pool, pool_scale, pw, u_col_block)
    y_attn = _branch_attn(o_attn, w_ba)
    y_pool, merged = _branch_pool_merge(pm, w_bp, h, y_attn, gate_col0)
    w1, w2 = ffn_weights(merged)
    x1, x1b, xhat1, rstd1 = _out_proj_ln(merged, w_out, x, g_mix, b_mix)
    a = _ffn_up(x1b, w1)
    dz2, dz2b, dg_ff, db_ff, loss = _ffn_down_loss(a, w2, x1, g_ff, b_ff, target)

    tok = send("ff2", [_grad_weight("grad_w_ff2", a, dz2b)])
    dh1 = _ffn_down_bwd(dz2b, w2, a, after=tok)
    tok = send("ff1", [_grad_weight("grad_w_ff1", x1b, dh1)])
    dz1, dz1b, dg_mix, db_mix = _ffn_up_bwd_ln(dh1, w1, dz2, xhat1, rstd1, g_mix, after=tok)
    tok = send("out", [_grad_weight("grad_w_out", merged, dz1b)])
    dy_attn, dy_pool, dh = _out_proj_bwd(dz1b, w_out, h, y_attn, y_pool, gate_col0, after=tok)
    tok = send("branch", [_grad_weight("grad_w_branch_attn", o_attn, dy_attn),
                          _grad_weight("grad_w_branch_pool", pm, dy_pool)])
    d_out, delta = _branch_attn_bwd(dy_attn, w_ba, o_attn, after=tok)
    dy_pre, d_scale = _branch_pool_bwd(dy_pool, w_bp, y_pre, pool_scale)
    dh, dw_pool = _pool_bwd(dh, dy_pre, p, w_pool, pw, u_col_block)
    per_pattern = [_attn_bwd(h, d_out, l_tot, delta, d, aw) for d in DILATIONS]
    dh = _attn_bwd_finish(dh, per_pattern, cos2, sin_bwd, aw)
    small = jnp.concatenate((d_scale, dg_mix, db_mix, dg_ff, db_ff), axis=-1)
    tok = send("in", [_grad_weight("grad_w_in", xb, dh), dw_pool.astype(BF16),
                      small.reshape(small.shape[-1] // HEAD_DIM, HEAD_DIM)])
    grad_x = _in_proj_bwd(dh, w_in, dz1, after=tok)
    return loss, grad_x


def _rope_tables(positions):
    half = HEAD_DIM // 2
    inv_freq = ROPE_THETA ** (-jnp.arange(half, dtype=F32) / half)
    ang = positions.astype(F32)[:, None] * inv_freq
    cos, sin = jnp.cos(ang), jnp.sin(ang)
    cos2 = jnp.concatenate([cos, cos], axis=-1)
    sin_fwd = jnp.concatenate([-sin, sin], axis=-1)
    return cos2, sin_fwd, -sin_fwd


def kernel(x, positions, w_in, w_pool, pool_scale, w_branch_attn, w_branch_pool, w_out, ln_mix_g, ln_mix_b, w_ff1, w_ff2, ln_ff_g, ln_ff_b, loss_target, m_w_in, m_w_pool, m_pool_scale, m_w_branch_attn, m_w_branch_pool, m_w_out, m_ln_mix_g, m_ln_mix_b, m_w_ff1, m_w_ff2, m_ln_ff_g, m_ln_ff_b, v_w_in, v_w_pool, v_pool_scale, v_w_branch_attn, v_w_branch_pool, v_w_out, v_ln_mix_g, v_ln_mix_b, v_w_ff1, v_w_ff2, v_ln_ff_g, v_ln_ff_b):
    big_w = (w_in[0], w_pool[0], w_branch_attn[0], w_branch_pool[0], w_out[0], w_ff1[0], w_ff2[0])
    big_m = (m_w_in[0], m_w_pool[0], m_w_branch_attn[0], m_w_branch_pool[0], m_w_out[0], m_w_ff1[0], m_w_ff2[0])
    big_v = (v_w_in[0], v_w_pool[0], v_w_branch_attn[0], v_w_branch_pool[0], v_w_out[0], v_w_ff1[0], v_w_ff2[0])
    shard_axes = (1, 1, 1, 1, 0, 1, 0)
    small_w = (pool_scale, ln_mix_g, ln_mix_b, ln_ff_g, ln_ff_b)
    small_m = (m_pool_scale, m_ln_mix_g, m_ln_mix_b, m_ln_ff_g, m_ln_ff_b)
    small_v = (v_pool_scale, v_ln_mix_g, v_ln_mix_b, v_ln_ff_g, v_ln_ff_b)

    me = 4 * lax.axis_index("x") + 2 * lax.axis_index("y") + lax.axis_index("c")
    names = ("w_in", "w_pool", "w_branch_attn", "w_branch_pool", "w_out", "w_ff1", "w_ff2")
    axis_of = dict(zip(names, shard_axes))
    shard_of = dict(zip(names, [w.astype(BF16) for w in big_w]))

    def own_place(n):
        s, ax = shard_of[n], axis_of[n]
        full = list(s.shape)
        full[ax] *= N_DEV
        return lax.dynamic_update_slice_in_dim(lax.empty(tuple(full), s.dtype), s, me * s.shape[ax], ax)

    def gather_group(tag, group, after):
        ex = _gather_exchange(tag, [axis_of[n] for n in group], [shard_of[n].shape[axis_of[n]] for n in group])
        return ex, ex.start([shard_of[n] for n in group], [own_place(n) for n in group], after)

    (gw_in,) = _all_gather_weights([shard_of["w_in"]], (axis_of["w_in"],))
    mix_ex, mix_started = gather_group("gather_mix", ("w_pool", "w_branch_attn", "w_branch_pool", "w_out"), gw_in)
    ffn_ex, ffn_started = gather_group("gather_ffn", ("w_ff1", "w_ff2"), mix_started[4])

    groups = {"ff2": ("w_ff2",), "ff1": ("w_ff1",), "out": ("w_out",),
              "branch": ("w_branch_attn", "w_branch_pool"), "in": ("w_in", "w_pool", "small")}
    sent = {}

    def send(key, grads_):
        axes = [axis_of.get(n) for n in groups[key]]
        sizes = [None if ax is None else g.shape[ax] // N_DEV for g, ax in zip(grads_, axes)]
        lands = []
        for g, ax, size in zip(grads_, axes, sizes):
            own = g if ax is None else lax.dynamic_slice_in_dim(g, me * size, size, ax)
            lands.append(lax.dynamic_update_slice_in_dim(lax.empty((N_DEV,) + own.shape, g.dtype), own[None], me, 0))
        ex = _scatter_exchange("scatter_" + key, axes, sizes)
        sent[key] = (ex, ex.start(list(grads_), lands, grads_[0]))
        return sent[key][1][4]

    cos2, sin_fwd, sin_bwd = _rope_tables(positions[0])
    loss, grad_x = _local_step(
        x[0], cos2, sin_fwd, sin_bwd, gw_in, lambda after: mix_ex.wait(mix_started, after),
        lambda after: ffn_ex.wait(ffn_started, after), pool_scale, ln_mix_g, ln_mix_b, ln_ff_g, ln_ff_b,
        loss_target[0], send, start_after=ffn_started[4])

    state = dict(zip(names, zip(big_w, big_m, big_v)))
    n_small = sum(w.shape[-1] for w in small_w)
    small_2d = (n_small // HEAD_DIM, HEAD_DIM)
    state["small"] = tuple(jnp.concatenate(t, axis=-1).reshape(small_2d) for t in (small_w, small_m, small_v))
    grads, deltas, new_ms, new_vs = {}, {}, {}, {}
    after = grad_x
    for key in ("ff2", "ff1", "out", "branch", "in"):
        ex, started = sent[key]
        for n, part in zip(groups[key], ex.wait(started, after)):
            w, m, v = state[n]
            r2 = (-1, w.shape[-1])
            w2d = w.reshape(r2)
            res = _adamw("adamw_" + n, part.reshape((N_DEV,) + w2d.shape), w2d, m.reshape(r2), v.reshape(r2))
            after = res[0]
            if n == "small":
                small_out = [t.reshape(1, n_small) for t in res]
            else:
                grads[n], deltas[n], new_ms[n], new_vs[n] = (t.reshape((1,) + w.shape) for t in res)
    small_names = ("pool_scale", "ln_mix_g", "ln_mix_b", "ln_ff_g", "ln_ff_b")
    off = 0
    for n, w in zip(small_names, small_w):
        width = w.shape[-1]
        grads[n], deltas[n], new_ms[n], new_vs[n] = (t[:, off:off + width] for t in small_out)
        off += width

    order = ("w_in", "w_pool", "pool_scale", "w_branch_attn", "w_branch_pool", "w_out", "ln_mix_g", "ln_mix_b",
             "w_ff1", "w_ff2", "ln_ff_g", "ln_ff_b")
    total_loss = lax.psum(loss[0, 0], ("x", "y", "c"))
    return (total_loss, grad_x[None], *[grads[n] for n in order], *[deltas[n] for n in order],
            *[new_ms[n] for n in order], *[new_vs[n] for n in order])
```

```python
import functools

import jax
import jax.numpy as jnp
from jax import lax
from jax.experimental import pallas as pl
from jax.experimental.pallas import tpu as pltpu

F32 = jnp.float32
BF16 = jnp.bfloat16

N_DEV = 8
HEAD_DIM = 128
SUB_BLOCK = 128
DILATIONS = (1, 4, 16)
POOL_WINDOWS = (2, 4, 8, 16)
MAX_POOL_WINDOW = 16
POOL_HALO = 128
PERM_ROWS = 512
LN_EPS = 1e-5
DEEPNORM_ALPHA = 2.0 ** 0.25
ROPE_THETA = 10000.0
ATTN_SCALE = HEAD_DIM ** -0.5
ADAM_LR, ADAM_B1, ADAM_B2, ADAM_EPS, ADAM_WD, ADAM_STEP = 0.001, 0.9, 0.999, 1e-08, 0.01, 10
NEG_BIG = -1e30
VMEM_CAP_V7X = 64 * 1024 * 1024
MESH = pl.DeviceIdType.MESH


def _vmem_limit(est_bytes):
    return int(min(max(est_bytes * 5 // 4 + (4 << 20), 16 << 20), VMEM_CAP_V7X - (6 << 20)))


def _nbytes(shape, dtype):
    n = 1
    for s in shape:
        n *= s
    return n * jnp.dtype(dtype).itemsize


def _mm(name, a, b, form, tiles, outs, epi, extras=(), sequential=False, after=None):
    tm, tn, tk = tiles
    if form == "nn":
        (M, K), (K2, N) = a.shape, b.shape
    elif form == "nt":
        (M, K), (N, K2) = a.shape, b.shape
    else:
        (K, M), (K2, N) = a.shape, b.shape
    assert K == K2, (name, a.shape, b.shape)
    tm, tn, tk = min(tm, M), min(tn, N), min(tk, K)
    assert M % tm == 0 and N % tn == 0 and K % tk == 0, (name, M, N, K, tm, tn, tk)
    grid = (M // tm, N // tn, K // tk)
    nk = grid[2]
    if form == "nn":
        a_spec = pl.BlockSpec((tm, tk), lambda i, j, k: (i, k))
        b_spec = pl.BlockSpec((tk, tn), lambda i, j, k: (k, j))
        contract = ((1,), (0,))
    elif form == "nt":
        a_spec = pl.BlockSpec((tm, tk), lambda i, j, k: (i, k))
        b_spec = pl.BlockSpec((tn, tk), lambda i, j, k: (j, k))
        contract = ((1,), (1,))
    else:
        a_spec = pl.BlockSpec((tk, tm), lambda i, j, k: (k, i))
        b_spec = pl.BlockSpec((tk, tn), lambda i, j, k: (k, j))
        contract = ((0,), (0,))
    n_ex, n_out = len(extras), len(outs)
    n_after = 0 if after is None else 1

    def body(a_ref, b_ref, *rest):
        ex_refs = rest[:n_ex]
        rest = rest[n_ex + n_after:]
        out_refs = rest[:n_out]
        i, j, k = pl.program_id(0), pl.program_id(1), pl.program_id(2)

        def prod():
            return lax.dot_general(a_ref[...].astype(BF16), b_ref[...].astype(BF16),
                                   (contract, ((), ())), preferred_element_type=F32)

        if nk == 1:
            epi(prod(), ex_refs, out_refs, i, j)
        else:
            acc = rest[n_out]

            @pl.when(k == 0)
            def _():
                acc[...] = prod()

            @pl.when(k > 0)
            def _():
                acc[...] += prod()

            @pl.when(k == nk - 1)
            def _():
                epi(acc[...], ex_refs, out_refs, i, j)

    est = 2 * (_nbytes(a_spec.block_shape, a.dtype) + _nbytes(b_spec.block_shape, b.dtype))
    est += sum(2 * _nbytes(bs, arr.dtype) for arr, bs, _ in extras)
    est += sum(2 * _nbytes(bs, dt) for _, dt, bs, _ in outs)
    est += 4 * tm * tn * 4
    sem = ("arbitrary",) * 3 if sequential else ("parallel", "parallel", "arbitrary")
    return pl.pallas_call(
        body, name=name, grid=grid,
        in_specs=([a_spec, b_spec] + [pl.BlockSpec(bs, im) for _, bs, im in extras]
                  + [pl.BlockSpec(memory_space=pl.ANY)] * n_after),
        out_specs=[pl.BlockSpec(bs, im) for _, _, bs, im in outs],
        out_shape=[jax.ShapeDtypeStruct(sh, dt) for sh, dt, _, _ in outs],
        scratch_shapes=[pltpu.VMEM((tm, tn), F32)] if nk > 1 else [],
        compiler_params=pltpu.CompilerParams(dimension_semantics=sem, vmem_limit_bytes=_vmem_limit(est)),
    )(a, b, *[arr for arr, _, _ in extras], *([after] if n_after else []))


def _tile_out(shape, dtype, tm, tn):
    return (shape, dtype, (tm, tn), lambda i, j, k: (i, j))


def _row_sum_out(width):
    return ((1, width), F32, (1, width), lambda i, j, k: (0, 0))


def _accumulate_rows(ref, value, i):
    @pl.when(i == 0)
    def _():
        ref[...] = value

    @pl.when(i > 0)
    def _():
        ref[...] += value


def _layer_norm_bwd(dy, xhat, rstd, g):
    dxh = dy * g
    m1 = jnp.mean(dxh, axis=-1, keepdims=True)
    m2 = jnp.mean(dxh * xhat, axis=-1, keepdims=True)
    return rstd * (dxh - m1 - xhat * m2)


def _rope_apply(t, cos2, sin_signed):
    return t * cos2 + pltpu.roll(t, HEAD_DIM // 2, axis=1) * sin_signed


def _in_proj(xb, w_in, cos2, sin_fwd, n_rope_cols, after=None):
    S, D = xb.shape
    W = w_in.shape[1]
    tm, tn = min(1024, S), min(2048, n_rope_cols)
    assert n_rope_cols % tn == 0
    n_rope_tiles = n_rope_cols // tn

    def epi(acc, ex, out, i, j):
        cos_ref, sin_ref = ex
        (h_ref,) = out

        @pl.when(j < n_rope_tiles)
        def _():
            c, s = cos_ref[...], sin_ref[...]
            for hd in range(tn // HEAD_DIM):
                sl = slice(hd * HEAD_DIM, (hd + 1) * HEAD_DIM)
                h_ref[:, sl] = _rope_apply(acc[:, sl], c, s).astype(BF16)

        @pl.when(j >= n_rope_tiles)
        def _():
            h_ref[...] = acc.astype(BF16)

    row = lambda i, j, k: (i, 0)
    (h,) = _mm("in_proj", xb, w_in, "nn", (tm, tn, 512),
               [_tile_out((S, W), BF16, tm, tn)], epi,
               extras=[(cos2, (tm, HEAD_DIM), row), (sin_fwd, (tm, HEAD_DIM), row)], after=after)
    return h


def _attn_masks(mb):
    qi = lax.broadcasted_iota(jnp.int32, (SUB_BLOCK, SUB_BLOCK), 0)
    kj = lax.broadcasted_iota(jnp.int32, (SUB_BLOCK, SUB_BLOCK), 1)
    mask_cur = kj <= qi
    mask_prev = jnp.logical_and(kj >= qi, mb > 0)
    return mask_cur, mask_prev


def _dot_nt(a, b):
    return lax.dot_general(a, b, (((1,), (1,)), ((), ())), preferred_element_type=F32)


def _dot_tn(a, b):
    return lax.dot_general(a, b, (((0,), (0,)), ((), ())), preferred_element_type=F32)


def _dot_nn(a, b):
    return lax.dot_general(a, b, (((1,), (0,)), ((), ())), preferred_element_type=F32)


def _perm_matrix(d, to_residue_major):
    g = PERM_ROWS // d
    i = lax.broadcasted_iota(jnp.int32, (PERM_ROWS, PERM_ROWS), 0)
    j = lax.broadcasted_iota(jnp.int32, (PERM_ROWS, PERM_ROWS), 1)
    if to_residue_major:
        hit = j == (i % g) * d + i // g
    else:
        hit = j == (i % d) * g + i // d
    return hit.astype(BF16)


def _permute_rows(perm, x):
    if x.dtype == BF16:
        return _dot_nn(perm, x)
    hi = x.astype(BF16)
    r1 = x - hi.astype(F32)
    mid = r1.astype(BF16)
    lo = (r1 - mid.astype(F32)).astype(BF16)
    return (_dot_nn(perm, hi) + _dot_nn(perm, mid)) + _dot_nn(perm, lo)


def _rm_block(d, width):
    return pl.BlockSpec((d, PERM_ROWS // d, width), lambda i: (0, i, 0))


def _to_residue_major(name, x, col_block, width):
    S = x.shape[0]
    dils = [d for d in DILATIONS if d > 1]
    chunk = min(width, 1024)

    def body(x_ref, *out_refs):
        for d, o_ref in zip(dils, out_refs):
            perm = _perm_matrix(d, True)
            for c0 in range(0, width, chunk):
                cw = min(chunk, width - c0)
                y = _permute_rows(perm, x_ref[:, c0:c0 + cw])
                o_ref[:, :, c0:c0 + cw] = y.astype(x.dtype).reshape(d, PERM_ROWS // d, cw)

    return pl.pallas_call(
        body, name=name, grid=(S // PERM_ROWS,),
        in_specs=[pl.BlockSpec((PERM_ROWS, width), lambda i: (i, col_block))],
        out_specs=[_rm_block(d, width) for d in dils],
        out_shape=[jax.ShapeDtypeStruct((d, S // d, width), x.dtype) for d in dils],
        compiler_params=pltpu.CompilerParams(dimension_semantics=("parallel",),
                                             vmem_limit_bytes=_vmem_limit(32 << 20)),
    )(x)


def _qkv_specs(aw):
    def spec(col, prev):
        if prev:
            return pl.BlockSpec((None, SUB_BLOCK, aw), lambda r, mb: (r, jnp.maximum(mb - 1, 0), col))
        return pl.BlockSpec((None, SUB_BLOCK, aw), lambda r, mb: (r, mb, col))
    return [spec(0, False), spec(1, True), spec(1, False), spec(2, True), spec(2, False)]


def _put_column(tile, col, value):
    lane = lax.broadcasted_iota(jnp.int32, tile.shape, 1)
    return jnp.where(lane == col, value, tile)


def _attn_fwd(qkv, d, aw):
    _, rows, _ = qkv.shape
    n_heads = aw // HEAD_DIM
    nb = rows // SUB_BLOCK

    def body(q_ref, kp_ref, kc_ref, vp_ref, vc_ref, o_ref, lse_ref):
        mask_cur, mask_prev = _attn_masks(pl.program_id(1))
        lse_tile = jnp.zeros((SUB_BLOCK, HEAD_DIM), F32)
        for hd in range(n_heads):
            sl = slice(hd * HEAD_DIM, (hd + 1) * HEAD_DIM)
            q = q_ref[:, sl]
            sc = jnp.where(mask_cur, _dot_nt(q, kc_ref[:, sl]) * ATTN_SCALE, NEG_BIG)
            sp = jnp.where(mask_prev, _dot_nt(q, kp_ref[:, sl]) * ATTN_SCALE, NEG_BIG)
            m = jnp.maximum(jnp.max(sc, axis=-1, keepdims=True), jnp.max(sp, axis=-1, keepdims=True))
            pc = jnp.exp(sc - m)
            pp = jnp.exp(sp - m)
            l = jnp.sum(pc, axis=-1, keepdims=True) + jnp.sum(pp, axis=-1, keepdims=True)
            o = _dot_nn(pc.astype(BF16), vc_ref[:, sl]) + _dot_nn(pp.astype(BF16), vp_ref[:, sl])
            o_ref[:, sl] = o / l
            lse_tile = _put_column(lse_tile, hd, m + jnp.log(l))
        lse_ref[...] = lse_tile

    return pl.pallas_call(
        body, name=f"attn_fwd_d{d}", grid=(d, nb),
        in_specs=_qkv_specs(aw),
        out_specs=[pl.BlockSpec((None, SUB_BLOCK, aw), lambda r, mb: (r, mb, 0)),
                   pl.BlockSpec((None, SUB_BLOCK, HEAD_DIM), lambda r, mb: (r, mb, 0))],
        out_shape=[jax.ShapeDtypeStruct((d, rows, aw), F32), jax.ShapeDtypeStruct((d, rows, HEAD_DIM), F32)],
        compiler_params=pltpu.CompilerParams(dimension_semantics=("parallel", "parallel"),
                                             vmem_limit_bytes=_vmem_limit(16 << 20)),
    )(qkv, qkv, qkv, qkv, qkv)


def _attn_combine(outs, lses, aw):
    S = outs[0].shape[1]
    n_heads = aw // HEAD_DIM
    n_pat = len(DILATIONS)

    def body(*refs):
        o_refs, l_refs = refs[:n_pat], refs[n_pat:2 * n_pat]
        o_ref, lt_ref = refs[2 * n_pat], refs[2 * n_pat + 1]
        o_nat, l_nat = [], []
        for d, o_r, l_r in zip(DILATIONS, o_refs, l_refs):
            o_p = o_r[...].reshape(PERM_ROWS, aw)
            l_p = l_r[...].reshape(PERM_ROWS, HEAD_DIM)
            if d > 1:
                perm = _perm_matrix(d, False)
                o_p, l_p = _permute_rows(perm, o_p), _permute_rows(perm, l_p)
            o_nat.append(o_p)
            l_nat.append(l_p)
        mx = functools.reduce(jnp.maximum, l_nat)
        es = [jnp.exp(l_p - mx) for l_p in l_nat]
        den = functools.reduce(jnp.add, es)
        lt_ref[...] = mx + jnp.log(den)
        ws = [e / den for e in es]
        for hd in range(n_heads):
            sl = slice(hd * HEAD_DIM, (hd + 1) * HEAD_DIM)
            o = ws[0][:, hd:hd + 1] * o_nat[0][:, sl]
            for pi in range(1, n_pat):
                o = o + ws[pi][:, hd:hd + 1] * o_nat[pi][:, sl]
            o_ref[:, sl] = o.astype(BF16)

    return pl.pallas_call(
        body, name="attn_combine", grid=(S // PERM_ROWS,),
        in_specs=[_rm_block(d, aw) for d in DILATIONS] + [_rm_block(d, HEAD_DIM) for d in DILATIONS],
        out_specs=[pl.BlockSpec((PERM_ROWS, aw), lambda i: (i, 0)), pl.BlockSpec((PERM_ROWS, HEAD_DIM), lambda i: (i, 0))],
        out_shape=[jax.ShapeDtypeStruct((S, aw), BF16), jax.ShapeDtypeStruct((S, HEAD_DIM), F32)],
        compiler_params=pltpu.CompilerParams(dimension_semantics=("parallel",),
                                             vmem_limit_bytes=_vmem_limit(40 << 20)),
    )(*outs, *lses)


def _band(tm, width, w, row_offset, transpose):
    t = lax.broadcasted_iota(jnp.int32, (tm, width), 0)
    u = lax.broadcasted_iota(jnp.int32, (tm, width), 1)
    dist = (u - t - row_offset) if transpose else (t + row_offset - u)
    return jnp.logical_and(dist >= 0, dist < w).astype(BF16)


def _pool_fwd(h, w_pool, pool_scale, pw, u_col_block):
    S, W = h.shape
    n_groups = len(POOL_WINDOWS)
    gw = pw // n_groups
    tm = min(512, S)
    halo_per_tile = tm // POOL_HALO

    def body(uc_ref, uh_ref, w_ref, sc_ref, p_ref, y_ref, pm_ref):
        i = pl.program_id(0)
        t_abs = i * tm + lax.broadcasted_iota(jnp.int32, (tm, 1), 0)
        for g, w in enumerate(POOL_WINDOWS):
            sl = slice(g * gw, (g + 1) * gw)
            uc = uc_ref[:, sl]
            uh = jnp.where(i > 0, uh_ref[:, sl], jnp.zeros((POOL_HALO, gw), BF16))
            ssum = _dot_nn(_band(tm, tm, w, 0, False), uc) + _dot_nn(_band(tm, POOL_HALO, w, POOL_HALO, False), uh)
            cnt = jnp.minimum(t_abs + 1, w).astype(F32)
            p = (ssum / cnt - uc.astype(F32)).astype(BF16)
            y = _dot_nn(p, w_ref[g])
            p_ref[:, sl] = p
            y_ref[:, sl] = y.astype(BF16)
            pm_ref[:, sl] = (y * sc_ref[:, sl]).astype(BF16)

    row = pl.BlockSpec((tm, pw), lambda i: (i, 0))
    return pl.pallas_call(
        body, name="pool_fwd", grid=(S // tm,),
        in_specs=[pl.BlockSpec((tm, pw), lambda i: (i, u_col_block)),
                  pl.BlockSpec((POOL_HALO, pw), lambda i: (jnp.maximum(i * halo_per_tile - 1, 0), u_col_block)),
                  pl.BlockSpec((n_groups, gw, gw), lambda i: (0, 0, 0)),
                  pl.BlockSpec((1, pw), lambda i: (0, 0))],
        out_specs=[row, row, row],
        out_shape=[jax.ShapeDtypeStruct((S, pw), BF16)] * 3,
        compiler_params=pltpu.CompilerParams(dimension_semantics=("parallel",),
                                             vmem_limit_bytes=_vmem_limit(24 << 20)),
    )(h, h, w_pool, pool_scale)


def _branch_attn(o_attn, w_ba):
    S, _ = o_attn.shape
    D = w_ba.shape[1]
    tm, tn = min(1024, S), min(1024, D)

    def epi(acc, ex, out, i, j):
        out[0][...] = acc.astype(BF16)

    (y,) = _mm("branch_attn", o_attn, w_ba, "nn", (tm, tn, 1024), [_tile_out((S, D), BF16, tm, tn)], epi)
    return y


def _branch_pool_merge(pm, w_bp, h, y_attn, gate_col0):
    S, _ = pm.shape
    D = w_bp.shape[1]
    tm, tn = min(1024, S), min(1024, D)
    ga0, gp0 = gate_col0 // tn, (gate_col0 + D) // tn

    def epi(acc, ex, out, i, j):
        ga_ref, gp_ref, ya_ref = ex
        yp_ref, mg_ref = out
        yp = acc.astype(BF16)
        yp_ref[...] = yp
        mg = (jax.nn.sigmoid(ga_ref[...].astype(F32)) * ya_ref[...].astype(F32)
              + jax.nn.sigmoid(gp_ref[...].astype(F32)) * acc)
        mg_ref[...] = mg.astype(BF16)

    y_pool, merged = _mm(
        "branch_pool_merge", pm, w_bp, "nn", (tm, tn, 1024),
        [_tile_out((S, D), BF16, tm, tn), _tile_out((S, D), BF16, tm, tn)], epi,
        extras=[(h, (tm, tn), lambda i, j, k: (i, ga0 + j)), (h, (tm, tn), lambda i, j, k: (i, gp0 + j)),
                (y_attn, (tm, tn), lambda i, j, k: (i, j))])
    return y_pool, merged


def _layer_norm_rows(z, g, b):
    mu = jnp.mean(z, axis=-1, keepdims=True)
    zc = z - mu
    var = jnp.mean(zc * zc, axis=-1, keepdims=True)
    rstd = lax.rsqrt(var + LN_EPS)
    xhat = zc * rstd
    return xhat * g + b, xhat, rstd


def _out_proj_ln(merged, w_out, x, g, b):
    S, D = x.shape
    tm = min(512, S)

    def epi(acc, ex, out, i, j):
        x_ref, g_ref, b_ref = ex
        x1_ref, x1b_ref, xh_ref, rs_ref = out
        y, xhat, rstd = _layer_norm_rows(DEEPNORM_ALPHA * x_ref[...] + acc, g_ref[...], b_ref[...])
        x1_ref[...] = y
        x1b_ref[...] = y.astype(BF16)
        xh_ref[...] = xhat
        rs_ref[...] = jnp.broadcast_to(rstd, (tm, HEAD_DIM))

    row = lambda i, j, k: (i, 0)
    vec = lambda i, j, k: (0, 0)
    return _mm("out_proj_ln", merged, w_out, "nn", (tm, D, 512),
               [((S, D), F32, (tm, D), row), ((S, D), BF16, (tm, D), row), ((S, D), F32, (tm, D), row),
                ((S, HEAD_DIM), F32, (tm, HEAD_DIM), row)], epi,
               extras=[(x, (tm, D), row), (g, (1, D), vec), (b, (1, D), vec)])


def _ffn_up(x1b, w1):
    S, D = x1b.shape
    F = w1.shape[1]
    tm, tn = min(1024, S), min(2048, F)

    def epi(acc, ex, out, i, j):
        r = jnp.maximum(acc, 0.0)
        out[0][...] = (r * r).astype(BF16)

    (a,) = _mm("ffn_up", x1b, w1, "nn", (tm, tn, 512), [_tile_out((S, F), BF16, tm, tn)], epi)
    return a


def _ffn_down_loss(a, w2, x1, g, b, target):
    S, D = x1.shape
    tm = min(512, S)

    def epi(acc, ex, out, i, j):
        x1_ref, g_ref, b_ref, t_ref = ex
        dz_ref, dzb_ref, dg_ref, db_ref, loss_ref = out
        gv = g_ref[...]
        y, xhat, rstd = _layer_norm_rows(DEEPNORM_ALPHA * x1_ref[...] + acc, gv, b_ref[...])
        err = y - t_ref[...]
        loss = 0.5 * jnp.sum(jnp.mean(err * err, axis=-1, keepdims=True), axis=0, keepdims=True)
        dy = err * (1.0 / D)
        dz = _layer_norm_bwd(dy, xhat, rstd, gv)
        dz_ref[...] = dz
        dzb_ref[...] = dz.astype(BF16)
        _accumulate_rows(dg_ref, jnp.sum(dy * xhat, axis=0, keepdims=True), i)
        _accumulate_rows(db_ref, jnp.sum(dy, axis=0, keepdims=True), i)
        _accumulate_rows(loss_ref, jnp.broadcast_to(loss, (1, HEAD_DIM)), i)

    row = lambda i, j, k: (i, 0)
    vec = lambda i, j, k: (0, 0)
    return _mm("ffn_down_loss", a, w2, "nn", (tm, D, 512),
               [((S, D), F32, (tm, D), row), ((S, D), BF16, (tm, D), row),
                _row_sum_out(D), _row_sum_out(D), _row_sum_out(HEAD_DIM)], epi,
               extras=[(x1, (tm, D), row), (g, (1, D), vec), (b, (1, D), vec), (target, (tm, D), row)],
               sequential=True)


def _grad_weight(name, act, cot):
    M, N = act.shape[1], cot.shape[1]
    tm, tn = min(1024, M), min(2048, N)

    def epi(acc, ex, out, i, j):
        out[0][...] = acc.astype(BF16)

    (g,) = _mm(name, act, cot, "tn", (tm, tn, 512), [_tile_out((M, N), BF16, tm, tn)], epi)
    return g


def _ffn_down_bwd(dz2b, w2, a, after=None):
    S, D = dz2b.shape
    F = w2.shape[0]
    tm, tn = min(1024, S), min(2048, F)

    def epi(acc, ex, out, i, j):
        out[0][...] = (acc * (2.0 * jnp.sqrt(ex[0][...].astype(F32)))).astype(BF16)

    (dh1,) = _mm("ffn_down_bwd", dz2b, w2, "nt", (tm, tn, 512), [_tile_out((S, F), BF16, tm, tn)], epi,
                 extras=[(a, (tm, tn), lambda i, j, k: (i, j))], after=after)
    return dh1


def _ffn_up_bwd_ln(dh1, w1, dz2, xhat1, rstd1, g1, after=None):
    S, D = dz2.shape
    tm = min(512, S)

    def epi(acc, ex, out, i, j):
        dz2_ref, xh_ref, rs_ref, g_ref = ex
        dz_ref, dzb_ref, dg_ref, db_ref = out
        dy = DEEPNORM_ALPHA * dz2_ref[...] + acc
        xhat = xh_ref[...]
        dz = _layer_norm_bwd(dy, xhat, rs_ref[:, :1], g_ref[...])
        dz_ref[...] = dz
        dzb_ref[...] = dz.astype(BF16)
        _accumulate_rows(dg_ref, jnp.sum(dy * xhat, axis=0, keepdims=True), i)
        _accumulate_rows(db_ref, jnp.sum(dy, axis=0, keepdims=True), i)

    row = lambda i, j, k: (i, 0)
    vec = lambda i, j, k: (0, 0)
    return _mm("ffn_up_bwd_ln", dh1, w1, "nt", (tm, D, 512),
               [((S, D), F32, (tm, D), row), ((S, D), BF16, (tm, D), row), _row_sum_out(D), _row_sum_out(D)], epi,
               extras=[(dz2, (tm, D), row), (xhat1, (tm, D), row), (rstd1, (tm, HEAD_DIM), row), (g1, (1, D), vec)],
               sequential=True, after=after)


def _out_proj_bwd(dz1b, w_out, h, y_attn, y_pool, gate_col0, after=None):
    S, D = dz1b.shape
    W = h.shape[1]
    tm = min(256, S)
    assert gate_col0 == 2 * D and W == 4 * D

    def epi(acc, ex, out, i, j):
        gates_ref, ya_ref, yp_ref = ex
        dya_ref, dyp_ref, dh_ref = out
        sa = jax.nn.sigmoid(gates_ref[:, :D].astype(F32))
        sp = jax.nn.sigmoid(gates_ref[:, D:].astype(F32))
        dya_ref[...] = (acc * sa).astype(BF16)
        dyp_ref[...] = (acc * sp).astype(BF16)
        dh_ref[:, :D] = (acc * ya_ref[...].astype(F32) * (sa * (1.0 - sa))).astype(BF16)
        dh_ref[:, D:] = (acc * yp_ref[...].astype(F32) * (sp * (1.0 - sp))).astype(BF16)

    row = lambda i, j, k: (i, 0)
    return _mm("out_proj_bwd", dz1b, w_out, "nt", (tm, D, 512),
               [((S, D), BF16, (tm, D), row), ((S, D), BF16, (tm, D), row),
                ((S, W), BF16, (tm, 2 * D), lambda i, j, k: (i, 1))], epi,
               extras=[(h, (tm, 2 * D), lambda i, j, k: (i, 1)), (y_attn, (tm, D), row), (y_pool, (tm, D), row)],
               after=after)


def _branch_attn_bwd(dy_attn, w_ba, o_attn, l_tot, after=None):
    S, D = dy_attn.shape
    aw = w_ba.shape[0]
    n_heads = aw // HEAD_DIM
    tm = min(512, S)

    def epi(acc, ex, out, i, j):
        do_ref, st_ref = out
        do_ref[...] = acc.astype(BF16)
        o = ex[0][...].astype(F32)
        stats = ex[1][...]
        for hd in range(n_heads):
            sl = slice(hd * HEAD_DIM, (hd + 1) * HEAD_DIM)
            stats = _put_column(stats, n_heads + hd, jnp.sum(acc[:, sl] * o[:, sl], axis=-1, keepdims=True))
        st_ref[...] = stats

    row = lambda i, j, k: (i, 0)
    return _mm("branch_attn_bwd", dy_attn, w_ba, "nt", (tm, aw, 512),
               [((S, aw), BF16, (tm, aw), row), ((S, HEAD_DIM), F32, (tm, HEAD_DIM), row)], epi,
               extras=[(o_attn, (tm, aw), row), (l_tot, (tm, HEAD_DIM), row)], after=after)


def _branch_pool_bwd(dy_pool, w_bp, y_pre, pool_scale):
    S, D = dy_pool.shape
    pw = w_bp.shape[0]
    tm = min(512, S)

    def epi(acc, ex, out, i, j):
        y_ref, sc_ref = ex
        dyp_ref, dsc_ref = out
        dyp_ref[...] = (acc * sc_ref[...]).astype(BF16)
        _accumulate_rows(dsc_ref, jnp.sum(acc * y_ref[...].astype(F32), axis=0, keepdims=True), i)

    row = lambda i, j, k: (i, 0)
    return _mm("branch_pool_bwd", dy_pool, w_bp, "nt", (tm, pw, 512),
               [((S, pw), BF16, (tm, pw), row), _row_sum_out(pw)], epi,
               extras=[(y_pre, (tm, pw), row), (pool_scale, (1, pw), lambda i, j, k: (0, 0))],
               sequential=True)


def _pool_bwd(dh, dy_pre, p, w_pool, pw, u_col_block):
    S, W = dh.shape
    n_groups = len(POOL_WINDOWS)
    gw = pw // n_groups
    tm = min(512, S)
    n_tiles = S // tm
    halo_per_tile = tm // POOL_HALO
    n_halo_blocks = S // POOL_HALO

    def body(dh_in_ref, dyc_ref, dyh_ref, p_ref, w_ref, dh_ref, dw_ref):
        del dh_in_ref
        i = pl.program_id(0)
        t_cur = i * tm + lax.broadcasted_iota(jnp.int32, (tm, 1), 0)
        t_halo = (i + 1) * tm + lax.broadcasted_iota(jnp.int32, (POOL_HALO, 1), 0)
        for g, w in enumerate(POOL_WINDOWS):
            sl = slice(g * gw, (g + 1) * gw)
            wg = w_ref[g]
            dyc = dyc_ref[:, sl]
            dyh = jnp.where(i < n_tiles - 1, dyh_ref[:, sl], jnp.zeros((POOL_HALO, gw), BF16))
            dp_cur = _dot_nt(dyc, wg)
            dp_halo = _dot_nt(dyh, wg)
            dpc_cur = (dp_cur / jnp.minimum(t_cur + 1, w).astype(F32)).astype(BF16)
            dpc_halo = (dp_halo / jnp.minimum(t_halo + 1, w).astype(F32)).astype(BF16)
            du = (_dot_nn(_band(tm, tm, w, 0, True), dpc_cur)
                  + _dot_nn(_band(tm, POOL_HALO, w, -tm, True), dpc_halo) - dp_cur)
            dh_ref[:, sl] = du.astype(BF16)
            dw = _dot_tn(p_ref[:, sl], dyc)

            @pl.when(i == 0)
            def _():
                dw_ref[g] = dw

            @pl.when(i > 0)
            def _():
                dw_ref[g] += dw

    row = pl.BlockSpec((tm, pw), lambda i: (i, 0))
    dh_new, dw_pool = pl.pallas_call(
        body, name="pool_bwd", grid=(n_tiles,),
        in_specs=[pl.BlockSpec(memory_space=pl.ANY), row,
                  pl.BlockSpec((POOL_HALO, pw), lambda i: (jnp.minimum((i + 1) * halo_per_tile, n_halo_blocks - 1), 0)),
                  row, pl.BlockSpec((n_groups, gw, gw), lambda i: (0, 0, 0))],
        out_specs=[pl.BlockSpec((tm, pw), lambda i: (i, u_col_block)),
                   pl.BlockSpec((n_groups, gw, gw), lambda i: (0, 0, 0))],
        out_shape=[jax.ShapeDtypeStruct((S, W), BF16), jax.ShapeDtypeStruct((n_groups, gw, gw), F32)],
        input_output_aliases={0: 0},
        compiler_params=pltpu.CompilerParams(dimension_semantics=("arbitrary",),
                                             vmem_limit_bytes=_vmem_limit(24 << 20)),
    )(dh, dy_pre, dy_pre, p, w_pool)
    return dh_new, dw_pool


def _attn_bwd(qkv, d_out, stats, d, aw):
    _, rows, _ = qkv.shape
    n_heads = aw // HEAD_DIM
    nb = rows // SUB_BLOCK

    def body(q_ref, kp_ref, kc_ref, vp_ref, vc_ref, do_ref, st_ref, dq_ref, dk_ref, dv_ref, carry_k, carry_v):
        mb = pl.program_id(1)

        @pl.when(mb == 0)
        def _():
            carry_k[...] = jnp.zeros_like(carry_k)
            carry_v[...] = jnp.zeros_like(carry_v)

        @pl.when(mb < nb)
        def _():
            mask_cur, mask_prev = _attn_masks(mb)
            st = st_ref[...]
            for hd in range(n_heads):
                sl = slice(hd * HEAD_DIM, (hd + 1) * HEAD_DIM)
                q, kc, kp, vc, vp, do = q_ref[:, sl], kc_ref[:, sl], kp_ref[:, sl], vc_ref[:, sl], vp_ref[:, sl], do_ref[:, sl]
                lt, dl = st[:, hd:hd + 1], st[:, n_heads + hd:n_heads + hd + 1]
                pc = jnp.where(mask_cur, jnp.exp(jnp.where(mask_cur, _dot_nt(q, kc) * ATTN_SCALE - lt, NEG_BIG)), 0.0)
                pp = jnp.where(mask_prev, jnp.exp(jnp.where(mask_prev, _dot_nt(q, kp) * ATTN_SCALE - lt, NEG_BIG)), 0.0)
                dsc = (pc * (_dot_nt(do, vc) - dl) * ATTN_SCALE).astype(BF16)
                dsp = (pp * (_dot_nt(do, vp) - dl) * ATTN_SCALE).astype(BF16)
                dq_ref[:, sl] = (_dot_nn(dsc, kc) + _dot_nn(dsp, kp)).astype(BF16)
                dk_ref[:, sl] = (carry_k[:, sl] + _dot_tn(dsp, q)).astype(BF16)
                dv_ref[:, sl] = (carry_v[:, sl] + _dot_tn(pp.astype(BF16), do)).astype(BF16)
                carry_k[:, sl] = _dot_tn(dsc, q)
                carry_v[:, sl] = _dot_tn(pc.astype(BF16), do)

        @pl.when(mb == nb)
        def _():
            dk_ref[...] = carry_k[...].astype(BF16)
            dv_ref[...] = carry_v[...].astype(BF16)

    def cur(r, mb):
        return jnp.minimum(mb, nb - 1)

    def qkv_spec(col, prev):
        if prev:
            return pl.BlockSpec((None, SUB_BLOCK, aw), lambda r, mb: (r, jnp.maximum(cur(r, mb) - 1, 0), col))
        return pl.BlockSpec((None, SUB_BLOCK, aw), lambda r, mb: (r, cur(r, mb), col))

    def at_cur(width):
        return pl.BlockSpec((None, SUB_BLOCK, width), lambda r, mb: (r, cur(r, mb), 0))

    finished = pl.BlockSpec((None, SUB_BLOCK, aw), lambda r, mb: (r, jnp.maximum(mb - 1, 0), 0))
    return pl.pallas_call(
        body, name=f"attn_bwd_d{d}", grid=(d, nb + 1),
        in_specs=[qkv_spec(0, False), qkv_spec(1, True), qkv_spec(1, False), qkv_spec(2, True), qkv_spec(2, False),
                  at_cur(aw), at_cur(HEAD_DIM)],
        out_specs=[at_cur(aw), finished, finished],
        out_shape=[jax.ShapeDtypeStruct((d, rows, aw), BF16)] * 3,
        scratch_shapes=[pltpu.VMEM((SUB_BLOCK, aw), F32), pltpu.VMEM((SUB_BLOCK, aw), F32)],
        compiler_params=pltpu.CompilerParams(dimension_semantics=("arbitrary", "arbitrary"),
                                             vmem_limit_bytes=_vmem_limit(24 << 20)),
    )(qkv, qkv, qkv, qkv, qkv, d_out, stats)


def _attn_bwd_finish(dh, per_pattern, cos2, sin_bwd, aw):
    S, W = dh.shape
    n_heads = aw // HEAD_DIM
    n_pat = len(DILATIONS)

    def body(*refs):
        grad_refs = refs[1:1 + 3 * n_pat]
        cos_ref, sin_ref = refs[1 + 3 * n_pat], refs[2 + 3 * n_pat]
        out_ref = refs[3 + 3 * n_pat]
        perms = {d: _perm_matrix(d, False) for d in DILATIONS if d > 1}
        totals = []
        for which in range(3):
            tot = None
            for pi, d in enumerate(DILATIONS):
                g = grad_refs[which * n_pat + pi][...].reshape(PERM_ROWS, aw)
                g = _permute_rows(perms[d], g) if d > 1 else g.astype(F32)
                tot = g if tot is None else tot + g
            totals.append(tot)
        dq, dk, dv = totals
        c, s = cos_ref[...], sin_ref[...]
        for hd in range(n_heads):
            sl = slice(hd * HEAD_DIM, (hd + 1) * HEAD_DIM)
            out_ref[:, sl] = _rope_apply(dq[:, sl], c, s).astype(BF16)
            out_ref[:, aw + hd * HEAD_DIM:aw + (hd + 1) * HEAD_DIM] = _rope_apply(dk[:, sl], c, s).astype(BF16)
        out_ref[:, 2 * aw:] = dv.astype(BF16)

    grads = [pp[which] for which in range(3) for pp in per_pattern]
    rope_spec = pl.BlockSpec((PERM_ROWS, HEAD_DIM), lambda i: (i, 0))
    return pl.pallas_call(
        body, name="attn_bwd_finish", grid=(S // PERM_ROWS,),
        in_specs=([pl.BlockSpec(memory_space=pl.ANY)] + [_rm_block(d, aw) for d in DILATIONS] * 3
                  + [rope_spec, rope_spec]),
        out_specs=pl.BlockSpec((PERM_ROWS, 3 * aw), lambda i: (i, 0)),
        out_shape=jax.ShapeDtypeStruct((S, W), BF16),
        input_output_aliases={0: 0},
        compiler_params=pltpu.CompilerParams(dimension_semantics=("parallel",),
                                             vmem_limit_bytes=_vmem_limit(32 << 20)),
    )(dh, *grads, cos2, sin_bwd)


def _in_proj_bwd(dh, w_in, dz1, after=None):
    S, D = dz1.shape
    tm, tn = min(1024, S), min(2048, D)

    def epi(acc, ex, out, i, j):
        out[0][...] = DEEPNORM_ALPHA * ex[0][...] + acc

    (gx,) = _mm("in_proj_bwd", dh, w_in, "nt", (tm, tn, 512), [_tile_out((S, D), F32, tm, tn)], epi,
                extras=[(dz1, (tm, tn), lambda i, j, k: (i, j))], after=after)
    return gx


def _my_place():
    x, y, c = lax.axis_index("x"), lax.axis_index("y"), lax.axis_index("c")
    return x, y, c


def _flat(px, py, pc):
    return 4 * px + 2 * py + pc


def _shard_slice(ref, axis, idx, size):
    start = pl.multiple_of(idx * size, size)
    ix = [slice(None)] * len(ref.shape)
    ix[axis] = pl.ds(start, size)
    return ref.at[tuple(ix)]


def _all_gather_weights(shards, axes):
    n_w = len(shards)
    full_shapes = []
    for s, ax in zip(shards, axes):
        sh = list(s.shape)
        sh[ax] *= N_DEV
        full_shapes.append(tuple(sh))

    def body(*refs):
        src = refs[:n_w]
        dst = refs[n_w:2 * n_w]
        send_sems, recv_sems, local_sems = refs[2 * n_w:]
        x, y, c = _my_place()
        me, sibling = (x, y, c), (x, y, 1 - c)
        chips = [(1 - x, y), (x, 1 - y), (1 - x, 1 - y)]

        def place(w, dev):
            return _shard_slice(dst[w], axes[w], _flat(*dev), src[w].shape[axes[w]])

        def copy(w, k, block, to, from_src=False):
            return pltpu.make_async_remote_copy(
                src_ref=src[w] if from_src else place(w, block), dst_ref=place(w, block),
                send_sem=send_sems.at[w * 7 + k], recv_sem=recv_sems.at[w * 7 + k], device_id=to, device_id_type=MESH)

        mine, first, passed = [], [], []
        for w in range(n_w):
            cp = pltpu.make_async_copy(src[w], place(w, me), local_sems.at[w])
            cp.start()
            mine.append(cp)
            fw = [copy(w, 0, me, sibling, True)] + [copy(w, 1 + j, me, (*chip, c), True) for j, chip in enumerate(chips)]
            for cp in fw:
                cp.start()
            first += fw
        for w in range(n_w):
            for j, chip in enumerate(chips):
                copy(w, 1 + j, (*chip, c), me).wait_recv()
                cp = copy(w, 4 + j, (*chip, c), sibling)
                cp.start()
                passed.append(cp)
        for w in range(n_w):
            copy(w, 0, sibling, me).wait_recv()
            for j, chip in enumerate(chips):
                copy(w, 4 + j, (*chip, 1 - c), me).wait_recv()
        for cp in first + passed:
            cp.wait_send()
        for cp in mine:
            cp.wait()

    any_spec = pl.BlockSpec(memory_space=pl.ANY)
    return pl.pallas_call(
        body, name="all_gather_weights",
        in_specs=[any_spec] * n_w, out_specs=[any_spec] * n_w,
        out_shape=[jax.ShapeDtypeStruct(sh, s.dtype) for sh, s in zip(full_shapes, shards)],
        scratch_shapes=[pltpu.SemaphoreType.DMA((n_w * 7,)), pltpu.SemaphoreType.DMA((n_w * 7,)),
                        pltpu.SemaphoreType.DMA((n_w,))],
    )(*shards)


_HBM_SPEC = pl.BlockSpec(memory_space=pltpu.HBM)
_SEM_SPEC = pl.BlockSpec(memory_space=pltpu.SEMAPHORE)
_ANY_SPEC = pl.BlockSpec(memory_space=pl.ANY)
_N_PEER = N_DEV - 1


def _peer_of(x, y, c, r):
    return (x ^ ((r >> 2) & 1), y ^ ((r >> 1) & 1), c ^ (r & 1))


class _Exchange:
    def __init__(self, name, part, slot):
        self.name, self.part, self.slot = name, part, slot

    def _copy(self, w, r, src, land, send_sems, recv_sems, sending):
        x, y, c = _my_place()
        peer = _peer_of(x, y, c, r)
        return pltpu.make_async_remote_copy(
            src_ref=self.part(w, src, _flat(*peer)),
            dst_ref=self.slot(w, land, _flat(x, y, c) if sending else _flat(*peer)),
            send_sem=send_sems.at[w * _N_PEER + r - 1], recv_sem=recv_sems.at[w * _N_PEER + r - 1],
            device_id=peer, device_id_type=MESH)

    def start(self, srcs, lands, after):
        n = len(srcs)

        def body(*refs):
            src, land = refs[:n], refs[n:2 * n]
            send_sems, recv_sems = refs[2 * n + 1], refs[2 * n + 2]
            token = refs[2 * n + 3 + 2 * n]
            for w in range(n):
                for r in range(1, N_DEV):
                    self._copy(w, r, src[w], land[w], send_sems, recv_sems, True).start()
            token[...] = jnp.zeros_like(token)

        sems = pltpu.SemaphoreType.DMA((n * _N_PEER,))
        outs = pl.pallas_call(
            body, name=self.name + "_start",
            out_shape=(sems, sems, *[pltpu.HBM(t.shape, t.dtype) for t in list(srcs) + list(lands)],
                       jax.ShapeDtypeStruct((8, 128), F32)),
            in_specs=[_HBM_SPEC] * (2 * n) + [_ANY_SPEC],
            out_specs=(_SEM_SPEC, _SEM_SPEC, *[_HBM_SPEC] * (2 * n), pl.BlockSpec(memory_space=pltpu.VMEM)),
            input_output_aliases={i: 2 + i for i in range(2 * n)},
            compiler_params=pltpu.CompilerParams(has_side_effects=pltpu.SideEffectType.DATAFLOW_SIDE_EFFECTING),
        )(*[pltpu.with_memory_space_constraint(t, pltpu.HBM) for t in list(srcs) + list(lands)], after)
        return outs[0], outs[1], outs[2:2 + n], outs[2 + n:2 + 2 * n], outs[2 + 2 * n]

    def wait(self, started, after):
        send_sems, recv_sems, srcs, lands, _ = started
        n = len(srcs)

        def body(*refs):
            src, land = refs[:n], refs[n:2 * n]
            s_sems, r_sems = refs[2 * n], refs[2 * n + 1]
            for w in range(n):
                for r in range(1, N_DEV):
                    cp = self._copy(w, r, src[w], land[w], s_sems, r_sems, False)
                    cp.wait_send()
                    cp.wait_recv()

        outs = pl.pallas_call(
            body, name=self.name + "_wait",
            out_shape=[pltpu.HBM(t.shape, t.dtype) for t in list(srcs) + list(lands)],
            in_specs=[_HBM_SPEC] * (2 * n) + [_SEM_SPEC, _SEM_SPEC, _ANY_SPEC],
            out_specs=[_HBM_SPEC] * (2 * n),
            input_output_aliases={i: i for i in range(2 * n)},
            compiler_params=pltpu.CompilerParams(has_side_effects=pltpu.SideEffectType.DATAFLOW_SIDE_EFFECTING),
        )(*srcs, *lands, send_sems, recv_sems, after)
        return outs[n:]


def _gather_exchange(name, axes, shard_sizes):
    return _Exchange(name, lambda w, src, dev: src,
                     lambda w, land, dev: _shard_slice(land, axes[w], dev, shard_sizes[w]))


def _scatter_exchange(name, axes, shard_sizes):
    def part(w, src, dev):
        return src if axes[w] is None else _shard_slice(src, axes[w], dev, shard_sizes[w])
    return _Exchange(name, part, lambda w, land, dev: land.at[dev])


def _adamw(name, partials, w, m, v):
    R, C = w.shape
    tr = R
    while tr * C * 4 > (1 << 20) and tr % 16 == 0:
        tr //= 2

    def body(p_ref, w_ref, m_ref, v_ref, g_ref, d_ref, nm_ref, nv_ref):
        g = p_ref[0].astype(F32)
        for jdev in range(1, N_DEV):
            g = g + p_ref[jdev].astype(F32)
        nm = ADAM_B1 * m_ref[...] + (1.0 - ADAM_B1) * g
        nv = ADAM_B2 * v_ref[...] + (1.0 - ADAM_B2) * (g * g)
        m_hat = nm / (1.0 - ADAM_B1 ** ADAM_STEP)
        v_hat = nv / (1.0 - ADAM_B2 ** ADAM_STEP)
        g_ref[...] = g
        d_ref[...] = -ADAM_LR * (m_hat / (jnp.sqrt(v_hat) + ADAM_EPS) + ADAM_WD * w_ref[...])
        nm_ref[...] = nm
        nv_ref[...] = nv

    spec = pl.BlockSpec((tr, C), lambda i: (i, 0))
    return pl.pallas_call(
        body, name=name, grid=(R // tr,),
        in_specs=[pl.BlockSpec((N_DEV, tr, C), lambda i: (0, i, 0)), spec, spec, spec],
        out_specs=[spec] * 4,
        out_shape=[jax.ShapeDtypeStruct((R, C), F32)] * 4,
        compiler_params=pltpu.CompilerParams(dimension_semantics=("parallel",),
                                             vmem_limit_bytes=_vmem_limit(24 << 20)),
    )(partials, w, m, v)


def _local_step(x, cos2, sin_fwd, sin_bwd, w_in, mix_weights, ffn_weights, pool_scale, g_mix, b_mix, g_ff, b_ff,
                target, send, start_after=None):
    S, D = x.shape
    aw = pw = D // 2
    u_col_block = 3
    gate_col0 = 4 * aw

    xb = x.astype(BF16)
    h = _in_proj(xb, w_in, cos2, sin_fwd, 2 * aw, after=start_after)
    dilated = [d for d in DILATIONS if d > 1]
    qkv = {1: h[None], **dict(zip(dilated, _to_residue_major("qkv_to_rm", h, 0, 3 * aw)))}
    fwd = [_attn_fwd(qkv[d], d, aw) for d in DILATIONS]
    o_attn, l_tot = _attn_combine([f[0] for f in fwd], [f[1] for f in fwd], aw)
    w_pool, w_ba, w_bp, w_out = mix_weights(o_attn)
    p, y_pre, pm = _pool_fwd(h, w_pool, pool_scale, pw, u_col_block)
    y_attn = _branch_attn(o_attn, w_ba)
    y_pool, merged = _branch_pool_merge(pm, w_bp, h, y_attn, gate_col0)
    w1, w2 = ffn_weights(merged)
    x1, x1b, xhat1, rstd1 = _out_proj_ln(merged, w_out, x, g_mix, b_mix)
    a = _ffn_up(x1b, w1)
    dz2, dz2b, dg_ff, db_ff, loss = _ffn_down_loss(a, w2, x1, g_ff, b_ff, target)

    tok = send("ff2", [_grad_weight("grad_w_ff2", a, dz2b)])
    dh1 = _ffn_down_bwd(dz2b, w2, a, after=tok)
    tok = send("ff1", [_grad_weight("grad_w_ff1", x1b, dh1)])
    dz1, dz1b, dg_mix, db_mix = _ffn_up_bwd_ln(dh1, w1, dz2, xhat1, rstd1, g_mix, after=tok)
    tok = send("out", [_grad_weight("grad_w_out", merged, dz1b)])
    dy_attn, dy_pool, dh = _out_proj_bwd(dz1b, w_out, h, y_attn, y_pool, gate_col0, after=tok)
    tok = send("branch", [_grad_weight("grad_w_branch_attn", o_attn, dy_attn),
                          _grad_weight("grad_w_branch_pool", pm, dy_pool)])
    d_out, stats = _branch_attn_bwd(dy_attn, w_ba, o_attn, l_tot, after=tok)
    dy_pre, d_scale = _branch_pool_bwd(dy_pool, w_bp, y_pre, pool_scale)
    dh, dw_pool = _pool_bwd(dh, dy_pre, p, w_pool, pw, u_col_block)
    d_outs = {1: d_out[None], **dict(zip(dilated, _to_residue_major("dout_to_rm", d_out, 0, aw)))}
    statss = {1: stats[None], **dict(zip(dilated, _to_residue_major("stats_to_rm", stats, 0, HEAD_DIM)))}
    per_pattern = [_attn_bwd(qkv[d], d_outs[d], statss[d], d, aw) for d in DILATIONS]
    dh = _attn_bwd_finish(dh, per_pattern, cos2, sin_bwd, aw)
    small = jnp.concatenate((d_scale, dg_mix, db_mix, dg_ff, db_ff), axis=-1)
    tok = send("in", [_grad_weight("grad_w_in", xb, dh), dw_pool.astype(BF16),
                      small.reshape(small.shape[-1] // HEAD_DIM, HEAD_DIM)])
    grad_x = _in_proj_bwd(dh, w_in, dz1, after=tok)
    return loss, grad_x


def _rope_tables(positions):
    half = HEAD_DIM // 2
    inv_freq = ROPE_THETA ** (-jnp.arange(half, dtype=F32) / half)
    ang = positions.astype(F32)[:, None] * inv_freq
    cos, sin = jnp.cos(ang), jnp.sin(ang)
    cos2 = jnp.concatenate([cos, cos], axis=-1)
    sin_fwd = jnp.concatenate([-sin, sin], axis=-1)
    return cos2, sin_fwd, -sin_fwd


def kernel(x, positions, w_in, w_pool, pool_scale, w_branch_attn, w_branch_pool, w_out, ln_mix_g, ln_mix_b, w_ff1, w_ff2, ln_ff_g, ln_ff_b, loss_target, m_w_in, m_w_pool, m_pool_scale, m_w_branch_attn, m_w_branch_pool, m_w_out, m_ln_mix_g, m_ln_mix_b, m_w_ff1, m_w_ff2, m_ln_ff_g, m_ln_ff_b, v_w_in, v_w_pool, v_pool_scale, v_w_branch_attn, v_w_branch_pool, v_w_out, v_ln_mix_g, v_ln_mix_b, v_w_ff1, v_w_ff2, v_ln_ff_g, v_ln_ff_b):
    big_w = (w_in[0], w_pool[0], w_branch_attn[0], w_branch_pool[0], w_out[0], w_ff1[0], w_ff2[0])
    big_m = (m_w_in[0], m_w_pool[0], m_w_branch_attn[0], m_w_branch_pool[0], m_w_out[0], m_w_ff1[0], m_w_ff2[0])
    big_v = (v_w_in[0], v_w_pool[0], v_w_branch_attn[0], v_w_branch_pool[0], v_w_out[0], v_w_ff1[0], v_w_ff2[0])
    shard_axes = (1, 1, 1, 1, 0, 1, 0)
    small_w = (pool_scale, ln_mix_g, ln_mix_b, ln_ff_g, ln_ff_b)
    small_m = (m_pool_scale, m_ln_mix_g, m_ln_mix_b, m_ln_ff_g, m_ln_ff_b)
    small_v = (v_pool_scale, v_ln_mix_g, v_ln_mix_b, v_ln_ff_g, v_ln_ff_b)

    me = 4 * lax.axis_index("x") + 2 * lax.axis_index("y") + lax.axis_index("c")
    names = ("w_in", "w_pool", "w_branch_attn", "w_branch_pool", "w_out", "w_ff1", "w_ff2")
    axis_of = dict(zip(names, shard_axes))
    shard_of = dict(zip(names, [w.astype(BF16) for w in big_w]))

    def own_place(n):
        s, ax = shard_of[n], axis_of[n]
        full = list(s.shape)
        full[ax] *= N_DEV
        return lax.dynamic_update_slice_in_dim(lax.empty(tuple(full), s.dtype), s, me * s.shape[ax], ax)

    def gather_group(tag, group, after):
        ex = _gather_exchange(tag, [axis_of[n] for n in group], [shard_of[n].shape[axis_of[n]] for n in group])
        return ex, ex.start([shard_of[n] for n in group], [own_place(n) for n in group], after)

    (gw_in,) = _all_gather_weights([shard_of["w_in"]], (axis_of["w_in"],))
    mix_ex, mix_started = gather_group("gather_mix", ("w_pool", "w_branch_attn", "w_branch_pool", "w_out"), gw_in)
    ffn_ex, ffn_started = gather_group("gather_ffn", ("w_ff1", "w_ff2"), mix_started[4])

    groups = {"ff2": ("w_ff2",), "ff1": ("w_ff1",), "out": ("w_out",),
              "branch": ("w_branch_attn", "w_branch_pool"), "in": ("w_in", "w_pool", "small")}
    sent = {}

    def send(key, grads_):
        axes = [axis_of.get(n) for n in groups[key]]
        sizes = [None if ax is None else g.shape[ax] // N_DEV for g, ax in zip(grads_, axes)]
        lands = []
        for g, ax, size in zip(grads_, axes, sizes):
            own = g if ax is None else lax.dynamic_slice_in_dim(g, me * size, size, ax)
            lands.append(lax.dynamic_update_slice_in_dim(lax.empty((N_DEV,) + own.shape, g.dtype), own[None], me, 0))
        ex = _scatter_exchange("scatter_" + key, axes, sizes)
        sent[key] = (ex, ex.start(list(grads_), lands, grads_[0]))
        return sent[key][1][4]

    cos2, sin_fwd, sin_bwd = _rope_tables(positions[0])
    loss, grad_x = _local_step(
        x[0], cos2, sin_fwd, sin_bwd, gw_in, lambda after: mix_ex.wait(mix_started, after),
        lambda after: ffn_ex.wait(ffn_started, after), pool_scale, ln_mix_g, ln_mix_b, ln_ff_g, ln_ff_b,
        loss_target[0], send, start_after=ffn_started[4])

    state = dict(zip(names, zip(big_w, big_m, big_v)))
    n_small = sum(w.shape[-1] for w in small_w)
    small_2d = (n_small // HEAD_DIM, HEAD_DIM)
    state["small"] = tuple(jnp.concatenate(t, axis=-1).reshape(small_2d) for t in (small_w, small_m, small_v))
    grads, deltas, new_ms, new_vs = {}, {}, {}, {}
    after = grad_x
    for key in ("ff2", "ff1", "out", "branch", "in"):
        ex, started = sent[key]
        for n, part in zip(groups[key], ex.wait(started, after)):
            w, m, v = state[n]
            r2 = (-1, w.shape[-1])
            w2d = w.reshape(r2)
            res = _adamw("adamw_" + n, part.reshape((N_DEV,) + w2d.shape), w2d, m.reshape(r2), v.reshape(r2))
            after = res[0]
            if n == "small":
                small_out = [t.reshape(1, n_small) for t in res]
            else:
                grads[n], deltas[n], new_ms[n], new_vs[n] = (t.reshape((1,) + w.shape) for t in res)
    small_names = ("pool_scale", "ln_mix_g", "ln_mix_b", "ln_ff_g", "ln_ff_b")
    off = 0
    for n, w in zip(small_names, small_w):
        width = w.shape[-1]
        grads[n], deltas[n], new_ms[n], new_vs[n] = (t[:, off:off + width] for t in small_out)
        off += width

    order = ("w_in", "w_pool", "pool_scale", "w_branch_attn", "w_branch_pool", "w_out", "ln_mix_g", "ln_mix_b",
             "w_ff1", "w_ff2", "ln_ff_g", "ln_ff_b")
    total_loss = lax.psum(loss[0, 0], ("x", "y", "c"))
    return (total_loss, grad_x[None], *[grads[n] for n in order], *[deltas[n] for n in order],
            *[new_ms[n] for n in order], *[new_vs[n] for n in order])
```

```python
import functools

import jax
import jax.numpy as jnp
from jax import lax
from jax.experimental import pallas as pl
from jax.experimental.pallas import tpu as pltpu

F32 = jnp.float32
BF16 = jnp.bfloat16

N_DEV = 8
HEAD_DIM = 128
SUB_BLOCK = 128
DILATIONS = (1, 4, 16)
POOL_WINDOWS = (2, 4, 8, 16)
MAX_POOL_WINDOW = 16
POOL_HALO = 128
PERM_ROWS = 512
LN_EPS = 1e-5
DEEPNORM_ALPHA = 2.0 ** 0.25
ROPE_THETA = 10000.0
ATTN_SCALE = HEAD_DIM ** -0.5
ADAM_LR, ADAM_B1, ADAM_B2, ADAM_EPS, ADAM_WD, ADAM_STEP = 0.001, 0.9, 0.999, 1e-08, 0.01, 10
NEG_BIG = -1e30
VMEM_CAP_V7X = 64 * 1024 * 1024
MESH = pl.DeviceIdType.MESH


def _vmem_limit(est_bytes):
    return int(min(max(est_bytes * 5 // 4 + (4 << 20), 16 << 20), VMEM_CAP_V7X - (6 << 20)))


def _nbytes(shape, dtype):
    n = 1
    for s in shape:
        n *= s
    return n * jnp.dtype(dtype).itemsize


def _mm(name, a, b, form, tiles, outs, epi, extras=(), sequential=False, after=None):
    tm, tn, tk = tiles
    if form == "nn":
        (M, K), (K2, N) = a.shape, b.shape
    elif form == "nt":
        (M, K), (N, K2) = a.shape, b.shape
    else:
        (K, M), (K2, N) = a.shape, b.shape
    assert K == K2, (name, a.shape, b.shape)
    tm, tn, tk = min(tm, M), min(tn, N), min(tk, K)
    assert M % tm == 0 and N % tn == 0 and K % tk == 0, (name, M, N, K, tm, tn, tk)
    grid = (M // tm, N // tn, K // tk)
    nk = grid[2]
    if form == "nn":
        a_spec = pl.BlockSpec((tm, tk), lambda i, j, k: (i, k))
        b_spec = pl.BlockSpec((tk, tn), lambda i, j, k: (k, j))
        contract = ((1,), (0,))
    elif form == "nt":
        a_spec = pl.BlockSpec((tm, tk), lambda i, j, k: (i, k))
        b_spec = pl.BlockSpec((tn, tk), lambda i, j, k: (j, k))
        contract = ((1,), (1,))
    else:
        a_spec = pl.BlockSpec((tk, tm), lambda i, j, k: (k, i))
        b_spec = pl.BlockSpec((tk, tn), lambda i, j, k: (k, j))
        contract = ((0,), (0,))
    n_ex, n_out = len(extras), len(outs)
    n_after = 0 if after is None else 1

    def body(a_ref, b_ref, *rest):
        ex_refs = rest[:n_ex]
        rest = rest[n_ex + n_after:]
        out_refs = rest[:n_out]
        i, j, k = pl.program_id(0), pl.program_id(1), pl.program_id(2)

        def prod():
            return lax.dot_general(a_ref[...].astype(BF16), b_ref[...].astype(BF16),
                                   (contract, ((), ())), preferred_element_type=F32)

        if nk == 1:
            epi(prod(), ex_refs, out_refs, i, j)
        else:
            acc = rest[n_out]

            @pl.when(k == 0)
            def _():
                acc[...] = prod()

            @pl.when(k > 0)
            def _():
                acc[...] += prod()

            @pl.when(k == nk - 1)
            def _():
                epi(acc[...], ex_refs, out_refs, i, j)

    est = 2 * (_nbytes(a_spec.block_shape, a.dtype) + _nbytes(b_spec.block_shape, b.dtype))
    est += sum(2 * _nbytes(bs, arr.dtype) for arr, bs, _ in extras)
    est += sum(2 * _nbytes(bs, dt) for _, dt, bs, _ in outs)
    est += 4 * tm * tn * 4
    sem = ("arbitrary",) * 3 if sequential else ("parallel", "parallel", "arbitrary")
    return pl.pallas_call(
        body, name=name, grid=grid,
        in_specs=([a_spec, b_spec] + [pl.BlockSpec(bs, im) for _, bs, im in extras]
                  + [pl.BlockSpec(memory_space=pl.ANY)] * n_after),
        out_specs=[pl.BlockSpec(bs, im) for _, _, bs, im in outs],
        out_shape=[jax.ShapeDtypeStruct(sh, dt) for sh, dt, _, _ in outs],
        scratch_shapes=[pltpu.VMEM((tm, tn), F32)] if nk > 1 else [],
        compiler_params=pltpu.CompilerParams(dimension_semantics=sem, vmem_limit_bytes=_vmem_limit(est)),
    )(a, b, *[arr for arr, _, _ in extras], *([after] if n_after else []))


def _tile_out(shape, dtype, tm, tn):
    return (shape, dtype, (tm, tn), lambda i, j, k: (i, j))


def _row_sum_out(width):
    return ((1, width), F32, (1, width), lambda i, j, k: (0, 0))


def _accumulate_rows(ref, value, i):
    @pl.when(i == 0)
    def _():
        ref[...] = value

    @pl.when(i > 0)
    def _():
        ref[...] += value


def _layer_norm_bwd(dy, xhat, rstd, g):
    dxh = dy * g
    m1 = jnp.mean(dxh, axis=-1, keepdims=True)
    m2 = jnp.mean(dxh * xhat, axis=-1, keepdims=True)
    return rstd * (dxh - m1 - xhat * m2)


def _rope_apply(t, cos2, sin_signed):
    return t * cos2 + pltpu.roll(t, HEAD_DIM // 2, axis=1) * sin_signed


def _in_proj(xb, w_in, cos2, sin_fwd, n_rope_cols, after=None):
    S, D = xb.shape
    W = w_in.shape[1]
    tm, tn = min(1024, S), min(2048, n_rope_cols)
    assert n_rope_cols % tn == 0
    n_rope_tiles = n_rope_cols // tn

    def epi(acc, ex, out, i, j):
        cos_ref, sin_ref = ex
        (h_ref,) = out

        @pl.when(j < n_rope_tiles)
        def _():
            c, s = cos_ref[...], sin_ref[...]
            for hd in range(tn // HEAD_DIM):
                sl = slice(hd * HEAD_DIM, (hd + 1) * HEAD_DIM)
                h_ref[:, sl] = _rope_apply(acc[:, sl], c, s).astype(BF16)

        @pl.when(j >= n_rope_tiles)
        def _():
            h_ref[...] = acc.astype(BF16)

    row = lambda i, j, k: (i, 0)
    (h,) = _mm("in_proj", xb, w_in, "nn", (tm, tn, 512),
               [_tile_out((S, W), BF16, tm, tn)], epi,
               extras=[(cos2, (tm, HEAD_DIM), row), (sin_fwd, (tm, HEAD_DIM), row)], after=after)
    return h


def _attn_mask(mb):
    qi = lax.broadcasted_iota(jnp.int32, (SUB_BLOCK, 2 * SUB_BLOCK), 0)
    kj = lax.broadcasted_iota(jnp.int32, (SUB_BLOCK, 2 * SUB_BLOCK), 1)
    prev = jnp.logical_and(jnp.logical_and(kj < SUB_BLOCK, kj >= qi), mb > 0)
    cur = jnp.logical_and(kj >= SUB_BLOCK, kj - SUB_BLOCK <= qi)
    return jnp.logical_or(prev, cur)


def _dot_nt(a, b):
    return lax.dot_general(a, b, (((1,), (1,)), ((), ())), preferred_element_type=F32)


def _dot_tn(a, b):
    return lax.dot_general(a, b, (((0,), (0,)), ((), ())), preferred_element_type=F32)


def _dot_nn(a, b):
    return lax.dot_general(a, b, (((1,), (0,)), ((), ())), preferred_element_type=F32)


def _perm_matrix(d, to_residue_major):
    g = PERM_ROWS // d
    i = lax.broadcasted_iota(jnp.int32, (PERM_ROWS, PERM_ROWS), 0)
    j = lax.broadcasted_iota(jnp.int32, (PERM_ROWS, PERM_ROWS), 1)
    if to_residue_major:
        hit = j == (i % g) * d + i // g
    else:
        hit = j == (i % d) * g + i // d
    return hit.astype(BF16)


def _permute_rows(perm, x):
    if x.dtype == BF16:
        return _dot_nn(perm, x)
    hi = x.astype(BF16)
    r1 = x - hi.astype(F32)
    mid = r1.astype(BF16)
    lo = (r1 - mid.astype(F32)).astype(BF16)
    return (_dot_nn(perm, hi) + _dot_nn(perm, mid)) + _dot_nn(perm, lo)


def _rm_block(d, width):
    return pl.BlockSpec((d, PERM_ROWS // d, width), lambda i: (0, i, 0))


def _to_residue_major(name, x, col_block, width):
    S = x.shape[0]
    dils = [d for d in DILATIONS if d > 1]
    chunk = min(width, 1024)

    def body(x_ref, *out_refs):
        for d, o_ref in zip(dils, out_refs):
            perm = _perm_matrix(d, True)
            for c0 in range(0, width, chunk):
                cw = min(chunk, width - c0)
                y = _permute_rows(perm, x_ref[:, c0:c0 + cw])
                o_ref[:, :, c0:c0 + cw] = y.astype(x.dtype).reshape(d, PERM_ROWS // d, cw)

    return pl.pallas_call(
        body, name=name, grid=(S // PERM_ROWS,),
        in_specs=[pl.BlockSpec((PERM_ROWS, width), lambda i: (i, col_block))],
        out_specs=[_rm_block(d, width) for d in dils],
        out_shape=[jax.ShapeDtypeStruct((d, S // d, width), x.dtype) for d in dils],
        compiler_params=pltpu.CompilerParams(dimension_semantics=("parallel",),
                                             vmem_limit_bytes=_vmem_limit(32 << 20)),
    )(x)


def _qkv_specs(aw):
    def spec(col, prev):
        if prev:
            return pl.BlockSpec((None, SUB_BLOCK, aw), lambda r, mb: (r, jnp.maximum(mb - 1, 0), col))
        return pl.BlockSpec((None, SUB_BLOCK, aw), lambda r, mb: (r, mb, col))
    return [spec(0, False), spec(1, True), spec(1, False), spec(2, True), spec(2, False)]


def _put_column(tile, col, value):
    lane = lax.broadcasted_iota(jnp.int32, tile.shape, 1)
    return jnp.where(lane == col, value, tile)


def _attn_fwd(qkv, d, aw):
    _, rows, _ = qkv.shape
    n_heads = aw // HEAD_DIM
    nb = rows // SUB_BLOCK

    def body(q_ref, kp_ref, kc_ref, vp_ref, vc_ref, o_ref, lse_ref, s_buf, p_buf):
        mask = _attn_mask(pl.program_id(1))
        for hd in range(n_heads):
            sl = slice(hd * HEAD_DIM, (hd + 1) * HEAD_DIM)
            q = q_ref[:, sl]
            s_buf[hd, :, :SUB_BLOCK] = _dot_nt(q, kp_ref[:, sl])
            s_buf[hd, :, SUB_BLOCK:] = _dot_nt(q, kc_ref[:, sl])
        lse_tile = jnp.zeros((SUB_BLOCK, HEAD_DIM), F32)
        inv_tile = jnp.zeros((SUB_BLOCK, HEAD_DIM), F32)
        for hd in range(n_heads):
            s = jnp.where(mask, s_buf[hd] * ATTN_SCALE, NEG_BIG)
            m = jnp.max(s, axis=-1, keepdims=True)
            p = jnp.exp(s - m)
            l = jnp.sum(p, axis=-1, keepdims=True)
            p_buf[hd] = p.astype(BF16)
            lse_tile = _put_column(lse_tile, hd, m + jnp.log(l))
            inv_tile = _put_column(inv_tile, hd, 1.0 / l)
        lse_ref[...] = lse_tile
        for hd in range(n_heads):
            sl = slice(hd * HEAD_DIM, (hd + 1) * HEAD_DIM)
            o = _dot_nn(p_buf[hd, :, :SUB_BLOCK], vp_ref[:, sl]) + _dot_nn(p_buf[hd, :, SUB_BLOCK:], vc_ref[:, sl])
            o_ref[:, sl] = o * inv_tile[:, hd:hd + 1]

    return pl.pallas_call(
        body, name=f"attn_fwd_d{d}", grid=(d, nb),
        in_specs=_qkv_specs(aw),
        out_specs=[pl.BlockSpec((None, SUB_BLOCK, aw), lambda r, mb: (r, mb, 0)),
                   pl.BlockSpec((None, SUB_BLOCK, HEAD_DIM), lambda r, mb: (r, mb, 0))],
        out_shape=[jax.ShapeDtypeStruct((d, rows, aw), F32), jax.ShapeDtypeStruct((d, rows, HEAD_DIM), F32)],
        scratch_shapes=[pltpu.VMEM((n_heads, SUB_BLOCK, 2 * SUB_BLOCK), F32),
                        pltpu.VMEM((n_heads, SUB_BLOCK, 2 * SUB_BLOCK), BF16)],
        compiler_params=pltpu.CompilerParams(dimension_semantics=("parallel", "parallel"),
                                             vmem_limit_bytes=_vmem_limit(16 << 20)),
    )(qkv, qkv, qkv, qkv, qkv)


def _attn_combine(outs, lses, aw):
    S = outs[0].shape[1]
    n_heads = aw // HEAD_DIM
    n_pat = len(DILATIONS)

    def body(*refs):
        o_refs, l_refs = refs[:n_pat], refs[n_pat:2 * n_pat]
        o_ref, lt_ref = refs[2 * n_pat], refs[2 * n_pat + 1]
        o_nat, l_nat = [], []
        for d, o_r, l_r in zip(DILATIONS, o_refs, l_refs):
            o_p = o_r[...].reshape(PERM_ROWS, aw)
            l_p = l_r[...].reshape(PERM_ROWS, HEAD_DIM)
            if d > 1:
                perm = _perm_matrix(d, False)
                o_p, l_p = _permute_rows(perm, o_p), _permute_rows(perm, l_p)
            o_nat.append(o_p)
            l_nat.append(l_p)
        mx = functools.reduce(jnp.maximum, l_nat)
        es = [jnp.exp(l_p - mx) for l_p in l_nat]
        den = functools.reduce(jnp.add, es)
        lt_ref[...] = mx + jnp.log(den)
        ws = [e / den for e in es]
        for hd in range(n_heads):
            sl = slice(hd * HEAD_DIM, (hd + 1) * HEAD_DIM)
            o = ws[0][:, hd:hd + 1] * o_nat[0][:, sl]
            for pi in range(1, n_pat):
                o = o + ws[pi][:, hd:hd + 1] * o_nat[pi][:, sl]
            o_ref[:, sl] = o.astype(BF16)

    return pl.pallas_call(
        body, name="attn_combine", grid=(S // PERM_ROWS,),
        in_specs=[_rm_block(d, aw) for d in DILATIONS] + [_rm_block(d, HEAD_DIM) for d in DILATIONS],
        out_specs=[pl.BlockSpec((PERM_ROWS, aw), lambda i: (i, 0)), pl.BlockSpec((PERM_ROWS, HEAD_DIM), lambda i: (i, 0))],
        out_shape=[jax.ShapeDtypeStruct((S, aw), BF16), jax.ShapeDtypeStruct((S, HEAD_DIM), F32)],
        compiler_params=pltpu.CompilerParams(dimension_semantics=("parallel",),
                                             vmem_limit_bytes=_vmem_limit(40 << 20)),
    )(*outs, *lses)


def _band(tm, width, w, row_offset, transpose):
    t = lax.broadcasted_iota(jnp.int32, (tm, width), 0)
    u = lax.broadcasted_iota(jnp.int32, (tm, width), 1)
    dist = (u - t - row_offset) if transpose else (t + row_offset - u)
    return jnp.logical_and(dist >= 0, dist < w).astype(BF16)


def _pool_fwd(h, w_pool, pool_scale, pw, u_col_block):
    S, W = h.shape
    n_groups = len(POOL_WINDOWS)
    gw = pw // n_groups
    tm = min(512, S)
    halo_per_tile = tm // POOL_HALO

    def body(uc_ref, uh_ref, w_ref, sc_ref, p_ref, y_ref, pm_ref):
        i = pl.program_id(0)
        t_abs = i * tm + lax.broadcasted_iota(jnp.int32, (tm, 1), 0)
        for g, w in enumerate(POOL_WINDOWS):
            sl = slice(g * gw, (g + 1) * gw)
            uc = uc_ref[:, sl]
            uh = jnp.where(i > 0, uh_ref[:, sl], jnp.zeros((POOL_HALO, gw), BF16))
            ssum = _dot_nn(_band(tm, tm, w, 0, False), uc) + _dot_nn(_band(tm, POOL_HALO, w, POOL_HALO, False), uh)
            cnt = jnp.minimum(t_abs + 1, w).astype(F32)
            p = (ssum / cnt - uc.astype(F32)).astype(BF16)
            y = _dot_nn(p, w_ref[g])
            p_ref[:, sl] = p
            y_ref[:, sl] = y.astype(BF16)
            pm_ref[:, sl] = (y * sc_ref[:, sl]).astype(BF16)

    row = pl.BlockSpec((tm, pw), lambda i: (i, 0))
    return pl.pallas_call(
        body, name="pool_fwd", grid=(S // tm,),
        in_specs=[pl.BlockSpec((tm, pw), lambda i: (i, u_col_block)),
                  pl.BlockSpec((POOL_HALO, pw), lambda i: (jnp.maximum(i * halo_per_tile - 1, 0), u_col_block)),
                  pl.BlockSpec((n_groups, gw, gw), lambda i: (0, 0, 0)),
                  pl.BlockSpec((1, pw), lambda i: (0, 0))],
        out_specs=[row, row, row],
        out_shape=[jax.ShapeDtypeStruct((S, pw), BF16)] * 3,
        compiler_params=pltpu.CompilerParams(dimension_semantics=("parallel",),
                                             vmem_limit_bytes=_vmem_limit(24 << 20)),
    )(h, h, w_pool, pool_scale)


def _branch_attn(o_attn, w_ba):
    S, _ = o_attn.shape
    D = w_ba.shape[1]
    tm, tn = min(1024, S), min(1024, D)

    def epi(acc, ex, out, i, j):
        out[0][...] = acc.astype(BF16)

    (y,) = _mm("branch_attn", o_attn, w_ba, "nn", (tm, tn, 1024), [_tile_out((S, D), BF16, tm, tn)], epi)
    return y


def _branch_pool_merge(pm, w_bp, h, y_attn, gate_col0):
    S, _ = pm.shape
    D = w_bp.shape[1]
    tm, tn = min(1024, S), min(1024, D)
    ga0, gp0 = gate_col0 // tn, (gate_col0 + D) // tn

    def epi(acc, ex, out, i, j):
        ga_ref, gp_ref, ya_ref = ex
        yp_ref, mg_ref = out
        yp = acc.astype(BF16)
        yp_ref[...] = yp
        mg = (jax.nn.sigmoid(ga_ref[...].astype(F32)) * ya_ref[...].astype(F32)
              + jax.nn.sigmoid(gp_ref[...].astype(F32)) * acc)
        mg_ref[...] = mg.astype(BF16)

    y_pool, merged = _mm(
        "branch_pool_merge", pm, w_bp, "nn", (tm, tn, 1024),
        [_tile_out((S, D), BF16, tm, tn), _tile_out((S, D), BF16, tm, tn)], epi,
        extras=[(h, (tm, tn), lambda i, j, k: (i, ga0 + j)), (h, (tm, tn), lambda i, j, k: (i, gp0 + j)),
                (y_attn, (tm, tn), lambda i, j, k: (i, j))])
    return y_pool, merged


def _layer_norm_rows(z, g, b):
    mu = jnp.mean(z, axis=-1, keepdims=True)
    zc = z - mu
    var = jnp.mean(zc * zc, axis=-1, keepdims=True)
    rstd = lax.rsqrt(var + LN_EPS)
    xhat = zc * rstd
    return xhat * g + b, xhat, rstd


def _out_proj_ln(merged, w_out, x, g, b):
    S, D = x.shape
    tm = min(512, S)

    def epi(acc, ex, out, i, j):
        x_ref, g_ref, b_ref = ex
        x1_ref, x1b_ref, xh_ref, rs_ref = out
        y, xhat, rstd = _layer_norm_rows(DEEPNORM_ALPHA * x_ref[...] + acc, g_ref[...], b_ref[...])
        x1_ref[...] = y
        x1b_ref[...] = y.astype(BF16)
        xh_ref[...] = xhat
        rs_ref[...] = jnp.broadcast_to(rstd, (tm, HEAD_DIM))

    row = lambda i, j, k: (i, 0)
    vec = lambda i, j, k: (0, 0)
    return _mm("out_proj_ln", merged, w_out, "nn", (tm, D, 512),
               [((S, D), F32, (tm, D), row), ((S, D), BF16, (tm, D), row), ((S, D), F32, (tm, D), row),
                ((S, HEAD_DIM), F32, (tm, HEAD_DIM), row)], epi,
               extras=[(x, (tm, D), row), (g, (1, D), vec), (b, (1, D), vec)])


def _ffn_up(x1b, w1):
    S, D = x1b.shape
    F = w1.shape[1]
    tm, tn = min(1024, S), min(2048, F)

    def epi(acc, ex, out, i, j):
        r = jnp.maximum(acc, 0.0)
        out[0][...] = (r * r).astype(BF16)

    (a,) = _mm("ffn_up", x1b, w1, "nn", (tm, tn, 512), [_tile_out((S, F), BF16, tm, tn)], epi)
    return a


def _ffn_down_loss(a, w2, x1, g, b, target):
    S, D = x1.shape
    tm = min(512, S)

    def epi(acc, ex, out, i, j):
        x1_ref, g_ref, b_ref, t_ref = ex
        dz_ref, dzb_ref, dg_ref, db_ref, loss_ref = out
        gv = g_ref[...]
        y, xhat, rstd = _layer_norm_rows(DEEPNORM_ALPHA * x1_ref[...] + acc, gv, b_ref[...])
        err = y - t_ref[...]
        loss = 0.5 * jnp.sum(jnp.mean(err * err, axis=-1, keepdims=True), axis=0, keepdims=True)
        dy = err * (1.0 / D)
        dz = _layer_norm_bwd(dy, xhat, rstd, gv)
        dz_ref[...] = dz
        dzb_ref[...] = dz.astype(BF16)
        _accumulate_rows(dg_ref, jnp.sum(dy * xhat, axis=0, keepdims=True), i)
        _accumulate_rows(db_ref, jnp.sum(dy, axis=0, keepdims=True), i)
        _accumulate_rows(loss_ref, jnp.broadcast_to(loss, (1, HEAD_DIM)), i)

    row = lambda i, j, k: (i, 0)
    vec = lambda i, j, k: (0, 0)
    return _mm("ffn_down_loss", a, w2, "nn", (tm, D, 512),
               [((S, D), F32, (tm, D), row), ((S, D), BF16, (tm, D), row),
                _row_sum_out(D), _row_sum_out(D), _row_sum_out(HEAD_DIM)], epi,
               extras=[(x1, (tm, D), row), (g, (1, D), vec), (b, (1, D), vec), (target, (tm, D), row)],
               sequential=True)


def _grad_weight(name, act, cot):
    M, N = act.shape[1], cot.shape[1]
    tm, tn = min(1024, M), min(2048, N)

    def epi(acc, ex, out, i, j):
        out[0][...] = acc.astype(BF16)

    (g,) = _mm(name, act, cot, "tn", (tm, tn, 512), [_tile_out((M, N), BF16, tm, tn)], epi)
    return g


def _ffn_down_bwd(dz2b, w2, a, after=None):
    S, D = dz2b.shape
    F = w2.shape[0]
    tm, tn = min(1024, S), min(2048, F)

    def epi(acc, ex, out, i, j):
        out[0][...] = (acc * (2.0 * jnp.sqrt(ex[0][...].astype(F32)))).astype(BF16)

    (dh1,) = _mm("ffn_down_bwd", dz2b, w2, "nt", (tm, tn, 512), [_tile_out((S, F), BF16, tm, tn)], epi,
                 extras=[(a, (tm, tn), lambda i, j, k: (i, j))], after=after)
    return dh1


def _ffn_up_bwd_ln(dh1, w1, dz2, xhat1, rstd1, g1, after=None):
    S, D = dz2.shape
    tm = min(512, S)

    def epi(acc, ex, out, i, j):
        dz2_ref, xh_ref, rs_ref, g_ref = ex
        dz_ref, dzb_ref, dg_ref, db_ref = out
        dy = DEEPNORM_ALPHA * dz2_ref[...] + acc
        xhat = xh_ref[...]
        dz = _layer_norm_bwd(dy, xhat, rs_ref[:, :1], g_ref[...])
        dz_ref[...] = dz
        dzb_ref[...] = dz.astype(BF16)
        _accumulate_rows(dg_ref, jnp.sum(dy * xhat, axis=0, keepdims=True), i)
        _accumulate_rows(db_ref, jnp.sum(dy, axis=0, keepdims=True), i)

    row = lambda i, j, k: (i, 0)
    vec = lambda i, j, k: (0, 0)
    return _mm("ffn_up_bwd_ln", dh1, w1, "nt", (tm, D, 512),
               [((S, D), F32, (tm, D), row), ((S, D), BF16, (tm, D), row), _row_sum_out(D), _row_sum_out(D)], epi,
               extras=[(dz2, (tm, D), row), (xhat1, (tm, D), row), (rstd1, (tm, HEAD_DIM), row), (g1, (1, D), vec)],
               sequential=True, after=after)


def _out_proj_bwd(dz1b, w_out, h, y_attn, y_pool, gate_col0, after=None):
    S, D = dz1b.shape
    W = h.shape[1]
    tm = min(256, S)
    assert gate_col0 == 2 * D and W == 4 * D

    def epi(acc, ex, out, i, j):
        gates_ref, ya_ref, yp_ref = ex
        dya_ref, dyp_ref, dh_ref = out
        sa = jax.nn.sigmoid(gates_ref[:, :D].astype(F32))
        sp = jax.nn.sigmoid(gates_ref[:, D:].astype(F32))
        dya_ref[...] = (acc * sa).astype(BF16)
        dyp_ref[...] = (acc * sp).astype(BF16)
        dh_ref[:, :D] = (acc * ya_ref[...].astype(F32) * (sa * (1.0 - sa))).astype(BF16)
        dh_ref[:, D:] = (acc * yp_ref[...].astype(F32) * (sp * (1.0 - sp))).astype(BF16)

    row = lambda i, j, k: (i, 0)
    return _mm("out_proj_bwd", dz1b, w_out, "nt", (tm, D, 512),
               [((S, D), BF16, (tm, D), row), ((S, D), BF16, (tm, D), row),
                ((S, W), BF16, (tm, 2 * D), lambda i, j, k: (i, 1))], epi,
               extras=[(h, (tm, 2 * D), lambda i, j, k: (i, 1)), (y_attn, (tm, D), row), (y_pool, (tm, D), row)],
               after=after)


def _branch_attn_bwd(dy_attn, w_ba, o_attn, l_tot, after=None):
    S, D = dy_attn.shape
    aw = w_ba.shape[0]
    n_heads = aw // HEAD_DIM
    tm = min(512, S)

    def epi(acc, ex, out, i, j):
        do_ref, st_ref = out
        do_ref[...] = acc.astype(BF16)
        o = ex[0][...].astype(F32)
        stats = ex[1][...]
        for hd in range(n_heads):
            sl = slice(hd * HEAD_DIM, (hd + 1) * HEAD_DIM)
            stats = _put_column(stats, n_heads + hd, jnp.sum(acc[:, sl] * o[:, sl], axis=-1, keepdims=True))
        st_ref[...] = stats

    row = lambda i, j, k: (i, 0)
    return _mm("branch_attn_bwd", dy_attn, w_ba, "nt", (tm, aw, 512),
               [((S, aw), BF16, (tm, aw), row), ((S, HEAD_DIM), F32, (tm, HEAD_DIM), row)], epi,
               extras=[(o_attn, (tm, aw), row), (l_tot, (tm, HEAD_DIM), row)], after=after)


def _branch_pool_bwd(dy_pool, w_bp, y_pre, pool_scale):
    S, D = dy_pool.shape
    pw = w_bp.shape[0]
    tm = min(512, S)

    def epi(acc, ex, out, i, j):
        y_ref, sc_ref = ex
        dyp_ref, dsc_ref = out
        dyp_ref[...] = (acc * sc_ref[...]).astype(BF16)
        _accumulate_rows(dsc_ref, jnp.sum(acc * y_ref[...].astype(F32), axis=0, keepdims=True), i)

    row = lambda i, j, k: (i, 0)
    return _mm("branch_pool_bwd", dy_pool, w_bp, "nt", (tm, pw, 512),
               [((S, pw), BF16, (tm, pw), row), _row_sum_out(pw)], epi,
               extras=[(y_pre, (tm, pw), row), (pool_scale, (1, pw), lambda i, j, k: (0, 0))],
               sequential=True)


def _pool_bwd(dh, dy_pre, p, w_pool, pw, u_col_block):
    S, W = dh.shape
    n_groups = len(POOL_WINDOWS)
    gw = pw // n_groups
    tm = min(512, S)
    n_tiles = S // tm
    halo_per_tile = tm // POOL_HALO
    n_halo_blocks = S // POOL_HALO

    def body(dh_in_ref, dyc_ref, dyh_ref, p_ref, w_ref, dh_ref, dw_ref):
        del dh_in_ref
        i = pl.program_id(0)
        t_cur = i * tm + lax.broadcasted_iota(jnp.int32, (tm, 1), 0)
        t_halo = (i + 1) * tm + lax.broadcasted_iota(jnp.int32, (POOL_HALO, 1), 0)
        for g, w in enumerate(POOL_WINDOWS):
            sl = slice(g * gw, (g + 1) * gw)
            wg = w_ref[g]
            dyc = dyc_ref[:, sl]
            dyh = jnp.where(i < n_tiles - 1, dyh_ref[:, sl], jnp.zeros((POOL_HALO, gw), BF16))
            dp_cur = _dot_nt(dyc, wg)
            dp_halo = _dot_nt(dyh, wg)
            dpc_cur = (dp_cur / jnp.minimum(t_cur + 1, w).astype(F32)).astype(BF16)
            dpc_halo = (dp_halo / jnp.minimum(t_halo + 1, w).astype(F32)).astype(BF16)
            du = (_dot_nn(_band(tm, tm, w, 0, True), dpc_cur)
                  + _dot_nn(_band(tm, POOL_HALO, w, -tm, True), dpc_halo) - dp_cur)
            dh_ref[:, sl] = du.astype(BF16)
            dw = _dot_tn(p_ref[:, sl], dyc)

            @pl.when(i == 0)
            def _():
                dw_ref[g] = dw

            @pl.when(i > 0)
            def _():
                dw_ref[g] += dw

    row = pl.BlockSpec((tm, pw), lambda i: (i, 0))
    dh_new, dw_pool = pl.pallas_call(
        body, name="pool_bwd", grid=(n_tiles,),
        in_specs=[pl.BlockSpec(memory_space=pl.ANY), row,
                  pl.BlockSpec((POOL_HALO, pw), lambda i: (jnp.minimum((i + 1) * halo_per_tile, n_halo_blocks - 1), 0)),
                  row, pl.BlockSpec((n_groups, gw, gw), lambda i: (0, 0, 0))],
        out_specs=[pl.BlockSpec((tm, pw), lambda i: (i, u_col_block)),
                   pl.BlockSpec((n_groups, gw, gw), lambda i: (0, 0, 0))],
        out_shape=[jax.ShapeDtypeStruct((S, W), BF16), jax.ShapeDtypeStruct((n_groups, gw, gw), F32)],
        input_output_aliases={0: 0},
        compiler_params=pltpu.CompilerParams(dimension_semantics=("arbitrary",),
                                             vmem_limit_bytes=_vmem_limit(24 << 20)),
    )(dh, dy_pre, dy_pre, p, w_pool)
    return dh_new, dw_pool


def _attn_bwd(qkv, d_out, stats, d, aw):
    _, rows, _ = qkv.shape
    n_heads = aw // HEAD_DIM
    nb = rows // SUB_BLOCK
    n_blocks = d * nb

    def body(q_ref, kp_ref, kc_ref, vp_ref, vc_ref, do_ref, st_ref, dq_ref, dk_ref, dv_ref,
             carry_k, carry_v, s_buf, dp_buf, p_buf, ds_buf):
        step = pl.program_id(0)

        @pl.when(step == 0)
        def _():
            carry_k[...] = jnp.zeros_like(carry_k)
            carry_v[...] = jnp.zeros_like(carry_v)

        @pl.when(step < n_blocks)
        def _():
            mask = _attn_mask(step % nb)
            st = st_ref[...]
            lo, hi = slice(0, SUB_BLOCK), slice(SUB_BLOCK, 2 * SUB_BLOCK)
            for hd in range(n_heads):
                sl = slice(hd * HEAD_DIM, (hd + 1) * HEAD_DIM)
                q, do = q_ref[:, sl], do_ref[:, sl]
                s_buf[hd, :, lo] = _dot_nt(q, kp_ref[:, sl])
                s_buf[hd, :, hi] = _dot_nt(q, kc_ref[:, sl])
                dp_buf[hd, :, lo] = _dot_nt(do, vp_ref[:, sl])
                dp_buf[hd, :, hi] = _dot_nt(do, vc_ref[:, sl])
            for hd in range(n_heads):
                lt, dl = st[:, hd:hd + 1], st[:, n_heads + hd:n_heads + hd + 1]
                p = jnp.where(mask, jnp.exp(jnp.where(mask, s_buf[hd] * ATTN_SCALE - lt, NEG_BIG)), 0.0)
                p_buf[hd] = p.astype(BF16)
                ds_buf[hd] = (p * (dp_buf[hd] - dl) * ATTN_SCALE).astype(BF16)
            for hd in range(n_heads):
                sl = slice(hd * HEAD_DIM, (hd + 1) * HEAD_DIM)
                q, do = q_ref[:, sl], do_ref[:, sl]
                dq_ref[:, sl] = (_dot_nn(ds_buf[hd, :, lo], kp_ref[:, sl])
                                 + _dot_nn(ds_buf[hd, :, hi], kc_ref[:, sl])).astype(BF16)
                dk_ref[:, sl] = (carry_k[:, sl] + _dot_tn(ds_buf[hd, :, lo], q)).astype(BF16)
                dv_ref[:, sl] = (carry_v[:, sl] + _dot_tn(p_buf[hd, :, lo], do)).astype(BF16)
                carry_k[:, sl] = _dot_tn(ds_buf[hd, :, hi], q)
                carry_v[:, sl] = _dot_tn(p_buf[hd, :, hi], do)

        @pl.when(step == n_blocks)
        def _():
            dk_ref[...] = carry_k[...].astype(BF16)
            dv_ref[...] = carry_v[...].astype(BF16)

    def cur(step):
        return jnp.minimum(step, n_blocks - 1)

    def qkv_spec(col, prev):
        if prev:
            return pl.BlockSpec((SUB_BLOCK, aw), lambda s: (jnp.maximum(cur(s) - 1, 0), col))
        return pl.BlockSpec((SUB_BLOCK, aw), lambda s: (cur(s), col))

    def at_cur(w):
        return pl.BlockSpec((SUB_BLOCK, w), lambda s: (cur(s), 0))

    finished = pl.BlockSpec((SUB_BLOCK, aw), lambda s: (jnp.maximum(s - 1, 0), 0))
    pair = (n_heads, SUB_BLOCK, 2 * SUB_BLOCK)
    flat = lambda t: t.reshape(d * rows, t.shape[-1])
    qkv2 = flat(qkv)
    outs = pl.pallas_call(
        body, name=f"attn_bwd_d{d}", grid=(n_blocks + 1,),
        in_specs=[qkv_spec(0, False), qkv_spec(1, True), qkv_spec(1, False), qkv_spec(2, True), qkv_spec(2, False),
                  at_cur(aw), at_cur(HEAD_DIM)],
        out_specs=[at_cur(aw), finished, finished],
        out_shape=[jax.ShapeDtypeStruct((d * rows, aw), BF16)] * 3,
        scratch_shapes=[pltpu.VMEM((SUB_BLOCK, aw), F32), pltpu.VMEM((SUB_BLOCK, aw), F32),
                        pltpu.VMEM(pair, F32), pltpu.VMEM(pair, F32), pltpu.VMEM(pair, BF16), pltpu.VMEM(pair, BF16)],
        compiler_params=pltpu.CompilerParams(dimension_semantics=("arbitrary",),
                                             vmem_limit_bytes=_vmem_limit(24 << 20)),
    )(qkv2, qkv2, qkv2, qkv2, qkv2, flat(d_out), flat(stats))
    return [t.reshape(d, rows, aw) for t in outs]


def _attn_bwd_finish(dh, per_pattern, cos2, sin_bwd, aw):
    S, W = dh.shape
    n_heads = aw // HEAD_DIM
    n_pat = len(DILATIONS)

    def body(*refs):
        grad_refs = refs[1:1 + 3 * n_pat]
        cos_ref, sin_ref = refs[1 + 3 * n_pat], refs[2 + 3 * n_pat]
        out_ref = refs[3 + 3 * n_pat]
        perms = {d: _perm_matrix(d, False) for d in DILATIONS if d > 1}
        totals = []
        for which in range(3):
            tot = None
            for pi, d in enumerate(DILATIONS):
                g = grad_refs[which * n_pat + pi][...].reshape(PERM_ROWS, aw)
                g = _permute_rows(perms[d], g) if d > 1 else g.astype(F32)
                tot = g if tot is None else tot + g
            totals.append(tot)
        dq, dk, dv = totals
        c, s = cos_ref[...], sin_ref[...]
        for hd in range(n_heads):
            sl = slice(hd * HEAD_DIM, (hd + 1) * HEAD_DIM)
            out_ref[:, sl] = _rope_apply(dq[:, sl], c, s).astype(BF16)
            out_ref[:, aw + hd * HEAD_DIM:aw + (hd + 1) * HEAD_DIM] = _rope_apply(dk[:, sl], c, s).astype(BF16)
        out_ref[:, 2 * aw:] = dv.astype(BF16)

    grads = [pp[which] for which in range(3) for pp in per_pattern]
    rope_spec = pl.BlockSpec((PERM_ROWS, HEAD_DIM), lambda i: (i, 0))
    return pl.pallas_call(
        body, name="attn_bwd_finish", grid=(S // PERM_ROWS,),
        in_specs=([pl.BlockSpec(memory_space=pl.ANY)] + [_rm_block(d, aw) for d in DILATIONS] * 3
                  + [rope_spec, rope_spec]),
        out_specs=pl.BlockSpec((PERM_ROWS, 3 * aw), lambda i: (i, 0)),
        out_shape=jax.ShapeDtypeStruct((S, W), BF16),
        input_output_aliases={0: 0},
        compiler_params=pltpu.CompilerParams(dimension_semantics=("parallel",),
                                             vmem_limit_bytes=_vmem_limit(32 << 20)),
    )(dh, *grads, cos2, sin_bwd)


def _in_proj_bwd(dh, w_in, dz1, after=None):
    S, D = dz1.shape
    tm, tn = min(1024, S), min(2048, D)

    def epi(acc, ex, out, i, j):
        out[0][...] = DEEPNORM_ALPHA * ex[0][...] + acc

    (gx,) = _mm("in_proj_bwd", dh, w_in, "nt", (tm, tn, 512), [_tile_out((S, D), F32, tm, tn)], epi,
                extras=[(dz1, (tm, tn), lambda i, j, k: (i, j))], after=after)
    return gx


def _my_place():
    x, y, c = lax.axis_index("x"), lax.axis_index("y"), lax.axis_index("c")
    return x, y, c


def _flat(px, py, pc):
    return 4 * px + 2 * py + pc


def _shard_slice(ref, axis, idx, size):
    start = pl.multiple_of(idx * size, size)
    ix = [slice(None)] * len(ref.shape)
    ix[axis] = pl.ds(start, size)
    return ref.at[tuple(ix)]


def _all_gather_weights(shards, axes):
    n_w = len(shards)
    full_shapes = []
    for s, ax in zip(shards, axes):
        sh = list(s.shape)
        sh[ax] *= N_DEV
        full_shapes.append(tuple(sh))

    def body(*refs):
        src = refs[:n_w]
        dst = refs[n_w:2 * n_w]
        send_sems, recv_sems, local_sems = refs[2 * n_w:]
        x, y, c = _my_place()
        me, sibling = (x, y, c), (x, y, 1 - c)
        chips = [(1 - x, y), (x, 1 - y), (1 - x, 1 - y)]

        def place(w, dev):
            return _shard_slice(dst[w], axes[w], _flat(*dev), src[w].shape[axes[w]])

        def copy(w, k, block, to, from_src=False):
            return pltpu.make_async_remote_copy(
                src_ref=src[w] if from_src else place(w, block), dst_ref=place(w, block),
                send_sem=send_sems.at[w * 7 + k], recv_sem=recv_sems.at[w * 7 + k], device_id=to, device_id_type=MESH)

        mine, first, passed = [], [], []
        for w in range(n_w):
            cp = pltpu.make_async_copy(src[w], place(w, me), local_sems.at[w])
            cp.start()
            mine.append(cp)
            fw = [copy(w, 0, me, sibling, True)] + [copy(w, 1 + j, me, (*chip, c), True) for j, chip in enumerate(chips)]
            for cp in fw:
                cp.start()
            first += fw
        for w in range(n_w):
            for j, chip in enumerate(chips):
                copy(w, 1 + j, (*chip, c), me).wait_recv()
                cp = copy(w, 4 + j, (*chip, c), sibling)
                cp.start()
                passed.append(cp)
        for w in range(n_w):
            copy(w, 0, sibling, me).wait_recv()
            for j, chip in enumerate(chips):
                copy(w, 4 + j, (*chip, 1 - c), me).wait_recv()
        for cp in first + passed:
            cp.wait_send()
        for cp in mine:
            cp.wait()

    any_spec = pl.BlockSpec(memory_space=pl.ANY)
    return pl.pallas_call(
        body, name="all_gather_weights",
        in_specs=[any_spec] * n_w, out_specs=[any_spec] * n_w,
        out_shape=[jax.ShapeDtypeStruct(sh, s.dtype) for sh, s in zip(full_shapes, shards)],
        scratch_shapes=[pltpu.SemaphoreType.DMA((n_w * 7,)), pltpu.SemaphoreType.DMA((n_w * 7,)),
                        pltpu.SemaphoreType.DMA((n_w,))],
    )(*shards)


_HBM_SPEC = pl.BlockSpec(memory_space=pltpu.HBM)
_SEM_SPEC = pl.BlockSpec(memory_space=pltpu.SEMAPHORE)
_ANY_SPEC = pl.BlockSpec(memory_space=pl.ANY)
_N_PEER = N_DEV - 1


def _peer_of(x, y, c, r):
    return (x ^ ((r >> 2) & 1), y ^ ((r >> 1) & 1), c ^ (r & 1))


class _Exchange:
    def __init__(self, name, part, slot):
        self.name, self.part, self.slot = name, part, slot

    def _copy(self, w, r, src, land, send_sems, recv_sems, sending):
        x, y, c = _my_place()
        peer = _peer_of(x, y, c, r)
        return pltpu.make_async_remote_copy(
            src_ref=self.part(w, src, _flat(*peer)),
            dst_ref=self.slot(w, land, _flat(x, y, c) if sending else _flat(*peer)),
            send_sem=send_sems.at[w * _N_PEER + r - 1], recv_sem=recv_sems.at[w * _N_PEER + r - 1],
            device_id=peer, device_id_type=MESH)

    def start(self, srcs, lands, after):
        n = len(srcs)

        def body(*refs):
            src, land = refs[:n], refs[n:2 * n]
            send_sems, recv_sems = refs[2 * n + 1], refs[2 * n + 2]
            token = refs[2 * n + 3 + 2 * n]
            for w in range(n):
                for r in range(1, N_DEV):
                    self._copy(w, r, src[w], land[w], send_sems, recv_sems, True).start()
            token[...] = jnp.zeros_like(token)

        sems = pltpu.SemaphoreType.DMA((n * _N_PEER,))
        outs = pl.pallas_call(
            body, name=self.name + "_start",
            out_shape=(sems, sems, *[pltpu.HBM(t.shape, t.dtype) for t in list(srcs) + list(lands)],
                       jax.ShapeDtypeStruct((8, 128), F32)),
            in_specs=[_HBM_SPEC] * (2 * n) + [_ANY_SPEC],
            out_specs=(_SEM_SPEC, _SEM_SPEC, *[_HBM_SPEC] * (2 * n), pl.BlockSpec(memory_space=pltpu.VMEM)),
            input_output_aliases={i: 2 + i for i in range(2 * n)},
            compiler_params=pltpu.CompilerParams(has_side_effects=pltpu.SideEffectType.DATAFLOW_SIDE_EFFECTING),
        )(*[pltpu.with_memory_space_constraint(t, pltpu.HBM) for t in list(srcs) + list(lands)], after)
        return outs[0], outs[1], outs[2:2 + n], outs[2 + n:2 + 2 * n], outs[2 + 2 * n]

    def wait(self, started, after):
        send_sems, recv_sems, srcs, lands, _ = started
        n = len(srcs)

        def body(*refs):
            src, land = refs[:n], refs[n:2 * n]
            s_sems, r_sems = refs[2 * n], refs[2 * n + 1]
            for w in range(n):
                for r in range(1, N_DEV):
                    cp = self._copy(w, r, src[w], land[w], s_sems, r_sems, False)
                    cp.wait_send()
                    cp.wait_recv()

        outs = pl.pallas_call(
            body, name=self.name + "_wait",
            out_shape=[pltpu.HBM(t.shape, t.dtype) for t in list(srcs) + list(lands)],
            in_specs=[_HBM_SPEC] * (2 * n) + [_SEM_SPEC, _SEM_SPEC, _ANY_SPEC],
            out_specs=[_HBM_SPEC] * (2 * n),
            input_output_aliases={i: i for i in range(2 * n)},
            compiler_params=pltpu.CompilerParams(has_side_effects=pltpu.SideEffectType.DATAFLOW_SIDE_EFFECTING),
        )(*srcs, *lands, send_sems, recv_sems, after)
        return outs[n:]


def _gather_exchange(name, axes, shard_sizes):
    return _Exchange(name, lambda w, src, dev: src,
                     lambda w, land, dev: _shard_slice(land, axes[w], dev, shard_sizes[w]))


def _scatter_exchange(name, axes, shard_sizes):
    def part(w, src, dev):
        return src if axes[w] is None else _shard_slice(src, axes[w], dev, shard_sizes[w])
    return _Exchange(name, part, lambda w, land, dev: land.at[dev])


def _adamw(name, partials, w, m, v):
    R, C = w.shape
    tr = R
    while tr * C * 4 > (1 << 20) and tr % 16 == 0:
        tr //= 2

    def body(p_ref, w_ref, m_ref, v_ref, g_ref, d_ref, nm_ref, nv_ref):
        g = p_ref[0].astype(F32)
        for jdev in range(1, N_DEV):
            g = g + p_ref[jdev].astype(F32)
        nm = ADAM_B1 * m_ref[...] + (1.0 - ADAM_B1) * g
        nv = ADAM_B2 * v_ref[...] + (1.0 - ADAM_B2) * (g * g)
        m_hat = nm / (1.0 - ADAM_B1 ** ADAM_STEP)
        v_hat = nv / (1.0 - ADAM_B2 ** ADAM_STEP)
        g_ref[...] = g
        d_ref[...] = -ADAM_LR * (m_hat / (jnp.sqrt(v_hat) + ADAM_EPS) + ADAM_WD * w_ref[...])
        nm_ref[...] = nm
        nv_ref[...] = nv

    spec = pl.BlockSpec((tr, C), lambda i: (i, 0))
    return pl.pallas_call(
        body, name=name, grid=(R // tr,),
        in_specs=[pl.BlockSpec((N_DEV, tr, C), lambda i: (0, i, 0)), spec, spec, spec],
        out_specs=[spec] * 4,
        out_shape=[jax.ShapeDtypeStruct((R, C), F32)] * 4,
        compiler_params=pltpu.CompilerParams(dimension_semantics=("parallel",),
                                             vmem_limit_bytes=_vmem_limit(24 << 20)),
    )(partials, w, m, v)


def _local_step(x, cos2, sin_fwd, sin_bwd, w_in, mix_weights, ffn_weights, pool_scale, g_mix, b_mix, g_ff, b_ff,
                target, send, start_after=None):
    S, D = x.shape
    aw = pw = D // 2
    u_col_block = 3
    gate_col0 = 4 * aw

    xb = x.astype(BF16)
    h = _in_proj(xb, w_in, cos2, sin_fwd, 2 * aw, after=start_after)
    dilated = [d for d in DILATIONS if d > 1]
    qkv = {1: h[None], **dict(zip(dilated, _to_residue_major("qkv_to_rm", h, 0, 3 * aw)))}
    fwd = [_attn_fwd(qkv[d], d, aw) for d in DILATIONS]
    o_attn, l_tot = _attn_combine([f[0] for f in fwd], [f[1] for f in fwd], aw)
    w_pool, w_ba, w_bp, w_out = mix_weights(o_attn)
    p, y_pre, pm = _pool_fwd(h, w_pool, pool_scale, pw, u_col_block)
    y_attn = _branch_attn(o_attn, w_ba)
    y_pool, merged = _branch_pool_merge(pm, w_bp, h, y_attn, gate_col0)
    w1, w2 = ffn_weights(merged)
    x1, x1b, xhat1, rstd1 = _out_proj_ln(merged, w_out, x, g_mix, b_mix)
    a = _ffn_up(x1b, w1)
    dz2, dz2b, dg_ff, db_ff, loss = _ffn_down_loss(a, w2, x1, g_ff, b_ff, target)

    tok = send("ff2", [_grad_weight("grad_w_ff2", a, dz2b)])
    dh1 = _ffn_down_bwd(dz2b, w2, a, after=tok)
    tok = send("ff1", [_grad_weight("grad_w_ff1", x1b, dh1)])
    dz1, dz1b, dg_mix, db_mix = _ffn_up_bwd_ln(dh1, w1, dz2, xhat1, rstd1, g_mix, after=tok)
    tok = send("out", [_grad_weight("grad_w_out", merged, dz1b)])
    dy_attn, dy_pool, dh = _out_proj_bwd(dz1b, w_out, h, y_attn, y_pool, gate_col0, after=tok)
    tok = send("branch", [_grad_weight("grad_w_branch_attn", o_attn, dy_attn),
                          _grad_weight("grad_w_branch_pool", pm, dy_pool)])
    d_out, stats = _branch_attn_bwd(dy_attn, w_ba, o_attn, l_tot, after=tok)
    dy_pre, d_scale = _branch_pool_bwd(dy_pool, w_bp, y_pre, pool_scale)
    dh, dw_pool = _pool_bwd(dh, dy_pre, p, w_pool, pw, u_col_block)
    d_outs = {1: d_out[None], **dict(zip(dilated, _to_residue_major("dout_to_rm", d_out, 0, aw)))}
    statss = {1: stats[None], **dict(zip(dilated, _to_residue_major("stats_to_rm", stats, 0, HEAD_DIM)))}
    per_pattern = [_attn_bwd(qkv[d], d_outs[d], statss[d], d, aw) for d in DILATIONS]
    dh = _attn_bwd_finish(dh, per_pattern, cos2, sin_bwd, aw)
    small = jnp.concatenate((d_scale, dg_mix, db_mix, dg_ff, db_ff), axis=-1)
    tok = send("in", [_grad_weight("grad_w_in", xb, dh), dw_pool.astype(BF16),
                      small.reshape(small.shape[-1] // HEAD_DIM, HEAD_DIM)])
    grad_x = _in_proj_bwd(dh, w_in, dz1, after=tok)
    return loss, grad_x


def _rope_tables(positions):
    half = HEAD_DIM // 2
    inv_freq = ROPE_THETA ** (-jnp.arange(half, dtype=F32) / half)
    ang = positions.astype(F32)[:, None] * inv_freq
    cos, sin = jnp.cos(ang), jnp.sin(ang)
    cos2 = jnp.concatenate([cos, cos], axis=-1)
    sin_fwd = jnp.concatenate([-sin, sin], axis=-1)
    return cos2, sin_fwd, -sin_fwd


def kernel(x, positions, w_in, w_pool, pool_scale, w_branch_attn, w_branch_pool, w_out, ln_mix_g, ln_mix_b, w_ff1, w_ff2, ln_ff_g, ln_ff_b, loss_target, m_w_in, m_w_pool, m_pool_scale, m_w_branch_attn, m_w_branch_pool, m_w_out, m_ln_mix_g, m_ln_mix_b, m_w_ff1, m_w_ff2, m_ln_ff_g, m_ln_ff_b, v_w_in, v_w_pool, v_pool_scale, v_w_branch_attn, v_w_branch_pool, v_w_out, v_ln_mix_g, v_ln_mix_b, v_w_ff1, v_w_ff2, v_ln_ff_g, v_ln_ff_b):
    big_w = (w_in[0], w_pool[0], w_branch_attn[0], w_branch_pool[0], w_out[0], w_ff1[0], w_ff2[0])
    big_m = (m_w_in[0], m_w_pool[0], m_w_branch_attn[0], m_w_branch_pool[0], m_w_out[0], m_w_ff1[0], m_w_ff2[0])
    big_v = (v_w_in[0], v_w_pool[0], v_w_branch_attn[0], v_w_branch_pool[0], v_w_out[0], v_w_ff1[0], v_w_ff2[0])
    shard_axes = (1, 1, 1, 1, 0, 1, 0)
    small_w = (pool_scale, ln_mix_g, ln_mix_b, ln_ff_g, ln_ff_b)
    small_m = (m_pool_scale, m_ln_mix_g, m_ln_mix_b, m_ln_ff_g, m_ln_ff_b)
    small_v = (v_pool_scale, v_ln_mix_g, v_ln_mix_b, v_ln_ff_g, v_ln_ff_b)

    me = 4 * lax.axis_index("x") + 2 * lax.axis_index("y") + lax.axis_index("c")
    names = ("w_in", "w_pool", "w_branch_attn", "w_branch_pool", "w_out", "w_ff1", "w_ff2")
    axis_of = dict(zip(names, shard_axes))
    shard_of = dict(zip(names, [w.astype(BF16) for w in big_w]))

    def own_place(n):
        s, ax = shard_of[n], axis_of[n]
        full = list(s.shape)
        full[ax] *= N_DEV
        return lax.dynamic_update_slice_in_dim(lax.empty(tuple(full), s.dtype), s, me * s.shape[ax], ax)

    def gather_group(tag, group, after):
        ex = _gather_exchange(tag, [axis_of[n] for n in group], [shard_of[n].shape[axis_of[n]] for n in group])
        return ex, ex.start([shard_of[n] for n in group], [own_place(n) for n in group], after)

    (gw_in,) = _all_gather_weights([shard_of["w_in"]], (axis_of["w_in"],))
    mix_ex, mix_started = gather_group("gather_mix", ("w_pool", "w_branch_attn", "w_branch_pool", "w_out"), gw_in)
    ffn_ex, ffn_started = gather_group("gather_ffn", ("w_ff1", "w_ff2"), mix_started[4])

    groups = {"ff2": ("w_ff2",), "ff1": ("w_ff1",), "out": ("w_out",),
              "branch": ("w_branch_attn", "w_branch_pool"), "in": ("w_in", "w_pool", "small")}
    sent = {}

    def send(key, grads_):
        axes = [axis_of.get(n) for n in groups[key]]
        sizes = [None if ax is None else g.shape[ax] // N_DEV for g, ax in zip(grads_, axes)]
        lands = []
        for g, ax, size in zip(grads_, axes, sizes):
            own = g if ax is None else lax.dynamic_slice_in_dim(g, me * size, size, ax)
            lands.append(lax.dynamic_update_slice_in_dim(lax.empty((N_DEV,) + own.shape, g.dtype), own[None], me, 0))
        ex = _scatter_exchange("scatter_" + key, axes, sizes)
        sent[key] = (ex, ex.start(list(grads_), lands, grads_[0]))
        return sent[key][1][4]

    cos2, sin_fwd, sin_bwd = _rope_tables(positions[0])
    loss, grad_x = _local_step(
        x[0], cos2, sin_fwd, sin_bwd, gw_in, lambda after: mix_ex.wait(mix_started, after),
        lambda after: ffn_ex.wait(ffn_started, after), pool_scale, ln_mix_g, ln_mix_b, ln_ff_g, ln_ff_b,
        loss_target[0], send, start_after=ffn_started[4])

    state = dict(zip(names, zip(big_w, big_m, big_v)))
    n_small = sum(w.shape[-1] for w in small_w)
    small_2d = (n_small // HEAD_DIM, HEAD_DIM)
    state["small"] = tuple(jnp.concatenate(t, axis=-1).reshape(small_2d) for t in (small_w, small_m, small_v))
    grads, deltas, new_ms, new_vs = {}, {}, {}, {}
    after = grad_x
    for key in ("ff2", "ff1", "out", "branch", "in"):
        ex, started = sent[key]
        for n, part in zip(groups[key], ex.wait(started, after)):
            w, m, v = state[n]
            r2 = (-1, w.shape[-1])
            w2d = w.reshape(r2)
            res = _adamw("adamw_" + n, part.reshape((N_DEV,) + w2d.shape), w2d, m.reshape(r2), v.reshape(r2))
            after = res[0]
            if n == "small":
                small_out = [t.reshape(1, n_small) for t in res]
            else:
                grads[n], deltas[n], new_ms[n], new_vs[n] = (t.reshape((1,) + w.shape) for t in res)
    small_names = ("pool_scale", "ln_mix_g", "ln_mix_b", "ln_ff_g", "ln_ff_b")
    off = 0
    for n, w in zip(small_names, small_w):
        width = w.shape[-1]
        grads[n], deltas[n], new_ms[n], new_vs[n] = (t[:, off:off + width] for t in small_out)
        off += width

    order = ("w_in", "w_pool", "pool_scale", "w_branch_attn", "w_branch_pool", "w_out", "ln_mix_g", "ln_mix_b",
             "w_ff1", "w_ff2", "ln_ff_g", "ln_ff_b")
    total_loss = lax.psum(loss[0, 0], ("x", "y", "c"))
    return (total_loss, grad_x[None], *[grads[n] for n in order], *[deltas[n] for n in order],
            *[new_ms[n] for n in order], *[new_vs[n] for n in order])
```

```python
import functools

import jax
import jax.numpy as jnp
from jax import lax
from jax.experimental import pallas as pl
from jax.experimental.pallas import tpu as pltpu

F32 = jnp.float32
BF16 = jnp.bfloat16

N_DEV = 8
HEAD_DIM = 128
SUB_BLOCK = 128
DILATIONS = (1, 4, 16)
POOL_WINDOWS = (2, 4, 8, 16)
MAX_POOL_WINDOW = 16
POOL_HALO = 128
PERM_ROWS = 512
LN_EPS = 1e-5
DEEPNORM_ALPHA = 2.0 ** 0.25
ROPE_THETA = 10000.0
ATTN_SCALE = HEAD_DIM ** -0.5
ADAM_LR, ADAM_B1, ADAM_B2, ADAM_EPS, ADAM_WD, ADAM_STEP = 0.001, 0.9, 0.999, 1e-08, 0.01, 10
NEG_BIG = -1e30
VMEM_CAP_V7X = 64 * 1024 * 1024
MESH = pl.DeviceIdType.MESH


def _vmem_limit(est_bytes):
    return int(min(max(est_bytes * 5 // 4 + (4 << 20), 16 << 20), VMEM_CAP_V7X - (6 << 20)))


def _nbytes(shape, dtype):
    n = 1
    for s in shape:
        n *= s
    return n * jnp.dtype(dtype).itemsize


def _mm(name, a, b, form, tiles, outs, epi, extras=(), sequential=False, after=None):
    tm, tn, tk = tiles
    if form == "nn":
        (M, K), (K2, N) = a.shape, b.shape
    elif form == "nt":
        (M, K), (N, K2) = a.shape, b.shape
    else:
        (K, M), (K2, N) = a.shape, b.shape
    assert K == K2, (name, a.shape, b.shape)
    tm, tn, tk = min(tm, M), min(tn, N), min(tk, K)
    assert M % tm == 0 and N % tn == 0 and K % tk == 0, (name, M, N, K, tm, tn, tk)
    grid = (M // tm, N // tn, K // tk)
    nk = grid[2]
    if form == "nn":
        a_spec = pl.BlockSpec((tm, tk), lambda i, j, k: (i, k))
        b_spec = pl.BlockSpec((tk, tn), lambda i, j, k: (k, j))
        contract = ((1,), (0,))
    elif form == "nt":
        a_spec = pl.BlockSpec((tm, tk), lambda i, j, k: (i, k))
        b_spec = pl.BlockSpec((tn, tk), lambda i, j, k: (j, k))
        contract = ((1,), (1,))
    else:
        a_spec = pl.BlockSpec((tk, tm), lambda i, j, k: (k, i))
        b_spec = pl.BlockSpec((tk, tn), lambda i, j, k: (k, j))
        contract = ((0,), (0,))
    n_ex, n_out = len(extras), len(outs)
    n_after = 0 if after is None else 1

    def body(a_ref, b_ref, *rest):
        ex_refs = rest[:n_ex]
        rest = rest[n_ex + n_after:]
        out_refs = rest[:n_out]
        i, j, k = pl.program_id(0), pl.program_id(1), pl.program_id(2)

        def prod():
            return lax.dot_general(a_ref[...].astype(BF16), b_ref[...].astype(BF16),
                                   (contract, ((), ())), preferred_element_type=F32)

        if nk == 1:
            epi(prod(), ex_refs, out_refs, i, j)
        else:
            acc = rest[n_out]

            @pl.when(k == 0)
            def _():
                acc[...] = prod()

            @pl.when(k > 0)
            def _():
                acc[...] += prod()

            @pl.when(k == nk - 1)
            def _():
                epi(acc[...], ex_refs, out_refs, i, j)

    est = 2 * (_nbytes(a_spec.block_shape, a.dtype) + _nbytes(b_spec.block_shape, b.dtype))
    est += sum(2 * _nbytes(bs, arr.dtype) for arr, bs, _ in extras)
    est += sum(2 * _nbytes(bs, dt) for _, dt, bs, _ in outs)
    est += 4 * tm * tn * 4
    sem = ("arbitrary",) * 3 if sequential else ("parallel", "parallel", "arbitrary")
    return pl.pallas_call(
        body, name=name, grid=grid,
        in_specs=([a_spec, b_spec] + [pl.BlockSpec(bs, im) for _, bs, im in extras]
                  + [pl.BlockSpec(memory_space=pl.ANY)] * n_after),
        out_specs=[pl.BlockSpec(bs, im) for _, _, bs, im in outs],
        out_shape=[jax.ShapeDtypeStruct(sh, dt) for sh, dt, _, _ in outs],
        scratch_shapes=[pltpu.VMEM((tm, tn), F32)] if nk > 1 else [],
        compiler_params=pltpu.CompilerParams(dimension_semantics=sem, vmem_limit_bytes=_vmem_limit(est)),
    )(a, b, *[arr for arr, _, _ in extras], *([after] if n_after else []))


def _tile_out(shape, dtype, tm, tn):
    return (shape, dtype, (tm, tn), lambda i, j, k: (i, j))


def _row_sum_out(width):
    return ((1, width), F32, (1, width), lambda i, j, k: (0, 0))


def _accumulate_rows(ref, value, i):
    @pl.when(i == 0)
    def _():
        ref[...] = value

    @pl.when(i > 0)
    def _():
        ref[...] += value


def _layer_norm_bwd(dy, xhat, rstd, g):
    dxh = dy * g
    m1 = jnp.mean(dxh, axis=-1, keepdims=True)
    m2 = jnp.mean(dxh * xhat, axis=-1, keepdims=True)
    return rstd * (dxh - m1 - xhat * m2)


def _rope_apply(t, cos2, sin_signed):
    return t * cos2 + pltpu.roll(t, HEAD_DIM // 2, axis=1) * sin_signed


def _in_proj(xb, w_in, cos2, sin_fwd, n_rope_cols, after=None):
    S, D = xb.shape
    W = w_in.shape[1]
    tm, tn = min(1024, S), min(2048, n_rope_cols)
    assert n_rope_cols % tn == 0
    n_rope_tiles = n_rope_cols // tn

    def epi(acc, ex, out, i, j):
        cos_ref, sin_ref = ex
        (h_ref,) = out

        @pl.when(j < n_rope_tiles)
        def _():
            c, s = cos_ref[...], sin_ref[...]
            for hd in range(tn // HEAD_DIM):
                sl = slice(hd * HEAD_DIM, (hd + 1) * HEAD_DIM)
                h_ref[:, sl] = _rope_apply(acc[:, sl], c, s).astype(BF16)

        @pl.when(j >= n_rope_tiles)
        def _():
            h_ref[...] = acc.astype(BF16)

    row = lambda i, j, k: (i, 0)
    (h,) = _mm("in_proj", xb, w_in, "nn", (tm, tn, 512),
               [_tile_out((S, W), BF16, tm, tn)], epi,
               extras=[(cos2, (tm, HEAD_DIM), row), (sin_fwd, (tm, HEAD_DIM), row)], after=after)
    return h


def _attn_mask(mb):
    qi = lax.broadcasted_iota(jnp.int32, (SUB_BLOCK, 2 * SUB_BLOCK), 0)
    kj = lax.broadcasted_iota(jnp.int32, (SUB_BLOCK, 2 * SUB_BLOCK), 1)
    prev = jnp.logical_and(jnp.logical_and(kj < SUB_BLOCK, kj >= qi), mb > 0)
    cur = jnp.logical_and(kj >= SUB_BLOCK, kj - SUB_BLOCK <= qi)
    return jnp.logical_or(prev, cur)


def _dot_nt(a, b):
    return lax.dot_general(a, b, (((1,), (1,)), ((), ())), preferred_element_type=F32)


def _dot_tn(a, b):
    return lax.dot_general(a, b, (((0,), (0,)), ((), ())), preferred_element_type=F32)


def _dot_nn(a, b):
    return lax.dot_general(a, b, (((1,), (0,)), ((), ())), preferred_element_type=F32)


def _perm_matrix(d, to_residue_major):
    g = PERM_ROWS // d
    i = lax.broadcasted_iota(jnp.int32, (PERM_ROWS, PERM_ROWS), 0)
    j = lax.broadcasted_iota(jnp.int32, (PERM_ROWS, PERM_ROWS), 1)
    if to_residue_major:
        hit = j == (i % g) * d + i // g
    else:
        hit = j == (i % d) * g + i // d
    return hit.astype(BF16)


def _permute_rows(perm, x):
    if x.dtype == BF16:
        return _dot_nn(perm, x)
    hi = x.astype(BF16)
    r1 = x - hi.astype(F32)
    mid = r1.astype(BF16)
    lo = (r1 - mid.astype(F32)).astype(BF16)
    return (_dot_nn(perm, hi) + _dot_nn(perm, mid)) + _dot_nn(perm, lo)


def _rm_block(d, width):
    return pl.BlockSpec((d, PERM_ROWS // d, width), lambda i: (0, i, 0))


def _to_residue_major(name, x, col_block, width):
    S = x.shape[0]
    dils = [d for d in DILATIONS if d > 1]
    chunk = min(width, 1024)

    def body(x_ref, *out_refs):
        for d, o_ref in zip(dils, out_refs):
            perm = _perm_matrix(d, True)
            for c0 in range(0, width, chunk):
                cw = min(chunk, width - c0)
                y = _permute_rows(perm, x_ref[:, c0:c0 + cw])
                o_ref[:, :, c0:c0 + cw] = y.astype(x.dtype).reshape(d, PERM_ROWS // d, cw)

    return pl.pallas_call(
        body, name=name, grid=(S // PERM_ROWS,),
        in_specs=[pl.BlockSpec((PERM_ROWS, width), lambda i: (i, col_block))],
        out_specs=[_rm_block(d, width) for d in dils],
        out_shape=[jax.ShapeDtypeStruct((d, S // d, width), x.dtype) for d in dils],
        compiler_params=pltpu.CompilerParams(dimension_semantics=("parallel",),
                                             vmem_limit_bytes=_vmem_limit(32 << 20)),
    )(x)


def _qkv_specs(aw):
    def spec(col, prev):
        if prev:
            return pl.BlockSpec((None, SUB_BLOCK, aw), lambda r, mb: (r, jnp.maximum(mb - 1, 0), col))
        return pl.BlockSpec((None, SUB_BLOCK, aw), lambda r, mb: (r, mb, col))
    return [spec(0, False), spec(1, True), spec(1, False), spec(2, True), spec(2, False)]


def _put_column(tile, col, value):
    lane = lax.broadcasted_iota(jnp.int32, tile.shape, 1)
    return jnp.where(lane == col, value, tile)


def _attn_fwd(qkv, d, aw):
    _, rows, _ = qkv.shape
    n_heads = aw // HEAD_DIM
    nb = rows // SUB_BLOCK

    def body(q_ref, kp_ref, kc_ref, vp_ref, vc_ref, o_ref, lse_ref, s_buf, p_buf):
        mask = _attn_mask(pl.program_id(1))
        for hd in range(n_heads):
            sl = slice(hd * HEAD_DIM, (hd + 1) * HEAD_DIM)
            q = q_ref[:, sl]
            s_buf[hd, :, :SUB_BLOCK] = _dot_nt(q, kp_ref[:, sl])
            s_buf[hd, :, SUB_BLOCK:] = _dot_nt(q, kc_ref[:, sl])
        lse_tile = jnp.zeros((SUB_BLOCK, HEAD_DIM), F32)
        inv_tile = jnp.zeros((SUB_BLOCK, HEAD_DIM), F32)
        for hd in range(n_heads):
            s = jnp.where(mask, s_buf[hd] * ATTN_SCALE, NEG_BIG)
            m = jnp.max(s, axis=-1, keepdims=True)
            p = jnp.exp(s - m)
            l = jnp.sum(p, axis=-1, keepdims=True)
            p_buf[hd] = p.astype(BF16)
            lse_tile = _put_column(lse_tile, hd, m + jnp.log(l))
            inv_tile = _put_column(inv_tile, hd, 1.0 / l)
        lse_ref[...] = lse_tile
        for hd in range(n_heads):
            sl = slice(hd * HEAD_DIM, (hd + 1) * HEAD_DIM)
            o = _dot_nn(p_buf[hd, :, :SUB_BLOCK], vp_ref[:, sl]) + _dot_nn(p_buf[hd, :, SUB_BLOCK:], vc_ref[:, sl])
            o_ref[:, sl] = o * inv_tile[:, hd:hd + 1]

    return pl.pallas_call(
        body, name=f"attn_fwd_d{d}", grid=(d, nb),
        in_specs=_qkv_specs(aw),
        out_specs=[pl.BlockSpec((None, SUB_BLOCK, aw), lambda r, mb: (r, mb, 0)),
                   pl.BlockSpec((None, SUB_BLOCK, HEAD_DIM), lambda r, mb: (r, mb, 0))],
        out_shape=[jax.ShapeDtypeStruct((d, rows, aw), F32), jax.ShapeDtypeStruct((d, rows, HEAD_DIM), F32)],
        scratch_shapes=[pltpu.VMEM((n_heads, SUB_BLOCK, 2 * SUB_BLOCK), F32),
                        pltpu.VMEM((n_heads, SUB_BLOCK, 2 * SUB_BLOCK), BF16)],
        compiler_params=pltpu.CompilerParams(dimension_semantics=("parallel", "parallel"),
                                             vmem_limit_bytes=_vmem_limit(16 << 20)),
    )(qkv, qkv, qkv, qkv, qkv)


def _attn_combine(outs, lses, aw):
    S = outs[0].shape[1]
    n_heads = aw // HEAD_DIM
    n_pat = len(DILATIONS)

    def body(*refs):
        o_refs, l_refs = refs[:n_pat], refs[n_pat:2 * n_pat]
        o_ref, lt_ref = refs[2 * n_pat], refs[2 * n_pat + 1]
        o_nat, l_nat = [], []
        for d, o_r, l_r in zip(DILATIONS, o_refs, l_refs):
            o_p = o_r[...].reshape(PERM_ROWS, aw)
            l_p = l_r[...].reshape(PERM_ROWS, HEAD_DIM)
            if d > 1:
                perm = _perm_matrix(d, False)
                o_p, l_p = _permute_rows(perm, o_p), _permute_rows(perm, l_p)
            o_nat.append(o_p)
            l_nat.append(l_p)
        mx = functools.reduce(jnp.maximum, l_nat)
        es = [jnp.exp(l_p - mx) for l_p in l_nat]
        den = functools.reduce(jnp.add, es)
        lt_ref[...] = mx + jnp.log(den)
        ws = [e / den for e in es]
        for hd in range(n_heads):
            sl = slice(hd * HEAD_DIM, (hd + 1) * HEAD_DIM)
            o = ws[0][:, hd:hd + 1] * o_nat[0][:, sl]
            for pi in range(1, n_pat):
                o = o + ws[pi][:, hd:hd + 1] * o_nat[pi][:, sl]
            o_ref[:, sl] = o.astype(BF16)

    return pl.pallas_call(
        body, name="attn_combine", grid=(S // PERM_ROWS,),
        in_specs=[_rm_block(d, aw) for d in DILATIONS] + [_rm_block(d, HEAD_DIM) for d in DILATIONS],
        out_specs=[pl.BlockSpec((PERM_ROWS, aw), lambda i: (i, 0)), pl.BlockSpec((PERM_ROWS, HEAD_DIM), lambda i: (i, 0))],
        out_shape=[jax.ShapeDtypeStruct((S, aw), BF16), jax.ShapeDtypeStruct((S, HEAD_DIM), F32)],
        compiler_params=pltpu.CompilerParams(dimension_semantics=("parallel",),
                                             vmem_limit_bytes=_vmem_limit(40 << 20)),
    )(*outs, *lses)


def _band(tm, width, w, row_offset, transpose):
    t = lax.broadcasted_iota(jnp.int32, (tm, width), 0)
    u = lax.broadcasted_iota(jnp.int32, (tm, width), 1)
    dist = (u - t - row_offset) if transpose else (t + row_offset - u)
    return jnp.logical_and(dist >= 0, dist < w).astype(BF16)


def _pool_fwd(h, w_pool, pool_scale, pw, u_col_block):
    S, W = h.shape
    n_groups = len(POOL_WINDOWS)
    gw = pw // n_groups
    tm = min(512, S)
    halo_per_tile = tm // POOL_HALO

    def body(uc_ref, uh_ref, w_ref, sc_ref, p_ref, y_ref, pm_ref):
        i = pl.program_id(0)
        t_abs = i * tm + lax.broadcasted_iota(jnp.int32, (tm, 1), 0)
        for g, w in enumerate(POOL_WINDOWS):
            sl = slice(g * gw, (g + 1) * gw)
            uc = uc_ref[:, sl]
            uh = jnp.where(i > 0, uh_ref[:, sl], jnp.zeros((POOL_HALO, gw), BF16))
            ssum = _dot_nn(_band(tm, tm, w, 0, False), uc) + _dot_nn(_band(tm, POOL_HALO, w, POOL_HALO, False), uh)
            cnt = jnp.minimum(t_abs + 1, w).astype(F32)
            p = (ssum / cnt - uc.astype(F32)).astype(BF16)
            y = _dot_nn(p, w_ref[g])
            p_ref[:, sl] = p
            y_ref[:, sl] = y.astype(BF16)
            pm_ref[:, sl] = (y * sc_ref[:, sl]).astype(BF16)

    row = pl.BlockSpec((tm, pw), lambda i: (i, 0))
    return pl.pallas_call(
        body, name="pool_fwd", grid=(S // tm,),
        in_specs=[pl.BlockSpec((tm, pw), lambda i: (i, u_col_block)),
                  pl.BlockSpec((POOL_HALO, pw), lambda i: (jnp.maximum(i * halo_per_tile - 1, 0), u_col_block)),
                  pl.BlockSpec((n_groups, gw, gw), lambda i: (0, 0, 0)),
                  pl.BlockSpec((1, pw), lambda i: (0, 0))],
        out_specs=[row, row, row],
        out_shape=[jax.ShapeDtypeStruct((S, pw), BF16)] * 3,
        compiler_params=pltpu.CompilerParams(dimension_semantics=("parallel",),
                                             vmem_limit_bytes=_vmem_limit(24 << 20)),
    )(h, h, w_pool, pool_scale)


def _branch_attn(o_attn, w_ba):
    S, _ = o_attn.shape
    D = w_ba.shape[1]
    tm, tn = min(1024, S), D

    def epi(acc, ex, out, i, j):
        out[0][...] = acc.astype(BF16)

    (y,) = _mm("branch_attn", o_attn, w_ba, "nn", (tm, tn, 1024), [_tile_out((S, D), BF16, tm, tn)], epi)
    return y


def _branch_pool_merge(pm, w_bp, h, y_attn, gate_col0):
    S, _ = pm.shape
    D = w_bp.shape[1]
    tm, tn = min(512, S), D
    ga0, gp0 = gate_col0 // tn, (gate_col0 + D) // tn

    def epi(acc, ex, out, i, j):
        ga_ref, gp_ref, ya_ref = ex
        yp_ref, mg_ref = out
        yp = acc.astype(BF16)
        yp_ref[...] = yp
        mg = (jax.nn.sigmoid(ga_ref[...].astype(F32)) * ya_ref[...].astype(F32)
              + jax.nn.sigmoid(gp_ref[...].astype(F32)) * acc)
        mg_ref[...] = mg.astype(BF16)

    y_pool, merged = _mm(
        "branch_pool_merge", pm, w_bp, "nn", (tm, tn, 1024),
        [_tile_out((S, D), BF16, tm, tn), _tile_out((S, D), BF16, tm, tn)], epi,
        extras=[(h, (tm, tn), lambda i, j, k: (i, ga0 + j)), (h, (tm, tn), lambda i, j, k: (i, gp0 + j)),
                (y_attn, (tm, tn), lambda i, j, k: (i, j))])
    return y_pool, merged


def _layer_norm_rows(z, g, b):
    mu = jnp.mean(z, axis=-1, keepdims=True)
    zc = z - mu
    var = jnp.mean(zc * zc, axis=-1, keepdims=True)
    rstd = lax.rsqrt(var + LN_EPS)
    xhat = zc * rstd
    return xhat * g + b, xhat, rstd


def _out_proj_ln(merged, w_out, x, g, b):
    S, D = x.shape
    tm = min(256, S)

    def epi(acc, ex, out, i, j):
        x_ref, g_ref, b_ref = ex
        x1_ref, x1b_ref, xh_ref, rs_ref = out
        y, xhat, rstd = _layer_norm_rows(DEEPNORM_ALPHA * x_ref[...] + acc, g_ref[...], b_ref[...])
        x1_ref[...] = y
        x1b_ref[...] = y.astype(BF16)
        xh_ref[...] = xhat
        rs_ref[...] = jnp.broadcast_to(rstd, (tm, HEAD_DIM))

    row = lambda i, j, k: (i, 0)
    vec = lambda i, j, k: (0, 0)
    return _mm("out_proj_ln", merged, w_out, "nn", (tm, D, D),
               [((S, D), F32, (tm, D), row), ((S, D), BF16, (tm, D), row), ((S, D), F32, (tm, D), row),
                ((S, HEAD_DIM), F32, (tm, HEAD_DIM), row)], epi,
               extras=[(x, (tm, D), row), (g, (1, D), vec), (b, (1, D), vec)])


def _ffn_up(x1b, w1):
    S, D = x1b.shape
    F = w1.shape[1]
    tm, tn = min(1024, S), min(2048, F)

    def epi(acc, ex, out, i, j):
        r = jnp.maximum(acc, 0.0)
        out[0][...] = (r * r).astype(BF16)

    (a,) = _mm("ffn_up", x1b, w1, "nn", (tm, tn, 512), [_tile_out((S, F), BF16, tm, tn)], epi)
    return a


def _ffn_down_loss(a, w2, x1, g, b, target):
    S, D = x1.shape
    tm = min(512, S)

    def epi(acc, ex, out, i, j):
        x1_ref, g_ref, b_ref, t_ref = ex
        dz_ref, dzb_ref, dg_ref, db_ref, loss_ref = out
        gv = g_ref[...]
        y, xhat, rstd = _layer_norm_rows(DEEPNORM_ALPHA * x1_ref[...] + acc, gv, b_ref[...])
        err = y - t_ref[...]
        loss = 0.5 * jnp.sum(jnp.mean(err * err, axis=-1, keepdims=True), axis=0, keepdims=True)
        dy = err * (1.0 / D)
        dz = _layer_norm_bwd(dy, xhat, rstd, gv)
        dz_ref[...] = dz
        dzb_ref[...] = dz.astype(BF16)
        _accumulate_rows(dg_ref, jnp.sum(dy * xhat, axis=0, keepdims=True), i)
        _accumulate_rows(db_ref, jnp.sum(dy, axis=0, keepdims=True), i)
        _accumulate_rows(loss_ref, jnp.broadcast_to(loss, (1, HEAD_DIM)), i)

    row = lambda i, j, k: (i, 0)
    vec = lambda i, j, k: (0, 0)
    return _mm("ffn_down_loss", a, w2, "nn", (tm, D, 512),
               [((S, D), F32, (tm, D), row), ((S, D), BF16, (tm, D), row),
                _row_sum_out(D), _row_sum_out(D), _row_sum_out(HEAD_DIM)], epi,
               extras=[(x1, (tm, D), row), (g, (1, D), vec), (b, (1, D), vec), (target, (tm, D), row)],
               sequential=True)


def _grad_weight(name, act, cot):
    M, N = act.shape[1], cot.shape[1]
    tm, tn = min(1024, M), min(2048, N)

    def epi(acc, ex, out, i, j):
        out[0][...] = acc.astype(BF16)

    (g,) = _mm(name, act, cot, "tn", (tm, tn, 512), [_tile_out((M, N), BF16, tm, tn)], epi)
    return g


def _ffn_down_bwd(dz2b, w2, a, after=None):
    S, D = dz2b.shape
    F = w2.shape[0]
    tm, tn = min(1024, S), min(2048, F)

    def epi(acc, ex, out, i, j):
        out[0][...] = (acc * (2.0 * jnp.sqrt(ex[0][...].astype(F32)))).astype(BF16)

    (dh1,) = _mm("ffn_down_bwd", dz2b, w2, "nt", (tm, tn, 512), [_tile_out((S, F), BF16, tm, tn)], epi,
                 extras=[(a, (tm, tn), lambda i, j, k: (i, j))], after=after)
    return dh1


def _ffn_up_bwd_ln(dh1, w1, dz2, xhat1, rstd1, g1, after=None):
    S, D = dz2.shape
    tm = min(512, S)

    def epi(acc, ex, out, i, j):
        dz2_ref, xh_ref, rs_ref, g_ref = ex
        dz_ref, dzb_ref, dg_ref, db_ref = out
        dy = DEEPNORM_ALPHA * dz2_ref[...] + acc
        xhat = xh_ref[...]
        dz = _layer_norm_bwd(dy, xhat, rs_ref[:, :1], g_ref[...])
        dz_ref[...] = dz
        dzb_ref[...] = dz.astype(BF16)
        _accumulate_rows(dg_ref, jnp.sum(dy * xhat, axis=0, keepdims=True), i)
        _accumulate_rows(db_ref, jnp.sum(dy, axis=0, keepdims=True), i)

    row = lambda i, j, k: (i, 0)
    vec = lambda i, j, k: (0, 0)
    return _mm("ffn_up_bwd_ln", dh1, w1, "nt", (tm, D, 512),
               [((S, D), F32, (tm, D), row), ((S, D), BF16, (tm, D), row), _row_sum_out(D), _row_sum_out(D)], epi,
               extras=[(dz2, (tm, D), row), (xhat1, (tm, D), row), (rstd1, (tm, HEAD_DIM), row), (g1, (1, D), vec)],
               sequential=True, after=after)


def _out_proj_bwd(dz1b, w_out, h, y_attn, y_pool, gate_col0, after=None):
    S, D = dz1b.shape
    W = h.shape[1]
    tm = min(256, S)
    assert gate_col0 == 2 * D and W == 4 * D

    def epi(acc, ex, out, i, j):
        gates_ref, ya_ref, yp_ref = ex
        dya_ref, dyp_ref, dh_ref = out
        sa = jax.nn.sigmoid(gates_ref[:, :D].astype(F32))
        sp = jax.nn.sigmoid(gates_ref[:, D:].astype(F32))
        dya_ref[...] = (acc * sa).astype(BF16)
        dyp_ref[...] = (acc * sp).astype(BF16)
        dh_ref[:, :D] = (acc * ya_ref[...].astype(F32) * (sa * (1.0 - sa))).astype(BF16)
        dh_ref[:, D:] = (acc * yp_ref[...].astype(F32) * (sp * (1.0 - sp))).astype(BF16)

    row = lambda i, j, k: (i, 0)
    return _mm("out_proj_bwd", dz1b, w_out, "nt", (tm, D, D),
               [((S, D), BF16, (tm, D), row), ((S, D), BF16, (tm, D), row),
                ((S, W), BF16, (tm, 2 * D), lambda i, j, k: (i, 1))], epi,
               extras=[(h, (tm, 2 * D), lambda i, j, k: (i, 1)), (y_attn, (tm, D), row), (y_pool, (tm, D), row)],
               after=after)


def _branch_attn_bwd(dy_attn, w_ba, o_attn, l_tot, after=None):
    S, D = dy_attn.shape
    aw = w_ba.shape[0]
    n_heads = aw // HEAD_DIM
    tm = min(512, S)

    def epi(acc, ex, out, i, j):
        do_ref, st_ref = out
        do_ref[...] = acc.astype(BF16)
        o = ex[0][...].astype(F32)
        stats = ex[1][...]
        for hd in range(n_heads):
            sl = slice(hd * HEAD_DIM, (hd + 1) * HEAD_DIM)
            stats = _put_column(stats, n_heads + hd, jnp.sum(acc[:, sl] * o[:, sl], axis=-1, keepdims=True))
        st_ref[...] = stats

    row = lambda i, j, k: (i, 0)
    return _mm("branch_attn_bwd", dy_attn, w_ba, "nt", (tm, aw, D),
               [((S, aw), BF16, (tm, aw), row), ((S, HEAD_DIM), F32, (tm, HEAD_DIM), row)], epi,
               extras=[(o_attn, (tm, aw), row), (l_tot, (tm, HEAD_DIM), row)], after=after)


def _branch_pool_bwd(dy_pool, w_bp, y_pre, pool_scale):
    S, D = dy_pool.shape
    pw = w_bp.shape[0]
    tm = min(512, S)

    def epi(acc, ex, out, i, j):
        y_ref, sc_ref = ex
        dyp_ref, dsc_ref = out
        dyp_ref[...] = (acc * sc_ref[...]).astype(BF16)
        _accumulate_rows(dsc_ref, jnp.sum(acc * y_ref[...].astype(F32), axis=0, keepdims=True), i)

    row = lambda i, j, k: (i, 0)
    return _mm("branch_pool_bwd", dy_pool, w_bp, "nt", (tm, pw, D),
               [((S, pw), BF16, (tm, pw), row), _row_sum_out(pw)], epi,
               extras=[(y_pre, (tm, pw), row), (pool_scale, (1, pw), lambda i, j, k: (0, 0))],
               sequential=True)


def _pool_bwd(dh, dy_pre, p, w_pool, pw, u_col_block):
    S, W = dh.shape
    n_groups = len(POOL_WINDOWS)
    gw = pw // n_groups
    tm = min(512, S)
    n_tiles = S // tm
    halo_per_tile = tm // POOL_HALO
    n_halo_blocks = S // POOL_HALO

    def body(dh_in_ref, dyc_ref, dyh_ref, p_ref, w_ref, dh_ref, dw_ref):
        del dh_in_ref
        i = pl.program_id(0)
        t_cur = i * tm + lax.broadcasted_iota(jnp.int32, (tm, 1), 0)
        t_halo = (i + 1) * tm + lax.broadcasted_iota(jnp.int32, (POOL_HALO, 1), 0)
        for g, w in enumerate(POOL_WINDOWS):
            sl = slice(g * gw, (g + 1) * gw)
            wg = w_ref[g]
            dyc = dyc_ref[:, sl]
            dyh = jnp.where(i < n_tiles - 1, dyh_ref[:, sl], jnp.zeros((POOL_HALO, gw), BF16))
            dp_cur = _dot_nt(dyc, wg)
            dp_halo = _dot_nt(dyh, wg)
            dpc_cur = (dp_cur / jnp.minimum(t_cur + 1, w).astype(F32)).astype(BF16)
            dpc_halo = (dp_halo / jnp.minimum(t_halo + 1, w).astype(F32)).astype(BF16)
            du = (_dot_nn(_band(tm, tm, w, 0, True), dpc_cur)
                  + _dot_nn(_band(tm, POOL_HALO, w, -tm, True), dpc_halo) - dp_cur)
            dh_ref[:, sl] = du.astype(BF16)
            dw = _dot_tn(p_ref[:, sl], dyc)

            @pl.when(i == 0)
            def _():
                dw_ref[g] = dw

            @pl.when(i > 0)
            def _():
                dw_ref[g] += dw

    row = pl.BlockSpec((tm, pw), lambda i: (i, 0))
    dh_new, dw_pool = pl.pallas_call(
        body, name="pool_bwd", grid=(n_tiles,),
        in_specs=[pl.BlockSpec(memory_space=pl.ANY), row,
                  pl.BlockSpec((POOL_HALO, pw), lambda i: (jnp.minimum((i + 1) * halo_per_tile, n_halo_blocks - 1), 0)),
                  row, pl.BlockSpec((n_groups, gw, gw), lambda i: (0, 0, 0))],
        out_specs=[pl.BlockSpec((tm, pw), lambda i: (i, u_col_block)),
                   pl.BlockSpec((n_groups, gw, gw), lambda i: (0, 0, 0))],
        out_shape=[jax.ShapeDtypeStruct((S, W), BF16), jax.ShapeDtypeStruct((n_groups, gw, gw), F32)],
        input_output_aliases={0: 0},
        compiler_params=pltpu.CompilerParams(dimension_semantics=("arbitrary",),
                                             vmem_limit_bytes=_vmem_limit(24 << 20)),
    )(dh, dy_pre, dy_pre, p, w_pool)
    return dh_new, dw_pool


def _attn_bwd(qkv, d_out, stats, d, aw):
    _, rows, _ = qkv.shape
    n_heads = aw // HEAD_DIM
    nb = rows // SUB_BLOCK
    n_blocks = d * nb

    def body(q_ref, kp_ref, kc_ref, vp_ref, vc_ref, do_ref, st_ref, dq_ref, dk_ref, dv_ref,
             carry_k, carry_v, s_buf, dp_buf, p_buf, ds_buf):
        step = pl.program_id(0)

        @pl.when(step == 0)
        def _():
            carry_k[...] = jnp.zeros_like(carry_k)
            carry_v[...] = jnp.zeros_like(carry_v)

        @pl.when(step < n_blocks)
        def _():
            mask = _attn_mask(step % nb)
            st = st_ref[...]
            lo, hi = slice(0, SUB_BLOCK), slice(SUB_BLOCK, 2 * SUB_BLOCK)
            for hd in range(n_heads):
                sl = slice(hd * HEAD_DIM, (hd + 1) * HEAD_DIM)
                q, do = q_ref[:, sl], do_ref[:, sl]
                s_buf[hd, :, lo] = _dot_nt(q, kp_ref[:, sl])
                s_buf[hd, :, hi] = _dot_nt(q, kc_ref[:, sl])
                dp_buf[hd, :, lo] = _dot_nt(do, vp_ref[:, sl])
                dp_buf[hd, :, hi] = _dot_nt(do, vc_ref[:, sl])
            for hd in range(n_heads):
                lt, dl = st[:, hd:hd + 1], st[:, n_heads + hd:n_heads + hd + 1]
                p = jnp.where(mask, jnp.exp(jnp.where(mask, s_buf[hd] * ATTN_SCALE - lt, NEG_BIG)), 0.0)
                p_buf[hd] = p.astype(BF16)
                ds_buf[hd] = (p * (dp_buf[hd] - dl) * ATTN_SCALE).astype(BF16)
            for hd in range(n_heads):
                sl = slice(hd * HEAD_DIM, (hd + 1) * HEAD_DIM)
                q, do = q_ref[:, sl], do_ref[:, sl]
                dq_ref[:, sl] = (_dot_nn(ds_buf[hd, :, lo], kp_ref[:, sl])
                                 + _dot_nn(ds_buf[hd, :, hi], kc_ref[:, sl])).astype(BF16)
                dk_ref[:, sl] = (carry_k[:, sl] + _dot_tn(ds_buf[hd, :, lo], q)).astype(BF16)
                dv_ref[:, sl] = (carry_v[:, sl] + _dot_tn(p_buf[hd, :, lo], do)).astype(BF16)
                carry_k[:, sl] = _dot_tn(ds_buf[hd, :, hi], q)
                carry_v[:, sl] = _dot_tn(p_buf[hd, :, hi], do)

        @pl.when(step == n_blocks)
        def _():
            dk_ref[...] = carry_k[...].astype(BF16)
            dv_ref[...] = carry_v[...].astype(BF16)

    def cur(step):
        return jnp.minimum(step, n_blocks - 1)

    def qkv_spec(col, prev):
        if prev:
            return pl.BlockSpec((SUB_BLOCK, aw), lambda s: (jnp.maximum(cur(s) - 1, 0), col))
        return pl.BlockSpec((SUB_BLOCK, aw), lambda s: (cur(s), col))

    def at_cur(w):
        return pl.BlockSpec((SUB_BLOCK, w), lambda s: (cur(s), 0))

    finished = pl.BlockSpec((SUB_BLOCK, aw), lambda s: (jnp.maximum(s - 1, 0), 0))
    pair = (n_heads, SUB_BLOCK, 2 * SUB_BLOCK)
    flat = lambda t: t.reshape(d * rows, t.shape[-1])
    qkv2 = flat(qkv)
    outs = pl.pallas_call(
        body, name=f"attn_bwd_d{d}", grid=(n_blocks + 1,),
        in_specs=[qkv_spec(0, False), qkv_spec(1, True), qkv_spec(1, False), qkv_spec(2, True), qkv_spec(2, False),
                  at_cur(aw), at_cur(HEAD_DIM)],
        out_specs=[at_cur(aw), finished, finished],
        out_shape=[jax.ShapeDtypeStruct((d * rows, aw), BF16)] * 3,
        scratch_shapes=[pltpu.VMEM((SUB_BLOCK, aw), F32), pltpu.VMEM((SUB_BLOCK, aw), F32),
                        pltpu.VMEM(pair, F32), pltpu.VMEM(pair, F32), pltpu.VMEM(pair, BF16), pltpu.VMEM(pair, BF16)],
        compiler_params=pltpu.CompilerParams(dimension_semantics=("arbitrary",),
                                             vmem_limit_bytes=_vmem_limit(24 << 20)),
    )(qkv2, qkv2, qkv2, qkv2, qkv2, flat(d_out), flat(stats))
    return [t.reshape(d, rows, aw) for t in outs]


def _attn_bwd_finish(dh, per_pattern, cos2, sin_bwd, aw):
    S, W = dh.shape
    n_heads = aw // HEAD_DIM
    n_pat = len(DILATIONS)

    def body(*refs):
        grad_refs = refs[1:1 + 3 * n_pat]
        cos_ref, sin_ref = refs[1 + 3 * n_pat], refs[2 + 3 * n_pat]
        out_ref = refs[3 + 3 * n_pat]
        perms = {d: _perm_matrix(d, False) for d in DILATIONS if d > 1}
        totals = []
        for which in range(3):
            tot = None
            for pi, d in enumerate(DILATIONS):
                g = grad_refs[which * n_pat + pi][...].reshape(PERM_ROWS, aw)
                g = _permute_rows(perms[d], g) if d > 1 else g.astype(F32)
                tot = g if tot is None else tot + g
            totals.append(tot)
        dq, dk, dv = totals
        c, s = cos_ref[...], sin_ref[...]
        for hd in range(n_heads):
            sl = slice(hd * HEAD_DIM, (hd + 1) * HEAD_DIM)
            out_ref[:, sl] = _rope_apply(dq[:, sl], c, s).astype(BF16)
            out_ref[:, aw + hd * HEAD_DIM:aw + (hd + 1) * HEAD_DIM] = _rope_apply(dk[:, sl], c, s).astype(BF16)
        out_ref[:, 2 * aw:] = dv.astype(BF16)

    grads = [pp[which] for which in range(3) for pp in per_pattern]
    rope_spec = pl.BlockSpec((PERM_ROWS, HEAD_DIM), lambda i: (i, 0))
    return pl.pallas_call(
        body, name="attn_bwd_finish", grid=(S // PERM_ROWS,),
        in_specs=([pl.BlockSpec(memory_space=pl.ANY)] + [_rm_block(d, aw) for d in DILATIONS] * 3
                  + [rope_spec, rope_spec]),
        out_specs=pl.BlockSpec((PERM_ROWS, 3 * aw), lambda i: (i, 0)),
        out_shape=jax.ShapeDtypeStruct((S, W), BF16),
        input_output_aliases={0: 0},
        compiler_params=pltpu.CompilerParams(dimension_semantics=("parallel",),
                                             vmem_limit_bytes=_vmem_limit(32 << 20)),
    )(dh, *grads, cos2, sin_bwd)


def _in_proj_bwd(dh, w_in, dz1, after=None):
    S, D = dz1.shape
    tm, tn = min(1024, S), min(2048, D)

    def epi(acc, ex, out, i, j):
        out[0][...] = DEEPNORM_ALPHA * ex[0][...] + acc

    (gx,) = _mm("in_proj_bwd", dh, w_in, "nt", (tm, tn, 512), [_tile_out((S, D), F32, tm, tn)], epi,
                extras=[(dz1, (tm, tn), lambda i, j, k: (i, j))], after=after)
    return gx


def _my_place():
    x, y, c = lax.axis_index("x"), lax.axis_index("y"), lax.axis_index("c")
    return x, y, c


def _flat(px, py, pc):
    return 4 * px + 2 * py + pc


def _shard_slice(ref, axis, idx, size):
    start = pl.multiple_of(idx * size, size)
    ix = [slice(None)] * len(ref.shape)
    ix[axis] = pl.ds(start, size)
    return ref.at[tuple(ix)]


def _all_gather_weights(shards, axes):
    n_w = len(shards)
    full_shapes = []
    for s, ax in zip(shards, axes):
        sh = list(s.shape)
        sh[ax] *= N_DEV
        full_shapes.append(tuple(sh))

    def body(*refs):
        src = refs[:n_w]
        dst = refs[n_w:2 * n_w]
        send_sems, recv_sems, local_sems = refs[2 * n_w:]
        x, y, c = _my_place()
        me, sibling = (x, y, c), (x, y, 1 - c)
        chips = [(1 - x, y), (x, 1 - y), (1 - x, 1 - y)]

        def place(w, dev):
            return _shard_slice(dst[w], axes[w], _flat(*dev), src[w].shape[axes[w]])

        def copy(w, k, block, to, from_src=False):
            return pltpu.make_async_remote_copy(
                src_ref=src[w] if from_src else place(w, block), dst_ref=place(w, block),
                send_sem=send_sems.at[w * 7 + k], recv_sem=recv_sems.at[w * 7 + k], device_id=to, device_id_type=MESH)

        mine, first, passed = [], [], []
        for w in range(n_w):
            cp = pltpu.make_async_copy(src[w], place(w, me), local_sems.at[w])
            cp.start()
            mine.append(cp)
            fw = [copy(w, 0, me, sibling, True)] + [copy(w, 1 + j, me, (*chip, c), True) for j, chip in enumerate(chips)]
            for cp in fw:
                cp.start()
            first += fw
        for w in range(n_w):
            for j, chip in enumerate(chips):
                copy(w, 1 + j, (*chip, c), me).wait_recv()
                cp = copy(w, 4 + j, (*chip, c), sibling)
                cp.start()
                passed.append(cp)
        for w in range(n_w):
            copy(w, 0, sibling, me).wait_recv()
            for j, chip in enumerate(chips):
                copy(w, 4 + j, (*chip, 1 - c), me).wait_recv()
        for cp in first + passed:
            cp.wait_send()
        for cp in mine:
            cp.wait()

    any_spec = pl.BlockSpec(memory_space=pl.ANY)
    return pl.pallas_call(
        body, name="all_gather_weights",
        in_specs=[any_spec] * n_w, out_specs=[any_spec] * n_w,
        out_shape=[jax.ShapeDtypeStruct(sh, s.dtype) for sh, s in zip(full_shapes, shards)],
        scratch_shapes=[pltpu.SemaphoreType.DMA((n_w * 7,)), pltpu.SemaphoreType.DMA((n_w * 7,)),
                        pltpu.SemaphoreType.DMA((n_w,))],
    )(*shards)


_HBM_SPEC = pl.BlockSpec(memory_space=pltpu.HBM)
_SEM_SPEC = pl.BlockSpec(memory_space=pltpu.SEMAPHORE)
_ANY_SPEC = pl.BlockSpec(memory_space=pl.ANY)
_N_PEER = N_DEV - 1


def _peer_of(x, y, c, r):
    return (x ^ ((r >> 2) & 1), y ^ ((r >> 1) & 1), c ^ (r & 1))


class _Exchange:
    def __init__(self, name, part, slot):
        self.name, self.part, self.slot = name, part, slot

    def _copy(self, w, r, src, land, send_sems, recv_sems, sending):
        x, y, c = _my_place()
        peer = _peer_of(x, y, c, r)
        return pltpu.make_async_remote_copy(
            src_ref=self.part(w, src, _flat(*peer)),
            dst_ref=self.slot(w, land, _flat(x, y, c) if sending else _flat(*peer)),
            send_sem=send_sems.at[w * _N_PEER + r - 1], recv_sem=recv_sems.at[w * _N_PEER + r - 1],
            device_id=peer, device_id_type=MESH)

    def start(self, srcs, lands, after=None):
        n = len(srcs)
        n_after = 0 if after is None else 1

        def body(*refs):
            src, land = refs[:n], refs[n:2 * n]
            outs = refs[2 * n + n_after:]
            send_sems, recv_sems, token, local_sems = outs[0], outs[1], outs[2 + 2 * n], outs[3 + 2 * n]
            x, y, c = _my_place()
            me = _flat(x, y, c)
            own = [pltpu.make_async_copy(self.part(w, src[w], me), self.slot(w, land[w], me), local_sems.at[w])
                   for w in range(n)]
            for w in range(n):
                own[w].start()
                for r in range(1, N_DEV):
                    self._copy(w, r, src[w], land[w], send_sems, recv_sems, True).start()
            token[...] = jnp.zeros_like(token)
            for w in range(n):
                own[w].wait()

        sems = pltpu.SemaphoreType.DMA((n * _N_PEER,))
        outs = pl.pallas_call(
            body, name=self.name + "_start",
            out_shape=(sems, sems, *[pltpu.HBM(t.shape, t.dtype) for t in list(srcs) + list(lands)],
                       jax.ShapeDtypeStruct((8, 128), F32)),
            in_specs=[_HBM_SPEC] * (2 * n) + [_ANY_SPEC] * n_after,
            out_specs=(_SEM_SPEC, _SEM_SPEC, *[_HBM_SPEC] * (2 * n), pl.BlockSpec(memory_space=pltpu.VMEM)),
            scratch_shapes=[pltpu.SemaphoreType.DMA((n,))],
            input_output_aliases={i: 2 + i for i in range(2 * n)},
            compiler_params=pltpu.CompilerParams(has_side_effects=pltpu.SideEffectType.DATAFLOW_SIDE_EFFECTING),
        )(*[pltpu.with_memory_space_constraint(t, pltpu.HBM) for t in list(srcs) + list(lands)],
          *([after] if n_after else []))
        return outs[0], outs[1], outs[2:2 + n], outs[2 + n:2 + 2 * n], outs[2 + 2 * n]

    def wait(self, started, after):
        send_sems, recv_sems, srcs, lands, _ = started
        n = len(srcs)

        def body(*refs):
            src, land = refs[:n], refs[n:2 * n]
            s_sems, r_sems = refs[2 * n], refs[2 * n + 1]
            for w in range(n):
                for r in range(1, N_DEV):
                    cp = self._copy(w, r, src[w], land[w], s_sems, r_sems, False)
                    cp.wait_send()
                    cp.wait_recv()

        outs = pl.pallas_call(
            body, name=self.name + "_wait",
            out_shape=[pltpu.HBM(t.shape, t.dtype) for t in list(srcs) + list(lands)],
            in_specs=[_HBM_SPEC] * (2 * n) + [_SEM_SPEC, _SEM_SPEC, _ANY_SPEC],
            out_specs=[_HBM_SPEC] * (2 * n),
            input_output_aliases={i: i for i in range(2 * n)},
            compiler_params=pltpu.CompilerParams(has_side_effects=pltpu.SideEffectType.DATAFLOW_SIDE_EFFECTING),
        )(*srcs, *lands, send_sems, recv_sems, after)
        return outs[n:]


def _gather_exchange(name, axes, shard_sizes):
    return _Exchange(name, lambda w, src, dev: src,
                     lambda w, land, dev: _shard_slice(land, axes[w], dev, shard_sizes[w]))


def _scatter_exchange(name, axes, shard_sizes):
    def part(w, src, dev):
        return src if axes[w] is None else _shard_slice(src, axes[w], dev, shard_sizes[w])
    return _Exchange(name, part, lambda w, land, dev: land.at[dev])


def _adamw(name, partials, w, m, v):
    R, C = w.shape
    tr = R
    while tr * C * 4 > (1 << 20) and tr % 16 == 0:
        tr //= 2

    def body(p_ref, w_ref, m_ref, v_ref, g_ref, d_ref, nm_ref, nv_ref):
        g = p_ref[0].astype(F32)
        for jdev in range(1, N_DEV):
            g = g + p_ref[jdev].astype(F32)
        nm = ADAM_B1 * m_ref[...] + (1.0 - ADAM_B1) * g
        nv = ADAM_B2 * v_ref[...] + (1.0 - ADAM_B2) * (g * g)
        m_hat = nm / (1.0 - ADAM_B1 ** ADAM_STEP)
        v_hat = nv / (1.0 - ADAM_B2 ** ADAM_STEP)
        g_ref[...] = g
        d_ref[...] = -ADAM_LR * (m_hat / (jnp.sqrt(v_hat) + ADAM_EPS) + ADAM_WD * w_ref[...])
        nm_ref[...] = nm
        nv_ref[...] = nv

    spec = pl.BlockSpec((tr, C), lambda i: (i, 0))
    return pl.pallas_call(
        body, name=name, grid=(R // tr,),
        in_specs=[pl.BlockSpec((N_DEV, tr, C), lambda i: (0, i, 0)), spec, spec, spec],
        out_specs=[spec] * 4,
        out_shape=[jax.ShapeDtypeStruct((R, C), F32)] * 4,
        compiler_params=pltpu.CompilerParams(dimension_semantics=("parallel",),
                                             vmem_limit_bytes=_vmem_limit(24 << 20)),
    )(partials, w, m, v)


def _local_step(x, cos2, sin_fwd, sin_bwd, w_in, mix_weights, ffn_weights, pool_scale, g_mix, b_mix, g_ff, b_ff,
                target, send, start_after=None):
    S, D = x.shape
    aw = pw = D // 2
    u_col_block = 3
    gate_col0 = 4 * aw

    xb = x.astype(BF16)
    h = _in_proj(xb, w_in, cos2, sin_fwd, 2 * aw, after=start_after)
    dilated = [d for d in DILATIONS if d > 1]
    qkv = {1: h[None], **dict(zip(dilated, _to_residue_major("qkv_to_rm", h, 0, 3 * aw)))}
    fwd = [_attn_fwd(qkv[d], d, aw) for d in DILATIONS]
    o_attn, l_tot = _attn_combine([f[0] for f in fwd], [f[1] for f in fwd], aw)
    w_pool, w_ba, w_bp, w_out = mix_weights(o_attn)
    p, y_pre, pm = _pool_fwd(h, w_pool, pool_scale, pw, u_col_block)
    y_attn = _branch_attn(o_attn, w_ba)
    y_pool, merged = _branch_pool_merge(pm, w_bp, h, y_attn, gate_col0)
    w1, w2 = ffn_weights(merged)
    x1, x1b, xhat1, rstd1 = _out_proj_ln(merged, w_out, x, g_mix, b_mix)
    a = _ffn_up(x1b, w1)
    dz2, dz2b, dg_ff, db_ff, loss = _ffn_down_loss(a, w2, x1, g_ff, b_ff, target)

    tok = send("ff2", [_grad_weight("grad_w_ff2", a, dz2b)])
    dh1 = _ffn_down_bwd(dz2b, w2, a, after=tok)
    tok = send("ff1", [_grad_weight("grad_w_ff1", x1b, dh1)])
    dz1, dz1b, dg_mix, db_mix = _ffn_up_bwd_ln(dh1, w1, dz2, xhat1, rstd1, g_mix, after=tok)
    tok = send("out", [_grad_weight("grad_w_out", merged, dz1b)])
    dy_attn, dy_pool, dh = _out_proj_bwd(dz1b, w_out, h, y_attn, y_pool, gate_col0, after=tok)
    tok = send("branch", [_grad_weight("grad_w_branch_attn", o_attn, dy_attn),
                          _grad_weight("grad_w_branch_pool", pm, dy_pool)])
    d_out, stats = _branch_attn_bwd(dy_attn, w_ba, o_attn, l_tot, after=tok)
    dy_pre, d_scale = _branch_pool_bwd(dy_pool, w_bp, y_pre, pool_scale)
    dh, dw_pool = _pool_bwd(dh, dy_pre, p, w_pool, pw, u_col_block)
    d_outs = {1: d_out[None], **dict(zip(dilated, _to_residue_major("dout_to_rm", d_out, 0, aw)))}
    statss = {1: stats[None], **dict(zip(dilated, _to_residue_major("stats_to_rm", stats, 0, HEAD_DIM)))}
    per_pattern = [_attn_bwd(qkv[d], d_outs[d], statss[d], d, aw) for d in DILATIONS]
    dh = _attn_bwd_finish(dh, per_pattern, cos2, sin_bwd, aw)
    small = jnp.concatenate((d_scale, dg_mix, db_mix, dg_ff, db_ff), axis=-1)
    tok = send("in", [_grad_weight("grad_w_in", xb, dh), dw_pool.astype(BF16),
                      small.reshape(small.shape[-1] // HEAD_DIM, HEAD_DIM)])
    grad_x = _in_proj_bwd(dh, w_in, dz1, after=tok)
    return loss, grad_x


def _rope_tables(positions):
    half = HEAD_DIM // 2
    inv_freq = ROPE_THETA ** (-jnp.arange(half, dtype=F32) / half)
    ang = positions.astype(F32)[:, None] * inv_freq
    cos, sin = jnp.cos(ang), jnp.sin(ang)
    cos2 = jnp.concatenate([cos, cos], axis=-1)
    sin_fwd = jnp.concatenate([-sin, sin], axis=-1)
    return cos2, sin_fwd, -sin_fwd


def kernel(x, positions, w_in, w_pool, pool_scale, w_branch_attn, w_branch_pool, w_out, ln_mix_g, ln_mix_b, w_ff1, w_ff2, ln_ff_g, ln_ff_b, loss_target, m_w_in, m_w_pool, m_pool_scale, m_w_branch_attn, m_w_branch_pool, m_w_out, m_ln_mix_g, m_ln_mix_b, m_w_ff1, m_w_ff2, m_ln_ff_g, m_ln_ff_b, v_w_in, v_w_pool, v_pool_scale, v_w_branch_attn, v_w_branch_pool, v_w_out, v_ln_mix_g, v_ln_mix_b, v_w_ff1, v_w_ff2, v_ln_ff_g, v_ln_ff_b):
    big_w = (w_in[0], w_pool[0], w_branch_attn[0], w_branch_pool[0], w_out[0], w_ff1[0], w_ff2[0])
    big_m = (m_w_in[0], m_w_pool[0], m_w_branch_attn[0], m_w_branch_pool[0], m_w_out[0], m_w_ff1[0], m_w_ff2[0])
    big_v = (v_w_in[0], v_w_pool[0], v_w_branch_attn[0], v_w_branch_pool[0], v_w_out[0], v_w_ff1[0], v_w_ff2[0])
    shard_axes = (1, 1, 1, 1, 0, 1, 0)
    small_w = (pool_scale, ln_mix_g, ln_mix_b, ln_ff_g, ln_ff_b)
    small_m = (m_pool_scale, m_ln_mix_g, m_ln_mix_b, m_ln_ff_g, m_ln_ff_b)
    small_v = (v_pool_scale, v_ln_mix_g, v_ln_mix_b, v_ln_ff_g, v_ln_ff_b)

    names = ("w_in", "w_pool", "w_branch_attn", "w_branch_pool", "w_out", "w_ff1", "w_ff2")
    axis_of = dict(zip(names, shard_axes))
    shard_of = dict(zip(names, [w.astype(BF16) for w in big_w]))

    def full_buffer(n):
        s, ax = shard_of[n], axis_of[n]
        full = list(s.shape)
        full[ax] *= N_DEV
        return lax.empty(tuple(full), s.dtype)

    def gather_group(tag, group, after):
        ex = _gather_exchange(tag, [axis_of[n] for n in group], [shard_of[n].shape[axis_of[n]] for n in group])
        return ex, ex.start([shard_of[n] for n in group], [full_buffer(n) for n in group], after)

    (gw_in,) = _all_gather_weights([shard_of["w_in"]], (axis_of["w_in"],))
    mix_ex, mix_started = gather_group("gather_mix", ("w_pool", "w_branch_attn", "w_branch_pool", "w_out"), gw_in)
    ffn_ex, ffn_started = gather_group("gather_ffn", ("w_ff1", "w_ff2"), mix_started[4])

    groups = {"ff2": ("w_ff2",), "ff1": ("w_ff1",), "out": ("w_out",),
              "branch": ("w_branch_attn", "w_branch_pool"), "in": ("w_in", "w_pool", "small")}
    sent = {}

    def send(key, grads_):
        axes = [axis_of.get(n) for n in groups[key]]
        sizes = [None if ax is None else g.shape[ax] // N_DEV for g, ax in zip(grads_, axes)]
        lands = []
        for g, ax, size in zip(grads_, axes, sizes):
            shard = list(g.shape)
            if ax is not None:
                shard[ax] = size
            lands.append(lax.empty((N_DEV, *shard), g.dtype))
        ex = _scatter_exchange("scatter_" + key, axes, sizes)
        sent[key] = (ex, ex.start(list(grads_), lands))
        return sent[key][1][4]

    cos2, sin_fwd, sin_bwd = _rope_tables(positions[0])
    loss, grad_x = _local_step(
        x[0], cos2, sin_fwd, sin_bwd, gw_in, lambda after: mix_ex.wait(mix_started, after),
        lambda after: ffn_ex.wait(ffn_started, after), pool_scale, ln_mix_g, ln_mix_b, ln_ff_g, ln_ff_b,
        loss_target[0], send, start_after=ffn_started[4])

    state = dict(zip(names, zip(big_w, big_m, big_v)))
    n_small = sum(w.shape[-1] for w in small_w)
    small_2d = (n_small // HEAD_DIM, HEAD_DIM)
    state["small"] = tuple(jnp.concatenate(t, axis=-1).reshape(small_2d) for t in (small_w, small_m, small_v))
    grads, deltas, new_ms, new_vs = {}, {}, {}, {}
    after = grad_x
    for key in ("ff2", "ff1", "out", "branch", "in"):
        ex, started = sent[key]
        for n, part in zip(groups[key], ex.wait(started, after)):
            w, m, v = state[n]
            r2 = (-1, w.shape[-1])
            w2d = w.reshape(r2)
            res = _adamw("adamw_" + n, part.reshape((N_DEV,) + w2d.shape), w2d, m.reshape(r2), v.reshape(r2))
            after = res[0]
            if n == "small":
                small_out = [t.reshape(1, n_small) for t in res]
            else:
                grads[n], deltas[n], new_ms[n], new_vs[n] = (t.reshape((1,) + w.shape) for t in res)
    small_names = ("pool_scale", "ln_mix_g", "ln_mix_b", "ln_ff_g", "ln_ff_b")
    off = 0
    for n, w in zip(small_names, small_w):
        width = w.shape[-1]
        grads[n], deltas[n], new_ms[n], new_vs[n] = (t[:, off:off + width] for t in small_out)
        off += width

    order = ("w_in", "w_pool", "pool_scale", "w_branch_attn", "w_branch_pool", "w_out", "ln_mix_g", "ln_mix_b",
             "w_ff1", "w_ff2", "ln_ff_g", "ln_ff_b")
    total_loss = lax.psum(loss[0, 0], ("x", "y", "c"))
    return (total_loss, grad_x[None], *[grads[n] for n in order], *[deltas[n] for n in order],
            *[new_ms[n] for n in order], *[new_vs[n] for n in order])
```

```python
import functools

import jax
import jax.numpy as jnp
from jax import lax
from jax.experimental import pallas as pl
from jax.experimental.pallas import tpu as pltpu

F32 = jnp.float32
BF16 = jnp.bfloat16

N_DEV = 8
HEAD_DIM = 128
SUB_BLOCK = 128
DILATIONS = (1, 4, 16)
POOL_WINDOWS = (2, 4, 8, 16)
MAX_POOL_WINDOW = 16
POOL_HALO = 128
PERM_ROWS = 512
LN_EPS = 1e-5
DEEPNORM_ALPHA = 2.0 ** 0.25
ROPE_THETA = 10000.0
ATTN_SCALE = HEAD_DIM ** -0.5
ADAM_LR, ADAM_B1, ADAM_B2, ADAM_EPS, ADAM_WD, ADAM_STEP = 0.001, 0.9, 0.999, 1e-08, 0.01, 10
NEG_BIG = -1e30
VMEM_CAP_V7X = 64 * 1024 * 1024
MESH = pl.DeviceIdType.MESH


def _vmem_limit(est_bytes):
    return int(min(max(est_bytes * 5 // 4 + (4 << 20), 16 << 20), VMEM_CAP_V7X - (6 << 20)))


def _nbytes(shape, dtype):
    n = 1
    for s in shape:
        n *= s
    return n * jnp.dtype(dtype).itemsize


def _mm(name, a, b, form, tiles, outs, epi, extras=(), sequential=False, after=None):
    tm, tn, tk = tiles
    if form == "nn":
        (M, K), (K2, N) = a.shape, b.shape
    elif form == "nt":
        (M, K), (N, K2) = a.shape, b.shape
    else:
        (K, M), (K2, N) = a.shape, b.shape
    assert K == K2, (name, a.shape, b.shape)
    tm, tn, tk = min(tm, M), min(tn, N), min(tk, K)
    assert M % tm == 0 and N % tn == 0 and K % tk == 0, (name, M, N, K, tm, tn, tk)
    grid = (M // tm, N // tn, K // tk)
    nk = grid[2]
    if form == "nn":
        a_spec = pl.BlockSpec((tm, tk), lambda i, j, k: (i, k))
        b_spec = pl.BlockSpec((tk, tn), lambda i, j, k: (k, j))
        contract = ((1,), (0,))
    elif form == "nt":
        a_spec = pl.BlockSpec((tm, tk), lambda i, j, k: (i, k))
        b_spec = pl.BlockSpec((tn, tk), lambda i, j, k: (j, k))
        contract = ((1,), (1,))
    else:
        a_spec = pl.BlockSpec((tk, tm), lambda i, j, k: (k, i))
        b_spec = pl.BlockSpec((tk, tn), lambda i, j, k: (k, j))
        contract = ((0,), (0,))
    n_ex, n_out = len(extras), len(outs)
    n_after = 0 if after is None else 1

    def body(a_ref, b_ref, *rest):
        ex_refs = rest[:n_ex]
        rest = rest[n_ex + n_after:]
        out_refs = rest[:n_out]
        i, j, k = pl.program_id(0), pl.program_id(1), pl.program_id(2)

        def prod():
            return lax.dot_general(a_ref[...].astype(BF16), b_ref[...].astype(BF16),
                                   (contract, ((), ())), preferred_element_type=F32)

        if nk == 1:
            epi(prod(), ex_refs, out_refs, i, j)
        else:
            acc = rest[n_out]

            @pl.when(k == 0)
            def _():
                acc[...] = prod()

            @pl.when(k > 0)
            def _():
                acc[...] += prod()

            @pl.when(k == nk - 1)
            def _():
                epi(acc[...], ex_refs, out_refs, i, j)

    est = 2 * (_nbytes(a_spec.block_shape, a.dtype) + _nbytes(b_spec.block_shape, b.dtype))
    est += sum(2 * _nbytes(bs, arr.dtype) for arr, bs, _ in extras)
    est += sum(2 * _nbytes(bs, dt) for _, dt, bs, _ in outs)
    est += 4 * tm * tn * 4
    sem = ("arbitrary",) * 3 if sequential else ("parallel", "parallel", "arbitrary")
    return pl.pallas_call(
        body, name=name, grid=grid,
        in_specs=([a_spec, b_spec] + [pl.BlockSpec(bs, im) for _, bs, im in extras]
                  + [pl.BlockSpec(memory_space=pl.ANY)] * n_after),
        out_specs=[pl.BlockSpec(bs, im) for _, _, bs, im in outs],
        out_shape=[jax.ShapeDtypeStruct(sh, dt) for sh, dt, _, _ in outs],
        scratch_shapes=[pltpu.VMEM((tm, tn), F32)] if nk > 1 else [],
        compiler_params=pltpu.CompilerParams(dimension_semantics=sem, vmem_limit_bytes=_vmem_limit(est)),
    )(a, b, *[arr for arr, _, _ in extras], *([after] if n_after else []))


def _tile_out(shape, dtype, tm, tn):
    return (shape, dtype, (tm, tn), lambda i, j, k: (i, j))


def _row_sum_out(width):
    return ((1, width), F32, (1, width), lambda i, j, k: (0, 0))


def _accumulate_rows(ref, value, i):
    @pl.when(i == 0)
    def _():
        ref[...] = value

    @pl.when(i > 0)
    def _():
        ref[...] += value


def _layer_norm_bwd(dy, xhat, rstd, g):
    dxh = dy * g
    m1 = jnp.mean(dxh, axis=-1, keepdims=True)
    m2 = jnp.mean(dxh * xhat, axis=-1, keepdims=True)
    return rstd * (dxh - m1 - xhat * m2)


def _rope_apply(t, cos2, sin_signed):
    return t * cos2 + pltpu.roll(t, HEAD_DIM // 2, axis=1) * sin_signed


def _in_proj(xb, w_in, cos2, sin_fwd, n_rope_cols, after=None):
    S, D = xb.shape
    W = w_in.shape[1]
    tm, tn = min(1024, S), min(2048, n_rope_cols)
    assert n_rope_cols % tn == 0
    n_rope_tiles = n_rope_cols // tn

    def epi(acc, ex, out, i, j):
        cos_ref, sin_ref = ex
        (h_ref,) = out

        @pl.when(j < n_rope_tiles)
        def _():
            c, s = cos_ref[...], sin_ref[...]
            for hd in range(tn // HEAD_DIM):
                sl = slice(hd * HEAD_DIM, (hd + 1) * HEAD_DIM)
                h_ref[:, sl] = _rope_apply(acc[:, sl], c, s).astype(BF16)

        @pl.when(j >= n_rope_tiles)
        def _():
            h_ref[...] = acc.astype(BF16)

    row = lambda i, j, k: (i, 0)
    (h,) = _mm("in_proj", xb, w_in, "nn", (tm, tn, 512),
               [_tile_out((S, W), BF16, tm, tn)], epi,
               extras=[(cos2, (tm, HEAD_DIM), row), (sin_fwd, (tm, HEAD_DIM), row)], after=after)
    return h


def _attn_mask(mb):
    qi = lax.broadcasted_iota(jnp.int32, (SUB_BLOCK, 2 * SUB_BLOCK), 0)
    kj = lax.broadcasted_iota(jnp.int32, (SUB_BLOCK, 2 * SUB_BLOCK), 1)
    prev = jnp.logical_and(jnp.logical_and(kj < SUB_BLOCK, kj >= qi), mb > 0)
    cur = jnp.logical_and(kj >= SUB_BLOCK, kj - SUB_BLOCK <= qi)
    return jnp.logical_or(prev, cur)


def _dot_nt(a, b):
    return lax.dot_general(a, b, (((1,), (1,)), ((), ())), preferred_element_type=F32)


def _dot_tn(a, b):
    return lax.dot_general(a, b, (((0,), (0,)), ((), ())), preferred_element_type=F32)


def _dot_nn(a, b):
    return lax.dot_general(a, b, (((1,), (0,)), ((), ())), preferred_element_type=F32)


def _perm_matrix(d, to_residue_major):
    g = PERM_ROWS // d
    i = lax.broadcasted_iota(jnp.int32, (PERM_ROWS, PERM_ROWS), 0)
    j = lax.broadcasted_iota(jnp.int32, (PERM_ROWS, PERM_ROWS), 1)
    if to_residue_major:
        hit = j == (i % g) * d + i // g
    else:
        hit = j == (i % d) * g + i // d
    return hit.astype(BF16)


def _permute_rows(perm, x):
    if x.dtype == BF16:
        return _dot_nn(perm, x)
    hi = x.astype(BF16)
    r1 = x - hi.astype(F32)
    mid = r1.astype(BF16)
    lo = (r1 - mid.astype(F32)).astype(BF16)
    return (_dot_nn(perm, hi) + _dot_nn(perm, mid)) + _dot_nn(perm, lo)


def _rm_block(d, width):
    return pl.BlockSpec((d, PERM_ROWS // d, width), lambda i: (0, i, 0))


def _to_residue_major(name, x, col_block, width):
    S = x.shape[0]
    dils = [d for d in DILATIONS if d > 1]
    chunk = min(width, 1024)

    def body(x_ref, *out_refs):
        for d, o_ref in zip(dils, out_refs):
            perm = _perm_matrix(d, True)
            for c0 in range(0, width, chunk):
                cw = min(chunk, width - c0)
                y = _permute_rows(perm, x_ref[:, c0:c0 + cw])
                o_ref[:, :, c0:c0 + cw] = y.astype(x.dtype).reshape(d, PERM_ROWS // d, cw)

    return pl.pallas_call(
        body, name=name, grid=(S // PERM_ROWS,),
        in_specs=[pl.BlockSpec((PERM_ROWS, width), lambda i: (i, col_block))],
        out_specs=[_rm_block(d, width) for d in dils],
        out_shape=[jax.ShapeDtypeStruct((d, S // d, width), x.dtype) for d in dils],
        compiler_params=pltpu.CompilerParams(dimension_semantics=("parallel",),
                                             vmem_limit_bytes=_vmem_limit(32 << 20)),
    )(x)


def _qkv_specs(aw):
    def spec(col, prev):
        if prev:
            return pl.BlockSpec((None, SUB_BLOCK, aw), lambda r, mb: (r, jnp.maximum(mb - 1, 0), col))
        return pl.BlockSpec((None, SUB_BLOCK, aw), lambda r, mb: (r, mb, col))
    return [spec(0, False), spec(1, True), spec(1, False), spec(2, True), spec(2, False)]


def _put_column(tile, col, value):
    lane = lax.broadcasted_iota(jnp.int32, tile.shape, 1)
    return jnp.where(lane == col, value, tile)


def _attn_fwd(qkv, d, aw):
    _, rows, _ = qkv.shape
    n_heads = aw // HEAD_DIM
    nb = rows // SUB_BLOCK

    def body(q_ref, kp_ref, kc_ref, vp_ref, vc_ref, o_ref, lse_ref, s_buf, p_buf):
        mask = _attn_mask(pl.program_id(1))
        for hd in range(n_heads):
            sl = slice(hd * HEAD_DIM, (hd + 1) * HEAD_DIM)
            q = q_ref[:, sl]
            s_buf[hd, :, :SUB_BLOCK] = _dot_nt(q, kp_ref[:, sl])
            s_buf[hd, :, SUB_BLOCK:] = _dot_nt(q, kc_ref[:, sl])
        lse_tile = jnp.zeros((SUB_BLOCK, HEAD_DIM), F32)
        inv_tile = jnp.zeros((SUB_BLOCK, HEAD_DIM), F32)
        for hd in range(n_heads):
            s = jnp.where(mask, s_buf[hd] * ATTN_SCALE, NEG_BIG)
            m = jnp.max(s, axis=-1, keepdims=True)
            p = jnp.exp(s - m)
            l = jnp.sum(p, axis=-1, keepdims=True)
            p_buf[hd] = p.astype(BF16)
            lse_tile = _put_column(lse_tile, hd, m + jnp.log(l))
            inv_tile = _put_column(inv_tile, hd, 1.0 / l)
        lse_ref[...] = lse_tile
        for hd in range(n_heads):
            sl = slice(hd * HEAD_DIM, (hd + 1) * HEAD_DIM)
            o = _dot_nn(p_buf[hd, :, :SUB_BLOCK], vp_ref[:, sl]) + _dot_nn(p_buf[hd, :, SUB_BLOCK:], vc_ref[:, sl])
            o_ref[:, sl] = o * inv_tile[:, hd:hd + 1]

    return pl.pallas_call(
        body, name=f"attn_fwd_d{d}", grid=(d, nb),
        in_specs=_qkv_specs(aw),
        out_specs=[pl.BlockSpec((None, SUB_BLOCK, aw), lambda r, mb: (r, mb, 0)),
                   pl.BlockSpec((None, SUB_BLOCK, HEAD_DIM), lambda r, mb: (r, mb, 0))],
        out_shape=[jax.ShapeDtypeStruct((d, rows, aw), F32), jax.ShapeDtypeStruct((d, rows, HEAD_DIM), F32)],
        scratch_shapes=[pltpu.VMEM((n_heads, SUB_BLOCK, 2 * SUB_BLOCK), F32),
                        pltpu.VMEM((n_heads, SUB_BLOCK, 2 * SUB_BLOCK), BF16)],
        compiler_params=pltpu.CompilerParams(dimension_semantics=("parallel", "parallel"),
                                             vmem_limit_bytes=_vmem_limit(16 << 20)),
    )(qkv, qkv, qkv, qkv, qkv)


def _attn_combine(outs, lses, aw):
    S = outs[0].shape[1]
    n_heads = aw // HEAD_DIM
    n_pat = len(DILATIONS)

    def body(*refs):
        o_refs, l_refs = refs[:n_pat], refs[n_pat:2 * n_pat]
        o_ref, lt_ref = refs[2 * n_pat], refs[2 * n_pat + 1]
        o_nat, l_nat = [], []
        for d, o_r, l_r in zip(DILATIONS, o_refs, l_refs):
            o_p = o_r[...].reshape(PERM_ROWS, aw)
            l_p = l_r[...].reshape(PERM_ROWS, HEAD_DIM)
            if d > 1:
                perm = _perm_matrix(d, False)
                o_p, l_p = _permute_rows(perm, o_p), _permute_rows(perm, l_p)
            o_nat.append(o_p)
            l_nat.append(l_p)
        mx = functools.reduce(jnp.maximum, l_nat)
        es = [jnp.exp(l_p - mx) for l_p in l_nat]
        den = functools.reduce(jnp.add, es)
        lt_ref[...] = mx + jnp.log(den)
        ws = [e / den for e in es]
        for hd in range(n_heads):
            sl = slice(hd * HEAD_DIM, (hd + 1) * HEAD_DIM)
            o = ws[0][:, hd:hd + 1] * o_nat[0][:, sl]
            for pi in range(1, n_pat):
                o = o + ws[pi][:, hd:hd + 1] * o_nat[pi][:, sl]
            o_ref[:, sl] = o.astype(BF16)

    return pl.pallas_call(
        body, name="attn_combine", grid=(S // PERM_ROWS,),
        in_specs=[_rm_block(d, aw) for d in DILATIONS] + [_rm_block(d, HEAD_DIM) for d in DILATIONS],
        out_specs=[pl.BlockSpec((PERM_ROWS, aw), lambda i: (i, 0)), pl.BlockSpec((PERM_ROWS, HEAD_DIM), lambda i: (i, 0))],
        out_shape=[jax.ShapeDtypeStruct((S, aw), BF16), jax.ShapeDtypeStruct((S, HEAD_DIM), F32)],
        compiler_params=pltpu.CompilerParams(dimension_semantics=("parallel",),
                                             vmem_limit_bytes=_vmem_limit(40 << 20)),
    )(*outs, *lses)


def _band(tm, width, w, row_offset, transpose):
    t = lax.broadcasted_iota(jnp.int32, (tm, width), 0)
    u = lax.broadcasted_iota(jnp.int32, (tm, width), 1)
    dist = (u - t - row_offset) if transpose else (t + row_offset - u)
    return jnp.logical_and(dist >= 0, dist < w).astype(BF16)


def _pool_fwd(h, w_pool, pool_scale, pw, u_col_block):
    S, W = h.shape
    n_groups = len(POOL_WINDOWS)
    gw = pw // n_groups
    tm = min(512, S)
    halo_per_tile = tm // POOL_HALO

    def body(uc_ref, uh_ref, w_ref, sc_ref, p_ref, y_ref, pm_ref):
        i = pl.program_id(0)
        t_abs = i * tm + lax.broadcasted_iota(jnp.int32, (tm, 1), 0)
        for g, w in enumerate(POOL_WINDOWS):
            sl = slice(g * gw, (g + 1) * gw)
            uc = uc_ref[:, sl]
            uh = jnp.where(i > 0, uh_ref[:, sl], jnp.zeros((POOL_HALO, gw), BF16))
            ssum = _dot_nn(_band(tm, tm, w, 0, False), uc) + _dot_nn(_band(tm, POOL_HALO, w, POOL_HALO, False), uh)
            cnt = jnp.minimum(t_abs + 1, w).astype(F32)
            p = (ssum / cnt - uc.astype(F32)).astype(BF16)
            y = _dot_nn(p, w_ref[g])
            p_ref[:, sl] = p
            y_ref[:, sl] = y.astype(BF16)
            pm_ref[:, sl] = (y * sc_ref[:, sl]).astype(BF16)

    row = pl.BlockSpec((tm, pw), lambda i: (i, 0))
    return pl.pallas_call(
        body, name="pool_fwd", grid=(S // tm,),
        in_specs=[pl.BlockSpec((tm, pw), lambda i: (i, u_col_block)),
                  pl.BlockSpec((POOL_HALO, pw), lambda i: (jnp.maximum(i * halo_per_tile - 1, 0), u_col_block)),
                  pl.BlockSpec((n_groups, gw, gw), lambda i: (0, 0, 0)),
                  pl.BlockSpec((1, pw), lambda i: (0, 0))],
        out_specs=[row, row, row],
        out_shape=[jax.ShapeDtypeStruct((S, pw), BF16)] * 3,
        compiler_params=pltpu.CompilerParams(dimension_semantics=("parallel",),
                                             vmem_limit_bytes=_vmem_limit(24 << 20)),
    )(h, h, w_pool, pool_scale)


def _branch_attn(o_attn, w_ba):
    S, _ = o_attn.shape
    D = w_ba.shape[1]
    tm, tn = min(1024, S), D

    def epi(acc, ex, out, i, j):
        out[0][...] = acc.astype(BF16)

    (y,) = _mm("branch_attn", o_attn, w_ba, "nn", (tm, tn, 1024), [_tile_out((S, D), BF16, tm, tn)], epi)
    return y


def _branch_pool_merge(pm, w_bp, h, y_attn, gate_col0):
    S, _ = pm.shape
    D = w_bp.shape[1]
    tm, tn = min(512, S), D
    ga0, gp0 = gate_col0 // tn, (gate_col0 + D) // tn

    def epi(acc, ex, out, i, j):
        ga_ref, gp_ref, ya_ref = ex
        yp_ref, mg_ref = out
        yp = acc.astype(BF16)
        yp_ref[...] = yp
        mg = (jax.nn.sigmoid(ga_ref[...].astype(F32)) * ya_ref[...].astype(F32)
              + jax.nn.sigmoid(gp_ref[...].astype(F32)) * acc)
        mg_ref[...] = mg.astype(BF16)

    y_pool, merged = _mm(
        "branch_pool_merge", pm, w_bp, "nn", (tm, tn, 1024),
        [_tile_out((S, D), BF16, tm, tn), _tile_out((S, D), BF16, tm, tn)], epi,
        extras=[(h, (tm, tn), lambda i, j, k: (i, ga0 + j)), (h, (tm, tn), lambda i, j, k: (i, gp0 + j)),
                (y_attn, (tm, tn), lambda i, j, k: (i, j))])
    return y_pool, merged


def _layer_norm_rows(z, g, b):
    mu = jnp.mean(z, axis=-1, keepdims=True)
    zc = z - mu
    var = jnp.mean(zc * zc, axis=-1, keepdims=True)
    rstd = lax.rsqrt(var + LN_EPS)
    xhat = zc * rstd
    return xhat * g + b, xhat, rstd


def _out_proj_ln(merged, w_out, x, g, b):
    S, D = x.shape
    tm = min(256, S)

    def epi(acc, ex, out, i, j):
        x_ref, g_ref, b_ref = ex
        x1_ref, x1b_ref, xh_ref, rs_ref = out
        y, xhat, rstd = _layer_norm_rows(DEEPNORM_ALPHA * x_ref[...] + acc, g_ref[...], b_ref[...])
        x1_ref[...] = y
        x1b_ref[...] = y.astype(BF16)
        xh_ref[...] = xhat
        rs_ref[...] = jnp.broadcast_to(rstd, (tm, HEAD_DIM))

    row = lambda i, j, k: (i, 0)
    vec = lambda i, j, k: (0, 0)
    return _mm("out_proj_ln", merged, w_out, "nn", (tm, D, D),
               [((S, D), F32, (tm, D), row), ((S, D), BF16, (tm, D), row), ((S, D), F32, (tm, D), row),
                ((S, HEAD_DIM), F32, (tm, HEAD_DIM), row)], epi,
               extras=[(x, (tm, D), row), (g, (1, D), vec), (b, (1, D), vec)])


def _ffn_up(x1b, w1):
    S, D = x1b.shape
    F = w1.shape[1]
    tm, tn = min(1024, S), min(2048, F)

    def epi(acc, ex, out, i, j):
        r = jnp.maximum(acc, 0.0)
        out[0][...] = (r * r).astype(BF16)

    (a,) = _mm("ffn_up", x1b, w1, "nn", (tm, tn, 512), [_tile_out((S, F), BF16, tm, tn)], epi)
    return a


def _residual_matmul(name, a, w, form, resid, after=None):
    S, D = resid.shape
    tm, tn = min(1024, S), min(2048, D)

    def epi(acc, ex, out, i, j):
        out[0][...] = DEEPNORM_ALPHA * ex[0][...] + acc

    (z,) = _mm(name, a, w, form, (tm, tn, 512), [_tile_out((S, D), F32, tm, tn)], epi,
               extras=[(resid, (tm, tn), lambda i, j, k: (i, j))], after=after)
    return z


def _row_kernel(name, body, row_inputs, vec_inputs, row_outputs, sum_widths, tr):
    S = row_inputs[0].shape[0]
    row = lambda w: pl.BlockSpec((tr, w), lambda i: (i, 0))
    vec = lambda w: pl.BlockSpec((1, w), lambda i: (0, 0))

    def wrapped(*refs):
        body(pl.program_id(0), *refs)

    return pl.pallas_call(
        wrapped, name=name, grid=(S // tr,),
        in_specs=[row(t.shape[1]) for t in row_inputs] + [vec(t.shape[1]) for t in vec_inputs],
        out_specs=[row(w) for w, _ in row_outputs] + [vec(w) for w in sum_widths],
        out_shape=([jax.ShapeDtypeStruct((S, w), dt) for w, dt in row_outputs]
                   + [jax.ShapeDtypeStruct((1, w), F32) for w in sum_widths]),
        compiler_params=pltpu.CompilerParams(dimension_semantics=("arbitrary",),
                                             vmem_limit_bytes=_vmem_limit(40 << 20)),
    )(*row_inputs, *vec_inputs)


def _ln_loss_bwd(z2, g, b, target):
    S, D = z2.shape

    def body(i, z_ref, t_ref, g_ref, b_ref, dz_ref, dzb_ref, dg_ref, db_ref, loss_ref):
        gv = g_ref[...]
        y, xhat, rstd = _layer_norm_rows(z_ref[...], gv, b_ref[...])
        err = y - t_ref[...]
        loss = 0.5 * jnp.sum(jnp.mean(err * err, axis=-1, keepdims=True), axis=0, keepdims=True)
        dy = err * (1.0 / D)
        dz = _layer_norm_bwd(dy, xhat, rstd, gv)
        dz_ref[...] = dz
        dzb_ref[...] = dz.astype(BF16)
        _accumulate_rows(dg_ref, jnp.sum(dy * xhat, axis=0, keepdims=True), i)
        _accumulate_rows(db_ref, jnp.sum(dy, axis=0, keepdims=True), i)
        _accumulate_rows(loss_ref, jnp.broadcast_to(loss, (1, HEAD_DIM)), i)

    return _row_kernel("ln_loss_bwd", body, [z2, target], [g, b], [(D, F32), (D, BF16)], [D, D, HEAD_DIM],
                       min(256, S))


def _ln_bwd(dy, xhat, rstd, g):
    S, D = dy.shape

    def body(i, dy_ref, xh_ref, rs_ref, g_ref, dz_ref, dzb_ref, dg_ref, db_ref):
        dyv, xhat_v = dy_ref[...], xh_ref[...]
        dz = _layer_norm_bwd(dyv, xhat_v, rs_ref[:, :1], g_ref[...])
        dz_ref[...] = dz
        dzb_ref[...] = dz.astype(BF16)
        _accumulate_rows(dg_ref, jnp.sum(dyv * xhat_v, axis=0, keepdims=True), i)
        _accumulate_rows(db_ref, jnp.sum(dyv, axis=0, keepdims=True), i)

    return _row_kernel("ln_bwd", body, [dy, xhat, rstd], [g], [(D, F32), (D, BF16)], [D, D], min(256, S))


def _grad_weight(name, act, cot):
    M, N = act.shape[1], cot.shape[1]
    tm, tn = min(1024, M), min(2048, N)

    def epi(acc, ex, out, i, j):
        out[0][...] = acc.astype(BF16)

    (g,) = _mm(name, act, cot, "tn", (tm, tn, 512), [_tile_out((M, N), BF16, tm, tn)], epi)
    return g


def _ffn_down_bwd(dz2b, w2, a, after=None):
    S, D = dz2b.shape
    F = w2.shape[0]
    tm, tn = min(1024, S), min(2048, F)

    def epi(acc, ex, out, i, j):
        out[0][...] = (acc * (2.0 * jnp.sqrt(ex[0][...].astype(F32)))).astype(BF16)

    (dh1,) = _mm("ffn_down_bwd", dz2b, w2, "nt", (tm, tn, 512), [_tile_out((S, F), BF16, tm, tn)], epi,
                 extras=[(a, (tm, tn), lambda i, j, k: (i, j))], after=after)
    return dh1


def _out_proj_bwd(dz1b, w_out, h, y_attn, y_pool, gate_col0, after=None):
    S, D = dz1b.shape
    W = h.shape[1]
    tm = min(256, S)
    assert gate_col0 == 2 * D and W == 4 * D

    def epi(acc, ex, out, i, j):
        gates_ref, ya_ref, yp_ref = ex
        dya_ref, dyp_ref, dh_ref = out
        sa = jax.nn.sigmoid(gates_ref[:, :D].astype(F32))
        sp = jax.nn.sigmoid(gates_ref[:, D:].astype(F32))
        dya_ref[...] = (acc * sa).astype(BF16)
        dyp_ref[...] = (acc * sp).astype(BF16)
        dh_ref[:, :D] = (acc * ya_ref[...].astype(F32) * (sa * (1.0 - sa))).astype(BF16)
        dh_ref[:, D:] = (acc * yp_ref[...].astype(F32) * (sp * (1.0 - sp))).astype(BF16)

    row = lambda i, j, k: (i, 0)
    return _mm("out_proj_bwd", dz1b, w_out, "nt", (tm, D, D),
               [((S, D), BF16, (tm, D), row), ((S, D), BF16, (tm, D), row),
                ((S, W), BF16, (tm, 2 * D), lambda i, j, k: (i, 1))], epi,
               extras=[(h, (tm, 2 * D), lambda i, j, k: (i, 1)), (y_attn, (tm, D), row), (y_pool, (tm, D), row)],
               after=after)


def _branch_attn_bwd(dy_attn, w_ba, o_attn, l_tot, after=None):
    S, D = dy_attn.shape
    aw = w_ba.shape[0]
    n_heads = aw // HEAD_DIM
    tm = min(512, S)

    def epi(acc, ex, out, i, j):
        do_ref, st_ref = out
        do_ref[...] = acc.astype(BF16)
        o = ex[0][...].astype(F32)
        stats = ex[1][...]
        for hd in range(n_heads):
            sl = slice(hd * HEAD_DIM, (hd + 1) * HEAD_DIM)
            stats = _put_column(stats, n_heads + hd, jnp.sum(acc[:, sl] * o[:, sl], axis=-1, keepdims=True))
        st_ref[...] = stats

    row = lambda i, j, k: (i, 0)
    return _mm("branch_attn_bwd", dy_attn, w_ba, "nt", (tm, aw, D),
               [((S, aw), BF16, (tm, aw), row), ((S, HEAD_DIM), F32, (tm, HEAD_DIM), row)], epi,
               extras=[(o_attn, (tm, aw), row), (l_tot, (tm, HEAD_DIM), row)], after=after)


def _branch_pool_bwd(dy_pool, w_bp, y_pre, pool_scale):
    S, D = dy_pool.shape
    pw = w_bp.shape[0]
    tm = min(512, S)

    def epi(acc, ex, out, i, j):
        y_ref, sc_ref = ex
        dyp_ref, dsc_ref = out
        dyp_ref[...] = (acc * sc_ref[...]).astype(BF16)
        _accumulate_rows(dsc_ref, jnp.sum(acc * y_ref[...].astype(F32), axis=0, keepdims=True), i)

    row = lambda i, j, k: (i, 0)
    return _mm("branch_pool_bwd", dy_pool, w_bp, "nt", (tm, pw, D),
               [((S, pw), BF16, (tm, pw), row), _row_sum_out(pw)], epi,
               extras=[(y_pre, (tm, pw), row), (pool_scale, (1, pw), lambda i, j, k: (0, 0))],
               sequential=True)


def _pool_bwd(dh, dy_pre, p, w_pool, pw, u_col_block):
    S, W = dh.shape
    n_groups = len(POOL_WINDOWS)
    gw = pw // n_groups
    tm = min(512, S)
    n_tiles = S // tm
    halo_per_tile = tm // POOL_HALO
    n_halo_blocks = S // POOL_HALO

    def body(dh_in_ref, dyc_ref, dyh_ref, p_ref, w_ref, dh_ref, dw_ref):
        del dh_in_ref
        i = pl.program_id(0)
        t_cur = i * tm + lax.broadcasted_iota(jnp.int32, (tm, 1), 0)
        t_halo = (i + 1) * tm + lax.broadcasted_iota(jnp.int32, (POOL_HALO, 1), 0)
        for g, w in enumerate(POOL_WINDOWS):
            sl = slice(g * gw, (g + 1) * gw)
            wg = w_ref[g]
            dyc = dyc_ref[:, sl]
            dyh = jnp.where(i < n_tiles - 1, dyh_ref[:, sl], jnp.zeros((POOL_HALO, gw), BF16))
            dp_cur = _dot_nt(dyc, wg)
            dp_halo = _dot_nt(dyh, wg)
            dpc_cur = (dp_cur / jnp.minimum(t_cur + 1, w).astype(F32)).astype(BF16)
            dpc_halo = (dp_halo / jnp.minimum(t_halo + 1, w).astype(F32)).astype(BF16)
            du = (_dot_nn(_band(tm, tm, w, 0, True), dpc_cur)
                  + _dot_nn(_band(tm, POOL_HALO, w, -tm, True), dpc_halo) - dp_cur)
            dh_ref[:, sl] = du.astype(BF16)
            dw = _dot_tn(p_ref[:, sl], dyc)

            @pl.when(i == 0)
            def _():
                dw_ref[g] = dw

            @pl.when(i > 0)
            def _():
                dw_ref[g] += dw

    row = pl.BlockSpec((tm, pw), lambda i: (i, 0))
    dh_new, dw_pool = pl.pallas_call(
        body, name="pool_bwd", grid=(n_tiles,),
        in_specs=[pl.BlockSpec(memory_space=pl.ANY), row,
                  pl.BlockSpec((POOL_HALO, pw), lambda i: (jnp.minimum((i + 1) * halo_per_tile, n_halo_blocks - 1), 0)),
                  row, pl.BlockSpec((n_groups, gw, gw), lambda i: (0, 0, 0))],
        out_specs=[pl.BlockSpec((tm, pw), lambda i: (i, u_col_block)),
                   pl.BlockSpec((n_groups, gw, gw), lambda i: (0, 0, 0))],
        out_shape=[jax.ShapeDtypeStruct((S, W), BF16), jax.ShapeDtypeStruct((n_groups, gw, gw), F32)],
        input_output_aliases={0: 0},
        compiler_params=pltpu.CompilerParams(dimension_semantics=("arbitrary",),
                                             vmem_limit_bytes=_vmem_limit(24 << 20)),
    )(dh, dy_pre, dy_pre, p, w_pool)
    return dh_new, dw_pool


def _attn_bwd(qkv, d_out, stats, d, aw):
    _, rows, _ = qkv.shape
    n_heads = aw // HEAD_DIM
    nb = rows // SUB_BLOCK
    n_blocks = d * nb

    def body(q_ref, kp_ref, kc_ref, vp_ref, vc_ref, do_ref, st_ref, dq_ref, dk_ref, dv_ref,
             carry_k, carry_v, s_buf, dp_buf, p_buf, ds_buf):
        step = pl.program_id(0)

        @pl.when(step == 0)
        def _():
            carry_k[...] = jnp.zeros_like(carry_k)
            carry_v[...] = jnp.zeros_like(carry_v)

        @pl.when(step < n_blocks)
        def _():
            mask = _attn_mask(step % nb)
            st = st_ref[...]
            lo, hi = slice(0, SUB_BLOCK), slice(SUB_BLOCK, 2 * SUB_BLOCK)
            for hd in range(n_heads):
                sl = slice(hd * HEAD_DIM, (hd + 1) * HEAD_DIM)
                q, do = q_ref[:, sl], do_ref[:, sl]
                s_buf[hd, :, lo] = _dot_nt(q, kp_ref[:, sl])
                s_buf[hd, :, hi] = _dot_nt(q, kc_ref[:, sl])
                dp_buf[hd, :, lo] = _dot_nt(do, vp_ref[:, sl])
                dp_buf[hd, :, hi] = _dot_nt(do, vc_ref[:, sl])
            for hd in range(n_heads):
                lt, dl = st[:, hd:hd + 1], st[:, n_heads + hd:n_heads + hd + 1]
                p = jnp.where(mask, jnp.exp(jnp.where(mask, s_buf[hd] * ATTN_SCALE - lt, NEG_BIG)), 0.0)
                p_buf[hd] = p.astype(BF16)
                ds_buf[hd] = (p * (dp_buf[hd] - dl) * ATTN_SCALE).astype(BF16)
            for hd in range(n_heads):
                sl = slice(hd * HEAD_DIM, (hd + 1) * HEAD_DIM)
                q, do = q_ref[:, sl], do_ref[:, sl]
                dq_ref[:, sl] = (_dot_nn(ds_buf[hd, :, lo], kp_ref[:, sl])
                                 + _dot_nn(ds_buf[hd, :, hi], kc_ref[:, sl])).astype(BF16)
                dk_ref[:, sl] = (carry_k[:, sl] + _dot_tn(ds_buf[hd, :, lo], q)).astype(BF16)
                dv_ref[:, sl] = (carry_v[:, sl] + _dot_tn(p_buf[hd, :, lo], do)).astype(BF16)
                carry_k[:, sl] = _dot_tn(ds_buf[hd, :, hi], q)
                carry_v[:, sl] = _dot_tn(p_buf[hd, :, hi], do)

        @pl.when(step == n_blocks)
        def _():
            dk_ref[...] = carry_k[...].astype(BF16)
            dv_ref[...] = carry_v[...].astype(BF16)

    def cur(step):
        return jnp.minimum(step, n_blocks - 1)

    def qkv_spec(col, prev):
        if prev:
            return pl.BlockSpec((SUB_BLOCK, aw), lambda s: (jnp.maximum(cur(s) - 1, 0), col))
        return pl.BlockSpec((SUB_BLOCK, aw), lambda s: (cur(s), col))

    def at_cur(w):
        return pl.BlockSpec((SUB_BLOCK, w), lambda s: (cur(s), 0))

    finished = pl.BlockSpec((SUB_BLOCK, aw), lambda s: (jnp.maximum(s - 1, 0), 0))
    pair = (n_heads, SUB_BLOCK, 2 * SUB_BLOCK)
    flat = lambda t: t.reshape(d * rows, t.shape[-1])
    qkv2 = flat(qkv)
    outs = pl.pallas_call(
        body, name=f"attn_bwd_d{d}", grid=(n_blocks + 1,),
        in_specs=[qkv_spec(0, False), qkv_spec(1, True), qkv_spec(1, False), qkv_spec(2, True), qkv_spec(2, False),
                  at_cur(aw), at_cur(HEAD_DIM)],
        out_specs=[at_cur(aw), finished, finished],
        out_shape=[jax.ShapeDtypeStruct((d * rows, aw), BF16)] * 3,
        scratch_shapes=[pltpu.VMEM((SUB_BLOCK, aw), F32), pltpu.VMEM((SUB_BLOCK, aw), F32),
                        pltpu.VMEM(pair, F32), pltpu.VMEM(pair, F32), pltpu.VMEM(pair, BF16), pltpu.VMEM(pair, BF16)],
        compiler_params=pltpu.CompilerParams(dimension_semantics=("arbitrary",),
                                             vmem_limit_bytes=_vmem_limit(24 << 20)),
    )(qkv2, qkv2, qkv2, qkv2, qkv2, flat(d_out), flat(stats))
    return [t.reshape(d, rows, aw) for t in outs]


def _attn_bwd_finish(dh, per_pattern, cos2, sin_bwd, aw):
    S, W = dh.shape
    n_heads = aw // HEAD_DIM
    n_pat = len(DILATIONS)

    def body(*refs):
        grad_refs = refs[1:1 + 3 * n_pat]
        cos_ref, sin_ref = refs[1 + 3 * n_pat], refs[2 + 3 * n_pat]
        out_ref = refs[3 + 3 * n_pat]
        perms = {d: _perm_matrix(d, False) for d in DILATIONS if d > 1}
        totals = []
        for which in range(3):
            tot = None
            for pi, d in enumerate(DILATIONS):
                g = grad_refs[which * n_pat + pi][...].reshape(PERM_ROWS, aw)
                g = _permute_rows(perms[d], g) if d > 1 else g.astype(F32)
                tot = g if tot is None else tot + g
            totals.append(tot)
        dq, dk, dv = totals
        c, s = cos_ref[...], sin_ref[...]
        for hd in range(n_heads):
            sl = slice(hd * HEAD_DIM, (hd + 1) * HEAD_DIM)
            out_ref[:, sl] = _rope_apply(dq[:, sl], c, s).astype(BF16)
            out_ref[:, aw + hd * HEAD_DIM:aw + (hd + 1) * HEAD_DIM] = _rope_apply(dk[:, sl], c, s).astype(BF16)
        out_ref[:, 2 * aw:] = dv.astype(BF16)

    grads = [pp[which] for which in range(3) for pp in per_pattern]
    rope_spec = pl.BlockSpec((PERM_ROWS, HEAD_DIM), lambda i: (i, 0))
    return pl.pallas_call(
        body, name="attn_bwd_finish", grid=(S // PERM_ROWS,),
        in_specs=([pl.BlockSpec(memory_space=pl.ANY)] + [_rm_block(d, aw) for d in DILATIONS] * 3
                  + [rope_spec, rope_spec]),
        out_specs=pl.BlockSpec((PERM_ROWS, 3 * aw), lambda i: (i, 0)),
        out_shape=jax.ShapeDtypeStruct((S, W), BF16),
        input_output_aliases={0: 0},
        compiler_params=pltpu.CompilerParams(dimension_semantics=("parallel",),
                                             vmem_limit_bytes=_vmem_limit(32 << 20)),
    )(dh, *grads, cos2, sin_bwd)


def _my_place():
    x, y, c = lax.axis_index("x"), lax.axis_index("y"), lax.axis_index("c")
    return x, y, c


def _flat(px, py, pc):
    return 4 * px + 2 * py + pc


def _shard_slice(ref, axis, idx, size):
    start = pl.multiple_of(idx * size, size)
    ix = [slice(None)] * len(ref.shape)
    ix[axis] = pl.ds(start, size)
    return ref.at[tuple(ix)]


def _all_gather_weights(shards, axes):
    n_w = len(shards)
    full_shapes = []
    for s, ax in zip(shards, axes):
        sh = list(s.shape)
        sh[ax] *= N_DEV
        full_shapes.append(tuple(sh))

    def body(*refs):
        src = refs[:n_w]
        dst = refs[n_w:2 * n_w]
        send_sems, recv_sems, local_sems = refs[2 * n_w:]
        x, y, c = _my_place()
        me, sibling = (x, y, c), (x, y, 1 - c)
        chips = [(1 - x, y), (x, 1 - y), (1 - x, 1 - y)]

        def place(w, dev):
            return _shard_slice(dst[w], axes[w], _flat(*dev), src[w].shape[axes[w]])

        def copy(w, k, block, to, from_src=False):
            return pltpu.make_async_remote_copy(
                src_ref=src[w] if from_src else place(w, block), dst_ref=place(w, block),
                send_sem=send_sems.at[w * 7 + k], recv_sem=recv_sems.at[w * 7 + k], device_id=to, device_id_type=MESH)

        mine, first, passed = [], [], []
        for w in range(n_w):
            cp = pltpu.make_async_copy(src[w], place(w, me), local_sems.at[w])
            cp.start()
            mine.append(cp)
            fw = [copy(w, 0, me, sibling, True)] + [copy(w, 1 + j, me, (*chip, c), True) for j, chip in enumerate(chips)]
            for cp in fw:
                cp.start()
            first += fw
        for w in range(n_w):
            for j, chip in enumerate(chips):
                copy(w, 1 + j, (*chip, c), me).wait_recv()
                cp = copy(w, 4 + j, (*chip, c), sibling)
                cp.start()
                passed.append(cp)
        for w in range(n_w):
            copy(w, 0, sibling, me).wait_recv()
            for j, chip in enumerate(chips):
                copy(w, 4 + j, (*chip, 1 - c), me).wait_recv()
        for cp in first + passed:
            cp.wait_send()
        for cp in mine:
            cp.wait()

    any_spec = pl.BlockSpec(memory_space=pl.ANY)
    return pl.pallas_call(
        body, name="all_gather_weights",
        in_specs=[any_spec] * n_w, out_specs=[any_spec] * n_w,
        out_shape=[jax.ShapeDtypeStruct(sh, s.dtype) for sh, s in zip(full_shapes, shards)],
        scratch_shapes=[pltpu.SemaphoreType.DMA((n_w * 7,)), pltpu.SemaphoreType.DMA((n_w * 7,)),
                        pltpu.SemaphoreType.DMA((n_w,))],
    )(*shards)


_HBM_SPEC = pl.BlockSpec(memory_space=pltpu.HBM)
_SEM_SPEC = pl.BlockSpec(memory_space=pltpu.SEMAPHORE)
_ANY_SPEC = pl.BlockSpec(memory_space=pl.ANY)
_N_PEER = N_DEV - 1


def _peer_of(x, y, c, r):
    return (x ^ ((r >> 2) & 1), y ^ ((r >> 1) & 1), c ^ (r & 1))


class _Exchange:
    def __init__(self, name, part, slot):
        self.name, self.part, self.slot = name, part, slot

    def _copy(self, w, r, src, land, send_sems, recv_sems, sending):
        x, y, c = _my_place()
        peer = _peer_of(x, y, c, r)
        return pltpu.make_async_remote_copy(
            src_ref=self.part(w, src, _flat(*peer)),
            dst_ref=self.slot(w, land, _flat(x, y, c) if sending else _flat(*peer)),
            send_sem=send_sems.at[w * _N_PEER + r - 1], recv_sem=recv_sems.at[w * _N_PEER + r - 1],
            device_id=peer, device_id_type=MESH)

    def start(self, srcs, lands, after=None):
        n = len(srcs)
        n_after = 0 if after is None else 1

        def body(*refs):
            src, land = refs[:n], refs[n:2 * n]
            outs = refs[2 * n + n_after:]
            send_sems, recv_sems, local_sems, token = outs[0], outs[1], outs[2], outs[3 + 2 * n]
            for w in range(n):
                self._own_copy(w, src[w], land[w], local_sems).start()
                for r in range(1, N_DEV):
                    self._copy(w, r, src[w], land[w], send_sems, recv_sems, True).start()
            token[...] = jnp.zeros_like(token)

        sems = pltpu.SemaphoreType.DMA((n * _N_PEER,))
        outs = pl.pallas_call(
            body, name=self.name + "_start",
            out_shape=(sems, sems, pltpu.SemaphoreType.DMA((n,)),
                       *[pltpu.HBM(t.shape, t.dtype) for t in list(srcs) + list(lands)],
                       jax.ShapeDtypeStruct((8, 128), F32)),
            in_specs=[_HBM_SPEC] * (2 * n) + [_ANY_SPEC] * n_after,
            out_specs=(_SEM_SPEC, _SEM_SPEC, _SEM_SPEC, *[_HBM_SPEC] * (2 * n), pl.BlockSpec(memory_space=pltpu.VMEM)),
            input_output_aliases={i: 3 + i for i in range(2 * n)},
            compiler_params=pltpu.CompilerParams(has_side_effects=pltpu.SideEffectType.DATAFLOW_SIDE_EFFECTING),
        )(*[pltpu.with_memory_space_constraint(t, pltpu.HBM) for t in list(srcs) + list(lands)],
          *([after] if n_after else []))
        return outs[0], outs[1], outs[2], outs[3:3 + n], outs[3 + n:3 + 2 * n], outs[3 + 2 * n]

    def _own_copy(self, w, src, land, local_sems):
        me = _flat(*_my_place())
        return pltpu.make_async_copy(self.part(w, src, me), self.slot(w, land, me), local_sems.at[w])

    def wait(self, started, after):
        send_sems, recv_sems, local_sems, srcs, lands, _ = started
        n = len(srcs)

        def body(*refs):
            src, land = refs[:n], refs[n:2 * n]
            s_sems, r_sems, l_sems = refs[2 * n], refs[2 * n + 1], refs[2 * n + 2]
            for w in range(n):
                self._own_copy(w, src[w], land[w], l_sems).wait()
                for r in range(1, N_DEV):
                    cp = self._copy(w, r, src[w], land[w], s_sems, r_sems, False)
                    cp.wait_send()
                    cp.wait_recv()

        outs = pl.pallas_call(
            body, name=self.name + "_wait",
            out_shape=[pltpu.HBM(t.shape, t.dtype) for t in list(srcs) + list(lands)],
            in_specs=[_HBM_SPEC] * (2 * n) + [_SEM_SPEC, _SEM_SPEC, _SEM_SPEC, _ANY_SPEC],
            out_specs=[_HBM_SPEC] * (2 * n),
            input_output_aliases={i: i for i in range(2 * n)},
            compiler_params=pltpu.CompilerParams(has_side_effects=pltpu.SideEffectType.DATAFLOW_SIDE_EFFECTING),
        )(*srcs, *lands, send_sems, recv_sems, local_sems, after)
        return outs[n:]


def _gather_exchange(name, axes, shard_sizes):
    return _Exchange(name, lambda w, src, dev: src,
                     lambda w, land, dev: _shard_slice(land, axes[w], dev, shard_sizes[w]))


def _scatter_exchange(name, axes, shard_sizes):
    def part(w, src, dev):
        return src if axes[w] is None else _shard_slice(src, axes[w], dev, shard_sizes[w])
    return _Exchange(name, part, lambda w, land, dev: land.at[dev])


def _adamw(name, partials, w, m, v):
    R, C = w.shape
    tr = R
    while tr * C * 4 > (1 << 20) and tr % 16 == 0:
        tr //= 2

    def body(p_ref, w_ref, m_ref, v_ref, g_ref, d_ref, nm_ref, nv_ref):
        g = p_ref[0].astype(F32)
        for jdev in range(1, N_DEV):
            g = g + p_ref[jdev].astype(F32)
        nm = ADAM_B1 * m_ref[...] + (1.0 - ADAM_B1) * g
        nv = ADAM_B2 * v_ref[...] + (1.0 - ADAM_B2) * (g * g)
        m_hat = nm / (1.0 - ADAM_B1 ** ADAM_STEP)
        v_hat = nv / (1.0 - ADAM_B2 ** ADAM_STEP)
        g_ref[...] = g
        d_ref[...] = -ADAM_LR * (m_hat / (jnp.sqrt(v_hat) + ADAM_EPS) + ADAM_WD * w_ref[...])
        nm_ref[...] = nm
        nv_ref[...] = nv

    spec = pl.BlockSpec((tr, C), lambda i: (i, 0))
    return pl.pallas_call(
        body, name=name, grid=(R // tr,),
        in_specs=[pl.BlockSpec((N_DEV, tr, C), lambda i: (0, i, 0)), spec, spec, spec],
        out_specs=[spec] * 4,
        out_shape=[jax.ShapeDtypeStruct((R, C), F32)] * 4,
        compiler_params=pltpu.CompilerParams(dimension_semantics=("parallel",),
                                             vmem_limit_bytes=_vmem_limit(24 << 20)),
    )(partials, w, m, v)


def _local_step(x, cos2, sin_fwd, sin_bwd, w_in, mix_weights, ffn_weights, pool_scale, g_mix, b_mix, g_ff, b_ff,
                target, send, start_after=None):
    S, D = x.shape
    aw = pw = D // 2
    u_col_block = 3
    gate_col0 = 4 * aw

    xb = x.astype(BF16)
    h = _in_proj(xb, w_in, cos2, sin_fwd, 2 * aw, after=start_after)
    dilated = [d for d in DILATIONS if d > 1]
    qkv = {1: h[None], **dict(zip(dilated, _to_residue_major("qkv_to_rm", h, 0, 3 * aw)))}
    fwd = [_attn_fwd(qkv[d], d, aw) for d in DILATIONS]
    o_attn, l_tot = _attn_combine([f[0] for f in fwd], [f[1] for f in fwd], aw)
    w_pool, w_ba, w_bp, w_out = mix_weights(o_attn)
    p, y_pre, pm = _pool_fwd(h, w_pool, pool_scale, pw, u_col_block)
    y_attn = _branch_attn(o_attn, w_ba)
    y_pool, merged = _branch_pool_merge(pm, w_bp, h, y_attn, gate_col0)
    w1, w2 = ffn_weights(merged)
    x1, x1b, xhat1, rstd1 = _out_proj_ln(merged, w_out, x, g_mix, b_mix)
    a = _ffn_up(x1b, w1)
    z2 = _residual_matmul("ffn_down", a, w2, "nn", x1)
    dz2, dz2b, dg_ff, db_ff, loss = _ln_loss_bwd(z2, g_ff, b_ff, target)

    tok = send("ff2", [_grad_weight("grad_w_ff2", a, dz2b)])
    dh1 = _ffn_down_bwd(dz2b, w2, a, after=tok)
    tok = send("ff1", [_grad_weight("grad_w_ff1", x1b, dh1)])
    dy1 = _residual_matmul("ffn_up_bwd", dh1, w1, "nt", dz2, after=tok)
    dz1, dz1b, dg_mix, db_mix = _ln_bwd(dy1, xhat1, rstd1, g_mix)
    tok = send("out", [_grad_weight("grad_w_out", merged, dz1b)])
    dy_attn, dy_pool, dh = _out_proj_bwd(dz1b, w_out, h, y_attn, y_pool, gate_col0, after=tok)
    tok = send("branch", [_grad_weight("grad_w_branch_attn", o_attn, dy_attn),
                          _grad_weight("grad_w_branch_pool", pm, dy_pool)])
    d_out, stats = _branch_attn_bwd(dy_attn, w_ba, o_attn, l_tot, after=tok)
    dy_pre, d_scale = _branch_pool_bwd(dy_pool, w_bp, y_pre, pool_scale)
    dh, dw_pool = _pool_bwd(dh, dy_pre, p, w_pool, pw, u_col_block)
    d_outs = {1: d_out[None], **dict(zip(dilated, _to_residue_major("dout_to_rm", d_out, 0, aw)))}
    statss = {1: stats[None], **dict(zip(dilated, _to_residue_major("stats_to_rm", stats, 0, HEAD_DIM)))}
    per_pattern = [_attn_bwd(qkv[d], d_outs[d], statss[d], d, aw) for d in DILATIONS]
    dh = _attn_bwd_finish(dh, per_pattern, cos2, sin_bwd, aw)
    small = jnp.concatenate((d_scale, dg_mix, db_mix, dg_ff, db_ff), axis=-1)
    tok = send("in", [_grad_weight("grad_w_in", xb, dh), dw_pool.astype(BF16),
                      small.reshape(small.shape[-1] // HEAD_DIM, HEAD_DIM)])
    grad_x = _residual_matmul("in_proj_bwd", dh, w_in, "nt", dz1, after=tok)
    return loss, grad_x


def _rope_tables(positions):
    half = HEAD_DIM // 2
    inv_freq = ROPE_THETA ** (-jnp.arange(half, dtype=F32) / half)
    ang = positions.astype(F32)[:, None] * inv_freq
    cos, sin = jnp.cos(ang), jnp.sin(ang)
    cos2 = jnp.concatenate([cos, cos], axis=-1)
    sin_fwd = jnp.concatenate([-sin, sin], axis=-1)
    return cos2, sin_fwd, -sin_fwd


def kernel(x, positions, w_in, w_pool, pool_scale, w_branch_attn, w_branch_pool, w_out, ln_mix_g, ln_mix_b, w_ff1, w_ff2, ln_ff_g, ln_ff_b, loss_target, m_w_in, m_w_pool, m_pool_scale, m_w_branch_attn, m_w_branch_pool, m_w_out, m_ln_mix_g, m_ln_mix_b, m_w_ff1, m_w_ff2, m_ln_ff_g, m_ln_ff_b, v_w_in, v_w_pool, v_pool_scale, v_w_branch_attn, v_w_branch_pool, v_w_out, v_ln_mix_g, v_ln_mix_b, v_w_ff1, v_w_ff2, v_ln_ff_g, v_ln_ff_b):
    big_w = (w_in[0], w_pool[0], w_branch_attn[0], w_branch_pool[0], w_out[0], w_ff1[0], w_ff2[0])
    big_m = (m_w_in[0], m_w_pool[0], m_w_branch_attn[0], m_w_branch_pool[0], m_w_out[0], m_w_ff1[0], m_w_ff2[0])
    big_v = (v_w_in[0], v_w_pool[0], v_w_branch_attn[0], v_w_branch_pool[0], v_w_out[0], v_w_ff1[0], v_w_ff2[0])
    shard_axes = (1, 1, 1, 1, 0, 1, 0)
    small_w = (pool_scale, ln_mix_g, ln_mix_b, ln_ff_g, ln_ff_b)
    small_m = (m_pool_scale, m_ln_mix_g, m_ln_mix_b, m_ln_ff_g, m_ln_ff_b)
    small_v = (v_pool_scale, v_ln_mix_g, v_ln_mix_b, v_ln_ff_g, v_ln_ff_b)

    names = ("w_in", "w_pool", "w_branch_attn", "w_branch_pool", "w_out", "w_ff1", "w_ff2")
    axis_of = dict(zip(names, shard_axes))
    shard_of = dict(zip(names, [w.astype(BF16) for w in big_w]))

    def full_buffer(n):
        s, ax = shard_of[n], axis_of[n]
        full = list(s.shape)
        full[ax] *= N_DEV
        return lax.empty(tuple(full), s.dtype)

    def gather_group(tag, group, after):
        ex = _gather_exchange(tag, [axis_of[n] for n in group], [shard_of[n].shape[axis_of[n]] for n in group])
        return ex, ex.start([shard_of[n] for n in group], [full_buffer(n) for n in group], after)

    (gw_in,) = _all_gather_weights([shard_of["w_in"]], (axis_of["w_in"],))
    mix_ex, mix_started = gather_group("gather_mix", ("w_pool", "w_branch_attn", "w_branch_pool", "w_out"), gw_in)
    ffn_ex, ffn_started = gather_group("gather_ffn", ("w_ff1", "w_ff2"), mix_started[-1])

    groups = {"ff2": ("w_ff2",), "ff1": ("w_ff1",), "out": ("w_out",),
              "branch": ("w_branch_attn", "w_branch_pool"), "in": ("w_in", "w_pool", "small")}
    sent = {}

    def send(key, grads_):
        axes = [axis_of.get(n) for n in groups[key]]
        sizes = [None if ax is None else g.shape[ax] // N_DEV for g, ax in zip(grads_, axes)]
        lands = []
        for g, ax, size in zip(grads_, axes, sizes):
            shard = list(g.shape)
            if ax is not None:
                shard[ax] = size
            lands.append(lax.empty((N_DEV, *shard), g.dtype))
        ex = _scatter_exchange("scatter_" + key, axes, sizes)
        sent[key] = (ex, ex.start(list(grads_), lands))
        return sent[key][1][-1]

    cos2, sin_fwd, sin_bwd = _rope_tables(positions[0])
    loss, grad_x = _local_step(
        x[0], cos2, sin_fwd, sin_bwd, gw_in, lambda after: mix_ex.wait(mix_started, after),
        lambda after: ffn_ex.wait(ffn_started, after), pool_scale, ln_mix_g, ln_mix_b, ln_ff_g, ln_ff_b,
        loss_target[0], send, start_after=ffn_started[-1])

    state = dict(zip(names, zip(big_w, big_m, big_v)))
    n_small = sum(w.shape[-1] for w in small_w)
    small_2d = (n_small // HEAD_DIM, HEAD_DIM)
    state["small"] = tuple(jnp.concatenate(t, axis=-1).reshape(small_2d) for t in (small_w, small_m, small_v))
    grads, deltas, new_ms, new_vs = {}, {}, {}, {}
    after = grad_x
    for key in ("ff2", "ff1", "out", "branch", "in"):
        ex, started = sent[key]
        for n, part in zip(groups[key], ex.wait(started, after)):
            w, m, v = state[n]
            r2 = (-1, w.shape[-1])
            w2d = w.reshape(r2)
            res = _adamw("adamw_" + n, part.reshape((N_DEV,) + w2d.shape), w2d, m.reshape(r2), v.reshape(r2))
            after = res[0]
            if n == "small":
                small_out = [t.reshape(1, n_small) for t in res]
            else:
                grads[n], deltas[n], new_ms[n], new_vs[n] = (t.reshape((1,) + w.shape) for t in res)
    small_names = ("pool_scale", "ln_mix_g", "ln_mix_b", "ln_ff_g", "ln_ff_b")
    off = 0
    for n, w in zip(small_names, small_w):
        width = w.shape[-1]
        grads[n], deltas[n], new_ms[n], new_vs[n] = (t[:, off:off + width] for t in small_out)
        off += width

    order = ("w_in", "w_pool", "pool_scale", "w_branch_attn", "w_branch_pool", "w_out", "ln_mix_g", "ln_mix_b",
             "w_ff1", "w_ff2", "ln_ff_g", "ln_ff_b")
    total_loss = lax.psum(loss[0, 0], ("x", "y", "c"))
    return (total_loss, grad_x[None], *[grads[n] for n in order], *[deltas[n] for n in order],
            *[new_ms[n] for n in order], *[new_vs[n] for n in order])
```

```python
import functools

import jax
import jax.numpy as jnp
from jax import lax
from jax.experimental import pallas as pl
from jax.experimental.pallas import tpu as pltpu

F32 = jnp.float32
BF16 = jnp.bfloat16

N_DEV = 8
HEAD_DIM = 128
SUB_BLOCK = 128
DILATIONS = (1, 4, 16)
POOL_WINDOWS = (2, 4, 8, 16)
MAX_POOL_WINDOW = 16
POOL_HALO = 128
PERM_ROWS = 512
K_TILE = 1024
LN_EPS = 1e-5
DEEPNORM_ALPHA = 2.0 ** 0.25
ROPE_THETA = 10000.0
ATTN_SCALE = HEAD_DIM ** -0.5
ADAM_LR, ADAM_B1, ADAM_B2, ADAM_EPS, ADAM_WD, ADAM_STEP = 0.001, 0.9, 0.999, 1e-08, 0.01, 10
NEG_BIG = -1e30
VMEM_CAP_V7X = 64 * 1024 * 1024
MESH = pl.DeviceIdType.MESH


def _vmem_limit(est_bytes):
    return int(min(max(est_bytes * 5 // 4 + (4 << 20), 16 << 20), VMEM_CAP_V7X - (6 << 20)))


def _nbytes(shape, dtype):
    n = 1
    for s in shape:
        n *= s
    return n * jnp.dtype(dtype).itemsize


def _mm(name, a, b, form, tiles, outs, epi, extras=(), sequential=False, after=None):
    tm, tn, tk = tiles
    if form == "nn":
        (M, K), (K2, N) = a.shape, b.shape
    elif form == "nt":
        (M, K), (N, K2) = a.shape, b.shape
    else:
        (K, M), (K2, N) = a.shape, b.shape
    assert K == K2, (name, a.shape, b.shape)
    tm, tn, tk = min(tm, M), min(tn, N), min(tk, K)
    assert M % tm == 0 and N % tn == 0 and K % tk == 0, (name, M, N, K, tm, tn, tk)
    grid = (M // tm, N // tn, K // tk)
    nk = grid[2]
    if form == "nn":
        a_spec = pl.BlockSpec((tm, tk), lambda i, j, k: (i, k))
        b_spec = pl.BlockSpec((tk, tn), lambda i, j, k: (k, j))
        contract = ((1,), (0,))
    elif form == "nt":
        a_spec = pl.BlockSpec((tm, tk), lambda i, j, k: (i, k))
        b_spec = pl.BlockSpec((tn, tk), lambda i, j, k: (j, k))
        contract = ((1,), (1,))
    else:
        a_spec = pl.BlockSpec((tk, tm), lambda i, j, k: (k, i))
        b_spec = pl.BlockSpec((tk, tn), lambda i, j, k: (k, j))
        contract = ((0,), (0,))
    n_ex, n_out = len(extras), len(outs)
    n_after = 0 if after is None else 1

    def body(a_ref, b_ref, *rest):
        ex_refs = rest[:n_ex]
        rest = rest[n_ex + n_after:]
        out_refs = rest[:n_out]
        i, j, k = pl.program_id(0), pl.program_id(1), pl.program_id(2)

        def prod():
            return lax.dot_general(a_ref[...].astype(BF16), b_ref[...].astype(BF16),
                                   (contract, ((), ())), preferred_element_type=F32)

        if nk == 1:
            epi(prod(), ex_refs, out_refs, i, j)
        else:
            acc = rest[n_out]

            @pl.when(k == 0)
            def _():
                acc[...] = prod()

            @pl.when(k > 0)
            def _():
                acc[...] += prod()

            @pl.when(k == nk - 1)
            def _():
                epi(acc[...], ex_refs, out_refs, i, j)

    est = 2 * (_nbytes(a_spec.block_shape, a.dtype) + _nbytes(b_spec.block_shape, b.dtype))
    est += sum(2 * _nbytes(bs, arr.dtype) for arr, bs, _ in extras)
    est += sum(2 * _nbytes(bs, dt) for _, dt, bs, _ in outs)
    est += 4 * tm * tn * 4
    sem = ("arbitrary",) * 3 if sequential else ("parallel", "parallel", "arbitrary")
    return pl.pallas_call(
        body, name=name, grid=grid,
        in_specs=([a_spec, b_spec] + [pl.BlockSpec(bs, im) for _, bs, im in extras]
                  + [pl.BlockSpec(memory_space=pl.ANY)] * n_after),
        out_specs=[pl.BlockSpec(bs, im) for _, _, bs, im in outs],
        out_shape=[jax.ShapeDtypeStruct(sh, dt) for sh, dt, _, _ in outs],
        scratch_shapes=[pltpu.VMEM((tm, tn), F32)] if nk > 1 else [],
        compiler_params=pltpu.CompilerParams(dimension_semantics=sem, vmem_limit_bytes=_vmem_limit(est)),
    )(a, b, *[arr for arr, _, _ in extras], *([after] if n_after else []))


def _tile_out(shape, dtype, tm, tn):
    return (shape, dtype, (tm, tn), lambda i, j, k: (i, j))


def _row_sum_out(width):
    return ((1, width), F32, (1, width), lambda i, j, k: (0, 0))


def _accumulate_rows(ref, value, i):
    @pl.when(i == 0)
    def _():
        ref[...] = value

    @pl.when(i > 0)
    def _():
        ref[...] += value


def _layer_norm_bwd(dy, xhat, rstd, g):
    dxh = dy * g
    m1 = jnp.mean(dxh, axis=-1, keepdims=True)
    m2 = jnp.mean(dxh * xhat, axis=-1, keepdims=True)
    return rstd * (dxh - m1 - xhat * m2)


def _rope_apply(t, cos2, sin_signed):
    return t * cos2 + pltpu.roll(t, HEAD_DIM // 2, axis=1) * sin_signed


def _in_proj(xb, w_in, cos2, sin_fwd, n_rope_cols, after=None):
    S, D = xb.shape
    W = w_in.shape[1]
    tm, tn = min(1024, S), min(2048, n_rope_cols)
    assert n_rope_cols % tn == 0
    n_rope_tiles = n_rope_cols // tn

    def epi(acc, ex, out, i, j):
        cos_ref, sin_ref = ex
        (h_ref,) = out

        @pl.when(j < n_rope_tiles)
        def _():
            c, s = cos_ref[...], sin_ref[...]
            for hd in range(tn // HEAD_DIM):
                sl = slice(hd * HEAD_DIM, (hd + 1) * HEAD_DIM)
                h_ref[:, sl] = _rope_apply(acc[:, sl], c, s).astype(BF16)

        @pl.when(j >= n_rope_tiles)
        def _():
            h_ref[...] = acc.astype(BF16)

    row = lambda i, j, k: (i, 0)
    (h,) = _mm("in_proj", xb, w_in, "nn", (tm, tn, K_TILE),
               [_tile_out((S, W), BF16, tm, tn)], epi,
               extras=[(cos2, (tm, HEAD_DIM), row), (sin_fwd, (tm, HEAD_DIM), row)], after=after)
    return h


def _attn_mask(mb):
    qi = lax.broadcasted_iota(jnp.int32, (SUB_BLOCK, 2 * SUB_BLOCK), 0)
    kj = lax.broadcasted_iota(jnp.int32, (SUB_BLOCK, 2 * SUB_BLOCK), 1)
    prev = jnp.logical_and(jnp.logical_and(kj < SUB_BLOCK, kj >= qi), mb > 0)
    cur = jnp.logical_and(kj >= SUB_BLOCK, kj - SUB_BLOCK <= qi)
    return jnp.logical_or(prev, cur)


def _dot_nt(a, b):
    return lax.dot_general(a, b, (((1,), (1,)), ((), ())), preferred_element_type=F32)


def _dot_tn(a, b):
    return lax.dot_general(a, b, (((0,), (0,)), ((), ())), preferred_element_type=F32)


def _dot_nn(a, b):
    return lax.dot_general(a, b, (((1,), (0,)), ((), ())), preferred_element_type=F32)


def _perm_matrix(d, to_residue_major):
    g = PERM_ROWS // d
    i = lax.broadcasted_iota(jnp.int32, (PERM_ROWS, PERM_ROWS), 0)
    j = lax.broadcasted_iota(jnp.int32, (PERM_ROWS, PERM_ROWS), 1)
    if to_residue_major:
        hit = j == (i % g) * d + i // g
    else:
        hit = j == (i % d) * g + i // d
    return hit.astype(BF16)


def _permute_rows(perm, x):
    if x.dtype == BF16:
        return _dot_nn(perm, x)
    hi = x.astype(BF16)
    r1 = x - hi.astype(F32)
    mid = r1.astype(BF16)
    lo = (r1 - mid.astype(F32)).astype(BF16)
    return (_dot_nn(perm, hi) + _dot_nn(perm, mid)) + _dot_nn(perm, lo)


def _rm_block(d, width):
    return pl.BlockSpec((d, PERM_ROWS // d, width), lambda i: (0, i, 0))


def _to_residue_major(name, x, col_block, width):
    S = x.shape[0]
    dils = [d for d in DILATIONS if d > 1]
    chunk = min(width, 1024)

    def body(x_ref, *out_refs):
        for d, o_ref in zip(dils, out_refs):
            perm = _perm_matrix(d, True)
            for c0 in range(0, width, chunk):
                cw = min(chunk, width - c0)
                y = _permute_rows(perm, x_ref[:, c0:c0 + cw])
                o_ref[:, :, c0:c0 + cw] = y.astype(x.dtype).reshape(d, PERM_ROWS // d, cw)

    return pl.pallas_call(
        body, name=name, grid=(S // PERM_ROWS,),
        in_specs=[pl.BlockSpec((PERM_ROWS, width), lambda i: (i, col_block))],
        out_specs=[_rm_block(d, width) for d in dils],
        out_shape=[jax.ShapeDtypeStruct((d, S // d, width), x.dtype) for d in dils],
        compiler_params=pltpu.CompilerParams(dimension_semantics=("parallel",),
                                             vmem_limit_bytes=_vmem_limit(32 << 20)),
    )(x)


def _qkv_specs(aw):
    def spec(col, prev):
        if prev:
            return pl.BlockSpec((None, SUB_BLOCK, aw), lambda r, mb: (r, jnp.maximum(mb - 1, 0), col))
        return pl.BlockSpec((None, SUB_BLOCK, aw), lambda r, mb: (r, mb, col))
    return [spec(0, False), spec(1, True), spec(1, False), spec(2, True), spec(2, False)]


def _put_column(tile, col, value):
    lane = lax.broadcasted_iota(jnp.int32, tile.shape, 1)
    return jnp.where(lane == col, value, tile)


def _attn_fwd(qkv, d, aw):
    _, rows, _ = qkv.shape
    n_heads = aw // HEAD_DIM
    nb = rows // SUB_BLOCK

    def body(q_ref, kp_ref, kc_ref, vp_ref, vc_ref, o_ref, lse_ref, s_buf, p_buf):
        mask = _attn_mask(pl.program_id(1))
        for hd in range(n_heads):
            sl = slice(hd * HEAD_DIM, (hd + 1) * HEAD_DIM)
            q = q_ref[:, sl]
            s_buf[hd, :, :SUB_BLOCK] = _dot_nt(q, kp_ref[:, sl])
            s_buf[hd, :, SUB_BLOCK:] = _dot_nt(q, kc_ref[:, sl])
        lse_tile = jnp.zeros((SUB_BLOCK, HEAD_DIM), F32)
        inv_tile = jnp.zeros((SUB_BLOCK, HEAD_DIM), F32)
        for hd in range(n_heads):
            s = jnp.where(mask, s_buf[hd] * ATTN_SCALE, NEG_BIG)
            m = jnp.max(s, axis=-1, keepdims=True)
            p = jnp.exp(s - m)
            l = jnp.sum(p, axis=-1, keepdims=True)
            p_buf[hd] = p.astype(BF16)
            lse_tile = _put_column(lse_tile, hd, m + jnp.log(l))
            inv_tile = _put_column(inv_tile, hd, 1.0 / l)
        lse_ref[...] = lse_tile
        for hd in range(n_heads):
            sl = slice(hd * HEAD_DIM, (hd + 1) * HEAD_DIM)
            o = _dot_nn(p_buf[hd, :, :SUB_BLOCK], vp_ref[:, sl]) + _dot_nn(p_buf[hd, :, SUB_BLOCK:], vc_ref[:, sl])
            o_ref[:, sl] = o * inv_tile[:, hd:hd + 1]

    return pl.pallas_call(
        body, name=f"attn_fwd_d{d}", grid=(d, nb),
        in_specs=_qkv_specs(aw),
        out_specs=[pl.BlockSpec((None, SUB_BLOCK, aw), lambda r, mb: (r, mb, 0)),
                   pl.BlockSpec((None, SUB_BLOCK, HEAD_DIM), lambda r, mb: (r, mb, 0))],
        out_shape=[jax.ShapeDtypeStruct((d, rows, aw), F32), jax.ShapeDtypeStruct((d, rows, HEAD_DIM), F32)],
        scratch_shapes=[pltpu.VMEM((n_heads, SUB_BLOCK, 2 * SUB_BLOCK), F32),
                        pltpu.VMEM((n_heads, SUB_BLOCK, 2 * SUB_BLOCK), BF16)],
        compiler_params=pltpu.CompilerParams(dimension_semantics=("parallel", "parallel"),
                                             vmem_limit_bytes=_vmem_limit(16 << 20)),
    )(qkv, qkv, qkv, qkv, qkv)


def _attn_combine(outs, lses, aw):
    S = outs[0].shape[1]
    n_heads = aw // HEAD_DIM
    n_pat = len(DILATIONS)

    def body(*refs):
        o_refs, l_refs = refs[:n_pat], refs[n_pat:2 * n_pat]
        o_ref, lt_ref = refs[2 * n_pat], refs[2 * n_pat + 1]
        o_nat, l_nat = [], []
        for d, o_r, l_r in zip(DILATIONS, o_refs, l_refs):
            o_p = o_r[...].reshape(PERM_ROWS, aw)
            l_p = l_r[...].reshape(PERM_ROWS, HEAD_DIM)
            if d > 1:
                perm = _perm_matrix(d, False)
                o_p, l_p = _permute_rows(perm, o_p), _permute_rows(perm, l_p)
            o_nat.append(o_p)
            l_nat.append(l_p)
        mx = functools.reduce(jnp.maximum, l_nat)
        es = [jnp.exp(l_p - mx) for l_p in l_nat]
        den = functools.reduce(jnp.add, es)
        lt_ref[...] = mx + jnp.log(den)
        ws = [e / den for e in es]
        for hd in range(n_heads):
            sl = slice(hd * HEAD_DIM, (hd + 1) * HEAD_DIM)
            o = ws[0][:, hd:hd + 1] * o_nat[0][:, sl]
            for pi in range(1, n_pat):
                o = o + ws[pi][:, hd:hd + 1] * o_nat[pi][:, sl]
            o_ref[:, sl] = o.astype(BF16)

    return pl.pallas_call(
        body, name="attn_combine", grid=(S // PERM_ROWS,),
        in_specs=[_rm_block(d, aw) for d in DILATIONS] + [_rm_block(d, HEAD_DIM) for d in DILATIONS],
        out_specs=[pl.BlockSpec((PERM_ROWS, aw), lambda i: (i, 0)), pl.BlockSpec((PERM_ROWS, HEAD_DIM), lambda i: (i, 0))],
        out_shape=[jax.ShapeDtypeStruct((S, aw), BF16), jax.ShapeDtypeStruct((S, HEAD_DIM), F32)],
        compiler_params=pltpu.CompilerParams(dimension_semantics=("parallel",),
                                             vmem_limit_bytes=_vmem_limit(40 << 20)),
    )(*outs, *lses)


def _band(tm, width, w, row_offset, transpose):
    t = lax.broadcasted_iota(jnp.int32, (tm, width), 0)
    u = lax.broadcasted_iota(jnp.int32, (tm, width), 1)
    dist = (u - t - row_offset) if transpose else (t + row_offset - u)
    return jnp.logical_and(dist >= 0, dist < w).astype(BF16)


def _pool_fwd(h, w_pool, pool_scale, pw, u_col_block):
    S, W = h.shape
    n_groups = len(POOL_WINDOWS)
    gw = pw // n_groups
    tm = min(512, S)
    halo_per_tile = tm // POOL_HALO

    def body(uc_ref, uh_ref, w_ref, sc_ref, p_ref, y_ref, pm_ref):
        i = pl.program_id(0)
        t_abs = i * tm + lax.broadcasted_iota(jnp.int32, (tm, 1), 0)
        for g, w in enumerate(POOL_WINDOWS):
            sl = slice(g * gw, (g + 1) * gw)
            uc = uc_ref[:, sl]
            uh = jnp.where(i > 0, uh_ref[:, sl], jnp.zeros((POOL_HALO, gw), BF16))
            ssum = _dot_nn(_band(tm, tm, w, 0, False), uc) + _dot_nn(_band(tm, POOL_HALO, w, POOL_HALO, False), uh)
            cnt = jnp.minimum(t_abs + 1, w).astype(F32)
            p = (ssum / cnt - uc.astype(F32)).astype(BF16)
            y = _dot_nn(p, w_ref[g])
            p_ref[:, sl] = p
            y_ref[:, sl] = y.astype(BF16)
            pm_ref[:, sl] = (y * sc_ref[:, sl]).astype(BF16)

    row = pl.BlockSpec((tm, pw), lambda i: (i, 0))
    return pl.pallas_call(
        body, name="pool_fwd", grid=(S // tm,),
        in_specs=[pl.BlockSpec((tm, pw), lambda i: (i, u_col_block)),
                  pl.BlockSpec((POOL_HALO, pw), lambda i: (jnp.maximum(i * halo_per_tile - 1, 0), u_col_block)),
                  pl.BlockSpec((n_groups, gw, gw), lambda i: (0, 0, 0)),
                  pl.BlockSpec((1, pw), lambda i: (0, 0))],
        out_specs=[row, row, row],
        out_shape=[jax.ShapeDtypeStruct((S, pw), BF16)] * 3,
        compiler_params=pltpu.CompilerParams(dimension_semantics=("parallel",),
                                             vmem_limit_bytes=_vmem_limit(24 << 20)),
    )(h, h, w_pool, pool_scale)


def _branch_attn(o_attn, w_ba):
    S, _ = o_attn.shape
    D = w_ba.shape[1]
    tm, tn = min(1024, S), D

    def epi(acc, ex, out, i, j):
        out[0][...] = acc.astype(BF16)

    (y,) = _mm("branch_attn", o_attn, w_ba, "nn", (tm, tn, 1024), [_tile_out((S, D), BF16, tm, tn)], epi)
    return y


def _branch_pool_merge(pm, w_bp, h, y_attn, gate_col0):
    S, _ = pm.shape
    D = w_bp.shape[1]
    tm, tn = min(512, S), D
    ga0, gp0 = gate_col0 // tn, (gate_col0 + D) // tn

    def epi(acc, ex, out, i, j):
        ga_ref, gp_ref, ya_ref = ex
        yp_ref, mg_ref = out
        yp = acc.astype(BF16)
        yp_ref[...] = yp
        mg = (jax.nn.sigmoid(ga_ref[...].astype(F32)) * ya_ref[...].astype(F32)
              + jax.nn.sigmoid(gp_ref[...].astype(F32)) * acc)
        mg_ref[...] = mg.astype(BF16)

    y_pool, merged = _mm(
        "branch_pool_merge", pm, w_bp, "nn", (tm, tn, 1024),
        [_tile_out((S, D), BF16, tm, tn), _tile_out((S, D), BF16, tm, tn)], epi,
        extras=[(h, (tm, tn), lambda i, j, k: (i, ga0 + j)), (h, (tm, tn), lambda i, j, k: (i, gp0 + j)),
                (y_attn, (tm, tn), lambda i, j, k: (i, j))])
    return y_pool, merged


def _layer_norm_rows(z, g, b):
    mu = jnp.mean(z, axis=-1, keepdims=True)
    zc = z - mu
    var = jnp.mean(zc * zc, axis=-1, keepdims=True)
    rstd = lax.rsqrt(var + LN_EPS)
    xhat = zc * rstd
    return xhat * g + b, xhat, rstd


def _out_proj_ln(merged, w_out, x, g, b):
    S, D = x.shape
    tm = min(256, S)

    def epi(acc, ex, out, i, j):
        x_ref, g_ref, b_ref = ex
        x1_ref, x1b_ref, xh_ref, rs_ref = out
        y, xhat, rstd = _layer_norm_rows(DEEPNORM_ALPHA * x_ref[...] + acc, g_ref[...], b_ref[...])
        x1_ref[...] = y
        x1b_ref[...] = y.astype(BF16)
        xh_ref[...] = xhat
        rs_ref[...] = jnp.broadcast_to(rstd, (tm, HEAD_DIM))

    row = lambda i, j, k: (i, 0)
    vec = lambda i, j, k: (0, 0)
    return _mm("out_proj_ln", merged, w_out, "nn", (tm, D, D),
               [((S, D), F32, (tm, D), row), ((S, D), BF16, (tm, D), row), ((S, D), F32, (tm, D), row),
                ((S, HEAD_DIM), F32, (tm, HEAD_DIM), row)], epi,
               extras=[(x, (tm, D), row), (g, (1, D), vec), (b, (1, D), vec)])


def _ffn_up(x1b, w1):
    S, D = x1b.shape
    F = w1.shape[1]
    tm, tn = min(1024, S), min(2048, F)

    def epi(acc, ex, out, i, j):
        r = jnp.maximum(acc, 0.0)
        out[0][...] = (r * r).astype(BF16)

    (a,) = _mm("ffn_up", x1b, w1, "nn", (tm, tn, K_TILE), [_tile_out((S, F), BF16, tm, tn)], epi)
    return a


def _residual_matmul(name, a, w, form, resid, after=None):
    S, D = resid.shape
    tm, tn = min(1024, S), min(2048, D)

    def epi(acc, ex, out, i, j):
        out[0][...] = DEEPNORM_ALPHA * ex[0][...] + acc

    (z,) = _mm(name, a, w, form, (tm, tn, K_TILE), [_tile_out((S, D), F32, tm, tn)], epi,
               extras=[(resid, (tm, tn), lambda i, j, k: (i, j))], after=after)
    return z


def _row_kernel(name, body, row_inputs, vec_inputs, row_outputs, sum_widths, tr):
    S = row_inputs[0].shape[0]
    row = lambda w: pl.BlockSpec((tr, w), lambda i: (i, 0))
    vec = lambda w: pl.BlockSpec((1, w), lambda i: (0, 0))

    def wrapped(*refs):
        body(pl.program_id(0), *refs)

    return pl.pallas_call(
        wrapped, name=name, grid=(S // tr,),
        in_specs=[row(t.shape[1]) for t in row_inputs] + [vec(t.shape[1]) for t in vec_inputs],
        out_specs=[row(w) for w, _ in row_outputs] + [vec(w) for w in sum_widths],
        out_shape=([jax.ShapeDtypeStruct((S, w), dt) for w, dt in row_outputs]
                   + [jax.ShapeDtypeStruct((1, w), F32) for w in sum_widths]),
        compiler_params=pltpu.CompilerParams(dimension_semantics=("arbitrary",),
                                             vmem_limit_bytes=_vmem_limit(40 << 20)),
    )(*row_inputs, *vec_inputs)


def _ln_loss_bwd(z2, g, b, target):
    S, D = z2.shape

    def body(i, z_ref, t_ref, g_ref, b_ref, dz_ref, dzb_ref, dg_ref, db_ref, loss_ref):
        gv = g_ref[...]
        y, xhat, rstd = _layer_norm_rows(z_ref[...], gv, b_ref[...])
        err = y - t_ref[...]
        loss = 0.5 * jnp.sum(jnp.mean(err * err, axis=-1, keepdims=True), axis=0, keepdims=True)
        dy = err * (1.0 / D)
        dz = _layer_norm_bwd(dy, xhat, rstd, gv)
        dz_ref[...] = dz
        dzb_ref[...] = dz.astype(BF16)
        _accumulate_rows(dg_ref, jnp.sum(dy * xhat, axis=0, keepdims=True), i)
        _accumulate_rows(db_ref, jnp.sum(dy, axis=0, keepdims=True), i)
        _accumulate_rows(loss_ref, jnp.broadcast_to(loss, (1, HEAD_DIM)), i)

    return _row_kernel("ln_loss_bwd", body, [z2, target], [g, b], [(D, F32), (D, BF16)], [D, D, HEAD_DIM],
                       min(256, S))


def _ln_bwd(dy, xhat, rstd, g):
    S, D = dy.shape

    def body(i, dy_ref, xh_ref, rs_ref, g_ref, dz_ref, dzb_ref, dg_ref, db_ref):
        dyv, xhat_v = dy_ref[...], xh_ref[...]
        dz = _layer_norm_bwd(dyv, xhat_v, rs_ref[:, :1], g_ref[...])
        dz_ref[...] = dz
        dzb_ref[...] = dz.astype(BF16)
        _accumulate_rows(dg_ref, jnp.sum(dyv * xhat_v, axis=0, keepdims=True), i)
        _accumulate_rows(db_ref, jnp.sum(dyv, axis=0, keepdims=True), i)

    return _row_kernel("ln_bwd", body, [dy, xhat, rstd], [g], [(D, F32), (D, BF16)], [D, D], min(256, S))


def _grad_weight(name, act, cot):
    M, N = act.shape[1], cot.shape[1]
    tm, tn = min(1024, M), min(2048, N)

    def epi(acc, ex, out, i, j):
        out[0][...] = acc.astype(BF16)

    (g,) = _mm(name, act, cot, "tn", (tm, tn, K_TILE), [_tile_out((M, N), BF16, tm, tn)], epi)
    return g


def _ffn_down_bwd(dz2b, w2, a, after=None):
    S, D = dz2b.shape
    F = w2.shape[0]
    tm, tn = min(1024, S), min(2048, F)

    def epi(acc, ex, out, i, j):
        out[0][...] = (acc * (2.0 * jnp.sqrt(ex[0][...])).astype(F32)).astype(BF16)

    (dh1,) = _mm("ffn_down_bwd", dz2b, w2, "nt", (tm, tn, K_TILE), [_tile_out((S, F), BF16, tm, tn)], epi,
                 extras=[(a, (tm, tn), lambda i, j, k: (i, j))], after=after)
    return dh1


def _out_proj_bwd(dz1b, w_out, h, y_attn, y_pool, gate_col0, after=None):
    S, D = dz1b.shape
    W = h.shape[1]
    tm = min(256, S)
    assert gate_col0 == 2 * D and W == 4 * D

    def epi(acc, ex, out, i, j):
        gates_ref, ya_ref, yp_ref = ex
        dya_ref, dyp_ref, dh_ref = out
        sa = jax.nn.sigmoid(gates_ref[:, :D].astype(F32))
        sp = jax.nn.sigmoid(gates_ref[:, D:].astype(F32))
        dya_ref[...] = (acc * sa).astype(BF16)
        dyp_ref[...] = (acc * sp).astype(BF16)
        dh_ref[:, :D] = (acc * ya_ref[...].astype(F32) * (sa * (1.0 - sa))).astype(BF16)
        dh_ref[:, D:] = (acc * yp_ref[...].astype(F32) * (sp * (1.0 - sp))).astype(BF16)

    row = lambda i, j, k: (i, 0)
    return _mm("out_proj_bwd", dz1b, w_out, "nt", (tm, D, D),
               [((S, D), BF16, (tm, D), row), ((S, D), BF16, (tm, D), row),
                ((S, W), BF16, (tm, 2 * D), lambda i, j, k: (i, 1))], epi,
               extras=[(h, (tm, 2 * D), lambda i, j, k: (i, 1)), (y_attn, (tm, D), row), (y_pool, (tm, D), row)],
               after=after)


def _branch_attn_bwd(dy_attn, w_ba, o_attn, l_tot, after=None):
    S, D = dy_attn.shape
    aw = w_ba.shape[0]
    n_heads = aw // HEAD_DIM
    tm = min(512, S)

    def epi(acc, ex, out, i, j):
        do_ref, st_ref = out
        do_ref[...] = acc.astype(BF16)
        o = ex[0][...].astype(F32)
        stats = ex[1][...]
        for hd in range(n_heads):
            sl = slice(hd * HEAD_DIM, (hd + 1) * HEAD_DIM)
            stats = _put_column(stats, n_heads + hd, jnp.sum(acc[:, sl] * o[:, sl], axis=-1, keepdims=True))
        st_ref[...] = stats

    row = lambda i, j, k: (i, 0)
    return _mm("branch_attn_bwd", dy_attn, w_ba, "nt", (tm, aw, D),
               [((S, aw), BF16, (tm, aw), row), ((S, HEAD_DIM), F32, (tm, HEAD_DIM), row)], epi,
               extras=[(o_attn, (tm, aw), row), (l_tot, (tm, HEAD_DIM), row)], after=after)


def _branch_pool_bwd(dy_pool, w_bp, y_pre, pool_scale):
    S, D = dy_pool.shape
    pw = w_bp.shape[0]
    tm = min(512, S)

    def epi(acc, ex, out, i, j):
        y_ref, sc_ref = ex
        dyp_ref, dsc_ref = out
        dyp_ref[...] = (acc * sc_ref[...]).astype(BF16)
        _accumulate_rows(dsc_ref, jnp.sum(acc * y_ref[...].astype(F32), axis=0, keepdims=True), i)

    row = lambda i, j, k: (i, 0)
    return _mm("branch_pool_bwd", dy_pool, w_bp, "nt", (tm, pw, D),
               [((S, pw), BF16, (tm, pw), row), _row_sum_out(pw)], epi,
               extras=[(y_pre, (tm, pw), row), (pool_scale, (1, pw), lambda i, j, k: (0, 0))],
               sequential=True)


def _pool_bwd(dh, dy_pre, p, w_pool, pw, u_col_block):
    S, W = dh.shape
    n_groups = len(POOL_WINDOWS)
    gw = pw // n_groups
    tm = min(512, S)
    n_tiles = S // tm
    halo_per_tile = tm // POOL_HALO
    n_halo_blocks = S // POOL_HALO

    def body(dh_in_ref, dyc_ref, dyh_ref, p_ref, w_ref, dh_ref, dw_ref):
        del dh_in_ref
        i = pl.program_id(0)
        t_cur = i * tm + lax.broadcasted_iota(jnp.int32, (tm, 1), 0)
        t_halo = (i + 1) * tm + lax.broadcasted_iota(jnp.int32, (POOL_HALO, 1), 0)
        for g, w in enumerate(POOL_WINDOWS):
            sl = slice(g * gw, (g + 1) * gw)
            wg = w_ref[g]
            dyc = dyc_ref[:, sl]
            dyh = jnp.where(i < n_tiles - 1, dyh_ref[:, sl], jnp.zeros((POOL_HALO, gw), BF16))
            dp_cur = _dot_nt(dyc, wg)
            dp_halo = _dot_nt(dyh, wg)
            dpc_cur = (dp_cur / jnp.minimum(t_cur + 1, w).astype(F32)).astype(BF16)
            dpc_halo = (dp_halo / jnp.minimum(t_halo + 1, w).astype(F32)).astype(BF16)
            du = (_dot_nn(_band(tm, tm, w, 0, True), dpc_cur)
                  + _dot_nn(_band(tm, POOL_HALO, w, -tm, True), dpc_halo) - dp_cur)
            dh_ref[:, sl] = du.astype(BF16)
            dw = _dot_tn(p_ref[:, sl], dyc)

            @pl.when(i == 0)
            def _():
                dw_ref[g] = dw

            @pl.when(i > 0)
            def _():
                dw_ref[g] += dw

    row = pl.BlockSpec((tm, pw), lambda i: (i, 0))
    dh_new, dw_pool = pl.pallas_call(
        body, name="pool_bwd", grid=(n_tiles,),
        in_specs=[pl.BlockSpec(memory_space=pl.ANY), row,
                  pl.BlockSpec((POOL_HALO, pw), lambda i: (jnp.minimum((i + 1) * halo_per_tile, n_halo_blocks - 1), 0)),
                  row, pl.BlockSpec((n_groups, gw, gw), lambda i: (0, 0, 0))],
        out_specs=[pl.BlockSpec((tm, pw), lambda i: (i, u_col_block)),
                   pl.BlockSpec((n_groups, gw, gw), lambda i: (0, 0, 0))],
        out_shape=[jax.ShapeDtypeStruct((S, W), BF16), jax.ShapeDtypeStruct((n_groups, gw, gw), F32)],
        input_output_aliases={0: 0},
        compiler_params=pltpu.CompilerParams(dimension_semantics=("arbitrary",),
                                             vmem_limit_bytes=_vmem_limit(24 << 20)),
    )(dh, dy_pre, dy_pre, p, w_pool)
    return dh_new, dw_pool


def _attn_bwd(qkv, d_out, stats, d, aw):
    _, rows, _ = qkv.shape
    n_heads = aw // HEAD_DIM
    nb = rows // SUB_BLOCK
    n_blocks = d * nb

    def body(q_ref, kp_ref, kc_ref, vp_ref, vc_ref, do_ref, st_ref, dq_ref, dk_ref, dv_ref,
             carry_k, carry_v, s_buf, dp_buf, p_buf, ds_buf):
        step = pl.program_id(0)

        @pl.when(step == 0)
        def _():
            carry_k[...] = jnp.zeros_like(carry_k)
            carry_v[...] = jnp.zeros_like(carry_v)

        @pl.when(step < n_blocks)
        def _():
            mask = _attn_mask(step % nb)
            st = st_ref[...]
            lo, hi = slice(0, SUB_BLOCK), slice(SUB_BLOCK, 2 * SUB_BLOCK)
            for hd in range(n_heads):
                sl = slice(hd * HEAD_DIM, (hd + 1) * HEAD_DIM)
                q, do = q_ref[:, sl], do_ref[:, sl]
                s_buf[hd, :, lo] = _dot_nt(q, kp_ref[:, sl])
                s_buf[hd, :, hi] = _dot_nt(q, kc_ref[:, sl])
                dp_buf[hd, :, lo] = _dot_nt(do, vp_ref[:, sl])
                dp_buf[hd, :, hi] = _dot_nt(do, vc_ref[:, sl])
            for hd in range(n_heads):
                lt, dl = st[:, hd:hd + 1], st[:, n_heads + hd:n_heads + hd + 1]
                p = jnp.where(mask, jnp.exp(jnp.where(mask, s_buf[hd] * ATTN_SCALE - lt, NEG_BIG)), 0.0)
                p_buf[hd] = p.astype(BF16)
                ds_buf[hd] = (p * (dp_buf[hd] - dl) * ATTN_SCALE).astype(BF16)
            for hd in range(n_heads):
                sl = slice(hd * HEAD_DIM, (hd + 1) * HEAD_DIM)
                q, do = q_ref[:, sl], do_ref[:, sl]
                dq_ref[:, sl] = (_dot_nn(ds_buf[hd, :, lo], kp_ref[:, sl])
                                 + _dot_nn(ds_buf[hd, :, hi], kc_ref[:, sl])).astype(BF16)
                dk_ref[:, sl] = (carry_k[:, sl] + _dot_tn(ds_buf[hd, :, lo], q)).astype(BF16)
                dv_ref[:, sl] = (carry_v[:, sl] + _dot_tn(p_buf[hd, :, lo], do)).astype(BF16)
                carry_k[:, sl] = _dot_tn(ds_buf[hd, :, hi], q)
                carry_v[:, sl] = _dot_tn(p_buf[hd, :, hi], do)

        @pl.when(step == n_blocks)
        def _():
            dk_ref[...] = carry_k[...].astype(BF16)
            dv_ref[...] = carry_v[...].astype(BF16)

    def cur(step):
        return jnp.minimum(step, n_blocks - 1)

    def qkv_spec(col, prev):
        if prev:
            return pl.BlockSpec((SUB_BLOCK, aw), lambda s: (jnp.maximum(cur(s) - 1, 0), col))
        return pl.BlockSpec((SUB_BLOCK, aw), lambda s: (cur(s), col))

    def at_cur(w):
        return pl.BlockSpec((SUB_BLOCK, w), lambda s: (cur(s), 0))

    finished = pl.BlockSpec((SUB_BLOCK, aw), lambda s: (jnp.maximum(s - 1, 0), 0))
    pair = (n_heads, SUB_BLOCK, 2 * SUB_BLOCK)
    flat = lambda t: t.reshape(d * rows, t.shape[-1])
    qkv2 = flat(qkv)
    outs = pl.pallas_call(
        body, name=f"attn_bwd_d{d}", grid=(n_blocks + 1,),
        in_specs=[qkv_spec(0, False), qkv_spec(1, True), qkv_spec(1, False), qkv_spec(2, True), qkv_spec(2, False),
                  at_cur(aw), at_cur(HEAD_DIM)],
        out_specs=[at_cur(aw), finished, finished],
        out_shape=[jax.ShapeDtypeStruct((d * rows, aw), BF16)] * 3,
        scratch_shapes=[pltpu.VMEM((SUB_BLOCK, aw), F32), pltpu.VMEM((SUB_BLOCK, aw), F32),
                        pltpu.VMEM(pair, F32), pltpu.VMEM(pair, F32), pltpu.VMEM(pair, BF16), pltpu.VMEM(pair, BF16)],
        compiler_params=pltpu.CompilerParams(dimension_semantics=("arbitrary",),
                                             vmem_limit_bytes=_vmem_limit(24 << 20)),
    )(qkv2, qkv2, qkv2, qkv2, qkv2, flat(d_out), flat(stats))
    return [t.reshape(d, rows, aw) for t in outs]


def _attn_bwd_finish(dh, per_pattern, cos2, sin_bwd, aw):
    S, W = dh.shape
    n_heads = aw // HEAD_DIM
    n_pat = len(DILATIONS)

    def body(*refs):
        grad_refs = refs[1:1 + 3 * n_pat]
        cos_ref, sin_ref = refs[1 + 3 * n_pat], refs[2 + 3 * n_pat]
        out_ref = refs[3 + 3 * n_pat]
        perms = {d: _perm_matrix(d, False) for d in DILATIONS if d > 1}
        totals = []
        for which in range(3):
            tot = None
            for pi, d in enumerate(DILATIONS):
                g = grad_refs[which * n_pat + pi][...].reshape(PERM_ROWS, aw)
                g = _permute_rows(perms[d], g) if d > 1 else g.astype(F32)
                tot = g if tot is None else tot + g
            totals.append(tot)
        dq, dk, dv = totals
        c, s = cos_ref[...], sin_ref[...]
        for hd in range(n_heads):
            sl = slice(hd * HEAD_DIM, (hd + 1) * HEAD_DIM)
            out_ref[:, sl] = _rope_apply(dq[:, sl], c, s).astype(BF16)
            out_ref[:, aw + hd * HEAD_DIM:aw + (hd + 1) * HEAD_DIM] = _rope_apply(dk[:, sl], c, s).astype(BF16)
        out_ref[:, 2 * aw:] = dv.astype(BF16)

    grads = [pp[which] for which in range(3) for pp in per_pattern]
    rope_spec = pl.BlockSpec((PERM_ROWS, HEAD_DIM), lambda i: (i, 0))
    return pl.pallas_call(
        body, name="attn_bwd_finish", grid=(S // PERM_ROWS,),
        in_specs=([pl.BlockSpec(memory_space=pl.ANY)] + [_rm_block(d, aw) for d in DILATIONS] * 3
                  + [rope_spec, rope_spec]),
        out_specs=pl.BlockSpec((PERM_ROWS, 3 * aw), lambda i: (i, 0)),
        out_shape=jax.ShapeDtypeStruct((S, W), BF16),
        input_output_aliases={0: 0},
        compiler_params=pltpu.CompilerParams(dimension_semantics=("parallel",),
                                             vmem_limit_bytes=_vmem_limit(32 << 20)),
    )(dh, *grads, cos2, sin_bwd)


def _my_place():
    x, y, c = lax.axis_index("x"), lax.axis_index("y"), lax.axis_index("c")
    return x, y, c


def _flat(px, py, pc):
    return 4 * px + 2 * py + pc


def _shard_slice(ref, axis, idx, size):
    start = pl.multiple_of(idx * size, size)
    ix = [slice(None)] * len(ref.shape)
    ix[axis] = pl.ds(start, size)
    return ref.at[tuple(ix)]


def _all_gather_weights(shards, axes):
    n_w = len(shards)
    full_shapes = []
    for s, ax in zip(shards, axes):
        sh = list(s.shape)
        sh[ax] *= N_DEV
        full_shapes.append(tuple(sh))

    def body(*refs):
        src = refs[:n_w]
        dst = refs[n_w:2 * n_w]
        send_sems, recv_sems, local_sems = refs[2 * n_w:]
        x, y, c = _my_place()
        me, sibling = (x, y, c), (x, y, 1 - c)
        chips = [(1 - x, y), (x, 1 - y), (1 - x, 1 - y)]

        def place(w, dev):
            return _shard_slice(dst[w], axes[w], _flat(*dev), src[w].shape[axes[w]])

        def copy(w, k, block, to, from_src=False):
            return pltpu.make_async_remote_copy(
                src_ref=src[w] if from_src else place(w, block), dst_ref=place(w, block),
                send_sem=send_sems.at[w * 7 + k], recv_sem=recv_sems.at[w * 7 + k], device_id=to, device_id_type=MESH)

        mine, first, passed = [], [], []
        for w in range(n_w):
            cp = pltpu.make_async_copy(src[w], place(w, me), local_sems.at[w])
            cp.start()
            mine.append(cp)
            fw = [copy(w, 0, me, sibling, True)] + [copy(w, 1 + j, me, (*chip, c), True) for j, chip in enumerate(chips)]
            for cp in fw:
                cp.start()
            first += fw
        for w in range(n_w):
            for j, chip in enumerate(chips):
                copy(w, 1 + j, (*chip, c), me).wait_recv()
                cp = copy(w, 4 + j, (*chip, c), sibling)
                cp.start()
                passed.append(cp)
        for w in range(n_w):
            copy(w, 0, sibling, me).wait_recv()
            for j, chip in enumerate(chips):
                copy(w, 4 + j, (*chip, 1 - c), me).wait_recv()
        for cp in first + passed:
            cp.wait_send()
        for cp in mine:
            cp.wait()

    any_spec = pl.BlockSpec(memory_space=pl.ANY)
    return pl.pallas_call(
        body, name="all_gather_weights",
        in_specs=[any_spec] * n_w, out_specs=[any_spec] * n_w,
        out_shape=[jax.ShapeDtypeStruct(sh, s.dtype) for sh, s in zip(full_shapes, shards)],
        scratch_shapes=[pltpu.SemaphoreType.DMA((n_w * 7,)), pltpu.SemaphoreType.DMA((n_w * 7,)),
                        pltpu.SemaphoreType.DMA((n_w,))],
    )(*shards)


_HBM_SPEC = pl.BlockSpec(memory_space=pltpu.HBM)
_SEM_SPEC = pl.BlockSpec(memory_space=pltpu.SEMAPHORE)
_ANY_SPEC = pl.BlockSpec(memory_space=pl.ANY)
_N_PEER = N_DEV - 1


def _peer_of(x, y, c, r):
    return (x ^ ((r >> 2) & 1), y ^ ((r >> 1) & 1), c ^ (r & 1))


class _Exchange:
    def __init__(self, name, part, slot):
        self.name, self.part, self.slot = name, part, slot

    def _copy(self, w, r, src, land, send_sems, recv_sems, sending):
        x, y, c = _my_place()
        peer = _peer_of(x, y, c, r)
        return pltpu.make_async_remote_copy(
            src_ref=self.part(w, src, _flat(*peer)),
            dst_ref=self.slot(w, land, _flat(x, y, c) if sending else _flat(*peer)),
            send_sem=send_sems.at[w * _N_PEER + r - 1], recv_sem=recv_sems.at[w * _N_PEER + r - 1],
            device_id=peer, device_id_type=MESH)

    def start(self, srcs, lands, after=None):
        n = len(srcs)
        n_after = 0 if after is None else 1

        def body(*refs):
            src, land = refs[:n], refs[n:2 * n]
            outs = refs[2 * n + n_after:]
            send_sems, recv_sems, local_sems, token = outs[0], outs[1], outs[2], outs[3 + 2 * n]
            for w in range(n):
                self._own_copy(w, src[w], land[w], local_sems).start()
                for r in range(1, N_DEV):
                    self._copy(w, r, src[w], land[w], send_sems, recv_sems, True).start()
            token[...] = jnp.zeros_like(token)

        sems = pltpu.SemaphoreType.DMA((n * _N_PEER,))
        outs = pl.pallas_call(
            body, name=self.name + "_start",
            out_shape=(sems, sems, pltpu.SemaphoreType.DMA((n,)),
                       *[pltpu.HBM(t.shape, t.dtype) for t in list(srcs) + list(lands)],
                       jax.ShapeDtypeStruct((8, 128), F32)),
            in_specs=[_HBM_SPEC] * (2 * n) + [_ANY_SPEC] * n_after,
            out_specs=(_SEM_SPEC, _SEM_SPEC, _SEM_SPEC, *[_HBM_SPEC] * (2 * n), pl.BlockSpec(memory_space=pltpu.VMEM)),
            input_output_aliases={i: 3 + i for i in range(2 * n)},
            compiler_params=pltpu.CompilerParams(has_side_effects=pltpu.SideEffectType.DATAFLOW_SIDE_EFFECTING),
        )(*[pltpu.with_memory_space_constraint(t, pltpu.HBM) for t in list(srcs) + list(lands)],
          *([after] if n_after else []))
        return outs[0], outs[1], outs[2], outs[3:3 + n], outs[3 + n:3 + 2 * n], outs[3 + 2 * n]

    def _own_copy(self, w, src, land, local_sems):
        me = _flat(*_my_place())
        return pltpu.make_async_copy(self.part(w, src, me), self.slot(w, land, me), local_sems.at[w])

    def wait(self, started, after):
        send_sems, recv_sems, local_sems, srcs, lands, _ = started
        n = len(srcs)

        def body(*refs):
            src, land = refs[:n], refs[n:2 * n]
            s_sems, r_sems, l_sems = refs[2 * n], refs[2 * n + 1], refs[2 * n + 2]
            for w in range(n):
                self._own_copy(w, src[w], land[w], l_sems).wait()
                for r in range(1, N_DEV):
                    cp = self._copy(w, r, src[w], land[w], s_sems, r_sems, False)
                    cp.wait_send()
                    cp.wait_recv()

        outs = pl.pallas_call(
            body, name=self.name + "_wait",
            out_shape=[pltpu.HBM(t.shape, t.dtype) for t in list(srcs) + list(lands)],
            in_specs=[_HBM_SPEC] * (2 * n) + [_SEM_SPEC, _SEM_SPEC, _SEM_SPEC, _ANY_SPEC],
            out_specs=[_HBM_SPEC] * (2 * n),
            input_output_aliases={i: i for i in range(2 * n)},
            compiler_params=pltpu.CompilerParams(has_side_effects=pltpu.SideEffectType.DATAFLOW_SIDE_EFFECTING),
        )(*srcs, *lands, send_sems, recv_sems, local_sems, after)
        return outs[n:]


def _gather_exchange(name, axes, shard_sizes):
    return _Exchange(name, lambda w, src, dev: src,
                     lambda w, land, dev: _shard_slice(land, axes[w], dev, shard_sizes[w]))


def _scatter_exchange(name, axes, shard_sizes):
    def part(w, src, dev):
        return src if axes[w] is None else _shard_slice(src, axes[w], dev, shard_sizes[w])
    return _Exchange(name, part, lambda w, land, dev: land.at[dev])


def _adamw(name, partials, w, m, v):
    R, C = w.shape
    tr = R
    while tr * C * 4 > (1 << 20) and tr % 16 == 0:
        tr //= 2

    def body(p_ref, w_ref, m_ref, v_ref, g_ref, d_ref, nm_ref, nv_ref):
        g = p_ref[0].astype(F32)
        for jdev in range(1, N_DEV):
            g = g + p_ref[jdev].astype(F32)
        nm = ADAM_B1 * m_ref[...] + (1.0 - ADAM_B1) * g
        nv = ADAM_B2 * v_ref[...] + (1.0 - ADAM_B2) * (g * g)
        m_hat = nm / (1.0 - ADAM_B1 ** ADAM_STEP)
        v_hat = nv / (1.0 - ADAM_B2 ** ADAM_STEP)
        g_ref[...] = g
        d_ref[...] = -ADAM_LR * (m_hat / (jnp.sqrt(v_hat) + ADAM_EPS) + ADAM_WD * w_ref[...])
        nm_ref[...] = nm
        nv_ref[...] = nv

    spec = pl.BlockSpec((tr, C), lambda i: (i, 0))
    return pl.pallas_call(
        body, name=name, grid=(R // tr,),
        in_specs=[pl.BlockSpec((N_DEV, tr, C), lambda i: (0, i, 0)), spec, spec, spec],
        out_specs=[spec] * 4,
        out_shape=[jax.ShapeDtypeStruct((R, C), F32)] * 4,
        compiler_params=pltpu.CompilerParams(dimension_semantics=("parallel",),
                                             vmem_limit_bytes=_vmem_limit(24 << 20)),
    )(partials, w, m, v)


def _local_step(x, cos2, sin_fwd, sin_bwd, w_in, mix_weights, ffn_weights, pool_scale, g_mix, b_mix, g_ff, b_ff,
                target, send, start_after=None):
    S, D = x.shape
    aw = pw = D // 2
    u_col_block = 3
    gate_col0 = 4 * aw

    xb = x.astype(BF16)
    h = _in_proj(xb, w_in, cos2, sin_fwd, 2 * aw, after=start_after)
    dilated = [d for d in DILATIONS if d > 1]
    qkv = {1: h[None], **dict(zip(dilated, _to_residue_major("qkv_to_rm", h, 0, 3 * aw)))}
    fwd = [_attn_fwd(qkv[d], d, aw) for d in DILATIONS]
    o_attn, l_tot = _attn_combine([f[0] for f in fwd], [f[1] for f in fwd], aw)
    w_pool, w_ba, w_bp, w_out = mix_weights(o_attn)
    p, y_pre, pm = _pool_fwd(h, w_pool, pool_scale, pw, u_col_block)
    y_attn = _branch_attn(o_attn, w_ba)
    y_pool, merged = _branch_pool_merge(pm, w_bp, h, y_attn, gate_col0)
    w1, w2 = ffn_weights(merged)
    x1, x1b, xhat1, rstd1 = _out_proj_ln(merged, w_out, x, g_mix, b_mix)
    a = _ffn_up(x1b, w1)
    z2 = _residual_matmul("ffn_down", a, w2, "nn", x1)
    dz2, dz2b, dg_ff, db_ff, loss = _ln_loss_bwd(z2, g_ff, b_ff, target)

    tok = send("ff2", [_grad_weight("grad_w_ff2", a, dz2b)])
    dh1 = _ffn_down_bwd(dz2b, w2, a, after=tok)
    tok = send("ff1", [_grad_weight("grad_w_ff1", x1b, dh1)])
    dy1 = _residual_matmul("ffn_up_bwd", dh1, w1, "nt", dz2, after=tok)
    dz1, dz1b, dg_mix, db_mix = _ln_bwd(dy1, xhat1, rstd1, g_mix)
    tok = send("out", [_grad_weight("grad_w_out", merged, dz1b)])
    dy_attn, dy_pool, dh = _out_proj_bwd(dz1b, w_out, h, y_attn, y_pool, gate_col0, after=tok)
    tok = send("branch", [_grad_weight("grad_w_branch_attn", o_attn, dy_attn),
                          _grad_weight("grad_w_branch_pool", pm, dy_pool)])
    d_out, stats = _branch_attn_bwd(dy_attn, w_ba, o_attn, l_tot, after=tok)
    dy_pre, d_scale = _branch_pool_bwd(dy_pool, w_bp, y_pre, pool_scale)
    dh, dw_pool = _pool_bwd(dh, dy_pre, p, w_pool, pw, u_col_block)
    d_outs = {1: d_out[None], **dict(zip(dilated, _to_residue_major("dout_to_rm", d_out, 0, aw)))}
    statss = {1: stats[None], **dict(zip(dilated, _to_residue_major("stats_to_rm", stats, 0, HEAD_DIM)))}
    per_pattern = [_attn_bwd(qkv[d], d_outs[d], statss[d], d, aw) for d in DILATIONS]
    dh = _attn_bwd_finish(dh, per_pattern, cos2, sin_bwd, aw)
    small = jnp.concatenate((d_scale, dg_mix, db_mix, dg_ff, db_ff), axis=-1)
    tok = send("in", [_grad_weight("grad_w_in", xb, dh), dw_pool.astype(BF16),
                      small.reshape(small.shape[-1] // HEAD_DIM, HEAD_DIM)])
    grad_x = _residual_matmul("in_proj_bwd", dh, w_in, "nt", dz1, after=tok)
    return loss, grad_x


def _rope_tables(positions):
    half = HEAD_DIM // 2
    inv_freq = ROPE_THETA ** (-jnp.arange(half, dtype=F32) / half)
    ang = positions.astype(F32)[:, None] * inv_freq
    cos, sin = jnp.cos(ang), jnp.sin(ang)
    cos2 = jnp.concatenate([cos, cos], axis=-1)
    sin_fwd = jnp.concatenate([-sin, sin], axis=-1)
    return cos2, sin_fwd, -sin_fwd


def kernel(x, positions, w_in, w_pool, pool_scale, w_branch_attn, w_branch_pool, w_out, ln_mix_g, ln_mix_b, w_ff1, w_ff2, ln_ff_g, ln_ff_b, loss_target, m_w_in, m_w_pool, m_pool_scale, m_w_branch_attn, m_w_branch_pool, m_w_out, m_ln_mix_g, m_ln_mix_b, m_w_ff1, m_w_ff2, m_ln_ff_g, m_ln_ff_b, v_w_in, v_w_pool, v_pool_scale, v_w_branch_attn, v_w_branch_pool, v_w_out, v_ln_mix_g, v_ln_mix_b, v_w_ff1, v_w_ff2, v_ln_ff_g, v_ln_ff_b):
    big_w = (w_in[0], w_pool[0], w_branch_attn[0], w_branch_pool[0], w_out[0], w_ff1[0], w_ff2[0])
    big_m = (m_w_in[0], m_w_pool[0], m_w_branch_attn[0], m_w_branch_pool[0], m_w_out[0], m_w_ff1[0], m_w_ff2[0])
    big_v = (v_w_in[0], v_w_pool[0], v_w_branch_attn[0], v_w_branch_pool[0], v_w_out[0], v_w_ff1[0], v_w_ff2[0])
    shard_axes = (1, 1, 1, 1, 0, 1, 0)
    small_w = (pool_scale, ln_mix_g, ln_mix_b, ln_ff_g, ln_ff_b)
    small_m = (m_pool_scale, m_ln_mix_g, m_ln_mix_b, m_ln_ff_g, m_ln_ff_b)
    small_v = (v_pool_scale, v_ln_mix_g, v_ln_mix_b, v_ln_ff_g, v_ln_ff_b)

    names = ("w_in", "w_pool", "w_branch_attn", "w_branch_pool", "w_out", "w_ff1", "w_ff2")
    axis_of = dict(zip(names, shard_axes))
    shard_of = dict(zip(names, [w.astype(BF16) for w in big_w]))

    def full_buffer(n):
        s, ax = shard_of[n], axis_of[n]
        full = list(s.shape)
        full[ax] *= N_DEV
        return lax.empty(tuple(full), s.dtype)

    def gather_group(tag, group, after):
        ex = _gather_exchange(tag, [axis_of[n] for n in group], [shard_of[n].shape[axis_of[n]] for n in group])
        return ex, ex.start([shard_of[n] for n in group], [full_buffer(n) for n in group], after)

    (gw_in,) = _all_gather_weights([shard_of["w_in"]], (axis_of["w_in"],))
    mix_ex, mix_started = gather_group("gather_mix", ("w_pool", "w_branch_attn", "w_branch_pool", "w_out"), gw_in)
    ffn_ex, ffn_started = gather_group("gather_ffn", ("w_ff1", "w_ff2"), mix_started[-1])

    groups = {"ff2": ("w_ff2",), "ff1": ("w_ff1",), "out": ("w_out",),
              "branch": ("w_branch_attn", "w_branch_pool"), "in": ("w_in", "w_pool", "small")}
    sent = {}

    def send(key, grads_):
        axes = [axis_of.get(n) for n in groups[key]]
        sizes = [None if ax is None else g.shape[ax] // N_DEV for g, ax in zip(grads_, axes)]
        lands = []
        for g, ax, size in zip(grads_, axes, sizes):
            shard = list(g.shape)
            if ax is not None:
                shard[ax] = size
            lands.append(lax.empty((N_DEV, *shard), g.dtype))
        ex = _scatter_exchange("scatter_" + key, axes, sizes)
        sent[key] = (ex, ex.start(list(grads_), lands))
        return sent[key][1][-1]

    cos2, sin_fwd, sin_bwd = _rope_tables(positions[0])
    loss, grad_x = _local_step(
        x[0], cos2, sin_fwd, sin_bwd, gw_in, lambda after: mix_ex.wait(mix_started, after),
        lambda after: ffn_ex.wait(ffn_started, after), pool_scale, ln_mix_g, ln_mix_b, ln_ff_g, ln_ff_b,
        loss_target[0], send, start_after=ffn_started[-1])

    state = dict(zip(names, zip(big_w, big_m, big_v)))
    n_small = sum(w.shape[-1] for w in small_w)
    small_2d = (n_small // HEAD_DIM, HEAD_DIM)
    state["small"] = tuple(jnp.concatenate(t, axis=-1).reshape(small_2d) for t in (small_w, small_m, small_v))
    grads, deltas, new_ms, new_vs = {}, {}, {}, {}
    after = grad_x
    for key in ("ff2", "ff1", "out", "branch", "in"):
        ex, started = sent[key]
        for n, part in zip(groups[key], ex.wait(started, after)):
            w, m, v = state[n]
            r2 = (-1, w.shape[-1])
            w2d = w.reshape(r2)
            res = _adamw("adamw_" + n, part.reshape((N_DEV,) + w2d.shape), w2d, m.reshape(r2), v.reshape(r2))
            after = res[0]
            if n == "small":
                small_out = [t.reshape(1, n_small) for t in res]
            else:
                grads[n], deltas[n], new_ms[n], new_vs[n] = (t.reshape((1,) + w.shape) for t in res)
    small_names = ("pool_scale", "ln_mix_g", "ln_mix_b", "ln_ff_g", "ln_ff_b")
    off = 0
    for n, w in zip(small_names, small_w):
        width = w.shape[-1]
        grads[n], deltas[n], new_ms[n], new_vs[n] = (t[:, off:off + width] for t in small_out)
        off += width

    order = ("w_in", "w_pool", "pool_scale", "w_branch_attn", "w_branch_pool", "w_out", "ln_mix_g", "ln_mix_b",
             "w_ff1", "w_ff2", "ln_ff_g", "ln_ff_b")
    total_loss = lax.psum(loss[0, 0], ("x", "y", "c"))
    return (total_loss, grad_x[None], *[grads[n] for n in order], *[deltas[n] for n in order],
            *[new_ms[n] for n in order], *[new_vs[n] for n in order])
```

```python
import functools

import jax
import jax.numpy as jnp
from jax import lax
from jax.experimental import pallas as pl
from jax.experimental.pallas import tpu as pltpu

F32 = jnp.float32
BF16 = jnp.bfloat16

N_DEV = 8
HEAD_DIM = 128
SUB_BLOCK = 128
DILATIONS = (1, 4, 16)
POOL_WINDOWS = (2, 4, 8, 16)
MAX_POOL_WINDOW = 16
POOL_HALO = 128
PERM_ROWS = 512
K_TILE = 1024
LN_EPS = 1e-5
DEEPNORM_ALPHA = 2.0 ** 0.25
ROPE_THETA = 10000.0
ATTN_SCALE = HEAD_DIM ** -0.5
ADAM_LR, ADAM_B1, ADAM_B2, ADAM_EPS, ADAM_WD, ADAM_STEP = 0.001, 0.9, 0.999, 1e-08, 0.01, 10
NEG_BIG = -1e30
VMEM_CAP_V7X = 64 * 1024 * 1024
MESH = pl.DeviceIdType.MESH


def _vmem_limit(est_bytes):
    return int(min(max(est_bytes * 5 // 4 + (4 << 20), 16 << 20), VMEM_CAP_V7X - (6 << 20)))


def _nbytes(shape, dtype):
    n = 1
    for s in shape:
        n *= s
    return n * jnp.dtype(dtype).itemsize


def _mm(name, a, b, form, tiles, outs, epi, extras=(), sequential=False, after=None):
    tm, tn, tk = tiles
    if form == "nn":
        (M, K), (K2, N) = a.shape, b.shape
    elif form == "nt":
        (M, K), (N, K2) = a.shape, b.shape
    else:
        (K, M), (K2, N) = a.shape, b.shape
    assert K == K2, (name, a.shape, b.shape)
    tm, tn, tk = min(tm, M), min(tn, N), min(tk, K)
    assert M % tm == 0 and N % tn == 0 and K % tk == 0, (name, M, N, K, tm, tn, tk)
    grid = (M // tm, N // tn, K // tk)
    nk = grid[2]
    if form == "nn":
        a_spec = pl.BlockSpec((tm, tk), lambda i, j, k: (i, k))
        b_spec = pl.BlockSpec((tk, tn), lambda i, j, k: (k, j))
        contract = ((1,), (0,))
    elif form == "nt":
        a_spec = pl.BlockSpec((tm, tk), lambda i, j, k: (i, k))
        b_spec = pl.BlockSpec((tn, tk), lambda i, j, k: (j, k))
        contract = ((1,), (1,))
    else:
        a_spec = pl.BlockSpec((tk, tm), lambda i, j, k: (k, i))
        b_spec = pl.BlockSpec((tk, tn), lambda i, j, k: (k, j))
        contract = ((0,), (0,))
    n_ex, n_out = len(extras), len(outs)
    n_after = 0 if after is None else 1

    def body(a_ref, b_ref, *rest):
        ex_refs = rest[:n_ex]
        rest = rest[n_ex + n_after:]
        out_refs = rest[:n_out]
        i, j, k = pl.program_id(0), pl.program_id(1), pl.program_id(2)

        def prod():
            return lax.dot_general(a_ref[...].astype(BF16), b_ref[...].astype(BF16),
                                   (contract, ((), ())), preferred_element_type=F32)

        if nk == 1:
            epi(prod(), ex_refs, out_refs, i, j)
        else:
            acc = rest[n_out]

            @pl.when(k == 0)
            def _():
                acc[...] = prod()

            @pl.when(k > 0)
            def _():
                acc[...] += prod()

            @pl.when(k == nk - 1)
            def _():
                epi(acc[...], ex_refs, out_refs, i, j)

    est = 2 * (_nbytes(a_spec.block_shape, a.dtype) + _nbytes(b_spec.block_shape, b.dtype))
    est += sum(2 * _nbytes(bs, arr.dtype) for arr, bs, _ in extras)
    est += sum(2 * _nbytes(bs, dt) for _, dt, bs, _ in outs)
    est += 4 * tm * tn * 4
    sem = ("arbitrary",) * 3 if sequential else ("parallel", "parallel", "arbitrary")
    return pl.pallas_call(
        body, name=name, grid=grid,
        in_specs=([a_spec, b_spec] + [pl.BlockSpec(bs, im) for _, bs, im in extras]
                  + [pl.BlockSpec(memory_space=pl.ANY)] * n_after),
        out_specs=[pl.BlockSpec(bs, im) for _, _, bs, im in outs],
        out_shape=[jax.ShapeDtypeStruct(sh, dt) for sh, dt, _, _ in outs],
        scratch_shapes=[pltpu.VMEM((tm, tn), F32)] if nk > 1 else [],
        compiler_params=pltpu.CompilerParams(dimension_semantics=sem, vmem_limit_bytes=_vmem_limit(est)),
    )(a, b, *[arr for arr, _, _ in extras], *([after] if n_after else []))


def _tile_out(shape, dtype, tm, tn):
    return (shape, dtype, (tm, tn), lambda i, j, k: (i, j))


def _row_sum_out(width):
    return ((1, width), F32, (1, width), lambda i, j, k: (0, 0))


def _accumulate_rows(ref, value, i):
    @pl.when(i == 0)
    def _():
        ref[...] = value

    @pl.when(i > 0)
    def _():
        ref[...] += value


def _layer_norm_bwd(dy, xhat, rstd, g):
    dxh = dy * g
    m1 = jnp.mean(dxh, axis=-1, keepdims=True)
    m2 = jnp.mean(dxh * xhat, axis=-1, keepdims=True)
    return rstd * (dxh - m1 - xhat * m2)


def _rope_apply(t, cos2, sin_signed):
    return t * cos2 + pltpu.roll(t, HEAD_DIM // 2, axis=1) * sin_signed


def _in_proj_piece(name, xb, w_in, col_blocks, cos2, sin_fwd, h_so_far, block_cols, n_rope_blocks):
    S, D = xb.shape
    W = w_in.shape[1]
    tm, tk = min(2048, S), min(K_TILE, D)
    nk = D // tk
    n_blocks = col_blocks.shape[0]

    def body(cols_ref, x_ref, w_ref, cos_ref, sin_ref, *rest):
        h_ref, acc = rest[-2], rest[-1]
        j, k = pl.program_id(1), pl.program_id(2)

        def prod():
            return _dot_nn(x_ref[...], w_ref[...])

        @pl.when(k == 0)
        def _():
            acc[...] = prod()

        @pl.when(k > 0)
        def _():
            acc[...] += prod()

        @pl.when(jnp.logical_and(k == nk - 1, cols_ref[j] < n_rope_blocks))
        def _():
            c, s = cos_ref[...], sin_ref[...]
            for hd in range(block_cols // HEAD_DIM):
                sl = slice(hd * HEAD_DIM, (hd + 1) * HEAD_DIM)
                h_ref[:, sl] = _rope_apply(acc[:, sl], c, s).astype(BF16)

        @pl.when(jnp.logical_and(k == nk - 1, cols_ref[j] >= n_rope_blocks))
        def _():
            h_ref[...] = acc[...].astype(BF16)

    row = pl.BlockSpec((tm, HEAD_DIM), lambda i, j, k, cols: (i, 0))
    carried = [] if h_so_far is None else [h_so_far]
    est = 2 * (tm * tk * 2 + tk * block_cols * 2 + tm * block_cols * 2 + 2 * tm * HEAD_DIM * 4) + 3 * tm * block_cols * 4
    return pl.pallas_call(
        body, name=name,
        grid_spec=pltpu.PrefetchScalarGridSpec(
            num_scalar_prefetch=1, grid=(S // tm, n_blocks, nk),
            in_specs=[pl.BlockSpec((tm, tk), lambda i, j, k, cols: (i, k)),
                      pl.BlockSpec((tk, block_cols), lambda i, j, k, cols: (k, cols[j])), row, row]
                     + [pl.BlockSpec(memory_space=pl.ANY)] * len(carried),
            out_specs=pl.BlockSpec((tm, block_cols), lambda i, j, k, cols: (i, cols[j])),
            scratch_shapes=[pltpu.VMEM((tm, block_cols), F32)]),
        out_shape=jax.ShapeDtypeStruct((S, W), BF16),
        input_output_aliases={5: 0} if carried else {},
        compiler_params=pltpu.CompilerParams(dimension_semantics=("parallel", "arbitrary", "arbitrary"),
                                             vmem_limit_bytes=_vmem_limit(est)),
    )(col_blocks, xb, w_in, cos2, sin_fwd, *carried)


def _attn_mask(mb):
    qi = lax.broadcasted_iota(jnp.int32, (SUB_BLOCK, 2 * SUB_BLOCK), 0)
    kj = lax.broadcasted_iota(jnp.int32, (SUB_BLOCK, 2 * SUB_BLOCK), 1)
    prev = jnp.logical_and(jnp.logical_and(kj < SUB_BLOCK, kj >= qi), mb > 0)
    cur = jnp.logical_and(kj >= SUB_BLOCK, kj - SUB_BLOCK <= qi)
    return jnp.logical_or(prev, cur)


def _both_blocks(prev_ref, cur_ref, sl):
    return jnp.concatenate([prev_ref[:, sl], cur_ref[:, sl]], axis=0)


def _dot_nt(a, b):
    return lax.dot_general(a, b, (((1,), (1,)), ((), ())), preferred_element_type=F32)


def _dot_tn(a, b):
    return lax.dot_general(a, b, (((0,), (0,)), ((), ())), preferred_element_type=F32)


def _dot_nn(a, b):
    return lax.dot_general(a, b, (((1,), (0,)), ((), ())), preferred_element_type=F32)


def _perm_matrix(d, to_residue_major):
    g = PERM_ROWS // d
    i = lax.broadcasted_iota(jnp.int32, (PERM_ROWS, PERM_ROWS), 0)
    j = lax.broadcasted_iota(jnp.int32, (PERM_ROWS, PERM_ROWS), 1)
    if to_residue_major:
        hit = j == (i % g) * d + i // g
    else:
        hit = j == (i % d) * g + i // d
    return hit.astype(BF16)


def _permute_rows(perm, x, terms=3):
    if x.dtype == BF16:
        return _dot_nn(perm, x)
    hi = x.astype(BF16)
    r1 = x - hi.astype(F32)
    mid = r1.astype(BF16)
    out = _dot_nn(perm, hi) + _dot_nn(perm, mid)
    if terms == 3:
        out = out + _dot_nn(perm, (r1 - mid.astype(F32)).astype(BF16))
    return out


def _rm_block(d, width):
    return pl.BlockSpec((d, PERM_ROWS // d, width), lambda i: (0, i, 0))


def _to_residue_major(name, x, col_block, width):
    S = x.shape[0]
    dils = [d for d in DILATIONS if d > 1]
    chunk = min(width, 1024)

    def body(x_ref, *out_refs):
        for d, o_ref in zip(dils, out_refs):
            perm = _perm_matrix(d, True)
            for c0 in range(0, width, chunk):
                cw = min(chunk, width - c0)
                y = _permute_rows(perm, x_ref[:, c0:c0 + cw])
                o_ref[:, :, c0:c0 + cw] = y.astype(x.dtype).reshape(d, PERM_ROWS // d, cw)

    return pl.pallas_call(
        body, name=name, grid=(S // PERM_ROWS,),
        in_specs=[pl.BlockSpec((PERM_ROWS, width), lambda i: (i, col_block))],
        out_specs=[_rm_block(d, width) for d in dils],
        out_shape=[jax.ShapeDtypeStruct((d, S // d, width), x.dtype) for d in dils],
        compiler_params=pltpu.CompilerParams(dimension_semantics=("parallel",),
                                             vmem_limit_bytes=_vmem_limit(32 << 20)),
    )(x)


def _qkv_specs(aw):
    def spec(col, prev):
        if prev:
            return pl.BlockSpec((None, SUB_BLOCK, aw), lambda r, mb: (r, jnp.maximum(mb - 1, 0), col))
        return pl.BlockSpec((None, SUB_BLOCK, aw), lambda r, mb: (r, mb, col))
    return [spec(0, False), spec(1, True), spec(1, False), spec(2, True), spec(2, False)]


def _put_column(tile, col, value):
    lane = lax.broadcasted_iota(jnp.int32, tile.shape, 1)
    return jnp.where(lane == col, value, tile)


def _attn_fwd(qkv, d, aw):
    _, rows, _ = qkv.shape
    n_heads = aw // HEAD_DIM
    nb = rows // SUB_BLOCK

    def body(q_ref, kp_ref, kc_ref, vp_ref, vc_ref, o_ref, lse_ref, s_buf, p_buf):
        mask = _attn_mask(pl.program_id(1))
        for hd in range(n_heads):
            sl = slice(hd * HEAD_DIM, (hd + 1) * HEAD_DIM)
            s_buf[hd] = _dot_nt(q_ref[:, sl], _both_blocks(kp_ref, kc_ref, sl))
        lse_tile = jnp.zeros((SUB_BLOCK, HEAD_DIM), F32)
        inv_tile = jnp.zeros((SUB_BLOCK, HEAD_DIM), F32)
        for hd in range(n_heads):
            s = jnp.where(mask, s_buf[hd] * ATTN_SCALE, NEG_BIG)
            m = jnp.max(s, axis=-1, keepdims=True)
            p = jnp.exp(s - m)
            l = jnp.sum(p, axis=-1, keepdims=True)
            p_buf[hd] = p.astype(BF16)
            lse_tile = _put_column(lse_tile, hd, m + jnp.log(l))
            inv_tile = _put_column(inv_tile, hd, 1.0 / l)
        lse_ref[...] = lse_tile
        for hd in range(n_heads):
            sl = slice(hd * HEAD_DIM, (hd + 1) * HEAD_DIM)
            o = _dot_nn(p_buf[hd], _both_blocks(vp_ref, vc_ref, sl))
            o_ref[:, sl] = o * inv_tile[:, hd:hd + 1]

    return pl.pallas_call(
        body, name=f"attn_fwd_d{d}", grid=(d, nb),
        in_specs=_qkv_specs(aw),
        out_specs=[pl.BlockSpec((None, SUB_BLOCK, aw), lambda r, mb: (r, mb, 0)),
                   pl.BlockSpec((None, SUB_BLOCK, HEAD_DIM), lambda r, mb: (r, mb, 0))],
        out_shape=[jax.ShapeDtypeStruct((d, rows, aw), F32), jax.ShapeDtypeStruct((d, rows, HEAD_DIM), F32)],
        scratch_shapes=[pltpu.VMEM((n_heads, SUB_BLOCK, 2 * SUB_BLOCK), F32),
                        pltpu.VMEM((n_heads, SUB_BLOCK, 2 * SUB_BLOCK), BF16)],
        compiler_params=pltpu.CompilerParams(dimension_semantics=("parallel", "parallel"),
                                             vmem_limit_bytes=_vmem_limit(16 << 20)),
    )(qkv, qkv, qkv, qkv, qkv)


def _attn_combine(outs, lses, aw):
    S = outs[0].shape[1]
    n_heads = aw // HEAD_DIM
    n_pat = len(DILATIONS)

    def body(*refs):
        o_refs, l_refs = refs[:n_pat], refs[n_pat:2 * n_pat]
        o_ref, lt_ref = refs[2 * n_pat], refs[2 * n_pat + 1]
        o_nat, l_nat = [], []
        for d, o_r, l_r in zip(DILATIONS, o_refs, l_refs):
            o_p = o_r[...].reshape(PERM_ROWS, aw)
            l_p = l_r[...].reshape(PERM_ROWS, HEAD_DIM)
            if d > 1:
                perm = _perm_matrix(d, False)
                o_p, l_p = _permute_rows(perm, o_p, terms=2), _permute_rows(perm, l_p)
            o_nat.append(o_p)
            l_nat.append(l_p)
        mx = functools.reduce(jnp.maximum, l_nat)
        es = [jnp.exp(l_p - mx) for l_p in l_nat]
        den = functools.reduce(jnp.add, es)
        lt_ref[...] = mx + jnp.log(den)
        ws = [e / den for e in es]
        for hd in range(n_heads):
            sl = slice(hd * HEAD_DIM, (hd + 1) * HEAD_DIM)
            o = ws[0][:, hd:hd + 1] * o_nat[0][:, sl]
            for pi in range(1, n_pat):
                o = o + ws[pi][:, hd:hd + 1] * o_nat[pi][:, sl]
            o_ref[:, sl] = o.astype(BF16)

    return pl.pallas_call(
        body, name="attn_combine", grid=(S // PERM_ROWS,),
        in_specs=[_rm_block(d, aw) for d in DILATIONS] + [_rm_block(d, HEAD_DIM) for d in DILATIONS],
        out_specs=[pl.BlockSpec((PERM_ROWS, aw), lambda i: (i, 0)), pl.BlockSpec((PERM_ROWS, HEAD_DIM), lambda i: (i, 0))],
        out_shape=[jax.ShapeDtypeStruct((S, aw), BF16), jax.ShapeDtypeStruct((S, HEAD_DIM), F32)],
        compiler_params=pltpu.CompilerParams(dimension_semantics=("parallel",),
                                             vmem_limit_bytes=_vmem_limit(40 << 20)),
    )(*outs, *lses)


def _band(tm, width, w, row_offset, transpose):
    t = lax.broadcasted_iota(jnp.int32, (tm, width), 0)
    u = lax.broadcasted_iota(jnp.int32, (tm, width), 1)
    dist = (u - t - row_offset) if transpose else (t + row_offset - u)
    return jnp.logical_and(dist >= 0, dist < w).astype(BF16)


def _pool_fwd(h, w_pool, pool_scale, pw, u_col_block):
    S, W = h.shape
    n_groups = len(POOL_WINDOWS)
    gw = pw // n_groups
    tm = min(512, S)
    halo_per_tile = tm // POOL_HALO

    def body(uc_ref, uh_ref, w_ref, sc_ref, p_ref, y_ref, pm_ref):
        i = pl.program_id(0)
        t_abs = i * tm + lax.broadcasted_iota(jnp.int32, (tm, 1), 0)
        for g, w in enumerate(POOL_WINDOWS):
            sl = slice(g * gw, (g + 1) * gw)
            uc = uc_ref[:, sl]
            uh = jnp.where(i > 0, uh_ref[:, sl], jnp.zeros((POOL_HALO, gw), BF16))
            ssum = _dot_nn(_band(tm, tm, w, 0, False), uc) + _dot_nn(_band(tm, POOL_HALO, w, POOL_HALO, False), uh)
            cnt = jnp.minimum(t_abs + 1, w).astype(F32)
            p = (ssum / cnt - uc.astype(F32)).astype(BF16)
            y = _dot_nn(p, w_ref[g])
            p_ref[:, sl] = p
            y_ref[:, sl] = y.astype(BF16)
            pm_ref[:, sl] = (y * sc_ref[:, sl]).astype(BF16)

    row = pl.BlockSpec((tm, pw), lambda i: (i, 0))
    return pl.pallas_call(
        body, name="pool_fwd", grid=(S // tm,),
        in_specs=[pl.BlockSpec((tm, pw), lambda i: (i, u_col_block)),
                  pl.BlockSpec((POOL_HALO, pw), lambda i: (jnp.maximum(i * halo_per_tile - 1, 0), u_col_block)),
                  pl.BlockSpec((n_groups, gw, gw), lambda i: (0, 0, 0)),
                  pl.BlockSpec((1, pw), lambda i: (0, 0))],
        out_specs=[row, row, row],
        out_shape=[jax.ShapeDtypeStruct((S, pw), BF16)] * 3,
        compiler_params=pltpu.CompilerParams(dimension_semantics=("parallel",),
                                             vmem_limit_bytes=_vmem_limit(24 << 20)),
    )(h, h, w_pool, pool_scale)


def _branch_attn(o_attn, w_ba):
    S, _ = o_attn.shape
    D = w_ba.shape[1]
    tm, tn = min(1024, S), D

    def epi(acc, ex, out, i, j):
        out[0][...] = acc.astype(BF16)

    (y,) = _mm("branch_attn", o_attn, w_ba, "nn", (tm, tn, 1024), [_tile_out((S, D), BF16, tm, tn)], epi)
    return y


def _branch_pool_merge(pm, w_bp, h, y_attn, gate_col0):
    S, _ = pm.shape
    D = w_bp.shape[1]
    tm, tn = min(512, S), D
    ga0, gp0 = gate_col0 // tn, (gate_col0 + D) // tn

    def epi(acc, ex, out, i, j):
        ga_ref, gp_ref, ya_ref = ex
        yp_ref, mg_ref = out
        yp = acc.astype(BF16)
        yp_ref[...] = yp
        mg = (jax.nn.sigmoid(ga_ref[...].astype(F32)) * ya_ref[...].astype(F32)
              + jax.nn.sigmoid(gp_ref[...].astype(F32)) * acc)
        mg_ref[...] = mg.astype(BF16)

    y_pool, merged = _mm(
        "branch_pool_merge", pm, w_bp, "nn", (tm, tn, 1024),
        [_tile_out((S, D), BF16, tm, tn), _tile_out((S, D), BF16, tm, tn)], epi,
        extras=[(h, (tm, tn), lambda i, j, k: (i, ga0 + j)), (h, (tm, tn), lambda i, j, k: (i, gp0 + j)),
                (y_attn, (tm, tn), lambda i, j, k: (i, j))])
    return y_pool, merged


def _layer_norm_rows(z, g, b):
    mu = jnp.mean(z, axis=-1, keepdims=True)
    zc = z - mu
    var = jnp.mean(zc * zc, axis=-1, keepdims=True)
    rstd = lax.rsqrt(var + LN_EPS)
    xhat = zc * rstd
    return xhat * g + b, xhat, rstd


def _out_proj_ln(merged, w_out, x, g, b):
    S, D = x.shape
    tm = min(256, S)

    def epi(acc, ex, out, i, j):
        x_ref, g_ref, b_ref = ex
        x1_ref, x1b_ref, xh_ref, rs_ref = out
        y, xhat, rstd = _layer_norm_rows(DEEPNORM_ALPHA * x_ref[...] + acc, g_ref[...], b_ref[...])
        x1_ref[...] = y
        x1b_ref[...] = y.astype(BF16)
        xh_ref[...] = xhat
        rs_ref[...] = jnp.broadcast_to(rstd, (tm, HEAD_DIM))

    row = lambda i, j, k: (i, 0)
    vec = lambda i, j, k: (0, 0)
    return _mm("out_proj_ln", merged, w_out, "nn", (tm, D, D),
               [((S, D), F32, (tm, D), row), ((S, D), BF16, (tm, D), row), ((S, D), F32, (tm, D), row),
                ((S, HEAD_DIM), F32, (tm, HEAD_DIM), row)], epi,
               extras=[(x, (tm, D), row), (g, (1, D), vec), (b, (1, D), vec)])


def _ffn_up(x1b, w1):
    S, D = x1b.shape
    F = w1.shape[1]
    tm, tn = min(1024, S), min(2048, F)

    def epi(acc, ex, out, i, j):
        r = jnp.maximum(acc, 0.0)
        out[0][...] = (r * r).astype(BF16)

    (a,) = _mm("ffn_up", x1b, w1, "nn", (tm, tn, K_TILE), [_tile_out((S, F), BF16, tm, tn)], epi)
    return a


def _residual_matmul(name, a, w, form, resid, after=None):
    S, D = resid.shape
    tm, tn = min(1024, S), min(2048, D)

    def epi(acc, ex, out, i, j):
        out[0][...] = DEEPNORM_ALPHA * ex[0][...] + acc

    (z,) = _mm(name, a, w, form, (tm, tn, K_TILE), [_tile_out((S, D), F32, tm, tn)], epi,
               extras=[(resid, (tm, tn), lambda i, j, k: (i, j))], after=after)
    return z


def _row_kernel(name, body, row_inputs, vec_inputs, row_outputs, sum_widths, tr):
    S = row_inputs[0].shape[0]
    row = lambda w: pl.BlockSpec((tr, w), lambda i: (i, 0))
    vec = lambda w: pl.BlockSpec((1, w), lambda i: (0, 0))

    def wrapped(*refs):
        body(pl.program_id(0), *refs)

    return pl.pallas_call(
        wrapped, name=name, grid=(S // tr,),
        in_specs=[row(t.shape[1]) for t in row_inputs] + [vec(t.shape[1]) for t in vec_inputs],
        out_specs=[row(w) for w, _ in row_outputs] + [vec(w) for w in sum_widths],
        out_shape=([jax.ShapeDtypeStruct((S, w), dt) for w, dt in row_outputs]
                   + [jax.ShapeDtypeStruct((1, w), F32) for w in sum_widths]),
        compiler_params=pltpu.CompilerParams(dimension_semantics=("arbitrary",),
                                             vmem_limit_bytes=_vmem_limit(40 << 20)),
    )(*row_inputs, *vec_inputs)


def _ln_loss_bwd(z2, g, b, target):
    S, D = z2.shape

    def body(i, z_ref, t_ref, g_ref, b_ref, dz_ref, dzb_ref, dg_ref, db_ref, loss_ref):
        gv = g_ref[...]
        y, xhat, rstd = _layer_norm_rows(z_ref[...], gv, b_ref[...])
        err = y - t_ref[...]
        loss = 0.5 * jnp.sum(jnp.mean(err * err, axis=-1, keepdims=True), axis=0, keepdims=True)
        dy = err * (1.0 / D)
        dz = _layer_norm_bwd(dy, xhat, rstd, gv)
        dz_ref[...] = dz
        dzb_ref[...] = dz.astype(BF16)
        _accumulate_rows(dg_ref, jnp.sum(dy * xhat, axis=0, keepdims=True), i)
        _accumulate_rows(db_ref, jnp.sum(dy, axis=0, keepdims=True), i)
        _accumulate_rows(loss_ref, jnp.broadcast_to(loss, (1, HEAD_DIM)), i)

    return _row_kernel("ln_loss_bwd", body, [z2, target], [g, b], [(D, F32), (D, BF16)], [D, D, HEAD_DIM],
                       min(256, S))


def _ln_bwd(dy, xhat, rstd, g):
    S, D = dy.shape

    def body(i, dy_ref, xh_ref, rs_ref, g_ref, dz_ref, dzb_ref, dg_ref, db_ref):
        dyv, xhat_v = dy_ref[...], xh_ref[...]
        dz = _layer_norm_bwd(dyv, xhat_v, rs_ref[:, :1], g_ref[...])
        dz_ref[...] = dz
        dzb_ref[...] = dz.astype(BF16)
        _accumulate_rows(dg_ref, jnp.sum(dyv * xhat_v, axis=0, keepdims=True), i)
        _accumulate_rows(db_ref, jnp.sum(dyv, axis=0, keepdims=True), i)

    return _row_kernel("ln_bwd", body, [dy, xhat, rstd], [g], [(D, F32), (D, BF16)], [D, D], min(256, S))


def _grad_weight(name, act, cot):
    M, N = act.shape[1], cot.shape[1]
    tm, tn = min(1024, M), min(2048, N)

    def epi(acc, ex, out, i, j):
        out[0][...] = acc.astype(BF16)

    (g,) = _mm(name, act, cot, "tn", (tm, tn, K_TILE), [_tile_out((M, N), BF16, tm, tn)], epi)
    return g


def _ffn_down_bwd(dz2b, w2, a, after=None):
    S, D = dz2b.shape
    F = w2.shape[0]
    tm, tn = min(1024, S), min(2048, F)

    def epi(acc, ex, out, i, j):
        out[0][...] = (acc * (2.0 * jnp.sqrt(ex[0][...])).astype(F32)).astype(BF16)

    (dh1,) = _mm("ffn_down_bwd", dz2b, w2, "nt", (tm, tn, K_TILE), [_tile_out((S, F), BF16, tm, tn)], epi,
                 extras=[(a, (tm, tn), lambda i, j, k: (i, j))], after=after)
    return dh1


def _out_proj_bwd(dz1b, w_out, h, y_attn, y_pool, gate_col0, after=None):
    S, D = dz1b.shape
    W = h.shape[1]
    tm = min(256, S)
    assert gate_col0 == 2 * D and W == 4 * D

    def epi(acc, ex, out, i, j):
        gates_ref, ya_ref, yp_ref = ex
        dya_ref, dyp_ref, dh_ref = out
        sa = jax.nn.sigmoid(gates_ref[:, :D].astype(F32))
        sp = jax.nn.sigmoid(gates_ref[:, D:].astype(F32))
        dya_ref[...] = (acc * sa).astype(BF16)
        dyp_ref[...] = (acc * sp).astype(BF16)
        dh_ref[:, :D] = (acc * ya_ref[...].astype(F32) * (sa * (1.0 - sa))).astype(BF16)
        dh_ref[:, D:] = (acc * yp_ref[...].astype(F32) * (sp * (1.0 - sp))).astype(BF16)

    row = lambda i, j, k: (i, 0)
    return _mm("out_proj_bwd", dz1b, w_out, "nt", (tm, D, D),
               [((S, D), BF16, (tm, D), row), ((S, D), BF16, (tm, D), row),
                ((S, W), BF16, (tm, 2 * D), lambda i, j, k: (i, 1))], epi,
               extras=[(h, (tm, 2 * D), lambda i, j, k: (i, 1)), (y_attn, (tm, D), row), (y_pool, (tm, D), row)],
               after=after)


def _branch_attn_bwd(dy_attn, w_ba, o_attn, l_tot, after=None):
    S, D = dy_attn.shape
    aw = w_ba.shape[0]
    n_heads = aw // HEAD_DIM
    tm = min(512, S)

    def epi(acc, ex, out, i, j):
        do_ref, st_ref = out
        do_ref[...] = acc.astype(BF16)
        o = ex[0][...].astype(F32)
        stats = ex[1][...]
        for hd in range(n_heads):
            sl = slice(hd * HEAD_DIM, (hd + 1) * HEAD_DIM)
            stats = _put_column(stats, n_heads + hd, jnp.sum(acc[:, sl] * o[:, sl], axis=-1, keepdims=True))
        st_ref[...] = stats

    row = lambda i, j, k: (i, 0)
    return _mm("branch_attn_bwd", dy_attn, w_ba, "nt", (tm, aw, D),
               [((S, aw), BF16, (tm, aw), row), ((S, HEAD_DIM), F32, (tm, HEAD_DIM), row)], epi,
               extras=[(o_attn, (tm, aw), row), (l_tot, (tm, HEAD_DIM), row)], after=after)


def _branch_pool_bwd(dy_pool, w_bp, y_pre, pool_scale):
    S, D = dy_pool.shape
    pw = w_bp.shape[0]
    tm = min(512, S)

    def epi(acc, ex, out, i, j):
        y_ref, sc_ref = ex
        dyp_ref, dsc_ref = out
        dyp_ref[...] = (acc * sc_ref[...]).astype(BF16)
        _accumulate_rows(dsc_ref, jnp.sum(acc * y_ref[...].astype(F32), axis=0, keepdims=True), i)

    row = lambda i, j, k: (i, 0)
    return _mm("branch_pool_bwd", dy_pool, w_bp, "nt", (tm, pw, D),
               [((S, pw), BF16, (tm, pw), row), _row_sum_out(pw)], epi,
               extras=[(y_pre, (tm, pw), row), (pool_scale, (1, pw), lambda i, j, k: (0, 0))],
               sequential=True)


def _pool_bwd(dh, dy_pre, p, w_pool, pw, u_col_block):
    S, W = dh.shape
    n_groups = len(POOL_WINDOWS)
    gw = pw // n_groups
    tm = min(512, S)
    n_tiles = S // tm
    halo_per_tile = tm // POOL_HALO
    n_halo_blocks = S // POOL_HALO

    def body(dh_in_ref, dyc_ref, dyh_ref, p_ref, w_ref, dh_ref, dw_ref):
        del dh_in_ref
        i = pl.program_id(0)
        t_cur = i * tm + lax.broadcasted_iota(jnp.int32, (tm, 1), 0)
        t_halo = (i + 1) * tm + lax.broadcasted_iota(jnp.int32, (POOL_HALO, 1), 0)
        for g, w in enumerate(POOL_WINDOWS):
            sl = slice(g * gw, (g + 1) * gw)
            wg = w_ref[g]
            dyc = dyc_ref[:, sl]
            dyh = jnp.where(i < n_tiles - 1, dyh_ref[:, sl], jnp.zeros((POOL_HALO, gw), BF16))
            dp_cur = _dot_nt(dyc, wg)
            dp_halo = _dot_nt(dyh, wg)
            dpc_cur = (dp_cur / jnp.minimum(t_cur + 1, w).astype(F32)).astype(BF16)
            dpc_halo = (dp_halo / jnp.minimum(t_halo + 1, w).astype(F32)).astype(BF16)
            du = (_dot_nn(_band(tm, tm, w, 0, True), dpc_cur)
                  + _dot_nn(_band(tm, POOL_HALO, w, -tm, True), dpc_halo) - dp_cur)
            dh_ref[:, sl] = du.astype(BF16)
            dw = _dot_tn(p_ref[:, sl], dyc)

            @pl.when(i == 0)
            def _():
                dw_ref[g] = dw

            @pl.when(i > 0)
            def _():
                dw_ref[g] += dw

    row = pl.BlockSpec((tm, pw), lambda i: (i, 0))
    dh_new, dw_pool = pl.pallas_call(
        body, name="pool_bwd", grid=(n_tiles,),
        in_specs=[pl.BlockSpec(memory_space=pl.ANY), row,
                  pl.BlockSpec((POOL_HALO, pw), lambda i: (jnp.minimum((i + 1) * halo_per_tile, n_halo_blocks - 1), 0)),
                  row, pl.BlockSpec((n_groups, gw, gw), lambda i: (0, 0, 0))],
        out_specs=[pl.BlockSpec((tm, pw), lambda i: (i, u_col_block)),
                   pl.BlockSpec((n_groups, gw, gw), lambda i: (0, 0, 0))],
        out_shape=[jax.ShapeDtypeStruct((S, W), BF16), jax.ShapeDtypeStruct((n_groups, gw, gw), F32)],
        input_output_aliases={0: 0},
        compiler_params=pltpu.CompilerParams(dimension_semantics=("arbitrary",),
                                             vmem_limit_bytes=_vmem_limit(24 << 20)),
    )(dh, dy_pre, dy_pre, p, w_pool)
    return dh_new, dw_pool


def _attn_bwd(qkv, d_out, stats, d, aw):
    _, rows, _ = qkv.shape
    n_heads = aw // HEAD_DIM
    nb = rows // SUB_BLOCK
    n_blocks = d * nb

    def body(q_ref, kp_ref, kc_ref, vp_ref, vc_ref, do_ref, st_ref, dq_ref, dk_ref, dv_ref,
             carry_k, carry_v, s_buf, dp_buf, p_buf, ds_buf):
        step = pl.program_id(0)

        @pl.when(step == 0)
        def _():
            carry_k[...] = jnp.zeros_like(carry_k)
            carry_v[...] = jnp.zeros_like(carry_v)

        @pl.when(step < n_blocks)
        def _():
            mask = _attn_mask(step % nb)
            st = st_ref[...]
            for hd in range(n_heads):
                sl = slice(hd * HEAD_DIM, (hd + 1) * HEAD_DIM)
                s_buf[hd] = _dot_nt(q_ref[:, sl], _both_blocks(kp_ref, kc_ref, sl))
                dp_buf[hd] = _dot_nt(do_ref[:, sl], _both_blocks(vp_ref, vc_ref, sl))
            for hd in range(n_heads):
                lt, dl = st[:, hd:hd + 1], st[:, n_heads + hd:n_heads + hd + 1]
                p = jnp.where(mask, jnp.exp(jnp.where(mask, s_buf[hd] * ATTN_SCALE - lt, NEG_BIG)), 0.0)
                p_buf[hd] = p.astype(BF16)
                ds_buf[hd] = (p * (dp_buf[hd] - dl) * ATTN_SCALE).astype(BF16)
            for hd in range(n_heads):
                sl = slice(hd * HEAD_DIM, (hd + 1) * HEAD_DIM)
                dq_ref[:, sl] = _dot_nn(ds_buf[hd], _both_blocks(kp_ref, kc_ref, sl)).astype(BF16)
                dk_both = _dot_tn(ds_buf[hd], q_ref[:, sl])
                dv_both = _dot_tn(p_buf[hd], do_ref[:, sl])
                dk_ref[:, sl] = (carry_k[:, sl] + dk_both[:SUB_BLOCK]).astype(BF16)
                dv_ref[:, sl] = (carry_v[:, sl] + dv_both[:SUB_BLOCK]).astype(BF16)
                carry_k[:, sl] = dk_both[SUB_BLOCK:]
                carry_v[:, sl] = dv_both[SUB_BLOCK:]

        @pl.when(step == n_blocks)
        def _():
            dk_ref[...] = carry_k[...].astype(BF16)
            dv_ref[...] = carry_v[...].astype(BF16)

    def cur(step):
        return jnp.minimum(step, n_blocks - 1)

    def qkv_spec(col, prev):
        if prev:
            return pl.BlockSpec((SUB_BLOCK, aw), lambda s: (jnp.maximum(cur(s) - 1, 0), col))
        return pl.BlockSpec((SUB_BLOCK, aw), lambda s: (cur(s), col))

    def at_cur(w):
        return pl.BlockSpec((SUB_BLOCK, w), lambda s: (cur(s), 0))

    finished = pl.BlockSpec((SUB_BLOCK, aw), lambda s: (jnp.maximum(s - 1, 0), 0))
    pair = (n_heads, SUB_BLOCK, 2 * SUB_BLOCK)
    flat = lambda t: t.reshape(d * rows, t.shape[-1])
    qkv2 = flat(qkv)
    outs = pl.pallas_call(
        body, name=f"attn_bwd_d{d}", grid=(n_blocks + 1,),
        in_specs=[qkv_spec(0, False), qkv_spec(1, True), qkv_spec(1, False), qkv_spec(2, True), qkv_spec(2, False),
                  at_cur(aw), at_cur(HEAD_DIM)],
        out_specs=[at_cur(aw), finished, finished],
        out_shape=[jax.ShapeDtypeStruct((d * rows, aw), BF16)] * 3,
        scratch_shapes=[pltpu.VMEM((SUB_BLOCK, aw), F32), pltpu.VMEM((SUB_BLOCK, aw), F32),
                        pltpu.VMEM(pair, F32), pltpu.VMEM(pair, F32), pltpu.VMEM(pair, BF16), pltpu.VMEM(pair, BF16)],
        compiler_params=pltpu.CompilerParams(dimension_semantics=("arbitrary",),
                                             vmem_limit_bytes=_vmem_limit(24 << 20)),
    )(qkv2, qkv2, qkv2, qkv2, qkv2, flat(d_out), flat(stats))
    return [t.reshape(d, rows, aw) for t in outs]


def _attn_bwd_finish(dh, per_pattern, cos2, sin_bwd, aw):
    S, W = dh.shape
    n_heads = aw // HEAD_DIM
    n_pat = len(DILATIONS)

    def body(*refs):
        grad_refs = refs[1:1 + 3 * n_pat]
        cos_ref, sin_ref = refs[1 + 3 * n_pat], refs[2 + 3 * n_pat]
        out_ref = refs[3 + 3 * n_pat]
        perms = {d: _perm_matrix(d, False) for d in DILATIONS if d > 1}
        totals = []
        for which in range(3):
            tot = None
            for pi, d in enumerate(DILATIONS):
                g = grad_refs[which * n_pat + pi][...].reshape(PERM_ROWS, aw)
                g = _permute_rows(perms[d], g) if d > 1 else g.astype(F32)
                tot = g if tot is None else tot + g
            totals.append(tot)
        dq, dk, dv = totals
        c, s = cos_ref[...], sin_ref[...]
        for hd in range(n_heads):
            sl = slice(hd * HEAD_DIM, (hd + 1) * HEAD_DIM)
            out_ref[:, sl] = _rope_apply(dq[:, sl], c, s).astype(BF16)
            out_ref[:, aw + hd * HEAD_DIM:aw + (hd + 1) * HEAD_DIM] = _rope_apply(dk[:, sl], c, s).astype(BF16)
        out_ref[:, 2 * aw:] = dv.astype(BF16)

    grads = [pp[which] for which in range(3) for pp in per_pattern]
    rope_spec = pl.BlockSpec((PERM_ROWS, HEAD_DIM), lambda i: (i, 0))
    return pl.pallas_call(
        body, name="attn_bwd_finish", grid=(S // PERM_ROWS,),
        in_specs=([pl.BlockSpec(memory_space=pl.ANY)] + [_rm_block(d, aw) for d in DILATIONS] * 3
                  + [rope_spec, rope_spec]),
        out_specs=pl.BlockSpec((PERM_ROWS, 3 * aw), lambda i: (i, 0)),
        out_shape=jax.ShapeDtypeStruct((S, W), BF16),
        input_output_aliases={0: 0},
        compiler_params=pltpu.CompilerParams(dimension_semantics=("parallel",),
                                             vmem_limit_bytes=_vmem_limit(32 << 20)),
    )(dh, *grads, cos2, sin_bwd)


def _my_place():
    x, y, c = lax.axis_index("x"), lax.axis_index("y"), lax.axis_index("c")
    return x, y, c


def _flat(px, py, pc):
    return 4 * px + 2 * py + pc


def _shard_slice(ref, axis, idx, size):
    start = pl.multiple_of(idx * size, size)
    ix = [slice(None)] * len(ref.shape)
    ix[axis] = pl.ds(start, size)
    return ref.at[tuple(ix)]


_HBM_SPEC = pl.BlockSpec(memory_space=pltpu.HBM)
_SEM_SPEC = pl.BlockSpec(memory_space=pltpu.SEMAPHORE)
_ANY_SPEC = pl.BlockSpec(memory_space=pl.ANY)
_N_PEER = N_DEV - 1
SIBLING, SAME_CORE_NEIGHBOURS, OTHER_CORE_NEIGHBOURS, DIAGONAL = (1,), (2, 4), (3, 5), (6, 7)
PEER_ORDER = SIBLING + SAME_CORE_NEIGHBOURS + OTHER_CORE_NEIGHBOURS + DIAGONAL


def _peer_of(x, y, c, r):
    return (x ^ ((r >> 2) & 1), y ^ ((r >> 1) & 1), c ^ (r & 1))


class _Exchange:
    def __init__(self, name, part, slot):
        self.name, self.part, self.slot = name, part, slot

    def _copy(self, w, r, src, land, send_sems, recv_sems, sending):
        x, y, c = _my_place()
        peer = _peer_of(x, y, c, r)
        return pltpu.make_async_remote_copy(
            src_ref=self.part(w, src, _flat(*peer)),
            dst_ref=self.slot(w, land, _flat(x, y, c) if sending else _flat(*peer)),
            send_sem=send_sems.at[w * _N_PEER + r - 1], recv_sem=recv_sems.at[w * _N_PEER + r - 1],
            device_id=peer, device_id_type=MESH)

    def start(self, srcs, lands, after=None):
        n = len(srcs)
        n_after = 0 if after is None else 1

        def body(*refs):
            src, land = refs[:n], refs[n:2 * n]
            outs = refs[2 * n + n_after:]
            send_sems, recv_sems, local_sems, token = outs[0], outs[1], outs[2], outs[3 + 2 * n]
            for w in range(n):
                self._own_copy(w, src[w], land[w], local_sems).start()
                for r in PEER_ORDER:
                    self._copy(w, r, src[w], land[w], send_sems, recv_sems, True).start()
            token[...] = jnp.zeros_like(token)

        sems = pltpu.SemaphoreType.DMA((n * _N_PEER,))
        outs = pl.pallas_call(
            body, name=self.name + "_start",
            out_shape=(sems, sems, pltpu.SemaphoreType.DMA((n,)),
                       *[pltpu.HBM(t.shape, t.dtype) for t in list(srcs) + list(lands)],
                       jax.ShapeDtypeStruct((8, 128), F32)),
            in_specs=[_HBM_SPEC] * (2 * n) + [_ANY_SPEC] * n_after,
            out_specs=(_SEM_SPEC, _SEM_SPEC, _SEM_SPEC, *[_HBM_SPEC] * (2 * n), pl.BlockSpec(memory_space=pltpu.VMEM)),
            input_output_aliases={i: 3 + i for i in range(2 * n)},
            compiler_params=pltpu.CompilerParams(has_side_effects=pltpu.SideEffectType.DATAFLOW_SIDE_EFFECTING),
        )(*[pltpu.with_memory_space_constraint(t, pltpu.HBM) for t in list(srcs) + list(lands)],
          *([after] if n_after else []))
        return outs[0], outs[1], outs[2], outs[3:3 + n], outs[3 + n:3 + 2 * n], outs[3 + 2 * n]

    def _own_copy(self, w, src, land, local_sems):
        me = _flat(*_my_place())
        return pltpu.make_async_copy(self.part(w, src, me), self.slot(w, land, me), local_sems.at[w])

    def wait(self, started, after, peers=PEER_ORDER, own=True, tag=""):
        send_sems, recv_sems, local_sems, srcs, lands, token = started
        n = len(srcs)

        def body(*refs):
            src, land = refs[:n], refs[n:2 * n]
            s_sems, r_sems, l_sems = refs[2 * n], refs[2 * n + 1], refs[2 * n + 2]
            for w in range(n):
                if own:
                    self._own_copy(w, src[w], land[w], l_sems).wait()
                for r in peers:
                    cp = self._copy(w, r, src[w], land[w], s_sems, r_sems, False)
                    cp.wait_send()
                    cp.wait_recv()

        outs = pl.pallas_call(
            body, name=self.name + "_wait" + tag,
            out_shape=[pltpu.HBM(t.shape, t.dtype) for t in list(srcs) + list(lands)],
            in_specs=[_HBM_SPEC] * (2 * n) + [_SEM_SPEC, _SEM_SPEC, _SEM_SPEC, _ANY_SPEC],
            out_specs=[_HBM_SPEC] * (2 * n),
            input_output_aliases={i: i for i in range(2 * n)},
            compiler_params=pltpu.CompilerParams(has_side_effects=pltpu.SideEffectType.DATAFLOW_SIDE_EFFECTING),
        )(*srcs, *lands, send_sems, recv_sems, local_sems, after)
        return outs[n:], (send_sems, recv_sems, local_sems, outs[:n], outs[n:], token)


def _gather_exchange(name, axes, shard_sizes):
    return _Exchange(name, lambda w, src, dev: src,
                     lambda w, land, dev: _shard_slice(land, axes[w], dev, shard_sizes[w]))


def _scatter_exchange(name, axes, shard_sizes):
    def part(w, src, dev):
        return src if axes[w] is None else _shard_slice(src, axes[w], dev, shard_sizes[w])
    return _Exchange(name, part, lambda w, land, dev: land.at[dev])


def _adamw(name, partials, w, m, v):
    R, C = w.shape
    tr = R
    while tr * C * 4 > (1 << 20) and tr % 16 == 0:
        tr //= 2

    def body(p_ref, w_ref, m_ref, v_ref, g_ref, d_ref, nm_ref, nv_ref):
        g = p_ref[0].astype(F32)
        for jdev in range(1, N_DEV):
            g = g + p_ref[jdev].astype(F32)
        nm = ADAM_B1 * m_ref[...] + (1.0 - ADAM_B1) * g
        nv = ADAM_B2 * v_ref[...] + (1.0 - ADAM_B2) * (g * g)
        m_hat = nm / (1.0 - ADAM_B1 ** ADAM_STEP)
        v_hat = nv / (1.0 - ADAM_B2 ** ADAM_STEP)
        g_ref[...] = g
        d_ref[...] = -ADAM_LR * (m_hat / (jnp.sqrt(v_hat) + ADAM_EPS) + ADAM_WD * w_ref[...])
        nm_ref[...] = nm
        nv_ref[...] = nv

    spec = pl.BlockSpec((tr, C), lambda i: (i, 0))
    return pl.pallas_call(
        body, name=name, grid=(R // tr,),
        in_specs=[pl.BlockSpec((N_DEV, tr, C), lambda i: (0, i, 0)), spec, spec, spec],
        out_specs=[spec] * 4,
        out_shape=[jax.ShapeDtypeStruct((R, C), F32)] * 4,
        compiler_params=pltpu.CompilerParams(dimension_semantics=("parallel",),
                                             vmem_limit_bytes=_vmem_limit(24 << 20)),
    )(partials, w, m, v)


def _local_step(x, cos2, sin_fwd, sin_bwd, project_in, mix_weights, ffn_weights, pool_scale, g_mix, b_mix, g_ff, b_ff,
                target, send):
    S, D = x.shape
    aw = pw = D // 2
    u_col_block = 3
    gate_col0 = 4 * aw

    xb = x.astype(BF16)
    h, w_in = project_in(xb)
    dilated = [d for d in DILATIONS if d > 1]
    qkv = {1: h[None], **dict(zip(dilated, _to_residue_major("qkv_to_rm", h, 0, 3 * aw)))}
    fwd = [_attn_fwd(qkv[d], d, aw) for d in DILATIONS]
    o_attn, l_tot = _attn_combine([f[0] for f in fwd], [f[1] for f in fwd], aw)
    w_pool, w_ba, w_bp, w_out = mix_weights(o_attn)
    p, y_pre, pm = _pool_fwd(h, w_pool, pool_scale, pw, u_col_block)
    y_attn = _branch_attn(o_attn, w_ba)
    y_pool, merged = _branch_pool_merge(pm, w_bp, h, y_attn, gate_col0)
    w1, w2 = ffn_weights(merged)
    x1, x1b, xhat1, rstd1 = _out_proj_ln(merged, w_out, x, g_mix, b_mix)
    a = _ffn_up(x1b, w1)
    z2 = _residual_matmul("ffn_down", a, w2, "nn", x1)
    dz2, dz2b, dg_ff, db_ff, loss = _ln_loss_bwd(z2, g_ff, b_ff, target)

    tok = send("ff2", [_grad_weight("grad_w_ff2", a, dz2b)])
    dh1 = _ffn_down_bwd(dz2b, w2, a, after=tok)
    tok = send("ff1", [_grad_weight("grad_w_ff1", x1b, dh1)])
    dy1 = _residual_matmul("ffn_up_bwd", dh1, w1, "nt", dz2, after=tok)
    dz1, dz1b, dg_mix, db_mix = _ln_bwd(dy1, xhat1, rstd1, g_mix)
    tok = send("out", [_grad_weight("grad_w_out", merged, dz1b)])
    dy_attn, dy_pool, dh = _out_proj_bwd(dz1b, w_out, h, y_attn, y_pool, gate_col0, after=tok)
    tok = send("branch", [_grad_weight("grad_w_branch_attn", o_attn, dy_attn),
                          _grad_weight("grad_w_branch_pool", pm, dy_pool)])
    d_out, stats = _branch_attn_bwd(dy_attn, w_ba, o_attn, l_tot, after=tok)
    dy_pre, d_scale = _branch_pool_bwd(dy_pool, w_bp, y_pre, pool_scale)
    dh, dw_pool = _pool_bwd(dh, dy_pre, p, w_pool, pw, u_col_block)
    d_outs = {1: d_out[None], **dict(zip(dilated, _to_residue_major("dout_to_rm", d_out, 0, aw)))}
    statss = {1: stats[None], **dict(zip(dilated, _to_residue_major("stats_to_rm", stats, 0, HEAD_DIM)))}
    per_pattern = [_attn_bwd(qkv[d], d_outs[d], statss[d], d, aw) for d in DILATIONS]
    dh = _attn_bwd_finish(dh, per_pattern, cos2, sin_bwd, aw)
    small = jnp.concatenate((d_scale, dg_mix, db_mix, dg_ff, db_ff), axis=-1)
    tok = send("in", [_grad_weight("grad_w_in", xb, dh), dw_pool.astype(BF16),
                      small.reshape(small.shape[-1] // HEAD_DIM, HEAD_DIM)])
    grad_x = _residual_matmul("in_proj_bwd", dh, w_in, "nt", dz1, after=tok)
    return loss, grad_x


def _rope_tables(positions):
    half = HEAD_DIM // 2
    inv_freq = ROPE_THETA ** (-jnp.arange(half, dtype=F32) / half)
    ang = positions.astype(F32)[:, None] * inv_freq
    cos, sin = jnp.cos(ang), jnp.sin(ang)
    cos2 = jnp.concatenate([cos, cos], axis=-1)
    sin_fwd = jnp.concatenate([-sin, sin], axis=-1)
    return cos2, sin_fwd, -sin_fwd


def kernel(x, positions, w_in, w_pool, pool_scale, w_branch_attn, w_branch_pool, w_out, ln_mix_g, ln_mix_b, w_ff1, w_ff2, ln_ff_g, ln_ff_b, loss_target, m_w_in, m_w_pool, m_pool_scale, m_w_branch_attn, m_w_branch_pool, m_w_out, m_ln_mix_g, m_ln_mix_b, m_w_ff1, m_w_ff2, m_ln_ff_g, m_ln_ff_b, v_w_in, v_w_pool, v_pool_scale, v_w_branch_attn, v_w_branch_pool, v_w_out, v_ln_mix_g, v_ln_mix_b, v_w_ff1, v_w_ff2, v_ln_ff_g, v_ln_ff_b):
    big_w = (w_in[0], w_pool[0], w_branch_attn[0], w_branch_pool[0], w_out[0], w_ff1[0], w_ff2[0])
    big_m = (m_w_in[0], m_w_pool[0], m_w_branch_attn[0], m_w_branch_pool[0], m_w_out[0], m_w_ff1[0], m_w_ff2[0])
    big_v = (v_w_in[0], v_w_pool[0], v_w_branch_attn[0], v_w_branch_pool[0], v_w_out[0], v_w_ff1[0], v_w_ff2[0])
    shard_axes = (1, 1, 1, 1, 0, 1, 0)
    small_w = (pool_scale, ln_mix_g, ln_mix_b, ln_ff_g, ln_ff_b)
    small_m = (m_pool_scale, m_ln_mix_g, m_ln_mix_b, m_ln_ff_g, m_ln_ff_b)
    small_v = (v_pool_scale, v_ln_mix_g, v_ln_mix_b, v_ln_ff_g, v_ln_ff_b)

    names = ("w_in", "w_pool", "w_branch_attn", "w_branch_pool", "w_out", "w_ff1", "w_ff2")
    axis_of = dict(zip(names, shard_axes))
    shard_of = dict(zip(names, [w.astype(BF16) for w in big_w]))

    def full_buffer(n):
        s, ax = shard_of[n], axis_of[n]
        full = list(s.shape)
        full[ax] *= N_DEV
        return lax.empty(tuple(full), s.dtype)

    def gather_group(tag, group, after):
        ex = _gather_exchange(tag, [axis_of[n] for n in group], [shard_of[n].shape[axis_of[n]] for n in group])
        return ex, ex.start([shard_of[n] for n in group], [full_buffer(n) for n in group], after)

    in_ex, in_started = gather_group("gather_in", ("w_in",), None)
    mix_ex, mix_started = gather_group("gather_mix", ("w_pool", "w_branch_attn", "w_branch_pool", "w_out"),
                                       in_started[-1])
    ffn_ex, ffn_started = gather_group("gather_ffn", ("w_ff1", "w_ff2"), mix_started[-1])
    me = 4 * lax.axis_index("x") + 2 * lax.axis_index("y") + lax.axis_index("c")
    block_cols = shard_of["w_in"].shape[1]

    def project_in(xb):
        h, started, after = None, in_started, ffn_started[-1]
        for stage, peers in enumerate((SIBLING, SAME_CORE_NEIGHBOURS, OTHER_CORE_NEIGHBOURS, DIAGONAL)):
            (w_in_land,), started = in_ex.wait(started, after, peers=peers, own=stage == 0, tag=str(stage))
            blocks = ([me] if stage == 0 else []) + [me ^ r for r in peers]
            h = _in_proj_piece(f"in_proj_{stage}", xb, w_in_land, jnp.stack(blocks).astype(jnp.int32), cos2, sin_fwd,
                               h, block_cols, 2 * (x.shape[-1] // 2) // block_cols)
            after = h
        return h, w_in_land

    groups = {"ff2": ("w_ff2",), "ff1": ("w_ff1",), "out": ("w_out",),
              "branch": ("w_branch_attn", "w_branch_pool"), "in": ("w_in", "w_pool", "small")}
    sent = {}

    def send(key, grads_):
        axes = [axis_of.get(n) for n in groups[key]]
        sizes = [None if ax is None else g.shape[ax] // N_DEV for g, ax in zip(grads_, axes)]
        lands = []
        for g, ax, size in zip(grads_, axes, sizes):
            shard = list(g.shape)
            if ax is not None:
                shard[ax] = size
            lands.append(lax.empty((N_DEV, *shard), g.dtype))
        ex = _scatter_exchange("scatter_" + key, axes, sizes)
        sent[key] = (ex, ex.start(list(grads_), lands))
        return sent[key][1][-1]

    cos2, sin_fwd, sin_bwd = _rope_tables(positions[0])
    loss, grad_x = _local_step(
        x[0], cos2, sin_fwd, sin_bwd, project_in, lambda after: mix_ex.wait(mix_started, after)[0],
        lambda after: ffn_ex.wait(ffn_started, after)[0], pool_scale, ln_mix_g, ln_mix_b, ln_ff_g, ln_ff_b,
        loss_target[0], send)

    state = dict(zip(names, zip(big_w, big_m, big_v)))
    n_small = sum(w.shape[-1] for w in small_w)
    small_2d = (n_small // HEAD_DIM, HEAD_DIM)
    state["small"] = tuple(jnp.concatenate(t, axis=-1).reshape(small_2d) for t in (small_w, small_m, small_v))
    grads, deltas, new_ms, new_vs = {}, {}, {}, {}
    after = grad_x
    for key in ("ff2", "ff1", "out", "branch", "in"):
        ex, started = sent[key]
        for n, part in zip(groups[key], ex.wait(started, after)[0]):
            w, m, v = state[n]
            r2 = (-1, w.shape[-1])
            w2d = w.reshape(r2)
            res = _adamw("adamw_" + n, part.reshape((N_DEV,) + w2d.shape), w2d, m.reshape(r2), v.reshape(r2))
            after = res[0]
            if n == "small":
                small_out = [t.reshape(1, n_small) for t in res]
            else:
                grads[n], deltas[n], new_ms[n], new_vs[n] = (t.reshape((1,) + w.shape) for t in res)
    small_names = ("pool_scale", "ln_mix_g", "ln_mix_b", "ln_ff_g", "ln_ff_b")
    off = 0
    for n, w in zip(small_names, small_w):
        width = w.shape[-1]
        grads[n], deltas[n], new_ms[n], new_vs[n] = (t[:, off:off + width] for t in small_out)
        off += width

    order = ("w_in", "w_pool", "pool_scale", "w_branch_attn", "w_branch_pool", "w_out", "ln_mix_g", "ln_mix_b",
             "w_ff1", "w_ff2", "ln_ff_g", "ln_ff_b")
    total_loss = lax.psum(loss[0, 0], ("x", "y", "c"))
    return (total_loss, grad_x[None], *[grads[n] for n in order], *[deltas[n] for n in order],
            *[new_ms[n] for n in order], *[new_vs[n] for n in order])
```

```python
import functools

import jax
import jax.numpy as jnp
from jax import lax
from jax.experimental import pallas as pl
from jax.experimental.pallas import tpu as pltpu

F32 = jnp.float32
BF16 = jnp.bfloat16

N_DEV = 8
HEAD_DIM = 128
SUB_BLOCK = 128
DILATIONS = (1, 4, 16)
POOL_WINDOWS = (2, 4, 8, 16)
MAX_POOL_WINDOW = 16
POOL_HALO = 128
PERM_ROWS = 512
K_TILE = 1024
LN_EPS = 1e-5
DEEPNORM_ALPHA = 2.0 ** 0.25
ROPE_THETA = 10000.0
ATTN_SCALE = HEAD_DIM ** -0.5
ADAM_LR, ADAM_B1, ADAM_B2, ADAM_EPS, ADAM_WD, ADAM_STEP = 0.001, 0.9, 0.999, 1e-08, 0.01, 10
NEG_BIG = -1e30
VMEM_CAP_V7X = 64 * 1024 * 1024
MESH = pl.DeviceIdType.MESH


def _vmem_limit(est_bytes):
    return int(min(max(est_bytes * 5 // 4 + (4 << 20), 16 << 20), VMEM_CAP_V7X - (6 << 20)))


def _nbytes(shape, dtype):
    n = 1
    for s in shape:
        n *= s
    return n * jnp.dtype(dtype).itemsize


def _mm(name, a, b, form, tiles, outs, epi, extras=(), sequential=False, after=None):
    tm, tn, tk = tiles
    if form == "nn":
        (M, K), (K2, N) = a.shape, b.shape
    elif form == "nt":
        (M, K), (N, K2) = a.shape, b.shape
    else:
        (K, M), (K2, N) = a.shape, b.shape
    assert K == K2, (name, a.shape, b.shape)
    tm, tn, tk = min(tm, M), min(tn, N), min(tk, K)
    assert M % tm == 0 and N % tn == 0 and K % tk == 0, (name, M, N, K, tm, tn, tk)
    grid = (M // tm, N // tn, K // tk)
    nk = grid[2]
    if form == "nn":
        a_spec = pl.BlockSpec((tm, tk), lambda i, j, k: (i, k))
        b_spec = pl.BlockSpec((tk, tn), lambda i, j, k: (k, j))
        contract = ((1,), (0,))
    elif form == "nt":
        a_spec = pl.BlockSpec((tm, tk), lambda i, j, k: (i, k))
        b_spec = pl.BlockSpec((tn, tk), lambda i, j, k: (j, k))
        contract = ((1,), (1,))
    else:
        a_spec = pl.BlockSpec((tk, tm), lambda i, j, k: (k, i))
        b_spec = pl.BlockSpec((tk, tn), lambda i, j, k: (k, j))
        contract = ((0,), (0,))
    n_ex, n_out = len(extras), len(outs)
    n_after = 0 if after is None else 1

    def body(a_ref, b_ref, *rest):
        ex_refs = rest[:n_ex]
        rest = rest[n_ex + n_after:]
        out_refs = rest[:n_out]
        i, j, k = pl.program_id(0), pl.program_id(1), pl.program_id(2)

        def prod():
            return lax.dot_general(a_ref[...].astype(BF16), b_ref[...].astype(BF16),
                                   (contract, ((), ())), preferred_element_type=F32)

        if nk == 1:
            epi(prod(), ex_refs, out_refs, i, j)
        else:
            acc = rest[n_out]

            @pl.when(k == 0)
            def _():
                acc[...] = prod()

            @pl.when(k > 0)
            def _():
                acc[...] += prod()

            @pl.when(k == nk - 1)
            def _():
                epi(acc[...], ex_refs, out_refs, i, j)

    est = 2 * (_nbytes(a_spec.block_shape, a.dtype) + _nbytes(b_spec.block_shape, b.dtype))
    est += sum(2 * _nbytes(bs, arr.dtype) for arr, bs, _ in extras)
    est += sum(2 * _nbytes(bs, dt) for _, dt, bs, _ in outs)
    est += 4 * tm * tn * 4
    sem = ("arbitrary",) * 3 if sequential else ("parallel", "parallel", "arbitrary")
    return pl.pallas_call(
        body, name=name, grid=grid,
        in_specs=([a_spec, b_spec] + [pl.BlockSpec(bs, im) for _, bs, im in extras]
                  + [pl.BlockSpec(memory_space=pl.ANY)] * n_after),
        out_specs=[pl.BlockSpec(bs, im) for _, _, bs, im in outs],
        out_shape=[jax.ShapeDtypeStruct(sh, dt) for sh, dt, _, _ in outs],
        scratch_shapes=[pltpu.VMEM((tm, tn), F32)] if nk > 1 else [],
        compiler_params=pltpu.CompilerParams(dimension_semantics=sem, vmem_limit_bytes=_vmem_limit(est)),
    )(a, b, *[arr for arr, _, _ in extras], *([after] if n_after else []))


def _tile_out(shape, dtype, tm, tn):
    return (shape, dtype, (tm, tn), lambda i, j, k: (i, j))


def _row_sum_out(width):
    return ((1, width), F32, (1, width), lambda i, j, k: (0, 0))


def _accumulate_rows(ref, value, i):
    @pl.when(i == 0)
    def _():
        ref[...] = value

    @pl.when(i > 0)
    def _():
        ref[...] += value


def _layer_norm_bwd(dy, xhat, rstd, g):
    dxh = dy * g
    m1 = jnp.mean(dxh, axis=-1, keepdims=True)
    m2 = jnp.mean(dxh * xhat, axis=-1, keepdims=True)
    return rstd * (dxh - m1 - xhat * m2)


def _rope_apply(t, cos2, sin_signed):
    return t * cos2 + pltpu.roll(t, HEAD_DIM // 2, axis=1) * sin_signed


def _in_proj_piece(name, xb, w_in, col_blocks, cos2, sin_fwd, h_so_far, block_cols, n_rope_blocks, own_shard=False,
                   after=None):
    S, D = xb.shape
    W = w_in.shape[1] * (N_DEV if own_shard else 1)
    tm, tk = min(2048, S), min(K_TILE, D)
    nk = D // tk
    n_blocks = col_blocks.shape[0]

    def body(cols_ref, x_ref, w_ref, cos_ref, sin_ref, *rest):
        h_ref, acc = rest[-2], rest[-1]
        j, k = pl.program_id(1), pl.program_id(2)

        def prod():
            return _dot_nn(x_ref[...], w_ref[...])

        @pl.when(k == 0)
        def _():
            acc[...] = prod()

        @pl.when(k > 0)
        def _():
            acc[...] += prod()

        @pl.when(jnp.logical_and(k == nk - 1, cols_ref[j] < n_rope_blocks))
        def _():
            c, s = cos_ref[...], sin_ref[...]
            for hd in range(block_cols // HEAD_DIM):
                sl = slice(hd * HEAD_DIM, (hd + 1) * HEAD_DIM)
                h_ref[:, sl] = _rope_apply(acc[:, sl], c, s).astype(BF16)

        @pl.when(jnp.logical_and(k == nk - 1, cols_ref[j] >= n_rope_blocks))
        def _():
            h_ref[...] = acc[...].astype(BF16)

    row = pl.BlockSpec((tm, HEAD_DIM), lambda i, j, k, cols: (i, 0))
    carried = ([] if h_so_far is None else [h_so_far]) + ([] if after is None else [after])
    est = 2 * (tm * tk * 2 + tk * block_cols * 2 + tm * block_cols * 2 + 2 * tm * HEAD_DIM * 4) + 3 * tm * block_cols * 4
    return pl.pallas_call(
        body, name=name,
        grid_spec=pltpu.PrefetchScalarGridSpec(
            num_scalar_prefetch=1, grid=(S // tm, n_blocks, nk),
            in_specs=[pl.BlockSpec((tm, tk), lambda i, j, k, cols: (i, k)),
                      pl.BlockSpec((tk, block_cols), lambda i, j, k, cols: (k, 0 if own_shard else cols[j])), row, row]
                     + [pl.BlockSpec(memory_space=pl.ANY)] * len(carried),
            out_specs=pl.BlockSpec((tm, block_cols), lambda i, j, k, cols: (i, cols[j])),
            scratch_shapes=[pltpu.VMEM((tm, block_cols), F32)]),
        out_shape=jax.ShapeDtypeStruct((S, W), BF16),
        input_output_aliases={} if h_so_far is None else {5: 0},
        compiler_params=pltpu.CompilerParams(dimension_semantics=("parallel", "arbitrary", "arbitrary"),
                                             vmem_limit_bytes=_vmem_limit(est)),
    )(col_blocks, xb, w_in, cos2, sin_fwd, *carried)


def _attn_mask(mb):
    qi = lax.broadcasted_iota(jnp.int32, (SUB_BLOCK, 2 * SUB_BLOCK), 0)
    kj = lax.broadcasted_iota(jnp.int32, (SUB_BLOCK, 2 * SUB_BLOCK), 1)
    prev = jnp.logical_and(jnp.logical_and(kj < SUB_BLOCK, kj >= qi), mb > 0)
    cur = jnp.logical_and(kj >= SUB_BLOCK, kj - SUB_BLOCK <= qi)
    return jnp.logical_or(prev, cur)


def _both_blocks(prev_ref, cur_ref, sl):
    return jnp.concatenate([prev_ref[:, sl], cur_ref[:, sl]], axis=0)


def _dot_nt(a, b):
    return lax.dot_general(a, b, (((1,), (1,)), ((), ())), preferred_element_type=F32)


def _dot_tn(a, b):
    return lax.dot_general(a, b, (((0,), (0,)), ((), ())), preferred_element_type=F32)


def _dot_nn(a, b):
    return lax.dot_general(a, b, (((1,), (0,)), ((), ())), preferred_element_type=F32)


def _perm_matrix(d, to_residue_major):
    g = PERM_ROWS // d
    i = lax.broadcasted_iota(jnp.int32, (PERM_ROWS, PERM_ROWS), 0)
    j = lax.broadcasted_iota(jnp.int32, (PERM_ROWS, PERM_ROWS), 1)
    if to_residue_major:
        hit = j == (i % g) * d + i // g
    else:
        hit = j == (i % d) * g + i // d
    return hit.astype(BF16)


def _permute_rows(perm, x, terms=3):
    if x.dtype == BF16:
        return _dot_nn(perm, x)
    hi = x.astype(BF16)
    r1 = x - hi.astype(F32)
    mid = r1.astype(BF16)
    out = _dot_nn(perm, hi) + _dot_nn(perm, mid)
    if terms == 3:
        out = out + _dot_nn(perm, (r1 - mid.astype(F32)).astype(BF16))
    return out


def _rm_block(d, width):
    return pl.BlockSpec((d, PERM_ROWS // d, width), lambda i: (0, i, 0))


def _to_residue_major(name, x, col_block, width):
    S = x.shape[0]
    dils = [d for d in DILATIONS if d > 1]
    chunk = min(width, 1024)

    def body(x_ref, *out_refs):
        for d, o_ref in zip(dils, out_refs):
            perm = _perm_matrix(d, True)
            for c0 in range(0, width, chunk):
                cw = min(chunk, width - c0)
                y = _permute_rows(perm, x_ref[:, c0:c0 + cw])
                o_ref[:, :, c0:c0 + cw] = y.astype(x.dtype).reshape(d, PERM_ROWS // d, cw)

    return pl.pallas_call(
        body, name=name, grid=(S // PERM_ROWS,),
        in_specs=[pl.BlockSpec((PERM_ROWS, width), lambda i: (i, col_block))],
        out_specs=[_rm_block(d, width) for d in dils],
        out_shape=[jax.ShapeDtypeStruct((d, S // d, width), x.dtype) for d in dils],
        compiler_params=pltpu.CompilerParams(dimension_semantics=("parallel",),
                                             vmem_limit_bytes=_vmem_limit(32 << 20)),
    )(x)


def _qkv_specs(aw):
    def spec(col, prev):
        if prev:
            return pl.BlockSpec((None, SUB_BLOCK, aw), lambda r, mb: (r, jnp.maximum(mb - 1, 0), col))
        return pl.BlockSpec((None, SUB_BLOCK, aw), lambda r, mb: (r, mb, col))
    return [spec(0, False), spec(1, True), spec(1, False), spec(2, True), spec(2, False)]


def _put_column(tile, col, value):
    lane = lax.broadcasted_iota(jnp.int32, tile.shape, 1)
    return jnp.where(lane == col, value, tile)


def _attn_fwd(qkv, d, aw):
    _, rows, _ = qkv.shape
    n_heads = aw // HEAD_DIM
    nb = rows // SUB_BLOCK

    def body(q_ref, kp_ref, kc_ref, vp_ref, vc_ref, o_ref, lse_ref, s_buf, p_buf):
        mask = _attn_mask(pl.program_id(1))
        for hd in range(n_heads):
            sl = slice(hd * HEAD_DIM, (hd + 1) * HEAD_DIM)
            s_buf[hd] = _dot_nt(q_ref[:, sl], _both_blocks(kp_ref, kc_ref, sl))
        lse_tile = jnp.zeros((SUB_BLOCK, HEAD_DIM), F32)
        inv_tile = jnp.zeros((SUB_BLOCK, HEAD_DIM), F32)
        for hd in range(n_heads):
            s = jnp.where(mask, s_buf[hd] * ATTN_SCALE, NEG_BIG)
            m = jnp.max(s, axis=-1, keepdims=True)
            p = jnp.exp(s - m)
            l = jnp.sum(p, axis=-1, keepdims=True)
            p_buf[hd] = p.astype(BF16)
            lse_tile = _put_column(lse_tile, hd, m + jnp.log(l))
            inv_tile = _put_column(inv_tile, hd, 1.0 / l)
        lse_ref[...] = lse_tile
        for hd in range(n_heads):
            sl = slice(hd * HEAD_DIM, (hd + 1) * HEAD_DIM)
            o = _dot_nn(p_buf[hd], _both_blocks(vp_ref, vc_ref, sl))
            o_ref[:, sl] = o * inv_tile[:, hd:hd + 1]

    return pl.pallas_call(
        body, name=f"attn_fwd_d{d}", grid=(d, nb),
        in_specs=_qkv_specs(aw),
        out_specs=[pl.BlockSpec((None, SUB_BLOCK, aw), lambda r, mb: (r, mb, 0)),
                   pl.BlockSpec((None, SUB_BLOCK, HEAD_DIM), lambda r, mb: (r, mb, 0))],
        out_shape=[jax.ShapeDtypeStruct((d, rows, aw), F32), jax.ShapeDtypeStruct((d, rows, HEAD_DIM), F32)],
        scratch_shapes=[pltpu.VMEM((n_heads, SUB_BLOCK, 2 * SUB_BLOCK), F32),
                        pltpu.VMEM((n_heads, SUB_BLOCK, 2 * SUB_BLOCK), BF16)],
        compiler_params=pltpu.CompilerParams(dimension_semantics=("parallel", "parallel"),
                                             vmem_limit_bytes=_vmem_limit(16 << 20)),
    )(qkv, qkv, qkv, qkv, qkv)


def _attn_combine(outs, lses, aw):
    S = outs[0].shape[1]
    n_heads = aw // HEAD_DIM
    n_pat = len(DILATIONS)

    def body(*refs):
        o_refs, l_refs = refs[:n_pat], refs[n_pat:2 * n_pat]
        o_ref, lt_ref = refs[2 * n_pat], refs[2 * n_pat + 1]
        o_nat, l_nat = [], []
        for d, o_r, l_r in zip(DILATIONS, o_refs, l_refs):
            o_p = o_r[...].reshape(PERM_ROWS, aw)
            l_p = l_r[...].reshape(PERM_ROWS, HEAD_DIM)
            if d > 1:
                perm = _perm_matrix(d, False)
                o_p, l_p = _permute_rows(perm, o_p, terms=2), _permute_rows(perm, l_p)
            o_nat.append(o_p)
            l_nat.append(l_p)
        mx = functools.reduce(jnp.maximum, l_nat)
        es = [jnp.exp(l_p - mx) for l_p in l_nat]
        den = functools.reduce(jnp.add, es)
        lt_ref[...] = mx + jnp.log(den)
        ws = [e / den for e in es]
        for hd in range(n_heads):
            sl = slice(hd * HEAD_DIM, (hd + 1) * HEAD_DIM)
            o = ws[0][:, hd:hd + 1] * o_nat[0][:, sl]
            for pi in range(1, n_pat):
                o = o + ws[pi][:, hd:hd + 1] * o_nat[pi][:, sl]
            o_ref[:, sl] = o.astype(BF16)

    return pl.pallas_call(
        body, name="attn_combine", grid=(S // PERM_ROWS,),
        in_specs=[_rm_block(d, aw) for d in DILATIONS] + [_rm_block(d, HEAD_DIM) for d in DILATIONS],
        out_specs=[pl.BlockSpec((PERM_ROWS, aw), lambda i: (i, 0)), pl.BlockSpec((PERM_ROWS, HEAD_DIM), lambda i: (i, 0))],
        out_shape=[jax.ShapeDtypeStruct((S, aw), BF16), jax.ShapeDtypeStruct((S, HEAD_DIM), F32)],
        compiler_params=pltpu.CompilerParams(dimension_semantics=("parallel",),
                                             vmem_limit_bytes=_vmem_limit(40 << 20)),
    )(*outs, *lses)


def _band(tm, width, w, row_offset, transpose):
    t = lax.broadcasted_iota(jnp.int32, (tm, width), 0)
    u = lax.broadcasted_iota(jnp.int32, (tm, width), 1)
    dist = (u - t - row_offset) if transpose else (t + row_offset - u)
    return jnp.logical_and(dist >= 0, dist < w).astype(BF16)


def _pool_fwd(h, w_pool, pool_scale, pw, u_col_block):
    S, W = h.shape
    n_groups = len(POOL_WINDOWS)
    gw = pw // n_groups
    tm = min(512, S)
    halo_per_tile = tm // POOL_HALO

    def body(uc_ref, uh_ref, w_ref, sc_ref, p_ref, y_ref, pm_ref):
        i = pl.program_id(0)
        t_abs = i * tm + lax.broadcasted_iota(jnp.int32, (tm, 1), 0)
        for g, w in enumerate(POOL_WINDOWS):
            sl = slice(g * gw, (g + 1) * gw)
            uc = uc_ref[:, sl]
            uh = jnp.where(i > 0, uh_ref[:, sl], jnp.zeros((POOL_HALO, gw), BF16))
            ssum = _dot_nn(_band(tm, tm, w, 0, False), uc) + _dot_nn(_band(tm, POOL_HALO, w, POOL_HALO, False), uh)
            cnt = jnp.minimum(t_abs + 1, w).astype(F32)
            p = (ssum / cnt - uc.astype(F32)).astype(BF16)
            y = _dot_nn(p, w_ref[g])
            p_ref[:, sl] = p
            y_ref[:, sl] = y.astype(BF16)
            pm_ref[:, sl] = (y * sc_ref[:, sl]).astype(BF16)

    row = pl.BlockSpec((tm, pw), lambda i: (i, 0))
    return pl.pallas_call(
        body, name="pool_fwd", grid=(S // tm,),
        in_specs=[pl.BlockSpec((tm, pw), lambda i: (i, u_col_block)),
                  pl.BlockSpec((POOL_HALO, pw), lambda i: (jnp.maximum(i * halo_per_tile - 1, 0), u_col_block)),
                  pl.BlockSpec((n_groups, gw, gw), lambda i: (0, 0, 0)),
                  pl.BlockSpec((1, pw), lambda i: (0, 0))],
        out_specs=[row, row, row],
        out_shape=[jax.ShapeDtypeStruct((S, pw), BF16)] * 3,
        compiler_params=pltpu.CompilerParams(dimension_semantics=("parallel",),
                                             vmem_limit_bytes=_vmem_limit(24 << 20)),
    )(h, h, w_pool, pool_scale)


def _branch_attn(o_attn, w_ba):
    S, _ = o_attn.shape
    D = w_ba.shape[1]
    tm, tn = min(1024, S), D

    def epi(acc, ex, out, i, j):
        out[0][...] = acc.astype(BF16)

    (y,) = _mm("branch_attn", o_attn, w_ba, "nn", (tm, tn, 1024), [_tile_out((S, D), BF16, tm, tn)], epi)
    return y


def _branch_pool_merge(pm, w_bp, h, y_attn, gate_col0):
    S, _ = pm.shape
    D = w_bp.shape[1]
    tm, tn = min(512, S), D
    ga0, gp0 = gate_col0 // tn, (gate_col0 + D) // tn

    def epi(acc, ex, out, i, j):
        ga_ref, gp_ref, ya_ref = ex
        yp_ref, mg_ref = out
        yp = acc.astype(BF16)
        yp_ref[...] = yp
        mg = (jax.nn.sigmoid(ga_ref[...].astype(F32)) * ya_ref[...].astype(F32)
              + jax.nn.sigmoid(gp_ref[...].astype(F32)) * acc)
        mg_ref[...] = mg.astype(BF16)

    y_pool, merged = _mm(
        "branch_pool_merge", pm, w_bp, "nn", (tm, tn, 1024),
        [_tile_out((S, D), BF16, tm, tn), _tile_out((S, D), BF16, tm, tn)], epi,
        extras=[(h, (tm, tn), lambda i, j, k: (i, ga0 + j)), (h, (tm, tn), lambda i, j, k: (i, gp0 + j)),
                (y_attn, (tm, tn), lambda i, j, k: (i, j))])
    return y_pool, merged


def _layer_norm_rows(z, g, b):
    mu = jnp.mean(z, axis=-1, keepdims=True)
    zc = z - mu
    var = jnp.mean(zc * zc, axis=-1, keepdims=True)
    rstd = lax.rsqrt(var + LN_EPS)
    xhat = zc * rstd
    return xhat * g + b, xhat, rstd


def _out_proj_ln(merged, w_out, x, g, b):
    S, D = x.shape
    tm = min(256, S)

    def epi(acc, ex, out, i, j):
        x_ref, g_ref, b_ref = ex
        x1_ref, x1b_ref, xh_ref, rs_ref = out
        y, xhat, rstd = _layer_norm_rows(DEEPNORM_ALPHA * x_ref[...] + acc, g_ref[...], b_ref[...])
        x1_ref[...] = y
        x1b_ref[...] = y.astype(BF16)
        xh_ref[...] = xhat
        rs_ref[...] = jnp.broadcast_to(rstd, (tm, HEAD_DIM))

    row = lambda i, j, k: (i, 0)
    vec = lambda i, j, k: (0, 0)
    return _mm("out_proj_ln", merged, w_out, "nn", (tm, D, D),
               [((S, D), F32, (tm, D), row), ((S, D), BF16, (tm, D), row), ((S, D), F32, (tm, D), row),
                ((S, HEAD_DIM), F32, (tm, HEAD_DIM), row)], epi,
               extras=[(x, (tm, D), row), (g, (1, D), vec), (b, (1, D), vec)])


def _ffn_up(x1b, w1):
    S, D = x1b.shape
    F = w1.shape[1]
    tm, tn = min(1024, S), min(2048, F)

    def epi(acc, ex, out, i, j):
        r = jnp.maximum(acc, 0.0)
        out[0][...] = (r * r).astype(BF16)

    (a,) = _mm("ffn_up", x1b, w1, "nn", (tm, tn, K_TILE), [_tile_out((S, F), BF16, tm, tn)], epi)
    return a


def _residual_matmul(name, a, w, form, resid, after=None):
    S, D = resid.shape
    tm, tn = min(1024, S), min(2048, D)

    def epi(acc, ex, out, i, j):
        out[0][...] = DEEPNORM_ALPHA * ex[0][...] + acc

    (z,) = _mm(name, a, w, form, (tm, tn, K_TILE), [_tile_out((S, D), F32, tm, tn)], epi,
               extras=[(resid, (tm, tn), lambda i, j, k: (i, j))], after=after)
    return z


def _row_kernel(name, body, row_inputs, vec_inputs, row_outputs, sum_widths, tr):
    S = row_inputs[0].shape[0]
    row = lambda w: pl.BlockSpec((tr, w), lambda i: (i, 0))
    vec = lambda w: pl.BlockSpec((1, w), lambda i: (0, 0))

    def wrapped(*refs):
        body(pl.program_id(0), *refs)

    return pl.pallas_call(
        wrapped, name=name, grid=(S // tr,),
        in_specs=[row(t.shape[1]) for t in row_inputs] + [vec(t.shape[1]) for t in vec_inputs],
        out_specs=[row(w) for w, _ in row_outputs] + [vec(w) for w in sum_widths],
        out_shape=([jax.ShapeDtypeStruct((S, w), dt) for w, dt in row_outputs]
                   + [jax.ShapeDtypeStruct((1, w), F32) for w in sum_widths]),
        compiler_params=pltpu.CompilerParams(dimension_semantics=("arbitrary",),
                                             vmem_limit_bytes=_vmem_limit(40 << 20)),
    )(*row_inputs, *vec_inputs)


def _ln_loss_bwd(z2, g, b, target):
    S, D = z2.shape

    def body(i, z_ref, t_ref, g_ref, b_ref, dz_ref, dzb_ref, dg_ref, db_ref, loss_ref):
        gv = g_ref[...]
        y, xhat, rstd = _layer_norm_rows(z_ref[...], gv, b_ref[...])
        err = y - t_ref[...]
        loss = 0.5 * jnp.sum(jnp.mean(err * err, axis=-1, keepdims=True), axis=0, keepdims=True)
        dy = err * (1.0 / D)
        dz = _layer_norm_bwd(dy, xhat, rstd, gv)
        dz_ref[...] = dz
        dzb_ref[...] = dz.astype(BF16)
        _accumulate_rows(dg_ref, jnp.sum(dy * xhat, axis=0, keepdims=True), i)
        _accumulate_rows(db_ref, jnp.sum(dy, axis=0, keepdims=True), i)
        _accumulate_rows(loss_ref, jnp.broadcast_to(loss, (1, HEAD_DIM)), i)

    return _row_kernel("ln_loss_bwd", body, [z2, target], [g, b], [(D, F32), (D, BF16)], [D, D, HEAD_DIM],
                       min(256, S))


def _ln_bwd(dy, xhat, rstd, g):
    S, D = dy.shape

    def body(i, dy_ref, xh_ref, rs_ref, g_ref, dz_ref, dzb_ref, dg_ref, db_ref):
        dyv, xhat_v = dy_ref[...], xh_ref[...]
        dz = _layer_norm_bwd(dyv, xhat_v, rs_ref[:, :1], g_ref[...])
        dz_ref[...] = dz
        dzb_ref[...] = dz.astype(BF16)
        _accumulate_rows(dg_ref, jnp.sum(dyv * xhat_v, axis=0, keepdims=True), i)
        _accumulate_rows(db_ref, jnp.sum(dyv, axis=0, keepdims=True), i)

    return _row_kernel("ln_bwd", body, [dy, xhat, rstd], [g], [(D, F32), (D, BF16)], [D, D], min(256, S))


def _grad_weight(name, act, cot):
    M, N = act.shape[1], cot.shape[1]
    tm, tn = min(1024, M), min(2048, N)

    def epi(acc, ex, out, i, j):
        out[0][...] = acc.astype(BF16)

    (g,) = _mm(name, act, cot, "tn", (tm, tn, K_TILE), [_tile_out((M, N), BF16, tm, tn)], epi)
    return g


def _ffn_down_bwd(dz2b, w2, a, after=None):
    S, D = dz2b.shape
    F = w2.shape[0]
    tm, tn = min(1024, S), min(2048, F)

    def epi(acc, ex, out, i, j):
        out[0][...] = (acc * (2.0 * jnp.sqrt(ex[0][...])).astype(F32)).astype(BF16)

    (dh1,) = _mm("ffn_down_bwd", dz2b, w2, "nt", (tm, tn, K_TILE), [_tile_out((S, F), BF16, tm, tn)], epi,
                 extras=[(a, (tm, tn), lambda i, j, k: (i, j))], after=after)
    return dh1


def _out_proj_bwd(dz1b, w_out, h, y_attn, y_pool, gate_col0, after=None):
    S, D = dz1b.shape
    W = h.shape[1]
    tm = min(256, S)
    assert gate_col0 == 2 * D and W == 4 * D

    def epi(acc, ex, out, i, j):
        gates_ref, ya_ref, yp_ref = ex
        dya_ref, dyp_ref, dh_ref = out
        sa = jax.nn.sigmoid(gates_ref[:, :D].astype(F32))
        sp = jax.nn.sigmoid(gates_ref[:, D:].astype(F32))
        dya_ref[...] = (acc * sa).astype(BF16)
        dyp_ref[...] = (acc * sp).astype(BF16)
        dh_ref[:, :D] = (acc * ya_ref[...].astype(F32) * (sa * (1.0 - sa))).astype(BF16)
        dh_ref[:, D:] = (acc * yp_ref[...].astype(F32) * (sp * (1.0 - sp))).astype(BF16)

    row = lambda i, j, k: (i, 0)
    return _mm("out_proj_bwd", dz1b, w_out, "nt", (tm, D, D),
               [((S, D), BF16, (tm, D), row), ((S, D), BF16, (tm, D), row),
                ((S, W), BF16, (tm, 2 * D), lambda i, j, k: (i, 1))], epi,
               extras=[(h, (tm, 2 * D), lambda i, j, k: (i, 1)), (y_attn, (tm, D), row), (y_pool, (tm, D), row)],
               after=after)


def _branch_attn_bwd(dy_attn, w_ba, o_attn, l_tot, after=None):
    S, D = dy_attn.shape
    aw = w_ba.shape[0]
    n_heads = aw // HEAD_DIM
    tm = min(512, S)

    def epi(acc, ex, out, i, j):
        do_ref, st_ref = out
        do_ref[...] = acc.astype(BF16)
        o = ex[0][...].astype(F32)
        stats = ex[1][...]
        for hd in range(n_heads):
            sl = slice(hd * HEAD_DIM, (hd + 1) * HEAD_DIM)
            stats = _put_column(stats, n_heads + hd, jnp.sum(acc[:, sl] * o[:, sl], axis=-1, keepdims=True))
        st_ref[...] = stats

    row = lambda i, j, k: (i, 0)
    return _mm("branch_attn_bwd", dy_attn, w_ba, "nt", (tm, aw, D),
               [((S, aw), BF16, (tm, aw), row), ((S, HEAD_DIM), F32, (tm, HEAD_DIM), row)], epi,
               extras=[(o_attn, (tm, aw), row), (l_tot, (tm, HEAD_DIM), row)], after=after)


def _branch_pool_bwd(dy_pool, w_bp, y_pre, pool_scale):
    S, D = dy_pool.shape
    pw = w_bp.shape[0]
    tm = min(512, S)

    def epi(acc, ex, out, i, j):
        y_ref, sc_ref = ex
        dyp_ref, dsc_ref = out
        dyp_ref[...] = (acc * sc_ref[...]).astype(BF16)
        _accumulate_rows(dsc_ref, jnp.sum(acc * y_ref[...].astype(F32), axis=0, keepdims=True), i)

    row = lambda i, j, k: (i, 0)
    return _mm("branch_pool_bwd", dy_pool, w_bp, "nt", (tm, pw, D),
               [((S, pw), BF16, (tm, pw), row), _row_sum_out(pw)], epi,
               extras=[(y_pre, (tm, pw), row), (pool_scale, (1, pw), lambda i, j, k: (0, 0))],
               sequential=True)


def _pool_bwd(dh, dy_pre, p, w_pool, pw, u_col_block):
    S, W = dh.shape
    n_groups = len(POOL_WINDOWS)
    gw = pw // n_groups
    tm = min(512, S)
    n_tiles = S // tm
    halo_per_tile = tm // POOL_HALO
    n_halo_blocks = S // POOL_HALO

    def body(dh_in_ref, dyc_ref, dyh_ref, p_ref, w_ref, dh_ref, dw_ref):
        del dh_in_ref
        i = pl.program_id(0)
        t_cur = i * tm + lax.broadcasted_iota(jnp.int32, (tm, 1), 0)
        t_halo = (i + 1) * tm + lax.broadcasted_iota(jnp.int32, (POOL_HALO, 1), 0)
        for g, w in enumerate(POOL_WINDOWS):
            sl = slice(g * gw, (g + 1) * gw)
            wg = w_ref[g]
            dyc = dyc_ref[:, sl]
            dyh = jnp.where(i < n_tiles - 1, dyh_ref[:, sl], jnp.zeros((POOL_HALO, gw), BF16))
            dp_cur = _dot_nt(dyc, wg)
            dp_halo = _dot_nt(dyh, wg)
            dpc_cur = (dp_cur / jnp.minimum(t_cur + 1, w).astype(F32)).astype(BF16)
            dpc_halo = (dp_halo / jnp.minimum(t_halo + 1, w).astype(F32)).astype(BF16)
            du = (_dot_nn(_band(tm, tm, w, 0, True), dpc_cur)
                  + _dot_nn(_band(tm, POOL_HALO, w, -tm, True), dpc_halo) - dp_cur)
            dh_ref[:, sl] = du.astype(BF16)
            dw = _dot_tn(p_ref[:, sl], dyc)

            @pl.when(i == 0)
            def _():
                dw_ref[g] = dw

            @pl.when(i > 0)
            def _():
                dw_ref[g] += dw

    row = pl.BlockSpec((tm, pw), lambda i: (i, 0))
    dh_new, dw_pool = pl.pallas_call(
        body, name="pool_bwd", grid=(n_tiles,),
        in_specs=[pl.BlockSpec(memory_space=pl.ANY), row,
                  pl.BlockSpec((POOL_HALO, pw), lambda i: (jnp.minimum((i + 1) * halo_per_tile, n_halo_blocks - 1), 0)),
                  row, pl.BlockSpec((n_groups, gw, gw), lambda i: (0, 0, 0))],
        out_specs=[pl.BlockSpec((tm, pw), lambda i: (i, u_col_block)),
                   pl.BlockSpec((n_groups, gw, gw), lambda i: (0, 0, 0))],
        out_shape=[jax.ShapeDtypeStruct((S, W), BF16), jax.ShapeDtypeStruct((n_groups, gw, gw), F32)],
        input_output_aliases={0: 0},
        compiler_params=pltpu.CompilerParams(dimension_semantics=("arbitrary",),
                                             vmem_limit_bytes=_vmem_limit(24 << 20)),
    )(dh, dy_pre, dy_pre, p, w_pool)
    return dh_new, dw_pool


def _attn_bwd(qkv, d_out, stats, d, aw):
    _, rows, _ = qkv.shape
    n_heads = aw // HEAD_DIM
    nb = rows // SUB_BLOCK
    n_blocks = d * nb

    def body(q_ref, kp_ref, kc_ref, vp_ref, vc_ref, do_ref, st_ref, dq_ref, dk_ref, dv_ref,
             carry_k, carry_v, s_buf, dp_buf, p_buf, ds_buf):
        step = pl.program_id(0)

        @pl.when(step == 0)
        def _():
            carry_k[...] = jnp.zeros_like(carry_k)
            carry_v[...] = jnp.zeros_like(carry_v)

        @pl.when(step < n_blocks)
        def _():
            mask = _attn_mask(step % nb)
            st = st_ref[...]
            for hd in range(n_heads):
                sl = slice(hd * HEAD_DIM, (hd + 1) * HEAD_DIM)
                s_buf[hd] = _dot_nt(q_ref[:, sl], _both_blocks(kp_ref, kc_ref, sl))
                dp_buf[hd] = _dot_nt(do_ref[:, sl], _both_blocks(vp_ref, vc_ref, sl))
            for hd in range(n_heads):
                lt, dl = st[:, hd:hd + 1], st[:, n_heads + hd:n_heads + hd + 1]
                p = jnp.where(mask, jnp.exp(jnp.where(mask, s_buf[hd] * ATTN_SCALE - lt, NEG_BIG)), 0.0)
                p_buf[hd] = p.astype(BF16)
                ds_buf[hd] = (p * (dp_buf[hd] - dl) * ATTN_SCALE).astype(BF16)
            for hd in range(n_heads):
                sl = slice(hd * HEAD_DIM, (hd + 1) * HEAD_DIM)
                dq_ref[:, sl] = _dot_nn(ds_buf[hd], _both_blocks(kp_ref, kc_ref, sl)).astype(BF16)
                dk_both = _dot_tn(ds_buf[hd], q_ref[:, sl])
                dv_both = _dot_tn(p_buf[hd], do_ref[:, sl])
                dk_ref[:, sl] = (carry_k[:, sl] + dk_both[:SUB_BLOCK]).astype(BF16)
                dv_ref[:, sl] = (carry_v[:, sl] + dv_both[:SUB_BLOCK]).astype(BF16)
                carry_k[:, sl] = dk_both[SUB_BLOCK:]
                carry_v[:, sl] = dv_both[SUB_BLOCK:]

        @pl.when(step == n_blocks)
        def _():
            dk_ref[...] = carry_k[...].astype(BF16)
            dv_ref[...] = carry_v[...].astype(BF16)

    def cur(step):
        return jnp.minimum(step, n_blocks - 1)

    def qkv_spec(col, prev):
        if prev:
            return pl.BlockSpec((SUB_BLOCK, aw), lambda s: (jnp.maximum(cur(s) - 1, 0), col))
        return pl.BlockSpec((SUB_BLOCK, aw), lambda s: (cur(s), col))

    def at_cur(w):
        return pl.BlockSpec((SUB_BLOCK, w), lambda s: (cur(s), 0))

    finished = pl.BlockSpec((SUB_BLOCK, aw), lambda s: (jnp.maximum(s - 1, 0), 0))
    pair = (n_heads, SUB_BLOCK, 2 * SUB_BLOCK)
    flat = lambda t: t.reshape(d * rows, t.shape[-1])
    qkv2 = flat(qkv)
    outs = pl.pallas_call(
        body, name=f"attn_bwd_d{d}", grid=(n_blocks + 1,),
        in_specs=[qkv_spec(0, False), qkv_spec(1, True), qkv_spec(1, False), qkv_spec(2, True), qkv_spec(2, False),
                  at_cur(aw), at_cur(HEAD_DIM)],
        out_specs=[at_cur(aw), finished, finished],
        out_shape=[jax.ShapeDtypeStruct((d * rows, aw), BF16)] * 3,
        scratch_shapes=[pltpu.VMEM((SUB_BLOCK, aw), F32), pltpu.VMEM((SUB_BLOCK, aw), F32),
                        pltpu.VMEM(pair, F32), pltpu.VMEM(pair, F32), pltpu.VMEM(pair, BF16), pltpu.VMEM(pair, BF16)],
        compiler_params=pltpu.CompilerParams(dimension_semantics=("arbitrary",),
                                             vmem_limit_bytes=_vmem_limit(24 << 20)),
    )(qkv2, qkv2, qkv2, qkv2, qkv2, flat(d_out), flat(stats))
    return [t.reshape(d, rows, aw) for t in outs]


def _attn_bwd_finish(dh, per_pattern, cos2, sin_bwd, aw):
    S, W = dh.shape
    n_heads = aw // HEAD_DIM
    n_pat = len(DILATIONS)

    def body(*refs):
        grad_refs = refs[1:1 + 3 * n_pat]
        cos_ref, sin_ref = refs[1 + 3 * n_pat], refs[2 + 3 * n_pat]
        out_ref = refs[3 + 3 * n_pat]
        perms = {d: _perm_matrix(d, False) for d in DILATIONS if d > 1}
        totals = []
        for which in range(3):
            tot = None
            for pi, d in enumerate(DILATIONS):
                g = grad_refs[which * n_pat + pi][...].reshape(PERM_ROWS, aw)
                g = _permute_rows(perms[d], g) if d > 1 else g.astype(F32)
                tot = g if tot is None else tot + g
            totals.append(tot)
        dq, dk, dv = totals
        c, s = cos_ref[...], sin_ref[...]
        for hd in range(n_heads):
            sl = slice(hd * HEAD_DIM, (hd + 1) * HEAD_DIM)
            out_ref[:, sl] = _rope_apply(dq[:, sl], c, s).astype(BF16)
            out_ref[:, aw + hd * HEAD_DIM:aw + (hd + 1) * HEAD_DIM] = _rope_apply(dk[:, sl], c, s).astype(BF16)
        out_ref[:, 2 * aw:] = dv.astype(BF16)

    grads = [pp[which] for which in range(3) for pp in per_pattern]
    rope_spec = pl.BlockSpec((PERM_ROWS, HEAD_DIM), lambda i: (i, 0))
    return pl.pallas_call(
        body, name="attn_bwd_finish", grid=(S // PERM_ROWS,),
        in_specs=([pl.BlockSpec(memory_space=pl.ANY)] + [_rm_block(d, aw) for d in DILATIONS] * 3
                  + [rope_spec, rope_spec]),
        out_specs=pl.BlockSpec((PERM_ROWS, 3 * aw), lambda i: (i, 0)),
        out_shape=jax.ShapeDtypeStruct((S, W), BF16),
        input_output_aliases={0: 0},
        compiler_params=pltpu.CompilerParams(dimension_semantics=("parallel",),
                                             vmem_limit_bytes=_vmem_limit(32 << 20)),
    )(dh, *grads, cos2, sin_bwd)


def _my_place():
    x, y, c = lax.axis_index("x"), lax.axis_index("y"), lax.axis_index("c")
    return x, y, c


def _flat(px, py, pc):
    return 4 * px + 2 * py + pc


def _shard_slice(ref, axis, idx, size):
    start = pl.multiple_of(idx * size, size)
    ix = [slice(None)] * len(ref.shape)
    ix[axis] = pl.ds(start, size)
    return ref.at[tuple(ix)]


_HBM_SPEC = pl.BlockSpec(memory_space=pltpu.HBM)
_SEM_SPEC = pl.BlockSpec(memory_space=pltpu.SEMAPHORE)
_ANY_SPEC = pl.BlockSpec(memory_space=pl.ANY)
_N_PEER = N_DEV - 1
SIBLING, SAME_CORE_NEIGHBOURS, OTHER_CORE_NEIGHBOURS, DIAGONAL = (1,), (2, 4), (3, 5), (6, 7)
PEER_ORDER = SIBLING + SAME_CORE_NEIGHBOURS + OTHER_CORE_NEIGHBOURS + DIAGONAL


def _peer_of(x, y, c, r):
    return (x ^ ((r >> 2) & 1), y ^ ((r >> 1) & 1), c ^ (r & 1))


class _Exchange:
    def __init__(self, name, part, slot):
        self.name, self.part, self.slot = name, part, slot

    def _copy(self, w, r, src, land, send_sems, recv_sems, sending):
        x, y, c = _my_place()
        peer = _peer_of(x, y, c, r)
        return pltpu.make_async_remote_copy(
            src_ref=self.part(w, src, _flat(*peer)),
            dst_ref=self.slot(w, land, _flat(x, y, c) if sending else _flat(*peer)),
            send_sem=send_sems.at[w * _N_PEER + r - 1], recv_sem=recv_sems.at[w * _N_PEER + r - 1],
            device_id=peer, device_id_type=MESH)

    def start(self, srcs, lands, after=None):
        n = len(srcs)
        n_after = 0 if after is None else 1

        def body(*refs):
            src, land = refs[:n], refs[n:2 * n]
            outs = refs[2 * n + n_after:]
            send_sems, recv_sems, local_sems, token = outs[0], outs[1], outs[2], outs[3 + 2 * n]
            for w in range(n):
                self._own_copy(w, src[w], land[w], local_sems).start()
                for r in PEER_ORDER:
                    self._copy(w, r, src[w], land[w], send_sems, recv_sems, True).start()
            token[...] = jnp.zeros_like(token)

        sems = pltpu.SemaphoreType.DMA((n * _N_PEER,))
        outs = pl.pallas_call(
            body, name=self.name + "_start",
            out_shape=(sems, sems, pltpu.SemaphoreType.DMA((n,)),
                       *[pltpu.HBM(t.shape, t.dtype) for t in list(srcs) + list(lands)],
                       jax.ShapeDtypeStruct((8, 128), F32)),
            in_specs=[_HBM_SPEC] * (2 * n) + [_ANY_SPEC] * n_after,
            out_specs=(_SEM_SPEC, _SEM_SPEC, _SEM_SPEC, *[_HBM_SPEC] * (2 * n), pl.BlockSpec(memory_space=pltpu.VMEM)),
            input_output_aliases={i: 3 + i for i in range(2 * n)},
            compiler_params=pltpu.CompilerParams(has_side_effects=pltpu.SideEffectType.DATAFLOW_SIDE_EFFECTING),
        )(*[pltpu.with_memory_space_constraint(t, pltpu.HBM) for t in list(srcs) + list(lands)],
          *([after] if n_after else []))
        return outs[0], outs[1], outs[2], outs[3:3 + n], outs[3 + n:3 + 2 * n], outs[3 + 2 * n]

    def _own_copy(self, w, src, land, local_sems):
        me = _flat(*_my_place())
        return pltpu.make_async_copy(self.part(w, src, me), self.slot(w, land, me), local_sems.at[w])

    def wait(self, started, after, peers=PEER_ORDER, own=True, tag=""):
        send_sems, recv_sems, local_sems, srcs, lands, token = started
        n = len(srcs)

        def body(*refs):
            src, land = refs[:n], refs[n:2 * n]
            s_sems, r_sems, l_sems = refs[2 * n], refs[2 * n + 1], refs[2 * n + 2]
            for w in range(n):
                if own:
                    self._own_copy(w, src[w], land[w], l_sems).wait()
                for r in peers:
                    cp = self._copy(w, r, src[w], land[w], s_sems, r_sems, False)
                    cp.wait_send()
                    cp.wait_recv()

        outs = pl.pallas_call(
            body, name=self.name + "_wait" + tag,
            out_shape=[pltpu.HBM(t.shape, t.dtype) for t in list(srcs) + list(lands)],
            in_specs=[_HBM_SPEC] * (2 * n) + [_SEM_SPEC, _SEM_SPEC, _SEM_SPEC, _ANY_SPEC],
            out_specs=[_HBM_SPEC] * (2 * n),
            input_output_aliases={i: i for i in range(2 * n)},
            compiler_params=pltpu.CompilerParams(has_side_effects=pltpu.SideEffectType.DATAFLOW_SIDE_EFFECTING),
        )(*srcs, *lands, send_sems, recv_sems, local_sems, after)
        return outs[n:], (send_sems, recv_sems, local_sems, outs[:n], outs[n:], token)


def _gather_exchange(name, axes, shard_sizes):
    return _Exchange(name, lambda w, src, dev: src,
                     lambda w, land, dev: _shard_slice(land, axes[w], dev, shard_sizes[w]))


def _scatter_exchange(name, axes, shard_sizes):
    def part(w, src, dev):
        return src if axes[w] is None else _shard_slice(src, axes[w], dev, shard_sizes[w])
    return _Exchange(name, part, lambda w, land, dev: land.at[dev])


def _adamw(name, partials, w, m, v):
    R, C = w.shape
    tr = R
    while tr * C * 4 > (1 << 20) and tr % 16 == 0:
        tr //= 2

    def body(p_ref, w_ref, m_ref, v_ref, g_ref, d_ref, nm_ref, nv_ref):
        g = p_ref[0].astype(F32)
        for jdev in range(1, N_DEV):
            g = g + p_ref[jdev].astype(F32)
        nm = ADAM_B1 * m_ref[...] + (1.0 - ADAM_B1) * g
        nv = ADAM_B2 * v_ref[...] + (1.0 - ADAM_B2) * (g * g)
        m_hat = nm / (1.0 - ADAM_B1 ** ADAM_STEP)
        v_hat = nv / (1.0 - ADAM_B2 ** ADAM_STEP)
        g_ref[...] = g
        d_ref[...] = -ADAM_LR * (m_hat / (jnp.sqrt(v_hat) + ADAM_EPS) + ADAM_WD * w_ref[...])
        nm_ref[...] = nm
        nv_ref[...] = nv

    spec = pl.BlockSpec((tr, C), lambda i: (i, 0))
    return pl.pallas_call(
        body, name=name, grid=(R // tr,),
        in_specs=[pl.BlockSpec((N_DEV, tr, C), lambda i: (0, i, 0)), spec, spec, spec],
        out_specs=[spec] * 4,
        out_shape=[jax.ShapeDtypeStruct((R, C), F32)] * 4,
        compiler_params=pltpu.CompilerParams(dimension_semantics=("parallel",),
                                             vmem_limit_bytes=_vmem_limit(24 << 20)),
    )(partials, w, m, v)


def _local_step(x, cos2, sin_fwd, sin_bwd, project_in, mix_weights, ffn_weights, pool_scale, g_mix, b_mix, g_ff, b_ff,
                target, send):
    S, D = x.shape
    aw = pw = D // 2
    u_col_block = 3
    gate_col0 = 4 * aw

    xb = x.astype(BF16)
    h, w_in = project_in(xb)
    dilated = [d for d in DILATIONS if d > 1]
    qkv = {1: h[None], **dict(zip(dilated, _to_residue_major("qkv_to_rm", h, 0, 3 * aw)))}
    fwd = [_attn_fwd(qkv[d], d, aw) for d in DILATIONS]
    o_attn, l_tot = _attn_combine([f[0] for f in fwd], [f[1] for f in fwd], aw)
    w_pool, w_ba, w_bp, w_out = mix_weights(o_attn)
    p, y_pre, pm = _pool_fwd(h, w_pool, pool_scale, pw, u_col_block)
    y_attn = _branch_attn(o_attn, w_ba)
    y_pool, merged = _branch_pool_merge(pm, w_bp, h, y_attn, gate_col0)
    w1, w2 = ffn_weights(merged)
    x1, x1b, xhat1, rstd1 = _out_proj_ln(merged, w_out, x, g_mix, b_mix)
    a = _ffn_up(x1b, w1)
    z2 = _residual_matmul("ffn_down", a, w2, "nn", x1)
    dz2, dz2b, dg_ff, db_ff, loss = _ln_loss_bwd(z2, g_ff, b_ff, target)

    tok = send("ff2", [_grad_weight("grad_w_ff2", a, dz2b)])
    dh1 = _ffn_down_bwd(dz2b, w2, a, after=tok)
    tok = send("ff1", [_grad_weight("grad_w_ff1", x1b, dh1)])
    dy1 = _residual_matmul("ffn_up_bwd", dh1, w1, "nt", dz2, after=tok)
    dz1, dz1b, dg_mix, db_mix = _ln_bwd(dy1, xhat1, rstd1, g_mix)
    tok = send("out", [_grad_weight("grad_w_out", merged, dz1b)])
    dy_attn, dy_pool, dh = _out_proj_bwd(dz1b, w_out, h, y_attn, y_pool, gate_col0, after=tok)
    tok = send("branch", [_grad_weight("grad_w_branch_attn", o_attn, dy_attn),
                          _grad_weight("grad_w_branch_pool", pm, dy_pool)])
    d_out, stats = _branch_attn_bwd(dy_attn, w_ba, o_attn, l_tot, after=tok)
    dy_pre, d_scale = _branch_pool_bwd(dy_pool, w_bp, y_pre, pool_scale)
    dh, dw_pool = _pool_bwd(dh, dy_pre, p, w_pool, pw, u_col_block)
    d_outs = {1: d_out[None], **dict(zip(dilated, _to_residue_major("dout_to_rm", d_out, 0, aw)))}
    statss = {1: stats[None], **dict(zip(dilated, _to_residue_major("stats_to_rm", stats, 0, HEAD_DIM)))}
    per_pattern = [_attn_bwd(qkv[d], d_outs[d], statss[d], d, aw) for d in DILATIONS]
    dh = _attn_bwd_finish(dh, per_pattern, cos2, sin_bwd, aw)
    small = jnp.concatenate((d_scale, dg_mix, db_mix, dg_ff, db_ff), axis=-1)
    tok = send("in", [_grad_weight("grad_w_in", xb, dh), dw_pool.astype(BF16),
                      small.reshape(small.shape[-1] // HEAD_DIM, HEAD_DIM)])
    grad_x = _residual_matmul("in_proj_bwd", dh, w_in, "nt", dz1, after=tok)
    return loss, grad_x


def _rope_tables(positions):
    half = HEAD_DIM // 2
    inv_freq = ROPE_THETA ** (-jnp.arange(half, dtype=F32) / half)
    ang = positions.astype(F32)[:, None] * inv_freq
    cos, sin = jnp.cos(ang), jnp.sin(ang)
    cos2 = jnp.concatenate([cos, cos], axis=-1)
    sin_fwd = jnp.concatenate([-sin, sin], axis=-1)
    return cos2, sin_fwd, -sin_fwd


def kernel(x, positions, w_in, w_pool, pool_scale, w_branch_attn, w_branch_pool, w_out, ln_mix_g, ln_mix_b, w_ff1, w_ff2, ln_ff_g, ln_ff_b, loss_target, m_w_in, m_w_pool, m_pool_scale, m_w_branch_attn, m_w_branch_pool, m_w_out, m_ln_mix_g, m_ln_mix_b, m_w_ff1, m_w_ff2, m_ln_ff_g, m_ln_ff_b, v_w_in, v_w_pool, v_pool_scale, v_w_branch_attn, v_w_branch_pool, v_w_out, v_ln_mix_g, v_ln_mix_b, v_w_ff1, v_w_ff2, v_ln_ff_g, v_ln_ff_b):
    big_w = (w_in[0], w_pool[0], w_branch_attn[0], w_branch_pool[0], w_out[0], w_ff1[0], w_ff2[0])
    big_m = (m_w_in[0], m_w_pool[0], m_w_branch_attn[0], m_w_branch_pool[0], m_w_out[0], m_w_ff1[0], m_w_ff2[0])
    big_v = (v_w_in[0], v_w_pool[0], v_w_branch_attn[0], v_w_branch_pool[0], v_w_out[0], v_w_ff1[0], v_w_ff2[0])
    shard_axes = (1, 1, 1, 1, 0, 1, 0)
    small_w = (pool_scale, ln_mix_g, ln_mix_b, ln_ff_g, ln_ff_b)
    small_m = (m_pool_scale, m_ln_mix_g, m_ln_mix_b, m_ln_ff_g, m_ln_ff_b)
    small_v = (v_pool_scale, v_ln_mix_g, v_ln_mix_b, v_ln_ff_g, v_ln_ff_b)

    names = ("w_in", "w_pool", "w_branch_attn", "w_branch_pool", "w_out", "w_ff1", "w_ff2")
    axis_of = dict(zip(names, shard_axes))
    shard_of = dict(zip(names, [w.astype(BF16) for w in big_w]))

    def full_buffer(n):
        s, ax = shard_of[n], axis_of[n]
        full = list(s.shape)
        full[ax] *= N_DEV
        return lax.empty(tuple(full), s.dtype)

    def gather_group(tag, group, after):
        ex = _gather_exchange(tag, [axis_of[n] for n in group], [shard_of[n].shape[axis_of[n]] for n in group])
        return ex, ex.start([shard_of[n] for n in group], [full_buffer(n) for n in group], after)

    in_ex, in_started = gather_group("gather_in", ("w_in",), None)
    mix_ex, mix_started = gather_group("gather_mix", ("w_pool", "w_branch_attn", "w_branch_pool", "w_out"),
                                       in_started[-1])
    ffn_ex, ffn_started = gather_group("gather_ffn", ("w_ff1", "w_ff2"), mix_started[-1])
    me = 4 * lax.axis_index("x") + 2 * lax.axis_index("y") + lax.axis_index("c")
    block_cols = shard_of["w_in"].shape[1]

    def project_in(xb):
        n_rope_blocks = 2 * (x.shape[-1] // 2) // block_cols
        cols = lambda blocks: jnp.stack(blocks).astype(jnp.int32)
        h = _in_proj_piece("in_proj_own", xb, shard_of["w_in"], cols([me]), cos2, sin_fwd, None, block_cols,
                           n_rope_blocks, own_shard=True, after=ffn_started[-1])
        started, after = in_started, h
        for stage, peers in enumerate((SIBLING, SAME_CORE_NEIGHBOURS, OTHER_CORE_NEIGHBOURS, DIAGONAL)):
            (w_in_land,), started = in_ex.wait(started, after, peers=peers, own=peers is DIAGONAL, tag=str(stage))
            h = _in_proj_piece(f"in_proj_{stage}", xb, w_in_land, cols([me ^ r for r in peers]), cos2, sin_fwd,
                               h, block_cols, n_rope_blocks)
            after = h
        return h, w_in_land

    groups = {"ff2": ("w_ff2",), "ff1": ("w_ff1",), "out": ("w_out",),
              "branch": ("w_branch_attn", "w_branch_pool"), "in": ("w_in", "w_pool", "small")}
    sent = {}

    def send(key, grads_):
        axes = [axis_of.get(n) for n in groups[key]]
        sizes = [None if ax is None else g.shape[ax] // N_DEV for g, ax in zip(grads_, axes)]
        lands = []
        for g, ax, size in zip(grads_, axes, sizes):
            shard = list(g.shape)
            if ax is not None:
                shard[ax] = size
            lands.append(lax.empty((N_DEV, *shard), g.dtype))
        ex = _scatter_exchange("scatter_" + key, axes, sizes)
        sent[key] = (ex, ex.start(list(grads_), lands))
        return sent[key][1][-1]

    cos2, sin_fwd, sin_bwd = _rope_tables(positions[0])
    loss, grad_x = _local_step(
        x[0], cos2, sin_fwd, sin_bwd, project_in, lambda after: mix_ex.wait(mix_started, after)[0],
        lambda after: ffn_ex.wait(ffn_started, after)[0], pool_scale, ln_mix_g, ln_mix_b, ln_ff_g, ln_ff_b,
        loss_target[0], send)

    state = dict(zip(names, zip(big_w, big_m, big_v)))
    n_small = sum(w.shape[-1] for w in small_w)
    small_2d = (n_small // HEAD_DIM, HEAD_DIM)
    state["small"] = tuple(jnp.concatenate(t, axis=-1).reshape(small_2d) for t in (small_w, small_m, small_v))
    grads, deltas, new_ms, new_vs = {}, {}, {}, {}
    after = grad_x
    for key in ("ff2", "ff1", "out", "branch", "in"):
        ex, started = sent[key]
        for n, part in zip(groups[key], ex.wait(started, after)[0]):
            w, m, v = state[n]
            r2 = (-1, w.shape[-1])
            w2d = w.reshape(r2)
            res = _adamw("adamw_" + n, part.reshape((N_DEV,) + w2d.shape), w2d, m.reshape(r2), v.reshape(r2))
            after = res[0]
            if n == "small":
                small_out = [t.reshape(1, n_small) for t in res]
            else:
                grads[n], deltas[n], new_ms[n], new_vs[n] = (t.reshape((1,) + w.shape) for t in res)
    small_names = ("pool_scale", "ln_mix_g", "ln_mix_b", "ln_ff_g", "ln_ff_b")
    off = 0
    for n, w in zip(small_names, small_w):
        width = w.shape[-1]
        grads[n], deltas[n], new_ms[n], new_vs[n] = (t[:, off:off + width] for t in small_out)
        off += width

    order = ("w_in", "w_pool", "pool_scale", "w_branch_attn", "w_branch_pool", "w_out", "ln_mix_g", "ln_mix_b",
             "w_ff1", "w_ff2", "ln_ff_g", "ln_ff_b")
    total_loss = lax.psum(loss[0, 0], ("x", "y", "c"))
    return (total_loss, grad_x[None], *[grads[n] for n in order], *[deltas[n] for n in order],
            *[new_ms[n] for n in order], *[new_vs[n] for n in order])
```

```python
import functools

import jax
import jax.numpy as jnp
from jax import lax
from jax.experimental import pallas as pl
from jax.experimental.pallas import tpu as pltpu

F32 = jnp.float32
BF16 = jnp.bfloat16

N_DEV = 8
HEAD_DIM = 128
SUB_BLOCK = 128
DILATIONS = (1, 4, 16)
POOL_WINDOWS = (2, 4, 8, 16)
MAX_POOL_WINDOW = 16
POOL_HALO = 128
PERM_ROWS = 512
K_TILE = 1024
LN_EPS = 1e-5
DEEPNORM_ALPHA = 2.0 ** 0.25
ROPE_THETA = 10000.0
ATTN_SCALE = HEAD_DIM ** -0.5
ADAM_LR, ADAM_B1, ADAM_B2, ADAM_EPS, ADAM_WD, ADAM_STEP = 0.001, 0.9, 0.999, 1e-08, 0.01, 10
NEG_BIG = -1e30
VMEM_CAP_V7X = 64 * 1024 * 1024
MESH = pl.DeviceIdType.MESH


def _vmem_limit(est_bytes):
    return int(min(max(est_bytes * 5 // 4 + (4 << 20), 16 << 20), VMEM_CAP_V7X - (6 << 20)))


def _nbytes(shape, dtype):
    n = 1
    for s in shape:
        n *= s
    return n * jnp.dtype(dtype).itemsize


def _mm(name, a, b, form, tiles, outs, epi, extras=(), sequential=False, after=None):
    tm, tn, tk = tiles
    if form == "nn":
        (M, K), (K2, N) = a.shape, b.shape
    elif form == "nt":
        (M, K), (N, K2) = a.shape, b.shape
    else:
        (K, M), (K2, N) = a.shape, b.shape
    assert K == K2, (name, a.shape, b.shape)
    tm, tn, tk = min(tm, M), min(tn, N), min(tk, K)
    assert M % tm == 0 and N % tn == 0 and K % tk == 0, (name, M, N, K, tm, tn, tk)
    grid = (M // tm, N // tn, K // tk)
    nk = grid[2]
    if form == "nn":
        a_spec = pl.BlockSpec((tm, tk), lambda i, j, k: (i, k))
        b_spec = pl.BlockSpec((tk, tn), lambda i, j, k: (k, j))
        contract = ((1,), (0,))
    elif form == "nt":
        a_spec = pl.BlockSpec((tm, tk), lambda i, j, k: (i, k))
        b_spec = pl.BlockSpec((tn, tk), lambda i, j, k: (j, k))
        contract = ((1,), (1,))
    else:
        a_spec = pl.BlockSpec((tk, tm), lambda i, j, k: (k, i))
        b_spec = pl.BlockSpec((tk, tn), lambda i, j, k: (k, j))
        contract = ((0,), (0,))
    n_ex, n_out = len(extras), len(outs)
    n_after = 0 if after is None else 1

    def body(a_ref, b_ref, *rest):
        ex_refs = rest[:n_ex]
        rest = rest[n_ex + n_after:]
        out_refs = rest[:n_out]
        i, j, k = pl.program_id(0), pl.program_id(1), pl.program_id(2)

        def prod():
            return lax.dot_general(a_ref[...].astype(BF16), b_ref[...].astype(BF16),
                                   (contract, ((), ())), preferred_element_type=F32)

        if nk == 1:
            epi(prod(), ex_refs, out_refs, i, j)
        else:
            acc = rest[n_out]

            @pl.when(k == 0)
            def _():
                acc[...] = prod()

            @pl.when(k > 0)
            def _():
                acc[...] += prod()

            @pl.when(k == nk - 1)
            def _():
                epi(acc[...], ex_refs, out_refs, i, j)

    est = 2 * (_nbytes(a_spec.block_shape, a.dtype) + _nbytes(b_spec.block_shape, b.dtype))
    est += sum(2 * _nbytes(bs, arr.dtype) for arr, bs, _ in extras)
    est += sum(2 * _nbytes(bs, dt) for _, dt, bs, _ in outs)
    est += 4 * tm * tn * 4
    sem = ("arbitrary",) * 3 if sequential else ("parallel", "parallel", "arbitrary")
    return pl.pallas_call(
        body, name=name, grid=grid,
        in_specs=([a_spec, b_spec] + [pl.BlockSpec(bs, im) for _, bs, im in extras]
                  + [pl.BlockSpec(memory_space=pl.ANY)] * n_after),
        out_specs=[pl.BlockSpec(bs, im) for _, _, bs, im in outs],
        out_shape=[jax.ShapeDtypeStruct(sh, dt) for sh, dt, _, _ in outs],
        scratch_shapes=[pltpu.VMEM((tm, tn), F32)] if nk > 1 else [],
        compiler_params=pltpu.CompilerParams(dimension_semantics=sem, vmem_limit_bytes=_vmem_limit(est)),
    )(a, b, *[arr for arr, _, _ in extras], *([after] if n_after else []))


def _tile_out(shape, dtype, tm, tn):
    return (shape, dtype, (tm, tn), lambda i, j, k: (i, j))


def _row_sum_out(width):
    return ((1, width), F32, (1, width), lambda i, j, k: (0, 0))


def _accumulate_rows(ref, value, i):
    @pl.when(i == 0)
    def _():
        ref[...] = value

    @pl.when(i > 0)
    def _():
        ref[...] += value


def _layer_norm_bwd(dy, xhat, rstd, g):
    dxh = dy * g
    m1 = jnp.mean(dxh, axis=-1, keepdims=True)
    m2 = jnp.mean(dxh * xhat, axis=-1, keepdims=True)
    return rstd * (dxh - m1 - xhat * m2)


def _rope_apply(t, cos2, sin_signed):
    return t * cos2 + pltpu.roll(t, HEAD_DIM // 2, axis=1) * sin_signed


def _in_proj_piece(name, xb, w_in, col_blocks, cos2, sin_fwd, h_so_far, block_cols, n_rope_blocks, own_shard=False,
                   after=None):
    S, D = xb.shape
    W = w_in.shape[1] * (N_DEV if own_shard else 1)
    tm, tk = min(2048, S), min(K_TILE, D)
    nk = D // tk
    n_blocks = col_blocks.shape[0]

    def body(cols_ref, x_ref, w_ref, cos_ref, sin_ref, *rest):
        h_ref, acc = rest[-2], rest[-1]
        j, k = pl.program_id(1), pl.program_id(2)

        def prod():
            return _dot_nn(x_ref[...], w_ref[...])

        @pl.when(k == 0)
        def _():
            acc[...] = prod()

        @pl.when(k > 0)
        def _():
            acc[...] += prod()

        @pl.when(jnp.logical_and(k == nk - 1, cols_ref[j] < n_rope_blocks))
        def _():
            c, s = cos_ref[...], sin_ref[...]
            for hd in range(block_cols // HEAD_DIM):
                sl = slice(hd * HEAD_DIM, (hd + 1) * HEAD_DIM)
                h_ref[:, sl] = _rope_apply(acc[:, sl], c, s).astype(BF16)

        @pl.when(jnp.logical_and(k == nk - 1, cols_ref[j] >= n_rope_blocks))
        def _():
            h_ref[...] = acc[...].astype(BF16)

    row = pl.BlockSpec((tm, HEAD_DIM), lambda i, j, k, cols: (i, 0))
    carried = ([] if h_so_far is None else [h_so_far]) + ([] if after is None else [after])
    est = 2 * (tm * tk * 2 + tk * block_cols * 2 + tm * block_cols * 2 + 2 * tm * HEAD_DIM * 4) + 3 * tm * block_cols * 4
    return pl.pallas_call(
        body, name=name,
        grid_spec=pltpu.PrefetchScalarGridSpec(
            num_scalar_prefetch=1, grid=(S // tm, n_blocks, nk),
            in_specs=[pl.BlockSpec((tm, tk), lambda i, j, k, cols: (i, k)),
                      pl.BlockSpec((tk, block_cols), lambda i, j, k, cols: (k, 0 if own_shard else cols[j])), row, row]
                     + [pl.BlockSpec(memory_space=pl.ANY)] * len(carried),
            out_specs=pl.BlockSpec((tm, block_cols), lambda i, j, k, cols: (i, cols[j])),
            scratch_shapes=[pltpu.VMEM((tm, block_cols), F32)]),
        out_shape=jax.ShapeDtypeStruct((S, W), BF16),
        input_output_aliases={} if h_so_far is None else {5: 0},
        compiler_params=pltpu.CompilerParams(dimension_semantics=("parallel", "arbitrary", "arbitrary"),
                                             vmem_limit_bytes=_vmem_limit(est)),
    )(col_blocks, xb, w_in, cos2, sin_fwd, *carried)


def _attn_mask(mb):
    qi = lax.broadcasted_iota(jnp.int32, (SUB_BLOCK, 2 * SUB_BLOCK), 0)
    kj = lax.broadcasted_iota(jnp.int32, (SUB_BLOCK, 2 * SUB_BLOCK), 1)
    prev = jnp.logical_and(jnp.logical_and(kj < SUB_BLOCK, kj >= qi), mb > 0)
    cur = jnp.logical_and(kj >= SUB_BLOCK, kj - SUB_BLOCK <= qi)
    return jnp.logical_or(prev, cur)


def _both_blocks(prev_ref, cur_ref, sl):
    return jnp.concatenate([prev_ref[:, sl], cur_ref[:, sl]], axis=0)


def _dot_nt(a, b):
    return lax.dot_general(a, b, (((1,), (1,)), ((), ())), preferred_element_type=F32)


def _dot_tn(a, b):
    return lax.dot_general(a, b, (((0,), (0,)), ((), ())), preferred_element_type=F32)


def _dot_nn(a, b):
    return lax.dot_general(a, b, (((1,), (0,)), ((), ())), preferred_element_type=F32)


def _perm_matrix(d, to_residue_major):
    g = PERM_ROWS // d
    i = lax.broadcasted_iota(jnp.int32, (PERM_ROWS, PERM_ROWS), 0)
    j = lax.broadcasted_iota(jnp.int32, (PERM_ROWS, PERM_ROWS), 1)
    if to_residue_major:
        hit = j == (i % g) * d + i // g
    else:
        hit = j == (i % d) * g + i // d
    return hit.astype(BF16)


def _permute_rows(perm, x, terms=3):
    if x.dtype == BF16:
        return _dot_nn(perm, x)
    hi = x.astype(BF16)
    r1 = x - hi.astype(F32)
    mid = r1.astype(BF16)
    out = _dot_nn(perm, hi) + _dot_nn(perm, mid)
    if terms == 3:
        out = out + _dot_nn(perm, (r1 - mid.astype(F32)).astype(BF16))
    return out


def _rm_block(d, width):
    return pl.BlockSpec((d, PERM_ROWS // d, width), lambda i: (0, i, 0))


def _to_residue_major(name, x, col_block, width):
    S = x.shape[0]
    dils = [d for d in DILATIONS if d > 1]
    chunk = min(width, 1024)

    def body(x_ref, *out_refs):
        for d, o_ref in zip(dils, out_refs):
            perm = _perm_matrix(d, True)
            for c0 in range(0, width, chunk):
                cw = min(chunk, width - c0)
                y = _permute_rows(perm, x_ref[:, c0:c0 + cw])
                o_ref[:, :, c0:c0 + cw] = y.astype(x.dtype).reshape(d, PERM_ROWS // d, cw)

    return pl.pallas_call(
        body, name=name, grid=(S // PERM_ROWS,),
        in_specs=[pl.BlockSpec((PERM_ROWS, width), lambda i: (i, col_block))],
        out_specs=[_rm_block(d, width) for d in dils],
        out_shape=[jax.ShapeDtypeStruct((d, S // d, width), x.dtype) for d in dils],
        compiler_params=pltpu.CompilerParams(dimension_semantics=("parallel",),
                                             vmem_limit_bytes=_vmem_limit(32 << 20)),
    )(x)


def _qkv_specs(aw):
    def spec(col, prev):
        if prev:
            return pl.BlockSpec((None, SUB_BLOCK, aw), lambda r, mb: (r, jnp.maximum(mb - 1, 0), col))
        return pl.BlockSpec((None, SUB_BLOCK, aw), lambda r, mb: (r, mb, col))
    return [spec(0, False), spec(1, True), spec(1, False), spec(2, True), spec(2, False)]


def _put_column(tile, col, value):
    lane = lax.broadcasted_iota(jnp.int32, tile.shape, 1)
    return jnp.where(lane == col, value, tile)


def _attn_fwd(qkv, d, aw):
    _, rows, _ = qkv.shape
    n_heads = aw // HEAD_DIM
    nb = rows // SUB_BLOCK

    def body(q_ref, kp_ref, kc_ref, vp_ref, vc_ref, o_ref, lse_ref, s_buf, p_buf):
        mask = _attn_mask(pl.program_id(1))
        for hd in range(n_heads):
            sl = slice(hd * HEAD_DIM, (hd + 1) * HEAD_DIM)
            s_buf[hd] = _dot_nt(q_ref[:, sl], _both_blocks(kp_ref, kc_ref, sl))
        lse_tile = jnp.zeros((SUB_BLOCK, HEAD_DIM), F32)
        inv_tile = jnp.zeros((SUB_BLOCK, HEAD_DIM), F32)
        for hd in range(n_heads):
            s = jnp.where(mask, s_buf[hd] * ATTN_SCALE, NEG_BIG)
            m = jnp.max(s, axis=-1, keepdims=True)
            p = jnp.exp(s - m)
            l = jnp.sum(p, axis=-1, keepdims=True)
            p_buf[hd] = p.astype(BF16)
            lse_tile = _put_column(lse_tile, hd, m + jnp.log(l))
            inv_tile = _put_column(inv_tile, hd, 1.0 / l)
        lse_ref[...] = lse_tile
        for hd in range(n_heads):
            sl = slice(hd * HEAD_DIM, (hd + 1) * HEAD_DIM)
            o = _dot_nn(p_buf[hd], _both_blocks(vp_ref, vc_ref, sl))
            o_ref[:, sl] = o * inv_tile[:, hd:hd + 1]

    return pl.pallas_call(
        body, name=f"attn_fwd_d{d}", grid=(d, nb),
        in_specs=_qkv_specs(aw),
        out_specs=[pl.BlockSpec((None, SUB_BLOCK, aw), lambda r, mb: (r, mb, 0)),
                   pl.BlockSpec((None, SUB_BLOCK, HEAD_DIM), lambda r, mb: (r, mb, 0))],
        out_shape=[jax.ShapeDtypeStruct((d, rows, aw), F32), jax.ShapeDtypeStruct((d, rows, HEAD_DIM), F32)],
        scratch_shapes=[pltpu.VMEM((n_heads, SUB_BLOCK, 2 * SUB_BLOCK), F32),
                        pltpu.VMEM((n_heads, SUB_BLOCK, 2 * SUB_BLOCK), BF16)],
        compiler_params=pltpu.CompilerParams(dimension_semantics=("parallel", "parallel"),
                                             vmem_limit_bytes=_vmem_limit(16 << 20)),
    )(qkv, qkv, qkv, qkv, qkv)


def _attn_combine(outs, lses, aw):
    S = outs[0].shape[1]
    n_heads = aw // HEAD_DIM
    n_pat = len(DILATIONS)

    def body(*refs):
        o_refs, l_refs = refs[:n_pat], refs[n_pat:2 * n_pat]
        o_ref, lt_ref = refs[2 * n_pat], refs[2 * n_pat + 1]
        o_nat, l_nat = [], []
        for d, o_r, l_r in zip(DILATIONS, o_refs, l_refs):
            o_p = o_r[...].reshape(PERM_ROWS, aw)
            l_p = l_r[...].reshape(PERM_ROWS, HEAD_DIM)
            if d > 1:
                perm = _perm_matrix(d, False)
                o_p, l_p = _permute_rows(perm, o_p, terms=2), _permute_rows(perm, l_p)
            o_nat.append(o_p)
            l_nat.append(l_p)
        mx = functools.reduce(jnp.maximum, l_nat)
        es = [jnp.exp(l_p - mx) for l_p in l_nat]
        den = functools.reduce(jnp.add, es)
        lt_ref[...] = mx + jnp.log(den)
        ws = [e / den for e in es]
        for hd in range(n_heads):
            sl = slice(hd * HEAD_DIM, (hd + 1) * HEAD_DIM)
            o = ws[0][:, hd:hd + 1] * o_nat[0][:, sl]
            for pi in range(1, n_pat):
                o = o + ws[pi][:, hd:hd + 1] * o_nat[pi][:, sl]
            o_ref[:, sl] = o.astype(BF16)

    return pl.pallas_call(
        body, name="attn_combine", grid=(S // PERM_ROWS,),
        in_specs=[_rm_block(d, aw) for d in DILATIONS] + [_rm_block(d, HEAD_DIM) for d in DILATIONS],
        out_specs=[pl.BlockSpec((PERM_ROWS, aw), lambda i: (i, 0)), pl.BlockSpec((PERM_ROWS, HEAD_DIM), lambda i: (i, 0))],
        out_shape=[jax.ShapeDtypeStruct((S, aw), BF16), jax.ShapeDtypeStruct((S, HEAD_DIM), F32)],
        compiler_params=pltpu.CompilerParams(dimension_semantics=("parallel",),
                                             vmem_limit_bytes=_vmem_limit(40 << 20)),
    )(*outs, *lses)


def _band(tm, width, w, row_offset, transpose):
    t = lax.broadcasted_iota(jnp.int32, (tm, width), 0)
    u = lax.broadcasted_iota(jnp.int32, (tm, width), 1)
    dist = (u - t - row_offset) if transpose else (t + row_offset - u)
    return jnp.logical_and(dist >= 0, dist < w).astype(BF16)


def _pool_fwd(h, w_pool, pool_scale, pw, u_col_block):
    S, W = h.shape
    n_groups = len(POOL_WINDOWS)
    gw = pw // n_groups
    tm = min(512, S)
    halo_per_tile = tm // POOL_HALO

    def body(uc_ref, uh_ref, w_ref, sc_ref, p_ref, y_ref, pm_ref):
        i = pl.program_id(0)
        t_abs = i * tm + lax.broadcasted_iota(jnp.int32, (tm, 1), 0)
        for g, w in enumerate(POOL_WINDOWS):
            sl = slice(g * gw, (g + 1) * gw)
            uc = uc_ref[:, sl]
            uh = jnp.where(i > 0, uh_ref[:, sl], jnp.zeros((POOL_HALO, gw), BF16))
            ssum = _dot_nn(_band(tm, tm, w, 0, False), uc) + _dot_nn(_band(tm, POOL_HALO, w, POOL_HALO, False), uh)
            cnt = jnp.minimum(t_abs + 1, w).astype(F32)
            p = (ssum / cnt - uc.astype(F32)).astype(BF16)
            y = _dot_nn(p, w_ref[g])
            p_ref[:, sl] = p
            y_ref[:, sl] = y.astype(BF16)
            pm_ref[:, sl] = (y * sc_ref[:, sl]).astype(BF16)

    row = pl.BlockSpec((tm, pw), lambda i: (i, 0))
    return pl.pallas_call(
        body, name="pool_fwd", grid=(S // tm,),
        in_specs=[pl.BlockSpec((tm, pw), lambda i: (i, u_col_block)),
                  pl.BlockSpec((POOL_HALO, pw), lambda i: (jnp.maximum(i * halo_per_tile - 1, 0), u_col_block)),
                  pl.BlockSpec((n_groups, gw, gw), lambda i: (0, 0, 0)),
                  pl.BlockSpec((1, pw), lambda i: (0, 0))],
        out_specs=[row, row, row],
        out_shape=[jax.ShapeDtypeStruct((S, pw), BF16)] * 3,
        compiler_params=pltpu.CompilerParams(dimension_semantics=("parallel",),
                                             vmem_limit_bytes=_vmem_limit(24 << 20)),
    )(h, h, w_pool, pool_scale)


def _branch_attn(o_attn, w_ba):
    S, _ = o_attn.shape
    D = w_ba.shape[1]
    tm, tn = min(1024, S), D

    def epi(acc, ex, out, i, j):
        out[0][...] = acc.astype(BF16)

    (y,) = _mm("branch_attn", o_attn, w_ba, "nn", (tm, tn, 1024), [_tile_out((S, D), BF16, tm, tn)], epi)
    return y


def _branch_pool_merge(pm, w_bp, h, y_attn, gate_col0):
    S, _ = pm.shape
    D = w_bp.shape[1]
    tm, tn = min(512, S), D
    ga0, gp0 = gate_col0 // tn, (gate_col0 + D) // tn

    def epi(acc, ex, out, i, j):
        ga_ref, gp_ref, ya_ref = ex
        yp_ref, mg_ref = out
        yp = acc.astype(BF16)
        yp_ref[...] = yp
        mg = (jax.nn.sigmoid(ga_ref[...].astype(F32)) * ya_ref[...].astype(F32)
              + jax.nn.sigmoid(gp_ref[...].astype(F32)) * acc)
        mg_ref[...] = mg.astype(BF16)

    y_pool, merged = _mm(
        "branch_pool_merge", pm, w_bp, "nn", (tm, tn, 1024),
        [_tile_out((S, D), BF16, tm, tn), _tile_out((S, D), BF16, tm, tn)], epi,
        extras=[(h, (tm, tn), lambda i, j, k: (i, ga0 + j)), (h, (tm, tn), lambda i, j, k: (i, gp0 + j)),
                (y_attn, (tm, tn), lambda i, j, k: (i, j))])
    return y_pool, merged


def _layer_norm_rows(z, g, b):
    mu = jnp.mean(z, axis=-1, keepdims=True)
    zc = z - mu
    var = jnp.mean(zc * zc, axis=-1, keepdims=True)
    rstd = lax.rsqrt(var + LN_EPS)
    xhat = zc * rstd
    return xhat * g + b, xhat, rstd


def _out_proj_ln(merged, w_out, x, g, b):
    S, D = x.shape
    tm = min(256, S)

    def epi(acc, ex, out, i, j):
        x_ref, g_ref, b_ref = ex
        x1_ref, x1b_ref, xh_ref, rs_ref = out
        y, xhat, rstd = _layer_norm_rows(DEEPNORM_ALPHA * x_ref[...] + acc, g_ref[...], b_ref[...])
        x1_ref[...] = y
        x1b_ref[...] = y.astype(BF16)
        xh_ref[...] = xhat
        rs_ref[...] = jnp.broadcast_to(rstd, (tm, HEAD_DIM))

    row = lambda i, j, k: (i, 0)
    vec = lambda i, j, k: (0, 0)
    return _mm("out_proj_ln", merged, w_out, "nn", (tm, D, D),
               [((S, D), F32, (tm, D), row), ((S, D), BF16, (tm, D), row), ((S, D), F32, (tm, D), row),
                ((S, HEAD_DIM), F32, (tm, HEAD_DIM), row)], epi,
               extras=[(x, (tm, D), row), (g, (1, D), vec), (b, (1, D), vec)])


def _ffn_up(x1b, w1):
    S, D = x1b.shape
    F = w1.shape[1]
    tm, tn = min(1024, S), min(2048, F)

    def epi(acc, ex, out, i, j):
        r = jnp.maximum(acc, 0.0)
        out[0][...] = (r * r).astype(BF16)

    (a,) = _mm("ffn_up", x1b, w1, "nn", (tm, tn, K_TILE), [_tile_out((S, F), BF16, tm, tn)], epi)
    return a


def _residual_matmul(name, a, w, form, resid, after=None):
    S, D = resid.shape
    tm, tn = min(1024, S), min(2048, D)

    def epi(acc, ex, out, i, j):
        out[0][...] = DEEPNORM_ALPHA * ex[0][...] + acc

    (z,) = _mm(name, a, w, form, (tm, tn, K_TILE), [_tile_out((S, D), F32, tm, tn)], epi,
               extras=[(resid, (tm, tn), lambda i, j, k: (i, j))], after=after)
    return z


def _row_kernel(name, body, row_inputs, vec_inputs, row_outputs, sum_widths, tr):
    S = row_inputs[0].shape[0]
    row = lambda w: pl.BlockSpec((tr, w), lambda i: (i, 0))
    vec = lambda w: pl.BlockSpec((1, w), lambda i: (0, 0))

    def wrapped(*refs):
        body(pl.program_id(0), *refs)

    return pl.pallas_call(
        wrapped, name=name, grid=(S // tr,),
        in_specs=[row(t.shape[1]) for t in row_inputs] + [vec(t.shape[1]) for t in vec_inputs],
        out_specs=[row(w) for w, _ in row_outputs] + [vec(w) for w in sum_widths],
        out_shape=([jax.ShapeDtypeStruct((S, w), dt) for w, dt in row_outputs]
                   + [jax.ShapeDtypeStruct((1, w), F32) for w in sum_widths]),
        compiler_params=pltpu.CompilerParams(dimension_semantics=("arbitrary",),
                                             vmem_limit_bytes=_vmem_limit(40 << 20)),
    )(*row_inputs, *vec_inputs)


def _ln_loss_bwd(z2, g, b, target):
    S, D = z2.shape

    def body(i, z_ref, t_ref, g_ref, b_ref, dz_ref, dzb_ref, dg_ref, db_ref, loss_ref):
        gv = g_ref[...]
        y, xhat, rstd = _layer_norm_rows(z_ref[...], gv, b_ref[...])
        err = y - t_ref[...]
        loss = 0.5 * jnp.sum(jnp.mean(err * err, axis=-1, keepdims=True), axis=0, keepdims=True)
        dy = err * (1.0 / D)
        dz = _layer_norm_bwd(dy, xhat, rstd, gv)
        dz_ref[...] = dz
        dzb_ref[...] = dz.astype(BF16)
        _accumulate_rows(dg_ref, jnp.sum(dy * xhat, axis=0, keepdims=True), i)
        _accumulate_rows(db_ref, jnp.sum(dy, axis=0, keepdims=True), i)
        _accumulate_rows(loss_ref, jnp.broadcast_to(loss, (1, HEAD_DIM)), i)

    return _row_kernel("ln_loss_bwd", body, [z2, target], [g, b], [(D, F32), (D, BF16)], [D, D, HEAD_DIM],
                       min(256, S))


def _ln_bwd(dy, xhat, rstd, g):
    S, D = dy.shape

    def body(i, dy_ref, xh_ref, rs_ref, g_ref, dz_ref, dzb_ref, dg_ref, db_ref):
        dyv, xhat_v = dy_ref[...], xh_ref[...]
        dz = _layer_norm_bwd(dyv, xhat_v, rs_ref[:, :1], g_ref[...])
        dz_ref[...] = dz
        dzb_ref[...] = dz.astype(BF16)
        _accumulate_rows(dg_ref, jnp.sum(dyv * xhat_v, axis=0, keepdims=True), i)
        _accumulate_rows(db_ref, jnp.sum(dyv, axis=0, keepdims=True), i)

    return _row_kernel("ln_bwd", body, [dy, xhat, rstd], [g], [(D, F32), (D, BF16)], [D, D], min(256, S))


def _grad_weight(name, act, cot):
    M, N = act.shape[1], cot.shape[1]
    tm, tn = min(1024, M), min(2048, N)

    def epi(acc, ex, out, i, j):
        out[0][...] = acc.astype(BF16)

    (g,) = _mm(name, act, cot, "tn", (tm, tn, K_TILE), [_tile_out((M, N), BF16, tm, tn)], epi)
    return g


def _ffn_down_bwd(dz2b, w2, a, after=None):
    S, D = dz2b.shape
    F = w2.shape[0]
    tm, tn = min(1024, S), min(2048, F)

    def epi(acc, ex, out, i, j):
        out[0][...] = (acc * (2.0 * jnp.sqrt(ex[0][...])).astype(F32)).astype(BF16)

    (dh1,) = _mm("ffn_down_bwd", dz2b, w2, "nt", (tm, tn, K_TILE), [_tile_out((S, F), BF16, tm, tn)], epi,
                 extras=[(a, (tm, tn), lambda i, j, k: (i, j))], after=after)
    return dh1


def _out_proj_bwd(dz1b, w_out, h, y_attn, y_pool, gate_col0, after=None):
    S, D = dz1b.shape
    W = h.shape[1]
    tm = min(256, S)
    assert gate_col0 == 2 * D and W == 4 * D

    def epi(acc, ex, out, i, j):
        gates_ref, ya_ref, yp_ref = ex
        dya_ref, dyp_ref, dh_ref = out
        sa = jax.nn.sigmoid(gates_ref[:, :D].astype(F32))
        sp = jax.nn.sigmoid(gates_ref[:, D:].astype(F32))
        dya_ref[...] = (acc * sa).astype(BF16)
        dyp_ref[...] = (acc * sp).astype(BF16)
        dh_ref[:, :D] = (acc * ya_ref[...].astype(F32) * (sa * (1.0 - sa))).astype(BF16)
        dh_ref[:, D:] = (acc * yp_ref[...].astype(F32) * (sp * (1.0 - sp))).astype(BF16)

    row = lambda i, j, k: (i, 0)
    return _mm("out_proj_bwd", dz1b, w_out, "nt", (tm, D, D),
               [((S, D), BF16, (tm, D), row), ((S, D), BF16, (tm, D), row),
                ((S, W), BF16, (tm, 2 * D), lambda i, j, k: (i, 1))], epi,
               extras=[(h, (tm, 2 * D), lambda i, j, k: (i, 1)), (y_attn, (tm, D), row), (y_pool, (tm, D), row)],
               after=after)


def _branch_attn_bwd(dy_attn, w_ba, o_attn, l_tot, after=None):
    S, D = dy_attn.shape
    aw = w_ba.shape[0]
    n_heads = aw // HEAD_DIM
    tm = min(512, S)

    def epi(acc, ex, out, i, j):
        do_ref, st_ref = out
        do_ref[...] = acc.astype(BF16)
        o = ex[0][...].astype(F32)
        stats = ex[1][...]
        for hd in range(n_heads):
            sl = slice(hd * HEAD_DIM, (hd + 1) * HEAD_DIM)
            stats = _put_column(stats, n_heads + hd, jnp.sum(acc[:, sl] * o[:, sl], axis=-1, keepdims=True))
        st_ref[...] = stats

    row = lambda i, j, k: (i, 0)
    return _mm("branch_attn_bwd", dy_attn, w_ba, "nt", (tm, aw, D),
               [((S, aw), BF16, (tm, aw), row), ((S, HEAD_DIM), F32, (tm, HEAD_DIM), row)], epi,
               extras=[(o_attn, (tm, aw), row), (l_tot, (tm, HEAD_DIM), row)], after=after)


def _branch_pool_bwd(dy_pool, w_bp, y_pre, pool_scale):
    S, D = dy_pool.shape
    pw = w_bp.shape[0]
    tm = min(512, S)

    def epi(acc, ex, out, i, j):
        y_ref, sc_ref = ex
        dyp_ref, dsc_ref = out
        dyp_ref[...] = (acc * sc_ref[...]).astype(BF16)
        _accumulate_rows(dsc_ref, jnp.sum(acc * y_ref[...].astype(F32), axis=0, keepdims=True), i)

    row = lambda i, j, k: (i, 0)
    return _mm("branch_pool_bwd", dy_pool, w_bp, "nt", (tm, pw, D),
               [((S, pw), BF16, (tm, pw), row), _row_sum_out(pw)], epi,
               extras=[(y_pre, (tm, pw), row), (pool_scale, (1, pw), lambda i, j, k: (0, 0))],
               sequential=True)


def _pool_bwd(dh, dy_pre, p, w_pool, pw, u_col_block):
    S, W = dh.shape
    n_groups = len(POOL_WINDOWS)
    gw = pw // n_groups
    tm = min(512, S)
    n_tiles = S // tm
    halo_per_tile = tm // POOL_HALO
    n_halo_blocks = S // POOL_HALO

    def body(dh_in_ref, dyc_ref, dyh_ref, p_ref, w_ref, dh_ref, dw_ref):
        del dh_in_ref
        i = pl.program_id(0)
        t_cur = i * tm + lax.broadcasted_iota(jnp.int32, (tm, 1), 0)
        t_halo = (i + 1) * tm + lax.broadcasted_iota(jnp.int32, (POOL_HALO, 1), 0)
        for g, w in enumerate(POOL_WINDOWS):
            sl = slice(g * gw, (g + 1) * gw)
            wg = w_ref[g]
            dyc = dyc_ref[:, sl]
            dyh = jnp.where(i < n_tiles - 1, dyh_ref[:, sl], jnp.zeros((POOL_HALO, gw), BF16))
            dp_cur = _dot_nt(dyc, wg)
            dp_halo = _dot_nt(dyh, wg)
            dpc_cur = (dp_cur / jnp.minimum(t_cur + 1, w).astype(F32)).astype(BF16)
            dpc_halo = (dp_halo / jnp.minimum(t_halo + 1, w).astype(F32)).astype(BF16)
            du = (_dot_nn(_band(tm, tm, w, 0, True), dpc_cur)
                  + _dot_nn(_band(tm, POOL_HALO, w, -tm, True), dpc_halo) - dp_cur)
            dh_ref[:, sl] = du.astype(BF16)
            dw = _dot_tn(p_ref[:, sl], dyc)

            @pl.when(i == 0)
            def _():
                dw_ref[g] = dw

            @pl.when(i > 0)
            def _():
                dw_ref[g] += dw

    row = pl.BlockSpec((tm, pw), lambda i: (i, 0))
    dh_new, dw_pool = pl.pallas_call(
        body, name="pool_bwd", grid=(n_tiles,),
        in_specs=[pl.BlockSpec(memory_space=pl.ANY), row,
                  pl.BlockSpec((POOL_HALO, pw), lambda i: (jnp.minimum((i + 1) * halo_per_tile, n_halo_blocks - 1), 0)),
                  row, pl.BlockSpec((n_groups, gw, gw), lambda i: (0, 0, 0))],
        out_specs=[pl.BlockSpec((tm, pw), lambda i: (i, u_col_block)),
                   pl.BlockSpec((n_groups, gw, gw), lambda i: (0, 0, 0))],
        out_shape=[jax.ShapeDtypeStruct((S, W), BF16), jax.ShapeDtypeStruct((n_groups, gw, gw), F32)],
        input_output_aliases={0: 0},
        compiler_params=pltpu.CompilerParams(dimension_semantics=("arbitrary",),
                                             vmem_limit_bytes=_vmem_limit(24 << 20)),
    )(dh, dy_pre, dy_pre, p, w_pool)
    return dh_new, dw_pool


def _attn_bwd(qkv, d_out, stats, d, aw):
    _, rows, _ = qkv.shape
    n_heads = aw // HEAD_DIM
    nb = rows // SUB_BLOCK
    n_blocks = d * nb

    def body(q_ref, kp_ref, kc_ref, vp_ref, vc_ref, do_ref, st_ref, dq_ref, dk_ref, dv_ref,
             carry_k, carry_v, s_buf, dp_buf, p_buf, ds_buf):
        step = pl.program_id(0)

        @pl.when(step == 0)
        def _():
            carry_k[...] = jnp.zeros_like(carry_k)
            carry_v[...] = jnp.zeros_like(carry_v)

        @pl.when(step < n_blocks)
        def _():
            mask = _attn_mask(step % nb)
            st = st_ref[...]
            for hd in range(n_heads):
                sl = slice(hd * HEAD_DIM, (hd + 1) * HEAD_DIM)
                s_buf[hd] = _dot_nt(q_ref[:, sl], _both_blocks(kp_ref, kc_ref, sl))
                dp_buf[hd] = _dot_nt(do_ref[:, sl], _both_blocks(vp_ref, vc_ref, sl))
            for hd in range(n_heads):
                lt, dl = st[:, hd:hd + 1], st[:, n_heads + hd:n_heads + hd + 1]
                p = jnp.where(mask, jnp.exp(jnp.where(mask, s_buf[hd] * ATTN_SCALE - lt, NEG_BIG)), 0.0)
                p_buf[hd] = p.astype(BF16)
                ds_buf[hd] = (p * (dp_buf[hd] - dl) * ATTN_SCALE).astype(BF16)
            for hd in range(n_heads):
                sl = slice(hd * HEAD_DIM, (hd + 1) * HEAD_DIM)
                dq_ref[:, sl] = _dot_nn(ds_buf[hd], _both_blocks(kp_ref, kc_ref, sl)).astype(BF16)
                dk_both = _dot_tn(ds_buf[hd], q_ref[:, sl])
                dv_both = _dot_tn(p_buf[hd], do_ref[:, sl])
                dk_ref[:, sl] = (carry_k[:, sl] + dk_both[:SUB_BLOCK]).astype(BF16)
                dv_ref[:, sl] = (carry_v[:, sl] + dv_both[:SUB_BLOCK]).astype(BF16)
                carry_k[:, sl] = dk_both[SUB_BLOCK:]
                carry_v[:, sl] = dv_both[SUB_BLOCK:]

        @pl.when(step == n_blocks)
        def _():
            dk_ref[...] = carry_k[...].astype(BF16)
            dv_ref[...] = carry_v[...].astype(BF16)

    def cur(step):
        return jnp.minimum(step, n_blocks - 1)

    def qkv_spec(col, prev):
        if prev:
            return pl.BlockSpec((SUB_BLOCK, aw), lambda s: (jnp.maximum(cur(s) - 1, 0), col))
        return pl.BlockSpec((SUB_BLOCK, aw), lambda s: (cur(s), col))

    def at_cur(w):
        return pl.BlockSpec((SUB_BLOCK, w), lambda s: (cur(s), 0))

    finished = pl.BlockSpec((SUB_BLOCK, aw), lambda s: (jnp.maximum(s - 1, 0), 0))
    pair = (n_heads, SUB_BLOCK, 2 * SUB_BLOCK)
    flat = lambda t: t.reshape(d * rows, t.shape[-1])
    qkv2 = flat(qkv)
    outs = pl.pallas_call(
        body, name=f"attn_bwd_d{d}", grid=(n_blocks + 1,),
        in_specs=[qkv_spec(0, False), qkv_spec(1, True), qkv_spec(1, False), qkv_spec(2, True), qkv_spec(2, False),
                  at_cur(aw), at_cur(HEAD_DIM)],
        out_specs=[at_cur(aw), finished, finished],
        out_shape=[jax.ShapeDtypeStruct((d * rows, aw), BF16)] * 3,
        scratch_shapes=[pltpu.VMEM((SUB_BLOCK, aw), F32), pltpu.VMEM((SUB_BLOCK, aw), F32),
                        pltpu.VMEM(pair, F32), pltpu.VMEM(pair, F32), pltpu.VMEM(pair, BF16), pltpu.VMEM(pair, BF16)],
        compiler_params=pltpu.CompilerParams(dimension_semantics=("arbitrary",),
                                             vmem_limit_bytes=_vmem_limit(24 << 20)),
    )(qkv2, qkv2, qkv2, qkv2, qkv2, flat(d_out), flat(stats))
    return [t.reshape(d, rows, aw) for t in outs]


def _attn_bwd_finish(dh, per_pattern, cos2, sin_bwd, aw):
    S, W = dh.shape
    n_heads = aw // HEAD_DIM
    n_pat = len(DILATIONS)

    def body(*refs):
        grad_refs = refs[1:1 + 3 * n_pat]
        cos_ref, sin_ref = refs[1 + 3 * n_pat], refs[2 + 3 * n_pat]
        out_ref = refs[3 + 3 * n_pat]
        perms = {d: _perm_matrix(d, False) for d in DILATIONS if d > 1}
        totals = []
        for which in range(3):
            tot = None
            for pi, d in enumerate(DILATIONS):
                g = grad_refs[which * n_pat + pi][...].reshape(PERM_ROWS, aw)
                g = _permute_rows(perms[d], g) if d > 1 else g.astype(F32)
                tot = g if tot is None else tot + g
            totals.append(tot)
        dq, dk, dv = totals
        c, s = cos_ref[...], sin_ref[...]
        for hd in range(n_heads):
            sl = slice(hd * HEAD_DIM, (hd + 1) * HEAD_DIM)
            out_ref[:, sl] = _rope_apply(dq[:, sl], c, s).astype(BF16)
            out_ref[:, aw + hd * HEAD_DIM:aw + (hd + 1) * HEAD_DIM] = _rope_apply(dk[:, sl], c, s).astype(BF16)
        out_ref[:, 2 * aw:] = dv.astype(BF16)

    grads = [pp[which] for which in range(3) for pp in per_pattern]
    rope_spec = pl.BlockSpec((PERM_ROWS, HEAD_DIM), lambda i: (i, 0))
    return pl.pallas_call(
        body, name="attn_bwd_finish", grid=(S // PERM_ROWS,),
        in_specs=([pl.BlockSpec(memory_space=pl.ANY)] + [_rm_block(d, aw) for d in DILATIONS] * 3
                  + [rope_spec, rope_spec]),
        out_specs=pl.BlockSpec((PERM_ROWS, 3 * aw), lambda i: (i, 0)),
        out_shape=jax.ShapeDtypeStruct((S, W), BF16),
        input_output_aliases={0: 0},
        compiler_params=pltpu.CompilerParams(dimension_semantics=("parallel",),
                                             vmem_limit_bytes=_vmem_limit(32 << 20)),
    )(dh, *grads, cos2, sin_bwd)


def _my_place():
    x, y, c = lax.axis_index("x"), lax.axis_index("y"), lax.axis_index("c")
    return x, y, c


def _flat(px, py, pc):
    return 4 * px + 2 * py + pc


def _shard_slice(ref, axis, idx, size):
    start = pl.multiple_of(idx * size, size)
    ix = [slice(None)] * len(ref.shape)
    ix[axis] = pl.ds(start, size)
    return ref.at[tuple(ix)]


_HBM_SPEC = pl.BlockSpec(memory_space=pltpu.HBM)
_SEM_SPEC = pl.BlockSpec(memory_space=pltpu.SEMAPHORE)
_ANY_SPEC = pl.BlockSpec(memory_space=pl.ANY)
_N_PEER = N_DEV - 1
SIBLING, SAME_CORE_NEIGHBOURS, OTHER_CORE_NEIGHBOURS, DIAGONAL = (1,), (2, 4), (3, 5), (6, 7)
PEER_ORDER = SIBLING + SAME_CORE_NEIGHBOURS + OTHER_CORE_NEIGHBOURS + DIAGONAL


def _peer_of(x, y, c, r):
    return (x ^ ((r >> 2) & 1), y ^ ((r >> 1) & 1), c ^ (r & 1))


class _Exchange:
    def __init__(self, name, part, slot):
        self.name, self.part, self.slot = name, part, slot

    def _copy(self, w, r, src, land, send_sems, recv_sems, sending):
        x, y, c = _my_place()
        peer = _peer_of(x, y, c, r)
        return pltpu.make_async_remote_copy(
            src_ref=self.part(w, src, _flat(*peer)),
            dst_ref=self.slot(w, land, _flat(x, y, c) if sending else _flat(*peer)),
            send_sem=send_sems.at[w * _N_PEER + r - 1], recv_sem=recv_sems.at[w * _N_PEER + r - 1],
            device_id=peer, device_id_type=MESH)

    def start(self, srcs, lands, after=None):
        n = len(srcs)
        n_after = 0 if after is None else 1

        def body(*refs):
            src, land = refs[:n], refs[n:2 * n]
            outs = refs[2 * n + n_after:]
            send_sems, recv_sems, local_sems, token = outs[0], outs[1], outs[2], outs[3 + 2 * n]
            for w in range(n):
                self._own_copy(w, src[w], land[w], local_sems).start()
                for r in PEER_ORDER:
                    self._copy(w, r, src[w], land[w], send_sems, recv_sems, True).start()
            token[...] = jnp.zeros_like(token)

        sems = pltpu.SemaphoreType.DMA((n * _N_PEER,))
        outs = pl.pallas_call(
            body, name=self.name + "_start",
            out_shape=(sems, sems, pltpu.SemaphoreType.DMA((n,)),
                       *[pltpu.HBM(t.shape, t.dtype) for t in list(srcs) + list(lands)],
                       jax.ShapeDtypeStruct((8, 128), F32)),
            in_specs=[_HBM_SPEC] * (2 * n) + [_ANY_SPEC] * n_after,
            out_specs=(_SEM_SPEC, _SEM_SPEC, _SEM_SPEC, *[_HBM_SPEC] * (2 * n), pl.BlockSpec(memory_space=pltpu.VMEM)),
            input_output_aliases={i: 3 + i for i in range(2 * n)},
            compiler_params=pltpu.CompilerParams(has_side_effects=pltpu.SideEffectType.DATAFLOW_SIDE_EFFECTING),
        )(*[pltpu.with_memory_space_constraint(t, pltpu.HBM) for t in list(srcs) + list(lands)],
          *([after] if n_after else []))
        return outs[0], outs[1], outs[2], outs[3:3 + n], outs[3 + n:3 + 2 * n], outs[3 + 2 * n]

    def _own_copy(self, w, src, land, local_sems):
        me = _flat(*_my_place())
        return pltpu.make_async_copy(self.part(w, src, me), self.slot(w, land, me), local_sems.at[w])

    def wait(self, started, after, peers=PEER_ORDER, own=True, tag=""):
        send_sems, recv_sems, local_sems, srcs, lands, token = started
        n = len(srcs)

        def body(*refs):
            src, land = refs[:n], refs[n:2 * n]
            s_sems, r_sems, l_sems = refs[2 * n], refs[2 * n + 1], refs[2 * n + 2]
            for w in range(n):
                if own:
                    self._own_copy(w, src[w], land[w], l_sems).wait()
                for r in peers:
                    cp = self._copy(w, r, src[w], land[w], s_sems, r_sems, False)
                    cp.wait_send()
                    cp.wait_recv()

        outs = pl.pallas_call(
            body, name=self.name + "_wait" + tag,
            out_shape=[pltpu.HBM(t.shape, t.dtype) for t in list(srcs) + list(lands)],
            in_specs=[_HBM_SPEC] * (2 * n) + [_SEM_SPEC, _SEM_SPEC, _SEM_SPEC, _ANY_SPEC],
            out_specs=[_HBM_SPEC] * (2 * n),
            input_output_aliases={i: i for i in range(2 * n)},
            compiler_params=pltpu.CompilerParams(has_side_effects=pltpu.SideEffectType.DATAFLOW_SIDE_EFFECTING),
        )(*srcs, *lands, send_sems, recv_sems, local_sems, after)
        return outs[n:], (send_sems, recv_sems, local_sems, outs[:n], outs[n:], token)


DIRECT_PEERS = (1, 2, 4, 6)
FORWARDED = (3, 5, 7)


class _TwoLevelGather:
    def __init__(self, name, axes, sizes):
        self.name, self.axes, self.sizes = name, axes, sizes

    def _place(self, w, land, dev):
        return _shard_slice(land, self.axes[w], dev, self.sizes[w])

    def _direct(self, w, r, src, land, sems, sending):
        x, y, c = _my_place()
        peer = _peer_of(x, y, c, r)
        k = w * len(DIRECT_PEERS) + DIRECT_PEERS.index(r)
        return pltpu.make_async_remote_copy(
            src_ref=src, dst_ref=self._place(w, land, _flat(x, y, c) if sending else _flat(*peer)),
            send_sem=sems[0].at[k], recv_sem=sems[1].at[k], device_id=peer, device_id_type=MESH)

    def _passed_on(self, w, f, land, sems, sending):
        x, y, c = _my_place()
        owner = _flat(*_peer_of(x, y, c, (f ^ 1) if sending else f))
        slot = self._place(w, land, owner)
        k = w * len(FORWARDED) + FORWARDED.index(f)
        return pltpu.make_async_remote_copy(
            src_ref=slot, dst_ref=slot, send_sem=sems[2].at[k], recv_sem=sems[3].at[k],
            device_id=(x, y, 1 - c), device_id_type=MESH)

    def _own(self, w, src, land, sems):
        return pltpu.make_async_copy(src, self._place(w, land, _flat(*_my_place())), sems[4].at[w])

    def _call(self, suffix, body, sems, srcs, lands, after, make_sems):
        n = len(srcs)
        n_after = 0 if after is None else 1
        bufs = list(srcs) + list(lands)

        def wrapped(*refs):
            ins = refs[:2 * n]
            rest = refs[2 * n + (n_after if make_sems else 0):]
            body(ins[:n], ins[n:], rest[:5], rest[-1] if make_sems else None)

        buf_shapes = [pltpu.HBM(t.shape, t.dtype) for t in bufs]
        if make_sems:
            sem_types = [pltpu.SemaphoreType.DMA((n * len(DIRECT_PEERS),))] * 2 \
                + [pltpu.SemaphoreType.DMA((n * len(FORWARDED),))] * 2 + [pltpu.SemaphoreType.DMA((n,))]
            outs = pl.pallas_call(
                wrapped, name=self.name + suffix,
                out_shape=(*sem_types, *buf_shapes, jax.ShapeDtypeStruct((8, 128), F32)),
                in_specs=[_HBM_SPEC] * (2 * n) + [_ANY_SPEC] * n_after,
                out_specs=(*[_SEM_SPEC] * 5, *[_HBM_SPEC] * (2 * n), pl.BlockSpec(memory_space=pltpu.VMEM)),
                input_output_aliases={i: 5 + i for i in range(2 * n)},
                compiler_params=pltpu.CompilerParams(has_side_effects=pltpu.SideEffectType.DATAFLOW_SIDE_EFFECTING),
            )(*[pltpu.with_memory_space_constraint(t, pltpu.HBM) for t in bufs], *([after] if n_after else []))
            return tuple(outs[:5]), outs[5:5 + n], outs[5 + n:5 + 2 * n], outs[5 + 2 * n]
        outs = pl.pallas_call(
            wrapped, name=self.name + suffix,
            out_shape=buf_shapes,
            in_specs=[_HBM_SPEC] * (2 * n) + [_SEM_SPEC] * 5 + [_ANY_SPEC] * n_after,
            out_specs=[_HBM_SPEC] * (2 * n),
            input_output_aliases={i: i for i in range(2 * n)},
            compiler_params=pltpu.CompilerParams(has_side_effects=pltpu.SideEffectType.DATAFLOW_SIDE_EFFECTING),
        )(*bufs, *sems, *([after] if n_after else []))
        return sems, outs[:n], outs[n:], None

    def start(self, srcs, lands, after=None):
        n = len(srcs)

        def body(src, land, sems, token):
            for w in range(n):
                self._own(w, src[w], land[w], sems).start()
                for r in DIRECT_PEERS:
                    self._direct(w, r, src[w], land[w], sems, True).start()
            token[...] = jnp.zeros_like(token)

        return self._call("_start", body, None, srcs, lands, after, True)

    def forward(self, state, after, which, tag=""):
        sems, srcs, lands, token = state
        n = len(srcs)

        def body(src, land, sem_refs, _):
            for w in range(n):
                for r in which:
                    self._direct(w, r, src[w], land[w], sem_refs, False).wait_recv()
                    self._passed_on(w, r | 1, land[w], sem_refs, True).start()

        sems, srcs, lands, _ = self._call("_forward" + tag, body, sems, srcs, lands, after, False)
        return sems, srcs, lands, token

    def wait(self, state, after, direct=(), passed_on=(), sends=False, tag=""):
        sems, srcs, lands, token = state
        n = len(srcs)

        def body(src, land, sem_refs, _):
            for w in range(n):
                for r in direct:
                    self._direct(w, r, src[w], land[w], sem_refs, False).wait_recv()
                for f in passed_on:
                    self._passed_on(w, f, land[w], sem_refs, False).wait_recv()
                if sends:
                    self._own(w, src[w], land[w], sem_refs).wait()
                    for r in DIRECT_PEERS:
                        self._direct(w, r, src[w], land[w], sem_refs, True).wait_send()
                    for f in FORWARDED:
                        self._passed_on(w, f, land[w], sem_refs, True).wait_send()

        sems, srcs, lands, _ = self._call("_wait" + tag, body, sems, srcs, lands, after, False)
        return lands, (sems, srcs, lands, token)


def _scatter_exchange(name, axes, shard_sizes):
    def part(w, src, dev):
        return src if axes[w] is None else _shard_slice(src, axes[w], dev, shard_sizes[w])
    return _Exchange(name, part, lambda w, land, dev: land.at[dev])


def _adamw(name, partials, w, m, v):
    R, C = w.shape
    tr = R
    while tr * C * 4 > (1 << 20) and tr % 16 == 0:
        tr //= 2

    def body(p_ref, w_ref, m_ref, v_ref, g_ref, d_ref, nm_ref, nv_ref):
        g = p_ref[0].astype(F32)
        for jdev in range(1, N_DEV):
            g = g + p_ref[jdev].astype(F32)
        nm = ADAM_B1 * m_ref[...] + (1.0 - ADAM_B1) * g
        nv = ADAM_B2 * v_ref[...] + (1.0 - ADAM_B2) * (g * g)
        m_hat = nm / (1.0 - ADAM_B1 ** ADAM_STEP)
        v_hat = nv / (1.0 - ADAM_B2 ** ADAM_STEP)
        g_ref[...] = g
        d_ref[...] = -ADAM_LR * (m_hat / (jnp.sqrt(v_hat) + ADAM_EPS) + ADAM_WD * w_ref[...])
        nm_ref[...] = nm
        nv_ref[...] = nv

    spec = pl.BlockSpec((tr, C), lambda i: (i, 0))
    return pl.pallas_call(
        body, name=name, grid=(R // tr,),
        in_specs=[pl.BlockSpec((N_DEV, tr, C), lambda i: (0, i, 0)), spec, spec, spec],
        out_specs=[spec] * 4,
        out_shape=[jax.ShapeDtypeStruct((R, C), F32)] * 4,
        compiler_params=pltpu.CompilerParams(dimension_semantics=("parallel",),
                                             vmem_limit_bytes=_vmem_limit(24 << 20)),
    )(partials, w, m, v)


def _local_step(x, cos2, sin_fwd, sin_bwd, project_in, mix_weights, ffn_weights, pool_scale, g_mix, b_mix, g_ff, b_ff,
                target, send):
    S, D = x.shape
    aw = pw = D // 2
    u_col_block = 3
    gate_col0 = 4 * aw

    xb = x.astype(BF16)
    h, w_in = project_in(xb)
    dilated = [d for d in DILATIONS if d > 1]
    qkv = {1: h[None], **dict(zip(dilated, _to_residue_major("qkv_to_rm", h, 0, 3 * aw)))}
    fwd = [_attn_fwd(qkv[d], d, aw) for d in DILATIONS]
    o_attn, l_tot = _attn_combine([f[0] for f in fwd], [f[1] for f in fwd], aw)
    w_pool, w_ba, w_bp, w_out = mix_weights(o_attn)
    p, y_pre, pm = _pool_fwd(h, w_pool, pool_scale, pw, u_col_block)
    y_attn = _branch_attn(o_attn, w_ba)
    y_pool, merged = _branch_pool_merge(pm, w_bp, h, y_attn, gate_col0)
    w1, w2 = ffn_weights(merged)
    x1, x1b, xhat1, rstd1 = _out_proj_ln(merged, w_out, x, g_mix, b_mix)
    a = _ffn_up(x1b, w1)
    z2 = _residual_matmul("ffn_down", a, w2, "nn", x1)
    dz2, dz2b, dg_ff, db_ff, loss = _ln_loss_bwd(z2, g_ff, b_ff, target)

    tok = send("ff2", [_grad_weight("grad_w_ff2", a, dz2b)])
    dh1 = _ffn_down_bwd(dz2b, w2, a, after=tok)
    tok = send("ff1", [_grad_weight("grad_w_ff1", x1b, dh1)])
    dy1 = _residual_matmul("ffn_up_bwd", dh1, w1, "nt", dz2, after=tok)
    dz1, dz1b, dg_mix, db_mix = _ln_bwd(dy1, xhat1, rstd1, g_mix)
    tok = send("out", [_grad_weight("grad_w_out", merged, dz1b)])
    dy_attn, dy_pool, dh = _out_proj_bwd(dz1b, w_out, h, y_attn, y_pool, gate_col0, after=tok)
    tok = send("branch", [_grad_weight("grad_w_branch_attn", o_attn, dy_attn),
                          _grad_weight("grad_w_branch_pool", pm, dy_pool)])
    d_out, stats = _branch_attn_bwd(dy_attn, w_ba, o_attn, l_tot, after=tok)
    dy_pre, d_scale = _branch_pool_bwd(dy_pool, w_bp, y_pre, pool_scale)
    dh, dw_pool = _pool_bwd(dh, dy_pre, p, w_pool, pw, u_col_block)
    d_outs = {1: d_out[None], **dict(zip(dilated, _to_residue_major("dout_to_rm", d_out, 0, aw)))}
    statss = {1: stats[None], **dict(zip(dilated, _to_residue_major("stats_to_rm", stats, 0, HEAD_DIM)))}
    per_pattern = [_attn_bwd(qkv[d], d_outs[d], statss[d], d, aw) for d in DILATIONS]
    dh = _attn_bwd_finish(dh, per_pattern, cos2, sin_bwd, aw)
    small = jnp.concatenate((d_scale, dg_mix, db_mix, dg_ff, db_ff), axis=-1)
    tok = send("in", [_grad_weight("grad_w_in", xb, dh), dw_pool.astype(BF16),
                      small.reshape(small.shape[-1] // HEAD_DIM, HEAD_DIM)])
    grad_x = _residual_matmul("in_proj_bwd", dh, w_in, "nt", dz1, after=tok)
    return loss, grad_x


def _rope_tables(positions):
    half = HEAD_DIM // 2
    inv_freq = ROPE_THETA ** (-jnp.arange(half, dtype=F32) / half)
    ang = positions.astype(F32)[:, None] * inv_freq
    cos, sin = jnp.cos(ang), jnp.sin(ang)
    cos2 = jnp.concatenate([cos, cos], axis=-1)
    sin_fwd = jnp.concatenate([-sin, sin], axis=-1)
    return cos2, sin_fwd, -sin_fwd


def kernel(x, positions, w_in, w_pool, pool_scale, w_branch_attn, w_branch_pool, w_out, ln_mix_g, ln_mix_b, w_ff1, w_ff2, ln_ff_g, ln_ff_b, loss_target, m_w_in, m_w_pool, m_pool_scale, m_w_branch_attn, m_w_branch_pool, m_w_out, m_ln_mix_g, m_ln_mix_b, m_w_ff1, m_w_ff2, m_ln_ff_g, m_ln_ff_b, v_w_in, v_w_pool, v_pool_scale, v_w_branch_attn, v_w_branch_pool, v_w_out, v_ln_mix_g, v_ln_mix_b, v_w_ff1, v_w_ff2, v_ln_ff_g, v_ln_ff_b):
    big_w = (w_in[0], w_pool[0], w_branch_attn[0], w_branch_pool[0], w_out[0], w_ff1[0], w_ff2[0])
    big_m = (m_w_in[0], m_w_pool[0], m_w_branch_attn[0], m_w_branch_pool[0], m_w_out[0], m_w_ff1[0], m_w_ff2[0])
    big_v = (v_w_in[0], v_w_pool[0], v_w_branch_attn[0], v_w_branch_pool[0], v_w_out[0], v_w_ff1[0], v_w_ff2[0])
    shard_axes = (1, 1, 1, 1, 0, 1, 0)
    small_w = (pool_scale, ln_mix_g, ln_mix_b, ln_ff_g, ln_ff_b)
    small_m = (m_pool_scale, m_ln_mix_g, m_ln_mix_b, m_ln_ff_g, m_ln_ff_b)
    small_v = (v_pool_scale, v_ln_mix_g, v_ln_mix_b, v_ln_ff_g, v_ln_ff_b)

    names = ("w_in", "w_pool", "w_branch_attn", "w_branch_pool", "w_out", "w_ff1", "w_ff2")
    axis_of = dict(zip(names, shard_axes))
    shard_of = dict(zip(names, [w.astype(BF16) for w in big_w]))

    def full_buffer(n):
        s, ax = shard_of[n], axis_of[n]
        full = list(s.shape)
        full[ax] *= N_DEV
        return lax.empty(tuple(full), s.dtype)

    def gather_group(tag, group, after):
        ex = _TwoLevelGather(tag, [axis_of[n] for n in group], [shard_of[n].shape[axis_of[n]] for n in group])
        return ex, ex.start([shard_of[n] for n in group], [full_buffer(n) for n in group], after)

    in_ex, in_state = gather_group("gather_in", ("w_in",), None)
    mix_ex, mix_state = gather_group("gather_mix", ("w_pool", "w_branch_attn", "w_branch_pool", "w_out"),
                                     in_state[-1])
    ffn_ex, ffn_state = gather_group("gather_ffn", ("w_ff1", "w_ff2"), mix_state[-1])
    states = {"mix": mix_state, "ffn": ffn_state}
    me = 4 * lax.axis_index("x") + 2 * lax.axis_index("y") + lax.axis_index("c")
    block_cols = shard_of["w_in"].shape[1]
    neighbours, diagonal = (2, 4), (6,)

    def project_in(xb):
        n_rope_blocks = 2 * (x.shape[-1] // 2) // block_cols

        def piece(tag, w, blocks, h, **kw):
            return _in_proj_piece("in_proj_" + tag, xb, w, jnp.stack(blocks).astype(jnp.int32), cos2, sin_fwd, h,
                                  block_cols, n_rope_blocks, **kw)

        h = piece("own", shard_of["w_in"], [me], None, own_shard=True, after=ffn_state[-1])
        (w_in_land,), state = in_ex.wait(in_state, h, direct=(1,), tag="_sibling")
        h = piece("sibling", w_in_land, [me ^ 1], h)
        state = in_ex.forward(state, h, neighbours, tag="_neighbours")
        h = piece("neighbours", state[2][0], [me ^ r for r in neighbours], h)
        state = in_ex.forward(state, h, diagonal, tag="_diagonal")
        h = piece("diagonal", state[2][0], [me ^ r for r in diagonal], h)
        states["mix"] = mix_ex.forward(states["mix"], h, neighbours + diagonal)
        (w_in_land,), _ = in_ex.wait(state, h, passed_on=FORWARDED, sends=True, tag="_passed_on")
        h = piece("passed_on", w_in_land, [me ^ f for f in FORWARDED], h)
        return h, w_in_land

    def mix_weights(after):
        states["ffn"] = ffn_ex.forward(states["ffn"], after, neighbours + diagonal)
        return mix_ex.wait(states["mix"], after, direct=(1,), passed_on=FORWARDED, sends=True)[0]

    def ffn_weights(after):
        return ffn_ex.wait(states["ffn"], after, direct=(1,), passed_on=FORWARDED, sends=True)[0]

    groups = {"ff2": ("w_ff2",), "ff1": ("w_ff1",), "out": ("w_out",),
              "branch": ("w_branch_attn", "w_branch_pool"), "in": ("w_in", "w_pool", "small")}
    sent = {}

    def send(key, grads_):
        axes = [axis_of.get(n) for n in groups[key]]
        sizes = [None if ax is None else g.shape[ax] // N_DEV for g, ax in zip(grads_, axes)]
        lands = []
        for g, ax, size in zip(grads_, axes, sizes):
            shard = list(g.shape)
            if ax is not None:
                shard[ax] = size
            lands.append(lax.empty((N_DEV, *shard), g.dtype))
        ex = _scatter_exchange("scatter_" + key, axes, sizes)
        sent[key] = (ex, ex.start(list(grads_), lands))
        return sent[key][1][-1]

    cos2, sin_fwd, sin_bwd = _rope_tables(positions[0])
    loss, grad_x = _local_step(
        x[0], cos2, sin_fwd, sin_bwd, project_in, mix_weights, ffn_weights, pool_scale, ln_mix_g, ln_mix_b,
        ln_ff_g, ln_ff_b, loss_target[0], send)

    state = dict(zip(names, zip(big_w, big_m, big_v)))
    n_small = sum(w.shape[-1] for w in small_w)
    small_2d = (n_small // HEAD_DIM, HEAD_DIM)
    state["small"] = tuple(jnp.concatenate(t, axis=-1).reshape(small_2d) for t in (small_w, small_m, small_v))
    grads, deltas, new_ms, new_vs = {}, {}, {}, {}
    after = grad_x
    for key in ("ff2", "ff1", "out", "branch", "in"):
        ex, started = sent[key]
        for n, part in zip(groups[key], ex.wait(started, after)[0]):
            w, m, v = state[n]
            r2 = (-1, w.shape[-1])
            w2d = w.reshape(r2)
            res = _adamw("adamw_" + n, part.reshape((N_DEV,) + w2d.shape), w2d, m.reshape(r2), v.reshape(r2))
            after = res[0]
            if n == "small":
                small_out = [t.reshape(1, n_small) for t in res]
            else:
                grads[n], deltas[n], new_ms[n], new_vs[n] = (t.reshape((1,) + w.shape) for t in res)
    small_names = ("pool_scale", "ln_mix_g", "ln_mix_b", "ln_ff_g", "ln_ff_b")
    off = 0
    for n, w in zip(small_names, small_w):
        width = w.shape[-1]
        grads[n], deltas[n], new_ms[n], new_vs[n] = (t[:, off:off + width] for t in small_out)
        off += width

    order = ("w_in", "w_pool", "pool_scale", "w_branch_attn", "w_branch_pool", "w_out", "ln_mix_g", "ln_mix_b",
             "w_ff1", "w_ff2", "ln_ff_g", "ln_ff_b")
    total_loss = lax.psum(loss[0, 0], ("x", "y", "c"))
    return (total_loss, grad_x[None], *[grads[n] for n in order], *[deltas[n] for n in order],
            *[new_ms[n] for n in order], *[new_vs[n] for n in order])
```

```python
import functools

import jax
import jax.numpy as jnp
from jax import lax
from jax.experimental import pallas as pl
from jax.experimental.pallas import tpu as pltpu

F32 = jnp.float32
BF16 = jnp.bfloat16

N_DEV = 8
HEAD_DIM = 128
SUB_BLOCK = 128
DILATIONS = (1, 4, 16)
POOL_WINDOWS = (2, 4, 8, 16)
MAX_POOL_WINDOW = 16
POOL_HALO = 128
PERM_ROWS = 256
K_TILE = 1024
LN_EPS = 1e-5
DEEPNORM_ALPHA = 2.0 ** 0.25
ROPE_THETA = 10000.0
ATTN_SCALE = HEAD_DIM ** -0.5
ADAM_LR, ADAM_B1, ADAM_B2, ADAM_EPS, ADAM_WD, ADAM_STEP = 0.001, 0.9, 0.999, 1e-08, 0.01, 10
NEG_BIG = -1e30
VMEM_CAP_V7X = 64 * 1024 * 1024
MESH = pl.DeviceIdType.MESH


def _vmem_limit(est_bytes):
    return int(min(max(est_bytes * 5 // 4 + (4 << 20), 16 << 20), VMEM_CAP_V7X - (6 << 20)))


def _nbytes(shape, dtype):
    n = 1
    for s in shape:
        n *= s
    return n * jnp.dtype(dtype).itemsize


def _mm(name, a, b, form, tiles, outs, epi, extras=(), sequential=False, after=None):
    tm, tn, tk = tiles
    if form == "nn":
        (M, K), (K2, N) = a.shape, b.shape
    elif form == "nt":
        (M, K), (N, K2) = a.shape, b.shape
    else:
        (K, M), (K2, N) = a.shape, b.shape
    assert K == K2, (name, a.shape, b.shape)
    tm, tn, tk = min(tm, M), min(tn, N), min(tk, K)
    assert M % tm == 0 and N % tn == 0 and K % tk == 0, (name, M, N, K, tm, tn, tk)
    grid = (M // tm, N // tn, K // tk)
    nk = grid[2]
    if form == "nn":
        a_spec = pl.BlockSpec((tm, tk), lambda i, j, k: (i, k))
        b_spec = pl.BlockSpec((tk, tn), lambda i, j, k: (k, j))
        contract = ((1,), (0,))
    elif form == "nt":
        a_spec = pl.BlockSpec((tm, tk), lambda i, j, k: (i, k))
        b_spec = pl.BlockSpec((tn, tk), lambda i, j, k: (j, k))
        contract = ((1,), (1,))
    else:
        a_spec = pl.BlockSpec((tk, tm), lambda i, j, k: (k, i))
        b_spec = pl.BlockSpec((tk, tn), lambda i, j, k: (k, j))
        contract = ((0,), (0,))
    n_ex, n_out = len(extras), len(outs)
    n_after = 0 if after is None else 1

    def body(a_ref, b_ref, *rest):
        ex_refs = rest[:n_ex]
        rest = rest[n_ex + n_after:]
        out_refs = rest[:n_out]
        i, j, k = pl.program_id(0), pl.program_id(1), pl.program_id(2)

        def prod():
            return lax.dot_general(a_ref[...].astype(BF16), b_ref[...].astype(BF16),
                                   (contract, ((), ())), preferred_element_type=F32)

        if nk == 1:
            epi(prod(), ex_refs, out_refs, i, j)
        else:
            acc = rest[n_out]

            @pl.when(k == 0)
            def _():
                acc[...] = prod()

            @pl.when(k > 0)
            def _():
                acc[...] += prod()

            @pl.when(k == nk - 1)
            def _():
                epi(acc[...], ex_refs, out_refs, i, j)

    est = 2 * (_nbytes(a_spec.block_shape, a.dtype) + _nbytes(b_spec.block_shape, b.dtype))
    est += sum(2 * _nbytes(bs, arr.dtype) for arr, bs, _ in extras)
    est += sum(2 * _nbytes(bs, dt) for _, dt, bs, _ in outs)
    est += 4 * tm * tn * 4
    sem = ("arbitrary",) * 3 if sequential else ("parallel", "parallel", "arbitrary")
    return pl.pallas_call(
        body, name=name, grid=grid,
        in_specs=([a_spec, b_spec] + [pl.BlockSpec(bs, im) for _, bs, im in extras]
                  + [pl.BlockSpec(memory_space=pl.ANY)] * n_after),
        out_specs=[pl.BlockSpec(bs, im) for _, _, bs, im in outs],
        out_shape=[jax.ShapeDtypeStruct(sh, dt) for sh, dt, _, _ in outs],
        scratch_shapes=[pltpu.VMEM((tm, tn), F32)] if nk > 1 else [],
        compiler_params=pltpu.CompilerParams(dimension_semantics=sem, vmem_limit_bytes=_vmem_limit(est)),
    )(a, b, *[arr for arr, _, _ in extras], *([after] if n_after else []))


def _tile_out(shape, dtype, tm, tn):
    return (shape, dtype, (tm, tn), lambda i, j, k: (i, j))


def _row_sum_out(width):
    return ((1, width), F32, (1, width), lambda i, j, k: (0, 0))


def _accumulate_rows(ref, value, i):
    @pl.when(i == 0)
    def _():
        ref[...] = value

    @pl.when(i > 0)
    def _():
        ref[...] += value


def _layer_norm_bwd(dy, xhat, rstd, g):
    dxh = dy * g
    m1 = jnp.mean(dxh, axis=-1, keepdims=True)
    m2 = jnp.mean(dxh * xhat, axis=-1, keepdims=True)
    return rstd * (dxh - m1 - xhat * m2)


def _rope_apply(t, cos2, sin_signed):
    return t * cos2 + pltpu.roll(t, HEAD_DIM // 2, axis=1) * sin_signed


def _in_proj_piece(name, xb, w_in, col_blocks, cos2, sin_fwd, h_so_far, block_cols, n_rope_blocks, own_shard=False,
                   after=None):
    S, D = xb.shape
    W = w_in.shape[1] * (N_DEV if own_shard else 1)
    tm, tk = min(2048, S), min(K_TILE, D)
    nk = D // tk
    n_blocks = col_blocks.shape[0]

    def body(cols_ref, x_ref, w_ref, cos_ref, sin_ref, *rest):
        h_ref, acc = rest[-2], rest[-1]
        j, k = pl.program_id(1), pl.program_id(2)

        def prod():
            return _dot_nn(x_ref[...], w_ref[...])

        @pl.when(k == 0)
        def _():
            acc[...] = prod()

        @pl.when(k > 0)
        def _():
            acc[...] += prod()

        @pl.when(jnp.logical_and(k == nk - 1, cols_ref[j] < n_rope_blocks))
        def _():
            c, s = cos_ref[...], sin_ref[...]
            for hd in range(block_cols // HEAD_DIM):
                sl = slice(hd * HEAD_DIM, (hd + 1) * HEAD_DIM)
                h_ref[:, sl] = _rope_apply(acc[:, sl], c, s).astype(BF16)

        @pl.when(jnp.logical_and(k == nk - 1, cols_ref[j] >= n_rope_blocks))
        def _():
            h_ref[...] = acc[...].astype(BF16)

    row = pl.BlockSpec((tm, HEAD_DIM), lambda i, j, k, cols: (i, 0))
    carried = ([] if h_so_far is None else [h_so_far]) + ([] if after is None else [after])
    est = 2 * (tm * tk * 2 + tk * block_cols * 2 + tm * block_cols * 2 + 2 * tm * HEAD_DIM * 4) + 3 * tm * block_cols * 4
    return pl.pallas_call(
        body, name=name,
        grid_spec=pltpu.PrefetchScalarGridSpec(
            num_scalar_prefetch=1, grid=(S // tm, n_blocks, nk),
            in_specs=[pl.BlockSpec((tm, tk), lambda i, j, k, cols: (i, k)),
                      pl.BlockSpec((tk, block_cols), lambda i, j, k, cols: (k, 0 if own_shard else cols[j])), row, row]
                     + [pl.BlockSpec(memory_space=pl.ANY)] * len(carried),
            out_specs=pl.BlockSpec((tm, block_cols), lambda i, j, k, cols: (i, cols[j])),
            scratch_shapes=[pltpu.VMEM((tm, block_cols), F32)]),
        out_shape=jax.ShapeDtypeStruct((S, W), BF16),
        input_output_aliases={} if h_so_far is None else {5: 0},
        compiler_params=pltpu.CompilerParams(dimension_semantics=("parallel", "arbitrary", "arbitrary"),
                                             vmem_limit_bytes=_vmem_limit(est)),
    )(col_blocks, xb, w_in, cos2, sin_fwd, *carried)


def _attn_mask(mb):
    qi = lax.broadcasted_iota(jnp.int32, (SUB_BLOCK, 2 * SUB_BLOCK), 0)
    kj = lax.broadcasted_iota(jnp.int32, (SUB_BLOCK, 2 * SUB_BLOCK), 1)
    prev = jnp.logical_and(jnp.logical_and(kj < SUB_BLOCK, kj >= qi), mb > 0)
    cur = jnp.logical_and(kj >= SUB_BLOCK, kj - SUB_BLOCK <= qi)
    return jnp.logical_or(prev, cur)


def _both_blocks(prev_ref, cur_ref, sl):
    return jnp.concatenate([prev_ref[:, sl], cur_ref[:, sl]], axis=0)


def _dot_nt(a, b):
    return lax.dot_general(a, b, (((1,), (1,)), ((), ())), preferred_element_type=F32)


def _dot_tn(a, b):
    return lax.dot_general(a, b, (((0,), (0,)), ((), ())), preferred_element_type=F32)


def _dot_nn(a, b):
    return lax.dot_general(a, b, (((1,), (0,)), ((), ())), preferred_element_type=F32)


def _perm_matrix(d, to_residue_major):
    g = PERM_ROWS // d
    i = lax.broadcasted_iota(jnp.int32, (PERM_ROWS, PERM_ROWS), 0)
    j = lax.broadcasted_iota(jnp.int32, (PERM_ROWS, PERM_ROWS), 1)
    if to_residue_major:
        hit = j == (i % g) * d + i // g
    else:
        hit = j == (i % d) * g + i // d
    return hit.astype(BF16)


def _permute_rows(perm, x, terms=3):
    if x.dtype == BF16:
        return _dot_nn(perm, x)
    hi = x.astype(BF16)
    r1 = x - hi.astype(F32)
    mid = r1.astype(BF16)
    out = _dot_nn(perm, hi) + _dot_nn(perm, mid)
    if terms == 3:
        out = out + _dot_nn(perm, (r1 - mid.astype(F32)).astype(BF16))
    return out


def _rm_block(d, width):
    return pl.BlockSpec((d, PERM_ROWS // d, width), lambda i: (0, i, 0))


def _to_residue_major(name, x, col_block, width):
    S = x.shape[0]
    dils = [d for d in DILATIONS if d > 1]
    chunk = min(width, 1024)

    def body(x_ref, *out_refs):
        for d, o_ref in zip(dils, out_refs):
            perm = _perm_matrix(d, True)
            for c0 in range(0, width, chunk):
                cw = min(chunk, width - c0)
                y = _permute_rows(perm, x_ref[:, c0:c0 + cw])
                o_ref[:, :, c0:c0 + cw] = y.astype(x.dtype).reshape(d, PERM_ROWS // d, cw)

    return pl.pallas_call(
        body, name=name, grid=(S // PERM_ROWS,),
        in_specs=[pl.BlockSpec((PERM_ROWS, width), lambda i: (i, col_block))],
        out_specs=[_rm_block(d, width) for d in dils],
        out_shape=[jax.ShapeDtypeStruct((d, S // d, width), x.dtype) for d in dils],
        compiler_params=pltpu.CompilerParams(dimension_semantics=("parallel",),
                                             vmem_limit_bytes=_vmem_limit(32 << 20)),
    )(x)


def _qkv_specs(aw):
    def spec(col, prev):
        if prev:
            return pl.BlockSpec((None, SUB_BLOCK, aw), lambda r, mb: (r, jnp.maximum(mb - 1, 0), col))
        return pl.BlockSpec((None, SUB_BLOCK, aw), lambda r, mb: (r, mb, col))
    return [spec(0, False), spec(1, True), spec(1, False), spec(2, True), spec(2, False)]


def _put_column(tile, col, value):
    lane = lax.broadcasted_iota(jnp.int32, tile.shape, 1)
    return jnp.where(lane == col, value, tile)


def _attn_fwd(qkv, d, aw):
    _, rows, _ = qkv.shape
    n_heads = aw // HEAD_DIM
    nb = rows // SUB_BLOCK

    def body(q_ref, kp_ref, kc_ref, vp_ref, vc_ref, o_ref, lse_ref, s_buf, p_buf):
        mask = _attn_mask(pl.program_id(1))
        for hd in range(n_heads):
            sl = slice(hd * HEAD_DIM, (hd + 1) * HEAD_DIM)
            s_buf[hd] = _dot_nt(q_ref[:, sl], _both_blocks(kp_ref, kc_ref, sl))
        lse_tile = jnp.zeros((SUB_BLOCK, HEAD_DIM), F32)
        inv_tile = jnp.zeros((SUB_BLOCK, HEAD_DIM), F32)
        for hd in range(n_heads):
            s = jnp.where(mask, s_buf[hd] * ATTN_SCALE, NEG_BIG)
            m = jnp.max(s, axis=-1, keepdims=True)
            p = jnp.exp(s - m)
            l = jnp.sum(p, axis=-1, keepdims=True)
            p_buf[hd] = p.astype(BF16)
            lse_tile = _put_column(lse_tile, hd, m + jnp.log(l))
            inv_tile = _put_column(inv_tile, hd, 1.0 / l)
        lse_ref[...] = lse_tile
        for hd in range(n_heads):
            sl = slice(hd * HEAD_DIM, (hd + 1) * HEAD_DIM)
            o = _dot_nn(p_buf[hd], _both_blocks(vp_ref, vc_ref, sl))
            o_ref[:, sl] = o * inv_tile[:, hd:hd + 1]

    return pl.pallas_call(
        body, name=f"attn_fwd_d{d}", grid=(d, nb),
        in_specs=_qkv_specs(aw),
        out_specs=[pl.BlockSpec((None, SUB_BLOCK, aw), lambda r, mb: (r, mb, 0)),
                   pl.BlockSpec((None, SUB_BLOCK, HEAD_DIM), lambda r, mb: (r, mb, 0))],
        out_shape=[jax.ShapeDtypeStruct((d, rows, aw), F32), jax.ShapeDtypeStruct((d, rows, HEAD_DIM), F32)],
        scratch_shapes=[pltpu.VMEM((n_heads, SUB_BLOCK, 2 * SUB_BLOCK), F32),
                        pltpu.VMEM((n_heads, SUB_BLOCK, 2 * SUB_BLOCK), BF16)],
        compiler_params=pltpu.CompilerParams(dimension_semantics=("parallel", "parallel"),
                                             vmem_limit_bytes=_vmem_limit(16 << 20)),
    )(qkv, qkv, qkv, qkv, qkv)


def _attn_combine(outs, lses, aw):
    S = outs[0].shape[1]
    n_heads = aw // HEAD_DIM
    n_pat = len(DILATIONS)

    def body(*refs):
        o_refs, l_refs = refs[:n_pat], refs[n_pat:2 * n_pat]
        o_ref, lt_ref = refs[2 * n_pat], refs[2 * n_pat + 1]
        o_nat, l_nat = [], []
        for d, o_r, l_r in zip(DILATIONS, o_refs, l_refs):
            o_p = o_r[...].reshape(PERM_ROWS, aw)
            l_p = l_r[...].reshape(PERM_ROWS, HEAD_DIM)
            if d > 1:
                perm = _perm_matrix(d, False)
                o_p, l_p = _permute_rows(perm, o_p, terms=2), _permute_rows(perm, l_p)
            o_nat.append(o_p)
            l_nat.append(l_p)
        mx = functools.reduce(jnp.maximum, l_nat)
        es = [jnp.exp(l_p - mx) for l_p in l_nat]
        den = functools.reduce(jnp.add, es)
        lt_ref[...] = mx + jnp.log(den)
        ws = [e / den for e in es]
        for hd in range(n_heads):
            sl = slice(hd * HEAD_DIM, (hd + 1) * HEAD_DIM)
            o = ws[0][:, hd:hd + 1] * o_nat[0][:, sl]
            for pi in range(1, n_pat):
                o = o + ws[pi][:, hd:hd + 1] * o_nat[pi][:, sl]
            o_ref[:, sl] = o.astype(BF16)

    return pl.pallas_call(
        body, name="attn_combine", grid=(S // PERM_ROWS,),
        in_specs=[_rm_block(d, aw) for d in DILATIONS] + [_rm_block(d, HEAD_DIM) for d in DILATIONS],
        out_specs=[pl.BlockSpec((PERM_ROWS, aw), lambda i: (i, 0)), pl.BlockSpec((PERM_ROWS, HEAD_DIM), lambda i: (i, 0))],
        out_shape=[jax.ShapeDtypeStruct((S, aw), BF16), jax.ShapeDtypeStruct((S, HEAD_DIM), F32)],
        compiler_params=pltpu.CompilerParams(dimension_semantics=("parallel",),
                                             vmem_limit_bytes=_vmem_limit(40 << 20)),
    )(*outs, *lses)


def _band(tm, width, w, row_offset, transpose):
    t = lax.broadcasted_iota(jnp.int32, (tm, width), 0)
    u = lax.broadcasted_iota(jnp.int32, (tm, width), 1)
    dist = (u - t - row_offset) if transpose else (t + row_offset - u)
    return jnp.logical_and(dist >= 0, dist < w).astype(BF16)


def _pool_fwd(h, w_pool, pool_scale, pw, u_col_block):
    S, W = h.shape
    n_groups = len(POOL_WINDOWS)
    gw = pw // n_groups
    tm = min(512, S)
    halo_per_tile = tm // POOL_HALO

    def body(uc_ref, uh_ref, w_ref, sc_ref, p_ref, y_ref, pm_ref):
        i = pl.program_id(0)
        t_abs = i * tm + lax.broadcasted_iota(jnp.int32, (tm, 1), 0)
        for g, w in enumerate(POOL_WINDOWS):
            sl = slice(g * gw, (g + 1) * gw)
            uc = uc_ref[:, sl]
            uh = jnp.where(i > 0, uh_ref[:, sl], jnp.zeros((POOL_HALO, gw), BF16))
            ssum = _dot_nn(_band(tm, tm, w, 0, False), uc) + _dot_nn(_band(tm, POOL_HALO, w, POOL_HALO, False), uh)
            cnt = jnp.minimum(t_abs + 1, w).astype(F32)
            p = (ssum / cnt - uc.astype(F32)).astype(BF16)
            y = _dot_nn(p, w_ref[g])
            p_ref[:, sl] = p
            y_ref[:, sl] = y.astype(BF16)
            pm_ref[:, sl] = (y * sc_ref[:, sl]).astype(BF16)

    row = pl.BlockSpec((tm, pw), lambda i: (i, 0))
    return pl.pallas_call(
        body, name="pool_fwd", grid=(S // tm,),
        in_specs=[pl.BlockSpec((tm, pw), lambda i: (i, u_col_block)),
                  pl.BlockSpec((POOL_HALO, pw), lambda i: (jnp.maximum(i * halo_per_tile - 1, 0), u_col_block)),
                  pl.BlockSpec((n_groups, gw, gw), lambda i: (0, 0, 0)),
                  pl.BlockSpec((1, pw), lambda i: (0, 0))],
        out_specs=[row, row, row],
        out_shape=[jax.ShapeDtypeStruct((S, pw), BF16)] * 3,
        compiler_params=pltpu.CompilerParams(dimension_semantics=("parallel",),
                                             vmem_limit_bytes=_vmem_limit(24 << 20)),
    )(h, h, w_pool, pool_scale)


def _branch_attn(o_attn, w_ba):
    S, _ = o_attn.shape
    D = w_ba.shape[1]
    tm, tn = min(1024, S), D

    def epi(acc, ex, out, i, j):
        out[0][...] = acc.astype(BF16)

    (y,) = _mm("branch_attn", o_attn, w_ba, "nn", (tm, tn, 1024), [_tile_out((S, D), BF16, tm, tn)], epi)
    return y


def _branch_pool_merge(pm, w_bp, h, y_attn, gate_col0):
    S, _ = pm.shape
    D = w_bp.shape[1]
    tm, tn = min(512, S), D
    ga0, gp0 = gate_col0 // tn, (gate_col0 + D) // tn

    def epi(acc, ex, out, i, j):
        ga_ref, gp_ref, ya_ref = ex
        yp_ref, mg_ref = out
        yp = acc.astype(BF16)
        yp_ref[...] = yp
        mg = (jax.nn.sigmoid(ga_ref[...]).astype(F32) * ya_ref[...].astype(F32)
              + jax.nn.sigmoid(gp_ref[...]).astype(F32) * acc)
        mg_ref[...] = mg.astype(BF16)

    y_pool, merged = _mm(
        "branch_pool_merge", pm, w_bp, "nn", (tm, tn, 1024),
        [_tile_out((S, D), BF16, tm, tn), _tile_out((S, D), BF16, tm, tn)], epi,
        extras=[(h, (tm, tn), lambda i, j, k: (i, ga0 + j)), (h, (tm, tn), lambda i, j, k: (i, gp0 + j)),
                (y_attn, (tm, tn), lambda i, j, k: (i, j))])
    return y_pool, merged


def _layer_norm_rows(z, g, b):
    mu = jnp.mean(z, axis=-1, keepdims=True)
    zc = z - mu
    var = jnp.mean(zc * zc, axis=-1, keepdims=True)
    rstd = lax.rsqrt(var + LN_EPS)
    xhat = zc * rstd
    return xhat * g + b, xhat, rstd


def _out_proj_ln(merged, w_out, x, g, b):
    S, D = x.shape
    tm = min(256, S)

    def epi(acc, ex, out, i, j):
        x_ref, g_ref, b_ref = ex
        x1_ref, x1b_ref, xh_ref, rs_ref = out
        y, xhat, rstd = _layer_norm_rows(DEEPNORM_ALPHA * x_ref[...] + acc, g_ref[...], b_ref[...])
        x1_ref[...] = y
        x1b_ref[...] = y.astype(BF16)
        xh_ref[...] = xhat
        rs_ref[...] = jnp.broadcast_to(rstd, (tm, HEAD_DIM))

    row = lambda i, j, k: (i, 0)
    vec = lambda i, j, k: (0, 0)
    return _mm("out_proj_ln", merged, w_out, "nn", (tm, D, D),
               [((S, D), F32, (tm, D), row), ((S, D), BF16, (tm, D), row), ((S, D), F32, (tm, D), row),
                ((S, HEAD_DIM), F32, (tm, HEAD_DIM), row)], epi,
               extras=[(x, (tm, D), row), (g, (1, D), vec), (b, (1, D), vec)])


def _ffn_up(x1b, w1):
    S, D = x1b.shape
    F = w1.shape[1]
    tm, tn = min(1024, S), min(2048, F)

    def epi(acc, ex, out, i, j):
        r = jnp.maximum(acc, 0.0)
        out[0][...] = (r * r).astype(BF16)

    (a,) = _mm("ffn_up", x1b, w1, "nn", (tm, tn, K_TILE), [_tile_out((S, F), BF16, tm, tn)], epi)
    return a


def _residual_matmul(name, a, w, form, resid, after=None):
    S, D = resid.shape
    tm, tn = min(1024, S), min(2048, D)

    def epi(acc, ex, out, i, j):
        out[0][...] = DEEPNORM_ALPHA * ex[0][...] + acc

    (z,) = _mm(name, a, w, form, (tm, tn, K_TILE), [_tile_out((S, D), F32, tm, tn)], epi,
               extras=[(resid, (tm, tn), lambda i, j, k: (i, j))], after=after)
    return z


def _row_kernel(name, body, row_inputs, vec_inputs, row_outputs, sum_widths, tr):
    S = row_inputs[0].shape[0]
    row = lambda w: pl.BlockSpec((tr, w), lambda i: (i, 0))
    vec = lambda w: pl.BlockSpec((1, w), lambda i: (0, 0))

    def wrapped(*refs):
        body(pl.program_id(0), *refs)

    return pl.pallas_call(
        wrapped, name=name, grid=(S // tr,),
        in_specs=[row(t.shape[1]) for t in row_inputs] + [vec(t.shape[1]) for t in vec_inputs],
        out_specs=[row(w) for w, _ in row_outputs] + [vec(w) for w in sum_widths],
        out_shape=([jax.ShapeDtypeStruct((S, w), dt) for w, dt in row_outputs]
                   + [jax.ShapeDtypeStruct((1, w), F32) for w in sum_widths]),
        compiler_params=pltpu.CompilerParams(dimension_semantics=("arbitrary",),
                                             vmem_limit_bytes=_vmem_limit(40 << 20)),
    )(*row_inputs, *vec_inputs)


def _ln_loss_bwd(z2, g, b, target):
    S, D = z2.shape

    def body(i, z_ref, t_ref, g_ref, b_ref, dz_ref, dzb_ref, dg_ref, db_ref, loss_ref):
        gv = g_ref[...]
        y, xhat, rstd = _layer_norm_rows(z_ref[...], gv, b_ref[...])
        err = y - t_ref[...]
        loss = 0.5 * jnp.sum(jnp.mean(err * err, axis=-1, keepdims=True), axis=0, keepdims=True)
        dy = err * (1.0 / D)
        dz = _layer_norm_bwd(dy, xhat, rstd, gv)
        dz_ref[...] = dz
        dzb_ref[...] = dz.astype(BF16)
        _accumulate_rows(dg_ref, jnp.sum(dy * xhat, axis=0, keepdims=True), i)
        _accumulate_rows(db_ref, jnp.sum(dy, axis=0, keepdims=True), i)
        _accumulate_rows(loss_ref, jnp.broadcast_to(loss, (1, HEAD_DIM)), i)

    return _row_kernel("ln_loss_bwd", body, [z2, target], [g, b], [(D, F32), (D, BF16)], [D, D, HEAD_DIM],
                       min(256, S))


def _ln_bwd(dy, xhat, rstd, g):
    S, D = dy.shape

    def body(i, dy_ref, xh_ref, rs_ref, g_ref, dz_ref, dzb_ref, dg_ref, db_ref):
        dyv, xhat_v = dy_ref[...], xh_ref[...]
        dz = _layer_norm_bwd(dyv, xhat_v, rs_ref[:, :1], g_ref[...])
        dz_ref[...] = dz
        dzb_ref[...] = dz.astype(BF16)
        _accumulate_rows(dg_ref, jnp.sum(dyv * xhat_v, axis=0, keepdims=True), i)
        _accumulate_rows(db_ref, jnp.sum(dyv, axis=0, keepdims=True), i)

    return _row_kernel("ln_bwd", body, [dy, xhat, rstd], [g], [(D, F32), (D, BF16)], [D, D], min(256, S))


def _grad_weight(name, act, cot):
    M, N = act.shape[1], cot.shape[1]
    tm, tn = min(1024, M), min(2048, N)

    def epi(acc, ex, out, i, j):
        out[0][...] = acc.astype(BF16)

    (g,) = _mm(name, act, cot, "tn", (tm, tn, K_TILE), [_tile_out((M, N), BF16, tm, tn)], epi)
    return g


def _ffn_down_bwd(dz2b, w2, a, after=None):
    S, D = dz2b.shape
    F = w2.shape[0]
    tm, tn = min(1024, S), min(2048, F)

    def epi(acc, ex, out, i, j):
        out[0][...] = (acc * (2.0 * jnp.sqrt(ex[0][...])).astype(F32)).astype(BF16)

    (dh1,) = _mm("ffn_down_bwd", dz2b, w2, "nt", (tm, tn, K_TILE), [_tile_out((S, F), BF16, tm, tn)], epi,
                 extras=[(a, (tm, tn), lambda i, j, k: (i, j))], after=after)
    return dh1


def _out_proj_bwd(dz1b, w_out, h, y_attn, y_pool, gate_col0, after=None):
    S, D = dz1b.shape
    W = h.shape[1]
    tm = min(256, S)
    assert gate_col0 == 2 * D and W == 4 * D

    def epi(acc, ex, out, i, j):
        gates_ref, ya_ref, yp_ref = ex
        dya_ref, dyp_ref, dh_ref = out
        sa = jax.nn.sigmoid(gates_ref[:, :D]).astype(F32)
        sp = jax.nn.sigmoid(gates_ref[:, D:]).astype(F32)
        dya_ref[...] = (acc * sa).astype(BF16)
        dyp_ref[...] = (acc * sp).astype(BF16)
        dh_ref[:, :D] = (acc * ya_ref[...].astype(F32) * (sa * (1.0 - sa))).astype(BF16)
        dh_ref[:, D:] = (acc * yp_ref[...].astype(F32) * (sp * (1.0 - sp))).astype(BF16)

    row = lambda i, j, k: (i, 0)
    return _mm("out_proj_bwd", dz1b, w_out, "nt", (tm, D, D),
               [((S, D), BF16, (tm, D), row), ((S, D), BF16, (tm, D), row),
                ((S, W), BF16, (tm, 2 * D), lambda i, j, k: (i, 1))], epi,
               extras=[(h, (tm, 2 * D), lambda i, j, k: (i, 1)), (y_attn, (tm, D), row), (y_pool, (tm, D), row)],
               after=after)


def _branch_attn_bwd(dy_attn, w_ba, o_attn, l_tot, after=None):
    S, D = dy_attn.shape
    aw = w_ba.shape[0]
    n_heads = aw // HEAD_DIM
    tm = min(512, S)

    def epi(acc, ex, out, i, j):
        do_ref, st_ref = out
        do_ref[...] = acc.astype(BF16)
        o = ex[0][...].astype(F32)
        stats = ex[1][...]
        for hd in range(n_heads):
            sl = slice(hd * HEAD_DIM, (hd + 1) * HEAD_DIM)
            stats = _put_column(stats, n_heads + hd, jnp.sum(acc[:, sl] * o[:, sl], axis=-1, keepdims=True))
        st_ref[...] = stats

    row = lambda i, j, k: (i, 0)
    return _mm("branch_attn_bwd", dy_attn, w_ba, "nt", (tm, aw, D),
               [((S, aw), BF16, (tm, aw), row), ((S, HEAD_DIM), F32, (tm, HEAD_DIM), row)], epi,
               extras=[(o_attn, (tm, aw), row), (l_tot, (tm, HEAD_DIM), row)], after=after)


def _branch_pool_bwd(dy_pool, w_bp, y_pre, pool_scale):
    S, D = dy_pool.shape
    pw = w_bp.shape[0]
    tm = min(512, S)

    def epi(acc, ex, out, i, j):
        y_ref, sc_ref = ex
        dyp_ref, dsc_ref = out
        dyp_ref[...] = (acc * sc_ref[...]).astype(BF16)
        _accumulate_rows(dsc_ref, jnp.sum(acc * y_ref[...].astype(F32), axis=0, keepdims=True), i)

    row = lambda i, j, k: (i, 0)
    return _mm("branch_pool_bwd", dy_pool, w_bp, "nt", (tm, pw, D),
               [((S, pw), BF16, (tm, pw), row), _row_sum_out(pw)], epi,
               extras=[(y_pre, (tm, pw), row), (pool_scale, (1, pw), lambda i, j, k: (0, 0))],
               sequential=True)


def _pool_bwd(dh, dy_pre, p, w_pool, pw, u_col_block):
    S, W = dh.shape
    n_groups = len(POOL_WINDOWS)
    gw = pw // n_groups
    tm = min(512, S)
    n_tiles = S // tm
    halo_per_tile = tm // POOL_HALO
    n_halo_blocks = S // POOL_HALO

    def body(dh_in_ref, dyc_ref, dyh_ref, p_ref, w_ref, dh_ref, dw_ref):
        del dh_in_ref
        i = pl.program_id(0)
        t_cur = i * tm + lax.broadcasted_iota(jnp.int32, (tm, 1), 0)
        t_halo = (i + 1) * tm + lax.broadcasted_iota(jnp.int32, (POOL_HALO, 1), 0)
        for g, w in enumerate(POOL_WINDOWS):
            sl = slice(g * gw, (g + 1) * gw)
            wg = w_ref[g]
            dyc = dyc_ref[:, sl]
            dyh = jnp.where(i < n_tiles - 1, dyh_ref[:, sl], jnp.zeros((POOL_HALO, gw), BF16))
            dp_cur = _dot_nt(dyc, wg)
            dp_halo = _dot_nt(dyh, wg)
            dpc_cur = (dp_cur / jnp.minimum(t_cur + 1, w).astype(F32)).astype(BF16)
            dpc_halo = (dp_halo / jnp.minimum(t_halo + 1, w).astype(F32)).astype(BF16)
            du = (_dot_nn(_band(tm, tm, w, 0, True), dpc_cur)
                  + _dot_nn(_band(tm, POOL_HALO, w, -tm, True), dpc_halo) - dp_cur)
            dh_ref[:, sl] = du.astype(BF16)
            dw = _dot_tn(p_ref[:, sl], dyc)

            @pl.when(i == 0)
            def _():
                dw_ref[g] = dw

            @pl.when(i > 0)
            def _():
                dw_ref[g] += dw

    row = pl.BlockSpec((tm, pw), lambda i: (i, 0))
    dh_new, dw_pool = pl.pallas_call(
        body, name="pool_bwd", grid=(n_tiles,),
        in_specs=[pl.BlockSpec(memory_space=pl.ANY), row,
                  pl.BlockSpec((POOL_HALO, pw), lambda i: (jnp.minimum((i + 1) * halo_per_tile, n_halo_blocks - 1), 0)),
                  row, pl.BlockSpec((n_groups, gw, gw), lambda i: (0, 0, 0))],
        out_specs=[pl.BlockSpec((tm, pw), lambda i: (i, u_col_block)),
                   pl.BlockSpec((n_groups, gw, gw), lambda i: (0, 0, 0))],
        out_shape=[jax.ShapeDtypeStruct((S, W), BF16), jax.ShapeDtypeStruct((n_groups, gw, gw), F32)],
        input_output_aliases={0: 0},
        compiler_params=pltpu.CompilerParams(dimension_semantics=("arbitrary",),
                                             vmem_limit_bytes=_vmem_limit(24 << 20)),
    )(dh, dy_pre, dy_pre, p, w_pool)
    return dh_new, dw_pool


def _attn_bwd(qkv, d_out, stats, d, aw):
    _, rows, _ = qkv.shape
    n_heads = aw // HEAD_DIM
    nb = rows // SUB_BLOCK
    n_blocks = d * nb

    def body(q_ref, kp_ref, kc_ref, vp_ref, vc_ref, do_ref, st_ref, dq_ref, dk_ref, dv_ref,
             carry_k, carry_v, s_buf, dp_buf, p_buf, ds_buf):
        step = pl.program_id(0)

        @pl.when(step == 0)
        def _():
            carry_k[...] = jnp.zeros_like(carry_k)
            carry_v[...] = jnp.zeros_like(carry_v)

        @pl.when(step < n_blocks)
        def _():
            mask = _attn_mask(step % nb)
            st = st_ref[...]
            for hd in range(n_heads):
                sl = slice(hd * HEAD_DIM, (hd + 1) * HEAD_DIM)
                s_buf[hd] = _dot_nt(q_ref[:, sl], _both_blocks(kp_ref, kc_ref, sl))
                dp_buf[hd] = _dot_nt(do_ref[:, sl], _both_blocks(vp_ref, vc_ref, sl))
            for hd in range(n_heads):
                lt, dl = st[:, hd:hd + 1], st[:, n_heads + hd:n_heads + hd + 1]
                p = jnp.where(mask, jnp.exp(jnp.where(mask, s_buf[hd] * ATTN_SCALE - lt, NEG_BIG)), 0.0)
                p_buf[hd] = p.astype(BF16)
                ds_buf[hd] = (p * (dp_buf[hd] - dl) * ATTN_SCALE).astype(BF16)
            for hd in range(n_heads):
                sl = slice(hd * HEAD_DIM, (hd + 1) * HEAD_DIM)
                dq_ref[:, sl] = _dot_nn(ds_buf[hd], _both_blocks(kp_ref, kc_ref, sl)).astype(BF16)
                dk_both = _dot_tn(ds_buf[hd], q_ref[:, sl])
                dv_both = _dot_tn(p_buf[hd], do_ref[:, sl])
                dk_ref[:, sl] = (carry_k[:, sl] + dk_both[:SUB_BLOCK]).astype(BF16)
                dv_ref[:, sl] = (carry_v[:, sl] + dv_both[:SUB_BLOCK]).astype(BF16)
                carry_k[:, sl] = dk_both[SUB_BLOCK:]
                carry_v[:, sl] = dv_both[SUB_BLOCK:]

        @pl.when(step == n_blocks)
        def _():
            dk_ref[...] = carry_k[...].astype(BF16)
            dv_ref[...] = carry_v[...].astype(BF16)

    def cur(step):
        return jnp.minimum(step, n_blocks - 1)

    def qkv_spec(col, prev):
        if prev:
            return pl.BlockSpec((SUB_BLOCK, aw), lambda s: (jnp.maximum(cur(s) - 1, 0), col))
        return pl.BlockSpec((SUB_BLOCK, aw), lambda s: (cur(s), col))

    def at_cur(w):
        return pl.BlockSpec((SUB_BLOCK, w), lambda s: (cur(s), 0))

    finished = pl.BlockSpec((SUB_BLOCK, aw), lambda s: (jnp.maximum(s - 1, 0), 0))
    pair = (n_heads, SUB_BLOCK, 2 * SUB_BLOCK)
    flat = lambda t: t.reshape(d * rows, t.shape[-1])
    qkv2 = flat(qkv)
    outs = pl.pallas_call(
        body, name=f"attn_bwd_d{d}", grid=(n_blocks + 1,),
        in_specs=[qkv_spec(0, False), qkv_spec(1, True), qkv_spec(1, False), qkv_spec(2, True), qkv_spec(2, False),
                  at_cur(aw), at_cur(HEAD_DIM)],
        out_specs=[at_cur(aw), finished, finished],
        out_shape=[jax.ShapeDtypeStruct((d * rows, aw), BF16)] * 3,
        scratch_shapes=[pltpu.VMEM((SUB_BLOCK, aw), F32), pltpu.VMEM((SUB_BLOCK, aw), F32),
                        pltpu.VMEM(pair, F32), pltpu.VMEM(pair, F32), pltpu.VMEM(pair, BF16), pltpu.VMEM(pair, BF16)],
        compiler_params=pltpu.CompilerParams(dimension_semantics=("arbitrary",),
                                             vmem_limit_bytes=_vmem_limit(24 << 20)),
    )(qkv2, qkv2, qkv2, qkv2, qkv2, flat(d_out), flat(stats))
    return [t.reshape(d, rows, aw) for t in outs]


def _attn_bwd_finish(dh, per_pattern, cos2, sin_bwd, aw):
    S, W = dh.shape
    n_heads = aw // HEAD_DIM
    n_pat = len(DILATIONS)

    def body(*refs):
        grad_refs = refs[1:1 + 3 * n_pat]
        cos_ref, sin_ref = refs[1 + 3 * n_pat], refs[2 + 3 * n_pat]
        out_ref = refs[3 + 3 * n_pat]
        perms = {d: _perm_matrix(d, False) for d in DILATIONS if d > 1}
        totals = []
        for which in range(3):
            tot = None
            for pi, d in enumerate(DILATIONS):
                g = grad_refs[which * n_pat + pi][...].reshape(PERM_ROWS, aw)
                g = _permute_rows(perms[d], g) if d > 1 else g.astype(F32)
                tot = g if tot is None else tot + g
            totals.append(tot)
        dq, dk, dv = totals
        c, s = cos_ref[...], sin_ref[...]
        for hd in range(n_heads):
            sl = slice(hd * HEAD_DIM, (hd + 1) * HEAD_DIM)
            out_ref[:, sl] = _rope_apply(dq[:, sl], c, s).astype(BF16)
            out_ref[:, aw + hd * HEAD_DIM:aw + (hd + 1) * HEAD_DIM] = _rope_apply(dk[:, sl], c, s).astype(BF16)
        out_ref[:, 2 * aw:] = dv.astype(BF16)

    grads = [pp[which] for which in range(3) for pp in per_pattern]
    rope_spec = pl.BlockSpec((PERM_ROWS, HEAD_DIM), lambda i: (i, 0))
    return pl.pallas_call(
        body, name="attn_bwd_finish", grid=(S // PERM_ROWS,),
        in_specs=([pl.BlockSpec(memory_space=pl.ANY)] + [_rm_block(d, aw) for d in DILATIONS] * 3
                  + [rope_spec, rope_spec]),
        out_specs=pl.BlockSpec((PERM_ROWS, 3 * aw), lambda i: (i, 0)),
        out_shape=jax.ShapeDtypeStruct((S, W), BF16),
        input_output_aliases={0: 0},
        compiler_params=pltpu.CompilerParams(dimension_semantics=("parallel",),
                                             vmem_limit_bytes=_vmem_limit(32 << 20)),
    )(dh, *grads, cos2, sin_bwd)


def _my_place():
    x, y, c = lax.axis_index("x"), lax.axis_index("y"), lax.axis_index("c")
    return x, y, c


def _flat(px, py, pc):
    return 4 * px + 2 * py + pc


def _shard_slice(ref, axis, idx, size):
    start = pl.multiple_of(idx * size, size)
    ix = [slice(None)] * len(ref.shape)
    ix[axis] = pl.ds(start, size)
    return ref.at[tuple(ix)]


_HBM_SPEC = pl.BlockSpec(memory_space=pltpu.HBM)
_SEM_SPEC = pl.BlockSpec(memory_space=pltpu.SEMAPHORE)
_ANY_SPEC = pl.BlockSpec(memory_space=pl.ANY)
_N_PEER = N_DEV - 1
SIBLING, SAME_CORE_NEIGHBOURS, OTHER_CORE_NEIGHBOURS, DIAGONAL = (1,), (2, 4), (3, 5), (6, 7)
PEER_ORDER = SIBLING + SAME_CORE_NEIGHBOURS + OTHER_CORE_NEIGHBOURS + DIAGONAL


def _peer_of(x, y, c, r):
    return (x ^ ((r >> 2) & 1), y ^ ((r >> 1) & 1), c ^ (r & 1))


class _Exchange:
    def __init__(self, name, part, slot):
        self.name, self.part, self.slot = name, part, slot

    def _copy(self, w, r, src, land, send_sems, recv_sems, sending):
        x, y, c = _my_place()
        peer = _peer_of(x, y, c, r)
        return pltpu.make_async_remote_copy(
            src_ref=self.part(w, src, _flat(*peer)),
            dst_ref=self.slot(w, land, _flat(x, y, c) if sending else _flat(*peer)),
            send_sem=send_sems.at[w * _N_PEER + r - 1], recv_sem=recv_sems.at[w * _N_PEER + r - 1],
            device_id=peer, device_id_type=MESH)

    def start(self, srcs, lands, after=None):
        n = len(srcs)
        n_after = 0 if after is None else 1

        def body(*refs):
            src, land = refs[:n], refs[n:2 * n]
            outs = refs[2 * n + n_after:]
            send_sems, recv_sems, local_sems, token = outs[0], outs[1], outs[2], outs[3 + 2 * n]
            for w in range(n):
                self._own_copy(w, src[w], land[w], local_sems).start()
                for r in PEER_ORDER:
                    self._copy(w, r, src[w], land[w], send_sems, recv_sems, True).start()
            token[...] = jnp.zeros_like(token)

        sems = pltpu.SemaphoreType.DMA((n * _N_PEER,))
        outs = pl.pallas_call(
            body, name=self.name + "_start",
            out_shape=(sems, sems, pltpu.SemaphoreType.DMA((n,)),
                       *[pltpu.HBM(t.shape, t.dtype) for t in list(srcs) + list(lands)],
                       jax.ShapeDtypeStruct((8, 128), F32)),
            in_specs=[_HBM_SPEC] * (2 * n) + [_ANY_SPEC] * n_after,
            out_specs=(_SEM_SPEC, _SEM_SPEC, _SEM_SPEC, *[_HBM_SPEC] * (2 * n), pl.BlockSpec(memory_space=pltpu.VMEM)),
            input_output_aliases={i: 3 + i for i in range(2 * n)},
            compiler_params=pltpu.CompilerParams(has_side_effects=pltpu.SideEffectType.DATAFLOW_SIDE_EFFECTING),
        )(*[pltpu.with_memory_space_constraint(t, pltpu.HBM) for t in list(srcs) + list(lands)],
          *([after] if n_after else []))
        return outs[0], outs[1], outs[2], outs[3:3 + n], outs[3 + n:3 + 2 * n], outs[3 + 2 * n]

    def _own_copy(self, w, src, land, local_sems):
        me = _flat(*_my_place())
        return pltpu.make_async_copy(self.part(w, src, me), self.slot(w, land, me), local_sems.at[w])

    def wait(self, started, after, peers=PEER_ORDER, own=True, tag=""):
        send_sems, recv_sems, local_sems, srcs, lands, token = started
        n = len(srcs)

        def body(*refs):
            src, land = refs[:n], refs[n:2 * n]
            s_sems, r_sems, l_sems = refs[2 * n], refs[2 * n + 1], refs[2 * n + 2]
            for w in range(n):
                if own:
                    self._own_copy(w, src[w], land[w], l_sems).wait()
                for r in peers:
                    cp = self._copy(w, r, src[w], land[w], s_sems, r_sems, False)
                    cp.wait_send()
                    cp.wait_recv()

        outs = pl.pallas_call(
            body, name=self.name + "_wait" + tag,
            out_shape=[pltpu.HBM(t.shape, t.dtype) for t in list(srcs) + list(lands)],
            in_specs=[_HBM_SPEC] * (2 * n) + [_SEM_SPEC, _SEM_SPEC, _SEM_SPEC, _ANY_SPEC],
            out_specs=[_HBM_SPEC] * (2 * n),
            input_output_aliases={i: i for i in range(2 * n)},
            compiler_params=pltpu.CompilerParams(has_side_effects=pltpu.SideEffectType.DATAFLOW_SIDE_EFFECTING),
        )(*srcs, *lands, send_sems, recv_sems, local_sems, after)
        return outs[n:], (send_sems, recv_sems, local_sems, outs[:n], outs[n:], token)


DIRECT_PEERS = (1, 2, 4, 6)
FORWARDED = (3, 5, 7)


class _TwoLevelGather:
    def __init__(self, name, axes, sizes):
        self.name, self.axes, self.sizes = name, axes, sizes

    def _place(self, w, land, dev):
        return _shard_slice(land, self.axes[w], dev, self.sizes[w])

    def _direct(self, w, r, src, land, sems, sending):
        x, y, c = _my_place()
        peer = _peer_of(x, y, c, r)
        k = w * len(DIRECT_PEERS) + DIRECT_PEERS.index(r)
        return pltpu.make_async_remote_copy(
            src_ref=src, dst_ref=self._place(w, land, _flat(x, y, c) if sending else _flat(*peer)),
            send_sem=sems[0].at[k], recv_sem=sems[1].at[k], device_id=peer, device_id_type=MESH)

    def _passed_on(self, w, f, land, sems, sending):
        x, y, c = _my_place()
        owner = _flat(*_peer_of(x, y, c, (f ^ 1) if sending else f))
        slot = self._place(w, land, owner)
        k = w * len(FORWARDED) + FORWARDED.index(f)
        return pltpu.make_async_remote_copy(
            src_ref=slot, dst_ref=slot, send_sem=sems[2].at[k], recv_sem=sems[3].at[k],
            device_id=(x, y, 1 - c), device_id_type=MESH)

    def _own(self, w, src, land, sems):
        return pltpu.make_async_copy(src, self._place(w, land, _flat(*_my_place())), sems[4].at[w])

    def _call(self, suffix, body, sems, srcs, lands, after, make_sems):
        n = len(srcs)
        n_after = 0 if after is None else 1
        bufs = list(srcs) + list(lands)

        def wrapped(*refs):
            ins = refs[:2 * n]
            rest = refs[2 * n + (n_after if make_sems else 0):]
            body(ins[:n], ins[n:], rest[:5], rest[-1] if make_sems else None)

        buf_shapes = [pltpu.HBM(t.shape, t.dtype) for t in bufs]
        if make_sems:
            sem_types = [pltpu.SemaphoreType.DMA((n * len(DIRECT_PEERS),))] * 2 \
                + [pltpu.SemaphoreType.DMA((n * len(FORWARDED),))] * 2 + [pltpu.SemaphoreType.DMA((n,))]
            outs = pl.pallas_call(
                wrapped, name=self.name + suffix,
                out_shape=(*sem_types, *buf_shapes, jax.ShapeDtypeStruct((8, 128), F32)),
                in_specs=[_HBM_SPEC] * (2 * n) + [_ANY_SPEC] * n_after,
                out_specs=(*[_SEM_SPEC] * 5, *[_HBM_SPEC] * (2 * n), pl.BlockSpec(memory_space=pltpu.VMEM)),
                input_output_aliases={i: 5 + i for i in range(2 * n)},
                compiler_params=pltpu.CompilerParams(has_side_effects=pltpu.SideEffectType.DATAFLOW_SIDE_EFFECTING),
            )(*[pltpu.with_memory_space_constraint(t, pltpu.HBM) for t in bufs], *([after] if n_after else []))
            return tuple(outs[:5]), outs[5:5 + n], outs[5 + n:5 + 2 * n], outs[5 + 2 * n]
        outs = pl.pallas_call(
            wrapped, name=self.name + suffix,
            out_shape=buf_shapes,
            in_specs=[_HBM_SPEC] * (2 * n) + [_SEM_SPEC] * 5 + [_ANY_SPEC] * n_after,
            out_specs=[_HBM_SPEC] * (2 * n),
            input_output_aliases={i: i for i in range(2 * n)},
            compiler_params=pltpu.CompilerParams(has_side_effects=pltpu.SideEffectType.DATAFLOW_SIDE_EFFECTING),
        )(*bufs, *sems, *([after] if n_after else []))
        return sems, outs[:n], outs[n:], None

    def start(self, srcs, lands, after=None):
        n = len(srcs)

        def body(src, land, sems, token):
            for w in range(n):
                self._own(w, src[w], land[w], sems).start()
                for r in DIRECT_PEERS:
                    self._direct(w, r, src[w], land[w], sems, True).start()
            token[...] = jnp.zeros_like(token)

        return self._call("_start", body, None, srcs, lands, after, True)

    def forward(self, state, after, which, tag=""):
        sems, srcs, lands, token = state
        n = len(srcs)

        def body(src, land, sem_refs, _):
            for w in range(n):
                for r in which:
                    self._direct(w, r, src[w], land[w], sem_refs, False).wait_recv()
                    self._passed_on(w, r | 1, land[w], sem_refs, True).start()

        sems, srcs, lands, _ = self._call("_forward" + tag, body, sems, srcs, lands, after, False)
        return sems, srcs, lands, token

    def wait(self, state, after, direct=(), passed_on=(), sends=False, tag=""):
        sems, srcs, lands, token = state
        n = len(srcs)

        def body(src, land, sem_refs, _):
            for w in range(n):
                for r in direct:
                    self._direct(w, r, src[w], land[w], sem_refs, False).wait_recv()
                for f in passed_on:
                    self._passed_on(w, f, land[w], sem_refs, False).wait_recv()
                if sends:
                    self._own(w, src[w], land[w], sem_refs).wait()
                    for r in DIRECT_PEERS:
                        self._direct(w, r, src[w], land[w], sem_refs, True).wait_send()
                    for f in FORWARDED:
                        self._passed_on(w, f, land[w], sem_refs, True).wait_send()

        sems, srcs, lands, _ = self._call("_wait" + tag, body, sems, srcs, lands, after, False)
        return lands, (sems, srcs, lands, token)


def _scatter_exchange(name, axes, shard_sizes):
    def part(w, src, dev):
        return src if axes[w] is None else _shard_slice(src, axes[w], dev, shard_sizes[w])
    return _Exchange(name, part, lambda w, land, dev: land.at[dev])


def _adamw(name, partials, w, m, v):
    R, C = w.shape
    tr = R
    while tr * C * 4 > (1 << 20) and tr % 16 == 0:
        tr //= 2

    def body(p_ref, w_ref, m_ref, v_ref, g_ref, d_ref, nm_ref, nv_ref):
        g = p_ref[0].astype(F32)
        for jdev in range(1, N_DEV):
            g = g + p_ref[jdev].astype(F32)
        nm = ADAM_B1 * m_ref[...] + (1.0 - ADAM_B1) * g
        nv = ADAM_B2 * v_ref[...] + (1.0 - ADAM_B2) * (g * g)
        m_hat = nm / (1.0 - ADAM_B1 ** ADAM_STEP)
        v_hat = nv / (1.0 - ADAM_B2 ** ADAM_STEP)
        g_ref[...] = g
        d_ref[...] = -ADAM_LR * (m_hat / (jnp.sqrt(v_hat) + ADAM_EPS) + ADAM_WD * w_ref[...])
        nm_ref[...] = nm
        nv_ref[...] = nv

    spec = pl.BlockSpec((tr, C), lambda i: (i, 0))
    return pl.pallas_call(
        body, name=name, grid=(R // tr,),
        in_specs=[pl.BlockSpec((N_DEV, tr, C), lambda i: (0, i, 0)), spec, spec, spec],
        out_specs=[spec] * 4,
        out_shape=[jax.ShapeDtypeStruct((R, C), F32)] * 4,
        compiler_params=pltpu.CompilerParams(dimension_semantics=("parallel",),
                                             vmem_limit_bytes=_vmem_limit(24 << 20)),
    )(partials, w, m, v)


def _local_step(x, cos2, sin_fwd, sin_bwd, project_in, mix_weights, ffn_weights, pool_scale, g_mix, b_mix, g_ff, b_ff,
                target, send):
    S, D = x.shape
    aw = pw = D // 2
    u_col_block = 3
    gate_col0 = 4 * aw

    xb = x.astype(BF16)
    h, w_in = project_in(xb)
    dilated = [d for d in DILATIONS if d > 1]
    qkv = {1: h[None], **dict(zip(dilated, _to_residue_major("qkv_to_rm", h, 0, 3 * aw)))}
    fwd = [_attn_fwd(qkv[d], d, aw) for d in DILATIONS]
    o_attn, l_tot = _attn_combine([f[0] for f in fwd], [f[1] for f in fwd], aw)
    w_pool, w_ba, w_bp, w_out = mix_weights(o_attn)
    p, y_pre, pm = _pool_fwd(h, w_pool, pool_scale, pw, u_col_block)
    y_attn = _branch_attn(o_attn, w_ba)
    y_pool, merged = _branch_pool_merge(pm, w_bp, h, y_attn, gate_col0)
    w1, w2 = ffn_weights(merged)
    x1, x1b, xhat1, rstd1 = _out_proj_ln(merged, w_out, x, g_mix, b_mix)
    a = _ffn_up(x1b, w1)
    z2 = _residual_matmul("ffn_down", a, w2, "nn", x1)
    dz2, dz2b, dg_ff, db_ff, loss = _ln_loss_bwd(z2, g_ff, b_ff, target)

    tok = send("ff2", [_grad_weight("grad_w_ff2", a, dz2b)])
    dh1 = _ffn_down_bwd(dz2b, w2, a, after=tok)
    tok = send("ff1", [_grad_weight("grad_w_ff1", x1b, dh1)])
    dy1 = _residual_matmul("ffn_up_bwd", dh1, w1, "nt", dz2, after=tok)
    dz1, dz1b, dg_mix, db_mix = _ln_bwd(dy1, xhat1, rstd1, g_mix)
    tok = send("out", [_grad_weight("grad_w_out", merged, dz1b)])
    dy_attn, dy_pool, dh = _out_proj_bwd(dz1b, w_out, h, y_attn, y_pool, gate_col0, after=tok)
    tok = send("branch", [_grad_weight("grad_w_branch_attn", o_attn, dy_attn),
                          _grad_weight("grad_w_branch_pool", pm, dy_pool)])
    d_out, stats = _branch_attn_bwd(dy_attn, w_ba, o_attn, l_tot, after=tok)
    dy_pre, d_scale = _branch_pool_bwd(dy_pool, w_bp, y_pre, pool_scale)
    dh, dw_pool = _pool_bwd(dh, dy_pre, p, w_pool, pw, u_col_block)
    d_outs = {1: d_out[None], **dict(zip(dilated, _to_residue_major("dout_to_rm", d_out, 0, aw)))}
    statss = {1: stats[None], **dict(zip(dilated, _to_residue_major("stats_to_rm", stats, 0, HEAD_DIM)))}
    per_pattern = [_attn_bwd(qkv[d], d_outs[d], statss[d], d, aw) for d in DILATIONS]
    dh = _attn_bwd_finish(dh, per_pattern, cos2, sin_bwd, aw)
    small = jnp.concatenate((d_scale, dg_mix, db_mix, dg_ff, db_ff), axis=-1)
    tok = send("in", [_grad_weight("grad_w_in", xb, dh), dw_pool.astype(BF16),
                      small.reshape(small.shape[-1] // HEAD_DIM, HEAD_DIM)])
    grad_x = _residual_matmul("in_proj_bwd", dh, w_in, "nt", dz1, after=tok)
    return loss, grad_x


def _rope_tables(positions):
    half = HEAD_DIM // 2
    inv_freq = ROPE_THETA ** (-jnp.arange(half, dtype=F32) / half)
    ang = positions.astype(F32)[:, None] * inv_freq
    cos, sin = jnp.cos(ang), jnp.sin(ang)
    cos2 = jnp.concatenate([cos, cos], axis=-1)
    sin_fwd = jnp.concatenate([-sin, sin], axis=-1)
    return cos2, sin_fwd, -sin_fwd


def kernel(x, positions, w_in, w_pool, pool_scale, w_branch_attn, w_branch_pool, w_out, ln_mix_g, ln_mix_b, w_ff1, w_ff2, ln_ff_g, ln_ff_b, loss_target, m_w_in, m_w_pool, m_pool_scale, m_w_branch_attn, m_w_branch_pool, m_w_out, m_ln_mix_g, m_ln_mix_b, m_w_ff1, m_w_ff2, m_ln_ff_g, m_ln_ff_b, v_w_in, v_w_pool, v_pool_scale, v_w_branch_attn, v_w_branch_pool, v_w_out, v_ln_mix_g, v_ln_mix_b, v_w_ff1, v_w_ff2, v_ln_ff_g, v_ln_ff_b):
    big_w = (w_in[0], w_pool[0], w_branch_attn[0], w_branch_pool[0], w_out[0], w_ff1[0], w_ff2[0])
    big_m = (m_w_in[0], m_w_pool[0], m_w_branch_attn[0], m_w_branch_pool[0], m_w_out[0], m_w_ff1[0], m_w_ff2[0])
    big_v = (v_w_in[0], v_w_pool[0], v_w_branch_attn[0], v_w_branch_pool[0], v_w_out[0], v_w_ff1[0], v_w_ff2[0])
    shard_axes = (1, 1, 1, 1, 0, 1, 0)
    small_w = (pool_scale, ln_mix_g, ln_mix_b, ln_ff_g, ln_ff_b)
    small_m = (m_pool_scale, m_ln_mix_g, m_ln_mix_b, m_ln_ff_g, m_ln_ff_b)
    small_v = (v_pool_scale, v_ln_mix_g, v_ln_mix_b, v_ln_ff_g, v_ln_ff_b)

    names = ("w_in", "w_pool", "w_branch_attn", "w_branch_pool", "w_out", "w_ff1", "w_ff2")
    axis_of = dict(zip(names, shard_axes))
    shard_of = dict(zip(names, [w.astype(BF16) for w in big_w]))

    def full_buffer(n):
        s, ax = shard_of[n], axis_of[n]
        full = list(s.shape)
        full[ax] *= N_DEV
        return lax.empty(tuple(full), s.dtype)

    def gather_group(tag, group, after):
        ex = _TwoLevelGather(tag, [axis_of[n] for n in group], [shard_of[n].shape[axis_of[n]] for n in group])
        return ex, ex.start([shard_of[n] for n in group], [full_buffer(n) for n in group], after)

    in_ex, in_state = gather_group("gather_in", ("w_in",), None)
    mix_ex, mix_state = gather_group("gather_mix", ("w_pool", "w_branch_attn", "w_branch_pool", "w_out"),
                                     in_state[-1])
    ffn_ex, ffn_state = gather_group("gather_ffn", ("w_ff1", "w_ff2"), mix_state[-1])
    states = {"mix": mix_state, "ffn": ffn_state}
    me = 4 * lax.axis_index("x") + 2 * lax.axis_index("y") + lax.axis_index("c")
    block_cols = shard_of["w_in"].shape[1]
    neighbours, diagonal = (2, 4), (6,)

    def project_in(xb):
        n_rope_blocks = 2 * (x.shape[-1] // 2) // block_cols

        def piece(tag, w, blocks, h, **kw):
            return _in_proj_piece("in_proj_" + tag, xb, w, jnp.stack(blocks).astype(jnp.int32), cos2, sin_fwd, h,
                                  block_cols, n_rope_blocks, **kw)

        h = piece("own", shard_of["w_in"], [me], None, own_shard=True, after=ffn_state[-1])
        (w_in_land,), state = in_ex.wait(in_state, h, direct=(1,), tag="_sibling")
        h = piece("sibling", w_in_land, [me ^ 1], h)
        state = in_ex.forward(state, h, neighbours, tag="_neighbours")
        h = piece("neighbours", state[2][0], [me ^ r for r in neighbours], h)
        state = in_ex.forward(state, h, diagonal, tag="_diagonal")
        h = piece("diagonal", state[2][0], [me ^ r for r in diagonal], h)
        states["mix"] = mix_ex.forward(states["mix"], h, neighbours + diagonal)
        (w_in_land,), _ = in_ex.wait(state, h, passed_on=FORWARDED, sends=True, tag="_passed_on")
        h = piece("passed_on", w_in_land, [me ^ f for f in FORWARDED], h)
        return h, w_in_land

    def mix_weights(after):
        states["ffn"] = ffn_ex.forward(states["ffn"], after, neighbours + diagonal)
        return mix_ex.wait(states["mix"], after, direct=(1,), passed_on=FORWARDED, sends=True)[0]

    def ffn_weights(after):
        return ffn_ex.wait(states["ffn"], after, direct=(1,), passed_on=FORWARDED, sends=True)[0]

    groups = {"ff2": ("w_ff2",), "ff1": ("w_ff1",), "out": ("w_out",),
              "branch": ("w_branch_attn", "w_branch_pool"), "in": ("w_in", "w_pool", "small")}
    sent = {}

    def send(key, grads_):
        axes = [axis_of.get(n) for n in groups[key]]
        sizes = [None if ax is None else g.shape[ax] // N_DEV for g, ax in zip(grads_, axes)]
        lands = []
        for g, ax, size in zip(grads_, axes, sizes):
            shard = list(g.shape)
            if ax is not None:
                shard[ax] = size
            lands.append(lax.empty((N_DEV, *shard), g.dtype))
        ex = _scatter_exchange("scatter_" + key, axes, sizes)
        sent[key] = (ex, ex.start(list(grads_), lands))
        return sent[key][1][-1]

    cos2, sin_fwd, sin_bwd = _rope_tables(positions[0])
    loss, grad_x = _local_step(
        x[0], cos2, sin_fwd, sin_bwd, project_in, mix_weights, ffn_weights, pool_scale, ln_mix_g, ln_mix_b,
        ln_ff_g, ln_ff_b, loss_target[0], send)

    state = dict(zip(names, zip(big_w, big_m, big_v)))
    n_small = sum(w.shape[-1] for w in small_w)
    small_2d = (n_small // HEAD_DIM, HEAD_DIM)
    state["small"] = tuple(jnp.concatenate(t, axis=-1).reshape(small_2d) for t in (small_w, small_m, small_v))
    grads, deltas, new_ms, new_vs = {}, {}, {}, {}
    after = grad_x
    for key in ("ff2", "ff1", "out", "branch", "in"):
        ex, started = sent[key]
        for n, part in zip(groups[key], ex.wait(started, after)[0]):
            w, m, v = state[n]
            r2 = (-1, w.shape[-1])
            w2d = w.reshape(r2)
            res = _adamw("adamw_" + n, part.reshape((N_DEV,) + w2d.shape), w2d, m.reshape(r2), v.reshape(r2))
            after = res[0]
            if n == "small":
                small_out = [t.reshape(1, n_small) for t in res]
            else:
                grads[n], deltas[n], new_ms[n], new_vs[n] = (t.reshape((1,) + w.shape) for t in res)
    small_names = ("pool_scale", "ln_mix_g", "ln_mix_b", "ln_ff_g", "ln_ff_b")
    off = 0
    for n, w in zip(small_names, small_w):
        width = w.shape[-1]
        grads[n], deltas[n], new_ms[n], new_vs[n] = (t[:, off:off + width] for t in small_out)
        off += width

    order = ("w_in", "w_pool", "pool_scale", "w_branch_attn", "w_branch_pool", "w_out", "ln_mix_g", "ln_mix_b",
             "w_ff1", "w_ff2", "ln_ff_g", "ln_ff_b")
    total_loss = lax.psum(loss[0, 0], ("x", "y", "c"))
    return (total_loss, grad_x[None], *[grads[n] for n in order], *[deltas[n] for n in order],
            *[new_ms[n] for n in order], *[new_vs[n] for n in order])
```

```python
import functools

import jax
import jax.numpy as jnp
from jax import lax
from jax.experimental import pallas as pl
from jax.experimental.pallas import tpu as pltpu

F32 = jnp.float32
BF16 = jnp.bfloat16

N_DEV = 8
HEAD_DIM = 128
SUB_BLOCK = 128
DILATIONS = (1, 4, 16)
POOL_WINDOWS = (2, 4, 8, 16)
MAX_POOL_WINDOW = 16
POOL_HALO = 128
PERM_ROWS = 256
K_TILE = 1024
LN_EPS = 1e-5
DEEPNORM_ALPHA = 2.0 ** 0.25
ROPE_THETA = 10000.0
ATTN_SCALE = HEAD_DIM ** -0.5
ADAM_LR, ADAM_B1, ADAM_B2, ADAM_EPS, ADAM_WD, ADAM_STEP = 0.001, 0.9, 0.999, 1e-08, 0.01, 10
NEG_BIG = -1e30
VMEM_CAP_V7X = 64 * 1024 * 1024
MESH = pl.DeviceIdType.MESH


def _vmem_limit(est_bytes):
    return int(min(max(est_bytes * 5 // 4 + (4 << 20), 16 << 20), VMEM_CAP_V7X - (6 << 20)))


def _nbytes(shape, dtype):
    n = 1
    for s in shape:
        n *= s
    return n * jnp.dtype(dtype).itemsize


def _mm(name, a, b, form, tiles, outs, epi, extras=(), sequential=False, after=None):
    tm, tn, tk = tiles
    if form == "nn":
        (M, K), (K2, N) = a.shape, b.shape
    elif form == "nt":
        (M, K), (N, K2) = a.shape, b.shape
    else:
        (K, M), (K2, N) = a.shape, b.shape
    assert K == K2, (name, a.shape, b.shape)
    tm, tn, tk = min(tm, M), min(tn, N), min(tk, K)
    assert M % tm == 0 and N % tn == 0 and K % tk == 0, (name, M, N, K, tm, tn, tk)
    grid = (M // tm, N // tn, K // tk)
    nk = grid[2]
    if form == "nn":
        a_spec = pl.BlockSpec((tm, tk), lambda i, j, k: (i, k))
        b_spec = pl.BlockSpec((tk, tn), lambda i, j, k: (k, j))
        contract = ((1,), (0,))
    elif form == "nt":
        a_spec = pl.BlockSpec((tm, tk), lambda i, j, k: (i, k))
        b_spec = pl.BlockSpec((tn, tk), lambda i, j, k: (j, k))
        contract = ((1,), (1,))
    else:
        a_spec = pl.BlockSpec((tk, tm), lambda i, j, k: (k, i))
        b_spec = pl.BlockSpec((tk, tn), lambda i, j, k: (k, j))
        contract = ((0,), (0,))
    n_ex, n_out = len(extras), len(outs)
    n_after = 0 if after is None else 1

    def body(a_ref, b_ref, *rest):
        ex_refs = rest[:n_ex]
        rest = rest[n_ex + n_after:]
        out_refs = rest[:n_out]
        i, j, k = pl.program_id(0), pl.program_id(1), pl.program_id(2)

        def prod():
            return lax.dot_general(a_ref[...].astype(BF16), b_ref[...].astype(BF16),
                                   (contract, ((), ())), preferred_element_type=F32)

        if nk == 1:
            epi(prod(), ex_refs, out_refs, i, j)
        else:
            acc = rest[n_out]

            @pl.when(k == 0)
            def _():
                acc[...] = prod()

            @pl.when(k > 0)
            def _():
                acc[...] += prod()

            @pl.when(k == nk - 1)
            def _():
                epi(acc[...], ex_refs, out_refs, i, j)

    est = 2 * (_nbytes(a_spec.block_shape, a.dtype) + _nbytes(b_spec.block_shape, b.dtype))
    est += sum(2 * _nbytes(bs, arr.dtype) for arr, bs, _ in extras)
    est += sum(2 * _nbytes(bs, dt) for _, dt, bs, _ in outs)
    est += 4 * tm * tn * 4
    sem = ("arbitrary",) * 3 if sequential else ("parallel", "parallel", "arbitrary")
    return pl.pallas_call(
        body, name=name, grid=grid,
        in_specs=([a_spec, b_spec] + [pl.BlockSpec(bs, im) for _, bs, im in extras]
                  + [pl.BlockSpec(memory_space=pl.ANY)] * n_after),
        out_specs=[pl.BlockSpec(bs, im) for _, _, bs, im in outs],
        out_shape=[jax.ShapeDtypeStruct(sh, dt) for sh, dt, _, _ in outs],
        scratch_shapes=[pltpu.VMEM((tm, tn), F32)] if nk > 1 else [],
        compiler_params=pltpu.CompilerParams(dimension_semantics=sem, vmem_limit_bytes=_vmem_limit(est)),
    )(a, b, *[arr for arr, _, _ in extras], *([after] if n_after else []))


def _tile_out(shape, dtype, tm, tn):
    return (shape, dtype, (tm, tn), lambda i, j, k: (i, j))


def _row_sum_out(width):
    return ((1, width), F32, (1, width), lambda i, j, k: (0, 0))


def _accumulate_rows(ref, value, i):
    @pl.when(i == 0)
    def _():
        ref[...] = value

    @pl.when(i > 0)
    def _():
        ref[...] += value


def _layer_norm_bwd(dy, xhat, rstd, g):
    dxh = dy * g
    m1 = jnp.mean(dxh, axis=-1, keepdims=True)
    m2 = jnp.mean(dxh * xhat, axis=-1, keepdims=True)
    return rstd * (dxh - m1 - xhat * m2)


def _rope_apply(t, cos2, sin_signed):
    return t * cos2 + pltpu.roll(t, HEAD_DIM // 2, axis=1) * sin_signed


def _in_proj_piece(name, xb, w_in, col_blocks, cos2, sin_fwd, h_so_far, block_cols, n_rope_blocks, own_shard=False,
                   after=None):
    S, D = xb.shape
    W = w_in.shape[1] * (N_DEV if own_shard else 1)
    tm = min(1024, S)
    n_blocks = col_blocks.shape[0]

    def body(cols_ref, x_ref, w_ref, cos_ref, sin_ref, *rest):
        h_ref = rest[-1]
        j = pl.program_id(1)

        @pl.when(cols_ref[j] < n_rope_blocks)
        def _():
            acc = _dot_nn(x_ref[...], w_ref[...])
            c, s = cos_ref[...], sin_ref[...]
            for hd in range(block_cols // HEAD_DIM):
                sl = slice(hd * HEAD_DIM, (hd + 1) * HEAD_DIM)
                h_ref[:, sl] = _rope_apply(acc[:, sl], c, s).astype(BF16)

        @pl.when(cols_ref[j] >= n_rope_blocks)
        def _():
            h_ref[...] = _dot_nn(x_ref[...], w_ref[...]).astype(BF16)

    row = pl.BlockSpec((tm, HEAD_DIM), lambda i, j, cols: (i, 0))
    carried = ([] if h_so_far is None else [h_so_far]) + ([] if after is None else [after])
    est = 2 * (tm * D * 2 + D * block_cols * 2 + tm * block_cols * 2 + 2 * tm * HEAD_DIM * 4) + 3 * tm * block_cols * 4
    return pl.pallas_call(
        body, name=name,
        grid_spec=pltpu.PrefetchScalarGridSpec(
            num_scalar_prefetch=1, grid=(S // tm, n_blocks),
            in_specs=[pl.BlockSpec((tm, D), lambda i, j, cols: (i, 0)),
                      pl.BlockSpec((D, block_cols), lambda i, j, cols: (0, 0 if own_shard else cols[j])), row, row]
                     + [pl.BlockSpec(memory_space=pl.ANY)] * len(carried),
            out_specs=pl.BlockSpec((tm, block_cols), lambda i, j, cols: (i, cols[j]))),
        out_shape=jax.ShapeDtypeStruct((S, W), BF16),
        input_output_aliases={} if h_so_far is None else {5: 0},
        compiler_params=pltpu.CompilerParams(dimension_semantics=("parallel", "arbitrary"),
                                             vmem_limit_bytes=_vmem_limit(est)),
    )(col_blocks, xb, w_in, cos2, sin_fwd, *carried)


def _attn_mask(mb):
    qi = lax.broadcasted_iota(jnp.int32, (SUB_BLOCK, 2 * SUB_BLOCK), 0)
    kj = lax.broadcasted_iota(jnp.int32, (SUB_BLOCK, 2 * SUB_BLOCK), 1)
    prev = jnp.logical_and(jnp.logical_and(kj < SUB_BLOCK, kj >= qi), mb > 0)
    cur = jnp.logical_and(kj >= SUB_BLOCK, kj - SUB_BLOCK <= qi)
    return jnp.logical_or(prev, cur)


def _both_blocks(prev_ref, cur_ref, sl):
    return jnp.concatenate([prev_ref[:, sl], cur_ref[:, sl]], axis=0)


def _dot_nt(a, b):
    return lax.dot_general(a, b, (((1,), (1,)), ((), ())), preferred_element_type=F32)


def _dot_tn(a, b):
    return lax.dot_general(a, b, (((0,), (0,)), ((), ())), preferred_element_type=F32)


def _dot_nn(a, b):
    return lax.dot_general(a, b, (((1,), (0,)), ((), ())), preferred_element_type=F32)


def _perm_matrix(d, to_residue_major):
    g = PERM_ROWS // d
    i = lax.broadcasted_iota(jnp.int32, (PERM_ROWS, PERM_ROWS), 0)
    j = lax.broadcasted_iota(jnp.int32, (PERM_ROWS, PERM_ROWS), 1)
    if to_residue_major:
        hit = j == (i % g) * d + i // g
    else:
        hit = j == (i % d) * g + i // d
    return hit.astype(BF16)


def _permute_rows(perm, x, terms=3):
    if x.dtype == BF16:
        return _dot_nn(perm, x)
    hi = x.astype(BF16)
    r1 = x - hi.astype(F32)
    mid = r1.astype(BF16)
    out = _dot_nn(perm, hi) + _dot_nn(perm, mid)
    if terms == 3:
        out = out + _dot_nn(perm, (r1 - mid.astype(F32)).astype(BF16))
    return out


def _rm_block(d, width):
    return pl.BlockSpec((d, PERM_ROWS // d, width), lambda i: (0, i, 0))


def _to_residue_major(name, x, col_block, width):
    S = x.shape[0]
    dils = [d for d in DILATIONS if d > 1]
    chunk = min(width, 1024)

    def body(x_ref, *out_refs):
        for d, o_ref in zip(dils, out_refs):
            perm = _perm_matrix(d, True)
            for c0 in range(0, width, chunk):
                cw = min(chunk, width - c0)
                y = _permute_rows(perm, x_ref[:, c0:c0 + cw])
                o_ref[:, :, c0:c0 + cw] = y.astype(x.dtype).reshape(d, PERM_ROWS // d, cw)

    return pl.pallas_call(
        body, name=name, grid=(S // PERM_ROWS,),
        in_specs=[pl.BlockSpec((PERM_ROWS, width), lambda i: (i, col_block))],
        out_specs=[_rm_block(d, width) for d in dils],
        out_shape=[jax.ShapeDtypeStruct((d, S // d, width), x.dtype) for d in dils],
        compiler_params=pltpu.CompilerParams(dimension_semantics=("parallel",),
                                             vmem_limit_bytes=_vmem_limit(32 << 20)),
    )(x)


def _qkv_specs(aw):
    def spec(col, prev):
        if prev:
            return pl.BlockSpec((None, SUB_BLOCK, aw), lambda r, mb: (r, jnp.maximum(mb - 1, 0), col))
        return pl.BlockSpec((None, SUB_BLOCK, aw), lambda r, mb: (r, mb, col))
    return [spec(0, False), spec(1, True), spec(1, False), spec(2, True), spec(2, False)]


def _put_column(tile, col, value):
    lane = lax.broadcasted_iota(jnp.int32, tile.shape, 1)
    return jnp.where(lane == col, value, tile)


def _attn_fwd(qkv, d, aw):
    _, rows, _ = qkv.shape
    n_heads = aw // HEAD_DIM
    nb = rows // SUB_BLOCK

    def body(q_ref, kp_ref, kc_ref, vp_ref, vc_ref, o_ref, lse_ref, s_buf, p_buf):
        mask = _attn_mask(pl.program_id(1))
        for hd in range(n_heads):
            sl = slice(hd * HEAD_DIM, (hd + 1) * HEAD_DIM)
            s_buf[hd] = _dot_nt(q_ref[:, sl], _both_blocks(kp_ref, kc_ref, sl))
        lse_tile = jnp.zeros((SUB_BLOCK, HEAD_DIM), F32)
        inv_tile = jnp.zeros((SUB_BLOCK, HEAD_DIM), F32)
        for hd in range(n_heads):
            s = jnp.where(mask, s_buf[hd] * ATTN_SCALE, NEG_BIG)
            m = jnp.max(s, axis=-1, keepdims=True)
            p = jnp.exp(s - m)
            l = jnp.sum(p, axis=-1, keepdims=True)
            p_buf[hd] = p.astype(BF16)
            lse_tile = _put_column(lse_tile, hd, m + jnp.log(l))
            inv_tile = _put_column(inv_tile, hd, 1.0 / l)
        lse_ref[...] = lse_tile
        for hd in range(n_heads):
            sl = slice(hd * HEAD_DIM, (hd + 1) * HEAD_DIM)
            o = _dot_nn(p_buf[hd], _both_blocks(vp_ref, vc_ref, sl))
            o_ref[:, sl] = o * inv_tile[:, hd:hd + 1]

    return pl.pallas_call(
        body, name=f"attn_fwd_d{d}", grid=(d, nb),
        in_specs=_qkv_specs(aw),
        out_specs=[pl.BlockSpec((None, SUB_BLOCK, aw), lambda r, mb: (r, mb, 0)),
                   pl.BlockSpec((None, SUB_BLOCK, HEAD_DIM), lambda r, mb: (r, mb, 0))],
        out_shape=[jax.ShapeDtypeStruct((d, rows, aw), F32), jax.ShapeDtypeStruct((d, rows, HEAD_DIM), F32)],
        scratch_shapes=[pltpu.VMEM((n_heads, SUB_BLOCK, 2 * SUB_BLOCK), F32),
                        pltpu.VMEM((n_heads, SUB_BLOCK, 2 * SUB_BLOCK), BF16)],
        compiler_params=pltpu.CompilerParams(dimension_semantics=("parallel", "parallel"),
                                             vmem_limit_bytes=_vmem_limit(16 << 20)),
    )(qkv, qkv, qkv, qkv, qkv)


def _attn_combine(outs, lses, aw):
    S = outs[0].shape[1]
    n_heads = aw // HEAD_DIM
    n_pat = len(DILATIONS)

    def body(*refs):
        o_refs, l_refs = refs[:n_pat], refs[n_pat:2 * n_pat]
        o_ref, lt_ref = refs[2 * n_pat], refs[2 * n_pat + 1]
        o_nat, l_nat = [], []
        for d, o_r, l_r in zip(DILATIONS, o_refs, l_refs):
            o_p = o_r[...].reshape(PERM_ROWS, aw)
            l_p = l_r[...].reshape(PERM_ROWS, HEAD_DIM)
            if d > 1:
                perm = _perm_matrix(d, False)
                o_p, l_p = _permute_rows(perm, o_p, terms=2), _permute_rows(perm, l_p)
            o_nat.append(o_p)
            l_nat.append(l_p)
        mx = functools.reduce(jnp.maximum, l_nat)
        es = [jnp.exp(l_p - mx) for l_p in l_nat]
        den = functools.reduce(jnp.add, es)
        lt_ref[...] = mx + jnp.log(den)
        ws = [e / den for e in es]
        for hd in range(n_heads):
            sl = slice(hd * HEAD_DIM, (hd + 1) * HEAD_DIM)
            o = ws[0][:, hd:hd + 1] * o_nat[0][:, sl]
            for pi in range(1, n_pat):
                o = o + ws[pi][:, hd:hd + 1] * o_nat[pi][:, sl]
            o_ref[:, sl] = o.astype(BF16)

    return pl.pallas_call(
        body, name="attn_combine", grid=(S // PERM_ROWS,),
        in_specs=[_rm_block(d, aw) for d in DILATIONS] + [_rm_block(d, HEAD_DIM) for d in DILATIONS],
        out_specs=[pl.BlockSpec((PERM_ROWS, aw), lambda i: (i, 0)), pl.BlockSpec((PERM_ROWS, HEAD_DIM), lambda i: (i, 0))],
        out_shape=[jax.ShapeDtypeStruct((S, aw), BF16), jax.ShapeDtypeStruct((S, HEAD_DIM), F32)],
        compiler_params=pltpu.CompilerParams(dimension_semantics=("parallel",),
                                             vmem_limit_bytes=_vmem_limit(40 << 20)),
    )(*outs, *lses)


def _band(tm, width, w, row_offset, transpose):
    t = lax.broadcasted_iota(jnp.int32, (tm, width), 0)
    u = lax.broadcasted_iota(jnp.int32, (tm, width), 1)
    dist = (u - t - row_offset) if transpose else (t + row_offset - u)
    return jnp.logical_and(dist >= 0, dist < w).astype(BF16)


def _pool_fwd(h, w_pool, pool_scale, pw, u_col_block):
    S, W = h.shape
    n_groups = len(POOL_WINDOWS)
    gw = pw // n_groups
    tm = min(512, S)
    halo_per_tile = tm // POOL_HALO

    def body(uc_ref, uh_ref, w_ref, sc_ref, p_ref, y_ref, pm_ref):
        i = pl.program_id(0)
        t_abs = i * tm + lax.broadcasted_iota(jnp.int32, (tm, 1), 0)
        for g, w in enumerate(POOL_WINDOWS):
            sl = slice(g * gw, (g + 1) * gw)
            uc = uc_ref[:, sl]
            uh = jnp.where(i > 0, uh_ref[:, sl], jnp.zeros((POOL_HALO, gw), BF16))
            ssum = _dot_nn(_band(tm, tm, w, 0, False), uc) + _dot_nn(_band(tm, POOL_HALO, w, POOL_HALO, False), uh)
            cnt = jnp.minimum(t_abs + 1, w).astype(F32)
            p = (ssum / cnt - uc.astype(F32)).astype(BF16)
            y = _dot_nn(p, w_ref[g])
            p_ref[:, sl] = p
            y_ref[:, sl] = y.astype(BF16)
            pm_ref[:, sl] = (y * sc_ref[:, sl]).astype(BF16)

    row = pl.BlockSpec((tm, pw), lambda i: (i, 0))
    return pl.pallas_call(
        body, name="pool_fwd", grid=(S // tm,),
        in_specs=[pl.BlockSpec((tm, pw), lambda i: (i, u_col_block)),
                  pl.BlockSpec((POOL_HALO, pw), lambda i: (jnp.maximum(i * halo_per_tile - 1, 0), u_col_block)),
                  pl.BlockSpec((n_groups, gw, gw), lambda i: (0, 0, 0)),
                  pl.BlockSpec((1, pw), lambda i: (0, 0))],
        out_specs=[row, row, row],
        out_shape=[jax.ShapeDtypeStruct((S, pw), BF16)] * 3,
        compiler_params=pltpu.CompilerParams(dimension_semantics=("parallel",),
                                             vmem_limit_bytes=_vmem_limit(24 << 20)),
    )(h, h, w_pool, pool_scale)


def _branch_attn(o_attn, w_ba):
    S, _ = o_attn.shape
    D = w_ba.shape[1]
    tm, tn = min(1024, S), D

    def epi(acc, ex, out, i, j):
        out[0][...] = acc.astype(BF16)

    (y,) = _mm("branch_attn", o_attn, w_ba, "nn", (tm, tn, 1024), [_tile_out((S, D), BF16, tm, tn)], epi)
    return y


def _branch_pool_merge(pm, w_bp, h, y_attn, gate_col0):
    S, _ = pm.shape
    D = w_bp.shape[1]
    tm, tn = min(512, S), D
    ga0, gp0 = gate_col0 // tn, (gate_col0 + D) // tn

    def epi(acc, ex, out, i, j):
        ga_ref, gp_ref, ya_ref = ex
        yp_ref, mg_ref = out
        yp = acc.astype(BF16)
        yp_ref[...] = yp
        mg = (jax.nn.sigmoid(ga_ref[...]).astype(F32) * ya_ref[...].astype(F32)
              + jax.nn.sigmoid(gp_ref[...]).astype(F32) * acc)
        mg_ref[...] = mg.astype(BF16)

    y_pool, merged = _mm(
        "branch_pool_merge", pm, w_bp, "nn", (tm, tn, 1024),
        [_tile_out((S, D), BF16, tm, tn), _tile_out((S, D), BF16, tm, tn)], epi,
        extras=[(h, (tm, tn), lambda i, j, k: (i, ga0 + j)), (h, (tm, tn), lambda i, j, k: (i, gp0 + j)),
                (y_attn, (tm, tn), lambda i, j, k: (i, j))])
    return y_pool, merged


def _layer_norm_rows(z, g, b):
    mu = jnp.mean(z, axis=-1, keepdims=True)
    zc = z - mu
    var = jnp.mean(zc * zc, axis=-1, keepdims=True)
    rstd = lax.rsqrt(var + LN_EPS)
    xhat = zc * rstd
    return xhat * g + b, xhat, rstd


def _out_proj_ln(merged, w_out, x, g, b):
    S, D = x.shape
    tm = min(256, S)

    def epi(acc, ex, out, i, j):
        x_ref, g_ref, b_ref = ex
        x1_ref, x1b_ref, xh_ref, rs_ref = out
        y, xhat, rstd = _layer_norm_rows(DEEPNORM_ALPHA * x_ref[...] + acc, g_ref[...], b_ref[...])
        x1_ref[...] = y
        x1b_ref[...] = y.astype(BF16)
        xh_ref[...] = xhat
        rs_ref[...] = jnp.broadcast_to(rstd, (tm, HEAD_DIM))

    row = lambda i, j, k: (i, 0)
    vec = lambda i, j, k: (0, 0)
    return _mm("out_proj_ln", merged, w_out, "nn", (tm, D, D),
               [((S, D), F32, (tm, D), row), ((S, D), BF16, (tm, D), row), ((S, D), F32, (tm, D), row),
                ((S, HEAD_DIM), F32, (tm, HEAD_DIM), row)], epi,
               extras=[(x, (tm, D), row), (g, (1, D), vec), (b, (1, D), vec)])


def _ffn_up(x1b, w1):
    S, D = x1b.shape
    F = w1.shape[1]
    tm, tn = min(1024, S), min(2048, F)

    def epi(acc, ex, out, i, j):
        r = jnp.maximum(acc, 0.0)
        out[0][...] = (r * r).astype(BF16)

    (a,) = _mm("ffn_up", x1b, w1, "nn", (tm, tn, 2 * K_TILE), [_tile_out((S, F), BF16, tm, tn)], epi)
    return a


def _residual_matmul(name, a, w, form, resid, after=None):
    S, D = resid.shape
    tm, tn = min(1024, S), min(2048, D)

    def epi(acc, ex, out, i, j):
        out[0][...] = DEEPNORM_ALPHA * ex[0][...] + acc

    (z,) = _mm(name, a, w, form, (tm, tn, K_TILE), [_tile_out((S, D), F32, tm, tn)], epi,
               extras=[(resid, (tm, tn), lambda i, j, k: (i, j))], after=after)
    return z


def _row_kernel(name, body, row_inputs, vec_inputs, row_outputs, sum_widths, tr):
    S = row_inputs[0].shape[0]
    row = lambda w: pl.BlockSpec((tr, w), lambda i: (i, 0))
    vec = lambda w: pl.BlockSpec((1, w), lambda i: (0, 0))

    def wrapped(*refs):
        body(pl.program_id(0), *refs)

    return pl.pallas_call(
        wrapped, name=name, grid=(S // tr,),
        in_specs=[row(t.shape[1]) for t in row_inputs] + [vec(t.shape[1]) for t in vec_inputs],
        out_specs=[row(w) for w, _ in row_outputs] + [vec(w) for w in sum_widths],
        out_shape=([jax.ShapeDtypeStruct((S, w), dt) for w, dt in row_outputs]
                   + [jax.ShapeDtypeStruct((1, w), F32) for w in sum_widths]),
        compiler_params=pltpu.CompilerParams(dimension_semantics=("arbitrary",),
                                             vmem_limit_bytes=_vmem_limit(40 << 20)),
    )(*row_inputs, *vec_inputs)


def _ln_loss_bwd(z2, g, b, target):
    S, D = z2.shape

    def body(i, z_ref, t_ref, g_ref, b_ref, dz_ref, dzb_ref, dg_ref, db_ref, loss_ref):
        gv = g_ref[...]
        y, xhat, rstd = _layer_norm_rows(z_ref[...], gv, b_ref[...])
        err = y - t_ref[...]
        loss = 0.5 * jnp.sum(jnp.mean(err * err, axis=-1, keepdims=True), axis=0, keepdims=True)
        dy = err * (1.0 / D)
        dz = _layer_norm_bwd(dy, xhat, rstd, gv)
        dz_ref[...] = dz
        dzb_ref[...] = dz.astype(BF16)
        _accumulate_rows(dg_ref, jnp.sum(dy * xhat, axis=0, keepdims=True), i)
        _accumulate_rows(db_ref, jnp.sum(dy, axis=0, keepdims=True), i)
        _accumulate_rows(loss_ref, jnp.broadcast_to(loss, (1, HEAD_DIM)), i)

    return _row_kernel("ln_loss_bwd", body, [z2, target], [g, b], [(D, F32), (D, BF16)], [D, D, HEAD_DIM],
                       min(256, S))


def _ln_bwd(dy, xhat, rstd, g):
    S, D = dy.shape

    def body(i, dy_ref, xh_ref, rs_ref, g_ref, dz_ref, dzb_ref, dg_ref, db_ref):
        dyv, xhat_v = dy_ref[...], xh_ref[...]
        dz = _layer_norm_bwd(dyv, xhat_v, rs_ref[:, :1], g_ref[...])
        dz_ref[...] = dz
        dzb_ref[...] = dz.astype(BF16)
        _accumulate_rows(dg_ref, jnp.sum(dyv * xhat_v, axis=0, keepdims=True), i)
        _accumulate_rows(db_ref, jnp.sum(dyv, axis=0, keepdims=True), i)

    return _row_kernel("ln_bwd", body, [dy, xhat, rstd], [g], [(D, F32), (D, BF16)], [D, D], min(256, S))


def _grad_weight(name, act, cot):
    M, N = act.shape[1], cot.shape[1]
    tm, tn = min(1024, M), min(2048, N)

    def epi(acc, ex, out, i, j):
        out[0][...] = acc.astype(BF16)

    (g,) = _mm(name, act, cot, "tn", (tm, tn, 2 * K_TILE), [_tile_out((M, N), BF16, tm, tn)], epi)
    return g


def _ffn_down_bwd(dz2b, w2, a, after=None):
    S, D = dz2b.shape
    F = w2.shape[0]
    tm, tn = min(1024, S), min(2048, F)

    def epi(acc, ex, out, i, j):
        out[0][...] = (acc * (2.0 * jnp.sqrt(ex[0][...])).astype(F32)).astype(BF16)

    (dh1,) = _mm("ffn_down_bwd", dz2b, w2, "nt", (tm, tn, 2 * K_TILE), [_tile_out((S, F), BF16, tm, tn)], epi,
                 extras=[(a, (tm, tn), lambda i, j, k: (i, j))], after=after)
    return dh1


def _out_proj_bwd(dz1b, w_out, h, y_attn, y_pool, gate_col0, after=None):
    S, D = dz1b.shape
    W = h.shape[1]
    tm = min(256, S)
    assert gate_col0 == 2 * D and W == 4 * D

    def epi(acc, ex, out, i, j):
        gates_ref, ya_ref, yp_ref = ex
        dya_ref, dyp_ref, dh_ref = out
        sa = jax.nn.sigmoid(gates_ref[:, :D]).astype(F32)
        sp = jax.nn.sigmoid(gates_ref[:, D:]).astype(F32)
        dya_ref[...] = (acc * sa).astype(BF16)
        dyp_ref[...] = (acc * sp).astype(BF16)
        dh_ref[:, :D] = (acc * ya_ref[...].astype(F32) * (sa * (1.0 - sa))).astype(BF16)
        dh_ref[:, D:] = (acc * yp_ref[...].astype(F32) * (sp * (1.0 - sp))).astype(BF16)

    row = lambda i, j, k: (i, 0)
    return _mm("out_proj_bwd", dz1b, w_out, "nt", (tm, D, D),
               [((S, D), BF16, (tm, D), row), ((S, D), BF16, (tm, D), row),
                ((S, W), BF16, (tm, 2 * D), lambda i, j, k: (i, 1))], epi,
               extras=[(h, (tm, 2 * D), lambda i, j, k: (i, 1)), (y_attn, (tm, D), row), (y_pool, (tm, D), row)],
               after=after)


def _branch_attn_bwd(dy_attn, w_ba, o_attn, l_tot, after=None):
    S, D = dy_attn.shape
    aw = w_ba.shape[0]
    n_heads = aw // HEAD_DIM
    tm = min(512, S)

    def epi(acc, ex, out, i, j):
        do_ref, st_ref = out
        do_ref[...] = acc.astype(BF16)
        o = ex[0][...].astype(F32)
        stats = ex[1][...]
        for hd in range(n_heads):
            sl = slice(hd * HEAD_DIM, (hd + 1) * HEAD_DIM)
            stats = _put_column(stats, n_heads + hd, jnp.sum(acc[:, sl] * o[:, sl], axis=-1, keepdims=True))
        st_ref[...] = stats

    row = lambda i, j, k: (i, 0)
    return _mm("branch_attn_bwd", dy_attn, w_ba, "nt", (tm, aw, D),
               [((S, aw), BF16, (tm, aw), row), ((S, HEAD_DIM), F32, (tm, HEAD_DIM), row)], epi,
               extras=[(o_attn, (tm, aw), row), (l_tot, (tm, HEAD_DIM), row)], after=after)


def _branch_pool_bwd(dy_pool, w_bp, y_pre, pool_scale):
    S, D = dy_pool.shape
    pw = w_bp.shape[0]
    tm = min(512, S)

    def epi(acc, ex, out, i, j):
        y_ref, sc_ref = ex
        dyp_ref, dsc_ref = out
        dyp_ref[...] = (acc * sc_ref[...]).astype(BF16)
        _accumulate_rows(dsc_ref, jnp.sum(acc * y_ref[...].astype(F32), axis=0, keepdims=True), i)

    row = lambda i, j, k: (i, 0)
    return _mm("branch_pool_bwd", dy_pool, w_bp, "nt", (tm, pw, D),
               [((S, pw), BF16, (tm, pw), row), _row_sum_out(pw)], epi,
               extras=[(y_pre, (tm, pw), row), (pool_scale, (1, pw), lambda i, j, k: (0, 0))],
               sequential=True)


def _pool_bwd(dh, dy_pre, p, w_pool, pw, u_col_block):
    S, W = dh.shape
    n_groups = len(POOL_WINDOWS)
    gw = pw // n_groups
    tm = min(512, S)
    n_tiles = S // tm
    halo_per_tile = tm // POOL_HALO
    n_halo_blocks = S // POOL_HALO

    def body(dh_in_ref, dyc_ref, dyh_ref, p_ref, w_ref, dh_ref, dw_ref):
        del dh_in_ref
        i = pl.program_id(0)
        t_cur = i * tm + lax.broadcasted_iota(jnp.int32, (tm, 1), 0)
        t_halo = (i + 1) * tm + lax.broadcasted_iota(jnp.int32, (POOL_HALO, 1), 0)
        for g, w in enumerate(POOL_WINDOWS):
            sl = slice(g * gw, (g + 1) * gw)
            wg = w_ref[g]
            dyc = dyc_ref[:, sl]
            dyh = jnp.where(i < n_tiles - 1, dyh_ref[:, sl], jnp.zeros((POOL_HALO, gw), BF16))
            dp_cur = _dot_nt(dyc, wg)
            dp_halo = _dot_nt(dyh, wg)
            dpc_cur = (dp_cur / jnp.minimum(t_cur + 1, w).astype(F32)).astype(BF16)
            dpc_halo = (dp_halo / jnp.minimum(t_halo + 1, w).astype(F32)).astype(BF16)
            du = (_dot_nn(_band(tm, tm, w, 0, True), dpc_cur)
                  + _dot_nn(_band(tm, POOL_HALO, w, -tm, True), dpc_halo) - dp_cur)
            dh_ref[:, sl] = du.astype(BF16)
            dw = _dot_tn(p_ref[:, sl], dyc)

            @pl.when(i == 0)
            def _():
                dw_ref[g] = dw

            @pl.when(i > 0)
            def _():
                dw_ref[g] += dw

    row = pl.BlockSpec((tm, pw), lambda i: (i, 0))
    dh_new, dw_pool = pl.pallas_call(
        body, name="pool_bwd", grid=(n_tiles,),
        in_specs=[pl.BlockSpec(memory_space=pl.ANY), row,
                  pl.BlockSpec((POOL_HALO, pw), lambda i: (jnp.minimum((i + 1) * halo_per_tile, n_halo_blocks - 1), 0)),
                  row, pl.BlockSpec((n_groups, gw, gw), lambda i: (0, 0, 0))],
        out_specs=[pl.BlockSpec((tm, pw), lambda i: (i, u_col_block)),
                   pl.BlockSpec((n_groups, gw, gw), lambda i: (0, 0, 0))],
        out_shape=[jax.ShapeDtypeStruct((S, W), BF16), jax.ShapeDtypeStruct((n_groups, gw, gw), F32)],
        input_output_aliases={0: 0},
        compiler_params=pltpu.CompilerParams(dimension_semantics=("arbitrary",),
                                             vmem_limit_bytes=_vmem_limit(24 << 20)),
    )(dh, dy_pre, dy_pre, p, w_pool)
    return dh_new, dw_pool


def _attn_bwd(qkv, d_out, stats, d, aw):
    _, rows, _ = qkv.shape
    n_heads = aw // HEAD_DIM
    nb = rows // SUB_BLOCK
    n_blocks = d * nb

    def body(q_ref, kp_ref, kc_ref, vp_ref, vc_ref, do_ref, st_ref, dq_ref, dk_ref, dv_ref,
             carry_k, carry_v, s_buf, dp_buf, p_buf, ds_buf):
        step = pl.program_id(0)

        @pl.when(step == 0)
        def _():
            carry_k[...] = jnp.zeros_like(carry_k)
            carry_v[...] = jnp.zeros_like(carry_v)

        @pl.when(step < n_blocks)
        def _():
            mask = _attn_mask(step % nb)
            st = st_ref[...]
            for hd in range(n_heads):
                sl = slice(hd * HEAD_DIM, (hd + 1) * HEAD_DIM)
                s_buf[hd] = _dot_nt(q_ref[:, sl], _both_blocks(kp_ref, kc_ref, sl))
                dp_buf[hd] = _dot_nt(do_ref[:, sl], _both_blocks(vp_ref, vc_ref, sl))
            for hd in range(n_heads):
                lt, dl = st[:, hd:hd + 1], st[:, n_heads + hd:n_heads + hd + 1]
                p = jnp.where(mask, jnp.exp(jnp.where(mask, s_buf[hd] * ATTN_SCALE - lt, NEG_BIG)), 0.0)
                p_buf[hd] = p.astype(BF16)
                ds_buf[hd] = (p * (dp_buf[hd] - dl) * ATTN_SCALE).astype(BF16)
            for hd in range(n_heads):
                sl = slice(hd * HEAD_DIM, (hd + 1) * HEAD_DIM)
                dq_ref[:, sl] = _dot_nn(ds_buf[hd], _both_blocks(kp_ref, kc_ref, sl)).astype(BF16)
                dk_both = _dot_tn(ds_buf[hd], q_ref[:, sl])
                dv_both = _dot_tn(p_buf[hd], do_ref[:, sl])
                dk_ref[:, sl] = (carry_k[:, sl] + dk_both[:SUB_BLOCK]).astype(BF16)
                dv_ref[:, sl] = (carry_v[:, sl] + dv_both[:SUB_BLOCK]).astype(BF16)
                carry_k[:, sl] = dk_both[SUB_BLOCK:]
                carry_v[:, sl] = dv_both[SUB_BLOCK:]

        @pl.when(step == n_blocks)
        def _():
            dk_ref[...] = carry_k[...].astype(BF16)
            dv_ref[...] = carry_v[...].astype(BF16)

    def cur(step):
        return jnp.minimum(step, n_blocks - 1)

    def qkv_spec(col, prev):
        if prev:
            return pl.BlockSpec((SUB_BLOCK, aw), lambda s: (jnp.maximum(cur(s) - 1, 0), col))
        return pl.BlockSpec((SUB_BLOCK, aw), lambda s: (cur(s), col))

    def at_cur(w):
        return pl.BlockSpec((SUB_BLOCK, w), lambda s: (cur(s), 0))

    finished = pl.BlockSpec((SUB_BLOCK, aw), lambda s: (jnp.maximum(s - 1, 0), 0))
    pair = (n_heads, SUB_BLOCK, 2 * SUB_BLOCK)
    flat = lambda t: t.reshape(d * rows, t.shape[-1])
    qkv2 = flat(qkv)
    outs = pl.pallas_call(
        body, name=f"attn_bwd_d{d}", grid=(n_blocks + 1,),
        in_specs=[qkv_spec(0, False), qkv_spec(1, True), qkv_spec(1, False), qkv_spec(2, True), qkv_spec(2, False),
                  at_cur(aw), at_cur(HEAD_DIM)],
        out_specs=[at_cur(aw), finished, finished],
        out_shape=[jax.ShapeDtypeStruct((d * rows, aw), BF16)] * 3,
        scratch_shapes=[pltpu.VMEM((SUB_BLOCK, aw), F32), pltpu.VMEM((SUB_BLOCK, aw), F32),
                        pltpu.VMEM(pair, F32), pltpu.VMEM(pair, F32), pltpu.VMEM(pair, BF16), pltpu.VMEM(pair, BF16)],
        compiler_params=pltpu.CompilerParams(dimension_semantics=("arbitrary",),
                                             vmem_limit_bytes=_vmem_limit(24 << 20)),
    )(qkv2, qkv2, qkv2, qkv2, qkv2, flat(d_out), flat(stats))
    return [t.reshape(d, rows, aw) for t in outs]


def _attn_bwd_finish(dh, per_pattern, cos2, sin_bwd, aw):
    S, W = dh.shape
    n_heads = aw // HEAD_DIM
    n_pat = len(DILATIONS)

    def body(*refs):
        grad_refs = refs[1:1 + 3 * n_pat]
        cos_ref, sin_ref = refs[1 + 3 * n_pat], refs[2 + 3 * n_pat]
        out_ref = refs[3 + 3 * n_pat]
        perms = {d: _perm_matrix(d, False) for d in DILATIONS if d > 1}
        totals = []
        for which in range(3):
            tot = None
            for pi, d in enumerate(DILATIONS):
                g = grad_refs[which * n_pat + pi][...].reshape(PERM_ROWS, aw)
                g = _permute_rows(perms[d], g) if d > 1 else g.astype(F32)
                tot = g if tot is None else tot + g
            totals.append(tot)
        dq, dk, dv = totals
        c, s = cos_ref[...], sin_ref[...]
        for hd in range(n_heads):
            sl = slice(hd * HEAD_DIM, (hd + 1) * HEAD_DIM)
            out_ref[:, sl] = _rope_apply(dq[:, sl], c, s).astype(BF16)
            out_ref[:, aw + hd * HEAD_DIM:aw + (hd + 1) * HEAD_DIM] = _rope_apply(dk[:, sl], c, s).astype(BF16)
        out_ref[:, 2 * aw:] = dv.astype(BF16)

    grads = [pp[which] for which in range(3) for pp in per_pattern]
    rope_spec = pl.BlockSpec((PERM_ROWS, HEAD_DIM), lambda i: (i, 0))
    return pl.pallas_call(
        body, name="attn_bwd_finish", grid=(S // PERM_ROWS,),
        in_specs=([pl.BlockSpec(memory_space=pl.ANY)] + [_rm_block(d, aw) for d in DILATIONS] * 3
                  + [rope_spec, rope_spec]),
        out_specs=pl.BlockSpec((PERM_ROWS, 3 * aw), lambda i: (i, 0)),
        out_shape=jax.ShapeDtypeStruct((S, W), BF16),
        input_output_aliases={0: 0},
        compiler_params=pltpu.CompilerParams(dimension_semantics=("parallel",),
                                             vmem_limit_bytes=_vmem_limit(32 << 20)),
    )(dh, *grads, cos2, sin_bwd)


def _my_place():
    x, y, c = lax.axis_index("x"), lax.axis_index("y"), lax.axis_index("c")
    return x, y, c


def _flat(px, py, pc):
    return 4 * px + 2 * py + pc


def _shard_slice(ref, axis, idx, size):
    start = pl.multiple_of(idx * size, size)
    ix = [slice(None)] * len(ref.shape)
    ix[axis] = pl.ds(start, size)
    return ref.at[tuple(ix)]


_HBM_SPEC = pl.BlockSpec(memory_space=pltpu.HBM)
_SEM_SPEC = pl.BlockSpec(memory_space=pltpu.SEMAPHORE)
_ANY_SPEC = pl.BlockSpec(memory_space=pl.ANY)
_N_PEER = N_DEV - 1
SIBLING, SAME_CORE_NEIGHBOURS, OTHER_CORE_NEIGHBOURS, DIAGONAL = (1,), (2, 4), (3, 5), (6, 7)
PEER_ORDER = SIBLING + SAME_CORE_NEIGHBOURS + OTHER_CORE_NEIGHBOURS + DIAGONAL


def _peer_of(x, y, c, r):
    return (x ^ ((r >> 2) & 1), y ^ ((r >> 1) & 1), c ^ (r & 1))


class _Exchange:
    def __init__(self, name, part, slot):
        self.name, self.part, self.slot = name, part, slot

    def _copy(self, w, r, src, land, send_sems, recv_sems, sending):
        x, y, c = _my_place()
        peer = _peer_of(x, y, c, r)
        return pltpu.make_async_remote_copy(
            src_ref=self.part(w, src, _flat(*peer)),
            dst_ref=self.slot(w, land, _flat(x, y, c) if sending else _flat(*peer)),
            send_sem=send_sems.at[w * _N_PEER + r - 1], recv_sem=recv_sems.at[w * _N_PEER + r - 1],
            device_id=peer, device_id_type=MESH)

    def start(self, srcs, lands, after=None):
        n = len(srcs)
        n_after = 0 if after is None else 1

        def body(*refs):
            src, land = refs[:n], refs[n:2 * n]
            outs = refs[2 * n + n_after:]
            send_sems, recv_sems, local_sems, token = outs[0], outs[1], outs[2], outs[3 + 2 * n]
            for w in range(n):
                self._own_copy(w, src[w], land[w], local_sems).start()
                for r in PEER_ORDER:
                    self._copy(w, r, src[w], land[w], send_sems, recv_sems, True).start()
            token[...] = jnp.zeros_like(token)

        sems = pltpu.SemaphoreType.DMA((n * _N_PEER,))
        outs = pl.pallas_call(
            body, name=self.name + "_start",
            out_shape=(sems, sems, pltpu.SemaphoreType.DMA((n,)),
                       *[pltpu.HBM(t.shape, t.dtype) for t in list(srcs) + list(lands)],
                       jax.ShapeDtypeStruct((8, 128), F32)),
            in_specs=[_HBM_SPEC] * (2 * n) + [_ANY_SPEC] * n_after,
            out_specs=(_SEM_SPEC, _SEM_SPEC, _SEM_SPEC, *[_HBM_SPEC] * (2 * n), pl.BlockSpec(memory_space=pltpu.VMEM)),
            input_output_aliases={i: 3 + i for i in range(2 * n)},
            compiler_params=pltpu.CompilerParams(has_side_effects=pltpu.SideEffectType.DATAFLOW_SIDE_EFFECTING),
        )(*[pltpu.with_memory_space_constraint(t, pltpu.HBM) for t in list(srcs) + list(lands)],
          *([after] if n_after else []))
        return outs[0], outs[1], outs[2], outs[3:3 + n], outs[3 + n:3 + 2 * n], outs[3 + 2 * n]

    def _own_copy(self, w, src, land, local_sems):
        me = _flat(*_my_place())
        return pltpu.make_async_copy(self.part(w, src, me), self.slot(w, land, me), local_sems.at[w])

    def wait(self, started, after, peers=PEER_ORDER, own=True, tag=""):
        send_sems, recv_sems, local_sems, srcs, lands, token = started
        n = len(srcs)

        def body(*refs):
            src, land = refs[:n], refs[n:2 * n]
            s_sems, r_sems, l_sems = refs[2 * n], refs[2 * n + 1], refs[2 * n + 2]
            for w in range(n):
                if own:
                    self._own_copy(w, src[w], land[w], l_sems).wait()
                for r in peers:
                    cp = self._copy(w, r, src[w], land[w], s_sems, r_sems, False)
                    cp.wait_send()
                    cp.wait_recv()

        outs = pl.pallas_call(
            body, name=self.name + "_wait" + tag,
            out_shape=[pltpu.HBM(t.shape, t.dtype) for t in list(srcs) + list(lands)],
            in_specs=[_HBM_SPEC] * (2 * n) + [_SEM_SPEC, _SEM_SPEC, _SEM_SPEC, _ANY_SPEC],
            out_specs=[_HBM_SPEC] * (2 * n),
            input_output_aliases={i: i for i in range(2 * n)},
            compiler_params=pltpu.CompilerParams(has_side_effects=pltpu.SideEffectType.DATAFLOW_SIDE_EFFECTING),
        )(*srcs, *lands, send_sems, recv_sems, local_sems, after)
        return outs[n:], (send_sems, recv_sems, local_sems, outs[:n], outs[n:], token)


DIRECT_PEERS = (1, 2, 4, 6)
FORWARDED = (3, 5, 7)


class _TwoLevelGather:
    def __init__(self, name, axes, sizes):
        self.name, self.axes, self.sizes = name, axes, sizes

    def _place(self, w, land, dev):
        return _shard_slice(land, self.axes[w], dev, self.sizes[w])

    def _direct(self, w, r, src, land, sems, sending):
        x, y, c = _my_place()
        peer = _peer_of(x, y, c, r)
        k = w * len(DIRECT_PEERS) + DIRECT_PEERS.index(r)
        return pltpu.make_async_remote_copy(
            src_ref=src, dst_ref=self._place(w, land, _flat(x, y, c) if sending else _flat(*peer)),
            send_sem=sems[0].at[k], recv_sem=sems[1].at[k], device_id=peer, device_id_type=MESH)

    def _passed_on(self, w, f, land, sems, sending):
        x, y, c = _my_place()
        owner = _flat(*_peer_of(x, y, c, (f ^ 1) if sending else f))
        slot = self._place(w, land, owner)
        k = w * len(FORWARDED) + FORWARDED.index(f)
        return pltpu.make_async_remote_copy(
            src_ref=slot, dst_ref=slot, send_sem=sems[2].at[k], recv_sem=sems[3].at[k],
            device_id=(x, y, 1 - c), device_id_type=MESH)

    def _own(self, w, src, land, sems):
        return pltpu.make_async_copy(src, self._place(w, land, _flat(*_my_place())), sems[4].at[w])

    def _call(self, suffix, body, sems, srcs, lands, after, make_sems):
        n = len(srcs)
        n_after = 0 if after is None else 1
        bufs = list(srcs) + list(lands)

        def wrapped(*refs):
            ins = refs[:2 * n]
            rest = refs[2 * n + (n_after if make_sems else 0):]
            body(ins[:n], ins[n:], rest[:5], rest[-1] if make_sems else None)

        buf_shapes = [pltpu.HBM(t.shape, t.dtype) for t in bufs]
        if make_sems:
            sem_types = [pltpu.SemaphoreType.DMA((n * len(DIRECT_PEERS),))] * 2 \
                + [pltpu.SemaphoreType.DMA((n * len(FORWARDED),))] * 2 + [pltpu.SemaphoreType.DMA((n,))]
            outs = pl.pallas_call(
                wrapped, name=self.name + suffix,
                out_shape=(*sem_types, *buf_shapes, jax.ShapeDtypeStruct((8, 128), F32)),
                in_specs=[_HBM_SPEC] * (2 * n) + [_ANY_SPEC] * n_after,
                out_specs=(*[_SEM_SPEC] * 5, *[_HBM_SPEC] * (2 * n), pl.BlockSpec(memory_space=pltpu.VMEM)),
                input_output_aliases={i: 5 + i for i in range(2 * n)},
                compiler_params=pltpu.CompilerParams(has_side_effects=pltpu.SideEffectType.DATAFLOW_SIDE_EFFECTING),
            )(*[pltpu.with_memory_space_constraint(t, pltpu.HBM) for t in bufs], *([after] if n_after else []))
            return tuple(outs[:5]), outs[5:5 + n], outs[5 + n:5 + 2 * n], outs[5 + 2 * n]
        outs = pl.pallas_call(
            wrapped, name=self.name + suffix,
            out_shape=buf_shapes,
            in_specs=[_HBM_SPEC] * (2 * n) + [_SEM_SPEC] * 5 + [_ANY_SPEC] * n_after,
            out_specs=[_HBM_SPEC] * (2 * n),
            input_output_aliases={i: i for i in range(2 * n)},
            compiler_params=pltpu.CompilerParams(has_side_effects=pltpu.SideEffectType.DATAFLOW_SIDE_EFFECTING),
        )(*bufs, *sems, *([after] if n_after else []))
        return sems, outs[:n], outs[n:], None

    def start(self, srcs, lands, after=None):
        n = len(srcs)

        def body(src, land, sems, token):
            for w in range(n):
                self._own(w, src[w], land[w], sems).start()
                for r in DIRECT_PEERS:
                    self._direct(w, r, src[w], land[w], sems, True).start()
            token[...] = jnp.zeros_like(token)

        return self._call("_start", body, None, srcs, lands, after, True)

    def forward(self, state, after, which, tag=""):
        sems, srcs, lands, token = state
        n = len(srcs)

        def body(src, land, sem_refs, _):
            for w in range(n):
                for r in which:
                    self._direct(w, r, src[w], land[w], sem_refs, False).wait_recv()
                    self._passed_on(w, r | 1, land[w], sem_refs, True).start()

        sems, srcs, lands, _ = self._call("_forward" + tag, body, sems, srcs, lands, after, False)
        return sems, srcs, lands, token

    def wait(self, state, after, direct=(), passed_on=(), sends=False, tag=""):
        sems, srcs, lands, token = state
        n = len(srcs)

        def body(src, land, sem_refs, _):
            for w in range(n):
                for r in direct:
                    self._direct(w, r, src[w], land[w], sem_refs, False).wait_recv()
                for f in passed_on:
                    self._passed_on(w, f, land[w], sem_refs, False).wait_recv()
                if sends:
                    self._own(w, src[w], land[w], sem_refs).wait()
                    for r in DIRECT_PEERS:
                        self._direct(w, r, src[w], land[w], sem_refs, True).wait_send()
                    for f in FORWARDED:
                        self._passed_on(w, f, land[w], sem_refs, True).wait_send()

        sems, srcs, lands, _ = self._call("_wait" + tag, body, sems, srcs, lands, after, False)
        return lands, (sems, srcs, lands, token)


def _scatter_exchange(name, axes, shard_sizes):
    def part(w, src, dev):
        return src if axes[w] is None else _shard_slice(src, axes[w], dev, shard_sizes[w])
    return _Exchange(name, part, lambda w, land, dev: land.at[dev])


def _adamw(name, partials, w, m, v):
    R, C = w.shape
    tr = R
    while tr * C * 4 > (1 << 20) and tr % 16 == 0:
        tr //= 2

    def body(p_ref, w_ref, m_ref, v_ref, g_ref, d_ref, nm_ref, nv_ref):
        g = p_ref[0].astype(F32)
        for jdev in range(1, N_DEV):
            g = g + p_ref[jdev].astype(F32)
        nm = ADAM_B1 * m_ref[...] + (1.0 - ADAM_B1) * g
        nv = ADAM_B2 * v_ref[...] + (1.0 - ADAM_B2) * (g * g)
        m_hat = nm / (1.0 - ADAM_B1 ** ADAM_STEP)
        v_hat = nv / (1.0 - ADAM_B2 ** ADAM_STEP)
        g_ref[...] = g
        d_ref[...] = -ADAM_LR * (m_hat / (jnp.sqrt(v_hat) + ADAM_EPS) + ADAM_WD * w_ref[...])
        nm_ref[...] = nm
        nv_ref[...] = nv

    spec = pl.BlockSpec((tr, C), lambda i: (i, 0))
    return pl.pallas_call(
        body, name=name, grid=(R // tr,),
        in_specs=[pl.BlockSpec((N_DEV, tr, C), lambda i: (0, i, 0)), spec, spec, spec],
        out_specs=[spec] * 4,
        out_shape=[jax.ShapeDtypeStruct((R, C), F32)] * 4,
        compiler_params=pltpu.CompilerParams(dimension_semantics=("parallel",),
                                             vmem_limit_bytes=_vmem_limit(24 << 20)),
    )(partials, w, m, v)


def _local_step(x, cos2, sin_fwd, sin_bwd, project_in, mix_weights, ffn_weights, pool_scale, g_mix, b_mix, g_ff, b_ff,
                target, send):
    S, D = x.shape
    aw = pw = D // 2
    u_col_block = 3
    gate_col0 = 4 * aw

    xb = x.astype(BF16)
    h, w_in = project_in(xb)
    dilated = [d for d in DILATIONS if d > 1]
    qkv = {1: h[None], **dict(zip(dilated, _to_residue_major("qkv_to_rm", h, 0, 3 * aw)))}
    fwd = [_attn_fwd(qkv[d], d, aw) for d in DILATIONS]
    o_attn, l_tot = _attn_combine([f[0] for f in fwd], [f[1] for f in fwd], aw)
    w_pool, w_ba, w_bp, w_out = mix_weights(o_attn)
    p, y_pre, pm = _pool_fwd(h, w_pool, pool_scale, pw, u_col_block)
    y_attn = _branch_attn(o_attn, w_ba)
    y_pool, merged = _branch_pool_merge(pm, w_bp, h, y_attn, gate_col0)
    w1, w2 = ffn_weights(merged)
    x1, x1b, xhat1, rstd1 = _out_proj_ln(merged, w_out, x, g_mix, b_mix)
    a = _ffn_up(x1b, w1)
    z2 = _residual_matmul("ffn_down", a, w2, "nn", x1)
    dz2, dz2b, dg_ff, db_ff, loss = _ln_loss_bwd(z2, g_ff, b_ff, target)

    tok = send("ff2", [_grad_weight("grad_w_ff2", a, dz2b)])
    dh1 = _ffn_down_bwd(dz2b, w2, a, after=tok)
    tok = send("ff1", [_grad_weight("grad_w_ff1", x1b, dh1)])
    dy1 = _residual_matmul("ffn_up_bwd", dh1, w1, "nt", dz2, after=tok)
    dz1, dz1b, dg_mix, db_mix = _ln_bwd(dy1, xhat1, rstd1, g_mix)
    tok = send("out", [_grad_weight("grad_w_out", merged, dz1b)])
    dy_attn, dy_pool, dh = _out_proj_bwd(dz1b, w_out, h, y_attn, y_pool, gate_col0, after=tok)
    tok = send("branch", [_grad_weight("grad_w_branch_attn", o_attn, dy_attn),
                          _grad_weight("grad_w_branch_pool", pm, dy_pool)])
    d_out, stats = _branch_attn_bwd(dy_attn, w_ba, o_attn, l_tot, after=tok)
    dy_pre, d_scale = _branch_pool_bwd(dy_pool, w_bp, y_pre, pool_scale)
    dh, dw_pool = _pool_bwd(dh, dy_pre, p, w_pool, pw, u_col_block)
    d_outs = {1: d_out[None], **dict(zip(dilated, _to_residue_major("dout_to_rm", d_out, 0, aw)))}
    statss = {1: stats[None], **dict(zip(dilated, _to_residue_major("stats_to_rm", stats, 0, HEAD_DIM)))}
    per_pattern = [_attn_bwd(qkv[d], d_outs[d], statss[d], d, aw) for d in DILATIONS]
    dh = _attn_bwd_finish(dh, per_pattern, cos2, sin_bwd, aw)
    small = jnp.concatenate((d_scale, dg_mix, db_mix, dg_ff, db_ff), axis=-1)
    tok = send("in", [_grad_weight("grad_w_in", xb, dh), dw_pool.astype(BF16),
                      small.reshape(small.shape[-1] // HEAD_DIM, HEAD_DIM)])
    grad_x = _residual_matmul("in_proj_bwd", dh, w_in, "nt", dz1, after=tok)
    return loss, grad_x


def _rope_tables(positions):
    half = HEAD_DIM // 2
    inv_freq = ROPE_THETA ** (-jnp.arange(half, dtype=F32) / half)
    ang = positions.astype(F32)[:, None] * inv_freq
    cos, sin = jnp.cos(ang), jnp.sin(ang)
    cos2 = jnp.concatenate([cos, cos], axis=-1)
    sin_fwd = jnp.concatenate([-sin, sin], axis=-1)
    return cos2, sin_fwd, -sin_fwd


def kernel(x, positions, w_in, w_pool, pool_scale, w_branch_attn, w_branch_pool, w_out, ln_mix_g, ln_mix_b, w_ff1, w_ff2, ln_ff_g, ln_ff_b, loss_target, m_w_in, m_w_pool, m_pool_scale, m_w_branch_attn, m_w_branch_pool, m_w_out, m_ln_mix_g, m_ln_mix_b, m_w_ff1, m_w_ff2, m_ln_ff_g, m_ln_ff_b, v_w_in, v_w_pool, v_pool_scale, v_w_branch_attn, v_w_branch_pool, v_w_out, v_ln_mix_g, v_ln_mix_b, v_w_ff1, v_w_ff2, v_ln_ff_g, v_ln_ff_b):
    big_w = (w_in[0], w_pool[0], w_branch_attn[0], w_branch_pool[0], w_out[0], w_ff1[0], w_ff2[0])
    big_m = (m_w_in[0], m_w_pool[0], m_w_branch_attn[0], m_w_branch_pool[0], m_w_out[0], m_w_ff1[0], m_w_ff2[0])
    big_v = (v_w_in[0], v_w_pool[0], v_w_branch_attn[0], v_w_branch_pool[0], v_w_out[0], v_w_ff1[0], v_w_ff2[0])
    shard_axes = (1, 1, 1, 1, 0, 1, 0)
    small_w = (pool_scale, ln_mix_g, ln_mix_b, ln_ff_g, ln_ff_b)
    small_m = (m_pool_scale, m_ln_mix_g, m_ln_mix_b, m_ln_ff_g, m_ln_ff_b)
    small_v = (v_pool_scale, v_ln_mix_g, v_ln_mix_b, v_ln_ff_g, v_ln_ff_b)

    names = ("w_in", "w_pool", "w_branch_attn", "w_branch_pool", "w_out", "w_ff1", "w_ff2")
    axis_of = dict(zip(names, shard_axes))
    shard_of = dict(zip(names, [w.astype(BF16) for w in big_w]))

    def full_buffer(n):
        s, ax = shard_of[n], axis_of[n]
        full = list(s.shape)
        full[ax] *= N_DEV
        return lax.empty(tuple(full), s.dtype)

    def gather_group(tag, group, after):
        ex = _TwoLevelGather(tag, [axis_of[n] for n in group], [shard_of[n].shape[axis_of[n]] for n in group])
        return ex, ex.start([shard_of[n] for n in group], [full_buffer(n) for n in group], after)

    in_ex, in_state = gather_group("gather_in", ("w_in",), None)
    mix_ex, mix_state = gather_group("gather_mix", ("w_pool", "w_branch_attn", "w_branch_pool", "w_out"),
                                     in_state[-1])
    ffn_ex, ffn_state = gather_group("gather_ffn", ("w_ff1", "w_ff2"), mix_state[-1])
    states = {"mix": mix_state, "ffn": ffn_state}
    me = 4 * lax.axis_index("x") + 2 * lax.axis_index("y") + lax.axis_index("c")
    block_cols = shard_of["w_in"].shape[1]
    neighbours, diagonal = (2, 4), (6,)

    def project_in(xb):
        n_rope_blocks = 2 * (x.shape[-1] // 2) // block_cols

        def piece(tag, w, blocks, h, **kw):
            return _in_proj_piece("in_proj_" + tag, xb, w, jnp.stack(blocks).astype(jnp.int32), cos2, sin_fwd, h,
                                  block_cols, n_rope_blocks, **kw)

        h = piece("own", shard_of["w_in"], [me], None, own_shard=True, after=ffn_state[-1])
        (w_in_land,), state = in_ex.wait(in_state, h, direct=(1,), tag="_sibling")
        h = piece("sibling", w_in_land, [me ^ 1], h)
        state = in_ex.forward(state, h, neighbours, tag="_neighbours")
        h = piece("neighbours", state[2][0], [me ^ r for r in neighbours], h)
        state = in_ex.forward(state, h, diagonal, tag="_diagonal")
        h = piece("diagonal", state[2][0], [me ^ r for r in diagonal], h)
        states["mix"] = mix_ex.forward(states["mix"], h, neighbours + diagonal)
        (w_in_land,), _ = in_ex.wait(state, h, passed_on=FORWARDED, sends=True, tag="_passed_on")
        h = piece("passed_on", w_in_land, [me ^ f for f in FORWARDED], h)
        return h, w_in_land

    def mix_weights(after):
        states["ffn"] = ffn_ex.forward(states["ffn"], after, neighbours + diagonal)
        return mix_ex.wait(states["mix"], after, direct=(1,), passed_on=FORWARDED, sends=True)[0]

    def ffn_weights(after):
        return ffn_ex.wait(states["ffn"], after, direct=(1,), passed_on=FORWARDED, sends=True)[0]

    groups = {"ff2": ("w_ff2",), "ff1": ("w_ff1",), "out": ("w_out",),
              "branch": ("w_branch_attn", "w_branch_pool"), "in": ("w_in", "w_pool", "small")}
    sent = {}

    def send(key, grads_):
        axes = [axis_of.get(n) for n in groups[key]]
        sizes = [None if ax is None else g.shape[ax] // N_DEV for g, ax in zip(grads_, axes)]
        lands = []
        for g, ax, size in zip(grads_, axes, sizes):
            shard = list(g.shape)
            if ax is not None:
                shard[ax] = size
            lands.append(lax.empty((N_DEV, *shard), g.dtype))
        ex = _scatter_exchange("scatter_" + key, axes, sizes)
        sent[key] = (ex, ex.start(list(grads_), lands))
        return sent[key][1][-1]

    cos2, sin_fwd, sin_bwd = _rope_tables(positions[0])
    loss, grad_x = _local_step(
        x[0], cos2, sin_fwd, sin_bwd, project_in, mix_weights, ffn_weights, pool_scale, ln_mix_g, ln_mix_b,
        ln_ff_g, ln_ff_b, loss_target[0], send)

    state = dict(zip(names, zip(big_w, big_m, big_v)))
    n_small = sum(w.shape[-1] for w in small_w)
    small_2d = (n_small // HEAD_DIM, HEAD_DIM)
    state["small"] = tuple(jnp.concatenate(t, axis=-1).reshape(small_2d) for t in (small_w, small_m, small_v))
    grads, deltas, new_ms, new_vs = {}, {}, {}, {}
    after = grad_x
    for key in ("ff2", "ff1", "out", "branch", "in"):
        ex, started = sent[key]
        for n, part in zip(groups[key], ex.wait(started, after)[0]):
            w, m, v = state[n]
            r2 = (-1, w.shape[-1])
            w2d = w.reshape(r2)
            res = _adamw("adamw_" + n, part.reshape((N_DEV,) + w2d.shape), w2d, m.reshape(r2), v.reshape(r2))
            after = res[0]
            if n == "small":
                small_out = [t.reshape(1, n_small) for t in res]
            else:
                grads[n], deltas[n], new_ms[n], new_vs[n] = (t.reshape((1,) + w.shape) for t in res)
    small_names = ("pool_scale", "ln_mix_g", "ln_mix_b", "ln_ff_g", "ln_ff_b")
    off = 0
    for n, w in zip(small_names, small_w):
        width = w.shape[-1]
        grads[n], deltas[n], new_ms[n], new_vs[n] = (t[:, off:off + width] for t in small_out)
        off += width

    order = ("w_in", "w_pool", "pool_scale", "w_branch_attn", "w_branch_pool", "w_out", "ln_mix_g", "ln_mix_b",
             "w_ff1", "w_ff2", "ln_ff_g", "ln_ff_b")
    total_loss = lax.psum(loss[0, 0], ("x", "y", "c"))
    return (total_loss, grad_x[None], *[grads[n] for n in order], *[deltas[n] for n in order],
            *[new_ms[n] for n in order], *[new_vs[n] for n in order])
```

```python
import functools

import jax
import jax.numpy as jnp
from jax import lax
from jax.experimental import pallas as pl
from jax.experimental.pallas import tpu as pltpu

F32 = jnp.float32
BF16 = jnp.bfloat16

N_DEV = 8
HEAD_DIM = 128
SUB_BLOCK = 128
DILATIONS = (1, 4, 16)
POOL_WINDOWS = (2, 4, 8, 16)
MAX_POOL_WINDOW = 16
POOL_HALO = 128
PERM_ROWS = 256
K_TILE = 1024
LN_EPS = 1e-5
DEEPNORM_ALPHA = 2.0 ** 0.25
ROPE_THETA = 10000.0
ATTN_SCALE = HEAD_DIM ** -0.5
ADAM_LR, ADAM_B1, ADAM_B2, ADAM_EPS, ADAM_WD, ADAM_STEP = 0.001, 0.9, 0.999, 1e-08, 0.01, 10
NEG_BIG = -1e30
VMEM_CAP_V7X = 64 * 1024 * 1024
MESH = pl.DeviceIdType.MESH


def _vmem_limit(est_bytes):
    return int(min(max(est_bytes * 5 // 4 + (4 << 20), 16 << 20), VMEM_CAP_V7X - (6 << 20)))


def _nbytes(shape, dtype):
    n = 1
    for s in shape:
        n *= s
    return n * jnp.dtype(dtype).itemsize


def _mm(name, a, b, form, tiles, outs, epi, extras=(), sequential=False, after=None):
    tm, tn, tk = tiles
    if form == "nn":
        (M, K), (K2, N) = a.shape, b.shape
    elif form == "nt":
        (M, K), (N, K2) = a.shape, b.shape
    else:
        (K, M), (K2, N) = a.shape, b.shape
    assert K == K2, (name, a.shape, b.shape)
    tm, tn, tk = min(tm, M), min(tn, N), min(tk, K)
    assert M % tm == 0 and N % tn == 0 and K % tk == 0, (name, M, N, K, tm, tn, tk)
    grid = (M // tm, N // tn, K // tk)
    nk = grid[2]
    if form == "nn":
        a_spec = pl.BlockSpec((tm, tk), lambda i, j, k: (i, k))
        b_spec = pl.BlockSpec((tk, tn), lambda i, j, k: (k, j))
        contract = ((1,), (0,))
    elif form == "nt":
        a_spec = pl.BlockSpec((tm, tk), lambda i, j, k: (i, k))
        b_spec = pl.BlockSpec((tn, tk), lambda i, j, k: (j, k))
        contract = ((1,), (1,))
    else:
        a_spec = pl.BlockSpec((tk, tm), lambda i, j, k: (k, i))
        b_spec = pl.BlockSpec((tk, tn), lambda i, j, k: (k, j))
        contract = ((0,), (0,))
    n_ex, n_out = len(extras), len(outs)
    n_after = 0 if after is None else 1

    def body(a_ref, b_ref, *rest):
        ex_refs = rest[:n_ex]
        rest = rest[n_ex + n_after:]
        out_refs = rest[:n_out]
        i, j, k = pl.program_id(0), pl.program_id(1), pl.program_id(2)

        def prod():
            return lax.dot_general(a_ref[...].astype(BF16), b_ref[...].astype(BF16),
                                   (contract, ((), ())), preferred_element_type=F32)

        if nk == 1:
            epi(prod(), ex_refs, out_refs, i, j)
        else:
            acc = rest[n_out]

            @pl.when(k == 0)
            def _():
                acc[...] = prod()

            @pl.when(jnp.logical_and(k > 0, k < nk - 1))
            def _():
                acc[...] += prod()

            @pl.when(k == nk - 1)
            def _():
                epi(acc[...] + prod(), ex_refs, out_refs, i, j)

    est = 2 * (_nbytes(a_spec.block_shape, a.dtype) + _nbytes(b_spec.block_shape, b.dtype))
    est += sum(2 * _nbytes(bs, arr.dtype) for arr, bs, _ in extras)
    est += sum(2 * _nbytes(bs, dt) for _, dt, bs, _ in outs)
    est += 4 * tm * tn * 4
    sem = ("arbitrary",) * 3 if sequential else ("parallel", "parallel", "arbitrary")
    return pl.pallas_call(
        body, name=name, grid=grid,
        in_specs=([a_spec, b_spec] + [pl.BlockSpec(bs, im) for _, bs, im in extras]
                  + [pl.BlockSpec(memory_space=pl.ANY)] * n_after),
        out_specs=[pl.BlockSpec(bs, im) for _, _, bs, im in outs],
        out_shape=[jax.ShapeDtypeStruct(sh, dt) for sh, dt, _, _ in outs],
        scratch_shapes=[pltpu.VMEM((tm, tn), F32)] if nk > 1 else [],
        compiler_params=pltpu.CompilerParams(dimension_semantics=sem, vmem_limit_bytes=_vmem_limit(est)),
    )(a, b, *[arr for arr, _, _ in extras], *([after] if n_after else []))


def _tile_out(shape, dtype, tm, tn):
    return (shape, dtype, (tm, tn), lambda i, j, k: (i, j))


def _row_sum_out(width):
    return ((1, width), F32, (1, width), lambda i, j, k: (0, 0))


def _accumulate_rows(ref, value, i):
    @pl.when(i == 0)
    def _():
        ref[...] = value

    @pl.when(i > 0)
    def _():
        ref[...] += value


def _layer_norm_bwd(dy, xhat, rstd, g):
    dxh = dy * g
    m1 = jnp.mean(dxh, axis=-1, keepdims=True)
    m2 = jnp.mean(dxh * xhat, axis=-1, keepdims=True)
    return rstd * (dxh - m1 - xhat * m2)


def _rope_apply(t, cos2, sin_signed):
    return t * cos2 + pltpu.roll(t, HEAD_DIM // 2, axis=1) * sin_signed


def _in_proj_piece(name, xb, w_in, col_blocks, cos2, sin_fwd, h_so_far, block_cols, n_rope_blocks, own_shard=False,
                   after=None):
    S, D = xb.shape
    W = w_in.shape[1] * (N_DEV if own_shard else 1)
    tm = min(1024, S)
    n_blocks = col_blocks.shape[0]

    def body(cols_ref, x_ref, w_ref, cos_ref, sin_ref, *rest):
        h_ref = rest[-1]
        j = pl.program_id(1)

        @pl.when(cols_ref[j] < n_rope_blocks)
        def _():
            acc = _dot_nn(x_ref[...], w_ref[...])
            c, s = cos_ref[...], sin_ref[...]
            for hd in range(block_cols // HEAD_DIM):
                sl = slice(hd * HEAD_DIM, (hd + 1) * HEAD_DIM)
                h_ref[:, sl] = _rope_apply(acc[:, sl], c, s).astype(BF16)

        @pl.when(cols_ref[j] >= n_rope_blocks)
        def _():
            h_ref[...] = _dot_nn(x_ref[...], w_ref[...]).astype(BF16)

    row = pl.BlockSpec((tm, HEAD_DIM), lambda i, j, cols: (i, 0))
    carried = ([] if h_so_far is None else [h_so_far]) + ([] if after is None else [after])
    est = 2 * (tm * D * 2 + D * block_cols * 2 + tm * block_cols * 2 + 2 * tm * HEAD_DIM * 4) + 3 * tm * block_cols * 4
    return pl.pallas_call(
        body, name=name,
        grid_spec=pltpu.PrefetchScalarGridSpec(
            num_scalar_prefetch=1, grid=(S // tm, n_blocks),
            in_specs=[pl.BlockSpec((tm, D), lambda i, j, cols: (i, 0)),
                      pl.BlockSpec((D, block_cols), lambda i, j, cols: (0, 0 if own_shard else cols[j])), row, row]
                     + [pl.BlockSpec(memory_space=pl.ANY)] * len(carried),
            out_specs=pl.BlockSpec((tm, block_cols), lambda i, j, cols: (i, cols[j]))),
        out_shape=jax.ShapeDtypeStruct((S, W), BF16),
        input_output_aliases={} if h_so_far is None else {5: 0},
        compiler_params=pltpu.CompilerParams(dimension_semantics=("parallel", "arbitrary"),
                                             vmem_limit_bytes=_vmem_limit(est)),
    )(col_blocks, xb, w_in, cos2, sin_fwd, *carried)


def _attn_mask(mb):
    qi = lax.broadcasted_iota(jnp.int32, (SUB_BLOCK, 2 * SUB_BLOCK), 0)
    kj = lax.broadcasted_iota(jnp.int32, (SUB_BLOCK, 2 * SUB_BLOCK), 1)
    prev = jnp.logical_and(jnp.logical_and(kj < SUB_BLOCK, kj >= qi), mb > 0)
    cur = jnp.logical_and(kj >= SUB_BLOCK, kj - SUB_BLOCK <= qi)
    return jnp.logical_or(prev, cur)


def _both_blocks(prev_ref, cur_ref, sl):
    return jnp.concatenate([prev_ref[:, sl], cur_ref[:, sl]], axis=0)


def _dot_nt(a, b):
    return lax.dot_general(a, b, (((1,), (1,)), ((), ())), preferred_element_type=F32)


def _dot_tn(a, b):
    return lax.dot_general(a, b, (((0,), (0,)), ((), ())), preferred_element_type=F32)


def _dot_nn(a, b):
    return lax.dot_general(a, b, (((1,), (0,)), ((), ())), preferred_element_type=F32)


def _perm_matrix(d, to_residue_major):
    g = PERM_ROWS // d
    i = lax.broadcasted_iota(jnp.int32, (PERM_ROWS, PERM_ROWS), 0)
    j = lax.broadcasted_iota(jnp.int32, (PERM_ROWS, PERM_ROWS), 1)
    if to_residue_major:
        hit = j == (i % g) * d + i // g
    else:
        hit = j == (i % d) * g + i // d
    return hit.astype(BF16)


def _permute_rows(perm, x, terms=3):
    if x.dtype == BF16:
        return _dot_nn(perm, x)
    hi = x.astype(BF16)
    r1 = x - hi.astype(F32)
    mid = r1.astype(BF16)
    out = _dot_nn(perm, hi) + _dot_nn(perm, mid)
    if terms == 3:
        out = out + _dot_nn(perm, (r1 - mid.astype(F32)).astype(BF16))
    return out


def _rm_block(d, width):
    return pl.BlockSpec((d, PERM_ROWS // d, width), lambda i: (0, i, 0))


def _to_residue_major(name, x, col_block, width):
    S = x.shape[0]
    dils = [d for d in DILATIONS if d > 1]
    chunk = min(width, 1024)

    def body(x_ref, *out_refs):
        for d, o_ref in zip(dils, out_refs):
            perm = _perm_matrix(d, True)
            for c0 in range(0, width, chunk):
                cw = min(chunk, width - c0)
                y = _permute_rows(perm, x_ref[:, c0:c0 + cw])
                o_ref[:, :, c0:c0 + cw] = y.astype(x.dtype).reshape(d, PERM_ROWS // d, cw)

    return pl.pallas_call(
        body, name=name, grid=(S // PERM_ROWS,),
        in_specs=[pl.BlockSpec((PERM_ROWS, width), lambda i: (i, col_block))],
        out_specs=[_rm_block(d, width) for d in dils],
        out_shape=[jax.ShapeDtypeStruct((d, S // d, width), x.dtype) for d in dils],
        compiler_params=pltpu.CompilerParams(dimension_semantics=("parallel",),
                                             vmem_limit_bytes=_vmem_limit(32 << 20)),
    )(x)


def _qkv_specs(aw):
    def spec(col, prev):
        if prev:
            return pl.BlockSpec((None, SUB_BLOCK, aw), lambda r, mb: (r, jnp.maximum(mb - 1, 0), col))
        return pl.BlockSpec((None, SUB_BLOCK, aw), lambda r, mb: (r, mb, col))
    return [spec(0, False), spec(1, True), spec(1, False), spec(2, True), spec(2, False)]


def _put_column(tile, col, value):
    lane = lax.broadcasted_iota(jnp.int32, tile.shape, 1)
    return jnp.where(lane == col, value, tile)


def _attn_fwd(qkv, d, aw):
    _, rows, _ = qkv.shape
    n_heads = aw // HEAD_DIM
    nb = rows // SUB_BLOCK

    def body(q_ref, kp_ref, kc_ref, vp_ref, vc_ref, o_ref, lse_ref, s_buf, p_buf):
        mask = _attn_mask(pl.program_id(1))
        for hd in range(n_heads):
            sl = slice(hd * HEAD_DIM, (hd + 1) * HEAD_DIM)
            s_buf[hd] = _dot_nt(q_ref[:, sl], _both_blocks(kp_ref, kc_ref, sl))
        lse_tile = jnp.zeros((SUB_BLOCK, HEAD_DIM), F32)
        inv_tile = jnp.zeros((SUB_BLOCK, HEAD_DIM), F32)
        for hd in range(n_heads):
            s = jnp.where(mask, s_buf[hd] * ATTN_SCALE, NEG_BIG)
            m = jnp.max(s, axis=-1, keepdims=True)
            p = jnp.exp(s - m)
            l = jnp.sum(p, axis=-1, keepdims=True)
            p_buf[hd] = p.astype(BF16)
            lse_tile = _put_column(lse_tile, hd, m + jnp.log(l))
            inv_tile = _put_column(inv_tile, hd, 1.0 / l)
        lse_ref[...] = lse_tile
        for hd in range(n_heads):
            sl = slice(hd * HEAD_DIM, (hd + 1) * HEAD_DIM)
            o = _dot_nn(p_buf[hd], _both_blocks(vp_ref, vc_ref, sl))
            o_ref[:, sl] = o * inv_tile[:, hd:hd + 1]

    return pl.pallas_call(
        body, name=f"attn_fwd_d{d}", grid=(d, nb),
        in_specs=_qkv_specs(aw),
        out_specs=[pl.BlockSpec((None, SUB_BLOCK, aw), lambda r, mb: (r, mb, 0)),
                   pl.BlockSpec((None, SUB_BLOCK, HEAD_DIM), lambda r, mb: (r, mb, 0))],
        out_shape=[jax.ShapeDtypeStruct((d, rows, aw), F32), jax.ShapeDtypeStruct((d, rows, HEAD_DIM), F32)],
        scratch_shapes=[pltpu.VMEM((n_heads, SUB_BLOCK, 2 * SUB_BLOCK), F32),
                        pltpu.VMEM((n_heads, SUB_BLOCK, 2 * SUB_BLOCK), BF16)],
        compiler_params=pltpu.CompilerParams(dimension_semantics=("parallel", "parallel"),
                                             vmem_limit_bytes=_vmem_limit(16 << 20)),
    )(qkv, qkv, qkv, qkv, qkv)


def _attn_combine(outs, lses, aw):
    S = outs[0].shape[1]
    n_heads = aw // HEAD_DIM
    n_pat = len(DILATIONS)

    def body(*refs):
        o_refs, l_refs = refs[:n_pat], refs[n_pat:2 * n_pat]
        o_ref, lt_ref = refs[2 * n_pat], refs[2 * n_pat + 1]
        o_nat, l_nat = [], []
        for d, o_r, l_r in zip(DILATIONS, o_refs, l_refs):
            o_p = o_r[...].reshape(PERM_ROWS, aw)
            l_p = l_r[...].reshape(PERM_ROWS, HEAD_DIM)
            if d > 1:
                perm = _perm_matrix(d, False)
                o_p, l_p = _permute_rows(perm, o_p, terms=2), _permute_rows(perm, l_p)
            o_nat.append(o_p)
            l_nat.append(l_p)
        mx = functools.reduce(jnp.maximum, l_nat)
        es = [jnp.exp(l_p - mx) for l_p in l_nat]
        den = functools.reduce(jnp.add, es)
        lt_ref[...] = mx + jnp.log(den)
        ws = [e / den for e in es]
        for hd in range(n_heads):
            sl = slice(hd * HEAD_DIM, (hd + 1) * HEAD_DIM)
            o = ws[0][:, hd:hd + 1] * o_nat[0][:, sl]
            for pi in range(1, n_pat):
                o = o + ws[pi][:, hd:hd + 1] * o_nat[pi][:, sl]
            o_ref[:, sl] = o.astype(BF16)

    return pl.pallas_call(
        body, name="attn_combine", grid=(S // PERM_ROWS,),
        in_specs=[_rm_block(d, aw) for d in DILATIONS] + [_rm_block(d, HEAD_DIM) for d in DILATIONS],
        out_specs=[pl.BlockSpec((PERM_ROWS, aw), lambda i: (i, 0)), pl.BlockSpec((PERM_ROWS, HEAD_DIM), lambda i: (i, 0))],
        out_shape=[jax.ShapeDtypeStruct((S, aw), BF16), jax.ShapeDtypeStruct((S, HEAD_DIM), F32)],
        compiler_params=pltpu.CompilerParams(dimension_semantics=("parallel",),
                                             vmem_limit_bytes=_vmem_limit(40 << 20)),
    )(*outs, *lses)


def _band(tm, width, w, row_offset, transpose):
    t = lax.broadcasted_iota(jnp.int32, (tm, width), 0)
    u = lax.broadcasted_iota(jnp.int32, (tm, width), 1)
    dist = (u - t - row_offset) if transpose else (t + row_offset - u)
    return jnp.logical_and(dist >= 0, dist < w).astype(BF16)


def _pool_fwd(h, w_pool, pool_scale, pw, u_col_block):
    S, W = h.shape
    n_groups = len(POOL_WINDOWS)
    gw = pw // n_groups
    tm = min(512, S)
    halo_per_tile = tm // POOL_HALO

    def body(uc_ref, uh_ref, w_ref, sc_ref, p_ref, y_ref, pm_ref):
        i = pl.program_id(0)
        t_abs = i * tm + lax.broadcasted_iota(jnp.int32, (tm, 1), 0)
        for g, w in enumerate(POOL_WINDOWS):
            sl = slice(g * gw, (g + 1) * gw)
            uc = uc_ref[:, sl]
            uh = jnp.where(i > 0, uh_ref[:, sl], jnp.zeros((POOL_HALO, gw), BF16))
            ssum = _dot_nn(_band(tm, tm, w, 0, False), uc) + _dot_nn(_band(tm, POOL_HALO, w, POOL_HALO, False), uh)
            cnt = jnp.minimum(t_abs + 1, w).astype(F32)
            p = (ssum / cnt - uc.astype(F32)).astype(BF16)
            y = _dot_nn(p, w_ref[g])
            p_ref[:, sl] = p
            y_ref[:, sl] = y.astype(BF16)
            pm_ref[:, sl] = (y * sc_ref[:, sl]).astype(BF16)

    row = pl.BlockSpec((tm, pw), lambda i: (i, 0))
    return pl.pallas_call(
        body, name="pool_fwd", grid=(S // tm,),
        in_specs=[pl.BlockSpec((tm, pw), lambda i: (i, u_col_block)),
                  pl.BlockSpec((POOL_HALO, pw), lambda i: (jnp.maximum(i * halo_per_tile - 1, 0), u_col_block)),
                  pl.BlockSpec((n_groups, gw, gw), lambda i: (0, 0, 0)),
                  pl.BlockSpec((1, pw), lambda i: (0, 0))],
        out_specs=[row, row, row],
        out_shape=[jax.ShapeDtypeStruct((S, pw), BF16)] * 3,
        compiler_params=pltpu.CompilerParams(dimension_semantics=("parallel",),
                                             vmem_limit_bytes=_vmem_limit(24 << 20)),
    )(h, h, w_pool, pool_scale)


def _branch_attn(o_attn, w_ba):
    S, _ = o_attn.shape
    D = w_ba.shape[1]
    tm, tn = min(1024, S), D

    def epi(acc, ex, out, i, j):
        out[0][...] = acc.astype(BF16)

    (y,) = _mm("branch_attn", o_attn, w_ba, "nn", (tm, tn, 1024), [_tile_out((S, D), BF16, tm, tn)], epi)
    return y


def _branch_pool_merge(pm, w_bp, h, y_attn, gate_col0):
    S, _ = pm.shape
    D = w_bp.shape[1]
    tm, tn = min(512, S), D
    ga0, gp0 = gate_col0 // tn, (gate_col0 + D) // tn

    def epi(acc, ex, out, i, j):
        ga_ref, gp_ref, ya_ref = ex
        yp_ref, mg_ref = out
        yp = acc.astype(BF16)
        yp_ref[...] = yp
        mg = (jax.nn.sigmoid(ga_ref[...]).astype(F32) * ya_ref[...].astype(F32)
              + jax.nn.sigmoid(gp_ref[...]).astype(F32) * acc)
        mg_ref[...] = mg.astype(BF16)

    y_pool, merged = _mm(
        "branch_pool_merge", pm, w_bp, "nn", (tm, tn, 1024),
        [_tile_out((S, D), BF16, tm, tn), _tile_out((S, D), BF16, tm, tn)], epi,
        extras=[(h, (tm, tn), lambda i, j, k: (i, ga0 + j)), (h, (tm, tn), lambda i, j, k: (i, gp0 + j)),
                (y_attn, (tm, tn), lambda i, j, k: (i, j))])
    return y_pool, merged


def _layer_norm_rows(z, g, b):
    mu = jnp.mean(z, axis=-1, keepdims=True)
    zc = z - mu
    var = jnp.mean(zc * zc, axis=-1, keepdims=True)
    rstd = lax.rsqrt(var + LN_EPS)
    xhat = zc * rstd
    return xhat * g + b, xhat, rstd


def _out_proj_ln(merged, w_out, x, g, b):
    S, D = x.shape
    tm = min(256, S)

    def epi(acc, ex, out, i, j):
        x_ref, g_ref, b_ref = ex
        x1_ref, x1b_ref, xh_ref, rs_ref = out
        y, xhat, rstd = _layer_norm_rows(DEEPNORM_ALPHA * x_ref[...] + acc, g_ref[...], b_ref[...])
        x1_ref[...] = y
        x1b_ref[...] = y.astype(BF16)
        xh_ref[...] = xhat
        rs_ref[...] = jnp.broadcast_to(rstd, (tm, HEAD_DIM))

    row = lambda i, j, k: (i, 0)
    vec = lambda i, j, k: (0, 0)
    return _mm("out_proj_ln", merged, w_out, "nn", (tm, D, D),
               [((S, D), F32, (tm, D), row), ((S, D), BF16, (tm, D), row), ((S, D), F32, (tm, D), row),
                ((S, HEAD_DIM), F32, (tm, HEAD_DIM), row)], epi,
               extras=[(x, (tm, D), row), (g, (1, D), vec), (b, (1, D), vec)])


def _ffn_up(x1b, w1):
    S, D = x1b.shape
    F = w1.shape[1]
    tm, tn = min(1024, S), min(2048, F)

    def epi(acc, ex, out, i, j):
        r = jnp.maximum(acc, 0.0)
        out[0][...] = (r * r).astype(BF16)

    (a,) = _mm("ffn_up", x1b, w1, "nn", (tm, tn, 2 * K_TILE), [_tile_out((S, F), BF16, tm, tn)], epi)
    return a


def _residual_matmul(name, a, w, form, resid, after=None):
    S, D = resid.shape
    tm, tn = min(1024, S), min(2048, D)

    def epi(acc, ex, out, i, j):
        out[0][...] = DEEPNORM_ALPHA * ex[0][...] + acc

    (z,) = _mm(name, a, w, form, (tm, tn, K_TILE), [_tile_out((S, D), F32, tm, tn)], epi,
               extras=[(resid, (tm, tn), lambda i, j, k: (i, j))], after=after)
    return z


def _row_kernel(name, body, row_inputs, vec_inputs, row_outputs, sum_widths, tr):
    S = row_inputs[0].shape[0]
    row = lambda w: pl.BlockSpec((tr, w), lambda i: (i, 0))
    vec = lambda w: pl.BlockSpec((1, w), lambda i: (0, 0))

    def wrapped(*refs):
        body(pl.program_id(0), *refs)

    return pl.pallas_call(
        wrapped, name=name, grid=(S // tr,),
        in_specs=[row(t.shape[1]) for t in row_inputs] + [vec(t.shape[1]) for t in vec_inputs],
        out_specs=[row(w) for w, _ in row_outputs] + [vec(w) for w in sum_widths],
        out_shape=([jax.ShapeDtypeStruct((S, w), dt) for w, dt in row_outputs]
                   + [jax.ShapeDtypeStruct((1, w), F32) for w in sum_widths]),
        compiler_params=pltpu.CompilerParams(dimension_semantics=("arbitrary",),
                                             vmem_limit_bytes=_vmem_limit(40 << 20)),
    )(*row_inputs, *vec_inputs)


def _ln_loss_bwd(z2, g, b, target):
    S, D = z2.shape

    def body(i, z_ref, t_ref, g_ref, b_ref, dz_ref, dzb_ref, dg_ref, db_ref, loss_ref):
        gv = g_ref[...]
        y, xhat, rstd = _layer_norm_rows(z_ref[...], gv, b_ref[...])
        err = y - t_ref[...]
        loss = 0.5 * jnp.sum(jnp.mean(err * err, axis=-1, keepdims=True), axis=0, keepdims=True)
        dy = err * (1.0 / D)
        dz = _layer_norm_bwd(dy, xhat, rstd, gv)
        dz_ref[...] = dz
        dzb_ref[...] = dz.astype(BF16)
        _accumulate_rows(dg_ref, jnp.sum(dy * xhat, axis=0, keepdims=True), i)
        _accumulate_rows(db_ref, jnp.sum(dy, axis=0, keepdims=True), i)
        _accumulate_rows(loss_ref, jnp.broadcast_to(loss, (1, HEAD_DIM)), i)

    return _row_kernel("ln_loss_bwd", body, [z2, target], [g, b], [(D, F32), (D, BF16)], [D, D, HEAD_DIM],
                       min(256, S))


def _ln_bwd(dy, xhat, rstd, g):
    S, D = dy.shape

    def body(i, dy_ref, xh_ref, rs_ref, g_ref, dz_ref, dzb_ref, dg_ref, db_ref):
        dyv, xhat_v = dy_ref[...], xh_ref[...]
        dz = _layer_norm_bwd(dyv, xhat_v, rs_ref[:, :1], g_ref[...])
        dz_ref[...] = dz
        dzb_ref[...] = dz.astype(BF16)
        _accumulate_rows(dg_ref, jnp.sum(dyv * xhat_v, axis=0, keepdims=True), i)
        _accumulate_rows(db_ref, jnp.sum(dyv, axis=0, keepdims=True), i)

    return _row_kernel("ln_bwd", body, [dy, xhat, rstd], [g], [(D, F32), (D, BF16)], [D, D], min(256, S))


def _grad_weight(name, act, cot):
    M, N = act.shape[1], cot.shape[1]
    tm, tn = min(1024, M), min(2048, N)

    def epi(acc, ex, out, i, j):
        out[0][...] = acc.astype(BF16)

    (g,) = _mm(name, act, cot, "tn", (tm, tn, 2 * K_TILE), [_tile_out((M, N), BF16, tm, tn)], epi)
    return g


def _ffn_down_bwd(dz2b, w2, a, after=None):
    S, D = dz2b.shape
    F = w2.shape[0]
    tm, tn = min(1024, S), min(2048, F)

    def epi(acc, ex, out, i, j):
        out[0][...] = (acc * (2.0 * jnp.sqrt(ex[0][...])).astype(F32)).astype(BF16)

    (dh1,) = _mm("ffn_down_bwd", dz2b, w2, "nt", (tm, tn, 2 * K_TILE), [_tile_out((S, F), BF16, tm, tn)], epi,
                 extras=[(a, (tm, tn), lambda i, j, k: (i, j))], after=after)
    return dh1


def _out_proj_bwd(dz1b, w_out, h, y_attn, y_pool, gate_col0, after=None):
    S, D = dz1b.shape
    W = h.shape[1]
    tm = min(256, S)
    assert gate_col0 == 2 * D and W == 4 * D

    def epi(acc, ex, out, i, j):
        gates_ref, ya_ref, yp_ref = ex
        dya_ref, dyp_ref, dh_ref = out
        sa = jax.nn.sigmoid(gates_ref[:, :D]).astype(F32)
        sp = jax.nn.sigmoid(gates_ref[:, D:]).astype(F32)
        dya_ref[...] = (acc * sa).astype(BF16)
        dyp_ref[...] = (acc * sp).astype(BF16)
        dh_ref[:, :D] = (acc * ya_ref[...].astype(F32) * (sa * (1.0 - sa))).astype(BF16)
        dh_ref[:, D:] = (acc * yp_ref[...].astype(F32) * (sp * (1.0 - sp))).astype(BF16)

    row = lambda i, j, k: (i, 0)
    return _mm("out_proj_bwd", dz1b, w_out, "nt", (tm, D, D),
               [((S, D), BF16, (tm, D), row), ((S, D), BF16, (tm, D), row),
                ((S, W), BF16, (tm, 2 * D), lambda i, j, k: (i, 1))], epi,
               extras=[(h, (tm, 2 * D), lambda i, j, k: (i, 1)), (y_attn, (tm, D), row), (y_pool, (tm, D), row)],
               after=after)


def _branch_attn_bwd(dy_attn, w_ba, o_attn, l_tot, after=None):
    S, D = dy_attn.shape
    aw = w_ba.shape[0]
    n_heads = aw // HEAD_DIM
    tm = min(512, S)

    def epi(acc, ex, out, i, j):
        do_ref, st_ref = out
        do_ref[...] = acc.astype(BF16)
        o = ex[0][...].astype(F32)
        stats = ex[1][...]
        for hd in range(n_heads):
            sl = slice(hd * HEAD_DIM, (hd + 1) * HEAD_DIM)
            stats = _put_column(stats, n_heads + hd, jnp.sum(acc[:, sl] * o[:, sl], axis=-1, keepdims=True))
        st_ref[...] = stats

    row = lambda i, j, k: (i, 0)
    return _mm("branch_attn_bwd", dy_attn, w_ba, "nt", (tm, aw, D),
               [((S, aw), BF16, (tm, aw), row), ((S, HEAD_DIM), F32, (tm, HEAD_DIM), row)], epi,
               extras=[(o_attn, (tm, aw), row), (l_tot, (tm, HEAD_DIM), row)], after=after)


def _branch_pool_bwd(dy_pool, w_bp, y_pre, pool_scale):
    S, D = dy_pool.shape
    pw = w_bp.shape[0]
    tm = min(512, S)

    def epi(acc, ex, out, i, j):
        y_ref, sc_ref = ex
        dyp_ref, dsc_ref = out
        dyp_ref[...] = (acc * sc_ref[...]).astype(BF16)
        _accumulate_rows(dsc_ref, jnp.sum(acc * y_ref[...].astype(F32), axis=0, keepdims=True), i)

    row = lambda i, j, k: (i, 0)
    return _mm("branch_pool_bwd", dy_pool, w_bp, "nt", (tm, pw, D),
               [((S, pw), BF16, (tm, pw), row), _row_sum_out(pw)], epi,
               extras=[(y_pre, (tm, pw), row), (pool_scale, (1, pw), lambda i, j, k: (0, 0))],
               sequential=True)


def _pool_bwd(dh, dy_pre, p, w_pool, pw, u_col_block):
    S, W = dh.shape
    n_groups = len(POOL_WINDOWS)
    gw = pw // n_groups
    tm = min(512, S)
    n_tiles = S // tm
    halo_per_tile = tm // POOL_HALO
    n_halo_blocks = S // POOL_HALO

    def body(dh_in_ref, dyc_ref, dyh_ref, p_ref, w_ref, dh_ref, dw_ref):
        del dh_in_ref
        i = pl.program_id(0)
        t_cur = i * tm + lax.broadcasted_iota(jnp.int32, (tm, 1), 0)
        t_halo = (i + 1) * tm + lax.broadcasted_iota(jnp.int32, (POOL_HALO, 1), 0)
        for g, w in enumerate(POOL_WINDOWS):
            sl = slice(g * gw, (g + 1) * gw)
            wg = w_ref[g]
            dyc = dyc_ref[:, sl]
            dyh = jnp.where(i < n_tiles - 1, dyh_ref[:, sl], jnp.zeros((POOL_HALO, gw), BF16))
            dp_cur = _dot_nt(dyc, wg)
            dp_halo = _dot_nt(dyh, wg)
            dpc_cur = (dp_cur / jnp.minimum(t_cur + 1, w).astype(F32)).astype(BF16)
            dpc_halo = (dp_halo / jnp.minimum(t_halo + 1, w).astype(F32)).astype(BF16)
            du = (_dot_nn(_band(tm, tm, w, 0, True), dpc_cur)
                  + _dot_nn(_band(tm, POOL_HALO, w, -tm, True), dpc_halo) - dp_cur)
            dh_ref[:, sl] = du.astype(BF16)
            dw = _dot_tn(p_ref[:, sl], dyc)

            @pl.when(i == 0)
            def _():
                dw_ref[g] = dw

            @pl.when(i > 0)
            def _():
                dw_ref[g] += dw

    row = pl.BlockSpec((tm, pw), lambda i: (i, 0))
    dh_new, dw_pool = pl.pallas_call(
        body, name="pool_bwd", grid=(n_tiles,),
        in_specs=[pl.BlockSpec(memory_space=pl.ANY), row,
                  pl.BlockSpec((POOL_HALO, pw), lambda i: (jnp.minimum((i + 1) * halo_per_tile, n_halo_blocks - 1), 0)),
                  row, pl.BlockSpec((n_groups, gw, gw), lambda i: (0, 0, 0))],
        out_specs=[pl.BlockSpec((tm, pw), lambda i: (i, u_col_block)),
                   pl.BlockSpec((n_groups, gw, gw), lambda i: (0, 0, 0))],
        out_shape=[jax.ShapeDtypeStruct((S, W), BF16), jax.ShapeDtypeStruct((n_groups, gw, gw), F32)],
        input_output_aliases={0: 0},
        compiler_params=pltpu.CompilerParams(dimension_semantics=("arbitrary",),
                                             vmem_limit_bytes=_vmem_limit(24 << 20)),
    )(dh, dy_pre, dy_pre, p, w_pool)
    return dh_new, dw_pool


def _attn_bwd(qkv, d_out, stats, d, aw):
    _, rows, _ = qkv.shape
    n_heads = aw // HEAD_DIM
    nb = rows // SUB_BLOCK
    n_blocks = d * nb

    def body(q_ref, kp_ref, kc_ref, vp_ref, vc_ref, do_ref, st_ref, dq_ref, dk_ref, dv_ref,
             carry_k, carry_v, s_buf, dp_buf, p_buf, ds_buf):
        step = pl.program_id(0)

        @pl.when(step == 0)
        def _():
            carry_k[...] = jnp.zeros_like(carry_k)
            carry_v[...] = jnp.zeros_like(carry_v)

        @pl.when(step < n_blocks)
        def _():
            mask = _attn_mask(step % nb)
            st = st_ref[...]
            for hd in range(n_heads):
                sl = slice(hd * HEAD_DIM, (hd + 1) * HEAD_DIM)
                s_buf[hd] = _dot_nt(q_ref[:, sl], _both_blocks(kp_ref, kc_ref, sl))
                dp_buf[hd] = _dot_nt(do_ref[:, sl], _both_blocks(vp_ref, vc_ref, sl))
            for hd in range(n_heads):
                lt, dl = st[:, hd:hd + 1], st[:, n_heads + hd:n_heads + hd + 1]
                p = jnp.where(mask, jnp.exp(jnp.where(mask, s_buf[hd] * ATTN_SCALE - lt, NEG_BIG)), 0.0)
                p_buf[hd] = p.astype(BF16)
                ds_buf[hd] = (p * (dp_buf[hd] - dl) * ATTN_SCALE).astype(BF16)
            for hd in range(n_heads):
                sl = slice(hd * HEAD_DIM, (hd + 1) * HEAD_DIM)
                dq_ref[:, sl] = _dot_nn(ds_buf[hd], _both_blocks(kp_ref, kc_ref, sl)).astype(BF16)
                dk_both = _dot_tn(ds_buf[hd], q_ref[:, sl])
                dv_both = _dot_tn(p_buf[hd], do_ref[:, sl])
                dk_ref[:, sl] = (carry_k[:, sl] + dk_both[:SUB_BLOCK]).astype(BF16)
                dv_ref[:, sl] = (carry_v[:, sl] + dv_both[:SUB_BLOCK]).astype(BF16)
                carry_k[:, sl] = dk_both[SUB_BLOCK:]
                carry_v[:, sl] = dv_both[SUB_BLOCK:]

        @pl.when(step == n_blocks)
        def _():
            dk_ref[...] = carry_k[...].astype(BF16)
            dv_ref[...] = carry_v[...].astype(BF16)

    def cur(step):
        return jnp.minimum(step, n_blocks - 1)

    def qkv_spec(col, prev):
        if prev:
            return pl.BlockSpec((SUB_BLOCK, aw), lambda s: (jnp.maximum(cur(s) - 1, 0), col))
        return pl.BlockSpec((SUB_BLOCK, aw), lambda s: (cur(s), col))

    def at_cur(w):
        return pl.BlockSpec((SUB_BLOCK, w), lambda s: (cur(s), 0))

    finished = pl.BlockSpec((SUB_BLOCK, aw), lambda s: (jnp.maximum(s - 1, 0), 0))
    pair = (n_heads, SUB_BLOCK, 2 * SUB_BLOCK)
    flat = lambda t: t.reshape(d * rows, t.shape[-1])
    qkv2 = flat(qkv)
    outs = pl.pallas_call(
        body, name=f"attn_bwd_d{d}", grid=(n_blocks + 1,),
        in_specs=[qkv_spec(0, False), qkv_spec(1, True), qkv_spec(1, False), qkv_spec(2, True), qkv_spec(2, False),
                  at_cur(aw), at_cur(HEAD_DIM)],
        out_specs=[at_cur(aw), finished, finished],
        out_shape=[jax.ShapeDtypeStruct((d * rows, aw), BF16)] * 3,
        scratch_shapes=[pltpu.VMEM((SUB_BLOCK, aw), F32), pltpu.VMEM((SUB_BLOCK, aw), F32),
                        pltpu.VMEM(pair, F32), pltpu.VMEM(pair, F32), pltpu.VMEM(pair, BF16), pltpu.VMEM(pair, BF16)],
        compiler_params=pltpu.CompilerParams(dimension_semantics=("arbitrary",),
                                             vmem_limit_bytes=_vmem_limit(24 << 20)),
    )(qkv2, qkv2, qkv2, qkv2, qkv2, flat(d_out), flat(stats))
    return [t.reshape(d, rows, aw) for t in outs]


def _attn_bwd_finish(dh, per_pattern, cos2, sin_bwd, aw):
    S, W = dh.shape
    n_heads = aw // HEAD_DIM
    n_pat = len(DILATIONS)

    def body(*refs):
        grad_refs = refs[1:1 + 3 * n_pat]
        cos_ref, sin_ref = refs[1 + 3 * n_pat], refs[2 + 3 * n_pat]
        out_ref = refs[3 + 3 * n_pat]
        perms = {d: _perm_matrix(d, False) for d in DILATIONS if d > 1}
        totals = []
        for which in range(3):
            tot = None
            for pi, d in enumerate(DILATIONS):
                g = grad_refs[which * n_pat + pi][...].reshape(PERM_ROWS, aw)
                g = _permute_rows(perms[d], g) if d > 1 else g.astype(F32)
                tot = g if tot is None else tot + g
            totals.append(tot)
        dq, dk, dv = totals
        c, s = cos_ref[...], sin_ref[...]
        for hd in range(n_heads):
            sl = slice(hd * HEAD_DIM, (hd + 1) * HEAD_DIM)
            out_ref[:, sl] = _rope_apply(dq[:, sl], c, s).astype(BF16)
            out_ref[:, aw + hd * HEAD_DIM:aw + (hd + 1) * HEAD_DIM] = _rope_apply(dk[:, sl], c, s).astype(BF16)
        out_ref[:, 2 * aw:] = dv.astype(BF16)

    grads = [pp[which] for which in range(3) for pp in per_pattern]
    rope_spec = pl.BlockSpec((PERM_ROWS, HEAD_DIM), lambda i: (i, 0))
    return pl.pallas_call(
        body, name="attn_bwd_finish", grid=(S // PERM_ROWS,),
        in_specs=([pl.BlockSpec(memory_space=pl.ANY)] + [_rm_block(d, aw) for d in DILATIONS] * 3
                  + [rope_spec, rope_spec]),
        out_specs=pl.BlockSpec((PERM_ROWS, 3 * aw), lambda i: (i, 0)),
        out_shape=jax.ShapeDtypeStruct((S, W), BF16),
        input_output_aliases={0: 0},
        compiler_params=pltpu.CompilerParams(dimension_semantics=("parallel",),
                                             vmem_limit_bytes=_vmem_limit(32 << 20)),
    )(dh, *grads, cos2, sin_bwd)


def _my_place():
    x, y, c = lax.axis_index("x"), lax.axis_index("y"), lax.axis_index("c")
    return x, y, c


def _flat(px, py, pc):
    return 4 * px + 2 * py + pc


def _shard_slice(ref, axis, idx, size):
    start = pl.multiple_of(idx * size, size)
    ix = [slice(None)] * len(ref.shape)
    ix[axis] = pl.ds(start, size)
    return ref.at[tuple(ix)]


_HBM_SPEC = pl.BlockSpec(memory_space=pltpu.HBM)
_SEM_SPEC = pl.BlockSpec(memory_space=pltpu.SEMAPHORE)
_ANY_SPEC = pl.BlockSpec(memory_space=pl.ANY)
_N_PEER = N_DEV - 1
SIBLING, SAME_CORE_NEIGHBOURS, OTHER_CORE_NEIGHBOURS, DIAGONAL = (1,), (2, 4), (3, 5), (6, 7)
PEER_ORDER = SIBLING + SAME_CORE_NEIGHBOURS + OTHER_CORE_NEIGHBOURS + DIAGONAL


def _peer_of(x, y, c, r):
    return (x ^ ((r >> 2) & 1), y ^ ((r >> 1) & 1), c ^ (r & 1))


class _Exchange:
    def __init__(self, name, part, slot):
        self.name, self.part, self.slot = name, part, slot

    def _copy(self, w, r, src, land, send_sems, recv_sems, sending):
        x, y, c = _my_place()
        peer = _peer_of(x, y, c, r)
        return pltpu.make_async_remote_copy(
            src_ref=self.part(w, src, _flat(*peer)),
            dst_ref=self.slot(w, land, _flat(x, y, c) if sending else _flat(*peer)),
            send_sem=send_sems.at[w * _N_PEER + r - 1], recv_sem=recv_sems.at[w * _N_PEER + r - 1],
            device_id=peer, device_id_type=MESH)

    def start(self, srcs, lands, after=None):
        n = len(srcs)
        n_after = 0 if after is None else 1

        def body(*refs):
            src, land = refs[:n], refs[n:2 * n]
            outs = refs[2 * n + n_after:]
            send_sems, recv_sems, local_sems, token = outs[0], outs[1], outs[2], outs[3 + 2 * n]
            for w in range(n):
                self._own_copy(w, src[w], land[w], local_sems).start()
                for r in PEER_ORDER:
                    self._copy(w, r, src[w], land[w], send_sems, recv_sems, True).start()
            token[...] = jnp.zeros_like(token)

        sems = pltpu.SemaphoreType.DMA((n * _N_PEER,))
        outs = pl.pallas_call(
            body, name=self.name + "_start",
            out_shape=(sems, sems, pltpu.SemaphoreType.DMA((n,)),
                       *[pltpu.HBM(t.shape, t.dtype) for t in list(srcs) + list(lands)],
                       jax.ShapeDtypeStruct((8, 128), F32)),
            in_specs=[_HBM_SPEC] * (2 * n) + [_ANY_SPEC] * n_after,
            out_specs=(_SEM_SPEC, _SEM_SPEC, _SEM_SPEC, *[_HBM_SPEC] * (2 * n), pl.BlockSpec(memory_space=pltpu.VMEM)),
            input_output_aliases={i: 3 + i for i in range(2 * n)},
            compiler_params=pltpu.CompilerParams(has_side_effects=pltpu.SideEffectType.DATAFLOW_SIDE_EFFECTING),
        )(*[pltpu.with_memory_space_constraint(t, pltpu.HBM) for t in list(srcs) + list(lands)],
          *([after] if n_after else []))
        return outs[0], outs[1], outs[2], outs[3:3 + n], outs[3 + n:3 + 2 * n], outs[3 + 2 * n]

    def _own_copy(self, w, src, land, local_sems):
        me = _flat(*_my_place())
        return pltpu.make_async_copy(self.part(w, src, me), self.slot(w, land, me), local_sems.at[w])

    def wait(self, started, after, peers=PEER_ORDER, own=True, tag=""):
        send_sems, recv_sems, local_sems, srcs, lands, token = started
        n = len(srcs)

        def body(*refs):
            src, land = refs[:n], refs[n:2 * n]
            s_sems, r_sems, l_sems = refs[2 * n], refs[2 * n + 1], refs[2 * n + 2]
            for w in range(n):
                if own:
                    self._own_copy(w, src[w], land[w], l_sems).wait()
                for r in peers:
                    cp = self._copy(w, r, src[w], land[w], s_sems, r_sems, False)
                    cp.wait_send()
                    cp.wait_recv()

        outs = pl.pallas_call(
            body, name=self.name + "_wait" + tag,
            out_shape=[pltpu.HBM(t.shape, t.dtype) for t in list(srcs) + list(lands)],
            in_specs=[_HBM_SPEC] * (2 * n) + [_SEM_SPEC, _SEM_SPEC, _SEM_SPEC, _ANY_SPEC],
            out_specs=[_HBM_SPEC] * (2 * n),
            input_output_aliases={i: i for i in range(2 * n)},
            compiler_params=pltpu.CompilerParams(has_side_effects=pltpu.SideEffectType.DATAFLOW_SIDE_EFFECTING),
        )(*srcs, *lands, send_sems, recv_sems, local_sems, after)
        return outs[n:], (send_sems, recv_sems, local_sems, outs[:n], outs[n:], token)


DIRECT_PEERS = (1, 2, 4, 6)
FORWARDED = (3, 5, 7)


class _TwoLevelGather:
    def __init__(self, name, axes, sizes):
        self.name, self.axes, self.sizes = name, axes, sizes

    def _place(self, w, land, dev):
        return _shard_slice(land, self.axes[w], dev, self.sizes[w])

    def _direct(self, w, r, src, land, sems, sending):
        x, y, c = _my_place()
        peer = _peer_of(x, y, c, r)
        k = w * len(DIRECT_PEERS) + DIRECT_PEERS.index(r)
        return pltpu.make_async_remote_copy(
            src_ref=src, dst_ref=self._place(w, land, _flat(x, y, c) if sending else _flat(*peer)),
            send_sem=sems[0].at[k], recv_sem=sems[1].at[k], device_id=peer, device_id_type=MESH)

    def _passed_on(self, w, f, land, sems, sending):
        x, y, c = _my_place()
        owner = _flat(*_peer_of(x, y, c, (f ^ 1) if sending else f))
        slot = self._place(w, land, owner)
        k = w * len(FORWARDED) + FORWARDED.index(f)
        return pltpu.make_async_remote_copy(
            src_ref=slot, dst_ref=slot, send_sem=sems[2].at[k], recv_sem=sems[3].at[k],
            device_id=(x, y, 1 - c), device_id_type=MESH)

    def _own(self, w, src, land, sems):
        return pltpu.make_async_copy(src, self._place(w, land, _flat(*_my_place())), sems[4].at[w])

    def _call(self, suffix, body, sems, srcs, lands, after, make_sems):
        n = len(srcs)
        n_after = 0 if after is None else 1
        bufs = list(srcs) + list(lands)

        def wrapped(*refs):
            ins = refs[:2 * n]
            rest = refs[2 * n + (n_after if make_sems else 0):]
            body(ins[:n], ins[n:], rest[:5], rest[-1] if make_sems else None)

        buf_shapes = [pltpu.HBM(t.shape, t.dtype) for t in bufs]
        if make_sems:
            sem_types = [pltpu.SemaphoreType.DMA((n * len(DIRECT_PEERS),))] * 2 \
                + [pltpu.SemaphoreType.DMA((n * len(FORWARDED),))] * 2 + [pltpu.SemaphoreType.DMA((n,))]
            outs = pl.pallas_call(
                wrapped, name=self.name + suffix,
                out_shape=(*sem_types, *buf_shapes, jax.ShapeDtypeStruct((8, 128), F32)),
                in_specs=[_HBM_SPEC] * (2 * n) + [_ANY_SPEC] * n_after,
                out_specs=(*[_SEM_SPEC] * 5, *[_HBM_SPEC] * (2 * n), pl.BlockSpec(memory_space=pltpu.VMEM)),
                input_output_aliases={i: 5 + i for i in range(2 * n)},
                compiler_params=pltpu.CompilerParams(has_side_effects=pltpu.SideEffectType.DATAFLOW_SIDE_EFFECTING),
            )(*[pltpu.with_memory_space_constraint(t, pltpu.HBM) for t in bufs], *([after] if n_after else []))
            return tuple(outs[:5]), outs[5:5 + n], outs[5 + n:5 + 2 * n], outs[5 + 2 * n]
        outs = pl.pallas_call(
            wrapped, name=self.name + suffix,
            out_shape=buf_shapes,
            in_specs=[_HBM_SPEC] * (2 * n) + [_SEM_SPEC] * 5 + [_ANY_SPEC] * n_after,
            out_specs=[_HBM_SPEC] * (2 * n),
            input_output_aliases={i: i for i in range(2 * n)},
            compiler_params=pltpu.CompilerParams(has_side_effects=pltpu.SideEffectType.DATAFLOW_SIDE_EFFECTING),
        )(*bufs, *sems, *([after] if n_after else []))
        return sems, outs[:n], outs[n:], None

    def start(self, srcs, lands, after=None):
        n = len(srcs)

        def body(src, land, sems, token):
            for w in range(n):
                self._own(w, src[w], land[w], sems).start()
                for r in DIRECT_PEERS:
                    self._direct(w, r, src[w], land[w], sems, True).start()
            token[...] = jnp.zeros_like(token)

        return self._call("_start", body, None, srcs, lands, after, True)

    def forward(self, state, after, which, tag=""):
        sems, srcs, lands, token = state
        n = len(srcs)

        def body(src, land, sem_refs, _):
            for w in range(n):
                for r in which:
                    self._direct(w, r, src[w], land[w], sem_refs, False).wait_recv()
                    self._passed_on(w, r | 1, land[w], sem_refs, True).start()

        sems, srcs, lands, _ = self._call("_forward" + tag, body, sems, srcs, lands, after, False)
        return sems, srcs, lands, token

    def wait(self, state, after, direct=(), passed_on=(), sends=False, tag=""):
        sems, srcs, lands, token = state
        n = len(srcs)

        def body(src, land, sem_refs, _):
            for w in range(n):
                for r in direct:
                    self._direct(w, r, src[w], land[w], sem_refs, False).wait_recv()
                for f in passed_on:
                    self._passed_on(w, f, land[w], sem_refs, False).wait_recv()
                if sends:
                    self._own(w, src[w], land[w], sem_refs).wait()
                    for r in DIRECT_PEERS:
                        self._direct(w, r, src[w], land[w], sem_refs, True).wait_send()
                    for f in FORWARDED:
                        self._passed_on(w, f, land[w], sem_refs, True).wait_send()

        sems, srcs, lands, _ = self._call("_wait" + tag, body, sems, srcs, lands, after, False)
        return lands, (sems, srcs, lands, token)


def _scatter_exchange(name, axes, shard_sizes):
    def part(w, src, dev):
        return src if axes[w] is None else _shard_slice(src, axes[w], dev, shard_sizes[w])
    return _Exchange(name, part, lambda w, land, dev: land.at[dev])


def _adamw(name, partials, w, m, v):
    R, C = w.shape
    tr = R
    while tr * C * 4 > (1 << 20) and tr % 16 == 0:
        tr //= 2

    def body(p_ref, w_ref, m_ref, v_ref, g_ref, d_ref, nm_ref, nv_ref):
        g = p_ref[0].astype(F32)
        for jdev in range(1, N_DEV):
            g = g + p_ref[jdev].astype(F32)
        nm = ADAM_B1 * m_ref[...] + (1.0 - ADAM_B1) * g
        nv = ADAM_B2 * v_ref[...] + (1.0 - ADAM_B2) * (g * g)
        m_hat = nm / (1.0 - ADAM_B1 ** ADAM_STEP)
        v_hat = nv / (1.0 - ADAM_B2 ** ADAM_STEP)
        g_ref[...] = g
        d_ref[...] = -ADAM_LR * (m_hat / (jnp.sqrt(v_hat) + ADAM_EPS) + ADAM_WD * w_ref[...])
        nm_ref[...] = nm
        nv_ref[...] = nv

    spec = pl.BlockSpec((tr, C), lambda i: (i, 0))
    return pl.pallas_call(
        body, name=name, grid=(R // tr,),
        in_specs=[pl.BlockSpec((N_DEV, tr, C), lambda i: (0, i, 0)), spec, spec, spec],
        out_specs=[spec] * 4,
        out_shape=[jax.ShapeDtypeStruct((R, C), F32)] * 4,
        compiler_params=pltpu.CompilerParams(dimension_semantics=("parallel",),
                                             vmem_limit_bytes=_vmem_limit(24 << 20)),
    )(partials, w, m, v)


def _local_step(x, cos2, sin_fwd, sin_bwd, project_in, mix_weights, ffn_weights, pool_scale, g_mix, b_mix, g_ff, b_ff,
                target, send):
    S, D = x.shape
    aw = pw = D // 2
    u_col_block = 3
    gate_col0 = 4 * aw

    xb = x.astype(BF16)
    h, w_in = project_in(xb)
    dilated = [d for d in DILATIONS if d > 1]
    qkv = {1: h[None], **dict(zip(dilated, _to_residue_major("qkv_to_rm", h, 0, 3 * aw)))}
    fwd = [_attn_fwd(qkv[d], d, aw) for d in DILATIONS]
    o_attn, l_tot = _attn_combine([f[0] for f in fwd], [f[1] for f in fwd], aw)
    w_pool, w_ba, w_bp, w_out = mix_weights(o_attn)
    p, y_pre, pm = _pool_fwd(h, w_pool, pool_scale, pw, u_col_block)
    y_attn = _branch_attn(o_attn, w_ba)
    y_pool, merged = _branch_pool_merge(pm, w_bp, h, y_attn, gate_col0)
    w1, w2 = ffn_weights(merged)
    x1, x1b, xhat1, rstd1 = _out_proj_ln(merged, w_out, x, g_mix, b_mix)
    a = _ffn_up(x1b, w1)
    z2 = _residual_matmul("ffn_down", a, w2, "nn", x1)
    dz2, dz2b, dg_ff, db_ff, loss = _ln_loss_bwd(z2, g_ff, b_ff, target)

    tok = send("ff2", [_grad_weight("grad_w_ff2", a, dz2b)])
    dh1 = _ffn_down_bwd(dz2b, w2, a, after=tok)
    tok = send("ff1", [_grad_weight("grad_w_ff1", x1b, dh1)])
    dy1 = _residual_matmul("ffn_up_bwd", dh1, w1, "nt", dz2, after=tok)
    dz1, dz1b, dg_mix, db_mix = _ln_bwd(dy1, xhat1, rstd1, g_mix)
    tok = send("out", [_grad_weight("grad_w_out", merged, dz1b)])
    dy_attn, dy_pool, dh = _out_proj_bwd(dz1b, w_out, h, y_attn, y_pool, gate_col0, after=tok)
    tok = send("branch", [_grad_weight("grad_w_branch_attn", o_attn, dy_attn),
                          _grad_weight("grad_w_branch_pool", pm, dy_pool)])
    d_out, stats = _branch_attn_bwd(dy_attn, w_ba, o_attn, l_tot, after=tok)
    dy_pre, d_scale = _branch_pool_bwd(dy_pool, w_bp, y_pre, pool_scale)
    dh, dw_pool = _pool_bwd(dh, dy_pre, p, w_pool, pw, u_col_block)
    d_outs = {1: d_out[None], **dict(zip(dilated, _to_residue_major("dout_to_rm", d_out, 0, aw)))}
    statss = {1: stats[None], **dict(zip(dilated, _to_residue_major("stats_to_rm", stats, 0, HEAD_DIM)))}
    per_pattern = [_attn_bwd(qkv[d], d_outs[d], statss[d], d, aw) for d in DILATIONS]
    dh = _attn_bwd_finish(dh, per_pattern, cos2, sin_bwd, aw)
    small = jnp.concatenate((d_scale, dg_mix, db_mix, dg_ff, db_ff), axis=-1)
    tok = send("in", [_grad_weight("grad_w_in", xb, dh), dw_pool.astype(BF16),
                      small.reshape(small.shape[-1] // HEAD_DIM, HEAD_DIM)])
    grad_x = _residual_matmul("in_proj_bwd", dh, w_in, "nt", dz1, after=tok)
    return loss, grad_x


def _rope_tables(positions):
    half = HEAD_DIM // 2
    inv_freq = ROPE_THETA ** (-jnp.arange(half, dtype=F32) / half)
    ang = positions.astype(F32)[:, None] * inv_freq
    cos, sin = jnp.cos(ang), jnp.sin(ang)
    cos2 = jnp.concatenate([cos, cos], axis=-1)
    sin_fwd = jnp.concatenate([-sin, sin], axis=-1)
    return cos2, sin_fwd, -sin_fwd


def kernel(x, positions, w_in, w_pool, pool_scale, w_branch_attn, w_branch_pool, w_out, ln_mix_g, ln_mix_b, w_ff1, w_ff2, ln_ff_g, ln_ff_b, loss_target, m_w_in, m_w_pool, m_pool_scale, m_w_branch_attn, m_w_branch_pool, m_w_out, m_ln_mix_g, m_ln_mix_b, m_w_ff1, m_w_ff2, m_ln_ff_g, m_ln_ff_b, v_w_in, v_w_pool, v_pool_scale, v_w_branch_attn, v_w_branch_pool, v_w_out, v_ln_mix_g, v_ln_mix_b, v_w_ff1, v_w_ff2, v_ln_ff_g, v_ln_ff_b):
    big_w = (w_in[0], w_pool[0], w_branch_attn[0], w_branch_pool[0], w_out[0], w_ff1[0], w_ff2[0])
    big_m = (m_w_in[0], m_w_pool[0], m_w_branch_attn[0], m_w_branch_pool[0], m_w_out[0], m_w_ff1[0], m_w_ff2[0])
    big_v = (v_w_in[0], v_w_pool[0], v_w_branch_attn[0], v_w_branch_pool[0], v_w_out[0], v_w_ff1[0], v_w_ff2[0])
    shard_axes = (1, 1, 1, 1, 0, 1, 0)
    small_w = (pool_scale, ln_mix_g, ln_mix_b, ln_ff_g, ln_ff_b)
    small_m = (m_pool_scale, m_ln_mix_g, m_ln_mix_b, m_ln_ff_g, m_ln_ff_b)
    small_v = (v_pool_scale, v_ln_mix_g, v_ln_mix_b, v_ln_ff_g, v_ln_ff_b)

    names = ("w_in", "w_pool", "w_branch_attn", "w_branch_pool", "w_out", "w_ff1", "w_ff2")
    axis_of = dict(zip(names, shard_axes))
    shard_of = dict(zip(names, [w.astype(BF16) for w in big_w]))

    def full_buffer(n):
        s, ax = shard_of[n], axis_of[n]
        full = list(s.shape)
        full[ax] *= N_DEV
        return lax.empty(tuple(full), s.dtype)

    def gather_group(tag, group, after):
        ex = _TwoLevelGather(tag, [axis_of[n] for n in group], [shard_of[n].shape[axis_of[n]] for n in group])
        return ex, ex.start([shard_of[n] for n in group], [full_buffer(n) for n in group], after)

    in_ex, in_state = gather_group("gather_in", ("w_in",), None)
    mix_ex, mix_state = gather_group("gather_mix", ("w_pool", "w_branch_attn", "w_branch_pool", "w_out"),
                                     in_state[-1])
    ffn_ex, ffn_state = gather_group("gather_ffn", ("w_ff1", "w_ff2"), mix_state[-1])
    states = {"mix": mix_state, "ffn": ffn_state}
    me = 4 * lax.axis_index("x") + 2 * lax.axis_index("y") + lax.axis_index("c")
    block_cols = shard_of["w_in"].shape[1]
    neighbours, diagonal = (2, 4), (6,)

    def project_in(xb):
        n_rope_blocks = 2 * (x.shape[-1] // 2) // block_cols

        def piece(tag, w, blocks, h, **kw):
            return _in_proj_piece("in_proj_" + tag, xb, w, jnp.stack(blocks).astype(jnp.int32), cos2, sin_fwd, h,
                                  block_cols, n_rope_blocks, **kw)

        h = piece("own", shard_of["w_in"], [me], None, own_shard=True, after=ffn_state[-1])
        (w_in_land,), state = in_ex.wait(in_state, h, direct=(1,), tag="_sibling")
        h = piece("sibling", w_in_land, [me ^ 1], h)
        state = in_ex.forward(state, h, neighbours, tag="_neighbours")
        h = piece("neighbours", state[2][0], [me ^ r for r in neighbours], h)
        state = in_ex.forward(state, h, diagonal, tag="_diagonal")
        h = piece("diagonal", state[2][0], [me ^ r for r in diagonal], h)
        states["mix"] = mix_ex.forward(states["mix"], h, neighbours + diagonal)
        (w_in_land,), _ = in_ex.wait(state, h, passed_on=FORWARDED, sends=True, tag="_passed_on")
        h = piece("passed_on", w_in_land, [me ^ f for f in FORWARDED], h)
        return h, w_in_land

    def mix_weights(after):
        states["ffn"] = ffn_ex.forward(states["ffn"], after, neighbours + diagonal)
        return mix_ex.wait(states["mix"], after, direct=(1,), passed_on=FORWARDED, sends=True)[0]

    def ffn_weights(after):
        return ffn_ex.wait(states["ffn"], after, direct=(1,), passed_on=FORWARDED, sends=True)[0]

    groups = {"ff2": ("w_ff2",), "ff1": ("w_ff1",), "out": ("w_out",),
              "branch": ("w_branch_attn", "w_branch_pool"), "in": ("w_in", "w_pool", "small")}
    sent = {}

    def send(key, grads_):
        axes = [axis_of.get(n) for n in groups[key]]
        sizes = [None if ax is None else g.shape[ax] // N_DEV for g, ax in zip(grads_, axes)]
        lands = []
        for g, ax, size in zip(grads_, axes, sizes):
            shard = list(g.shape)
            if ax is not None:
                shard[ax] = size
            lands.append(lax.empty((N_DEV, *shard), g.dtype))
        ex = _scatter_exchange("scatter_" + key, axes, sizes)
        sent[key] = (ex, ex.start(list(grads_), lands))
        return sent[key][1][-1]

    cos2, sin_fwd, sin_bwd = _rope_tables(positions[0])
    loss, grad_x = _local_step(
        x[0], cos2, sin_fwd, sin_bwd, project_in, mix_weights, ffn_weights, pool_scale, ln_mix_g, ln_mix_b,
        ln_ff_g, ln_ff_b, loss_target[0], send)

    state = dict(zip(names, zip(big_w, big_m, big_v)))
    n_small = sum(w.shape[-1] for w in small_w)
    small_2d = (n_small // HEAD_DIM, HEAD_DIM)
    state["small"] = tuple(jnp.concatenate(t, axis=-1).reshape(small_2d) for t in (small_w, small_m, small_v))
    grads, deltas, new_ms, new_vs = {}, {}, {}, {}
    after = grad_x
    for key in ("ff2", "ff1", "out", "branch", "in"):
        ex, started = sent[key]
        for n, part in zip(groups[key], ex.wait(started, after)[0]):
            w, m, v = state[n]
            r2 = (-1, w.shape[-1])
            w2d = w.reshape(r2)
            res = _adamw("adamw_" + n, part.reshape((N_DEV,) + w2d.shape), w2d, m.reshape(r2), v.reshape(r2))
            after = res[0]
            if n == "small":
                small_out = [t.reshape(1, n_small) for t in res]
            else:
                grads[n], deltas[n], new_ms[n], new_vs[n] = (t.reshape((1,) + w.shape) for t in res)
    small_names = ("pool_scale", "ln_mix_g", "ln_mix_b", "ln_ff_g", "ln_ff_b")
    off = 0
    for n, w in zip(small_names, small_w):
        width = w.shape[-1]
        grads[n], deltas[n], new_ms[n], new_vs[n] = (t[:, off:off + width] for t in small_out)
        off += width

    order = ("w_in", "w_pool", "pool_scale", "w_branch_attn", "w_branch_pool", "w_out", "ln_mix_g", "ln_mix_b",
             "w_ff1", "w_ff2", "ln_ff_g", "ln_ff_b")
    total_loss = lax.psum(loss[0, 0], ("x", "y", "c"))
    return (total_loss, grad_x[None], *[grads[n] for n in order], *[deltas[n] for n in order],
            *[new_ms[n] for n in order], *[new_vs[n] for n in order])
```

```python
import functools

import jax
import jax.numpy as jnp
from jax import lax
from jax.experimental import pallas as pl
from jax.experimental.pallas import tpu as pltpu

F32 = jnp.float32
BF16 = jnp.bfloat16

N_DEV = 8
HEAD_DIM = 128
SUB_BLOCK = 128
DILATIONS = (1, 4, 16)
POOL_WINDOWS = (2, 4, 8, 16)
MAX_POOL_WINDOW = 16
POOL_HALO = 128
PERM_ROWS = 256
K_TILE = 1024
LN_EPS = 1e-5
DEEPNORM_ALPHA = 2.0 ** 0.25
ROPE_THETA = 10000.0
ATTN_SCALE = HEAD_DIM ** -0.5
ADAM_LR, ADAM_B1, ADAM_B2, ADAM_EPS, ADAM_WD, ADAM_STEP = 0.001, 0.9, 0.999, 1e-08, 0.01, 10
NEG_BIG = -1e30
VMEM_CAP_V7X = 64 * 1024 * 1024
MESH = pl.DeviceIdType.MESH


def _vmem_limit(est_bytes):
    return int(min(max(est_bytes * 5 // 4 + (4 << 20), 16 << 20), VMEM_CAP_V7X - (6 << 20)))


def _nbytes(shape, dtype):
    n = 1
    for s in shape:
        n *= s
    return n * jnp.dtype(dtype).itemsize


def _mm(name, a, b, form, tiles, outs, epi, extras=(), sequential=False, after=None):
    tm, tn, tk = tiles
    if form == "nn":
        (M, K), (K2, N) = a.shape, b.shape
    elif form == "nt":
        (M, K), (N, K2) = a.shape, b.shape
    else:
        (K, M), (K2, N) = a.shape, b.shape
    assert K == K2, (name, a.shape, b.shape)
    tm, tn, tk = min(tm, M), min(tn, N), min(tk, K)
    assert M % tm == 0 and N % tn == 0 and K % tk == 0, (name, M, N, K, tm, tn, tk)
    grid = (M // tm, N // tn, K // tk)
    nk = grid[2]
    if form == "nn":
        a_spec = pl.BlockSpec((tm, tk), lambda i, j, k: (i, k))
        b_spec = pl.BlockSpec((tk, tn), lambda i, j, k: (k, j))
        contract = ((1,), (0,))
    elif form == "nt":
        a_spec = pl.BlockSpec((tm, tk), lambda i, j, k: (i, k))
        b_spec = pl.BlockSpec((tn, tk), lambda i, j, k: (j, k))
        contract = ((1,), (1,))
    else:
        a_spec = pl.BlockSpec((tk, tm), lambda i, j, k: (k, i))
        b_spec = pl.BlockSpec((tk, tn), lambda i, j, k: (k, j))
        contract = ((0,), (0,))
    n_ex, n_out = len(extras), len(outs)
    n_after = 0 if after is None else 1

    def body(a_ref, b_ref, *rest):
        ex_refs = rest[:n_ex]
        rest = rest[n_ex + n_after:]
        out_refs = rest[:n_out]
        i, j, k = pl.program_id(0), pl.program_id(1), pl.program_id(2)

        def prod():
            return lax.dot_general(a_ref[...].astype(BF16), b_ref[...].astype(BF16),
                                   (contract, ((), ())), preferred_element_type=F32)

        if nk == 1:
            epi(prod(), ex_refs, out_refs, i, j)
        else:
            acc = rest[n_out]

            @pl.when(k == 0)
            def _():
                acc[...] = prod()

            @pl.when(jnp.logical_and(k > 0, k < nk - 1))
            def _():
                acc[...] += prod()

            @pl.when(k == nk - 1)
            def _():
                epi(acc[...] + prod(), ex_refs, out_refs, i, j)

    est = 2 * (_nbytes(a_spec.block_shape, a.dtype) + _nbytes(b_spec.block_shape, b.dtype))
    est += sum(2 * _nbytes(bs, arr.dtype) for arr, bs, _ in extras)
    est += sum(2 * _nbytes(bs, dt) for _, dt, bs, _ in outs)
    est += 4 * tm * tn * 4
    sem = ("arbitrary",) * 3 if sequential else ("parallel", "parallel", "arbitrary")
    return pl.pallas_call(
        body, name=name, grid=grid,
        in_specs=([a_spec, b_spec] + [pl.BlockSpec(bs, im) for _, bs, im in extras]
                  + [pl.BlockSpec(memory_space=pl.ANY)] * n_after),
        out_specs=[pl.BlockSpec(bs, im) for _, _, bs, im in outs],
        out_shape=[jax.ShapeDtypeStruct(sh, dt) for sh, dt, _, _ in outs],
        scratch_shapes=[pltpu.VMEM((tm, tn), F32)] if nk > 1 else [],
        compiler_params=pltpu.CompilerParams(dimension_semantics=sem, vmem_limit_bytes=_vmem_limit(est)),
    )(a, b, *[arr for arr, _, _ in extras], *([after] if n_after else []))


def _tile_out(shape, dtype, tm, tn):
    return (shape, dtype, (tm, tn), lambda i, j, k: (i, j))


def _row_sum_out(width):
    return ((1, width), F32, (1, width), lambda i, j, k: (0, 0))


def _accumulate_rows(ref, value, i):
    @pl.when(i == 0)
    def _():
        ref[...] = value

    @pl.when(i > 0)
    def _():
        ref[...] += value


def _layer_norm_bwd(dy, xhat, rstd, g):
    dxh = dy * g
    m1 = jnp.mean(dxh, axis=-1, keepdims=True)
    m2 = jnp.mean(dxh * xhat, axis=-1, keepdims=True)
    return rstd * (dxh - m1 - xhat * m2)


def _rope_apply(t, cos2, sin_signed):
    return t * cos2 + pltpu.roll(t, HEAD_DIM // 2, axis=1) * sin_signed


def _in_proj_piece(name, xb, w_in, col_blocks, cos2, sin_fwd, h_so_far, block_cols, n_rope_blocks, own_shard=False,
                   after=None):
    S, D = xb.shape
    W = w_in.shape[1] * (N_DEV if own_shard else 1)
    tm = min(1024, S)
    n_blocks = col_blocks.shape[0]

    def body(cols_ref, x_ref, w_ref, cos_ref, sin_ref, *rest):
        h_ref = rest[-1]
        j = pl.program_id(1)

        @pl.when(cols_ref[j] < n_rope_blocks)
        def _():
            acc = _dot_nn(x_ref[...], w_ref[...])
            c, s = cos_ref[...], sin_ref[...]
            for hd in range(block_cols // HEAD_DIM):
                sl = slice(hd * HEAD_DIM, (hd + 1) * HEAD_DIM)
                h_ref[:, sl] = _rope_apply(acc[:, sl], c, s).astype(BF16)

        @pl.when(cols_ref[j] >= n_rope_blocks)
        def _():
            h_ref[...] = _dot_nn(x_ref[...], w_ref[...]).astype(BF16)

    row = pl.BlockSpec((tm, HEAD_DIM), lambda i, j, cols: (i, 0))
    carried = ([] if h_so_far is None else [h_so_far]) + ([] if after is None else [after])
    est = 2 * (tm * D * 2 + D * block_cols * 2 + tm * block_cols * 2 + 2 * tm * HEAD_DIM * 4) + 3 * tm * block_cols * 4
    return pl.pallas_call(
        body, name=name,
        grid_spec=pltpu.PrefetchScalarGridSpec(
            num_scalar_prefetch=1, grid=(S // tm, n_blocks),
            in_specs=[pl.BlockSpec((tm, D), lambda i, j, cols: (i, 0)),
                      pl.BlockSpec((D, block_cols), lambda i, j, cols: (0, 0 if own_shard else cols[j])), row, row]
                     + [pl.BlockSpec(memory_space=pl.ANY)] * len(carried),
            out_specs=pl.BlockSpec((tm, block_cols), lambda i, j, cols: (i, cols[j]))),
        out_shape=jax.ShapeDtypeStruct((S, W), BF16),
        input_output_aliases={} if h_so_far is None else {5: 0},
        compiler_params=pltpu.CompilerParams(dimension_semantics=("parallel", "arbitrary"),
                                             vmem_limit_bytes=_vmem_limit(est)),
    )(col_blocks, xb, w_in, cos2, sin_fwd, *carried)


def _attn_mask(mb):
    qi = lax.broadcasted_iota(jnp.int32, (SUB_BLOCK, 2 * SUB_BLOCK), 0)
    kj = lax.broadcasted_iota(jnp.int32, (SUB_BLOCK, 2 * SUB_BLOCK), 1)
    prev = jnp.logical_and(jnp.logical_and(kj < SUB_BLOCK, kj >= qi), mb > 0)
    cur = jnp.logical_and(kj >= SUB_BLOCK, kj - SUB_BLOCK <= qi)
    return jnp.logical_or(prev, cur)


def _both_blocks(prev_ref, cur_ref, sl):
    return jnp.concatenate([prev_ref[:, sl], cur_ref[:, sl]], axis=0)


def _dot_nt(a, b):
    return lax.dot_general(a, b, (((1,), (1,)), ((), ())), preferred_element_type=F32)


def _dot_tn(a, b):
    return lax.dot_general(a, b, (((0,), (0,)), ((), ())), preferred_element_type=F32)


def _dot_nn(a, b):
    return lax.dot_general(a, b, (((1,), (0,)), ((), ())), preferred_element_type=F32)


def _perm_matrix(d, to_residue_major):
    g = PERM_ROWS // d
    i = lax.broadcasted_iota(jnp.int32, (PERM_ROWS, PERM_ROWS), 0)
    j = lax.broadcasted_iota(jnp.int32, (PERM_ROWS, PERM_ROWS), 1)
    if to_residue_major:
        hit = j == (i % g) * d + i // g
    else:
        hit = j == (i % d) * g + i // d
    return hit.astype(BF16)


def _permute_rows(perm, x, terms=3):
    if x.dtype == BF16:
        return _dot_nn(perm, x)
    hi = x.astype(BF16)
    r1 = x - hi.astype(F32)
    mid = r1.astype(BF16)
    out = _dot_nn(perm, hi) + _dot_nn(perm, mid)
    if terms == 3:
        out = out + _dot_nn(perm, (r1 - mid.astype(F32)).astype(BF16))
    return out


def _rm_block(d, width):
    return pl.BlockSpec((d, PERM_ROWS // d, width), lambda i: (0, i, 0))


def _to_residue_major(name, x, col_block, width):
    S = x.shape[0]
    dils = [d for d in DILATIONS if d > 1]
    chunk = min(width, 1024)

    def body(x_ref, *out_refs):
        for d, o_ref in zip(dils, out_refs):
            perm = _perm_matrix(d, True)
            for c0 in range(0, width, chunk):
                cw = min(chunk, width - c0)
                y = _permute_rows(perm, x_ref[:, c0:c0 + cw])
                o_ref[:, :, c0:c0 + cw] = y.astype(x.dtype).reshape(d, PERM_ROWS // d, cw)

    return pl.pallas_call(
        body, name=name, grid=(S // PERM_ROWS,),
        in_specs=[pl.BlockSpec((PERM_ROWS, width), lambda i: (i, col_block))],
        out_specs=[_rm_block(d, width) for d in dils],
        out_shape=[jax.ShapeDtypeStruct((d, S // d, width), x.dtype) for d in dils],
        compiler_params=pltpu.CompilerParams(dimension_semantics=("parallel",),
                                             vmem_limit_bytes=_vmem_limit(32 << 20)),
    )(x)


def _put_column(tile, col, value):
    lane = lax.broadcasted_iota(jnp.int32, tile.shape, 1)
    return jnp.where(lane == col, value, tile)


def _attn_fwd(qkv, d, aw):
    _, rows, _ = qkv.shape
    n_heads = aw // HEAD_DIM
    nb = rows // SUB_BLOCK

    def body(q_ref, kc_ref, vc_ref, o_ref, lse_ref, kp_ref, vp_ref, s_buf, p_buf):
        step = pl.program_id(0)

        @pl.when(step == 0)
        def _():
            kp_ref[...] = jnp.zeros_like(kp_ref)
            vp_ref[...] = jnp.zeros_like(vp_ref)

        mask = _attn_mask(step % nb)
        for hd in range(n_heads):
            sl = slice(hd * HEAD_DIM, (hd + 1) * HEAD_DIM)
            s_buf[hd] = _dot_nt(q_ref[:, sl], _both_blocks(kp_ref, kc_ref, sl))
        lse_tile = jnp.zeros((SUB_BLOCK, HEAD_DIM), F32)
        inv_tile = jnp.zeros((SUB_BLOCK, HEAD_DIM), F32)
        for hd in range(n_heads):
            s = jnp.where(mask, s_buf[hd] * ATTN_SCALE, NEG_BIG)
            m = jnp.max(s, axis=-1, keepdims=True)
            p = jnp.exp(s - m)
            l = jnp.sum(p, axis=-1, keepdims=True)
            p_buf[hd] = p.astype(BF16)
            lse_tile = _put_column(lse_tile, hd, m + jnp.log(l))
            inv_tile = _put_column(inv_tile, hd, 1.0 / l)
        lse_ref[...] = lse_tile
        for hd in range(n_heads):
            sl = slice(hd * HEAD_DIM, (hd + 1) * HEAD_DIM)
            o = _dot_nn(p_buf[hd], _both_blocks(vp_ref, vc_ref, sl))
            o_ref[:, sl] = o * inv_tile[:, hd:hd + 1]
        kp_ref[...] = kc_ref[...]
        vp_ref[...] = vc_ref[...]

    def block(col, width):
        return pl.BlockSpec((SUB_BLOCK, width), lambda s: (s, col))

    qkv2 = qkv.reshape(d * rows, qkv.shape[-1])
    o, lse = pl.pallas_call(
        body, name=f"attn_fwd_d{d}", grid=(d * nb,),
        in_specs=[block(0, aw), block(1, aw), block(2, aw)],
        out_specs=[block(0, aw), block(0, HEAD_DIM)],
        out_shape=[jax.ShapeDtypeStruct((d * rows, aw), F32), jax.ShapeDtypeStruct((d * rows, HEAD_DIM), F32)],
        scratch_shapes=[pltpu.VMEM((SUB_BLOCK, aw), BF16), pltpu.VMEM((SUB_BLOCK, aw), BF16),
                        pltpu.VMEM((n_heads, SUB_BLOCK, 2 * SUB_BLOCK), F32),
                        pltpu.VMEM((n_heads, SUB_BLOCK, 2 * SUB_BLOCK), BF16)],
        compiler_params=pltpu.CompilerParams(dimension_semantics=("arbitrary",),
                                             vmem_limit_bytes=_vmem_limit(16 << 20)),
    )(qkv2, qkv2, qkv2)
    return o.reshape(d, rows, aw), lse.reshape(d, rows, HEAD_DIM)


def _attn_combine(outs, lses, aw):
    S = outs[0].shape[1]
    n_heads = aw // HEAD_DIM
    n_pat = len(DILATIONS)

    def body(*refs):
        o_refs, l_refs = refs[:n_pat], refs[n_pat:2 * n_pat]
        o_ref, lt_ref = refs[2 * n_pat], refs[2 * n_pat + 1]
        o_nat, l_nat = [], []
        for d, o_r, l_r in zip(DILATIONS, o_refs, l_refs):
            o_p = o_r[...].reshape(PERM_ROWS, aw)
            l_p = l_r[...].reshape(PERM_ROWS, HEAD_DIM)
            if d > 1:
                perm = _perm_matrix(d, False)
                o_p, l_p = _permute_rows(perm, o_p, terms=2), _permute_rows(perm, l_p)
            o_nat.append(o_p)
            l_nat.append(l_p)
        mx = functools.reduce(jnp.maximum, l_nat)
        es = [jnp.exp(l_p - mx) for l_p in l_nat]
        den = functools.reduce(jnp.add, es)
        lt_ref[...] = mx + jnp.log(den)
        ws = [e / den for e in es]
        for hd in range(n_heads):
            sl = slice(hd * HEAD_DIM, (hd + 1) * HEAD_DIM)
            o = ws[0][:, hd:hd + 1] * o_nat[0][:, sl]
            for pi in range(1, n_pat):
                o = o + ws[pi][:, hd:hd + 1] * o_nat[pi][:, sl]
            o_ref[:, sl] = o.astype(BF16)

    return pl.pallas_call(
        body, name="attn_combine", grid=(S // PERM_ROWS,),
        in_specs=[_rm_block(d, aw) for d in DILATIONS] + [_rm_block(d, HEAD_DIM) for d in DILATIONS],
        out_specs=[pl.BlockSpec((PERM_ROWS, aw), lambda i: (i, 0)), pl.BlockSpec((PERM_ROWS, HEAD_DIM), lambda i: (i, 0))],
        out_shape=[jax.ShapeDtypeStruct((S, aw), BF16), jax.ShapeDtypeStruct((S, HEAD_DIM), F32)],
        compiler_params=pltpu.CompilerParams(dimension_semantics=("parallel",),
                                             vmem_limit_bytes=_vmem_limit(40 << 20)),
    )(*outs, *lses)


def _band(tm, width, w, row_offset, transpose):
    t = lax.broadcasted_iota(jnp.int32, (tm, width), 0)
    u = lax.broadcasted_iota(jnp.int32, (tm, width), 1)
    dist = (u - t - row_offset) if transpose else (t + row_offset - u)
    return jnp.logical_and(dist >= 0, dist < w).astype(BF16)


def _pool_fwd(h, w_pool, pool_scale, pw, u_col_block):
    S, W = h.shape
    n_groups = len(POOL_WINDOWS)
    gw = pw // n_groups
    tm = min(512, S)
    halo_per_tile = tm // POOL_HALO

    def body(uc_ref, uh_ref, w_ref, sc_ref, p_ref, y_ref, pm_ref):
        i = pl.program_id(0)
        t_abs = i * tm + lax.broadcasted_iota(jnp.int32, (tm, 1), 0)
        for g, w in enumerate(POOL_WINDOWS):
            sl = slice(g * gw, (g + 1) * gw)
            uc = uc_ref[:, sl]
            uh = jnp.where(i > 0, uh_ref[:, sl], jnp.zeros((POOL_HALO, gw), BF16))
            ssum = _dot_nn(_band(tm, tm, w, 0, False), uc) + _dot_nn(_band(tm, POOL_HALO, w, POOL_HALO, False), uh)
            cnt = jnp.minimum(t_abs + 1, w).astype(F32)
            p = (ssum / cnt - uc.astype(F32)).astype(BF16)
            y = _dot_nn(p, w_ref[g])
            p_ref[:, sl] = p
            y_ref[:, sl] = y.astype(BF16)
            pm_ref[:, sl] = (y * sc_ref[:, sl]).astype(BF16)

    row = pl.BlockSpec((tm, pw), lambda i: (i, 0))
    return pl.pallas_call(
        body, name="pool_fwd", grid=(S // tm,),
        in_specs=[pl.BlockSpec((tm, pw), lambda i: (i, u_col_block)),
                  pl.BlockSpec((POOL_HALO, pw), lambda i: (jnp.maximum(i * halo_per_tile - 1, 0), u_col_block)),
                  pl.BlockSpec((n_groups, gw, gw), lambda i: (0, 0, 0)),
                  pl.BlockSpec((1, pw), lambda i: (0, 0))],
        out_specs=[row, row, row],
        out_shape=[jax.ShapeDtypeStruct((S, pw), BF16)] * 3,
        compiler_params=pltpu.CompilerParams(dimension_semantics=("parallel",),
                                             vmem_limit_bytes=_vmem_limit(24 << 20)),
    )(h, h, w_pool, pool_scale)


def _branch_attn(o_attn, w_ba):
    S, _ = o_attn.shape
    D = w_ba.shape[1]
    tm, tn = min(1024, S), D

    def epi(acc, ex, out, i, j):
        out[0][...] = acc.astype(BF16)

    (y,) = _mm("branch_attn", o_attn, w_ba, "nn", (tm, tn, 1024), [_tile_out((S, D), BF16, tm, tn)], epi)
    return y


def _branch_pool_merge(pm, w_bp, h, y_attn, gate_col0):
    S, _ = pm.shape
    D = w_bp.shape[1]
    tm, tn = min(512, S), D
    ga0, gp0 = gate_col0 // tn, (gate_col0 + D) // tn

    def epi(acc, ex, out, i, j):
        ga_ref, gp_ref, ya_ref = ex
        yp_ref, mg_ref = out
        yp = acc.astype(BF16)
        yp_ref[...] = yp
        mg = (jax.nn.sigmoid(ga_ref[...]).astype(F32) * ya_ref[...].astype(F32)
              + jax.nn.sigmoid(gp_ref[...]).astype(F32) * acc)
        mg_ref[...] = mg.astype(BF16)

    y_pool, merged = _mm(
        "branch_pool_merge", pm, w_bp, "nn", (tm, tn, 1024),
        [_tile_out((S, D), BF16, tm, tn), _tile_out((S, D), BF16, tm, tn)], epi,
        extras=[(h, (tm, tn), lambda i, j, k: (i, ga0 + j)), (h, (tm, tn), lambda i, j, k: (i, gp0 + j)),
                (y_attn, (tm, tn), lambda i, j, k: (i, j))])
    return y_pool, merged


def _layer_norm_rows(z, g, b):
    mu = jnp.mean(z, axis=-1, keepdims=True)
    zc = z - mu
    var = jnp.mean(zc * zc, axis=-1, keepdims=True)
    rstd = lax.rsqrt(var + LN_EPS)
    xhat = zc * rstd
    return xhat * g + b, xhat, rstd


def _out_proj_ln(merged, w_out, x, g, b):
    S, D = x.shape
    tm = min(256, S)

    def epi(acc, ex, out, i, j):
        x_ref, g_ref, b_ref = ex
        x1_ref, x1b_ref, xh_ref, rs_ref = out
        y, xhat, rstd = _layer_norm_rows(DEEPNORM_ALPHA * x_ref[...] + acc, g_ref[...], b_ref[...])
        x1_ref[...] = y
        x1b_ref[...] = y.astype(BF16)
        xh_ref[...] = xhat
        rs_ref[...] = jnp.broadcast_to(rstd, (tm, HEAD_DIM))

    row = lambda i, j, k: (i, 0)
    vec = lambda i, j, k: (0, 0)
    return _mm("out_proj_ln", merged, w_out, "nn", (tm, D, D),
               [((S, D), F32, (tm, D), row), ((S, D), BF16, (tm, D), row), ((S, D), F32, (tm, D), row),
                ((S, HEAD_DIM), F32, (tm, HEAD_DIM), row)], epi,
               extras=[(x, (tm, D), row), (g, (1, D), vec), (b, (1, D), vec)])


def _ffn_up(x1b, w1):
    S, D = x1b.shape
    F = w1.shape[1]
    tm, tn = min(1024, S), min(2048, F)

    def epi(acc, ex, out, i, j):
        r = jnp.maximum(acc, 0.0)
        out[0][...] = (r * r).astype(BF16)

    (a,) = _mm("ffn_up", x1b, w1, "nn", (tm, tn, 2 * K_TILE), [_tile_out((S, F), BF16, tm, tn)], epi)
    return a


def _residual_matmul(name, a, w, form, resid, after=None):
    S, D = resid.shape
    tm, tn = min(1024, S), min(2048, D)

    def epi(acc, ex, out, i, j):
        out[0][...] = DEEPNORM_ALPHA * ex[0][...] + acc

    (z,) = _mm(name, a, w, form, (tm, tn, K_TILE), [_tile_out((S, D), F32, tm, tn)], epi,
               extras=[(resid, (tm, tn), lambda i, j, k: (i, j))], after=after)
    return z


def _row_kernel(name, body, row_inputs, vec_inputs, row_outputs, sum_widths, tr):
    S = row_inputs[0].shape[0]
    row = lambda w: pl.BlockSpec((tr, w), lambda i: (i, 0))
    vec = lambda w: pl.BlockSpec((1, w), lambda i: (0, 0))

    def wrapped(*refs):
        body(pl.program_id(0), *refs)

    return pl.pallas_call(
        wrapped, name=name, grid=(S // tr,),
        in_specs=[row(t.shape[1]) for t in row_inputs] + [vec(t.shape[1]) for t in vec_inputs],
        out_specs=[row(w) for w, _ in row_outputs] + [vec(w) for w in sum_widths],
        out_shape=([jax.ShapeDtypeStruct((S, w), dt) for w, dt in row_outputs]
                   + [jax.ShapeDtypeStruct((1, w), F32) for w in sum_widths]),
        compiler_params=pltpu.CompilerParams(dimension_semantics=("arbitrary",),
                                             vmem_limit_bytes=_vmem_limit(40 << 20)),
    )(*row_inputs, *vec_inputs)


def _ln_loss_bwd(z2, g, b, target):
    S, D = z2.shape

    def body(i, z_ref, t_ref, g_ref, b_ref, dz_ref, dzb_ref, dg_ref, db_ref, loss_ref):
        gv = g_ref[...]
        y, xhat, rstd = _layer_norm_rows(z_ref[...], gv, b_ref[...])
        err = y - t_ref[...]
        loss = 0.5 * jnp.sum(jnp.mean(err * err, axis=-1, keepdims=True), axis=0, keepdims=True)
        dy = err * (1.0 / D)
        dz = _layer_norm_bwd(dy, xhat, rstd, gv)
        dz_ref[...] = dz
        dzb_ref[...] = dz.astype(BF16)
        _accumulate_rows(dg_ref, jnp.sum(dy * xhat, axis=0, keepdims=True), i)
        _accumulate_rows(db_ref, jnp.sum(dy, axis=0, keepdims=True), i)
        _accumulate_rows(loss_ref, jnp.broadcast_to(loss, (1, HEAD_DIM)), i)

    return _row_kernel("ln_loss_bwd", body, [z2, target], [g, b], [(D, F32), (D, BF16)], [D, D, HEAD_DIM],
                       min(256, S))


def _ln_bwd(dy, xhat, rstd, g):
    S, D = dy.shape

    def body(i, dy_ref, xh_ref, rs_ref, g_ref, dz_ref, dzb_ref, dg_ref, db_ref):
        dyv, xhat_v = dy_ref[...], xh_ref[...]
        dz = _layer_norm_bwd(dyv, xhat_v, rs_ref[:, :1], g_ref[...])
        dz_ref[...] = dz
        dzb_ref[...] = dz.astype(BF16)
        _accumulate_rows(dg_ref, jnp.sum(dyv * xhat_v, axis=0, keepdims=True), i)
        _accumulate_rows(db_ref, jnp.sum(dyv, axis=0, keepdims=True), i)

    return _row_kernel("ln_bwd", body, [dy, xhat, rstd], [g], [(D, F32), (D, BF16)], [D, D], min(256, S))


def _grad_weight(name, act, cot):
    M, N = act.shape[1], cot.shape[1]
    tm, tn = min(1024, M), min(2048, N)

    def epi(acc, ex, out, i, j):
        out[0][...] = acc.astype(BF16)

    (g,) = _mm(name, act, cot, "tn", (tm, tn, 2 * K_TILE), [_tile_out((M, N), BF16, tm, tn)], epi)
    return g


def _ffn_down_bwd(dz2b, w2, a, after=None):
    S, D = dz2b.shape
    F = w2.shape[0]
    tm, tn = min(1024, S), min(2048, F)

    def epi(acc, ex, out, i, j):
        out[0][...] = (acc * (2.0 * jnp.sqrt(ex[0][...])).astype(F32)).astype(BF16)

    (dh1,) = _mm("ffn_down_bwd", dz2b, w2, "nt", (tm, tn, 2 * K_TILE), [_tile_out((S, F), BF16, tm, tn)], epi,
                 extras=[(a, (tm, tn), lambda i, j, k: (i, j))], after=after)
    return dh1


def _out_proj_bwd(dz1b, w_out, h, y_attn, y_pool, gate_col0, after=None):
    S, D = dz1b.shape
    W = h.shape[1]
    tm = min(256, S)
    assert gate_col0 == 2 * D and W == 4 * D

    def epi(acc, ex, out, i, j):
        gates_ref, ya_ref, yp_ref = ex
        dya_ref, dyp_ref, dh_ref = out
        sa = jax.nn.sigmoid(gates_ref[:, :D])
        sp = jax.nn.sigmoid(gates_ref[:, D:])
        dya_ref[...] = (acc * sa.astype(F32)).astype(BF16)
        dyp_ref[...] = (acc * sp.astype(F32)).astype(BF16)
        dh_ref[:, :D] = (acc * (ya_ref[...] * (sa * (1.0 - sa))).astype(F32)).astype(BF16)
        dh_ref[:, D:] = (acc * (yp_ref[...] * (sp * (1.0 - sp))).astype(F32)).astype(BF16)

    row = lambda i, j, k: (i, 0)
    return _mm("out_proj_bwd", dz1b, w_out, "nt", (tm, D, D),
               [((S, D), BF16, (tm, D), row), ((S, D), BF16, (tm, D), row),
                ((S, W), BF16, (tm, 2 * D), lambda i, j, k: (i, 1))], epi,
               extras=[(h, (tm, 2 * D), lambda i, j, k: (i, 1)), (y_attn, (tm, D), row), (y_pool, (tm, D), row)],
               after=after)


def _branch_attn_bwd(dy_attn, w_ba, o_attn, l_tot, after=None):
    S, D = dy_attn.shape
    aw = w_ba.shape[0]
    n_heads = aw // HEAD_DIM
    tm = min(512, S)

    def epi(acc, ex, out, i, j):
        do_ref, st_ref = out
        do_ref[...] = acc.astype(BF16)
        o = ex[0][...].astype(F32)
        stats = ex[1][...]
        for hd in range(n_heads):
            sl = slice(hd * HEAD_DIM, (hd + 1) * HEAD_DIM)
            stats = _put_column(stats, n_heads + hd, jnp.sum(acc[:, sl] * o[:, sl], axis=-1, keepdims=True))
        st_ref[...] = stats

    row = lambda i, j, k: (i, 0)
    return _mm("branch_attn_bwd", dy_attn, w_ba, "nt", (tm, aw, D),
               [((S, aw), BF16, (tm, aw), row), ((S, HEAD_DIM), F32, (tm, HEAD_DIM), row)], epi,
               extras=[(o_attn, (tm, aw), row), (l_tot, (tm, HEAD_DIM), row)], after=after)


def _branch_pool_bwd(dy_pool, w_bp, y_pre, pool_scale):
    S, D = dy_pool.shape
    pw = w_bp.shape[0]
    tm = min(512, S)

    def epi(acc, ex, out, i, j):
        y_ref, sc_ref = ex
        dyp_ref, dsc_ref = out
        dyp_ref[...] = (acc * sc_ref[...]).astype(BF16)
        _accumulate_rows(dsc_ref, jnp.sum(acc * y_ref[...].astype(F32), axis=0, keepdims=True), i)

    row = lambda i, j, k: (i, 0)
    return _mm("branch_pool_bwd", dy_pool, w_bp, "nt", (tm, pw, D),
               [((S, pw), BF16, (tm, pw), row), _row_sum_out(pw)], epi,
               extras=[(y_pre, (tm, pw), row), (pool_scale, (1, pw), lambda i, j, k: (0, 0))],
               sequential=True)


def _pool_bwd(dh, dy_pre, p, w_pool, pw, u_col_block):
    S, W = dh.shape
    n_groups = len(POOL_WINDOWS)
    gw = pw // n_groups
    tm = min(512, S)
    n_tiles = S // tm
    halo_per_tile = tm // POOL_HALO
    n_halo_blocks = S // POOL_HALO

    def body(dh_in_ref, dyc_ref, dyh_ref, p_ref, w_ref, dh_ref, dw_ref):
        del dh_in_ref
        i = pl.program_id(0)
        t_cur = i * tm + lax.broadcasted_iota(jnp.int32, (tm, 1), 0)
        t_halo = (i + 1) * tm + lax.broadcasted_iota(jnp.int32, (POOL_HALO, 1), 0)
        for g, w in enumerate(POOL_WINDOWS):
            sl = slice(g * gw, (g + 1) * gw)
            wg = w_ref[g]
            dyc = dyc_ref[:, sl]
            dyh = jnp.where(i < n_tiles - 1, dyh_ref[:, sl], jnp.zeros((POOL_HALO, gw), BF16))
            dp_cur = _dot_nt(dyc, wg)
            dp_halo = _dot_nt(dyh, wg)
            dpc_cur = (dp_cur / jnp.minimum(t_cur + 1, w).astype(F32)).astype(BF16)
            dpc_halo = (dp_halo / jnp.minimum(t_halo + 1, w).astype(F32)).astype(BF16)
            du = (_dot_nn(_band(tm, tm, w, 0, True), dpc_cur)
                  + _dot_nn(_band(tm, POOL_HALO, w, -tm, True), dpc_halo) - dp_cur)
            dh_ref[:, sl] = du.astype(BF16)
            dw = _dot_tn(p_ref[:, sl], dyc)

            @pl.when(i == 0)
            def _():
                dw_ref[g] = dw

            @pl.when(i > 0)
            def _():
                dw_ref[g] += dw

    row = pl.BlockSpec((tm, pw), lambda i: (i, 0))
    dh_new, dw_pool = pl.pallas_call(
        body, name="pool_bwd", grid=(n_tiles,),
        in_specs=[pl.BlockSpec(memory_space=pl.ANY), row,
                  pl.BlockSpec((POOL_HALO, pw), lambda i: (jnp.minimum((i + 1) * halo_per_tile, n_halo_blocks - 1), 0)),
                  row, pl.BlockSpec((n_groups, gw, gw), lambda i: (0, 0, 0))],
        out_specs=[pl.BlockSpec((tm, pw), lambda i: (i, u_col_block)),
                   pl.BlockSpec((n_groups, gw, gw), lambda i: (0, 0, 0))],
        out_shape=[jax.ShapeDtypeStruct((S, W), BF16), jax.ShapeDtypeStruct((n_groups, gw, gw), F32)],
        input_output_aliases={0: 0},
        compiler_params=pltpu.CompilerParams(dimension_semantics=("arbitrary",),
                                             vmem_limit_bytes=_vmem_limit(24 << 20)),
    )(dh, dy_pre, dy_pre, p, w_pool)
    return dh_new, dw_pool


def _attn_bwd(qkv, d_out, stats, d, aw):
    _, rows, _ = qkv.shape
    n_heads = aw // HEAD_DIM
    nb = rows // SUB_BLOCK
    n_blocks = d * nb

    def body(q_ref, kp_ref, kc_ref, vp_ref, vc_ref, do_ref, st_ref, dq_ref, dk_ref, dv_ref,
             carry_k, carry_v, s_buf, dp_buf, p_buf, ds_buf):
        step = pl.program_id(0)

        @pl.when(step == 0)
        def _():
            carry_k[...] = jnp.zeros_like(carry_k)
            carry_v[...] = jnp.zeros_like(carry_v)

        @pl.when(step < n_blocks)
        def _():
            mask = _attn_mask(step % nb)
            st = st_ref[...]
            for hd in range(n_heads):
                sl = slice(hd * HEAD_DIM, (hd + 1) * HEAD_DIM)
                s_buf[hd] = _dot_nt(q_ref[:, sl], _both_blocks(kp_ref, kc_ref, sl))
                dp_buf[hd] = _dot_nt(do_ref[:, sl], _both_blocks(vp_ref, vc_ref, sl))
            for hd in range(n_heads):
                lt, dl = st[:, hd:hd + 1], st[:, n_heads + hd:n_heads + hd + 1]
                p = jnp.where(mask, jnp.exp(jnp.where(mask, s_buf[hd] * ATTN_SCALE - lt, NEG_BIG)), 0.0)
                p_buf[hd] = p.astype(BF16)
                ds_buf[hd] = (p * (dp_buf[hd] - dl) * ATTN_SCALE).astype(BF16)
            for hd in range(n_heads):
                sl = slice(hd * HEAD_DIM, (hd + 1) * HEAD_DIM)
                dq_ref[:, sl] = _dot_nn(ds_buf[hd], _both_blocks(kp_ref, kc_ref, sl)).astype(BF16)
                dk_both = _dot_tn(ds_buf[hd], q_ref[:, sl])
                dv_both = _dot_tn(p_buf[hd], do_ref[:, sl])
                dk_ref[:, sl] = (carry_k[:, sl] + dk_both[:SUB_BLOCK]).astype(BF16)
                dv_ref[:, sl] = (carry_v[:, sl] + dv_both[:SUB_BLOCK]).astype(BF16)
                carry_k[:, sl] = dk_both[SUB_BLOCK:]
                carry_v[:, sl] = dv_both[SUB_BLOCK:]

        @pl.when(step == n_blocks)
        def _():
            dk_ref[...] = carry_k[...].astype(BF16)
            dv_ref[...] = carry_v[...].astype(BF16)

    def cur(step):
        return jnp.minimum(step, n_blocks - 1)

    def qkv_spec(col, prev):
        if prev:
            return pl.BlockSpec((SUB_BLOCK, aw), lambda s: (jnp.maximum(cur(s) - 1, 0), col))
        return pl.BlockSpec((SUB_BLOCK, aw), lambda s: (cur(s), col))

    def at_cur(w):
        return pl.BlockSpec((SUB_BLOCK, w), lambda s: (cur(s), 0))

    finished = pl.BlockSpec((SUB_BLOCK, aw), lambda s: (jnp.maximum(s - 1, 0), 0))
    pair = (n_heads, SUB_BLOCK, 2 * SUB_BLOCK)
    flat = lambda t: t.reshape(d * rows, t.shape[-1])
    qkv2 = flat(qkv)
    outs = pl.pallas_call(
        body, name=f"attn_bwd_d{d}", grid=(n_blocks + 1,),
        in_specs=[qkv_spec(0, False), qkv_spec(1, True), qkv_spec(1, False), qkv_spec(2, True), qkv_spec(2, False),
                  at_cur(aw), at_cur(HEAD_DIM)],
        out_specs=[at_cur(aw), finished, finished],
        out_shape=[jax.ShapeDtypeStruct((d * rows, aw), BF16)] * 3,
        scratch_shapes=[pltpu.VMEM((SUB_BLOCK, aw), F32), pltpu.VMEM((SUB_BLOCK, aw), F32),
                        pltpu.VMEM(pair, F32), pltpu.VMEM(pair, F32), pltpu.VMEM(pair, BF16), pltpu.VMEM(pair, BF16)],
        compiler_params=pltpu.CompilerParams(dimension_semantics=("arbitrary",),
                                             vmem_limit_bytes=_vmem_limit(24 << 20)),
    )(qkv2, qkv2, qkv2, qkv2, qkv2, flat(d_out), flat(stats))
    return [t.reshape(d, rows, aw) for t in outs]


def _attn_bwd_finish(dh, per_pattern, cos2, sin_bwd, aw):
    S, W = dh.shape
    n_heads = aw // HEAD_DIM
    n_pat = len(DILATIONS)

    def body(*refs):
        grad_refs = refs[1:1 + 3 * n_pat]
        cos_ref, sin_ref = refs[1 + 3 * n_pat], refs[2 + 3 * n_pat]
        out_ref = refs[3 + 3 * n_pat]
        perms = {d: _perm_matrix(d, False) for d in DILATIONS if d > 1}
        totals = []
        for which in range(3):
            tot = None
            for pi, d in enumerate(DILATIONS):
                g = grad_refs[which * n_pat + pi][...].reshape(PERM_ROWS, aw)
                g = _permute_rows(perms[d], g) if d > 1 else g.astype(F32)
                tot = g if tot is None else tot + g
            totals.append(tot)
        dq, dk, dv = totals
        c, s = cos_ref[...], sin_ref[...]
        for hd in range(n_heads):
            sl = slice(hd * HEAD_DIM, (hd + 1) * HEAD_DIM)
            out_ref[:, sl] = _rope_apply(dq[:, sl], c, s).astype(BF16)
            out_ref[:, aw + hd * HEAD_DIM:aw + (hd + 1) * HEAD_DIM] = _rope_apply(dk[:, sl], c, s).astype(BF16)
        out_ref[:, 2 * aw:] = dv.astype(BF16)

    grads = [pp[which] for which in range(3) for pp in per_pattern]
    rope_spec = pl.BlockSpec((PERM_ROWS, HEAD_DIM), lambda i: (i, 0))
    return pl.pallas_call(
        body, name="attn_bwd_finish", grid=(S // PERM_ROWS,),
        in_specs=([pl.BlockSpec(memory_space=pl.ANY)] + [_rm_block(d, aw) for d in DILATIONS] * 3
                  + [rope_spec, rope_spec]),
        out_specs=pl.BlockSpec((PERM_ROWS, 3 * aw), lambda i: (i, 0)),
        out_shape=jax.ShapeDtypeStruct((S, W), BF16),
        input_output_aliases={0: 0},
        compiler_params=pltpu.CompilerParams(dimension_semantics=("parallel",),
                                             vmem_limit_bytes=_vmem_limit(32 << 20)),
    )(dh, *grads, cos2, sin_bwd)


def _my_place():
    x, y, c = lax.axis_index("x"), lax.axis_index("y"), lax.axis_index("c")
    return x, y, c


def _flat(px, py, pc):
    return 4 * px + 2 * py + pc


def _shard_slice(ref, axis, idx, size):
    start = pl.multiple_of(idx * size, size)
    ix = [slice(None)] * len(ref.shape)
    ix[axis] = pl.ds(start, size)
    return ref.at[tuple(ix)]


_HBM_SPEC = pl.BlockSpec(memory_space=pltpu.HBM)
_SEM_SPEC = pl.BlockSpec(memory_space=pltpu.SEMAPHORE)
_ANY_SPEC = pl.BlockSpec(memory_space=pl.ANY)
_N_PEER = N_DEV - 1
SIBLING, SAME_CORE_NEIGHBOURS, OTHER_CORE_NEIGHBOURS, DIAGONAL = (1,), (2, 4), (3, 5), (6, 7)
PEER_ORDER = SIBLING + SAME_CORE_NEIGHBOURS + OTHER_CORE_NEIGHBOURS + DIAGONAL


def _peer_of(x, y, c, r):
    return (x ^ ((r >> 2) & 1), y ^ ((r >> 1) & 1), c ^ (r & 1))


class _Exchange:
    def __init__(self, name, part, slot):
        self.name, self.part, self.slot = name, part, slot

    def _copy(self, w, r, src, land, send_sems, recv_sems, sending):
        x, y, c = _my_place()
        peer = _peer_of(x, y, c, r)
        return pltpu.make_async_remote_copy(
            src_ref=self.part(w, src, _flat(*peer)),
            dst_ref=self.slot(w, land, _flat(x, y, c) if sending else _flat(*peer)),
            send_sem=send_sems.at[w * _N_PEER + r - 1], recv_sem=recv_sems.at[w * _N_PEER + r - 1],
            device_id=peer, device_id_type=MESH)

    def start(self, srcs, lands, after=None):
        n = len(srcs)
        n_after = 0 if after is None else 1

        def body(*refs):
            src, land = refs[:n], refs[n:2 * n]
            outs = refs[2 * n + n_after:]
            send_sems, recv_sems, local_sems, token = outs[0], outs[1], outs[2], outs[3 + 2 * n]
            for w in range(n):
                self._own_copy(w, src[w], land[w], local_sems).start()
                for r in PEER_ORDER:
                    self._copy(w, r, src[w], land[w], send_sems, recv_sems, True).start()
            token[...] = jnp.zeros_like(token)

        sems = pltpu.SemaphoreType.DMA((n * _N_PEER,))
        outs = pl.pallas_call(
            body, name=self.name + "_start",
            out_shape=(sems, sems, pltpu.SemaphoreType.DMA((n,)),
                       *[pltpu.HBM(t.shape, t.dtype) for t in list(srcs) + list(lands)],
                       jax.ShapeDtypeStruct((8, 128), F32)),
            in_specs=[_HBM_SPEC] * (2 * n) + [_ANY_SPEC] * n_after,
            out_specs=(_SEM_SPEC, _SEM_SPEC, _SEM_SPEC, *[_HBM_SPEC] * (2 * n), pl.BlockSpec(memory_space=pltpu.VMEM)),
            input_output_aliases={i: 3 + i for i in range(2 * n)},
            compiler_params=pltpu.CompilerParams(has_side_effects=pltpu.SideEffectType.DATAFLOW_SIDE_EFFECTING),
        )(*[pltpu.with_memory_space_constraint(t, pltpu.HBM) for t in list(srcs) + list(lands)],
          *([after] if n_after else []))
        return outs[0], outs[1], outs[2], outs[3:3 + n], outs[3 + n:3 + 2 * n], outs[3 + 2 * n]

    def _own_copy(self, w, src, land, local_sems):
        me = _flat(*_my_place())
        return pltpu.make_async_copy(self.part(w, src, me), self.slot(w, land, me), local_sems.at[w])

    def wait(self, started, after, peers=PEER_ORDER, own=True, tag=""):
        send_sems, recv_sems, local_sems, srcs, lands, token = started
        n = len(srcs)

        def body(*refs):
            src, land = refs[:n], refs[n:2 * n]
            s_sems, r_sems, l_sems = refs[2 * n], refs[2 * n + 1], refs[2 * n + 2]
            for w in range(n):
                if own:
                    self._own_copy(w, src[w], land[w], l_sems).wait()
                for r in peers:
                    cp = self._copy(w, r, src[w], land[w], s_sems, r_sems, False)
                    cp.wait_send()
                    cp.wait_recv()

        outs = pl.pallas_call(
            body, name=self.name + "_wait" + tag,
            out_shape=[pltpu.HBM(t.shape, t.dtype) for t in list(srcs) + list(lands)],
            in_specs=[_HBM_SPEC] * (2 * n) + [_SEM_SPEC, _SEM_SPEC, _SEM_SPEC, _ANY_SPEC],
            out_specs=[_HBM_SPEC] * (2 * n),
            input_output_aliases={i: i for i in range(2 * n)},
            compiler_params=pltpu.CompilerParams(has_side_effects=pltpu.SideEffectType.DATAFLOW_SIDE_EFFECTING),
        )(*srcs, *lands, send_sems, recv_sems, local_sems, after)
        return outs[n:], (send_sems, recv_sems, local_sems, outs[:n], outs[n:], token)


DIRECT_PEERS = (1, 2, 4, 6)
FORWARDED = (3, 5, 7)


class _TwoLevelGather:
    def __init__(self, name, axes, sizes):
        self.name, self.axes, self.sizes = name, axes, sizes

    def _place(self, w, land, dev):
        return _shard_slice(land, self.axes[w], dev, self.sizes[w])

    def _direct(self, w, r, src, land, sems, sending):
        x, y, c = _my_place()
        peer = _peer_of(x, y, c, r)
        k = w * len(DIRECT_PEERS) + DIRECT_PEERS.index(r)
        return pltpu.make_async_remote_copy(
            src_ref=src, dst_ref=self._place(w, land, _flat(x, y, c) if sending else _flat(*peer)),
            send_sem=sems[0].at[k], recv_sem=sems[1].at[k], device_id=peer, device_id_type=MESH)

    def _passed_on(self, w, f, land, sems, sending):
        x, y, c = _my_place()
        owner = _flat(*_peer_of(x, y, c, (f ^ 1) if sending else f))
        slot = self._place(w, land, owner)
        k = w * len(FORWARDED) + FORWARDED.index(f)
        return pltpu.make_async_remote_copy(
            src_ref=slot, dst_ref=slot, send_sem=sems[2].at[k], recv_sem=sems[3].at[k],
            device_id=(x, y, 1 - c), device_id_type=MESH)

    def _own(self, w, src, land, sems):
        return pltpu.make_async_copy(src, self._place(w, land, _flat(*_my_place())), sems[4].at[w])

    def _call(self, suffix, body, sems, srcs, lands, after, make_sems):
        n = len(srcs)
        n_after = 0 if after is None else 1
        bufs = list(srcs) + list(lands)

        def wrapped(*refs):
            ins = refs[:2 * n]
            rest = refs[2 * n + (n_after if make_sems else 0):]
            body(ins[:n], ins[n:], rest[:5], rest[-1] if make_sems else None)

        buf_shapes = [pltpu.HBM(t.shape, t.dtype) for t in bufs]
        if make_sems:
            sem_types = [pltpu.SemaphoreType.DMA((n * len(DIRECT_PEERS),))] * 2 \
                + [pltpu.SemaphoreType.DMA((n * len(FORWARDED),))] * 2 + [pltpu.SemaphoreType.DMA((n,))]
            outs = pl.pallas_call(
                wrapped, name=self.name + suffix,
                out_shape=(*sem_types, *buf_shapes, jax.ShapeDtypeStruct((8, 128), F32)),
                in_specs=[_HBM_SPEC] * (2 * n) + [_ANY_SPEC] * n_after,
                out_specs=(*[_SEM_SPEC] * 5, *[_HBM_SPEC] * (2 * n), pl.BlockSpec(memory_space=pltpu.VMEM)),
                input_output_aliases={i: 5 + i for i in range(2 * n)},
                compiler_params=pltpu.CompilerParams(has_side_effects=pltpu.SideEffectType.DATAFLOW_SIDE_EFFECTING),
            )(*[pltpu.with_memory_space_constraint(t, pltpu.HBM) for t in bufs], *([after] if n_after else []))
            return tuple(outs[:5]), outs[5:5 + n], outs[5 + n:5 + 2 * n], outs[5 + 2 * n]
        outs = pl.pallas_call(
            wrapped, name=self.name + suffix,
            out_shape=buf_shapes,
            in_specs=[_HBM_SPEC] * (2 * n) + [_SEM_SPEC] * 5 + [_ANY_SPEC] * n_after,
            out_specs=[_HBM_SPEC] * (2 * n),
            input_output_aliases={i: i for i in range(2 * n)},
            compiler_params=pltpu.CompilerParams(has_side_effects=pltpu.SideEffectType.DATAFLOW_SIDE_EFFECTING),
        )(*bufs, *sems, *([after] if n_after else []))
        return sems, outs[:n], outs[n:], None

    def start(self, srcs, lands, after=None):
        n = len(srcs)

        def body(src, land, sems, token):
            for w in range(n):
                self._own(w, src[w], land[w], sems).start()
                for r in DIRECT_PEERS:
                    self._direct(w, r, src[w], land[w], sems, True).start()
            token[...] = jnp.zeros_like(token)

        return self._call("_start", body, None, srcs, lands, after, True)

    def forward(self, state, after, which, tag=""):
        sems, srcs, lands, token = state
        n = len(srcs)

        def body(src, land, sem_refs, _):
            for w in range(n):
                for r in which:
                    self._direct(w, r, src[w], land[w], sem_refs, False).wait_recv()
                    self._passed_on(w, r | 1, land[w], sem_refs, True).start()

        sems, srcs, lands, _ = self._call("_forward" + tag, body, sems, srcs, lands, after, False)
        return sems, srcs, lands, token

    def wait(self, state, after, direct=(), passed_on=(), sends=False, tag=""):
        sems, srcs, lands, token = state
        n = len(srcs)

        def body(src, land, sem_refs, _):
            for w in range(n):
                for r in direct:
                    self._direct(w, r, src[w], land[w], sem_refs, False).wait_recv()
                for f in passed_on:
                    self._passed_on(w, f, land[w], sem_refs, False).wait_recv()
                if sends:
                    self._own(w, src[w], land[w], sem_refs).wait()
                    for r in DIRECT_PEERS:
                        self._direct(w, r, src[w], land[w], sem_refs, True).wait_send()
                    for f in FORWARDED:
                        self._passed_on(w, f, land[w], sem_refs, True).wait_send()

        sems, srcs, lands, _ = self._call("_wait" + tag, body, sems, srcs, lands, after, False)
        return lands, (sems, srcs, lands, token)


def _scatter_exchange(name, axes, shard_sizes):
    def part(w, src, dev):
        return src if axes[w] is None else _shard_slice(src, axes[w], dev, shard_sizes[w])
    return _Exchange(name, part, lambda w, land, dev: land.at[dev])


def _adamw(name, partials, w, m, v):
    R, C = w.shape
    tr = R
    while tr * C * 4 > (1 << 20) and tr % 16 == 0:
        tr //= 2

    def body(p_ref, w_ref, m_ref, v_ref, g_ref, d_ref, nm_ref, nv_ref):
        g = p_ref[0].astype(F32)
        for jdev in range(1, N_DEV):
            g = g + p_ref[jdev].astype(F32)
        nm = ADAM_B1 * m_ref[...] + (1.0 - ADAM_B1) * g
        nv = ADAM_B2 * v_ref[...] + (1.0 - ADAM_B2) * (g * g)
        m_hat = nm / (1.0 - ADAM_B1 ** ADAM_STEP)
        v_hat = nv / (1.0 - ADAM_B2 ** ADAM_STEP)
        g_ref[...] = g
        d_ref[...] = -ADAM_LR * (m_hat / (jnp.sqrt(v_hat) + ADAM_EPS) + ADAM_WD * w_ref[...])
        nm_ref[...] = nm
        nv_ref[...] = nv

    spec = pl.BlockSpec((tr, C), lambda i: (i, 0))
    return pl.pallas_call(
        body, name=name, grid=(R // tr,),
        in_specs=[pl.BlockSpec((N_DEV, tr, C), lambda i: (0, i, 0)), spec, spec, spec],
        out_specs=[spec] * 4,
        out_shape=[jax.ShapeDtypeStruct((R, C), F32)] * 4,
        compiler_params=pltpu.CompilerParams(dimension_semantics=("parallel",),
                                             vmem_limit_bytes=_vmem_limit(24 << 20)),
    )(partials, w, m, v)


def _local_step(x, cos2, sin_fwd, sin_bwd, project_in, mix_weights, ffn_weights, pool_scale, g_mix, b_mix, g_ff, b_ff,
                target, send):
    S, D = x.shape
    aw = pw = D // 2
    u_col_block = 3
    gate_col0 = 4 * aw

    xb = x.astype(BF16)
    h, w_in = project_in(xb)
    dilated = [d for d in DILATIONS if d > 1]
    qkv = {1: h[None], **dict(zip(dilated, _to_residue_major("qkv_to_rm", h, 0, 3 * aw)))}
    fwd = [_attn_fwd(qkv[d], d, aw) for d in DILATIONS]
    o_attn, l_tot = _attn_combine([f[0] for f in fwd], [f[1] for f in fwd], aw)
    w_pool, w_ba, w_bp, w_out = mix_weights(o_attn)
    p, y_pre, pm = _pool_fwd(h, w_pool, pool_scale, pw, u_col_block)
    y_attn = _branch_attn(o_attn, w_ba)
    y_pool, merged = _branch_pool_merge(pm, w_bp, h, y_attn, gate_col0)
    w1, w2 = ffn_weights(merged)
    x1, x1b, xhat1, rstd1 = _out_proj_ln(merged, w_out, x, g_mix, b_mix)
    a = _ffn_up(x1b, w1)
    z2 = _residual_matmul("ffn_down", a, w2, "nn", x1)
    dz2, dz2b, dg_ff, db_ff, loss = _ln_loss_bwd(z2, g_ff, b_ff, target)

    tok = send("ff2", [_grad_weight("grad_w_ff2", a, dz2b)])
    dh1 = _ffn_down_bwd(dz2b, w2, a, after=tok)
    tok = send("ff1", [_grad_weight("grad_w_ff1", x1b, dh1)])
    dy1 = _residual_matmul("ffn_up_bwd", dh1, w1, "nt", dz2, after=tok)
    dz1, dz1b, dg_mix, db_mix = _ln_bwd(dy1, xhat1, rstd1, g_mix)
    tok = send("out", [_grad_weight("grad_w_out", merged, dz1b)])
    dy_attn, dy_pool, dh = _out_proj_bwd(dz1b, w_out, h, y_attn, y_pool, gate_col0, after=tok)
    tok = send("branch", [_grad_weight("grad_w_branch_attn", o_attn, dy_attn),
                          _grad_weight("grad_w_branch_pool", pm, dy_pool)])
    d_out, stats = _branch_attn_bwd(dy_attn, w_ba, o_attn, l_tot, after=tok)
    dy_pre, d_scale = _branch_pool_bwd(dy_pool, w_bp, y_pre, pool_scale)
    dh, dw_pool = _pool_bwd(dh, dy_pre, p, w_pool, pw, u_col_block)
    d_outs = {1: d_out[None], **dict(zip(dilated, _to_residue_major("dout_to_rm", d_out, 0, aw)))}
    statss = {1: stats[None], **dict(zip(dilated, _to_residue_major("stats_to_rm", stats, 0, HEAD_DIM)))}
    per_pattern = [_attn_bwd(qkv[d], d_outs[d], statss[d], d, aw) for d in DILATIONS]
    dh = _attn_bwd_finish(dh, per_pattern, cos2, sin_bwd, aw)
    small = jnp.concatenate((d_scale, dg_mix, db_mix, dg_ff, db_ff), axis=-1)
    tok = send("in", [_grad_weight("grad_w_in", xb, dh), dw_pool.astype(BF16),
                      small.reshape(small.shape[-1] // HEAD_DIM, HEAD_DIM)])
    grad_x = _residual_matmul("in_proj_bwd", dh, w_in, "nt", dz1, after=tok)
    return loss, grad_x


def _rope_tables(positions):
    half = HEAD_DIM // 2
    inv_freq = ROPE_THETA ** (-jnp.arange(half, dtype=F32) / half)
    ang = positions.astype(F32)[:, None] * inv_freq
    cos, sin = jnp.cos(ang), jnp.sin(ang)
    cos2 = jnp.concatenate([cos, cos], axis=-1)
    sin_fwd = jnp.concatenate([-sin, sin], axis=-1)
    return cos2, sin_fwd, -sin_fwd


def kernel(x, positions, w_in, w_pool, pool_scale, w_branch_attn, w_branch_pool, w_out, ln_mix_g, ln_mix_b, w_ff1, w_ff2, ln_ff_g, ln_ff_b, loss_target, m_w_in, m_w_pool, m_pool_scale, m_w_branch_attn, m_w_branch_pool, m_w_out, m_ln_mix_g, m_ln_mix_b, m_w_ff1, m_w_ff2, m_ln_ff_g, m_ln_ff_b, v_w_in, v_w_pool, v_pool_scale, v_w_branch_attn, v_w_branch_pool, v_w_out, v_ln_mix_g, v_ln_mix_b, v_w_ff1, v_w_ff2, v_ln_ff_g, v_ln_ff_b):
    big_w = (w_in[0], w_pool[0], w_branch_attn[0], w_branch_pool[0], w_out[0], w_ff1[0], w_ff2[0])
    big_m = (m_w_in[0], m_w_pool[0], m_w_branch_attn[0], m_w_branch_pool[0], m_w_out[0], m_w_ff1[0], m_w_ff2[0])
    big_v = (v_w_in[0], v_w_pool[0], v_w_branch_attn[0], v_w_branch_pool[0], v_w_out[0], v_w_ff1[0], v_w_ff2[0])
    shard_axes = (1, 1, 1, 1, 0, 1, 0)
    small_w = (pool_scale, ln_mix_g, ln_mix_b, ln_ff_g, ln_ff_b)
    small_m = (m_pool_scale, m_ln_mix_g, m_ln_mix_b, m_ln_ff_g, m_ln_ff_b)
    small_v = (v_pool_scale, v_ln_mix_g, v_ln_mix_b, v_ln_ff_g, v_ln_ff_b)

    names = ("w_in", "w_pool", "w_branch_attn", "w_branch_pool", "w_out", "w_ff1", "w_ff2")
    axis_of = dict(zip(names, shard_axes))
    shard_of = dict(zip(names, [w.astype(BF16) for w in big_w]))

    def full_buffer(n):
        s, ax = shard_of[n], axis_of[n]
        full = list(s.shape)
        full[ax] *= N_DEV
        return lax.empty(tuple(full), s.dtype)

    def gather_group(tag, group, after):
        ex = _TwoLevelGather(tag, [axis_of[n] for n in group], [shard_of[n].shape[axis_of[n]] for n in group])
        return ex, ex.start([shard_of[n] for n in group], [full_buffer(n) for n in group], after)

    in_ex, in_state = gather_group("gather_in", ("w_in",), None)
    mix_ex, mix_state = gather_group("gather_mix", ("w_pool", "w_branch_attn", "w_branch_pool", "w_out"),
                                     in_state[-1])
    ffn_ex, ffn_state = gather_group("gather_ffn", ("w_ff1", "w_ff2"), mix_state[-1])
    states = {"mix": mix_state, "ffn": ffn_state}
    me = 4 * lax.axis_index("x") + 2 * lax.axis_index("y") + lax.axis_index("c")
    block_cols = shard_of["w_in"].shape[1]
    neighbours, diagonal = (2, 4), (6,)

    def project_in(xb):
        n_rope_blocks = 2 * (x.shape[-1] // 2) // block_cols

        def piece(tag, w, blocks, h, **kw):
            return _in_proj_piece("in_proj_" + tag, xb, w, jnp.stack(blocks).astype(jnp.int32), cos2, sin_fwd, h,
                                  block_cols, n_rope_blocks, **kw)

        h = piece("own", shard_of["w_in"], [me], None, own_shard=True, after=ffn_state[-1])
        (w_in_land,), state = in_ex.wait(in_state, h, direct=(1,), tag="_sibling")
        h = piece("sibling", w_in_land, [me ^ 1], h)
        state = in_ex.forward(state, h, neighbours, tag="_neighbours")
        h = piece("neighbours", state[2][0], [me ^ r for r in neighbours], h)
        state = in_ex.forward(state, h, diagonal, tag="_diagonal")
        h = piece("diagonal", state[2][0], [me ^ r for r in diagonal], h)
        states["mix"] = mix_ex.forward(states["mix"], h, neighbours + diagonal)
        (w_in_land,), _ = in_ex.wait(state, h, passed_on=FORWARDED, sends=True, tag="_passed_on")
        h = piece("passed_on", w_in_land, [me ^ f for f in FORWARDED], h)
        return h, w_in_land

    def mix_weights(after):
        states["ffn"] = ffn_ex.forward(states["ffn"], after, neighbours + diagonal)
        return mix_ex.wait(states["mix"], after, direct=(1,), passed_on=FORWARDED, sends=True)[0]

    def ffn_weights(after):
        return ffn_ex.wait(states["ffn"], after, direct=(1,), passed_on=FORWARDED, sends=True)[0]

    groups = {"ff2": ("w_ff2",), "ff1": ("w_ff1",), "out": ("w_out",),
              "branch": ("w_branch_attn", "w_branch_pool"), "in": ("w_in", "w_pool", "small")}
    sent = {}

    def send(key, grads_):
        axes = [axis_of.get(n) for n in groups[key]]
        sizes = [None if ax is None else g.shape[ax] // N_DEV for g, ax in zip(grads_, axes)]
        lands = []
        for g, ax, size in zip(grads_, axes, sizes):
            shard = list(g.shape)
            if ax is not None:
                shard[ax] = size
            lands.append(lax.empty((N_DEV, *shard), g.dtype))
        ex = _scatter_exchange("scatter_" + key, axes, sizes)
        sent[key] = (ex, ex.start(list(grads_), lands))
        return sent[key][1][-1]

    cos2, sin_fwd, sin_bwd = _rope_tables(positions[0])
    loss, grad_x = _local_step(
        x[0], cos2, sin_fwd, sin_bwd, project_in, mix_weights, ffn_weights, pool_scale, ln_mix_g, ln_mix_b,
        ln_ff_g, ln_ff_b, loss_target[0], send)

    state = dict(zip(names, zip(big_w, big_m, big_v)))
    n_small = sum(w.shape[-1] for w in small_w)
    small_2d = (n_small // HEAD_DIM, HEAD_DIM)
    state["small"] = tuple(jnp.concatenate(t, axis=-1).reshape(small_2d) for t in (small_w, small_m, small_v))
    grads, deltas, new_ms, new_vs = {}, {}, {}, {}
    after = grad_x
    for key in ("ff2", "ff1", "out", "branch", "in"):
        ex, started = sent[key]
        for n, part in zip(groups[key], ex.wait(started, after)[0]):
            w, m, v = state[n]
            r2 = (-1, w.shape[-1])
            w2d = w.reshape(r2)
            res = _adamw("adamw_" + n, part.reshape((N_DEV,) + w2d.shape), w2d, m.reshape(r2), v.reshape(r2))
            after = res[0]
            if n == "small":
                small_out = [t.reshape(1, n_small) for t in res]
            else:
                grads[n], deltas[n], new_ms[n], new_vs[n] = (t.reshape((1,) + w.shape) for t in res)
    small_names = ("pool_scale", "ln_mix_g", "ln_mix_b", "ln_ff_g", "ln_ff_b")
    off = 0
    for n, w in zip(small_names, small_w):
        width = w.shape[-1]
        grads[n], deltas[n], new_ms[n], new_vs[n] = (t[:, off:off + width] for t in small_out)
        off += width

    order = ("w_in", "w_pool", "pool_scale", "w_branch_attn", "w_branch_pool", "w_out", "ln_mix_g", "ln_mix_b",
             "w_ff1", "w_ff2", "ln_ff_g", "ln_ff_b")
    total_loss = lax.psum(loss[0, 0], ("x", "y", "c"))
    return (total_loss, grad_x[None], *[grads[n] for n in order], *[deltas[n] for n in order],
            *[new_ms[n] for n in order], *[new_vs[n] for n in order])
```

```python
import functools

import jax
import jax.numpy as jnp
from jax import lax
from jax.experimental import pallas as pl
from jax.experimental.pallas import tpu as pltpu

F32 = jnp.float32
BF16 = jnp.bfloat16

N_DEV = 8
HEAD_DIM = 128
SUB_BLOCK = 128
DILATIONS = (1, 4, 16)
POOL_WINDOWS = (2, 4, 8, 16)
MAX_POOL_WINDOW = 16
POOL_HALO = 128
PERM_ROWS = 256
K_TILE = 1024
LN_EPS = 1e-5
DEEPNORM_ALPHA = 2.0 ** 0.25
ROPE_THETA = 10000.0
ATTN_SCALE = HEAD_DIM ** -0.5
ADAM_LR, ADAM_B1, ADAM_B2, ADAM_EPS, ADAM_WD, ADAM_STEP = 0.001, 0.9, 0.999, 1e-08, 0.01, 10
NEG_BIG = -1e30
VMEM_CAP_V7X = 64 * 1024 * 1024
MESH = pl.DeviceIdType.MESH


def _vmem_limit(est_bytes):
    return int(min(max(est_bytes * 5 // 4 + (4 << 20), 16 << 20), VMEM_CAP_V7X - (6 << 20)))


def _nbytes(shape, dtype):
    n = 1
    for s in shape:
        n *= s
    return n * jnp.dtype(dtype).itemsize


def _mm(name, a, b, form, tiles, outs, epi, extras=(), sequential=False, after=None):
    tm, tn, tk = tiles
    if form == "nn":
        (M, K), (K2, N) = a.shape, b.shape
    elif form == "nt":
        (M, K), (N, K2) = a.shape, b.shape
    else:
        (K, M), (K2, N) = a.shape, b.shape
    assert K == K2, (name, a.shape, b.shape)
    tm, tn, tk = min(tm, M), min(tn, N), min(tk, K)
    assert M % tm == 0 and N % tn == 0 and K % tk == 0, (name, M, N, K, tm, tn, tk)
    grid = (M // tm, N // tn, K // tk)
    nk = grid[2]
    if form == "nn":
        a_spec = pl.BlockSpec((tm, tk), lambda i, j, k: (i, k))
        b_spec = pl.BlockSpec((tk, tn), lambda i, j, k: (k, j))
        contract = ((1,), (0,))
    elif form == "nt":
        a_spec = pl.BlockSpec((tm, tk), lambda i, j, k: (i, k))
        b_spec = pl.BlockSpec((tn, tk), lambda i, j, k: (j, k))
        contract = ((1,), (1,))
    else:
        a_spec = pl.BlockSpec((tk, tm), lambda i, j, k: (k, i))
        b_spec = pl.BlockSpec((tk, tn), lambda i, j, k: (k, j))
        contract = ((0,), (0,))
    n_ex, n_out = len(extras), len(outs)
    n_after = 0 if after is None else 1

    def body(a_ref, b_ref, *rest):
        ex_refs = rest[:n_ex]
        rest = rest[n_ex + n_after:]
        out_refs = rest[:n_out]
        i, j, k = pl.program_id(0), pl.program_id(1), pl.program_id(2)

        def prod():
            return lax.dot_general(a_ref[...].astype(BF16), b_ref[...].astype(BF16),
                                   (contract, ((), ())), preferred_element_type=F32)

        if nk == 1:
            epi(prod(), ex_refs, out_refs, i, j)
        else:
            acc = rest[n_out]

            @pl.when(k == 0)
            def _():
                acc[...] = prod()

            @pl.when(jnp.logical_and(k > 0, k < nk - 1))
            def _():
                acc[...] += prod()

            @pl.when(k == nk - 1)
            def _():
                epi(acc[...] + prod(), ex_refs, out_refs, i, j)

    est = 2 * (_nbytes(a_spec.block_shape, a.dtype) + _nbytes(b_spec.block_shape, b.dtype))
    est += sum(2 * _nbytes(bs, arr.dtype) for arr, bs, _ in extras)
    est += sum(2 * _nbytes(bs, dt) for _, dt, bs, _ in outs)
    est += 4 * tm * tn * 4
    sem = ("arbitrary",) * 3 if sequential else ("parallel", "parallel", "arbitrary")
    return pl.pallas_call(
        body, name=name, grid=grid,
        in_specs=([a_spec, b_spec] + [pl.BlockSpec(bs, im) for _, bs, im in extras]
                  + [pl.BlockSpec(memory_space=pl.ANY)] * n_after),
        out_specs=[pl.BlockSpec(bs, im) for _, _, bs, im in outs],
        out_shape=[jax.ShapeDtypeStruct(sh, dt) for sh, dt, _, _ in outs],
        scratch_shapes=[pltpu.VMEM((tm, tn), F32)] if nk > 1 else [],
        compiler_params=pltpu.CompilerParams(dimension_semantics=sem, vmem_limit_bytes=_vmem_limit(est)),
    )(a, b, *[arr for arr, _, _ in extras], *([after] if n_after else []))


def _tile_out(shape, dtype, tm, tn):
    return (shape, dtype, (tm, tn), lambda i, j, k: (i, j))


def _row_sum_out(width):
    return ((1, width), F32, (1, width), lambda i, j, k: (0, 0))


def _accumulate_rows(ref, value, i):
    @pl.when(i == 0)
    def _():
        ref[...] = value

    @pl.when(i > 0)
    def _():
        ref[...] += value


def _layer_norm_bwd(dy, xhat, rstd, g):
    dxh = dy * g
    m1 = jnp.mean(dxh, axis=-1, keepdims=True)
    m2 = jnp.mean(dxh * xhat, axis=-1, keepdims=True)
    return rstd * (dxh - m1 - xhat * m2)


def _rope_apply(t, cos2, sin_signed):
    return t * cos2 + pltpu.roll(t, HEAD_DIM // 2, axis=1) * sin_signed


def _in_proj_piece(name, xb, w_in, col_blocks, cos2, sin_fwd, h_so_far, block_cols, n_rope_blocks, own_shard=False,
                   after=None):
    S, D = xb.shape
    W = w_in.shape[1] * (N_DEV if own_shard else 1)
    tm = min(1024, S)
    n_blocks = col_blocks.shape[0]

    def body(cols_ref, x_ref, w_ref, cos_ref, sin_ref, *rest):
        h_ref = rest[-1]
        j = pl.program_id(1)

        @pl.when(cols_ref[j] < n_rope_blocks)
        def _():
            acc = _dot_nn(x_ref[...], w_ref[...])
            c, s = cos_ref[...], sin_ref[...]
            for hd in range(block_cols // HEAD_DIM):
                sl = slice(hd * HEAD_DIM, (hd + 1) * HEAD_DIM)
                h_ref[:, sl] = _rope_apply(acc[:, sl], c, s).astype(BF16)

        @pl.when(cols_ref[j] >= n_rope_blocks)
        def _():
            h_ref[...] = _dot_nn(x_ref[...], w_ref[...]).astype(BF16)

    row = pl.BlockSpec((tm, HEAD_DIM), lambda i, j, cols: (i, 0))
    carried = ([] if h_so_far is None else [h_so_far]) + ([] if after is None else [after])
    est = 2 * (tm * D * 2 + D * block_cols * 2 + tm * block_cols * 2 + 2 * tm * HEAD_DIM * 4) + 3 * tm * block_cols * 4
    return pl.pallas_call(
        body, name=name,
        grid_spec=pltpu.PrefetchScalarGridSpec(
            num_scalar_prefetch=1, grid=(S // tm, n_blocks),
            in_specs=[pl.BlockSpec((tm, D), lambda i, j, cols: (i, 0)),
                      pl.BlockSpec((D, block_cols), lambda i, j, cols: (0, 0 if own_shard else cols[j])), row, row]
                     + [pl.BlockSpec(memory_space=pl.ANY)] * len(carried),
            out_specs=pl.BlockSpec((tm, block_cols), lambda i, j, cols: (i, cols[j]))),
        out_shape=jax.ShapeDtypeStruct((S, W), BF16),
        input_output_aliases={} if h_so_far is None else {5: 0},
        compiler_params=pltpu.CompilerParams(dimension_semantics=("parallel", "arbitrary"),
                                             vmem_limit_bytes=_vmem_limit(est)),
    )(col_blocks, xb, w_in, cos2, sin_fwd, *carried)


def _attn_mask(mb):
    qi = lax.broadcasted_iota(jnp.int32, (SUB_BLOCK, 2 * SUB_BLOCK), 0)
    kj = lax.broadcasted_iota(jnp.int32, (SUB_BLOCK, 2 * SUB_BLOCK), 1)
    prev = jnp.logical_and(jnp.logical_and(kj < SUB_BLOCK, kj >= qi), mb > 0)
    cur = jnp.logical_and(kj >= SUB_BLOCK, kj - SUB_BLOCK <= qi)
    return jnp.logical_or(prev, cur)


def _both_blocks(prev_ref, cur_ref, sl):
    return jnp.concatenate([prev_ref[:, sl], cur_ref[:, sl]], axis=0)


def _dot_nt(a, b):
    return lax.dot_general(a, b, (((1,), (1,)), ((), ())), preferred_element_type=F32)


def _dot_tn(a, b):
    return lax.dot_general(a, b, (((0,), (0,)), ((), ())), preferred_element_type=F32)


def _dot_nn(a, b):
    return lax.dot_general(a, b, (((1,), (0,)), ((), ())), preferred_element_type=F32)


def _perm_matrix(d, to_residue_major):
    g = PERM_ROWS // d
    i = lax.broadcasted_iota(jnp.int32, (PERM_ROWS, PERM_ROWS), 0)
    j = lax.broadcasted_iota(jnp.int32, (PERM_ROWS, PERM_ROWS), 1)
    if to_residue_major:
        hit = j == (i % g) * d + i // g
    else:
        hit = j == (i % d) * g + i // d
    return hit.astype(BF16)


def _permute_rows(perm, x, terms=3):
    if x.dtype == BF16:
        return _dot_nn(perm, x)
    hi = x.astype(BF16)
    r1 = x - hi.astype(F32)
    mid = r1.astype(BF16)
    out = _dot_nn(perm, hi) + _dot_nn(perm, mid)
    if terms == 3:
        out = out + _dot_nn(perm, (r1 - mid.astype(F32)).astype(BF16))
    return out


def _rm_block(d, width):
    return pl.BlockSpec((d, PERM_ROWS // d, width), lambda i: (0, i, 0))


def _to_residue_major(name, x, col_block, width):
    S = x.shape[0]
    dils = [d for d in DILATIONS if d > 1]
    chunk = min(width, 1024)

    def body(x_ref, *out_refs):
        for d, o_ref in zip(dils, out_refs):
            perm = _perm_matrix(d, True)
            for c0 in range(0, width, chunk):
                cw = min(chunk, width - c0)
                y = _permute_rows(perm, x_ref[:, c0:c0 + cw])
                o_ref[:, :, c0:c0 + cw] = y.astype(x.dtype).reshape(d, PERM_ROWS // d, cw)

    return pl.pallas_call(
        body, name=name, grid=(S // PERM_ROWS,),
        in_specs=[pl.BlockSpec((PERM_ROWS, width), lambda i: (i, col_block))],
        out_specs=[_rm_block(d, width) for d in dils],
        out_shape=[jax.ShapeDtypeStruct((d, S // d, width), x.dtype) for d in dils],
        compiler_params=pltpu.CompilerParams(dimension_semantics=("parallel",),
                                             vmem_limit_bytes=_vmem_limit(32 << 20)),
    )(x)


def _put_column(tile, col, value):
    lane = lax.broadcasted_iota(jnp.int32, tile.shape, 1)
    return jnp.where(lane == col, value, tile)


def _attn_fwd(qkv, d, aw):
    _, rows, _ = qkv.shape
    n_heads = aw // HEAD_DIM
    nb = rows // SUB_BLOCK

    def body(q_ref, kc_ref, vc_ref, o_ref, lse_ref, kp_ref, vp_ref, s_buf, p_buf):
        step = pl.program_id(0)

        @pl.when(step == 0)
        def _():
            kp_ref[...] = jnp.zeros_like(kp_ref)
            vp_ref[...] = jnp.zeros_like(vp_ref)

        mask = _attn_mask(step % nb)
        for hd in range(n_heads):
            sl = slice(hd * HEAD_DIM, (hd + 1) * HEAD_DIM)
            s_buf[hd] = _dot_nt(q_ref[:, sl], _both_blocks(kp_ref, kc_ref, sl))
        lse_tile = jnp.zeros((SUB_BLOCK, HEAD_DIM), F32)
        inv_tile = jnp.zeros((SUB_BLOCK, HEAD_DIM), F32)
        for hd in range(n_heads):
            s = jnp.where(mask, s_buf[hd] * ATTN_SCALE, NEG_BIG)
            m = jnp.max(s, axis=-1, keepdims=True)
            p = jnp.exp(s - m)
            l = jnp.sum(p, axis=-1, keepdims=True)
            p_buf[hd] = p.astype(BF16)
            lse_tile = _put_column(lse_tile, hd, m + jnp.log(l))
            inv_tile = _put_column(inv_tile, hd, 1.0 / l)
        lse_ref[...] = lse_tile
        for hd in range(n_heads):
            sl = slice(hd * HEAD_DIM, (hd + 1) * HEAD_DIM)
            o = _dot_nn(p_buf[hd], _both_blocks(vp_ref, vc_ref, sl))
            o_ref[:, sl] = o * inv_tile[:, hd:hd + 1]
        kp_ref[...] = kc_ref[...]
        vp_ref[...] = vc_ref[...]

    def block(col, width):
        return pl.BlockSpec((SUB_BLOCK, width), lambda s: (s, col))

    qkv2 = qkv.reshape(d * rows, qkv.shape[-1])
    o, lse = pl.pallas_call(
        body, name=f"attn_fwd_d{d}", grid=(d * nb,),
        in_specs=[block(0, aw), block(1, aw), block(2, aw)],
        out_specs=[block(0, aw), block(0, HEAD_DIM)],
        out_shape=[jax.ShapeDtypeStruct((d * rows, aw), F32), jax.ShapeDtypeStruct((d * rows, HEAD_DIM), F32)],
        scratch_shapes=[pltpu.VMEM((SUB_BLOCK, aw), BF16), pltpu.VMEM((SUB_BLOCK, aw), BF16),
                        pltpu.VMEM((n_heads, SUB_BLOCK, 2 * SUB_BLOCK), F32),
                        pltpu.VMEM((n_heads, SUB_BLOCK, 2 * SUB_BLOCK), BF16)],
        compiler_params=pltpu.CompilerParams(dimension_semantics=("arbitrary",),
                                             vmem_limit_bytes=_vmem_limit(16 << 20)),
    )(qkv2, qkv2, qkv2)
    return o.reshape(d, rows, aw), lse.reshape(d, rows, HEAD_DIM)


def _attn_combine(outs, lses, aw):
    S = outs[0].shape[1]
    n_heads = aw // HEAD_DIM
    n_pat = len(DILATIONS)

    def body(*refs):
        o_refs, l_refs = refs[:n_pat], refs[n_pat:2 * n_pat]
        o_ref, lt_ref = refs[2 * n_pat], refs[2 * n_pat + 1]
        o_nat, l_nat = [], []
        for d, o_r, l_r in zip(DILATIONS, o_refs, l_refs):
            o_p = o_r[...].reshape(PERM_ROWS, aw)
            l_p = l_r[...].reshape(PERM_ROWS, HEAD_DIM)
            if d > 1:
                perm = _perm_matrix(d, False)
                o_p, l_p = _permute_rows(perm, o_p, terms=2), _permute_rows(perm, l_p)
            o_nat.append(o_p)
            l_nat.append(l_p)
        mx = functools.reduce(jnp.maximum, l_nat)
        es = [jnp.exp(l_p - mx) for l_p in l_nat]
        den = functools.reduce(jnp.add, es)
        lt_ref[...] = mx + jnp.log(den)
        ws = [e / den for e in es]
        for hd in range(n_heads):
            sl = slice(hd * HEAD_DIM, (hd + 1) * HEAD_DIM)
            o = ws[0][:, hd:hd + 1] * o_nat[0][:, sl]
            for pi in range(1, n_pat):
                o = o + ws[pi][:, hd:hd + 1] * o_nat[pi][:, sl]
            o_ref[:, sl] = o.astype(BF16)

    return pl.pallas_call(
        body, name="attn_combine", grid=(S // PERM_ROWS,),
        in_specs=[_rm_block(d, aw) for d in DILATIONS] + [_rm_block(d, HEAD_DIM) for d in DILATIONS],
        out_specs=[pl.BlockSpec((PERM_ROWS, aw), lambda i: (i, 0)), pl.BlockSpec((PERM_ROWS, HEAD_DIM), lambda i: (i, 0))],
        out_shape=[jax.ShapeDtypeStruct((S, aw), BF16), jax.ShapeDtypeStruct((S, HEAD_DIM), F32)],
        compiler_params=pltpu.CompilerParams(dimension_semantics=("parallel",),
                                             vmem_limit_bytes=_vmem_limit(40 << 20)),
    )(*outs, *lses)


def _band(tm, width, w, row_offset, transpose):
    t = lax.broadcasted_iota(jnp.int32, (tm, width), 0)
    u = lax.broadcasted_iota(jnp.int32, (tm, width), 1)
    dist = (u - t - row_offset) if transpose else (t + row_offset - u)
    return jnp.logical_and(dist >= 0, dist < w).astype(BF16)


def _pool_fwd(h, w_pool, pool_scale, pw, u_col_block):
    S, W = h.shape
    n_groups = len(POOL_WINDOWS)
    gw = pw // n_groups
    tm = min(512, S)
    halo_per_tile = tm // POOL_HALO

    def body(uc_ref, uh_ref, w_ref, sc_ref, p_ref, y_ref, pm_ref, band_cur, band_halo):
        i = pl.program_id(0)

        @pl.when(i == 0)
        def _():
            for g, w in enumerate(POOL_WINDOWS):
                band_cur[g] = _band(tm, tm, w, 0, False)
                band_halo[g] = _band(tm, POOL_HALO, w, POOL_HALO, False)

        t_abs = i * tm + lax.broadcasted_iota(jnp.int32, (tm, 1), 0)
        for g, w in enumerate(POOL_WINDOWS):
            sl = slice(g * gw, (g + 1) * gw)
            uc = uc_ref[:, sl]
            uh = jnp.where(i > 0, uh_ref[:, sl], jnp.zeros((POOL_HALO, gw), BF16))
            ssum = _dot_nn(band_cur[g], uc) + _dot_nn(band_halo[g], uh)
            cnt = jnp.minimum(t_abs + 1, w).astype(F32)
            p = (ssum / cnt - uc.astype(F32)).astype(BF16)
            y = _dot_nn(p, w_ref[g])
            p_ref[:, sl] = p
            y_ref[:, sl] = y.astype(BF16)
            pm_ref[:, sl] = (y * sc_ref[:, sl]).astype(BF16)

    row = pl.BlockSpec((tm, pw), lambda i: (i, 0))
    return pl.pallas_call(
        body, name="pool_fwd", grid=(S // tm,),
        in_specs=[pl.BlockSpec((tm, pw), lambda i: (i, u_col_block)),
                  pl.BlockSpec((POOL_HALO, pw), lambda i: (jnp.maximum(i * halo_per_tile - 1, 0), u_col_block)),
                  pl.BlockSpec((n_groups, gw, gw), lambda i: (0, 0, 0)),
                  pl.BlockSpec((1, pw), lambda i: (0, 0))],
        out_specs=[row, row, row],
        out_shape=[jax.ShapeDtypeStruct((S, pw), BF16)] * 3,
        scratch_shapes=[pltpu.VMEM((n_groups, tm, tm), BF16), pltpu.VMEM((n_groups, tm, POOL_HALO), BF16)],
        compiler_params=pltpu.CompilerParams(dimension_semantics=("arbitrary",),
                                             vmem_limit_bytes=_vmem_limit(24 << 20)),
    )(h, h, w_pool, pool_scale)


def _branch_attn(o_attn, w_ba):
    S, _ = o_attn.shape
    D = w_ba.shape[1]
    tm, tn = min(1024, S), D

    def epi(acc, ex, out, i, j):
        out[0][...] = acc.astype(BF16)

    (y,) = _mm("branch_attn", o_attn, w_ba, "nn", (tm, tn, 1024), [_tile_out((S, D), BF16, tm, tn)], epi)
    return y


def _branch_pool_merge(pm, w_bp, h, y_attn, gate_col0):
    S, _ = pm.shape
    D = w_bp.shape[1]
    tm, tn = min(512, S), D
    ga0, gp0 = gate_col0 // tn, (gate_col0 + D) // tn

    def epi(acc, ex, out, i, j):
        ga_ref, gp_ref, ya_ref = ex
        yp_ref, mg_ref = out
        yp = acc.astype(BF16)
        yp_ref[...] = yp
        mg = ((jax.nn.sigmoid(ga_ref[...]) * ya_ref[...]).astype(F32)
              + jax.nn.sigmoid(gp_ref[...]).astype(F32) * acc)
        mg_ref[...] = mg.astype(BF16)

    y_pool, merged = _mm(
        "branch_pool_merge", pm, w_bp, "nn", (tm, tn, 1024),
        [_tile_out((S, D), BF16, tm, tn), _tile_out((S, D), BF16, tm, tn)], epi,
        extras=[(h, (tm, tn), lambda i, j, k: (i, ga0 + j)), (h, (tm, tn), lambda i, j, k: (i, gp0 + j)),
                (y_attn, (tm, tn), lambda i, j, k: (i, j))])
    return y_pool, merged


def _layer_norm_rows(z, g, b):
    mu = jnp.mean(z, axis=-1, keepdims=True)
    zc = z - mu
    var = jnp.mean(zc * zc, axis=-1, keepdims=True)
    rstd = lax.rsqrt(var + LN_EPS)
    xhat = zc * rstd
    return xhat * g + b, xhat, rstd


def _out_proj_ln(merged, w_out, x, g, b):
    S, D = x.shape
    tm = min(256, S)

    def epi(acc, ex, out, i, j):
        x_ref, g_ref, b_ref = ex
        x1_ref, x1b_ref, xh_ref, rs_ref = out
        y, xhat, rstd = _layer_norm_rows(DEEPNORM_ALPHA * x_ref[...] + acc, g_ref[...], b_ref[...])
        x1_ref[...] = y
        x1b_ref[...] = y.astype(BF16)
        xh_ref[...] = xhat
        rs_ref[...] = jnp.broadcast_to(rstd, (tm, HEAD_DIM))

    row = lambda i, j, k: (i, 0)
    vec = lambda i, j, k: (0, 0)
    return _mm("out_proj_ln", merged, w_out, "nn", (tm, D, D),
               [((S, D), F32, (tm, D), row), ((S, D), BF16, (tm, D), row), ((S, D), F32, (tm, D), row),
                ((S, HEAD_DIM), F32, (tm, HEAD_DIM), row)], epi,
               extras=[(x, (tm, D), row), (g, (1, D), vec), (b, (1, D), vec)])


def _ffn_up(x1b, w1):
    S, D = x1b.shape
    F = w1.shape[1]
    tm, tn = min(1024, S), min(2048, F)

    def epi(acc, ex, out, i, j):
        r = jnp.maximum(acc, 0.0)
        out[0][...] = (r * r).astype(BF16)

    (a,) = _mm("ffn_up", x1b, w1, "nn", (tm, tn, 2 * K_TILE), [_tile_out((S, F), BF16, tm, tn)], epi)
    return a


def _residual_matmul(name, a, w, form, resid, after=None):
    S, D = resid.shape
    tm, tn = min(1024, S), min(2048, D)

    def epi(acc, ex, out, i, j):
        out[0][...] = DEEPNORM_ALPHA * ex[0][...] + acc

    (z,) = _mm(name, a, w, form, (tm, tn, K_TILE), [_tile_out((S, D), F32, tm, tn)], epi,
               extras=[(resid, (tm, tn), lambda i, j, k: (i, j))], after=after)
    return z


def _row_kernel(name, body, row_inputs, vec_inputs, row_outputs, sum_widths, tr):
    S = row_inputs[0].shape[0]
    row = lambda w: pl.BlockSpec((tr, w), lambda i: (i, 0))
    vec = lambda w: pl.BlockSpec((1, w), lambda i: (0, 0))

    def wrapped(*refs):
        body(pl.program_id(0), *refs)

    return pl.pallas_call(
        wrapped, name=name, grid=(S // tr,),
        in_specs=[row(t.shape[1]) for t in row_inputs] + [vec(t.shape[1]) for t in vec_inputs],
        out_specs=[row(w) for w, _ in row_outputs] + [vec(w) for w in sum_widths],
        out_shape=([jax.ShapeDtypeStruct((S, w), dt) for w, dt in row_outputs]
                   + [jax.ShapeDtypeStruct((1, w), F32) for w in sum_widths]),
        compiler_params=pltpu.CompilerParams(dimension_semantics=("arbitrary",),
                                             vmem_limit_bytes=_vmem_limit(40 << 20)),
    )(*row_inputs, *vec_inputs)


def _ln_loss_bwd(z2, g, b, target):
    S, D = z2.shape

    def body(i, z_ref, t_ref, g_ref, b_ref, dz_ref, dzb_ref, dg_ref, db_ref, loss_ref):
        gv = g_ref[...]
        y, xhat, rstd = _layer_norm_rows(z_ref[...], gv, b_ref[...])
        err = y - t_ref[...]
        loss = 0.5 * jnp.sum(jnp.mean(err * err, axis=-1, keepdims=True), axis=0, keepdims=True)
        dy = err * (1.0 / D)
        dz = _layer_norm_bwd(dy, xhat, rstd, gv)
        dz_ref[...] = dz
        dzb_ref[...] = dz.astype(BF16)
        _accumulate_rows(dg_ref, jnp.sum(dy * xhat, axis=0, keepdims=True), i)
        _accumulate_rows(db_ref, jnp.sum(dy, axis=0, keepdims=True), i)
        _accumulate_rows(loss_ref, jnp.broadcast_to(loss, (1, HEAD_DIM)), i)

    return _row_kernel("ln_loss_bwd", body, [z2, target], [g, b], [(D, F32), (D, BF16)], [D, D, HEAD_DIM],
                       min(256, S))


def _ln_bwd(dy, xhat, rstd, g):
    S, D = dy.shape

    def body(i, dy_ref, xh_ref, rs_ref, g_ref, dz_ref, dzb_ref, dg_ref, db_ref):
        dyv, xhat_v = dy_ref[...], xh_ref[...]
        dz = _layer_norm_bwd(dyv, xhat_v, rs_ref[:, :1], g_ref[...])
        dz_ref[...] = dz
        dzb_ref[...] = dz.astype(BF16)
        _accumulate_rows(dg_ref, jnp.sum(dyv * xhat_v, axis=0, keepdims=True), i)
        _accumulate_rows(db_ref, jnp.sum(dyv, axis=0, keepdims=True), i)

    return _row_kernel("ln_bwd", body, [dy, xhat, rstd], [g], [(D, F32), (D, BF16)], [D, D], min(256, S))


def _grad_weight(name, act, cot):
    M, N = act.shape[1], cot.shape[1]
    tm, tn = min(1024, M), min(2048, N)

    def epi(acc, ex, out, i, j):
        out[0][...] = acc.astype(BF16)

    (g,) = _mm(name, act, cot, "tn", (tm, tn, 2 * K_TILE), [_tile_out((M, N), BF16, tm, tn)], epi)
    return g


def _ffn_down_bwd(dz2b, w2, a, after=None):
    S, D = dz2b.shape
    F = w2.shape[0]
    tm, tn = min(1024, S), min(2048, F)

    def epi(acc, ex, out, i, j):
        out[0][...] = (acc * (2.0 * jnp.sqrt(ex[0][...])).astype(F32)).astype(BF16)

    (dh1,) = _mm("ffn_down_bwd", dz2b, w2, "nt", (tm, tn, 2 * K_TILE), [_tile_out((S, F), BF16, tm, tn)], epi,
                 extras=[(a, (tm, tn), lambda i, j, k: (i, j))], after=after)
    return dh1


def _out_proj_bwd(dz1b, w_out, h, y_attn, y_pool, gate_col0, after=None):
    S, D = dz1b.shape
    W = h.shape[1]
    tm = min(256, S)
    assert gate_col0 == 2 * D and W == 4 * D

    def epi(acc, ex, out, i, j):
        gates_ref, ya_ref, yp_ref = ex
        dya_ref, dyp_ref, dh_ref = out
        sa = jax.nn.sigmoid(gates_ref[:, :D])
        sp = jax.nn.sigmoid(gates_ref[:, D:])
        dya_ref[...] = (acc * sa.astype(F32)).astype(BF16)
        dyp_ref[...] = (acc * sp.astype(F32)).astype(BF16)
        dh_ref[:, :D] = (acc * (ya_ref[...] * (sa * (1.0 - sa))).astype(F32)).astype(BF16)
        dh_ref[:, D:] = (acc * (yp_ref[...] * (sp * (1.0 - sp))).astype(F32)).astype(BF16)

    row = lambda i, j, k: (i, 0)
    return _mm("out_proj_bwd", dz1b, w_out, "nt", (tm, D, D),
               [((S, D), BF16, (tm, D), row), ((S, D), BF16, (tm, D), row),
                ((S, W), BF16, (tm, 2 * D), lambda i, j, k: (i, 1))], epi,
               extras=[(h, (tm, 2 * D), lambda i, j, k: (i, 1)), (y_attn, (tm, D), row), (y_pool, (tm, D), row)],
               after=after)


def _branch_attn_bwd(dy_attn, w_ba, o_attn, l_tot, after=None):
    S, D = dy_attn.shape
    aw = w_ba.shape[0]
    n_heads = aw // HEAD_DIM
    tm = min(512, S)

    def epi(acc, ex, out, i, j):
        do_ref, st_ref = out
        do_ref[...] = acc.astype(BF16)
        o = ex[0][...].astype(F32)
        stats = ex[1][...]
        for hd in range(n_heads):
            sl = slice(hd * HEAD_DIM, (hd + 1) * HEAD_DIM)
            stats = _put_column(stats, n_heads + hd, jnp.sum(acc[:, sl] * o[:, sl], axis=-1, keepdims=True))
        st_ref[...] = stats

    row = lambda i, j, k: (i, 0)
    return _mm("branch_attn_bwd", dy_attn, w_ba, "nt", (tm, aw, D),
               [((S, aw), BF16, (tm, aw), row), ((S, HEAD_DIM), F32, (tm, HEAD_DIM), row)], epi,
               extras=[(o_attn, (tm, aw), row), (l_tot, (tm, HEAD_DIM), row)], after=after)


def _branch_pool_bwd(dy_pool, w_bp, y_pre, pool_scale):
    S, D = dy_pool.shape
    pw = w_bp.shape[0]
    tm = min(512, S)

    def epi(acc, ex, out, i, j):
        y_ref, sc_ref = ex
        dyp_ref, dsc_ref = out
        dyp_ref[...] = (acc * sc_ref[...]).astype(BF16)
        _accumulate_rows(dsc_ref, jnp.sum(acc * y_ref[...].astype(F32), axis=0, keepdims=True), i)

    row = lambda i, j, k: (i, 0)
    return _mm("branch_pool_bwd", dy_pool, w_bp, "nt", (tm, pw, D),
               [((S, pw), BF16, (tm, pw), row), _row_sum_out(pw)], epi,
               extras=[(y_pre, (tm, pw), row), (pool_scale, (1, pw), lambda i, j, k: (0, 0))],
               sequential=True)


def _pool_bwd(dh, dy_pre, p, w_pool, pw, u_col_block):
    S, W = dh.shape
    n_groups = len(POOL_WINDOWS)
    gw = pw // n_groups
    tm = min(512, S)
    n_tiles = S // tm
    halo_per_tile = tm // POOL_HALO
    n_halo_blocks = S // POOL_HALO

    def body(dh_in_ref, dyc_ref, dyh_ref, p_ref, w_ref, dh_ref, dw_ref, band_cur, band_halo):
        del dh_in_ref
        i = pl.program_id(0)

        @pl.when(i == 0)
        def _():
            for g, w in enumerate(POOL_WINDOWS):
                band_cur[g] = _band(tm, tm, w, 0, True)
                band_halo[g] = _band(tm, POOL_HALO, w, -tm, True)

        t_cur = i * tm + lax.broadcasted_iota(jnp.int32, (tm, 1), 0)
        t_halo = (i + 1) * tm + lax.broadcasted_iota(jnp.int32, (POOL_HALO, 1), 0)
        for g, w in enumerate(POOL_WINDOWS):
            sl = slice(g * gw, (g + 1) * gw)
            wg = w_ref[g]
            dyc = dyc_ref[:, sl]
            dyh = jnp.where(i < n_tiles - 1, dyh_ref[:, sl], jnp.zeros((POOL_HALO, gw), BF16))
            dp_cur = _dot_nt(dyc, wg)
            dp_halo = _dot_nt(dyh, wg)
            dpc_cur = (dp_cur / jnp.minimum(t_cur + 1, w).astype(F32)).astype(BF16)
            dpc_halo = (dp_halo / jnp.minimum(t_halo + 1, w).astype(F32)).astype(BF16)
            du = _dot_nn(band_cur[g], dpc_cur) + _dot_nn(band_halo[g], dpc_halo) - dp_cur
            dh_ref[:, sl] = du.astype(BF16)
            dw = _dot_tn(p_ref[:, sl], dyc)

            @pl.when(i == 0)
            def _():
                dw_ref[g] = dw

            @pl.when(i > 0)
            def _():
                dw_ref[g] += dw

    row = pl.BlockSpec((tm, pw), lambda i: (i, 0))
    dh_new, dw_pool = pl.pallas_call(
        body, name="pool_bwd", grid=(n_tiles,),
        in_specs=[pl.BlockSpec(memory_space=pl.ANY), row,
                  pl.BlockSpec((POOL_HALO, pw), lambda i: (jnp.minimum((i + 1) * halo_per_tile, n_halo_blocks - 1), 0)),
                  row, pl.BlockSpec((n_groups, gw, gw), lambda i: (0, 0, 0))],
        out_specs=[pl.BlockSpec((tm, pw), lambda i: (i, u_col_block)),
                   pl.BlockSpec((n_groups, gw, gw), lambda i: (0, 0, 0))],
        out_shape=[jax.ShapeDtypeStruct((S, W), BF16), jax.ShapeDtypeStruct((n_groups, gw, gw), F32)],
        scratch_shapes=[pltpu.VMEM((n_groups, tm, tm), BF16), pltpu.VMEM((n_groups, tm, POOL_HALO), BF16)],
        input_output_aliases={0: 0},
        compiler_params=pltpu.CompilerParams(dimension_semantics=("arbitrary",),
                                             vmem_limit_bytes=_vmem_limit(24 << 20)),
    )(dh, dy_pre, dy_pre, p, w_pool)
    return dh_new, dw_pool


def _attn_bwd(qkv, d_out, stats, d, aw):
    _, rows, _ = qkv.shape
    n_heads = aw // HEAD_DIM
    nb = rows // SUB_BLOCK
    n_blocks = d * nb

    def body(q_ref, kp_ref, kc_ref, vp_ref, vc_ref, do_ref, st_ref, dq_ref, dk_ref, dv_ref,
             carry_k, carry_v, s_buf, dp_buf, p_buf, ds_buf):
        step = pl.program_id(0)

        @pl.when(step == 0)
        def _():
            carry_k[...] = jnp.zeros_like(carry_k)
            carry_v[...] = jnp.zeros_like(carry_v)

        @pl.when(step < n_blocks)
        def _():
            mask = _attn_mask(step % nb)
            st = st_ref[...]
            for hd in range(n_heads):
                sl = slice(hd * HEAD_DIM, (hd + 1) * HEAD_DIM)
                s_buf[hd] = _dot_nt(q_ref[:, sl], _both_blocks(kp_ref, kc_ref, sl))
                dp_buf[hd] = _dot_nt(do_ref[:, sl], _both_blocks(vp_ref, vc_ref, sl))
            for hd in range(n_heads):
                lt, dl = st[:, hd:hd + 1], st[:, n_heads + hd:n_heads + hd + 1]
                p = jnp.where(mask, jnp.exp(jnp.where(mask, s_buf[hd] * ATTN_SCALE - lt, NEG_BIG)), 0.0)
                p_buf[hd] = p.astype(BF16)
                ds_buf[hd] = (p * (dp_buf[hd] - dl) * ATTN_SCALE).astype(BF16)
            for hd in range(n_heads):
                sl = slice(hd * HEAD_DIM, (hd + 1) * HEAD_DIM)
                dq_ref[:, sl] = _dot_nn(ds_buf[hd], _both_blocks(kp_ref, kc_ref, sl)).astype(BF16)
                dk_both = _dot_tn(ds_buf[hd], q_ref[:, sl])
                dv_both = _dot_tn(p_buf[hd], do_ref[:, sl])
                dk_ref[:, sl] = (carry_k[:, sl] + dk_both[:SUB_BLOCK]).astype(BF16)
                dv_ref[:, sl] = (carry_v[:, sl] + dv_both[:SUB_BLOCK]).astype(BF16)
                carry_k[:, sl] = dk_both[SUB_BLOCK:]
                carry_v[:, sl] = dv_both[SUB_BLOCK:]

        @pl.when(step == n_blocks)
        def _():
            dk_ref[...] = carry_k[...].astype(BF16)
            dv_ref[...] = carry_v[...].astype(BF16)

    def cur(step):
        return jnp.minimum(step, n_blocks - 1)

    def qkv_spec(col, prev):
        if prev:
            return pl.BlockSpec((SUB_BLOCK, aw), lambda s: (jnp.maximum(cur(s) - 1, 0), col))
        return pl.BlockSpec((SUB_BLOCK, aw), lambda s: (cur(s), col))

    def at_cur(w):
        return pl.BlockSpec((SUB_BLOCK, w), lambda s: (cur(s), 0))

    finished = pl.BlockSpec((SUB_BLOCK, aw), lambda s: (jnp.maximum(s - 1, 0), 0))
    pair = (n_heads, SUB_BLOCK, 2 * SUB_BLOCK)
    flat = lambda t: t.reshape(d * rows, t.shape[-1])
    qkv2 = flat(qkv)
    outs = pl.pallas_call(
        body, name=f"attn_bwd_d{d}", grid=(n_blocks + 1,),
        in_specs=[qkv_spec(0, False), qkv_spec(1, True), qkv_spec(1, False), qkv_spec(2, True), qkv_spec(2, False),
                  at_cur(aw), at_cur(HEAD_DIM)],
        out_specs=[at_cur(aw), finished, finished],
        out_shape=[jax.ShapeDtypeStruct((d * rows, aw), BF16)] * 3,
        scratch_shapes=[pltpu.VMEM((SUB_BLOCK, aw), F32), pltpu.VMEM((SUB_BLOCK, aw), F32),
                        pltpu.VMEM(pair, F32), pltpu.VMEM(pair, F32), pltpu.VMEM(pair, BF16), pltpu.VMEM(pair, BF16)],
        compiler_params=pltpu.CompilerParams(dimension_semantics=("arbitrary",),
                                             vmem_limit_bytes=_vmem_limit(24 << 20)),
    )(qkv2, qkv2, qkv2, qkv2, qkv2, flat(d_out), flat(stats))
    return [t.reshape(d, rows, aw) for t in outs]


def _attn_bwd_finish(dh, per_pattern, cos2, sin_bwd, aw):
    S, W = dh.shape
    n_heads = aw // HEAD_DIM
    n_pat = len(DILATIONS)

    def body(*refs):
        grad_refs = refs[1:1 + 3 * n_pat]
        cos_ref, sin_ref = refs[1 + 3 * n_pat], refs[2 + 3 * n_pat]
        out_ref = refs[3 + 3 * n_pat]
        perms = {d: _perm_matrix(d, False) for d in DILATIONS if d > 1}
        totals = []
        for which in range(3):
            tot = None
            for pi, d in enumerate(DILATIONS):
                g = grad_refs[which * n_pat + pi][...].reshape(PERM_ROWS, aw)
                g = _permute_rows(perms[d], g) if d > 1 else g.astype(F32)
                tot = g if tot is None else tot + g
            totals.append(tot)
        dq, dk, dv = totals
        c, s = cos_ref[...], sin_ref[...]
        for hd in range(n_heads):
            sl = slice(hd * HEAD_DIM, (hd + 1) * HEAD_DIM)
            out_ref[:, sl] = _rope_apply(dq[:, sl], c, s).astype(BF16)
            out_ref[:, aw + hd * HEAD_DIM:aw + (hd + 1) * HEAD_DIM] = _rope_apply(dk[:, sl], c, s).astype(BF16)
        out_ref[:, 2 * aw:] = dv.astype(BF16)

    grads = [pp[which] for which in range(3) for pp in per_pattern]
    rope_spec = pl.BlockSpec((PERM_ROWS, HEAD_DIM), lambda i: (i, 0))
    return pl.pallas_call(
        body, name="attn_bwd_finish", grid=(S // PERM_ROWS,),
        in_specs=([pl.BlockSpec(memory_space=pl.ANY)] + [_rm_block(d, aw) for d in DILATIONS] * 3
                  + [rope_spec, rope_spec]),
        out_specs=pl.BlockSpec((PERM_ROWS, 3 * aw), lambda i: (i, 0)),
        out_shape=jax.ShapeDtypeStruct((S, W), BF16),
        input_output_aliases={0: 0},
        compiler_params=pltpu.CompilerParams(dimension_semantics=("parallel",),
                                             vmem_limit_bytes=_vmem_limit(32 << 20)),
    )(dh, *grads, cos2, sin_bwd)


def _my_place():
    x, y, c = lax.axis_index("x"), lax.axis_index("y"), lax.axis_index("c")
    return x, y, c


def _flat(px, py, pc):
    return 4 * px + 2 * py + pc


def _shard_slice(ref, axis, idx, size):
    start = pl.multiple_of(idx * size, size)
    ix = [slice(None)] * len(ref.shape)
    ix[axis] = pl.ds(start, size)
    return ref.at[tuple(ix)]


_HBM_SPEC = pl.BlockSpec(memory_space=pltpu.HBM)
_SEM_SPEC = pl.BlockSpec(memory_space=pltpu.SEMAPHORE)
_ANY_SPEC = pl.BlockSpec(memory_space=pl.ANY)
_N_PEER = N_DEV - 1
SIBLING, SAME_CORE_NEIGHBOURS, OTHER_CORE_NEIGHBOURS, DIAGONAL = (1,), (2, 4), (3, 5), (6, 7)
PEER_ORDER = SIBLING + SAME_CORE_NEIGHBOURS + OTHER_CORE_NEIGHBOURS + DIAGONAL


def _peer_of(x, y, c, r):
    return (x ^ ((r >> 2) & 1), y ^ ((r >> 1) & 1), c ^ (r & 1))


class _Exchange:
    def __init__(self, name, part, slot):
        self.name, self.part, self.slot = name, part, slot

    def _copy(self, w, r, src, land, send_sems, recv_sems, sending):
        x, y, c = _my_place()
        peer = _peer_of(x, y, c, r)
        return pltpu.make_async_remote_copy(
            src_ref=self.part(w, src, _flat(*peer)),
            dst_ref=self.slot(w, land, _flat(x, y, c) if sending else _flat(*peer)),
            send_sem=send_sems.at[w * _N_PEER + r - 1], recv_sem=recv_sems.at[w * _N_PEER + r - 1],
            device_id=peer, device_id_type=MESH)

    def start(self, srcs, lands, after=None):
        n = len(srcs)
        n_after = 0 if after is None else 1

        def body(*refs):
            src, land = refs[:n], refs[n:2 * n]
            outs = refs[2 * n + n_after:]
            send_sems, recv_sems, local_sems, token = outs[0], outs[1], outs[2], outs[3 + 2 * n]
            for w in range(n):
                self._own_copy(w, src[w], land[w], local_sems).start()
                for r in PEER_ORDER:
                    self._copy(w, r, src[w], land[w], send_sems, recv_sems, True).start()
            token[...] = jnp.zeros_like(token)

        sems = pltpu.SemaphoreType.DMA((n * _N_PEER,))
        outs = pl.pallas_call(
            body, name=self.name + "_start",
            out_shape=(sems, sems, pltpu.SemaphoreType.DMA((n,)),
                       *[pltpu.HBM(t.shape, t.dtype) for t in list(srcs) + list(lands)],
                       jax.ShapeDtypeStruct((8, 128), F32)),
            in_specs=[_HBM_SPEC] * (2 * n) + [_ANY_SPEC] * n_after,
            out_specs=(_SEM_SPEC, _SEM_SPEC, _SEM_SPEC, *[_HBM_SPEC] * (2 * n), pl.BlockSpec(memory_space=pltpu.VMEM)),
            input_output_aliases={i: 3 + i for i in range(2 * n)},
            compiler_params=pltpu.CompilerParams(has_side_effects=pltpu.SideEffectType.DATAFLOW_SIDE_EFFECTING),
        )(*[pltpu.with_memory_space_constraint(t, pltpu.HBM) for t in list(srcs) + list(lands)],
          *([after] if n_after else []))
        return outs[0], outs[1], outs[2], outs[3:3 + n], outs[3 + n:3 + 2 * n], outs[3 + 2 * n]

    def _own_copy(self, w, src, land, local_sems):
        me = _flat(*_my_place())
        return pltpu.make_async_copy(self.part(w, src, me), self.slot(w, land, me), local_sems.at[w])

    def wait(self, started, after, peers=PEER_ORDER, own=True, tag=""):
        send_sems, recv_sems, local_sems, srcs, lands, token = started
        n = len(srcs)

        def body(*refs):
            src, land = refs[:n], refs[n:2 * n]
            s_sems, r_sems, l_sems = refs[2 * n], refs[2 * n + 1], refs[2 * n + 2]
            for w in range(n):
                if own:
                    self._own_copy(w, src[w], land[w], l_sems).wait()
                for r in peers:
                    cp = self._copy(w, r, src[w], land[w], s_sems, r_sems, False)
                    cp.wait_send()
                    cp.wait_recv()

        outs = pl.pallas_call(
            body, name=self.name + "_wait" + tag,
            out_shape=[pltpu.HBM(t.shape, t.dtype) for t in list(srcs) + list(lands)],
            in_specs=[_HBM_SPEC] * (2 * n) + [_SEM_SPEC, _SEM_SPEC, _SEM_SPEC, _ANY_SPEC],
            out_specs=[_HBM_SPEC] * (2 * n),
            input_output_aliases={i: i for i in range(2 * n)},
            compiler_params=pltpu.CompilerParams(has_side_effects=pltpu.SideEffectType.DATAFLOW_SIDE_EFFECTING),
        )(*srcs, *lands, send_sems, recv_sems, local_sems, after)
        return outs[n:], (send_sems, recv_sems, local_sems, outs[:n], outs[n:], token)


DIRECT_PEERS = (1, 2, 4, 6)
FORWARDED = (3, 5, 7)


class _TwoLevelGather:
    def __init__(self, name, axes, sizes):
        self.name, self.axes, self.sizes = name, axes, sizes

    def _place(self, w, land, dev):
        return _shard_slice(land, self.axes[w], dev, self.sizes[w])

    def _direct(self, w, r, src, land, sems, sending):
        x, y, c = _my_place()
        peer = _peer_of(x, y, c, r)
        k = w * len(DIRECT_PEERS) + DIRECT_PEERS.index(r)
        return pltpu.make_async_remote_copy(
            src_ref=src, dst_ref=self._place(w, land, _flat(x, y, c) if sending else _flat(*peer)),
            send_sem=sems[0].at[k], recv_sem=sems[1].at[k], device_id=peer, device_id_type=MESH)

    def _passed_on(self, w, f, land, sems, sending):
        x, y, c = _my_place()
        owner = _flat(*_peer_of(x, y, c, (f ^ 1) if sending else f))
        slot = self._place(w, land, owner)
        k = w * len(FORWARDED) + FORWARDED.index(f)
        return pltpu.make_async_remote_copy(
            src_ref=slot, dst_ref=slot, send_sem=sems[2].at[k], recv_sem=sems[3].at[k],
            device_id=(x, y, 1 - c), device_id_type=MESH)

    def _own(self, w, src, land, sems):
        return pltpu.make_async_copy(src, self._place(w, land, _flat(*_my_place())), sems[4].at[w])

    def _call(self, suffix, body, sems, srcs, lands, after, make_sems):
        n = len(srcs)
        n_after = 0 if after is None else 1
        bufs = list(srcs) + list(lands)

        def wrapped(*refs):
            ins = refs[:2 * n]
            rest = refs[2 * n + (n_after if make_sems else 0):]
            body(ins[:n], ins[n:], rest[:5], rest[-1] if make_sems else None)

        buf_shapes = [pltpu.HBM(t.shape, t.dtype) for t in bufs]
        if make_sems:
            sem_types = [pltpu.SemaphoreType.DMA((n * len(DIRECT_PEERS),))] * 2 \
                + [pltpu.SemaphoreType.DMA((n * len(FORWARDED),))] * 2 + [pltpu.SemaphoreType.DMA((n,))]
            outs = pl.pallas_call(
                wrapped, name=self.name + suffix,
                out_shape=(*sem_types, *buf_shapes, jax.ShapeDtypeStruct((8, 128), F32)),
                in_specs=[_HBM_SPEC] * (2 * n) + [_ANY_SPEC] * n_after,
                out_specs=(*[_SEM_SPEC] * 5, *[_HBM_SPEC] * (2 * n), pl.BlockSpec(memory_space=pltpu.VMEM)),
                input_output_aliases={i: 5 + i for i in range(2 * n)},
                compiler_params=pltpu.CompilerParams(has_side_effects=pltpu.SideEffectType.DATAFLOW_SIDE_EFFECTING),
            )(*[pltpu.with_memory_space_constraint(t, pltpu.HBM) for t in bufs], *([after] if n_after else []))
            return tuple(outs[:5]), outs[5:5 + n], outs[5 + n:5 + 2 * n], outs[5 + 2 * n]
        outs = pl.pallas_call(
            wrapped, name=self.name + suffix,
            out_shape=buf_shapes,
            in_specs=[_HBM_SPEC] * (2 * n) + [_SEM_SPEC] * 5 + [_ANY_SPEC] * n_after,
            out_specs=[_HBM_SPEC] * (2 * n),
            input_output_aliases={i: i for i in range(2 * n)},
            compiler_params=pltpu.CompilerParams(has_side_effects=pltpu.SideEffectType.DATAFLOW_SIDE_EFFECTING),
        )(*bufs, *sems, *([after] if n_after else []))
        return sems, outs[:n], outs[n:], None

    def start(self, srcs, lands, after=None):
        n = len(srcs)

        def body(src, land, sems, token):
            for w in range(n):
                self._own(w, src[w], land[w], sems).start()
                for r in DIRECT_PEERS:
                    self._direct(w, r, src[w], land[w], sems, True).start()
            token[...] = jnp.zeros_like(token)

        return self._call("_start", body, None, srcs, lands, after, True)

    def forward(self, state, after, which, tag=""):
        sems, srcs, lands, token = state
        n = len(srcs)

        def body(src, land, sem_refs, _):
            for w in range(n):
                for r in which:
                    self._direct(w, r, src[w], land[w], sem_refs, False).wait_recv()
                    self._passed_on(w, r | 1, land[w], sem_refs, True).start()

        sems, srcs, lands, _ = self._call("_forward" + tag, body, sems, srcs, lands, after, False)
        return sems, srcs, lands, token

    def wait(self, state, after, direct=(), passed_on=(), sends=False, tag=""):
        sems, srcs, lands, token = state
        n = len(srcs)

        def body(src, land, sem_refs, _):
            for w in range(n):
                for r in direct:
                    self._direct(w, r, src[w], land[w], sem_refs, False).wait_recv()
                for f in passed_on:
                    self._passed_on(w, f, land[w], sem_refs, False).wait_recv()
                if sends:
                    self._own(w, src[w], land[w], sem_refs).wait()
                    for r in DIRECT_PEERS:
                        self._direct(w, r, src[w], land[w], sem_refs, True).wait_send()
                    for f in FORWARDED:
                        self._passed_on(w, f, land[w], sem_refs, True).wait_send()

        sems, srcs, lands, _ = self._call("_wait" + tag, body, sems, srcs, lands, after, False)
        return lands, (sems, srcs, lands, token)


def _scatter_exchange(name, axes, shard_sizes):
    def part(w, src, dev):
        return src if axes[w] is None else _shard_slice(src, axes[w], dev, shard_sizes[w])
    return _Exchange(name, part, lambda w, land, dev: land.at[dev])


def _adamw(name, partials, w, m, v):
    R, C = w.shape
    tr = R
    while tr * C * 4 > (1 << 20) and tr % 16 == 0:
        tr //= 2

    def body(p_ref, w_ref, m_ref, v_ref, g_ref, d_ref, nm_ref, nv_ref):
        g = p_ref[0].astype(F32)
        for jdev in range(1, N_DEV):
            g = g + p_ref[jdev].astype(F32)
        nm = ADAM_B1 * m_ref[...] + (1.0 - ADAM_B1) * g
        nv = ADAM_B2 * v_ref[...] + (1.0 - ADAM_B2) * (g * g)
        m_hat = nm / (1.0 - ADAM_B1 ** ADAM_STEP)
        v_hat = nv / (1.0 - ADAM_B2 ** ADAM_STEP)
        g_ref[...] = g
        d_ref[...] = -ADAM_LR * (m_hat / (jnp.sqrt(v_hat) + ADAM_EPS) + ADAM_WD * w_ref[...])
        nm_ref[...] = nm
        nv_ref[...] = nv

    spec = pl.BlockSpec((tr, C), lambda i: (i, 0))
    return pl.pallas_call(
        body, name=name, grid=(R // tr,),
        in_specs=[pl.BlockSpec((N_DEV, tr, C), lambda i: (0, i, 0)), spec, spec, spec],
        out_specs=[spec] * 4,
        out_shape=[jax.ShapeDtypeStruct((R, C), F32)] * 4,
        compiler_params=pltpu.CompilerParams(dimension_semantics=("parallel",),
                                             vmem_limit_bytes=_vmem_limit(24 << 20)),
    )(partials, w, m, v)


def _local_step(x, cos2, sin_fwd, sin_bwd, project_in, mix_weights, ffn_weights, pool_scale, g_mix, b_mix, g_ff, b_ff,
                target, send):
    S, D = x.shape
    aw = pw = D // 2
    u_col_block = 3
    gate_col0 = 4 * aw

    xb = x.astype(BF16)
    h, w_in = project_in(xb)
    dilated = [d for d in DILATIONS if d > 1]
    qkv = {1: h[None], **dict(zip(dilated, _to_residue_major("qkv_to_rm", h, 0, 3 * aw)))}
    fwd = [_attn_fwd(qkv[d], d, aw) for d in DILATIONS]
    o_attn, l_tot = _attn_combine([f[0] for f in fwd], [f[1] for f in fwd], aw)
    w_pool, w_ba, w_bp, w_out = mix_weights(o_attn)
    p, y_pre, pm = _pool_fwd(h, w_pool, pool_scale, pw, u_col_block)
    y_attn = _branch_attn(o_attn, w_ba)
    y_pool, merged = _branch_pool_merge(pm, w_bp, h, y_attn, gate_col0)
    w1, w2 = ffn_weights(merged)
    x1, x1b, xhat1, rstd1 = _out_proj_ln(merged, w_out, x, g_mix, b_mix)
    a = _ffn_up(x1b, w1)
    z2 = _residual_matmul("ffn_down", a, w2, "nn", x1)
    dz2, dz2b, dg_ff, db_ff, loss = _ln_loss_bwd(z2, g_ff, b_ff, target)

    tok = send("ff2", [_grad_weight("grad_w_ff2", a, dz2b)])
    dh1 = _ffn_down_bwd(dz2b, w2, a, after=tok)
    tok = send("ff1", [_grad_weight("grad_w_ff1", x1b, dh1)])
    dy1 = _residual_matmul("ffn_up_bwd", dh1, w1, "nt", dz2, after=tok)
    dz1, dz1b, dg_mix, db_mix = _ln_bwd(dy1, xhat1, rstd1, g_mix)
    tok = send("out", [_grad_weight("grad_w_out", merged, dz1b)])
    dy_attn, dy_pool, dh = _out_proj_bwd(dz1b, w_out, h, y_attn, y_pool, gate_col0, after=tok)
    tok = send("branch", [_grad_weight("grad_w_branch_attn", o_attn, dy_attn),
                          _grad_weight("grad_w_branch_pool", pm, dy_pool)])
    d_out, stats = _branch_attn_bwd(dy_attn, w_ba, o_attn, l_tot, after=tok)
    dy_pre, d_scale = _branch_pool_bwd(dy_pool, w_bp, y_pre, pool_scale)
    dh, dw_pool = _pool_bwd(dh, dy_pre, p, w_pool, pw, u_col_block)
    d_outs = {1: d_out[None], **dict(zip(dilated, _to_residue_major("dout_to_rm", d_out, 0, aw)))}
    statss = {1: stats[None], **dict(zip(dilated, _to_residue_major("stats_to_rm", stats, 0, HEAD_DIM)))}
    per_pattern = [_attn_bwd(qkv[d], d_outs[d], statss[d], d, aw) for d in DILATIONS]
    dh = _attn_bwd_finish(dh, per_pattern, cos2, sin_bwd, aw)
    small = jnp.concatenate((d_scale, dg_mix, db_mix, dg_ff, db_ff), axis=-1)
    tok = send("in", [_grad_weight("grad_w_in", xb, dh), dw_pool.astype(BF16),
                      small.reshape(small.shape[-1] // HEAD_DIM, HEAD_DIM)])
    grad_x = _residual_matmul("in_proj_bwd", dh, w_in, "nt", dz1, after=tok)
    return loss, grad_x


def _rope_tables(positions):
    half = HEAD_DIM // 2
    inv_freq = ROPE_THETA ** (-jnp.arange(half, dtype=F32) / half)
    ang = positions.astype(F32)[:, None] * inv_freq
    cos, sin = jnp.cos(ang), jnp.sin(ang)
    cos2 = jnp.concatenate([cos, cos], axis=-1)
    sin_fwd = jnp.concatenate([-sin, sin], axis=-1)
    return cos2, sin_fwd, -sin_fwd


def kernel(x, positions, w_in, w_pool, pool_scale, w_branch_attn, w_branch_pool, w_out, ln_mix_g, ln_mix_b, w_ff1, w_ff2, ln_ff_g, ln_ff_b, loss_target, m_w_in, m_w_pool, m_pool_scale, m_w_branch_attn, m_w_branch_pool, m_w_out, m_ln_mix_g, m_ln_mix_b, m_w_ff1, m_w_ff2, m_ln_ff_g, m_ln_ff_b, v_w_in, v_w_pool, v_pool_scale, v_w_branch_attn, v_w_branch_pool, v_w_out, v_ln_mix_g, v_ln_mix_b, v_w_ff1, v_w_ff2, v_ln_ff_g, v_ln_ff_b):
    big_w = (w_in[0], w_pool[0], w_branch_attn[0], w_branch_pool[0], w_out[0], w_ff1[0], w_ff2[0])
    big_m = (m_w_in[0], m_w_pool[0], m_w_branch_attn[0], m_w_branch_pool[0], m_w_out[0], m_w_ff1[0], m_w_ff2[0])
    big_v = (v_w_in[0], v_w_pool[0], v_w_branch_attn[0], v_w_branch_pool[0], v_w_out[0], v_w_ff1[0], v_w_ff2[0])
    shard_axes = (1, 1, 1, 1, 0, 1, 0)
    small_w = (pool_scale, ln_mix_g, ln_mix_b, ln_ff_g, ln_ff_b)
    small_m = (m_pool_scale, m_ln_mix_g, m_ln_mix_b, m_ln_ff_g, m_ln_ff_b)
    small_v = (v_pool_scale, v_ln_mix_g, v_ln_mix_b, v_ln_ff_g, v_ln_ff_b)

    names = ("w_in", "w_pool", "w_branch_attn", "w_branch_pool", "w_out", "w_ff1", "w_ff2")
    axis_of = dict(zip(names, shard_axes))
    shard_of = dict(zip(names, [w.astype(BF16) for w in big_w]))

    def full_buffer(n):
        s, ax = shard_of[n], axis_of[n]
        full = list(s.shape)
        full[ax] *= N_DEV
        return lax.empty(tuple(full), s.dtype)

    def gather_group(tag, group, after):
        ex = _TwoLevelGather(tag, [axis_of[n] for n in group], [shard_of[n].shape[axis_of[n]] for n in group])
        return ex, ex.start([shard_of[n] for n in group], [full_buffer(n) for n in group], after)

    in_ex, in_state = gather_group("gather_in", ("w_in",), None)
    mix_ex, mix_state = gather_group("gather_mix", ("w_pool", "w_branch_attn", "w_branch_pool", "w_out"),
                                     in_state[-1])
    ffn_ex, ffn_state = gather_group("gather_ffn", ("w_ff1", "w_ff2"), mix_state[-1])
    states = {"mix": mix_state, "ffn": ffn_state}
    me = 4 * lax.axis_index("x") + 2 * lax.axis_index("y") + lax.axis_index("c")
    block_cols = shard_of["w_in"].shape[1]
    neighbours, diagonal = (2, 4), (6,)

    def project_in(xb):
        n_rope_blocks = 2 * (x.shape[-1] // 2) // block_cols

        def piece(tag, w, blocks, h, **kw):
            return _in_proj_piece("in_proj_" + tag, xb, w, jnp.stack(blocks).astype(jnp.int32), cos2, sin_fwd, h,
                                  block_cols, n_rope_blocks, **kw)

        h = piece("own", shard_of["w_in"], [me], None, own_shard=True, after=ffn_state[-1])
        (w_in_land,), state = in_ex.wait(in_state, h, direct=(1,), tag="_sibling")
        h = piece("sibling", w_in_land, [me ^ 1], h)
        state = in_ex.forward(state, h, neighbours, tag="_neighbours")
        h = piece("neighbours", state[2][0], [me ^ r for r in neighbours], h)
        state = in_ex.forward(state, h, diagonal, tag="_diagonal")
        h = piece("diagonal", state[2][0], [me ^ r for r in diagonal], h)
        states["mix"] = mix_ex.forward(states["mix"], h, neighbours + diagonal)
        (w_in_land,), _ = in_ex.wait(state, h, passed_on=FORWARDED, sends=True, tag="_passed_on")
        h = piece("passed_on", w_in_land, [me ^ f for f in FORWARDED], h)
        return h, w_in_land

    def mix_weights(after):
        states["ffn"] = ffn_ex.forward(states["ffn"], after, neighbours + diagonal)
        return mix_ex.wait(states["mix"], after, direct=(1,), passed_on=FORWARDED, sends=True)[0]

    def ffn_weights(after):
        return ffn_ex.wait(states["ffn"], after, direct=(1,), passed_on=FORWARDED, sends=True)[0]

    groups = {"ff2": ("w_ff2",), "ff1": ("w_ff1",), "out": ("w_out",),
              "branch": ("w_branch_attn", "w_branch_pool"), "in": ("w_in", "w_pool", "small")}
    sent = {}

    def send(key, grads_):
        axes = [axis_of.get(n) for n in groups[key]]
        sizes = [None if ax is None else g.shape[ax] // N_DEV for g, ax in zip(grads_, axes)]
        lands = []
        for g, ax, size in zip(grads_, axes, sizes):
            shard = list(g.shape)
            if ax is not None:
                shard[ax] = size
            lands.append(lax.empty((N_DEV, *shard), g.dtype))
        ex = _scatter_exchange("scatter_" + key, axes, sizes)
        sent[key] = (ex, ex.start(list(grads_), lands))
        return sent[key][1][-1]

    cos2, sin_fwd, sin_bwd = _rope_tables(positions[0])
    loss, grad_x = _local_step(
        x[0], cos2, sin_fwd, sin_bwd, project_in, mix_weights, ffn_weights, pool_scale, ln_mix_g, ln_mix_b,
        ln_ff_g, ln_ff_b, loss_target[0], send)

    state = dict(zip(names, zip(big_w, big_m, big_v)))
    n_small = sum(w.shape[-1] for w in small_w)
    small_2d = (n_small // HEAD_DIM, HEAD_DIM)
    state["small"] = tuple(jnp.concatenate(t, axis=-1).reshape(small_2d) for t in (small_w, small_m, small_v))
    grads, deltas, new_ms, new_vs = {}, {}, {}, {}
    after = grad_x
    for key in ("ff2", "ff1", "out", "branch", "in"):
        ex, started = sent[key]
        for n, part in zip(groups[key], ex.wait(started, after)[0]):
            w, m, v = state[n]
            r2 = (-1, w.shape[-1])
            w2d = w.reshape(r2)
            res = _adamw("adamw_" + n, part.reshape((N_DEV,) + w2d.shape), w2d, m.reshape(r2), v.reshape(r2))
            after = res[0]
            if n == "small":
                small_out = [t.reshape(1, n_small) for t in res]
            else:
                grads[n], deltas[n], new_ms[n], new_vs[n] = (t.reshape((1,) + w.shape) for t in res)
    small_names = ("pool_scale", "ln_mix_g", "ln_mix_b", "ln_ff_g", "ln_ff_b")
    off = 0
    for n, w in zip(small_names, small_w):
        width = w.shape[-1]
        grads[n], deltas[n], new_ms[n], new_vs[n] = (t[:, off:off + width] for t in small_out)
        off += width

    order = ("w_in", "w_pool", "pool_scale", "w_branch_attn", "w_branch_pool", "w_out", "ln_mix_g", "ln_mix_b",
             "w_ff1", "w_ff2", "ln_ff_g", "ln_ff_b")
    total_loss = lax.psum(loss[0, 0], ("x", "y", "c"))
    return (total_loss, grad_x[None], *[grads[n] for n in order], *[deltas[n] for n in order],
            *[new_ms[n] for n in order], *[new_vs[n] for n in order])
```

```python
import functools

import jax
import jax.numpy as jnp
from jax import lax
from jax.experimental import pallas as pl
from jax.experimental.pallas import tpu as pltpu

F32 = jnp.float32
BF16 = jnp.bfloat16

N_DEV = 8
HEAD_DIM = 128
SUB_BLOCK = 128
DILATIONS = (1, 4, 16)
POOL_WINDOWS = (2, 4, 8, 16)
MAX_POOL_WINDOW = 16
POOL_HALO = 128
PERM_ROWS = 256
K_TILE = 1024
LN_EPS = 1e-5
DEEPNORM_ALPHA = 2.0 ** 0.25
ROPE_THETA = 10000.0
ATTN_SCALE = HEAD_DIM ** -0.5
ADAM_LR, ADAM_B1, ADAM_B2, ADAM_EPS, ADAM_WD, ADAM_STEP = 0.001, 0.9, 0.999, 1e-08, 0.01, 10
NEG_BIG = -1e30
VMEM_CAP_V7X = 64 * 1024 * 1024
MESH = pl.DeviceIdType.MESH


def _vmem_limit(est_bytes):
    return int(min(max(est_bytes * 5 // 4 + (4 << 20), 16 << 20), VMEM_CAP_V7X - (6 << 20)))


def _nbytes(shape, dtype):
    n = 1
    for s in shape:
        n *= s
    return n * jnp.dtype(dtype).itemsize


def _mm(name, a, b, form, tiles, outs, epi, extras=(), sequential=False, after=None):
    tm, tn, tk = tiles
    if form == "nn":
        (M, K), (K2, N) = a.shape, b.shape
    elif form == "nt":
        (M, K), (N, K2) = a.shape, b.shape
    else:
        (K, M), (K2, N) = a.shape, b.shape
    assert K == K2, (name, a.shape, b.shape)
    tm, tn, tk = min(tm, M), min(tn, N), min(tk, K)
    assert M % tm == 0 and N % tn == 0 and K % tk == 0, (name, M, N, K, tm, tn, tk)
    grid = (M // tm, N // tn, K // tk)
    nk = grid[2]
    if form == "nn":
        a_spec = pl.BlockSpec((tm, tk), lambda i, j, k: (i, k))
        b_spec = pl.BlockSpec((tk, tn), lambda i, j, k: (k, j))
        contract = ((1,), (0,))
    elif form == "nt":
        a_spec = pl.BlockSpec((tm, tk), lambda i, j, k: (i, k))
        b_spec = pl.BlockSpec((tn, tk), lambda i, j, k: (j, k))
        contract = ((1,), (1,))
    else:
        a_spec = pl.BlockSpec((tk, tm), lambda i, j, k: (k, i))
        b_spec = pl.BlockSpec((tk, tn), lambda i, j, k: (k, j))
        contract = ((0,), (0,))
    n_ex, n_out = len(extras), len(outs)
    n_after = 0 if after is None else 1

    def body(a_ref, b_ref, *rest):
        ex_refs = rest[:n_ex]
        rest = rest[n_ex + n_after:]
        out_refs = rest[:n_out]
        i, j, k = pl.program_id(0), pl.program_id(1), pl.program_id(2)

        def prod():
            return lax.dot_general(a_ref[...].astype(BF16), b_ref[...].astype(BF16),
                                   (contract, ((), ())), preferred_element_type=F32)

        if nk == 1:
            epi(prod(), ex_refs, out_refs, i, j)
        else:
            acc = rest[n_out]

            @pl.when(k == 0)
            def _():
                acc[...] = prod()

            @pl.when(jnp.logical_and(k > 0, k < nk - 1))
            def _():
                acc[...] += prod()

            @pl.when(k == nk - 1)
            def _():
                epi(acc[...] + prod(), ex_refs, out_refs, i, j)

    est = 2 * (_nbytes(a_spec.block_shape, a.dtype) + _nbytes(b_spec.block_shape, b.dtype))
    est += sum(2 * _nbytes(bs, arr.dtype) for arr, bs, _ in extras)
    est += sum(2 * _nbytes(bs, dt) for _, dt, bs, _ in outs)
    est += 4 * tm * tn * 4
    sem = ("arbitrary",) * 3 if sequential else ("parallel", "parallel", "arbitrary")
    return pl.pallas_call(
        body, name=name, grid=grid,
        in_specs=([a_spec, b_spec] + [pl.BlockSpec(bs, im) for _, bs, im in extras]
                  + [pl.BlockSpec(memory_space=pl.ANY)] * n_after),
        out_specs=[pl.BlockSpec(bs, im) for _, _, bs, im in outs],
        out_shape=[jax.ShapeDtypeStruct(sh, dt) for sh, dt, _, _ in outs],
        scratch_shapes=[pltpu.VMEM((tm, tn), F32)] if nk > 1 else [],
        compiler_params=pltpu.CompilerParams(dimension_semantics=sem, vmem_limit_bytes=_vmem_limit(est)),
    )(a, b, *[arr for arr, _, _ in extras], *([after] if n_after else []))


def _tile_out(shape, dtype, tm, tn):
    return (shape, dtype, (tm, tn), lambda i, j, k: (i, j))


def _row_sum_out(width):
    return ((1, width), F32, (1, width), lambda i, j, k: (0, 0))


def _accumulate_rows(ref, value, i):
    @pl.when(i == 0)
    def _():
        ref[...] = value

    @pl.when(i > 0)
    def _():
        ref[...] += value


def _layer_norm_bwd(dy, xhat, rstd, g):
    dxh = dy * g
    m1 = jnp.mean(dxh, axis=-1, keepdims=True)
    m2 = jnp.mean(dxh * xhat, axis=-1, keepdims=True)
    return rstd * (dxh - m1 - xhat * m2)


def _rope_apply(t, cos2, sin_signed):
    return t * cos2 + pltpu.roll(t, HEAD_DIM // 2, axis=1) * sin_signed


def _in_proj_piece(name, xb, w_in, col_blocks, cos2, sin_fwd, h_so_far, block_cols, n_rope_blocks, own_shard=False,
                   after=None):
    S, D = xb.shape
    W = w_in.shape[1] * (N_DEV if own_shard else 1)
    tm = min(1024, S)
    n_blocks = col_blocks.shape[0]

    def body(cols_ref, x_ref, w_ref, cos_ref, sin_ref, *rest):
        h_ref = rest[-2] if own_shard else rest[-1]
        j = pl.program_id(1)
        if own_shard:
            rest[-1][...] = x_ref[...].astype(BF16)

        @pl.when(cols_ref[j] < n_rope_blocks)
        def _():
            acc = _dot_nn(x_ref[...].astype(BF16), w_ref[...])
            c, s = cos_ref[...], sin_ref[...]
            for hd in range(block_cols // HEAD_DIM):
                sl = slice(hd * HEAD_DIM, (hd + 1) * HEAD_DIM)
                h_ref[:, sl] = _rope_apply(acc[:, sl], c, s).astype(BF16)

        @pl.when(cols_ref[j] >= n_rope_blocks)
        def _():
            h_ref[...] = _dot_nn(x_ref[...].astype(BF16), w_ref[...]).astype(BF16)

    row = pl.BlockSpec((tm, HEAD_DIM), lambda i, j, cols: (i, 0))
    x_spec = pl.BlockSpec((tm, D), lambda i, j, cols: (i, 0))
    carried = ([] if h_so_far is None else [h_so_far]) + ([] if after is None else [after])
    est = (2 * (_nbytes((tm, D), xb.dtype) + D * block_cols * 2 + tm * block_cols * 2 + 2 * tm * HEAD_DIM * 4)
           + 3 * tm * block_cols * 4 + (3 * tm * D * 2 if own_shard else 0))
    h_shape = jax.ShapeDtypeStruct((S, W), BF16)
    return pl.pallas_call(
        body, name=name,
        grid_spec=pltpu.PrefetchScalarGridSpec(
            num_scalar_prefetch=1, grid=(S // tm, n_blocks),
            in_specs=[x_spec, pl.BlockSpec((D, block_cols), lambda i, j, cols: (0, 0 if own_shard else cols[j])),
                      row, row] + [pl.BlockSpec(memory_space=pl.ANY)] * len(carried),
            out_specs=[pl.BlockSpec((tm, block_cols), lambda i, j, cols: (i, cols[j]))] + [x_spec] * own_shard),
        out_shape=[h_shape] + [jax.ShapeDtypeStruct((S, D), BF16)] * own_shard,
        input_output_aliases={} if h_so_far is None else {5: 0},
        compiler_params=pltpu.CompilerParams(dimension_semantics=("parallel", "arbitrary"),
                                             vmem_limit_bytes=_vmem_limit(est)),
    )(col_blocks, xb, w_in, cos2, sin_fwd, *carried)


def _attn_mask(mb):
    qi = lax.broadcasted_iota(jnp.int32, (SUB_BLOCK, 2 * SUB_BLOCK), 0)
    kj = lax.broadcasted_iota(jnp.int32, (SUB_BLOCK, 2 * SUB_BLOCK), 1)
    prev = jnp.logical_and(jnp.logical_and(kj < SUB_BLOCK, kj >= qi), mb > 0)
    cur = jnp.logical_and(kj >= SUB_BLOCK, kj - SUB_BLOCK <= qi)
    return jnp.logical_or(prev, cur)


def _both_blocks(prev_ref, cur_ref, sl):
    return jnp.concatenate([prev_ref[:, sl], cur_ref[:, sl]], axis=0)


def _dot_nt(a, b):
    return lax.dot_general(a, b, (((1,), (1,)), ((), ())), preferred_element_type=F32)


def _dot_tn(a, b):
    return lax.dot_general(a, b, (((0,), (0,)), ((), ())), preferred_element_type=F32)


def _dot_nn(a, b):
    return lax.dot_general(a, b, (((1,), (0,)), ((), ())), preferred_element_type=F32)


def _perm_matrix(d, to_residue_major):
    g = PERM_ROWS // d
    i = lax.broadcasted_iota(jnp.int32, (PERM_ROWS, PERM_ROWS), 0)
    j = lax.broadcasted_iota(jnp.int32, (PERM_ROWS, PERM_ROWS), 1)
    if to_residue_major:
        hit = j == (i % g) * d + i // g
    else:
        hit = j == (i % d) * g + i // d
    return hit.astype(BF16)


def _permute_rows(perm, x, terms=3):
    if x.dtype == BF16:
        return _dot_nn(perm, x)
    hi = x.astype(BF16)
    r1 = x - hi.astype(F32)
    mid = r1.astype(BF16)
    out = _dot_nn(perm, hi) + _dot_nn(perm, mid)
    if terms == 3:
        out = out + _dot_nn(perm, (r1 - mid.astype(F32)).astype(BF16))
    return out


def _rm_block(d, width):
    return pl.BlockSpec((d, PERM_ROWS // d, width), lambda i: (0, i, 0))


def _to_residue_major(name, x, col_block, width):
    S = x.shape[0]
    dils = [d for d in DILATIONS if d > 1]
    chunk = min(width, 1024)

    def body(x_ref, *out_refs):
        for d, o_ref in zip(dils, out_refs):
            perm = _perm_matrix(d, True)
            for c0 in range(0, width, chunk):
                cw = min(chunk, width - c0)
                y = _permute_rows(perm, x_ref[:, c0:c0 + cw])
                o_ref[:, :, c0:c0 + cw] = y.astype(x.dtype).reshape(d, PERM_ROWS // d, cw)

    return pl.pallas_call(
        body, name=name, grid=(S // PERM_ROWS,),
        in_specs=[pl.BlockSpec((PERM_ROWS, width), lambda i: (i, col_block))],
        out_specs=[_rm_block(d, width) for d in dils],
        out_shape=[jax.ShapeDtypeStruct((d, S // d, width), x.dtype) for d in dils],
        compiler_params=pltpu.CompilerParams(dimension_semantics=("parallel",),
                                             vmem_limit_bytes=_vmem_limit(32 << 20)),
    )(x)


def _put_column(tile, col, value):
    lane = lax.broadcasted_iota(jnp.int32, tile.shape, 1)
    return jnp.where(lane == col, value, tile)


def _attn_fwd(qkv, d, aw):
    _, rows, _ = qkv.shape
    n_heads = aw // HEAD_DIM
    nb = rows // SUB_BLOCK

    def body(q_ref, kc_ref, vc_ref, o_ref, lse_ref, kp_ref, vp_ref, s_buf, p_buf):
        step = pl.program_id(0)

        @pl.when(step == 0)
        def _():
            kp_ref[...] = jnp.zeros_like(kp_ref)
            vp_ref[...] = jnp.zeros_like(vp_ref)

        mask = _attn_mask(step % nb)
        for hd in range(n_heads):
            sl = slice(hd * HEAD_DIM, (hd + 1) * HEAD_DIM)
            s_buf[hd] = _dot_nt(q_ref[:, sl], _both_blocks(kp_ref, kc_ref, sl))
        lse_tile = jnp.zeros((SUB_BLOCK, HEAD_DIM), F32)
        inv_tile = jnp.zeros((SUB_BLOCK, HEAD_DIM), F32)
        for hd in range(n_heads):
            s = jnp.where(mask, s_buf[hd] * ATTN_SCALE, NEG_BIG)
            m = jnp.max(s, axis=-1, keepdims=True)
            p = jnp.exp(s - m)
            l = jnp.sum(p, axis=-1, keepdims=True)
            p_buf[hd] = p.astype(BF16)
            lse_tile = _put_column(lse_tile, hd, m + jnp.log(l))
            inv_tile = _put_column(inv_tile, hd, 1.0 / l)
        lse_ref[...] = lse_tile
        for hd in range(n_heads):
            sl = slice(hd * HEAD_DIM, (hd + 1) * HEAD_DIM)
            o = _dot_nn(p_buf[hd], _both_blocks(vp_ref, vc_ref, sl))
            o_ref[:, sl] = o * inv_tile[:, hd:hd + 1]
        kp_ref[...] = kc_ref[...]
        vp_ref[...] = vc_ref[...]

    def block(col, width):
        return pl.BlockSpec((SUB_BLOCK, width), lambda s: (s, col))

    qkv2 = qkv.reshape(d * rows, qkv.shape[-1])
    o, lse = pl.pallas_call(
        body, name=f"attn_fwd_d{d}", grid=(d * nb,),
        in_specs=[block(0, aw), block(1, aw), block(2, aw)],
        out_specs=[block(0, aw), block(0, HEAD_DIM)],
        out_shape=[jax.ShapeDtypeStruct((d * rows, aw), F32), jax.ShapeDtypeStruct((d * rows, HEAD_DIM), F32)],
        scratch_shapes=[pltpu.VMEM((SUB_BLOCK, aw), BF16), pltpu.VMEM((SUB_BLOCK, aw), BF16),
                        pltpu.VMEM((n_heads, SUB_BLOCK, 2 * SUB_BLOCK), F32),
                        pltpu.VMEM((n_heads, SUB_BLOCK, 2 * SUB_BLOCK), BF16)],
        compiler_params=pltpu.CompilerParams(dimension_semantics=("arbitrary",),
                                             vmem_limit_bytes=_vmem_limit(16 << 20)),
    )(qkv2, qkv2, qkv2)
    return o.reshape(d, rows, aw), lse.reshape(d, rows, HEAD_DIM)


def _attn_combine(outs, lses, aw):
    S = outs[0].shape[1]
    n_heads = aw // HEAD_DIM
    n_pat = len(DILATIONS)

    def body(*refs):
        o_refs, l_refs = refs[:n_pat], refs[n_pat:2 * n_pat]
        o_ref, lt_ref = refs[2 * n_pat], refs[2 * n_pat + 1]
        o_nat, l_nat = [], []
        for d, o_r, l_r in zip(DILATIONS, o_refs, l_refs):
            o_p = o_r[...].reshape(PERM_ROWS, aw)
            l_p = l_r[...].reshape(PERM_ROWS, HEAD_DIM)
            if d > 1:
                perm = _perm_matrix(d, False)
                o_p, l_p = _permute_rows(perm, o_p, terms=2), _permute_rows(perm, l_p)
            o_nat.append(o_p)
            l_nat.append(l_p)
        mx = functools.reduce(jnp.maximum, l_nat)
        es = [jnp.exp(l_p - mx) for l_p in l_nat]
        den = functools.reduce(jnp.add, es)
        lt_ref[...] = mx + jnp.log(den)
        ws = [e / den for e in es]
        for hd in range(n_heads):
            sl = slice(hd * HEAD_DIM, (hd + 1) * HEAD_DIM)
            o = ws[0][:, hd:hd + 1] * o_nat[0][:, sl]
            for pi in range(1, n_pat):
                o = o + ws[pi][:, hd:hd + 1] * o_nat[pi][:, sl]
            o_ref[:, sl] = o.astype(BF16)

    return pl.pallas_call(
        body, name="attn_combine", grid=(S // PERM_ROWS,),
        in_specs=[_rm_block(d, aw) for d in DILATIONS] + [_rm_block(d, HEAD_DIM) for d in DILATIONS],
        out_specs=[pl.BlockSpec((PERM_ROWS, aw), lambda i: (i, 0)), pl.BlockSpec((PERM_ROWS, HEAD_DIM), lambda i: (i, 0))],
        out_shape=[jax.ShapeDtypeStruct((S, aw), BF16), jax.ShapeDtypeStruct((S, HEAD_DIM), F32)],
        compiler_params=pltpu.CompilerParams(dimension_semantics=("parallel",),
                                             vmem_limit_bytes=_vmem_limit(40 << 20)),
    )(*outs, *lses)


def _band(tm, width, w, row_offset, transpose):
    t = lax.broadcasted_iota(jnp.int32, (tm, width), 0)
    u = lax.broadcasted_iota(jnp.int32, (tm, width), 1)
    dist = (u - t - row_offset) if transpose else (t + row_offset - u)
    return jnp.logical_and(dist >= 0, dist < w).astype(BF16)


def _pool_fwd(h, w_pool, pool_scale, pw, u_col_block):
    S, W = h.shape
    n_groups = len(POOL_WINDOWS)
    gw = pw // n_groups
    tm = min(512, S)
    halo_per_tile = tm // POOL_HALO

    def body(uc_ref, uh_ref, w_ref, sc_ref, p_ref, y_ref, pm_ref):
        i = pl.program_id(0)
        t_abs = i * tm + lax.broadcasted_iota(jnp.int32, (tm, 1), 0)
        for g, w in enumerate(POOL_WINDOWS):
            sl = slice(g * gw, (g + 1) * gw)
            uc = uc_ref[:, sl]
            uh = jnp.where(i > 0, uh_ref[:, sl], jnp.zeros((POOL_HALO, gw), BF16))
            ssum = _dot_nn(_band(tm, tm, w, 0, False), uc) + _dot_nn(_band(tm, POOL_HALO, w, POOL_HALO, False), uh)
            cnt = jnp.minimum(t_abs + 1, w).astype(F32)
            p = (ssum / cnt - uc.astype(F32)).astype(BF16)
            y = _dot_nn(p, w_ref[g])
            p_ref[:, sl] = p
            y_ref[:, sl] = y.astype(BF16)
            pm_ref[:, sl] = (y * sc_ref[:, sl]).astype(BF16)

    row = pl.BlockSpec((tm, pw), lambda i: (i, 0))
    return pl.pallas_call(
        body, name="pool_fwd", grid=(S // tm,),
        in_specs=[pl.BlockSpec((tm, pw), lambda i: (i, u_col_block)),
                  pl.BlockSpec((POOL_HALO, pw), lambda i: (jnp.maximum(i * halo_per_tile - 1, 0), u_col_block)),
                  pl.BlockSpec((n_groups, gw, gw), lambda i: (0, 0, 0)),
                  pl.BlockSpec((1, pw), lambda i: (0, 0))],
        out_specs=[row, row, row],
        out_shape=[jax.ShapeDtypeStruct((S, pw), BF16)] * 3,
        compiler_params=pltpu.CompilerParams(dimension_semantics=("parallel",),
                                             vmem_limit_bytes=_vmem_limit(24 << 20)),
    )(h, h, w_pool, pool_scale)


def _branch_attn(o_attn, w_ba):
    S, _ = o_attn.shape
    D = w_ba.shape[1]
    tm, tn = min(1024, S), D

    def epi(acc, ex, out, i, j):
        out[0][...] = acc.astype(BF16)

    (y,) = _mm("branch_attn", o_attn, w_ba, "nn", (tm, tn, 1024), [_tile_out((S, D), BF16, tm, tn)], epi)
    return y


def _branch_pool_merge(pm, w_bp, h, y_attn, gate_col0):
    S, _ = pm.shape
    D = w_bp.shape[1]
    tm, tn = min(512, S), D
    ga0, gp0 = gate_col0 // tn, (gate_col0 + D) // tn

    def epi(acc, ex, out, i, j):
        ga_ref, gp_ref, ya_ref = ex
        yp_ref, mg_ref = out
        yp = acc.astype(BF16)
        yp_ref[...] = yp
        mg = (jax.nn.sigmoid(ga_ref[...]).astype(F32) * ya_ref[...].astype(F32)
              + jax.nn.sigmoid(gp_ref[...]).astype(F32) * acc)
        mg_ref[...] = mg.astype(BF16)

    y_pool, merged = _mm(
        "branch_pool_merge", pm, w_bp, "nn", (tm, tn, 1024),
        [_tile_out((S, D), BF16, tm, tn), _tile_out((S, D), BF16, tm, tn)], epi,
        extras=[(h, (tm, tn), lambda i, j, k: (i, ga0 + j)), (h, (tm, tn), lambda i, j, k: (i, gp0 + j)),
                (y_attn, (tm, tn), lambda i, j, k: (i, j))])
    return y_pool, merged


def _layer_norm_rows(z, g, b):
    mu = jnp.mean(z, axis=-1, keepdims=True)
    zc = z - mu
    var = jnp.mean(zc * zc, axis=-1, keepdims=True)
    rstd = lax.rsqrt(var + LN_EPS)
    xhat = zc * rstd
    return xhat * g + b, xhat, rstd


def _out_proj_ln(merged, w_out, x, g, b):
    S, D = x.shape
    tm = min(256, S)

    def epi(acc, ex, out, i, j):
        x_ref, g_ref, b_ref = ex
        x1_ref, x1b_ref, xh_ref, rs_ref = out
        y, xhat, rstd = _layer_norm_rows(DEEPNORM_ALPHA * x_ref[...] + acc, g_ref[...], b_ref[...])
        x1_ref[...] = y
        x1b_ref[...] = y.astype(BF16)
        xh_ref[...] = xhat
        rs_ref[...] = jnp.broadcast_to(rstd, (tm, HEAD_DIM))

    row = lambda i, j, k: (i, 0)
    vec = lambda i, j, k: (0, 0)
    return _mm("out_proj_ln", merged, w_out, "nn", (tm, D, D),
               [((S, D), F32, (tm, D), row), ((S, D), BF16, (tm, D), row), ((S, D), F32, (tm, D), row),
                ((S, HEAD_DIM), F32, (tm, HEAD_DIM), row)], epi,
               extras=[(x, (tm, D), row), (g, (1, D), vec), (b, (1, D), vec)])


def _ffn_up(x1b, w1):
    S, D = x1b.shape
    F = w1.shape[1]
    tm, tn = min(1024, S), min(2048, F)

    def epi(acc, ex, out, i, j):
        r = jnp.maximum(acc, 0.0)
        out[0][...] = (r * r).astype(BF16)

    (a,) = _mm("ffn_up", x1b, w1, "nn", (tm, tn, 2 * K_TILE), [_tile_out((S, F), BF16, tm, tn)], epi)
    return a


def _residual_matmul(name, a, w, form, resid, after=None):
    S, D = resid.shape
    tm, tn = min(1024, S), min(2048, D)

    def epi(acc, ex, out, i, j):
        out[0][...] = DEEPNORM_ALPHA * ex[0][...] + acc

    (z,) = _mm(name, a, w, form, (tm, tn, K_TILE), [_tile_out((S, D), F32, tm, tn)], epi,
               extras=[(resid, (tm, tn), lambda i, j, k: (i, j))], after=after)
    return z


def _row_kernel(name, body, row_inputs, vec_inputs, row_outputs, sum_widths, tr):
    S = row_inputs[0].shape[0]
    row = lambda w: pl.BlockSpec((tr, w), lambda i: (i, 0))
    vec = lambda w: pl.BlockSpec((1, w), lambda i: (0, 0))

    def wrapped(*refs):
        body(pl.program_id(0), *refs)

    return pl.pallas_call(
        wrapped, name=name, grid=(S // tr,),
        in_specs=[row(t.shape[1]) for t in row_inputs] + [vec(t.shape[1]) for t in vec_inputs],
        out_specs=[row(w) for w, _ in row_outputs] + [vec(w) for w in sum_widths],
        out_shape=([jax.ShapeDtypeStruct((S, w), dt) for w, dt in row_outputs]
                   + [jax.ShapeDtypeStruct((1, w), F32) for w in sum_widths]),
        compiler_params=pltpu.CompilerParams(dimension_semantics=("arbitrary",),
                                             vmem_limit_bytes=_vmem_limit(40 << 20)),
    )(*row_inputs, *vec_inputs)


def _ln_loss_bwd(z2, g, b, target):
    S, D = z2.shape

    def body(i, z_ref, t_ref, g_ref, b_ref, dz_ref, dzb_ref, dg_ref, db_ref, loss_ref):
        gv = g_ref[...]
        y, xhat, rstd = _layer_norm_rows(z_ref[...], gv, b_ref[...])
        err = y - t_ref[...]
        loss = 0.5 * jnp.sum(jnp.mean(err * err, axis=-1, keepdims=True), axis=0, keepdims=True)
        dy = err * (1.0 / D)
        dz = _layer_norm_bwd(dy, xhat, rstd, gv)
        dz_ref[...] = dz
        dzb_ref[...] = dz.astype(BF16)
        _accumulate_rows(dg_ref, jnp.sum(dy * xhat, axis=0, keepdims=True), i)
        _accumulate_rows(db_ref, jnp.sum(dy, axis=0, keepdims=True), i)
        _accumulate_rows(loss_ref, jnp.broadcast_to(loss, (1, HEAD_DIM)), i)

    return _row_kernel("ln_loss_bwd", body, [z2, target], [g, b], [(D, F32), (D, BF16)], [D, D, HEAD_DIM],
                       min(256, S))


def _ln_bwd(dy, xhat, rstd, g):
    S, D = dy.shape

    def body(i, dy_ref, xh_ref, rs_ref, g_ref, dz_ref, dzb_ref, dg_ref, db_ref):
        dyv, xhat_v = dy_ref[...], xh_ref[...]
        dz = _layer_norm_bwd(dyv, xhat_v, rs_ref[:, :1], g_ref[...])
        dz_ref[...] = dz
        dzb_ref[...] = dz.astype(BF16)
        _accumulate_rows(dg_ref, jnp.sum(dyv * xhat_v, axis=0, keepdims=True), i)
        _accumulate_rows(db_ref, jnp.sum(dyv, axis=0, keepdims=True), i)

    return _row_kernel("ln_bwd", body, [dy, xhat, rstd], [g], [(D, F32), (D, BF16)], [D, D], min(256, S))


def _grad_weight(name, act, cot):
    M, N = act.shape[1], cot.shape[1]
    tm, tn = min(1024, M), min(2048, N)

    def epi(acc, ex, out, i, j):
        out[0][...] = acc.astype(BF16)

    (g,) = _mm(name, act, cot, "tn", (tm, tn, 2 * K_TILE), [_tile_out((M, N), BF16, tm, tn)], epi)
    return g


def _ffn_down_bwd(dz2b, w2, a, after=None):
    S, D = dz2b.shape
    F = w2.shape[0]
    tm, tn = min(1024, S), min(2048, F)

    def epi(acc, ex, out, i, j):
        out[0][...] = (acc * (2.0 * jnp.sqrt(ex[0][...])).astype(F32)).astype(BF16)

    (dh1,) = _mm("ffn_down_bwd", dz2b, w2, "nt", (tm, tn, 2 * K_TILE), [_tile_out((S, F), BF16, tm, tn)], epi,
                 extras=[(a, (tm, tn), lambda i, j, k: (i, j))], after=after)
    return dh1


def _out_proj_bwd(dz1b, w_out, h, y_attn, y_pool, gate_col0, after=None):
    S, D = dz1b.shape
    W = h.shape[1]
    tm = min(256, S)
    assert gate_col0 == 2 * D and W == 4 * D

    def epi(acc, ex, out, i, j):
        gates_ref, ya_ref, yp_ref = ex
        dya_ref, dyp_ref, dh_ref = out
        sa = jax.nn.sigmoid(gates_ref[:, :D])
        sp = jax.nn.sigmoid(gates_ref[:, D:])
        dya_ref[...] = (acc * sa.astype(F32)).astype(BF16)
        dyp_ref[...] = (acc * sp.astype(F32)).astype(BF16)
        dh_ref[:, :D] = (acc * (ya_ref[...] * (sa * (1.0 - sa))).astype(F32)).astype(BF16)
        dh_ref[:, D:] = (acc * (yp_ref[...] * (sp * (1.0 - sp))).astype(F32)).astype(BF16)

    row = lambda i, j, k: (i, 0)
    return _mm("out_proj_bwd", dz1b, w_out, "nt", (tm, D, D),
               [((S, D), BF16, (tm, D), row), ((S, D), BF16, (tm, D), row),
                ((S, W), BF16, (tm, 2 * D), lambda i, j, k: (i, 1))], epi,
               extras=[(h, (tm, 2 * D), lambda i, j, k: (i, 1)), (y_attn, (tm, D), row), (y_pool, (tm, D), row)],
               after=after)


def _branch_attn_bwd(dy_attn, w_ba, o_attn, l_tot, after=None):
    S, D = dy_attn.shape
    aw = w_ba.shape[0]
    n_heads = aw // HEAD_DIM
    tm = min(512, S)

    def epi(acc, ex, out, i, j):
        do_ref, st_ref = out
        do_ref[...] = acc.astype(BF16)
        o = ex[0][...].astype(F32)
        stats = ex[1][...]
        for hd in range(n_heads):
            sl = slice(hd * HEAD_DIM, (hd + 1) * HEAD_DIM)
            stats = _put_column(stats, n_heads + hd, jnp.sum(acc[:, sl] * o[:, sl], axis=-1, keepdims=True))
        st_ref[...] = stats

    row = lambda i, j, k: (i, 0)
    return _mm("branch_attn_bwd", dy_attn, w_ba, "nt", (tm, aw, D),
               [((S, aw), BF16, (tm, aw), row), ((S, HEAD_DIM), F32, (tm, HEAD_DIM), row)], epi,
               extras=[(o_attn, (tm, aw), row), (l_tot, (tm, HEAD_DIM), row)], after=after)


def _branch_pool_bwd(dy_pool, w_bp, y_pre, pool_scale):
    S, D = dy_pool.shape
    pw = w_bp.shape[0]
    tm = min(512, S)

    def epi(acc, ex, out, i, j):
        y_ref, sc_ref = ex
        dyp_ref, dsc_ref = out
        dyp_ref[...] = (acc * sc_ref[...]).astype(BF16)
        _accumulate_rows(dsc_ref, jnp.sum(acc * y_ref[...].astype(F32), axis=0, keepdims=True), i)

    row = lambda i, j, k: (i, 0)
    return _mm("branch_pool_bwd", dy_pool, w_bp, "nt", (tm, pw, D),
               [((S, pw), BF16, (tm, pw), row), _row_sum_out(pw)], epi,
               extras=[(y_pre, (tm, pw), row), (pool_scale, (1, pw), lambda i, j, k: (0, 0))],
               sequential=True)


def _pool_bwd(dh, dy_pre, p, w_pool, pw, u_col_block):
    S, W = dh.shape
    n_groups = len(POOL_WINDOWS)
    gw = pw // n_groups
    tm = min(512, S)
    n_tiles = S // tm
    halo_per_tile = tm // POOL_HALO
    n_halo_blocks = S // POOL_HALO

    def body(dh_in_ref, dyc_ref, dyh_ref, p_ref, w_ref, dh_ref, dw_ref):
        del dh_in_ref
        i = pl.program_id(0)
        t_cur = i * tm + lax.broadcasted_iota(jnp.int32, (tm, 1), 0)
        t_halo = (i + 1) * tm + lax.broadcasted_iota(jnp.int32, (POOL_HALO, 1), 0)
        for g, w in enumerate(POOL_WINDOWS):
            sl = slice(g * gw, (g + 1) * gw)
            wg = w_ref[g]
            dyc = dyc_ref[:, sl]
            dyh = jnp.where(i < n_tiles - 1, dyh_ref[:, sl], jnp.zeros((POOL_HALO, gw), BF16))
            dp_cur = _dot_nt(dyc, wg)
            dp_halo = _dot_nt(dyh, wg)
            dpc_cur = (dp_cur / jnp.minimum(t_cur + 1, w).astype(F32)).astype(BF16)
            dpc_halo = (dp_halo / jnp.minimum(t_halo + 1, w).astype(F32)).astype(BF16)
            du = (_dot_nn(_band(tm, tm, w, 0, True), dpc_cur)
                  + _dot_nn(_band(tm, POOL_HALO, w, -tm, True), dpc_halo) - dp_cur)
            dh_ref[:, sl] = du.astype(BF16)
            dw = _dot_tn(p_ref[:, sl], dyc)

            @pl.when(i == 0)
            def _():
                dw_ref[g] = dw

            @pl.when(i > 0)
            def _():
                dw_ref[g] += dw

    row = pl.BlockSpec((tm, pw), lambda i: (i, 0))
    dh_new, dw_pool = pl.pallas_call(
        body, name="pool_bwd", grid=(n_tiles,),
        in_specs=[pl.BlockSpec(memory_space=pl.ANY), row,
                  pl.BlockSpec((POOL_HALO, pw), lambda i: (jnp.minimum((i + 1) * halo_per_tile, n_halo_blocks - 1), 0)),
                  row, pl.BlockSpec((n_groups, gw, gw), lambda i: (0, 0, 0))],
        out_specs=[pl.BlockSpec((tm, pw), lambda i: (i, u_col_block)),
                   pl.BlockSpec((n_groups, gw, gw), lambda i: (0, 0, 0))],
        out_shape=[jax.ShapeDtypeStruct((S, W), BF16), jax.ShapeDtypeStruct((n_groups, gw, gw), F32)],
        input_output_aliases={0: 0},
        compiler_params=pltpu.CompilerParams(dimension_semantics=("arbitrary",),
                                             vmem_limit_bytes=_vmem_limit(24 << 20)),
    )(dh, dy_pre, dy_pre, p, w_pool)
    return dh_new, dw_pool


def _attn_bwd(qkv, d_out, stats, d, aw):
    _, rows, _ = qkv.shape
    n_heads = aw // HEAD_DIM
    nb = rows // SUB_BLOCK
    n_blocks = d * nb

    def body(q_ref, kp_ref, kc_ref, vp_ref, vc_ref, do_ref, st_ref, dq_ref, dk_ref, dv_ref,
             carry_k, carry_v, s_buf, dp_buf, p_buf, ds_buf):
        step = pl.program_id(0)

        @pl.when(step == 0)
        def _():
            carry_k[...] = jnp.zeros_like(carry_k)
            carry_v[...] = jnp.zeros_like(carry_v)

        @pl.when(step < n_blocks)
        def _():
            mask = _attn_mask(step % nb)
            st = st_ref[...]
            for hd in range(n_heads):
                sl = slice(hd * HEAD_DIM, (hd + 1) * HEAD_DIM)
                s_buf[hd] = _dot_nt(q_ref[:, sl], _both_blocks(kp_ref, kc_ref, sl))
                dp_buf[hd] = _dot_nt(do_ref[:, sl], _both_blocks(vp_ref, vc_ref, sl))
            for hd in range(n_heads):
                lt, dl = st[:, hd:hd + 1], st[:, n_heads + hd:n_heads + hd + 1]
                p = jnp.where(mask, jnp.exp(jnp.where(mask, s_buf[hd] * ATTN_SCALE - lt, NEG_BIG)), 0.0)
                p_buf[hd] = p.astype(BF16)
                ds_buf[hd] = (p * (dp_buf[hd] - dl) * ATTN_SCALE).astype(BF16)
            for hd in range(n_heads):
                sl = slice(hd * HEAD_DIM, (hd + 1) * HEAD_DIM)
                dq_ref[:, sl] = _dot_nn(ds_buf[hd], _both_blocks(kp_ref, kc_ref, sl)).astype(BF16)
                dk_both = _dot_tn(ds_buf[hd], q_ref[:, sl])
                dv_both = _dot_tn(p_buf[hd], do_ref[:, sl])
                dk_ref[:, sl] = (carry_k[:, sl] + dk_both[:SUB_BLOCK]).astype(BF16)
                dv_ref[:, sl] = (carry_v[:, sl] + dv_both[:SUB_BLOCK]).astype(BF16)
                carry_k[:, sl] = dk_both[SUB_BLOCK:]
                carry_v[:, sl] = dv_both[SUB_BLOCK:]

        @pl.when(step == n_blocks)
        def _():
            dk_ref[...] = carry_k[...].astype(BF16)
            dv_ref[...] = carry_v[...].astype(BF16)

    def cur(step):
        return jnp.minimum(step, n_blocks - 1)

    def qkv_spec(col, prev):
        if prev:
            return pl.BlockSpec((SUB_BLOCK, aw), lambda s: (jnp.maximum(cur(s) - 1, 0), col))
        return pl.BlockSpec((SUB_BLOCK, aw), lambda s: (cur(s), col))

    def at_cur(w):
        return pl.BlockSpec((SUB_BLOCK, w), lambda s: (cur(s), 0))

    finished = pl.BlockSpec((SUB_BLOCK, aw), lambda s: (jnp.maximum(s - 1, 0), 0))
    pair = (n_heads, SUB_BLOCK, 2 * SUB_BLOCK)
    flat = lambda t: t.reshape(d * rows, t.shape[-1])
    qkv2 = flat(qkv)
    outs = pl.pallas_call(
        body, name=f"attn_bwd_d{d}", grid=(n_blocks + 1,),
        in_specs=[qkv_spec(0, False), qkv_spec(1, True), qkv_spec(1, False), qkv_spec(2, True), qkv_spec(2, False),
                  at_cur(aw), at_cur(HEAD_DIM)],
        out_specs=[at_cur(aw), finished, finished],
        out_shape=[jax.ShapeDtypeStruct((d * rows, aw), BF16)] * 3,
        scratch_shapes=[pltpu.VMEM((SUB_BLOCK, aw), F32), pltpu.VMEM((SUB_BLOCK, aw), F32),
                        pltpu.VMEM(pair, F32), pltpu.VMEM(pair, F32), pltpu.VMEM(pair, BF16), pltpu.VMEM(pair, BF16)],
        compiler_params=pltpu.CompilerParams(dimension_semantics=("arbitrary",),
                                             vmem_limit_bytes=_vmem_limit(24 << 20)),
    )(qkv2, qkv2, qkv2, qkv2, qkv2, flat(d_out), flat(stats))
    return [t.reshape(d, rows, aw) for t in outs]


def _attn_bwd_finish(dh, per_pattern, cos2, sin_bwd, aw):
    S, W = dh.shape
    n_heads = aw // HEAD_DIM
    n_pat = len(DILATIONS)

    def body(*refs):
        grad_refs = refs[1:1 + 3 * n_pat]
        cos_ref, sin_ref = refs[1 + 3 * n_pat], refs[2 + 3 * n_pat]
        out_ref = refs[3 + 3 * n_pat]
        perms = {d: _perm_matrix(d, False) for d in DILATIONS if d > 1}
        totals = []
        for which in range(3):
            tot = None
            for pi, d in enumerate(DILATIONS):
                g = grad_refs[which * n_pat + pi][...].reshape(PERM_ROWS, aw)
                g = _permute_rows(perms[d], g) if d > 1 else g.astype(F32)
                tot = g if tot is None else tot + g
            totals.append(tot)
        dq, dk, dv = totals
        c, s = cos_ref[...], sin_ref[...]
        for hd in range(n_heads):
            sl = slice(hd * HEAD_DIM, (hd + 1) * HEAD_DIM)
            out_ref[:, sl] = _rope_apply(dq[:, sl], c, s).astype(BF16)
            out_ref[:, aw + hd * HEAD_DIM:aw + (hd + 1) * HEAD_DIM] = _rope_apply(dk[:, sl], c, s).astype(BF16)
        out_ref[:, 2 * aw:] = dv.astype(BF16)

    grads = [pp[which] for which in range(3) for pp in per_pattern]
    rope_spec = pl.BlockSpec((PERM_ROWS, HEAD_DIM), lambda i: (i, 0))
    return pl.pallas_call(
        body, name="attn_bwd_finish", grid=(S // PERM_ROWS,),
        in_specs=([pl.BlockSpec(memory_space=pl.ANY)] + [_rm_block(d, aw) for d in DILATIONS] * 3
                  + [rope_spec, rope_spec]),
        out_specs=pl.BlockSpec((PERM_ROWS, 3 * aw), lambda i: (i, 0)),
        out_shape=jax.ShapeDtypeStruct((S, W), BF16),
        input_output_aliases={0: 0},
        compiler_params=pltpu.CompilerParams(dimension_semantics=("parallel",),
                                             vmem_limit_bytes=_vmem_limit(32 << 20)),
    )(dh, *grads, cos2, sin_bwd)


def _my_place():
    x, y, c = lax.axis_index("x"), lax.axis_index("y"), lax.axis_index("c")
    return x, y, c


def _flat(px, py, pc):
    return 4 * px + 2 * py + pc


def _shard_slice(ref, axis, idx, size):
    start = pl.multiple_of(idx * size, size)
    ix = [slice(None)] * len(ref.shape)
    ix[axis] = pl.ds(start, size)
    return ref.at[tuple(ix)]


_HBM_SPEC = pl.BlockSpec(memory_space=pltpu.HBM)
_SEM_SPEC = pl.BlockSpec(memory_space=pltpu.SEMAPHORE)
_ANY_SPEC = pl.BlockSpec(memory_space=pl.ANY)
_N_PEER = N_DEV - 1
SIBLING, SAME_CORE_NEIGHBOURS, OTHER_CORE_NEIGHBOURS, DIAGONAL = (1,), (2, 4), (3, 5), (6, 7)
PEER_ORDER = SIBLING + SAME_CORE_NEIGHBOURS + OTHER_CORE_NEIGHBOURS + DIAGONAL


def _peer_of(x, y, c, r):
    return (x ^ ((r >> 2) & 1), y ^ ((r >> 1) & 1), c ^ (r & 1))


class _Exchange:
    def __init__(self, name, part, slot):
        self.name, self.part, self.slot = name, part, slot

    def _copy(self, w, r, src, land, send_sems, recv_sems, sending):
        x, y, c = _my_place()
        peer = _peer_of(x, y, c, r)
        return pltpu.make_async_remote_copy(
            src_ref=self.part(w, src, _flat(*peer)),
            dst_ref=self.slot(w, land, _flat(x, y, c) if sending else _flat(*peer)),
            send_sem=send_sems.at[w * _N_PEER + r - 1], recv_sem=recv_sems.at[w * _N_PEER + r - 1],
            device_id=peer, device_id_type=MESH)

    def start(self, srcs, lands, after=None):
        n = len(srcs)
        n_after = 0 if after is None else 1

        def body(*refs):
            src, land = refs[:n], refs[n:2 * n]
            outs = refs[2 * n + n_after:]
            send_sems, recv_sems, local_sems, token = outs[0], outs[1], outs[2], outs[3 + 2 * n]
            for w in range(n):
                self._own_copy(w, src[w], land[w], local_sems).start()
                for r in PEER_ORDER:
                    self._copy(w, r, src[w], land[w], send_sems, recv_sems, True).start()
            token[...] = jnp.zeros_like(token)

        sems = pltpu.SemaphoreType.DMA((n * _N_PEER,))
        outs = pl.pallas_call(
            body, name=self.name + "_start",
            out_shape=(sems, sems, pltpu.SemaphoreType.DMA((n,)),
                       *[pltpu.HBM(t.shape, t.dtype) for t in list(srcs) + list(lands)],
                       jax.ShapeDtypeStruct((8, 128), F32)),
            in_specs=[_HBM_SPEC] * (2 * n) + [_ANY_SPEC] * n_after,
            out_specs=(_SEM_SPEC, _SEM_SPEC, _SEM_SPEC, *[_HBM_SPEC] * (2 * n), pl.BlockSpec(memory_space=pltpu.VMEM)),
            input_output_aliases={i: 3 + i for i in range(2 * n)},
            compiler_params=pltpu.CompilerParams(has_side_effects=pltpu.SideEffectType.DATAFLOW_SIDE_EFFECTING),
        )(*[pltpu.with_memory_space_constraint(t, pltpu.HBM) for t in list(srcs) + list(lands)],
          *([after] if n_after else []))
        return outs[0], outs[1], outs[2], outs[3:3 + n], outs[3 + n:3 + 2 * n], outs[3 + 2 * n]

    def _own_copy(self, w, src, land, local_sems):
        me = _flat(*_my_place())
        return pltpu.make_async_copy(self.part(w, src, me), self.slot(w, land, me), local_sems.at[w])

    def wait(self, started, after, peers=PEER_ORDER, own=True, tag=""):
        send_sems, recv_sems, local_sems, srcs, lands, token = started
        n = len(srcs)

        def body(*refs):
            src, land = refs[:n], refs[n:2 * n]
            s_sems, r_sems, l_sems = refs[2 * n], refs[2 * n + 1], refs[2 * n + 2]
            for w in range(n):
                if own:
                    self._own_copy(w, src[w], land[w], l_sems).wait()
                for r in peers:
                    cp = self._copy(w, r, src[w], land[w], s_sems, r_sems, False)
                    cp.wait_send()
                    cp.wait_recv()

        outs = pl.pallas_call(
            body, name=self.name + "_wait" + tag,
            out_shape=[pltpu.HBM(t.shape, t.dtype) for t in list(srcs) + list(lands)],
            in_specs=[_HBM_SPEC] * (2 * n) + [_SEM_SPEC, _SEM_SPEC, _SEM_SPEC, _ANY_SPEC],
            out_specs=[_HBM_SPEC] * (2 * n),
            input_output_aliases={i: i for i in range(2 * n)},
            compiler_params=pltpu.CompilerParams(has_side_effects=pltpu.SideEffectType.DATAFLOW_SIDE_EFFECTING),
        )(*srcs, *lands, send_sems, recv_sems, local_sems, after)
        return outs[n:], (send_sems, recv_sems, local_sems, outs[:n], outs[n:], token)


DIRECT_PEERS = (1, 2, 4, 6)
FORWARDED = (3, 5, 7)


class _TwoLevelGather:
    def __init__(self, name, axes, sizes):
        self.name, self.axes, self.sizes = name, axes, sizes

    def _place(self, w, land, dev):
        return _shard_slice(land, self.axes[w], dev, self.sizes[w])

    def _direct(self, w, r, src, land, sems, sending):
        x, y, c = _my_place()
        peer = _peer_of(x, y, c, r)
        k = w * len(DIRECT_PEERS) + DIRECT_PEERS.index(r)
        return pltpu.make_async_remote_copy(
            src_ref=src, dst_ref=self._place(w, land, _flat(x, y, c) if sending else _flat(*peer)),
            send_sem=sems[0].at[k], recv_sem=sems[1].at[k], device_id=peer, device_id_type=MESH)

    def _passed_on(self, w, f, land, sems, sending):
        x, y, c = _my_place()
        owner = _flat(*_peer_of(x, y, c, (f ^ 1) if sending else f))
        slot = self._place(w, land, owner)
        k = w * len(FORWARDED) + FORWARDED.index(f)
        return pltpu.make_async_remote_copy(
            src_ref=slot, dst_ref=slot, send_sem=sems[2].at[k], recv_sem=sems[3].at[k],
            device_id=(x, y, 1 - c), device_id_type=MESH)

    def _own(self, w, src, land, sems):
        return pltpu.make_async_copy(src, self._place(w, land, _flat(*_my_place())), sems[4].at[w])

    def _call(self, suffix, body, sems, srcs, lands, after, make_sems):
        n = len(srcs)
        n_after = 0 if after is None else 1
        bufs = list(srcs) + list(lands)

        def wrapped(*refs):
            ins = refs[:2 * n]
            rest = refs[2 * n + (n_after if make_sems else 0):]
            body(ins[:n], ins[n:], rest[:5], rest[-1] if make_sems else None)

        buf_shapes = [pltpu.HBM(t.shape, t.dtype) for t in bufs]
        if make_sems:
            sem_types = [pltpu.SemaphoreType.DMA((n * len(DIRECT_PEERS),))] * 2 \
                + [pltpu.SemaphoreType.DMA((n * len(FORWARDED),))] * 2 + [pltpu.SemaphoreType.DMA((n,))]
            outs = pl.pallas_call(
                wrapped, name=self.name + suffix,
                out_shape=(*sem_types, *buf_shapes, jax.ShapeDtypeStruct((8, 128), F32)),
                in_specs=[_HBM_SPEC] * (2 * n) + [_ANY_SPEC] * n_after,
                out_specs=(*[_SEM_SPEC] * 5, *[_HBM_SPEC] * (2 * n), pl.BlockSpec(memory_space=pltpu.VMEM)),
                input_output_aliases={i: 5 + i for i in range(2 * n)},
                compiler_params=pltpu.CompilerParams(has_side_effects=pltpu.SideEffectType.DATAFLOW_SIDE_EFFECTING),
            )(*[pltpu.with_memory_space_constraint(t, pltpu.HBM) for t in bufs], *([after] if n_after else []))
            return tuple(outs[:5]), outs[5:5 + n], outs[5 + n:5 + 2 * n], outs[5 + 2 * n]
        outs = pl.pallas_call(
            wrapped, name=self.name + suffix,
            out_shape=buf_shapes,
            in_specs=[_HBM_SPEC] * (2 * n) + [_SEM_SPEC] * 5 + [_ANY_SPEC] * n_after,
            out_specs=[_HBM_SPEC] * (2 * n),
            input_output_aliases={i: i for i in range(2 * n)},
            compiler_params=pltpu.CompilerParams(has_side_effects=pltpu.SideEffectType.DATAFLOW_SIDE_EFFECTING),
        )(*bufs, *sems, *([after] if n_after else []))
        return sems, outs[:n], outs[n:], None

    def start(self, srcs, lands, after=None):
        n = len(srcs)

        def body(src, land, sems, token):
            for w in range(n):
                self._own(w, src[w], land[w], sems).start()
                for r in DIRECT_PEERS:
                    self._direct(w, r, src[w], land[w], sems, True).start()
            token[...] = jnp.zeros_like(token)

        return self._call("_start", body, None, srcs, lands, after, True)

    def forward(self, state, after, which, tag=""):
        sems, srcs, lands, token = state
        n = len(srcs)

        def body(src, land, sem_refs, _):
            for w in range(n):
                for r in which:
                    self._direct(w, r, src[w], land[w], sem_refs, False).wait_recv()
                    self._passed_on(w, r | 1, land[w], sem_refs, True).start()

        sems, srcs, lands, _ = self._call("_forward" + tag, body, sems, srcs, lands, after, False)
        return sems, srcs, lands, token

    def wait(self, state, after, direct=(), passed_on=(), sends=False, tag=""):
        sems, srcs, lands, token = state
        n = len(srcs)

        def body(src, land, sem_refs, _):
            for w in range(n):
                for r in direct:
                    self._direct(w, r, src[w], land[w], sem_refs, False).wait_recv()
                for f in passed_on:
                    self._passed_on(w, f, land[w], sem_refs, False).wait_recv()
                if sends:
                    self._own(w, src[w], land[w], sem_refs).wait()
                    for r in DIRECT_PEERS:
                        self._direct(w, r, src[w], land[w], sem_refs, True).wait_send()
                    for f in FORWARDED:
                        self._passed_on(w, f, land[w], sem_refs, True).wait_send()

        sems, srcs, lands, _ = self._call("_wait" + tag, body, sems, srcs, lands, after, False)
        return lands, (sems, srcs, lands, token)


def _scatter_exchange(name, axes, shard_sizes):
    def part(w, src, dev):
        return src if axes[w] is None else _shard_slice(src, axes[w], dev, shard_sizes[w])
    return _Exchange(name, part, lambda w, land, dev: land.at[dev])


def _adamw(name, partials, w, m, v):
    R, C = w.shape
    tr = R
    while tr * C * 4 > (1 << 20) and tr % 16 == 0:
        tr //= 2

    def body(p_ref, w_ref, m_ref, v_ref, g_ref, d_ref, nm_ref, nv_ref):
        g = p_ref[0].astype(F32)
        for jdev in range(1, N_DEV):
            g = g + p_ref[jdev].astype(F32)
        nm = ADAM_B1 * m_ref[...] + (1.0 - ADAM_B1) * g
        nv = ADAM_B2 * v_ref[...] + (1.0 - ADAM_B2) * (g * g)
        m_hat = nm / (1.0 - ADAM_B1 ** ADAM_STEP)
        v_hat = nv / (1.0 - ADAM_B2 ** ADAM_STEP)
        g_ref[...] = g
        d_ref[...] = -ADAM_LR * (m_hat / (jnp.sqrt(v_hat) + ADAM_EPS) + ADAM_WD * w_ref[...])
        nm_ref[...] = nm
        nv_ref[...] = nv

    spec = pl.BlockSpec((tr, C), lambda i: (i, 0))
    return pl.pallas_call(
        body, name=name, grid=(R // tr,),
        in_specs=[pl.BlockSpec((N_DEV, tr, C), lambda i: (0, i, 0)), spec, spec, spec],
        out_specs=[spec] * 4,
        out_shape=[jax.ShapeDtypeStruct((R, C), F32)] * 4,
        compiler_params=pltpu.CompilerParams(dimension_semantics=("parallel",),
                                             vmem_limit_bytes=_vmem_limit(24 << 20)),
    )(partials, w, m, v)


def _local_step(x, cos2, sin_fwd, sin_bwd, project_in, mix_weights, ffn_weights, pool_scale, g_mix, b_mix, g_ff, b_ff,
                target, send):
    S, D = x.shape
    aw = pw = D // 2
    u_col_block = 3
    gate_col0 = 4 * aw

    h, w_in, xb = project_in(x)
    dilated = [d for d in DILATIONS if d > 1]
    qkv = {1: h[None], **dict(zip(dilated, _to_residue_major("qkv_to_rm", h, 0, 3 * aw)))}
    fwd = [_attn_fwd(qkv[d], d, aw) for d in DILATIONS]
    o_attn, l_tot = _attn_combine([f[0] for f in fwd], [f[1] for f in fwd], aw)
    w_pool, w_ba, w_bp, w_out = mix_weights(o_attn)
    p, y_pre, pm = _pool_fwd(h, w_pool, pool_scale, pw, u_col_block)
    y_attn = _branch_attn(o_attn, w_ba)
    y_pool, merged = _branch_pool_merge(pm, w_bp, h, y_attn, gate_col0)
    w1, w2 = ffn_weights(merged)
    x1, x1b, xhat1, rstd1 = _out_proj_ln(merged, w_out, x, g_mix, b_mix)
    a = _ffn_up(x1b, w1)
    z2 = _residual_matmul("ffn_down", a, w2, "nn", x1)
    dz2, dz2b, dg_ff, db_ff, loss = _ln_loss_bwd(z2, g_ff, b_ff, target)

    tok = send("ff2", [_grad_weight("grad_w_ff2", a, dz2b)])
    dh1 = _ffn_down_bwd(dz2b, w2, a, after=tok)
    tok = send("ff1", [_grad_weight("grad_w_ff1", x1b, dh1)])
    dy1 = _residual_matmul("ffn_up_bwd", dh1, w1, "nt", dz2, after=tok)
    dz1, dz1b, dg_mix, db_mix = _ln_bwd(dy1, xhat1, rstd1, g_mix)
    tok = send("out", [_grad_weight("grad_w_out", merged, dz1b)])
    dy_attn, dy_pool, dh = _out_proj_bwd(dz1b, w_out, h, y_attn, y_pool, gate_col0, after=tok)
    tok = send("branch", [_grad_weight("grad_w_branch_attn", o_attn, dy_attn),
                          _grad_weight("grad_w_branch_pool", pm, dy_pool)])
    d_out, stats = _branch_attn_bwd(dy_attn, w_ba, o_attn, l_tot, after=tok)
    dy_pre, d_scale = _branch_pool_bwd(dy_pool, w_bp, y_pre, pool_scale)
    dh, dw_pool = _pool_bwd(dh, dy_pre, p, w_pool, pw, u_col_block)
    d_outs = {1: d_out[None], **dict(zip(dilated, _to_residue_major("dout_to_rm", d_out, 0, aw)))}
    statss = {1: stats[None], **dict(zip(dilated, _to_residue_major("stats_to_rm", stats, 0, HEAD_DIM)))}
    per_pattern = [_attn_bwd(qkv[d], d_outs[d], statss[d], d, aw) for d in DILATIONS]
    dh = _attn_bwd_finish(dh, per_pattern, cos2, sin_bwd, aw)
    small = jnp.concatenate((d_scale, dg_mix, db_mix, dg_ff, db_ff), axis=-1)
    tok = send("in", [_grad_weight("grad_w_in", xb, dh), dw_pool.astype(BF16),
                      small.reshape(small.shape[-1] // HEAD_DIM, HEAD_DIM)])
    grad_x = _residual_matmul("in_proj_bwd", dh, w_in, "nt", dz1, after=tok)
    return loss, grad_x


def _rope_tables(positions):
    half = HEAD_DIM // 2
    inv_freq = ROPE_THETA ** (-jnp.arange(half, dtype=F32) / half)
    ang = positions.astype(F32)[:, None] * inv_freq
    cos, sin = jnp.cos(ang), jnp.sin(ang)
    cos2 = jnp.concatenate([cos, cos], axis=-1)
    sin_fwd = jnp.concatenate([-sin, sin], axis=-1)
    return cos2, sin_fwd, -sin_fwd


def kernel(x, positions, w_in, w_pool, pool_scale, w_branch_attn, w_branch_pool, w_out, ln_mix_g, ln_mix_b, w_ff1, w_ff2, ln_ff_g, ln_ff_b, loss_target, m_w_in, m_w_pool, m_pool_scale, m_w_branch_attn, m_w_branch_pool, m_w_out, m_ln_mix_g, m_ln_mix_b, m_w_ff1, m_w_ff2, m_ln_ff_g, m_ln_ff_b, v_w_in, v_w_pool, v_pool_scale, v_w_branch_attn, v_w_branch_pool, v_w_out, v_ln_mix_g, v_ln_mix_b, v_w_ff1, v_w_ff2, v_ln_ff_g, v_ln_ff_b):
    big_w = (w_in[0], w_pool[0], w_branch_attn[0], w_branch_pool[0], w_out[0], w_ff1[0], w_ff2[0])
    big_m = (m_w_in[0], m_w_pool[0], m_w_branch_attn[0], m_w_branch_pool[0], m_w_out[0], m_w_ff1[0], m_w_ff2[0])
    big_v = (v_w_in[0], v_w_pool[0], v_w_branch_attn[0], v_w_branch_pool[0], v_w_out[0], v_w_ff1[0], v_w_ff2[0])
    shard_axes = (1, 1, 1, 1, 0, 1, 0)
    small_w = (pool_scale, ln_mix_g, ln_mix_b, ln_ff_g, ln_ff_b)
    small_m = (m_pool_scale, m_ln_mix_g, m_ln_mix_b, m_ln_ff_g, m_ln_ff_b)
    small_v = (v_pool_scale, v_ln_mix_g, v_ln_mix_b, v_ln_ff_g, v_ln_ff_b)

    names = ("w_in", "w_pool", "w_branch_attn", "w_branch_pool", "w_out", "w_ff1", "w_ff2")
    axis_of = dict(zip(names, shard_axes))
    shard_of = dict(zip(names, [w.astype(BF16) for w in big_w]))

    def full_buffer(n):
        s, ax = shard_of[n], axis_of[n]
        full = list(s.shape)
        full[ax] *= N_DEV
        return lax.empty(tuple(full), s.dtype)

    def gather_group(tag, group, after):
        ex = _TwoLevelGather(tag, [axis_of[n] for n in group], [shard_of[n].shape[axis_of[n]] for n in group])
        return ex, ex.start([shard_of[n] for n in group], [full_buffer(n) for n in group], after)

    in_ex, in_state = gather_group("gather_in", ("w_in",), None)
    mix_ex, mix_state = gather_group("gather_mix", ("w_pool", "w_branch_attn", "w_branch_pool", "w_out"),
                                     in_state[-1])
    ffn_ex, ffn_state = gather_group("gather_ffn", ("w_ff1", "w_ff2"), mix_state[-1])
    states = {"mix": mix_state, "ffn": ffn_state}
    me = 4 * lax.axis_index("x") + 2 * lax.axis_index("y") + lax.axis_index("c")
    block_cols = shard_of["w_in"].shape[1]
    neighbours, diagonal = (2, 4), (6,)

    def project_in(x_f32):
        n_rope_blocks = 2 * (x.shape[-1] // 2) // block_cols

        def first_piece():
            return _in_proj_piece("in_proj_own", x_f32, shard_of["w_in"], jnp.stack([me]).astype(jnp.int32), cos2,
                                  sin_fwd, None, block_cols, n_rope_blocks, own_shard=True, after=ffn_state[-1])

        h, xb = first_piece()

        def piece(tag, w, blocks, h):
            return _in_proj_piece("in_proj_" + tag, xb, w, jnp.stack(blocks).astype(jnp.int32), cos2, sin_fwd, h,
                                  block_cols, n_rope_blocks)[0]

        (w_in_land,), state = in_ex.wait(in_state, h, direct=(1,), tag="_sibling")
        h = piece("sibling", w_in_land, [me ^ 1], h)
        state = in_ex.forward(state, h, neighbours, tag="_neighbours")
        h = piece("neighbours", state[2][0], [me ^ r for r in neighbours], h)
        state = in_ex.forward(state, h, diagonal, tag="_diagonal")
        h = piece("diagonal", state[2][0], [me ^ r for r in diagonal], h)
        states["mix"] = mix_ex.forward(states["mix"], h, neighbours + diagonal)
        (w_in_land,), _ = in_ex.wait(state, h, passed_on=FORWARDED, sends=True, tag="_passed_on")
        h = piece("passed_on", w_in_land, [me ^ f for f in FORWARDED], h)
        return h, w_in_land, xb

    def mix_weights(after):
        states["ffn"] = ffn_ex.forward(states["ffn"], after, neighbours + diagonal)
        return mix_ex.wait(states["mix"], after, direct=(1,), passed_on=FORWARDED, sends=True)[0]

    def ffn_weights(after):
        return ffn_ex.wait(states["ffn"], after, direct=(1,), passed_on=FORWARDED, sends=True)[0]

    groups = {"ff2": ("w_ff2",), "ff1": ("w_ff1",), "out": ("w_out",),
              "branch": ("w_branch_attn", "w_branch_pool"), "in": ("w_in", "w_pool", "small")}
    sent = {}

    def send(key, grads_):
        axes = [axis_of.get(n) for n in groups[key]]
        sizes = [None if ax is None else g.shape[ax] // N_DEV for g, ax in zip(grads_, axes)]
        lands = []
        for g, ax, size in zip(grads_, axes, sizes):
            shard = list(g.shape)
            if ax is not None:
                shard[ax] = size
            lands.append(lax.empty((N_DEV, *shard), g.dtype))
        ex = _scatter_exchange("scatter_" + key, axes, sizes)
        sent[key] = (ex, ex.start(list(grads_), lands))
        return sent[key][1][-1]

    cos2, sin_fwd, sin_bwd = _rope_tables(positions[0])
    loss, grad_x = _local_step(
        x[0], cos2, sin_fwd, sin_bwd, project_in, mix_weights, ffn_weights, pool_scale, ln_mix_g, ln_mix_b,
        ln_ff_g, ln_ff_b, loss_target[0], send)

    state = dict(zip(names, zip(big_w, big_m, big_v)))
    n_small = sum(w.shape[-1] for w in small_w)
    small_2d = (n_small // HEAD_DIM, HEAD_DIM)
    state["small"] = tuple(jnp.concatenate(t, axis=-1).reshape(small_2d) for t in (small_w, small_m, small_v))
    grads, deltas, new_ms, new_vs = {}, {}, {}, {}
    after = grad_x
    for key in ("ff2", "ff1", "out", "branch", "in"):
        ex, started = sent[key]
        for n, part in zip(groups[key], ex.wait(started, after)[0]):
            w, m, v = state[n]
            r2 = (-1, w.shape[-1])
            w2d = w.reshape(r2)
            res = _adamw("adamw_" + n, part.reshape((N_DEV,) + w2d.shape), w2d, m.reshape(r2), v.reshape(r2))
            after = res[0]
            if n == "small":
                small_out = [t.reshape(1, n_small) for t in res]
            else:
                grads[n], deltas[n], new_ms[n], new_vs[n] = (t.reshape((1,) + w.shape) for t in res)
    small_names = ("pool_scale", "ln_mix_g", "ln_mix_b", "ln_ff_g", "ln_ff_b")
    off = 0
    for n, w in zip(small_names, small_w):
        width = w.shape[-1]
        grads[n], deltas[n], new_ms[n], new_vs[n] = (t[:, off:off + width] for t in small_out)
        off += width

    order = ("w_in", "w_pool", "pool_scale", "w_branch_attn", "w_branch_pool", "w_out", "ln_mix_g", "ln_mix_b",
             "w_ff1", "w_ff2", "ln_ff_g", "ln_ff_b")
    total_loss = lax.psum(loss[0, 0], ("x", "y", "c"))
    return (total_loss, grad_x[None], *[grads[n] for n in order], *[deltas[n] for n in order],
            *[new_ms[n] for n in order], *[new_vs[n] for n in order])
```

```python
import functools

import jax
import jax.numpy as jnp
from jax import lax
from jax.experimental import pallas as pl
from jax.experimental.pallas import tpu as pltpu

F32 = jnp.float32
BF16 = jnp.bfloat16

N_DEV = 8
HEAD_DIM = 128
SUB_BLOCK = 128
DILATIONS = (1, 4, 16)
POOL_WINDOWS = (2, 4, 8, 16)
MAX_POOL_WINDOW = 16
POOL_HALO = 128
PERM_ROWS = 256
K_TILE = 1024
LN_EPS = 1e-5
DEEPNORM_ALPHA = 2.0 ** 0.25
ROPE_THETA = 10000.0
ATTN_SCALE = HEAD_DIM ** -0.5
ADAM_LR, ADAM_B1, ADAM_B2, ADAM_EPS, ADAM_WD, ADAM_STEP = 0.001, 0.9, 0.999, 1e-08, 0.01, 10
NEG_BIG = -1e30
VMEM_CAP_V7X = 64 * 1024 * 1024
MESH = pl.DeviceIdType.MESH


def _vmem_limit(est_bytes):
    return int(min(max(est_bytes * 5 // 4 + (4 << 20), 16 << 20), VMEM_CAP_V7X - (6 << 20)))


def _nbytes(shape, dtype):
    n = 1
    for s in shape:
        n *= s
    return n * jnp.dtype(dtype).itemsize


def _mm(name, a, b, form, tiles, outs, epi, extras=(), sequential=False, after=None):
    tm, tn, tk = tiles
    if form == "nn":
        (M, K), (K2, N) = a.shape, b.shape
    elif form == "nt":
        (M, K), (N, K2) = a.shape, b.shape
    else:
        (K, M), (K2, N) = a.shape, b.shape
    assert K == K2, (name, a.shape, b.shape)
    tm, tn, tk = min(tm, M), min(tn, N), min(tk, K)
    assert M % tm == 0 and N % tn == 0 and K % tk == 0, (name, M, N, K, tm, tn, tk)
    grid = (M // tm, N // tn, K // tk)
    nk = grid[2]
    if form == "nn":
        a_spec = pl.BlockSpec((tm, tk), lambda i, j, k: (i, k))
        b_spec = pl.BlockSpec((tk, tn), lambda i, j, k: (k, j))
        contract = ((1,), (0,))
    elif form == "nt":
        a_spec = pl.BlockSpec((tm, tk), lambda i, j, k: (i, k))
        b_spec = pl.BlockSpec((tn, tk), lambda i, j, k: (j, k))
        contract = ((1,), (1,))
    else:
        a_spec = pl.BlockSpec((tk, tm), lambda i, j, k: (k, i))
        b_spec = pl.BlockSpec((tk, tn), lambda i, j, k: (k, j))
        contract = ((0,), (0,))
    n_ex, n_out = len(extras), len(outs)
    n_after = 0 if after is None else 1

    def body(a_ref, b_ref, *rest):
        ex_refs = rest[:n_ex]
        rest = rest[n_ex + n_after:]
        out_refs = rest[:n_out]
        i, j, k = pl.program_id(0), pl.program_id(1), pl.program_id(2)

        def prod():
            return lax.dot_general(a_ref[...].astype(BF16), b_ref[...].astype(BF16),
                                   (contract, ((), ())), preferred_element_type=F32)

        if nk == 1:
            epi(prod(), ex_refs, out_refs, i, j)
        else:
            acc = rest[n_out]

            @pl.when(k == 0)
            def _():
                acc[...] = prod()

            @pl.when(jnp.logical_and(k > 0, k < nk - 1))
            def _():
                acc[...] += prod()

            @pl.when(k == nk - 1)
            def _():
                epi(acc[...] + prod(), ex_refs, out_refs, i, j)

    est = 2 * (_nbytes(a_spec.block_shape, a.dtype) + _nbytes(b_spec.block_shape, b.dtype))
    est += sum(2 * _nbytes(bs, arr.dtype) for arr, bs, _ in extras)
    est += sum(2 * _nbytes(bs, dt) for _, dt, bs, _ in outs)
    est += 4 * tm * tn * 4
    sem = ("arbitrary",) * 3 if sequential else ("parallel", "parallel", "arbitrary")
    return pl.pallas_call(
        body, name=name, grid=grid,
        in_specs=([a_spec, b_spec] + [pl.BlockSpec(bs, im) for _, bs, im in extras]
                  + [pl.BlockSpec(memory_space=pl.ANY)] * n_after),
        out_specs=[pl.BlockSpec(bs, im) for _, _, bs, im in outs],
        out_shape=[jax.ShapeDtypeStruct(sh, dt) for sh, dt, _, _ in outs],
        scratch_shapes=[pltpu.VMEM((tm, tn), F32)] if nk > 1 else [],
        compiler_params=pltpu.CompilerParams(dimension_semantics=sem, vmem_limit_bytes=_vmem_limit(est)),
    )(a, b, *[arr for arr, _, _ in extras], *([after] if n_after else []))


def _tile_out(shape, dtype, tm, tn):
    return (shape, dtype, (tm, tn), lambda i, j, k: (i, j))


def _row_sum_out(width):
    return ((1, width), F32, (1, width), lambda i, j, k: (0, 0))


def _accumulate_rows(ref, value, i):
    @pl.when(i == 0)
    def _():
        ref[...] = value

    @pl.when(i > 0)
    def _():
        ref[...] += value


def _layer_norm_bwd(dy, xhat, rstd, g):
    dxh = dy * g
    m1 = jnp.mean(dxh, axis=-1, keepdims=True)
    m2 = jnp.mean(dxh * xhat, axis=-1, keepdims=True)
    return rstd * (dxh - m1 - xhat * m2)


def _rope_apply(t, cos2, sin_signed):
    return t * cos2 + pltpu.roll(t, HEAD_DIM // 2, axis=1) * sin_signed


def _in_proj_piece(name, xb, w_in, col_blocks, cos2, sin_fwd, h_so_far, block_cols, n_rope_blocks, own_shard=False,
                   after=None):
    S, D = xb.shape
    W = w_in.shape[1] * (N_DEV if own_shard else 1)
    tm = min(1024, S)
    n_blocks = col_blocks.shape[0]

    def body(cols_ref, x_ref, w_ref, cos_ref, sin_ref, *rest):
        h_ref = rest[-1]
        j = pl.program_id(1)

        @pl.when(cols_ref[j] < n_rope_blocks)
        def _():
            acc = _dot_nn(x_ref[...], w_ref[...])
            c, s = cos_ref[...], sin_ref[...]
            for hd in range(block_cols // HEAD_DIM):
                sl = slice(hd * HEAD_DIM, (hd + 1) * HEAD_DIM)
                h_ref[:, sl] = _rope_apply(acc[:, sl], c, s).astype(BF16)

        @pl.when(cols_ref[j] >= n_rope_blocks)
        def _():
            h_ref[...] = _dot_nn(x_ref[...], w_ref[...]).astype(BF16)

    row = pl.BlockSpec((tm, HEAD_DIM), lambda i, j, cols: (i, 0))
    carried = ([] if h_so_far is None else [h_so_far]) + ([] if after is None else [after])
    est = 2 * (tm * D * 2 + D * block_cols * 2 + tm * block_cols * 2 + 2 * tm * HEAD_DIM * 4) + 3 * tm * block_cols * 4
    return pl.pallas_call(
        body, name=name,
        grid_spec=pltpu.PrefetchScalarGridSpec(
            num_scalar_prefetch=1, grid=(S // tm, n_blocks),
            in_specs=[pl.BlockSpec((tm, D), lambda i, j, cols: (i, 0)),
                      pl.BlockSpec((D, block_cols), lambda i, j, cols: (0, 0 if own_shard else cols[j])), row, row]
                     + [pl.BlockSpec(memory_space=pl.ANY)] * len(carried),
            out_specs=pl.BlockSpec((tm, block_cols), lambda i, j, cols: (i, cols[j]))),
        out_shape=jax.ShapeDtypeStruct((S, W), BF16),
        input_output_aliases={} if h_so_far is None else {5: 0},
        compiler_params=pltpu.CompilerParams(dimension_semantics=("parallel", "arbitrary"),
                                             vmem_limit_bytes=_vmem_limit(est)),
    )(col_blocks, xb, w_in, cos2, sin_fwd, *carried)


def _attn_mask(mb):
    qi = lax.broadcasted_iota(jnp.int32, (SUB_BLOCK, 2 * SUB_BLOCK), 0)
    kj = lax.broadcasted_iota(jnp.int32, (SUB_BLOCK, 2 * SUB_BLOCK), 1)
    prev = jnp.logical_and(jnp.logical_and(kj < SUB_BLOCK, kj >= qi), mb > 0)
    cur = jnp.logical_and(kj >= SUB_BLOCK, kj - SUB_BLOCK <= qi)
    return jnp.logical_or(prev, cur)


def _both_blocks(prev_ref, cur_ref, sl):
    return jnp.concatenate([prev_ref[:, sl], cur_ref[:, sl]], axis=0)


def _dot_nt(a, b):
    return lax.dot_general(a, b, (((1,), (1,)), ((), ())), preferred_element_type=F32)


def _dot_tn(a, b):
    return lax.dot_general(a, b, (((0,), (0,)), ((), ())), preferred_element_type=F32)


def _dot_nn(a, b):
    return lax.dot_general(a, b, (((1,), (0,)), ((), ())), preferred_element_type=F32)


def _perm_matrix(d, to_residue_major):
    g = PERM_ROWS // d
    i = lax.broadcasted_iota(jnp.int32, (PERM_ROWS, PERM_ROWS), 0)
    j = lax.broadcasted_iota(jnp.int32, (PERM_ROWS, PERM_ROWS), 1)
    if to_residue_major:
        hit = j == (i % g) * d + i // g
    else:
        hit = j == (i % d) * g + i // d
    return hit.astype(BF16)


def _permute_rows(perm, x, terms=3):
    if x.dtype == BF16:
        return _dot_nn(perm, x)
    hi = x.astype(BF16)
    r1 = x - hi.astype(F32)
    mid = r1.astype(BF16)
    out = _dot_nn(perm, hi) + _dot_nn(perm, mid)
    if terms == 3:
        out = out + _dot_nn(perm, (r1 - mid.astype(F32)).astype(BF16))
    return out


def _rm_block(d, width):
    return pl.BlockSpec((d, PERM_ROWS // d, width), lambda i: (0, i, 0))


def _to_residue_major(name, x, col_block, width):
    S = x.shape[0]
    dils = [d for d in DILATIONS if d > 1]
    chunk = min(width, 1024)

    def body(x_ref, *out_refs):
        for d, o_ref in zip(dils, out_refs):
            perm = _perm_matrix(d, True)
            for c0 in range(0, width, chunk):
                cw = min(chunk, width - c0)
                y = _permute_rows(perm, x_ref[:, c0:c0 + cw])
                o_ref[:, :, c0:c0 + cw] = y.astype(x.dtype).reshape(d, PERM_ROWS // d, cw)

    return pl.pallas_call(
        body, name=name, grid=(S // PERM_ROWS,),
        in_specs=[pl.BlockSpec((PERM_ROWS, width), lambda i: (i, col_block))],
        out_specs=[_rm_block(d, width) for d in dils],
        out_shape=[jax.ShapeDtypeStruct((d, S // d, width), x.dtype) for d in dils],
        compiler_params=pltpu.CompilerParams(dimension_semantics=("parallel",),
                                             vmem_limit_bytes=_vmem_limit(32 << 20)),
    )(x)


def _put_column(tile, col, value):
    lane = lax.broadcasted_iota(jnp.int32, tile.shape, 1)
    return jnp.where(lane == col, value, tile)


def _attn_fwd(qkv, d, aw):
    _, rows, _ = qkv.shape
    n_heads = aw // HEAD_DIM
    nb = rows // SUB_BLOCK

    def body(q_ref, kc_ref, vc_ref, o_ref, lse_ref, kp_ref, vp_ref, s_buf, p_buf):
        step = pl.program_id(0)

        @pl.when(step == 0)
        def _():
            kp_ref[...] = jnp.zeros_like(kp_ref)
            vp_ref[...] = jnp.zeros_like(vp_ref)

        mask = _attn_mask(step % nb)
        for hd in range(n_heads):
            sl = slice(hd * HEAD_DIM, (hd + 1) * HEAD_DIM)
            s_buf[hd] = _dot_nt(q_ref[:, sl], _both_blocks(kp_ref, kc_ref, sl))
        lse_tile = jnp.zeros((SUB_BLOCK, HEAD_DIM), F32)
        inv_tile = jnp.zeros((SUB_BLOCK, HEAD_DIM), F32)
        for hd in range(n_heads):
            s = jnp.where(mask, s_buf[hd] * ATTN_SCALE, NEG_BIG)
            m = jnp.max(s, axis=-1, keepdims=True)
            p = jnp.exp(s - m)
            l = jnp.sum(p, axis=-1, keepdims=True)
            p_buf[hd] = p.astype(BF16)
            lse_tile = _put_column(lse_tile, hd, m + jnp.log(l))
            inv_tile = _put_column(inv_tile, hd, 1.0 / l)
        lse_ref[...] = lse_tile
        for hd in range(n_heads):
            sl = slice(hd * HEAD_DIM, (hd + 1) * HEAD_DIM)
            o = _dot_nn(p_buf[hd], _both_blocks(vp_ref, vc_ref, sl))
            o_ref[:, sl] = o * inv_tile[:, hd:hd + 1]
        kp_ref[...] = kc_ref[...]
        vp_ref[...] = vc_ref[...]

    def block(col, width):
        return pl.BlockSpec((SUB_BLOCK, width), lambda s: (s, col))

    qkv2 = qkv.reshape(d * rows, qkv.shape[-1])
    o, lse = pl.pallas_call(
        body, name=f"attn_fwd_d{d}", grid=(d * nb,),
        in_specs=[block(0, aw), block(1, aw), block(2, aw)],
        out_specs=[block(0, aw), block(0, HEAD_DIM)],
        out_shape=[jax.ShapeDtypeStruct((d * rows, aw), F32), jax.ShapeDtypeStruct((d * rows, HEAD_DIM), F32)],
        scratch_shapes=[pltpu.VMEM((SUB_BLOCK, aw), BF16), pltpu.VMEM((SUB_BLOCK, aw), BF16),
                        pltpu.VMEM((n_heads, SUB_BLOCK, 2 * SUB_BLOCK), F32),
                        pltpu.VMEM((n_heads, SUB_BLOCK, 2 * SUB_BLOCK), BF16)],
        compiler_params=pltpu.CompilerParams(dimension_semantics=("arbitrary",),
                                             vmem_limit_bytes=_vmem_limit(16 << 20)),
    )(qkv2, qkv2, qkv2)
    return o.reshape(d, rows, aw), lse.reshape(d, rows, HEAD_DIM)


def _attn_combine(outs, lses, aw):
    S = outs[0].shape[1]
    n_heads = aw // HEAD_DIM
    n_pat = len(DILATIONS)

    def body(*refs):
        o_refs, l_refs = refs[:n_pat], refs[n_pat:2 * n_pat]
        o_ref, lt_ref = refs[2 * n_pat], refs[2 * n_pat + 1]
        o_nat, l_nat = [], []
        for d, o_r, l_r in zip(DILATIONS, o_refs, l_refs):
            o_p = o_r[...].reshape(PERM_ROWS, aw)
            l_p = l_r[...].reshape(PERM_ROWS, HEAD_DIM)
            if d > 1:
                perm = _perm_matrix(d, False)
                o_p, l_p = _permute_rows(perm, o_p, terms=2), _permute_rows(perm, l_p)
            o_nat.append(o_p)
            l_nat.append(l_p)
        mx = functools.reduce(jnp.maximum, l_nat)
        es = [jnp.exp(l_p - mx) for l_p in l_nat]
        den = functools.reduce(jnp.add, es)
        lt_ref[...] = mx + jnp.log(den)
        ws = [e / den for e in es]
        for hd in range(n_heads):
            sl = slice(hd * HEAD_DIM, (hd + 1) * HEAD_DIM)
            o = ws[0][:, hd:hd + 1] * o_nat[0][:, sl]
            for pi in range(1, n_pat):
                o = o + ws[pi][:, hd:hd + 1] * o_nat[pi][:, sl]
            o_ref[:, sl] = o.astype(BF16)

    return pl.pallas_call(
        body, name="attn_combine", grid=(S // PERM_ROWS,),
        in_specs=[_rm_block(d, aw) for d in DILATIONS] + [_rm_block(d, HEAD_DIM) for d in DILATIONS],
        out_specs=[pl.BlockSpec((PERM_ROWS, aw), lambda i: (i, 0)), pl.BlockSpec((PERM_ROWS, HEAD_DIM), lambda i: (i, 0))],
        out_shape=[jax.ShapeDtypeStruct((S, aw), BF16), jax.ShapeDtypeStruct((S, HEAD_DIM), F32)],
        compiler_params=pltpu.CompilerParams(dimension_semantics=("parallel",),
                                             vmem_limit_bytes=_vmem_limit(40 << 20)),
    )(*outs, *lses)


def _band(tm, width, w, row_offset, transpose):
    t = lax.broadcasted_iota(jnp.int32, (tm, width), 0)
    u = lax.broadcasted_iota(jnp.int32, (tm, width), 1)
    dist = (u - t - row_offset) if transpose else (t + row_offset - u)
    return jnp.logical_and(dist >= 0, dist < w).astype(BF16)


def _pool_fwd(h, w_pool, pool_scale, pw, u_col_block):
    S, W = h.shape
    n_groups = len(POOL_WINDOWS)
    gw = pw // n_groups
    tm = min(512, S)
    halo_per_tile = tm // POOL_HALO

    def body(uc_ref, uh_ref, w_ref, sc_ref, p_ref, y_ref, pm_ref):
        i = pl.program_id(0)
        t_abs = i * tm + lax.broadcasted_iota(jnp.int32, (tm, 1), 0)
        for g, w in enumerate(POOL_WINDOWS):
            sl = slice(g * gw, (g + 1) * gw)
            uc = uc_ref[:, sl]
            uh = jnp.where(i > 0, uh_ref[:, sl], jnp.zeros((POOL_HALO, gw), BF16))
            ssum = _dot_nn(_band(tm, tm, w, 0, False), uc) + _dot_nn(_band(tm, POOL_HALO, w, POOL_HALO, False), uh)
            cnt = jnp.minimum(t_abs + 1, w).astype(F32)
            p = (ssum / cnt - uc.astype(F32)).astype(BF16)
            y = _dot_nn(p, w_ref[g])
            p_ref[:, sl] = p
            y_ref[:, sl] = y.astype(BF16)
            pm_ref[:, sl] = (y * sc_ref[:, sl]).astype(BF16)

    row = pl.BlockSpec((tm, pw), lambda i: (i, 0))
    return pl.pallas_call(
        body, name="pool_fwd", grid=(S // tm,),
        in_specs=[pl.BlockSpec((tm, pw), lambda i: (i, u_col_block)),
                  pl.BlockSpec((POOL_HALO, pw), lambda i: (jnp.maximum(i * halo_per_tile - 1, 0), u_col_block)),
                  pl.BlockSpec((n_groups, gw, gw), lambda i: (0, 0, 0)),
                  pl.BlockSpec((1, pw), lambda i: (0, 0))],
        out_specs=[row, row, row],
        out_shape=[jax.ShapeDtypeStruct((S, pw), BF16)] * 3,
        compiler_params=pltpu.CompilerParams(dimension_semantics=("parallel",),
                                             vmem_limit_bytes=_vmem_limit(24 << 20)),
    )(h, h, w_pool, pool_scale)


def _branch_attn(o_attn, w_ba):
    S, _ = o_attn.shape
    D = w_ba.shape[1]
    tm, tn = min(1024, S), D

    def epi(acc, ex, out, i, j):
        out[0][...] = acc.astype(BF16)

    (y,) = _mm("branch_attn", o_attn, w_ba, "nn", (tm, tn, 1024), [_tile_out((S, D), BF16, tm, tn)], epi)
    return y


def _branch_pool_merge(pm, w_bp, h, y_attn, gate_col0):
    S, _ = pm.shape
    D = w_bp.shape[1]
    tm, tn = min(512, S), D
    ga0, gp0 = gate_col0 // tn, (gate_col0 + D) // tn

    def epi(acc, ex, out, i, j):
        ga_ref, gp_ref, ya_ref = ex
        yp_ref, mg_ref = out
        yp = acc.astype(BF16)
        yp_ref[...] = yp
        mg = (jax.nn.sigmoid(ga_ref[...]).astype(F32) * ya_ref[...].astype(F32)
              + jax.nn.sigmoid(gp_ref[...]).astype(F32) * acc)
        mg_ref[...] = mg.astype(BF16)

    y_pool, merged = _mm(
        "branch_pool_merge", pm, w_bp, "nn", (tm, tn, 1024),
        [_tile_out((S, D), BF16, tm, tn), _tile_out((S, D), BF16, tm, tn)], epi,
        extras=[(h, (tm, tn), lambda i, j, k: (i, ga0 + j)), (h, (tm, tn), lambda i, j, k: (i, gp0 + j)),
                (y_attn, (tm, tn), lambda i, j, k: (i, j))])
    return y_pool, merged


def _layer_norm_rows(z, g, b):
    mu = jnp.mean(z, axis=-1, keepdims=True)
    zc = z - mu
    var = jnp.mean(zc * zc, axis=-1, keepdims=True)
    rstd = lax.rsqrt(var + LN_EPS)
    xhat = zc * rstd
    return xhat * g + b, xhat, rstd


def _out_proj_ln(merged, w_out, x, g, b):
    S, D = x.shape
    tm = min(256, S)

    def epi(acc, ex, out, i, j):
        x_ref, g_ref, b_ref = ex
        x1_ref, x1b_ref, xh_ref, rs_ref = out
        y, xhat, rstd = _layer_norm_rows(DEEPNORM_ALPHA * x_ref[...] + acc, g_ref[...], b_ref[...])
        x1_ref[...] = y
        x1b_ref[...] = y.astype(BF16)
        xh_ref[...] = xhat
        rs_ref[...] = jnp.broadcast_to(rstd, (tm, HEAD_DIM))

    row = lambda i, j, k: (i, 0)
    vec = lambda i, j, k: (0, 0)
    return _mm("out_proj_ln", merged, w_out, "nn", (tm, D, D),
               [((S, D), F32, (tm, D), row), ((S, D), BF16, (tm, D), row), ((S, D), F32, (tm, D), row),
                ((S, HEAD_DIM), F32, (tm, HEAD_DIM), row)], epi,
               extras=[(x, (tm, D), row), (g, (1, D), vec), (b, (1, D), vec)])


def _ffn_up(x1b, w1):
    S, D = x1b.shape
    F = w1.shape[1]
    tm, tn = min(1024, S), min(2048, F)

    def epi(acc, ex, out, i, j):
        r = jnp.maximum(acc, 0.0)
        out[0][...] = (r * r).astype(BF16)

    (a,) = _mm("ffn_up", x1b, w1, "nn", (tm, tn, 2 * K_TILE), [_tile_out((S, F), BF16, tm, tn)], epi)
    return a


def _residual_matmul(name, a, w, form, resid, after=None):
    S, D = resid.shape
    tm, tn = min(1024, S), min(2048, D)

    def epi(acc, ex, out, i, j):
        out[0][...] = DEEPNORM_ALPHA * ex[0][...] + acc

    (z,) = _mm(name, a, w, form, (tm, tn, K_TILE), [_tile_out((S, D), F32, tm, tn)], epi,
               extras=[(resid, (tm, tn), lambda i, j, k: (i, j))], after=after)
    return z


def _row_kernel(name, body, row_inputs, vec_inputs, row_outputs, sum_widths, tr):
    S = row_inputs[0].shape[0]
    row = lambda w: pl.BlockSpec((tr, w), lambda i: (i, 0))
    vec = lambda w: pl.BlockSpec((1, w), lambda i: (0, 0))

    def wrapped(*refs):
        body(pl.program_id(0), *refs)

    return pl.pallas_call(
        wrapped, name=name, grid=(S // tr,),
        in_specs=[row(t.shape[1]) for t in row_inputs] + [vec(t.shape[1]) for t in vec_inputs],
        out_specs=[row(w) for w, _ in row_outputs] + [vec(w) for w in sum_widths],
        out_shape=([jax.ShapeDtypeStruct((S, w), dt) for w, dt in row_outputs]
                   + [jax.ShapeDtypeStruct((1, w), F32) for w in sum_widths]),
        compiler_params=pltpu.CompilerParams(dimension_semantics=("arbitrary",),
                                             vmem_limit_bytes=_vmem_limit(40 << 20)),
    )(*row_inputs, *vec_inputs)


def _ln_loss_bwd(z2, g, b, target):
    S, D = z2.shape

    def body(i, z_ref, t_ref, g_ref, b_ref, dz_ref, dzb_ref, dg_ref, db_ref, loss_ref):
        gv = g_ref[...]
        y, xhat, rstd = _layer_norm_rows(z_ref[...], gv, b_ref[...])
        err = y - t_ref[...]
        loss = 0.5 * jnp.sum(jnp.mean(err * err, axis=-1, keepdims=True), axis=0, keepdims=True)
        dy = err * (1.0 / D)
        dz = _layer_norm_bwd(dy, xhat, rstd, gv)
        dz_ref[...] = dz
        dzb_ref[...] = dz.astype(BF16)
        _accumulate_rows(dg_ref, jnp.sum(dy * xhat, axis=0, keepdims=True), i)
        _accumulate_rows(db_ref, jnp.sum(dy, axis=0, keepdims=True), i)
        _accumulate_rows(loss_ref, jnp.broadcast_to(loss, (1, HEAD_DIM)), i)

    return _row_kernel("ln_loss_bwd", body, [z2, target], [g, b], [(D, F32), (D, BF16)], [D, D, HEAD_DIM],
                       min(256, S))


def _ln_bwd(dy, xhat, rstd, g):
    S, D = dy.shape

    def body(i, dy_ref, xh_ref, rs_ref, g_ref, dz_ref, dzb_ref, dg_ref, db_ref):
        dyv, xhat_v = dy_ref[...], xh_ref[...]
        dz = _layer_norm_bwd(dyv, xhat_v, rs_ref[:, :1], g_ref[...])
        dz_ref[...] = dz
        dzb_ref[...] = dz.astype(BF16)
        _accumulate_rows(dg_ref, jnp.sum(dyv * xhat_v, axis=0, keepdims=True), i)
        _accumulate_rows(db_ref, jnp.sum(dyv, axis=0, keepdims=True), i)

    return _row_kernel("ln_bwd", body, [dy, xhat, rstd], [g], [(D, F32), (D, BF16)], [D, D], min(256, S))


def _grad_weight(name, act, cot):
    M, N = act.shape[1], cot.shape[1]
    tm, tn = min(1024, M), min(2048, N)

    def epi(acc, ex, out, i, j):
        out[0][...] = acc.astype(BF16)

    (g,) = _mm(name, act, cot, "tn", (tm, tn, 2 * K_TILE), [_tile_out((M, N), BF16, tm, tn)], epi)
    return g


def _ffn_down_bwd(dz2b, w2, a, after=None):
    S, D = dz2b.shape
    F = w2.shape[0]
    tm, tn = min(1024, S), min(2048, F)

    def epi(acc, ex, out, i, j):
        out[0][...] = (acc * (2.0 * jnp.sqrt(ex[0][...])).astype(F32)).astype(BF16)

    (dh1,) = _mm("ffn_down_bwd", dz2b, w2, "nt", (tm, tn, 2 * K_TILE), [_tile_out((S, F), BF16, tm, tn)], epi,
                 extras=[(a, (tm, tn), lambda i, j, k: (i, j))], after=after)
    return dh1


def _out_proj_bwd(dz1b, w_out, h, y_attn, y_pool, gate_col0, after=None):
    S, D = dz1b.shape
    W = h.shape[1]
    tm = min(256, S)
    assert gate_col0 == 2 * D and W == 4 * D

    def epi(acc, ex, out, i, j):
        gates_ref, ya_ref, yp_ref = ex
        dya_ref, dyp_ref, dh_ref = out
        sa = jax.nn.sigmoid(gates_ref[:, :D])
        sp = jax.nn.sigmoid(gates_ref[:, D:])
        dya_ref[...] = (acc * sa.astype(F32)).astype(BF16)
        dyp_ref[...] = (acc * sp.astype(F32)).astype(BF16)
        dh_ref[:, :D] = (acc * (ya_ref[...] * (sa * (1.0 - sa))).astype(F32)).astype(BF16)
        dh_ref[:, D:] = (acc * (yp_ref[...] * (sp * (1.0 - sp))).astype(F32)).astype(BF16)

    row = lambda i, j, k: (i, 0)
    return _mm("out_proj_bwd", dz1b, w_out, "nt", (tm, D, D),
               [((S, D), BF16, (tm, D), row), ((S, D), BF16, (tm, D), row),
                ((S, W), BF16, (tm, 2 * D), lambda i, j, k: (i, 1))], epi,
               extras=[(h, (tm, 2 * D), lambda i, j, k: (i, 1)), (y_attn, (tm, D), row), (y_pool, (tm, D), row)],
               after=after)


def _branch_attn_bwd(dy_attn, w_ba, o_attn, l_tot, after=None):
    S, D = dy_attn.shape
    aw = w_ba.shape[0]
    n_heads = aw // HEAD_DIM
    tm = min(512, S)
    dils = [d for d in DILATIONS if d > 1]

    def epi(acc, ex, out, i, j):
        do_ref, st_ref = out[0], out[1]
        do = acc.astype(BF16)
        do_ref[...] = do
        o = ex[0][...].astype(F32)
        stats = ex[1][...]
        for hd in range(n_heads):
            sl = slice(hd * HEAD_DIM, (hd + 1) * HEAD_DIM)
            stats = _put_column(stats, n_heads + hd, jnp.sum(acc[:, sl] * o[:, sl], axis=-1, keepdims=True))
        st_ref[...] = stats
        for di, d in enumerate(dils):
            perm = _perm_matrix(d, True)
            g = PERM_ROWS // d
            for part in range(tm // PERM_ROWS):
                rows = slice(part * PERM_ROWS, (part + 1) * PERM_ROWS)
                to = slice(part * g, (part + 1) * g)
                out[2 + di][:, to, :] = _permute_rows(perm, do[rows]).astype(BF16).reshape(d, g, aw)
                out[2 + len(dils) + di][:, to, :] = _permute_rows(perm, stats[rows]).reshape(d, g, HEAD_DIM)

    row = lambda i, j, k: (i, 0)
    rm = lambda i, j, k: (0, i, 0)
    res = _mm("branch_attn_bwd", dy_attn, w_ba, "nt", (tm, aw, D),
              [((S, aw), BF16, (tm, aw), row), ((S, HEAD_DIM), F32, (tm, HEAD_DIM), row)]
              + [((d, S // d, aw), BF16, (d, tm // d, aw), rm) for d in dils]
              + [((d, S // d, HEAD_DIM), F32, (d, tm // d, HEAD_DIM), rm) for d in dils], epi,
              extras=[(o_attn, (tm, aw), row), (l_tot, (tm, HEAD_DIM), row)], after=after)
    n = len(dils)
    return ({1: res[0][None], **dict(zip(dils, res[2:2 + n]))}, {1: res[1][None], **dict(zip(dils, res[2 + n:]))})


def _branch_pool_bwd(dy_pool, w_bp, y_pre, pool_scale):
    S, D = dy_pool.shape
    pw = w_bp.shape[0]
    tm = min(512, S)

    def epi(acc, ex, out, i, j):
        y_ref, sc_ref = ex
        dyp_ref, dsc_ref = out
        dyp_ref[...] = (acc * sc_ref[...]).astype(BF16)
        _accumulate_rows(dsc_ref, jnp.sum(acc * y_ref[...].astype(F32), axis=0, keepdims=True), i)

    row = lambda i, j, k: (i, 0)
    return _mm("branch_pool_bwd", dy_pool, w_bp, "nt", (tm, pw, D),
               [((S, pw), BF16, (tm, pw), row), _row_sum_out(pw)], epi,
               extras=[(y_pre, (tm, pw), row), (pool_scale, (1, pw), lambda i, j, k: (0, 0))],
               sequential=True)


def _pool_bwd(dh, dy_pre, p, w_pool, pw, u_col_block):
    S, W = dh.shape
    n_groups = len(POOL_WINDOWS)
    gw = pw // n_groups
    tm = min(512, S)
    n_tiles = S // tm
    halo_per_tile = tm // POOL_HALO
    n_halo_blocks = S // POOL_HALO

    def body(dh_in_ref, dyc_ref, dyh_ref, p_ref, w_ref, dh_ref, dw_ref):
        del dh_in_ref
        i = pl.program_id(0)
        t_cur = i * tm + lax.broadcasted_iota(jnp.int32, (tm, 1), 0)
        t_halo = (i + 1) * tm + lax.broadcasted_iota(jnp.int32, (POOL_HALO, 1), 0)
        for g, w in enumerate(POOL_WINDOWS):
            sl = slice(g * gw, (g + 1) * gw)
            wg = w_ref[g]
            dyc = dyc_ref[:, sl]
            dyh = jnp.where(i < n_tiles - 1, dyh_ref[:, sl], jnp.zeros((POOL_HALO, gw), BF16))
            dp_cur = _dot_nt(dyc, wg)
            dp_halo = _dot_nt(dyh, wg)
            dpc_cur = (dp_cur / jnp.minimum(t_cur + 1, w).astype(F32)).astype(BF16)
            dpc_halo = (dp_halo / jnp.minimum(t_halo + 1, w).astype(F32)).astype(BF16)
            du = (_dot_nn(_band(tm, tm, w, 0, True), dpc_cur)
                  + _dot_nn(_band(tm, POOL_HALO, w, -tm, True), dpc_halo) - dp_cur)
            dh_ref[:, sl] = du.astype(BF16)
            dw = _dot_tn(p_ref[:, sl], dyc)

            @pl.when(i == 0)
            def _():
                dw_ref[g] = dw

            @pl.when(i > 0)
            def _():
                dw_ref[g] += dw

    row = pl.BlockSpec((tm, pw), lambda i: (i, 0))
    dh_new, dw_pool = pl.pallas_call(
        body, name="pool_bwd", grid=(n_tiles,),
        in_specs=[pl.BlockSpec(memory_space=pl.ANY), row,
                  pl.BlockSpec((POOL_HALO, pw), lambda i: (jnp.minimum((i + 1) * halo_per_tile, n_halo_blocks - 1), 0)),
                  row, pl.BlockSpec((n_groups, gw, gw), lambda i: (0, 0, 0))],
        out_specs=[pl.BlockSpec((tm, pw), lambda i: (i, u_col_block)),
                   pl.BlockSpec((n_groups, gw, gw), lambda i: (0, 0, 0))],
        out_shape=[jax.ShapeDtypeStruct((S, W), BF16), jax.ShapeDtypeStruct((n_groups, gw, gw), F32)],
        input_output_aliases={0: 0},
        compiler_params=pltpu.CompilerParams(dimension_semantics=("arbitrary",),
                                             vmem_limit_bytes=_vmem_limit(24 << 20)),
    )(dh, dy_pre, dy_pre, p, w_pool)
    return dh_new, dw_pool


def _attn_bwd(qkv, d_out, stats, d, aw):
    _, rows, _ = qkv.shape
    n_heads = aw // HEAD_DIM
    nb = rows // SUB_BLOCK
    n_blocks = d * nb

    def body(q_ref, kp_ref, kc_ref, vp_ref, vc_ref, do_ref, st_ref, dq_ref, dk_ref, dv_ref,
             carry_k, carry_v, s_buf, dp_buf, p_buf, ds_buf):
        step = pl.program_id(0)

        @pl.when(step == 0)
        def _():
            carry_k[...] = jnp.zeros_like(carry_k)
            carry_v[...] = jnp.zeros_like(carry_v)

        @pl.when(step < n_blocks)
        def _():
            mask = _attn_mask(step % nb)
            st = st_ref[...]
            for hd in range(n_heads):
                sl = slice(hd * HEAD_DIM, (hd + 1) * HEAD_DIM)
                s_buf[hd] = _dot_nt(q_ref[:, sl], _both_blocks(kp_ref, kc_ref, sl))
                dp_buf[hd] = _dot_nt(do_ref[:, sl], _both_blocks(vp_ref, vc_ref, sl))
            for hd in range(n_heads):
                lt, dl = st[:, hd:hd + 1], st[:, n_heads + hd:n_heads + hd + 1]
                p = jnp.where(mask, jnp.exp(jnp.where(mask, s_buf[hd] * ATTN_SCALE - lt, NEG_BIG)), 0.0)
                p_buf[hd] = p.astype(BF16)
                ds_buf[hd] = (p * (dp_buf[hd] - dl) * ATTN_SCALE).astype(BF16)
            for hd in range(n_heads):
                sl = slice(hd * HEAD_DIM, (hd + 1) * HEAD_DIM)
                dq_ref[:, sl] = _dot_nn(ds_buf[hd], _both_blocks(kp_ref, kc_ref, sl)).astype(BF16)
                dk_both = _dot_tn(ds_buf[hd], q_ref[:, sl])
                dv_both = _dot_tn(p_buf[hd], do_ref[:, sl])
                dk_ref[:, sl] = (carry_k[:, sl] + dk_both[:SUB_BLOCK]).astype(BF16)
                dv_ref[:, sl] = (carry_v[:, sl] + dv_both[:SUB_BLOCK]).astype(BF16)
                carry_k[:, sl] = dk_both[SUB_BLOCK:]
                carry_v[:, sl] = dv_both[SUB_BLOCK:]

        @pl.when(step == n_blocks)
        def _():
            dk_ref[...] = carry_k[...].astype(BF16)
            dv_ref[...] = carry_v[...].astype(BF16)

    def cur(step):
        return jnp.minimum(step, n_blocks - 1)

    def qkv_spec(col, prev):
        if prev:
            return pl.BlockSpec((SUB_BLOCK, aw), lambda s: (jnp.maximum(cur(s) - 1, 0), col))
        return pl.BlockSpec((SUB_BLOCK, aw), lambda s: (cur(s), col))

    def at_cur(w):
        return pl.BlockSpec((SUB_BLOCK, w), lambda s: (cur(s), 0))

    finished = pl.BlockSpec((SUB_BLOCK, aw), lambda s: (jnp.maximum(s - 1, 0), 0))
    pair = (n_heads, SUB_BLOCK, 2 * SUB_BLOCK)
    flat = lambda t: t.reshape(d * rows, t.shape[-1])
    qkv2 = flat(qkv)
    outs = pl.pallas_call(
        body, name=f"attn_bwd_d{d}", grid=(n_blocks + 1,),
        in_specs=[qkv_spec(0, False), qkv_spec(1, True), qkv_spec(1, False), qkv_spec(2, True), qkv_spec(2, False),
                  at_cur(aw), at_cur(HEAD_DIM)],
        out_specs=[at_cur(aw), finished, finished],
        out_shape=[jax.ShapeDtypeStruct((d * rows, aw), BF16)] * 3,
        scratch_shapes=[pltpu.VMEM((SUB_BLOCK, aw), F32), pltpu.VMEM((SUB_BLOCK, aw), F32),
                        pltpu.VMEM(pair, F32), pltpu.VMEM(pair, F32), pltpu.VMEM(pair, BF16), pltpu.VMEM(pair, BF16)],
        compiler_params=pltpu.CompilerParams(dimension_semantics=("arbitrary",),
                                             vmem_limit_bytes=_vmem_limit(24 << 20)),
    )(qkv2, qkv2, qkv2, qkv2, qkv2, flat(d_out), flat(stats))
    return [t.reshape(d, rows, aw) for t in outs]


def _attn_bwd_finish(dh, per_pattern, cos2, sin_bwd, aw):
    S, W = dh.shape
    n_heads = aw // HEAD_DIM
    n_pat = len(DILATIONS)

    def body(*refs):
        grad_refs = refs[1:1 + 3 * n_pat]
        cos_ref, sin_ref = refs[1 + 3 * n_pat], refs[2 + 3 * n_pat]
        out_ref = refs[3 + 3 * n_pat]
        perms = {d: _perm_matrix(d, False) for d in DILATIONS if d > 1}
        totals = []
        for which in range(3):
            tot = None
            for pi, d in enumerate(DILATIONS):
                g = grad_refs[which * n_pat + pi][...].reshape(PERM_ROWS, aw)
                g = _permute_rows(perms[d], g) if d > 1 else g.astype(F32)
                tot = g if tot is None else tot + g
            totals.append(tot)
        dq, dk, dv = totals
        c, s = cos_ref[...], sin_ref[...]
        for hd in range(n_heads):
            sl = slice(hd * HEAD_DIM, (hd + 1) * HEAD_DIM)
            out_ref[:, sl] = _rope_apply(dq[:, sl], c, s).astype(BF16)
            out_ref[:, aw + hd * HEAD_DIM:aw + (hd + 1) * HEAD_DIM] = _rope_apply(dk[:, sl], c, s).astype(BF16)
        out_ref[:, 2 * aw:] = dv.astype(BF16)

    grads = [pp[which] for which in range(3) for pp in per_pattern]
    rope_spec = pl.BlockSpec((PERM_ROWS, HEAD_DIM), lambda i: (i, 0))
    return pl.pallas_call(
        body, name="attn_bwd_finish", grid=(S // PERM_ROWS,),
        in_specs=([pl.BlockSpec(memory_space=pl.ANY)] + [_rm_block(d, aw) for d in DILATIONS] * 3
                  + [rope_spec, rope_spec]),
        out_specs=pl.BlockSpec((PERM_ROWS, 3 * aw), lambda i: (i, 0)),
        out_shape=jax.ShapeDtypeStruct((S, W), BF16),
        input_output_aliases={0: 0},
        compiler_params=pltpu.CompilerParams(dimension_semantics=("parallel",),
                                             vmem_limit_bytes=_vmem_limit(32 << 20)),
    )(dh, *grads, cos2, sin_bwd)


def _my_place():
    x, y, c = lax.axis_index("x"), lax.axis_index("y"), lax.axis_index("c")
    return x, y, c


def _flat(px, py, pc):
    return 4 * px + 2 * py + pc


def _shard_slice(ref, axis, idx, size):
    start = pl.multiple_of(idx * size, size)
    ix = [slice(None)] * len(ref.shape)
    ix[axis] = pl.ds(start, size)
    return ref.at[tuple(ix)]


_HBM_SPEC = pl.BlockSpec(memory_space=pltpu.HBM)
_SEM_SPEC = pl.BlockSpec(memory_space=pltpu.SEMAPHORE)
_ANY_SPEC = pl.BlockSpec(memory_space=pl.ANY)
_N_PEER = N_DEV - 1
SIBLING, SAME_CORE_NEIGHBOURS, OTHER_CORE_NEIGHBOURS, DIAGONAL = (1,), (2, 4), (3, 5), (6, 7)
PEER_ORDER = SIBLING + SAME_CORE_NEIGHBOURS + OTHER_CORE_NEIGHBOURS + DIAGONAL


def _peer_of(x, y, c, r):
    return (x ^ ((r >> 2) & 1), y ^ ((r >> 1) & 1), c ^ (r & 1))


class _Exchange:
    def __init__(self, name, part, slot):
        self.name, self.part, self.slot = name, part, slot

    def _copy(self, w, r, src, land, send_sems, recv_sems, sending):
        x, y, c = _my_place()
        peer = _peer_of(x, y, c, r)
        return pltpu.make_async_remote_copy(
            src_ref=self.part(w, src, _flat(*peer)),
            dst_ref=self.slot(w, land, _flat(x, y, c) if sending else _flat(*peer)),
            send_sem=send_sems.at[w * _N_PEER + r - 1], recv_sem=recv_sems.at[w * _N_PEER + r - 1],
            device_id=peer, device_id_type=MESH)

    def start(self, srcs, lands, after=None):
        n = len(srcs)
        n_after = 0 if after is None else 1

        def body(*refs):
            src, land = refs[:n], refs[n:2 * n]
            outs = refs[2 * n + n_after:]
            send_sems, recv_sems, local_sems, token = outs[0], outs[1], outs[2], outs[3 + 2 * n]
            for w in range(n):
                self._own_copy(w, src[w], land[w], local_sems).start()
                for r in PEER_ORDER:
                    self._copy(w, r, src[w], land[w], send_sems, recv_sems, True).start()
            token[...] = jnp.zeros_like(token)

        sems = pltpu.SemaphoreType.DMA((n * _N_PEER,))
        outs = pl.pallas_call(
            body, name=self.name + "_start",
            out_shape=(sems, sems, pltpu.SemaphoreType.DMA((n,)),
                       *[pltpu.HBM(t.shape, t.dtype) for t in list(srcs) + list(lands)],
                       jax.ShapeDtypeStruct((8, 128), F32)),
            in_specs=[_HBM_SPEC] * (2 * n) + [_ANY_SPEC] * n_after,
            out_specs=(_SEM_SPEC, _SEM_SPEC, _SEM_SPEC, *[_HBM_SPEC] * (2 * n), pl.BlockSpec(memory_space=pltpu.VMEM)),
            input_output_aliases={i: 3 + i for i in range(2 * n)},
            compiler_params=pltpu.CompilerParams(has_side_effects=pltpu.SideEffectType.DATAFLOW_SIDE_EFFECTING),
        )(*[pltpu.with_memory_space_constraint(t, pltpu.HBM) for t in list(srcs) + list(lands)],
          *([after] if n_after else []))
        return outs[0], outs[1], outs[2], outs[3:3 + n], outs[3 + n:3 + 2 * n], outs[3 + 2 * n]

    def _own_copy(self, w, src, land, local_sems):
        me = _flat(*_my_place())
        return pltpu.make_async_copy(self.part(w, src, me), self.slot(w, land, me), local_sems.at[w])

    def wait(self, started, after, peers=PEER_ORDER, own=True, tag=""):
        send_sems, recv_sems, local_sems, srcs, lands, token = started
        n = len(srcs)

        def body(*refs):
            src, land = refs[:n], refs[n:2 * n]
            s_sems, r_sems, l_sems = refs[2 * n], refs[2 * n + 1], refs[2 * n + 2]
            for w in range(n):
                if own:
                    self._own_copy(w, src[w], land[w], l_sems).wait()
                for r in peers:
                    cp = self._copy(w, r, src[w], land[w], s_sems, r_sems, False)
                    cp.wait_send()
                    cp.wait_recv()

        outs = pl.pallas_call(
            body, name=self.name + "_wait" + tag,
            out_shape=[pltpu.HBM(t.shape, t.dtype) for t in list(srcs) + list(lands)],
            in_specs=[_HBM_SPEC] * (2 * n) + [_SEM_SPEC, _SEM_SPEC, _SEM_SPEC, _ANY_SPEC],
            out_specs=[_HBM_SPEC] * (2 * n),
            input_output_aliases={i: i for i in range(2 * n)},
            compiler_params=pltpu.CompilerParams(has_side_effects=pltpu.SideEffectType.DATAFLOW_SIDE_EFFECTING),
        )(*srcs, *lands, send_sems, recv_sems, local_sems, after)
        return outs[n:], (send_sems, recv_sems, local_sems, outs[:n], outs[n:], token)


DIRECT_PEERS = (1, 2, 4, 6)
FORWARDED = (3, 5, 7)


class _TwoLevelGather:
    def __init__(self, name, axes, sizes):
        self.name, self.axes, self.sizes = name, axes, sizes

    def _place(self, w, land, dev):
        return _shard_slice(land, self.axes[w], dev, self.sizes[w])

    def _direct(self, w, r, src, land, sems, sending):
        x, y, c = _my_place()
        peer = _peer_of(x, y, c, r)
        k = w * len(DIRECT_PEERS) + DIRECT_PEERS.index(r)
        return pltpu.make_async_remote_copy(
            src_ref=src, dst_ref=self._place(w, land, _flat(x, y, c) if sending else _flat(*peer)),
            send_sem=sems[0].at[k], recv_sem=sems[1].at[k], device_id=peer, device_id_type=MESH)

    def _passed_on(self, w, f, land, sems, sending):
        x, y, c = _my_place()
        owner = _flat(*_peer_of(x, y, c, (f ^ 1) if sending else f))
        slot = self._place(w, land, owner)
        k = w * len(FORWARDED) + FORWARDED.index(f)
        return pltpu.make_async_remote_copy(
            src_ref=slot, dst_ref=slot, send_sem=sems[2].at[k], recv_sem=sems[3].at[k],
            device_id=(x, y, 1 - c), device_id_type=MESH)

    def _own(self, w, src, land, sems):
        return pltpu.make_async_copy(src, self._place(w, land, _flat(*_my_place())), sems[4].at[w])

    def _call(self, suffix, body, sems, srcs, lands, after, make_sems):
        n = len(srcs)
        n_after = 0 if after is None else 1
        bufs = list(srcs) + list(lands)

        def wrapped(*refs):
            ins = refs[:2 * n]
            rest = refs[2 * n + (n_after if make_sems else 0):]
            body(ins[:n], ins[n:], rest[:5], rest[-1] if make_sems else None)

        buf_shapes = [pltpu.HBM(t.shape, t.dtype) for t in bufs]
        if make_sems:
            sem_types = [pltpu.SemaphoreType.DMA((n * len(DIRECT_PEERS),))] * 2 \
                + [pltpu.SemaphoreType.DMA((n * len(FORWARDED),))] * 2 + [pltpu.SemaphoreType.DMA((n,))]
            outs = pl.pallas_call(
                wrapped, name=self.name + suffix,
                out_shape=(*sem_types, *buf_shapes, jax.ShapeDtypeStruct((8, 128), F32)),
                in_specs=[_HBM_SPEC] * (2 * n) + [_ANY_SPEC] * n_after,
                out_specs=(*[_SEM_SPEC] * 5, *[_HBM_SPEC] * (2 * n), pl.BlockSpec(memory_space=pltpu.VMEM)),
                input_output_aliases={i: 5 + i for i in range(2 * n)},
                compiler_params=pltpu.CompilerParams(has_side_effects=pltpu.SideEffectType.DATAFLOW_SIDE_EFFECTING),
            )(*[pltpu.with_memory_space_constraint(t, pltpu.HBM) for t in bufs], *([after] if n_after else []))
            return tuple(outs[:5]), outs[5:5 + n], outs[5 + n:5 + 2 * n], outs[5 + 2 * n]
        outs = pl.pallas_call(
            wrapped, name=self.name + suffix,
            out_shape=buf_shapes,
            in_specs=[_HBM_SPEC] * (2 * n) + [_SEM_SPEC] * 5 + [_ANY_SPEC] * n_after,
            out_specs=[_HBM_SPEC] * (2 * n),
            input_output_aliases={i: i for i in range(2 * n)},
            compiler_params=pltpu.CompilerParams(has_side_effects=pltpu.SideEffectType.DATAFLOW_SIDE_EFFECTING),
        )(*bufs, *sems, *([after] if n_after else []))
        return sems, outs[:n], outs[n:], None

    def start(self, srcs, lands, after=None):
        n = len(srcs)

        def body(src, land, sems, token):
            for w in range(n):
                self._own(w, src[w], land[w], sems).start()
                for r in DIRECT_PEERS:
                    self._direct(w, r, src[w], land[w], sems, True).start()
            token[...] = jnp.zeros_like(token)

        return self._call("_start", body, None, srcs, lands, after, True)

    def forward(self, state, after, which, tag=""):
        sems, srcs, lands, token = state
        n = len(srcs)

        def body(src, land, sem_refs, _):
            for w in range(n):
                for r in which:
                    self._direct(w, r, src[w], land[w], sem_refs, False).wait_recv()
                    self._passed_on(w, r | 1, land[w], sem_refs, True).start()

        sems, srcs, lands, _ = self._call("_forward" + tag, body, sems, srcs, lands, after, False)
        return sems, srcs, lands, token

    def wait(self, state, after, direct=(), passed_on=(), sends=False, tag=""):
        sems, srcs, lands, token = state
        n = len(srcs)

        def body(src, land, sem_refs, _):
            for w in range(n):
                for r in direct:
                    self._direct(w, r, src[w], land[w], sem_refs, False).wait_recv()
                for f in passed_on:
                    self._passed_on(w, f, land[w], sem_refs, False).wait_recv()
                if sends:
                    self._own(w, src[w], land[w], sem_refs).wait()
                    for r in DIRECT_PEERS:
                        self._direct(w, r, src[w], land[w], sem_refs, True).wait_send()
                    for f in FORWARDED:
                        self._passed_on(w, f, land[w], sem_refs, True).wait_send()

        sems, srcs, lands, _ = self._call("_wait" + tag, body, sems, srcs, lands, after, False)
        return lands, (sems, srcs, lands, token)


def _scatter_exchange(name, axes, shard_sizes):
    def part(w, src, dev):
        return src if axes[w] is None else _shard_slice(src, axes[w], dev, shard_sizes[w])
    return _Exchange(name, part, lambda w, land, dev: land.at[dev])


def _adamw(name, partials, w, m, v):
    R, C = w.shape
    tr = R
    while tr * C * 4 > (1 << 20) and tr % 16 == 0:
        tr //= 2

    def body(p_ref, w_ref, m_ref, v_ref, g_ref, d_ref, nm_ref, nv_ref):
        g = p_ref[0].astype(F32)
        for jdev in range(1, N_DEV):
            g = g + p_ref[jdev].astype(F32)
        nm = ADAM_B1 * m_ref[...] + (1.0 - ADAM_B1) * g
        nv = ADAM_B2 * v_ref[...] + (1.0 - ADAM_B2) * (g * g)
        m_hat = nm / (1.0 - ADAM_B1 ** ADAM_STEP)
        v_hat = nv / (1.0 - ADAM_B2 ** ADAM_STEP)
        g_ref[...] = g
        d_ref[...] = -ADAM_LR * (m_hat / (jnp.sqrt(v_hat) + ADAM_EPS) + ADAM_WD * w_ref[...])
        nm_ref[...] = nm
        nv_ref[...] = nv

    spec = pl.BlockSpec((tr, C), lambda i: (i, 0))
    return pl.pallas_call(
        body, name=name, grid=(R // tr,),
        in_specs=[pl.BlockSpec((N_DEV, tr, C), lambda i: (0, i, 0)), spec, spec, spec],
        out_specs=[spec] * 4,
        out_shape=[jax.ShapeDtypeStruct((R, C), F32)] * 4,
        compiler_params=pltpu.CompilerParams(dimension_semantics=("parallel",),
                                             vmem_limit_bytes=_vmem_limit(24 << 20)),
    )(partials, w, m, v)


def _local_step(x, cos2, sin_fwd, sin_bwd, project_in, mix_weights, ffn_weights, pool_scale, g_mix, b_mix, g_ff, b_ff,
                target, send):
    S, D = x.shape
    aw = pw = D // 2
    u_col_block = 3
    gate_col0 = 4 * aw

    xb = x.astype(BF16)
    h, w_in = project_in(xb)
    dilated = [d for d in DILATIONS if d > 1]
    qkv = {1: h[None], **dict(zip(dilated, _to_residue_major("qkv_to_rm", h, 0, 3 * aw)))}
    fwd = [_attn_fwd(qkv[d], d, aw) for d in DILATIONS]
    o_attn, l_tot = _attn_combine([f[0] for f in fwd], [f[1] for f in fwd], aw)
    w_pool, w_ba, w_bp, w_out = mix_weights(o_attn)
    p, y_pre, pm = _pool_fwd(h, w_pool, pool_scale, pw, u_col_block)
    y_attn = _branch_attn(o_attn, w_ba)
    y_pool, merged = _branch_pool_merge(pm, w_bp, h, y_attn, gate_col0)
    w1, w2 = ffn_weights(merged)
    x1, x1b, xhat1, rstd1 = _out_proj_ln(merged, w_out, x, g_mix, b_mix)
    a = _ffn_up(x1b, w1)
    z2 = _residual_matmul("ffn_down", a, w2, "nn", x1)
    dz2, dz2b, dg_ff, db_ff, loss = _ln_loss_bwd(z2, g_ff, b_ff, target)

    tok = send("ff2", [_grad_weight("grad_w_ff2", a, dz2b)])
    dh1 = _ffn_down_bwd(dz2b, w2, a, after=tok)
    tok = send("ff1", [_grad_weight("grad_w_ff1", x1b, dh1)])
    dy1 = _residual_matmul("ffn_up_bwd", dh1, w1, "nt", dz2, after=tok)
    dz1, dz1b, dg_mix, db_mix = _ln_bwd(dy1, xhat1, rstd1, g_mix)
    tok = send("out", [_grad_weight("grad_w_out", merged, dz1b)])
    dy_attn, dy_pool, dh = _out_proj_bwd(dz1b, w_out, h, y_attn, y_pool, gate_col0, after=tok)
    tok = send("branch", [_grad_weight("grad_w_branch_attn", o_attn, dy_attn),
                          _grad_weight("grad_w_branch_pool", pm, dy_pool)])
    d_outs, statss = _branch_attn_bwd(dy_attn, w_ba, o_attn, l_tot, after=tok)
    dy_pre, d_scale = _branch_pool_bwd(dy_pool, w_bp, y_pre, pool_scale)
    dh, dw_pool = _pool_bwd(dh, dy_pre, p, w_pool, pw, u_col_block)
    per_pattern = [_attn_bwd(qkv[d], d_outs[d], statss[d], d, aw) for d in DILATIONS]
    dh = _attn_bwd_finish(dh, per_pattern, cos2, sin_bwd, aw)
    small = jnp.concatenate((d_scale, dg_mix, db_mix, dg_ff, db_ff), axis=-1)
    tok = send("in", [_grad_weight("grad_w_in", xb, dh), dw_pool.astype(BF16),
                      small.reshape(small.shape[-1] // HEAD_DIM, HEAD_DIM)])
    grad_x = _residual_matmul("in_proj_bwd", dh, w_in, "nt", dz1, after=tok)
    return loss, grad_x


def _rope_tables(positions):
    half = HEAD_DIM // 2
    inv_freq = ROPE_THETA ** (-jnp.arange(half, dtype=F32) / half)
    ang = positions.astype(F32)[:, None] * inv_freq
    cos, sin = jnp.cos(ang), jnp.sin(ang)
    cos2 = jnp.concatenate([cos, cos], axis=-1)
    sin_fwd = jnp.concatenate([-sin, sin], axis=-1)
    return cos2, sin_fwd, -sin_fwd


def kernel(x, positions, w_in, w_pool, pool_scale, w_branch_attn, w_branch_pool, w_out, ln_mix_g, ln_mix_b, w_ff1, w_ff2, ln_ff_g, ln_ff_b, loss_target, m_w_in, m_w_pool, m_pool_scale, m_w_branch_attn, m_w_branch_pool, m_w_out, m_ln_mix_g, m_ln_mix_b, m_w_ff1, m_w_ff2, m_ln_ff_g, m_ln_ff_b, v_w_in, v_w_pool, v_pool_scale, v_w_branch_attn, v_w_branch_pool, v_w_out, v_ln_mix_g, v_ln_mix_b, v_w_ff1, v_w_ff2, v_ln_ff_g, v_ln_ff_b):
    big_w = (w_in[0], w_pool[0], w_branch_attn[0], w_branch_pool[0], w_out[0], w_ff1[0], w_ff2[0])
    big_m = (m_w_in[0], m_w_pool[0], m_w_branch_attn[0], m_w_branch_pool[0], m_w_out[0], m_w_ff1[0], m_w_ff2[0])
    big_v = (v_w_in[0], v_w_pool[0], v_w_branch_attn[0], v_w_branch_pool[0], v_w_out[0], v_w_ff1[0], v_w_ff2[0])
    shard_axes = (1, 1, 1, 1, 0, 1, 0)
    small_w = (pool_scale, ln_mix_g, ln_mix_b, ln_ff_g, ln_ff_b)
    small_m = (m_pool_scale, m_ln_mix_g, m_ln_mix_b, m_ln_ff_g, m_ln_ff_b)
    small_v = (v_pool_scale, v_ln_mix_g, v_ln_mix_b, v_ln_ff_g, v_ln_ff_b)

    names = ("w_in", "w_pool", "w_branch_attn", "w_branch_pool", "w_out", "w_ff1", "w_ff2")
    axis_of = dict(zip(names, shard_axes))
    shard_of = dict(zip(names, [w.astype(BF16) for w in big_w]))

    def full_buffer(n):
        s, ax = shard_of[n], axis_of[n]
        full = list(s.shape)
        full[ax] *= N_DEV
        return lax.empty(tuple(full), s.dtype)

    def gather_group(tag, group, after):
        ex = _TwoLevelGather(tag, [axis_of[n] for n in group], [shard_of[n].shape[axis_of[n]] for n in group])
        return ex, ex.start([shard_of[n] for n in group], [full_buffer(n) for n in group], after)

    in_ex, in_state = gather_group("gather_in", ("w_in",), None)
    mix_ex, mix_state = gather_group("gather_mix", ("w_pool", "w_branch_attn", "w_branch_pool", "w_out"),
                                     in_state[-1])
    ffn_ex, ffn_state = gather_group("gather_ffn", ("w_ff1", "w_ff2"), mix_state[-1])
    states = {"mix": mix_state, "ffn": ffn_state}
    me = 4 * lax.axis_index("x") + 2 * lax.axis_index("y") + lax.axis_index("c")
    block_cols = shard_of["w_in"].shape[1]
    neighbours, diagonal = (2, 4), (6,)

    def project_in(xb):
        n_rope_blocks = 2 * (x.shape[-1] // 2) // block_cols

        def piece(tag, w, blocks, h, **kw):
            return _in_proj_piece("in_proj_" + tag, xb, w, jnp.stack(blocks).astype(jnp.int32), cos2, sin_fwd, h,
                                  block_cols, n_rope_blocks, **kw)

        h = piece("own", shard_of["w_in"], [me], None, own_shard=True, after=ffn_state[-1])
        (w_in_land,), state = in_ex.wait(in_state, h, direct=(1,), tag="_sibling")
        h = piece("sibling", w_in_land, [me ^ 1], h)
        state = in_ex.forward(state, h, neighbours, tag="_neighbours")
        h = piece("neighbours", state[2][0], [me ^ r for r in neighbours], h)
        state = in_ex.forward(state, h, diagonal, tag="_diagonal")
        h = piece("diagonal", state[2][0], [me ^ r for r in diagonal], h)
        states["mix"] = mix_ex.forward(states["mix"], h, neighbours + diagonal)
        (w_in_land,), _ = in_ex.wait(state, h, passed_on=FORWARDED, sends=True, tag="_passed_on")
        h = piece("passed_on", w_in_land, [me ^ f for f in FORWARDED], h)
        return h, w_in_land

    def mix_weights(after):
        states["ffn"] = ffn_ex.forward(states["ffn"], after, neighbours + diagonal)
        return mix_ex.wait(states["mix"], after, direct=(1,), passed_on=FORWARDED, sends=True)[0]

    def ffn_weights(after):
        return ffn_ex.wait(states["ffn"], after, direct=(1,), passed_on=FORWARDED, sends=True)[0]

    groups = {"ff2": ("w_ff2",), "ff1": ("w_ff1",), "out": ("w_out",),
              "branch": ("w_branch_attn", "w_branch_pool"), "in": ("w_in", "w_pool", "small")}
    sent = {}

    def send(key, grads_):
        axes = [axis_of.get(n) for n in groups[key]]
        sizes = [None if ax is None else g.shape[ax] // N_DEV for g, ax in zip(grads_, axes)]
        lands = []
        for g, ax, size in zip(grads_, axes, sizes):
            shard = list(g.shape)
            if ax is not None:
                shard[ax] = size
            lands.append(lax.empty((N_DEV, *shard), g.dtype))
        ex = _scatter_exchange("scatter_" + key, axes, sizes)
        sent[key] = (ex, ex.start(list(grads_), lands))
        return sent[key][1][-1]

    cos2, sin_fwd, sin_bwd = _rope_tables(positions[0])
    loss, grad_x = _local_step(
        x[0], cos2, sin_fwd, sin_bwd, project_in, mix_weights, ffn_weights, pool_scale, ln_mix_g, ln_mix_b,
        ln_ff_g, ln_ff_b, loss_target[0], send)

    state = dict(zip(names, zip(big_w, big_m, big_v)))
    n_small = sum(w.shape[-1] for w in small_w)
    small_2d = (n_small // HEAD_DIM, HEAD_DIM)
    state["small"] = tuple(jnp.concatenate(t, axis=-1).reshape(small_2d) for t in (small_w, small_m, small_v))
    grads, deltas, new_ms, new_vs = {}, {}, {}, {}
    after = grad_x
    for key in ("ff2", "ff1", "out", "branch", "in"):
        ex, started = sent[key]
        for n, part in zip(groups[key], ex.wait(started, after)[0]):
            w, m, v = state[n]
            r2 = (-1, w.shape[-1])
            w2d = w.reshape(r2)
            res = _adamw("adamw_" + n, part.reshape((N_DEV,) + w2d.shape), w2d, m.reshape(r2), v.reshape(r2))
            after = res[0]
            if n == "small":
                small_out = [t.reshape(1, n_small) for t in res]
            else:
                grads[n], deltas[n], new_ms[n], new_vs[n] = (t.reshape((1,) + w.shape) for t in res)
    small_names = ("pool_scale", "ln_mix_g", "ln_mix_b", "ln_ff_g", "ln_ff_b")
    off = 0
    for n, w in zip(small_names, small_w):
        width = w.shape[-1]
        grads[n], deltas[n], new_ms[n], new_vs[n] = (t[:, off:off + width] for t in small_out)
        off += width

    order = ("w_in", "w_pool", "pool_scale", "w_branch_attn", "w_branch_pool", "w_out", "ln_mix_g", "ln_mix_b",
             "w_ff1", "w_ff2", "ln_ff_g", "ln_ff_b")
    total_loss = lax.psum(loss[0, 0], ("x", "y", "c"))
    return (total_loss, grad_x[None], *[grads[n] for n in order], *[deltas[n] for n in order],
            *[new_ms[n] for n in order], *[new_vs[n] for n in order])
```

```python
import functools

import jax
import jax.numpy as jnp
from jax import lax
from jax.experimental import pallas as pl
from jax.experimental.pallas import tpu as pltpu

F32 = jnp.float32
BF16 = jnp.bfloat16

N_DEV = 8
HEAD_DIM = 128
SUB_BLOCK = 128
DILATIONS = (1, 4, 16)
POOL_WINDOWS = (2, 4, 8, 16)
MAX_POOL_WINDOW = 16
POOL_HALO = 128
PERM_ROWS = 256
K_TILE = 1024
LN_EPS = 1e-5
DEEPNORM_ALPHA = 2.0 ** 0.25
ROPE_THETA = 10000.0
ATTN_SCALE = HEAD_DIM ** -0.5
ADAM_LR, ADAM_B1, ADAM_B2, ADAM_EPS, ADAM_WD, ADAM_STEP = 0.001, 0.9, 0.999, 1e-08, 0.01, 10
NEG_BIG = -1e30
VMEM_CAP_V7X = 64 * 1024 * 1024
MESH = pl.DeviceIdType.MESH


def _vmem_limit(est_bytes):
    return int(min(max(est_bytes * 5 // 4 + (4 << 20), 16 << 20), VMEM_CAP_V7X - (6 << 20)))


def _nbytes(shape, dtype):
    n = 1
    for s in shape:
        n *= s
    return n * jnp.dtype(dtype).itemsize


def _mm(name, a, b, form, tiles, outs, epi, extras=(), sequential=False, after=None):
    tm, tn, tk = tiles
    if form == "nn":
        (M, K), (K2, N) = a.shape, b.shape
    elif form == "nt":
        (M, K), (N, K2) = a.shape, b.shape
    else:
        (K, M), (K2, N) = a.shape, b.shape
    assert K == K2, (name, a.shape, b.shape)
    tm, tn, tk = min(tm, M), min(tn, N), min(tk, K)
    assert M % tm == 0 and N % tn == 0 and K % tk == 0, (name, M, N, K, tm, tn, tk)
    grid = (M // tm, N // tn, K // tk)
    nk = grid[2]
    if form == "nn":
        a_spec = pl.BlockSpec((tm, tk), lambda i, j, k: (i, k))
        b_spec = pl.BlockSpec((tk, tn), lambda i, j, k: (k, j))
        contract = ((1,), (0,))
    elif form == "nt":
        a_spec = pl.BlockSpec((tm, tk), lambda i, j, k: (i, k))
        b_spec = pl.BlockSpec((tn, tk), lambda i, j, k: (j, k))
        contract = ((1,), (1,))
    else:
        a_spec = pl.BlockSpec((tk, tm), lambda i, j, k: (k, i))
        b_spec = pl.BlockSpec((tk, tn), lambda i, j, k: (k, j))
        contract = ((0,), (0,))
    n_ex, n_out = len(extras), len(outs)
    n_after = 0 if after is None else 1

    def body(a_ref, b_ref, *rest):
        ex_refs = rest[:n_ex]
        rest = rest[n_ex + n_after:]
        out_refs = rest[:n_out]
        i, j, k = pl.program_id(0), pl.program_id(1), pl.program_id(2)

        def prod():
            return lax.dot_general(a_ref[...].astype(BF16), b_ref[...].astype(BF16),
                                   (contract, ((), ())), preferred_element_type=F32)

        if nk == 1:
            epi(prod(), ex_refs, out_refs, i, j)
        else:
            acc = rest[n_out]

            @pl.when(k == 0)
            def _():
                acc[...] = prod()

            @pl.when(jnp.logical_and(k > 0, k < nk - 1))
            def _():
                acc[...] += prod()

            @pl.when(k == nk - 1)
            def _():
                epi(acc[...] + prod(), ex_refs, out_refs, i, j)

    est = 2 * (_nbytes(a_spec.block_shape, a.dtype) + _nbytes(b_spec.block_shape, b.dtype))
    est += sum(2 * _nbytes(bs, arr.dtype) for arr, bs, _ in extras)
    est += sum(2 * _nbytes(bs, dt) for _, dt, bs, _ in outs)
    est += 4 * tm * tn * 4
    sem = ("arbitrary",) * 3 if sequential else ("parallel", "parallel", "arbitrary")
    return pl.pallas_call(
        body, name=name, grid=grid,
        in_specs=([a_spec, b_spec] + [pl.BlockSpec(bs, im) for _, bs, im in extras]
                  + [pl.BlockSpec(memory_space=pl.ANY)] * n_after),
        out_specs=[pl.BlockSpec(bs, im) for _, _, bs, im in outs],
        out_shape=[jax.ShapeDtypeStruct(sh, dt) for sh, dt, _, _ in outs],
        scratch_shapes=[pltpu.VMEM((tm, tn), F32)] if nk > 1 else [],
        compiler_params=pltpu.CompilerParams(dimension_semantics=sem, vmem_limit_bytes=_vmem_limit(est)),
    )(a, b, *[arr for arr, _, _ in extras], *([after] if n_after else []))


def _tile_out(shape, dtype, tm, tn):
    return (shape, dtype, (tm, tn), lambda i, j, k: (i, j))


def _row_sum_out(width):
    return ((1, width), F32, (1, width), lambda i, j, k: (0, 0))


def _accumulate_rows(ref, value, i):
    @pl.when(i == 0)
    def _():
        ref[...] = value

    @pl.when(i > 0)
    def _():
        ref[...] += value


def _layer_norm_bwd(dy, xhat, rstd, g):
    dxh = dy * g
    m1 = jnp.mean(dxh, axis=-1, keepdims=True)
    m2 = jnp.mean(dxh * xhat, axis=-1, keepdims=True)
    return rstd * (dxh - m1 - xhat * m2)


def _rope_apply(t, cos2, sin_signed):
    return t * cos2 + pltpu.roll(t, HEAD_DIM // 2, axis=1) * sin_signed


def _in_proj_piece(name, xb, w_in, col_blocks, cos2, sin_fwd, h_so_far, block_cols, n_rope_blocks, own_shard=False,
                   after=None):
    S, D = xb.shape
    W = w_in.shape[1] * (N_DEV if own_shard else 1)
    tm = min(1024, S)
    n_blocks = col_blocks.shape[0]

    def body(cols_ref, x_ref, w_ref, cos_ref, sin_ref, *rest):
        h_ref = rest[-1]
        j = pl.program_id(1)

        @pl.when(cols_ref[j] < n_rope_blocks)
        def _():
            acc = _dot_nn(x_ref[...], w_ref[...])
            c, s = cos_ref[...], sin_ref[...]
            for hd in range(block_cols // HEAD_DIM):
                sl = slice(hd * HEAD_DIM, (hd + 1) * HEAD_DIM)
                h_ref[:, sl] = _rope_apply(acc[:, sl], c, s).astype(BF16)

        @pl.when(cols_ref[j] >= n_rope_blocks)
        def _():
            h_ref[...] = _dot_nn(x_ref[...], w_ref[...]).astype(BF16)

    row = pl.BlockSpec((tm, HEAD_DIM), lambda i, j, cols: (i, 0))
    carried = ([] if h_so_far is None else [h_so_far]) + ([] if after is None else [after])
    est = 2 * (tm * D * 2 + D * block_cols * 2 + tm * block_cols * 2 + 2 * tm * HEAD_DIM * 4) + 3 * tm * block_cols * 4
    return pl.pallas_call(
        body, name=name,
        grid_spec=pltpu.PrefetchScalarGridSpec(
            num_scalar_prefetch=1, grid=(S // tm, n_blocks),
            in_specs=[pl.BlockSpec((tm, D), lambda i, j, cols: (i, 0)),
                      pl.BlockSpec((D, block_cols), lambda i, j, cols: (0, 0 if own_shard else cols[j])), row, row]
                     + [pl.BlockSpec(memory_space=pl.ANY)] * len(carried),
            out_specs=pl.BlockSpec((tm, block_cols), lambda i, j, cols: (i, cols[j]))),
        out_shape=jax.ShapeDtypeStruct((S, W), BF16),
        input_output_aliases={} if h_so_far is None else {5: 0},
        compiler_params=pltpu.CompilerParams(dimension_semantics=("parallel", "arbitrary"),
                                             vmem_limit_bytes=_vmem_limit(est)),
    )(col_blocks, xb, w_in, cos2, sin_fwd, *carried)


def _attn_mask(mb):
    qi = lax.broadcasted_iota(jnp.int32, (SUB_BLOCK, 2 * SUB_BLOCK), 0)
    kj = lax.broadcasted_iota(jnp.int32, (SUB_BLOCK, 2 * SUB_BLOCK), 1)
    prev = jnp.logical_and(jnp.logical_and(kj < SUB_BLOCK, kj >= qi), mb > 0)
    cur = jnp.logical_and(kj >= SUB_BLOCK, kj - SUB_BLOCK <= qi)
    return jnp.logical_or(prev, cur)


def _both_blocks(prev_ref, cur_ref, sl):
    return jnp.concatenate([prev_ref[:, sl], cur_ref[:, sl]], axis=0)


def _dot_nt(a, b):
    return lax.dot_general(a, b, (((1,), (1,)), ((), ())), preferred_element_type=F32)


def _dot_tn(a, b):
    return lax.dot_general(a, b, (((0,), (0,)), ((), ())), preferred_element_type=F32)


def _dot_nn(a, b):
    return lax.dot_general(a, b, (((1,), (0,)), ((), ())), preferred_element_type=F32)


def _perm_matrix(d, to_residue_major):
    g = PERM_ROWS // d
    i = lax.broadcasted_iota(jnp.int32, (PERM_ROWS, PERM_ROWS), 0)
    j = lax.broadcasted_iota(jnp.int32, (PERM_ROWS, PERM_ROWS), 1)
    if to_residue_major:
        hit = j == (i % g) * d + i // g
    else:
        hit = j == (i % d) * g + i // d
    return hit.astype(BF16)


def _permute_rows(perm, x, terms=3):
    if x.dtype == BF16:
        return _dot_nn(perm, x)
    hi = x.astype(BF16)
    r1 = x - hi.astype(F32)
    mid = r1.astype(BF16)
    out = _dot_nn(perm, hi) + _dot_nn(perm, mid)
    if terms == 3:
        out = out + _dot_nn(perm, (r1 - mid.astype(F32)).astype(BF16))
    return out


def _rm_block(d, width):
    return pl.BlockSpec((d, PERM_ROWS // d, width), lambda i: (0, i, 0))


def _to_residue_major(name, x, col_block, width):
    S = x.shape[0]
    dils = [d for d in DILATIONS if d > 1]
    chunk = min(width, 1024)

    def body(x_ref, *out_refs):
        for d, o_ref in zip(dils, out_refs):
            perm = _perm_matrix(d, True)
            for c0 in range(0, width, chunk):
                cw = min(chunk, width - c0)
                y = _permute_rows(perm, x_ref[:, c0:c0 + cw])
                o_ref[:, :, c0:c0 + cw] = y.astype(x.dtype).reshape(d, PERM_ROWS // d, cw)

    return pl.pallas_call(
        body, name=name, grid=(S // PERM_ROWS,),
        in_specs=[pl.BlockSpec((PERM_ROWS, width), lambda i: (i, col_block))],
        out_specs=[_rm_block(d, width) for d in dils],
        out_shape=[jax.ShapeDtypeStruct((d, S // d, width), x.dtype) for d in dils],
        compiler_params=pltpu.CompilerParams(dimension_semantics=("parallel",),
                                             vmem_limit_bytes=_vmem_limit(32 << 20)),
    )(x)


def _put_column(tile, col, value):
    lane = lax.broadcasted_iota(jnp.int32, tile.shape, 1)
    return jnp.where(lane == col, value, tile)


def _attn_fwd(qkv, d, aw):
    _, rows, _ = qkv.shape
    n_heads = aw // HEAD_DIM
    nb = rows // SUB_BLOCK

    def body(q_ref, kc_ref, vc_ref, o_ref, lse_ref, kp_ref, vp_ref, s_buf, p_buf):
        step = pl.program_id(0)

        @pl.when(step == 0)
        def _():
            kp_ref[...] = jnp.zeros_like(kp_ref)
            vp_ref[...] = jnp.zeros_like(vp_ref)

        mask = _attn_mask(step % nb)
        for hd in range(n_heads):
            sl = slice(hd * HEAD_DIM, (hd + 1) * HEAD_DIM)
            s_buf[hd] = _dot_nt(q_ref[:, sl], _both_blocks(kp_ref, kc_ref, sl))
        lse_tile = jnp.zeros((SUB_BLOCK, HEAD_DIM), F32)
        inv_tile = jnp.zeros((SUB_BLOCK, HEAD_DIM), F32)
        for hd in range(n_heads):
            s = jnp.where(mask, s_buf[hd] * ATTN_SCALE, NEG_BIG)
            m = jnp.max(s, axis=-1, keepdims=True)
            p = jnp.exp(s - m)
            l = jnp.sum(p, axis=-1, keepdims=True)
            p_buf[hd] = p.astype(BF16)
            lse_tile = _put_column(lse_tile, hd, m + jnp.log(l))
            inv_tile = _put_column(inv_tile, hd, 1.0 / l)
        lse_ref[...] = lse_tile
        for hd in range(n_heads):
            sl = slice(hd * HEAD_DIM, (hd + 1) * HEAD_DIM)
            o = _dot_nn(p_buf[hd], _both_blocks(vp_ref, vc_ref, sl))
            o_ref[:, sl] = (o * inv_tile[:, hd:hd + 1]).astype(BF16)
        kp_ref[...] = kc_ref[...]
        vp_ref[...] = vc_ref[...]

    def block(col, width):
        return pl.BlockSpec((SUB_BLOCK, width), lambda s: (s, col))

    qkv2 = qkv.reshape(d * rows, qkv.shape[-1])
    o, lse = pl.pallas_call(
        body, name=f"attn_fwd_d{d}", grid=(d * nb,),
        in_specs=[block(0, aw), block(1, aw), block(2, aw)],
        out_specs=[block(0, aw), block(0, HEAD_DIM)],
        out_shape=[jax.ShapeDtypeStruct((d * rows, aw), BF16), jax.ShapeDtypeStruct((d * rows, HEAD_DIM), F32)],
        scratch_shapes=[pltpu.VMEM((SUB_BLOCK, aw), BF16), pltpu.VMEM((SUB_BLOCK, aw), BF16),
                        pltpu.VMEM((n_heads, SUB_BLOCK, 2 * SUB_BLOCK), F32),
                        pltpu.VMEM((n_heads, SUB_BLOCK, 2 * SUB_BLOCK), BF16)],
        compiler_params=pltpu.CompilerParams(dimension_semantics=("arbitrary",),
                                             vmem_limit_bytes=_vmem_limit(16 << 20)),
    )(qkv2, qkv2, qkv2)
    return o.reshape(d, rows, aw), lse.reshape(d, rows, HEAD_DIM)


def _attn_combine(outs, lses, aw):
    S = outs[0].shape[1]
    n_heads = aw // HEAD_DIM
    n_pat = len(DILATIONS)

    def body(*refs):
        o_refs, l_refs = refs[:n_pat], refs[n_pat:2 * n_pat]
        o_ref, lt_ref = refs[2 * n_pat], refs[2 * n_pat + 1]
        o_nat, l_nat = [], []
        for d, o_r, l_r in zip(DILATIONS, o_refs, l_refs):
            o_p = o_r[...].reshape(PERM_ROWS, aw)
            l_p = l_r[...].reshape(PERM_ROWS, HEAD_DIM)
            if d > 1:
                perm = _perm_matrix(d, False)
                o_p, l_p = _permute_rows(perm, o_p, terms=2), _permute_rows(perm, l_p)
            o_nat.append(o_p)
            l_nat.append(l_p)
        mx = functools.reduce(jnp.maximum, l_nat)
        es = [jnp.exp(l_p - mx) for l_p in l_nat]
        den = functools.reduce(jnp.add, es)
        lt_ref[...] = mx + jnp.log(den)
        ws = [e / den for e in es]
        for hd in range(n_heads):
            sl = slice(hd * HEAD_DIM, (hd + 1) * HEAD_DIM)
            o = ws[0][:, hd:hd + 1] * o_nat[0][:, sl]
            for pi in range(1, n_pat):
                o = o + ws[pi][:, hd:hd + 1] * o_nat[pi][:, sl]
            o_ref[:, sl] = o.astype(BF16)

    return pl.pallas_call(
        body, name="attn_combine", grid=(S // PERM_ROWS,),
        in_specs=[_rm_block(d, aw) for d in DILATIONS] + [_rm_block(d, HEAD_DIM) for d in DILATIONS],
        out_specs=[pl.BlockSpec((PERM_ROWS, aw), lambda i: (i, 0)), pl.BlockSpec((PERM_ROWS, HEAD_DIM), lambda i: (i, 0))],
        out_shape=[jax.ShapeDtypeStruct((S, aw), BF16), jax.ShapeDtypeStruct((S, HEAD_DIM), F32)],
        compiler_params=pltpu.CompilerParams(dimension_semantics=("parallel",),
                                             vmem_limit_bytes=_vmem_limit(40 << 20)),
    )(*outs, *lses)


def _band(tm, width, w, row_offset, transpose):
    t = lax.broadcasted_iota(jnp.int32, (tm, width), 0)
    u = lax.broadcasted_iota(jnp.int32, (tm, width), 1)
    dist = (u - t - row_offset) if transpose else (t + row_offset - u)
    return jnp.logical_and(dist >= 0, dist < w).astype(BF16)


def _pool_fwd(h, w_pool, pool_scale, pw, u_col_block):
    S, W = h.shape
    n_groups = len(POOL_WINDOWS)
    gw = pw // n_groups
    tm = min(512, S)
    halo_per_tile = tm // POOL_HALO

    def body(uc_ref, uh_ref, w_ref, sc_ref, p_ref, y_ref, pm_ref):
        i = pl.program_id(0)
        t_abs = i * tm + lax.broadcasted_iota(jnp.int32, (tm, 1), 0)
        for g, w in enumerate(POOL_WINDOWS):
            sl = slice(g * gw, (g + 1) * gw)
            uc = uc_ref[:, sl]
            uh = jnp.where(i > 0, uh_ref[:, sl], jnp.zeros((POOL_HALO, gw), BF16))
            ssum = _dot_nn(_band(tm, tm, w, 0, False), uc) + _dot_nn(_band(tm, POOL_HALO, w, POOL_HALO, False), uh)
            cnt = jnp.minimum(t_abs + 1, w).astype(F32)
            p = (ssum / cnt - uc.astype(F32)).astype(BF16)
            y = _dot_nn(p, w_ref[g])
            p_ref[:, sl] = p
            y_ref[:, sl] = y.astype(BF16)
            pm_ref[:, sl] = (y * sc_ref[:, sl]).astype(BF16)

    row = pl.BlockSpec((tm, pw), lambda i: (i, 0))
    return pl.pallas_call(
        body, name="pool_fwd", grid=(S // tm,),
        in_specs=[pl.BlockSpec((tm, pw), lambda i: (i, u_col_block)),
                  pl.BlockSpec((POOL_HALO, pw), lambda i: (jnp.maximum(i * halo_per_tile - 1, 0), u_col_block)),
                  pl.BlockSpec((n_groups, gw, gw), lambda i: (0, 0, 0)),
                  pl.BlockSpec((1, pw), lambda i: (0, 0))],
        out_specs=[row, row, row],
        out_shape=[jax.ShapeDtypeStruct((S, pw), BF16)] * 3,
        compiler_params=pltpu.CompilerParams(dimension_semantics=("parallel",),
                                             vmem_limit_bytes=_vmem_limit(24 << 20)),
    )(h, h, w_pool, pool_scale)


def _branch_attn(o_attn, w_ba):
    S, _ = o_attn.shape
    D = w_ba.shape[1]
    tm, tn = min(1024, S), D

    def epi(acc, ex, out, i, j):
        out[0][...] = acc.astype(BF16)

    (y,) = _mm("branch_attn", o_attn, w_ba, "nn", (tm, tn, 1024), [_tile_out((S, D), BF16, tm, tn)], epi)
    return y


def _branch_pool_merge(pm, w_bp, h, y_attn, gate_col0):
    S, _ = pm.shape
    D = w_bp.shape[1]
    tm, tn = min(512, S), D
    ga0, gp0 = gate_col0 // tn, (gate_col0 + D) // tn

    def epi(acc, ex, out, i, j):
        ga_ref, gp_ref, ya_ref = ex
        yp_ref, mg_ref = out
        yp = acc.astype(BF16)
        yp_ref[...] = yp
        mg = (jax.nn.sigmoid(ga_ref[...]).astype(F32) * ya_ref[...].astype(F32)
              + jax.nn.sigmoid(gp_ref[...]).astype(F32) * acc)
        mg_ref[...] = mg.astype(BF16)

    y_pool, merged = _mm(
        "branch_pool_merge", pm, w_bp, "nn", (tm, tn, 1024),
        [_tile_out((S, D), BF16, tm, tn), _tile_out((S, D), BF16, tm, tn)], epi,
        extras=[(h, (tm, tn), lambda i, j, k: (i, ga0 + j)), (h, (tm, tn), lambda i, j, k: (i, gp0 + j)),
                (y_attn, (tm, tn), lambda i, j, k: (i, j))])
    return y_pool, merged


def _layer_norm_rows(z, g, b):
    mu = jnp.mean(z, axis=-1, keepdims=True)
    zc = z - mu
    var = jnp.mean(zc * zc, axis=-1, keepdims=True)
    rstd = lax.rsqrt(var + LN_EPS)
    xhat = zc * rstd
    return xhat * g + b, xhat, rstd


def _out_proj_ln(merged, w_out, x, g, b):
    S, D = x.shape
    tm = min(256, S)

    def epi(acc, ex, out, i, j):
        x_ref, g_ref, b_ref = ex
        x1_ref, x1b_ref, xh_ref, rs_ref = out
        y, xhat, rstd = _layer_norm_rows(DEEPNORM_ALPHA * x_ref[...] + acc, g_ref[...], b_ref[...])
        x1_ref[...] = y
        x1b_ref[...] = y.astype(BF16)
        xh_ref[...] = xhat
        rs_ref[...] = jnp.broadcast_to(rstd, (tm, HEAD_DIM))

    row = lambda i, j, k: (i, 0)
    vec = lambda i, j, k: (0, 0)
    return _mm("out_proj_ln", merged, w_out, "nn", (tm, D, D),
               [((S, D), F32, (tm, D), row), ((S, D), BF16, (tm, D), row), ((S, D), F32, (tm, D), row),
                ((S, HEAD_DIM), F32, (tm, HEAD_DIM), row)], epi,
               extras=[(x, (tm, D), row), (g, (1, D), vec), (b, (1, D), vec)])


def _ffn_up(x1b, w1):
    S, D = x1b.shape
    F = w1.shape[1]
    tm, tn = min(1024, S), min(2048, F)

    def epi(acc, ex, out, i, j):
        r = jnp.maximum(acc, 0.0)
        out[0][...] = (r * r).astype(BF16)

    (a,) = _mm("ffn_up", x1b, w1, "nn", (tm, tn, 2 * K_TILE), [_tile_out((S, F), BF16, tm, tn)], epi)
    return a


def _residual_matmul(name, a, w, form, resid, after=None):
    S, D = resid.shape
    tm, tn = min(1024, S), min(2048, D)

    def epi(acc, ex, out, i, j):
        out[0][...] = DEEPNORM_ALPHA * ex[0][...] + acc

    (z,) = _mm(name, a, w, form, (tm, tn, K_TILE), [_tile_out((S, D), F32, tm, tn)], epi,
               extras=[(resid, (tm, tn), lambda i, j, k: (i, j))], after=after)
    return z


def _row_kernel(name, body, row_inputs, vec_inputs, row_outputs, sum_widths, tr):
    S = row_inputs[0].shape[0]
    row = lambda w: pl.BlockSpec((tr, w), lambda i: (i, 0))
    vec = lambda w: pl.BlockSpec((1, w), lambda i: (0, 0))

    def wrapped(*refs):
        body(pl.program_id(0), *refs)

    return pl.pallas_call(
        wrapped, name=name, grid=(S // tr,),
        in_specs=[row(t.shape[1]) for t in row_inputs] + [vec(t.shape[1]) for t in vec_inputs],
        out_specs=[row(w) for w, _ in row_outputs] + [vec(w) for w in sum_widths],
        out_shape=([jax.ShapeDtypeStruct((S, w), dt) for w, dt in row_outputs]
                   + [jax.ShapeDtypeStruct((1, w), F32) for w in sum_widths]),
        compiler_params=pltpu.CompilerParams(dimension_semantics=("arbitrary",),
                                             vmem_limit_bytes=_vmem_limit(40 << 20)),
    )(*row_inputs, *vec_inputs)


def _ln_loss_bwd(z2, g, b, target):
    S, D = z2.shape

    def body(i, z_ref, t_ref, g_ref, b_ref, dz_ref, dzb_ref, dg_ref, db_ref, loss_ref):
        gv = g_ref[...]
        y, xhat, rstd = _layer_norm_rows(z_ref[...], gv, b_ref[...])
        err = y - t_ref[...]
        loss = 0.5 * jnp.sum(jnp.mean(err * err, axis=-1, keepdims=True), axis=0, keepdims=True)
        dy = err * (1.0 / D)
        dz = _layer_norm_bwd(dy, xhat, rstd, gv)
        dz_ref[...] = dz
        dzb_ref[...] = dz.astype(BF16)
        _accumulate_rows(dg_ref, jnp.sum(dy * xhat, axis=0, keepdims=True), i)
        _accumulate_rows(db_ref, jnp.sum(dy, axis=0, keepdims=True), i)
        _accumulate_rows(loss_ref, jnp.broadcast_to(loss, (1, HEAD_DIM)), i)

    return _row_kernel("ln_loss_bwd", body, [z2, target], [g, b], [(D, F32), (D, BF16)], [D, D, HEAD_DIM],
                       min(256, S))


def _ln_bwd(dy, xhat, rstd, g):
    S, D = dy.shape

    def body(i, dy_ref, xh_ref, rs_ref, g_ref, dz_ref, dzb_ref, dg_ref, db_ref):
        dyv, xhat_v = dy_ref[...], xh_ref[...]
        dz = _layer_norm_bwd(dyv, xhat_v, rs_ref[:, :1], g_ref[...])
        dz_ref[...] = dz
        dzb_ref[...] = dz.astype(BF16)
        _accumulate_rows(dg_ref, jnp.sum(dyv * xhat_v, axis=0, keepdims=True), i)
        _accumulate_rows(db_ref, jnp.sum(dyv, axis=0, keepdims=True), i)

    return _row_kernel("ln_bwd", body, [dy, xhat, rstd], [g], [(D, F32), (D, BF16)], [D, D], min(256, S))


def _grad_weight(name, act, cot):
    M, N = act.shape[1], cot.shape[1]
    tm, tn = min(1024, M), min(2048, N)

    def epi(acc, ex, out, i, j):
        out[0][...] = acc.astype(BF16)

    (g,) = _mm(name, act, cot, "tn", (tm, tn, 2 * K_TILE), [_tile_out((M, N), BF16, tm, tn)], epi)
    return g


def _ffn_down_bwd(dz2b, w2, a, after=None):
    S, D = dz2b.shape
    F = w2.shape[0]
    tm, tn = min(1024, S), min(2048, F)

    def epi(acc, ex, out, i, j):
        out[0][...] = (acc * (2.0 * jnp.sqrt(ex[0][...])).astype(F32)).astype(BF16)

    (dh1,) = _mm("ffn_down_bwd", dz2b, w2, "nt", (tm, tn, 2 * K_TILE), [_tile_out((S, F), BF16, tm, tn)], epi,
                 extras=[(a, (tm, tn), lambda i, j, k: (i, j))], after=after)
    return dh1


def _out_proj_bwd(dz1b, w_out, h, y_attn, y_pool, gate_col0, after=None):
    S, D = dz1b.shape
    W = h.shape[1]
    tm = min(256, S)
    assert gate_col0 == 2 * D and W == 4 * D

    def epi(acc, ex, out, i, j):
        gates_ref, ya_ref, yp_ref = ex
        dya_ref, dyp_ref, dh_ref = out
        sa = jax.nn.sigmoid(gates_ref[:, :D])
        sp = jax.nn.sigmoid(gates_ref[:, D:])
        dya_ref[...] = (acc * sa.astype(F32)).astype(BF16)
        dyp_ref[...] = (acc * sp.astype(F32)).astype(BF16)
        dh_ref[:, :D] = (acc * (ya_ref[...] * (sa * (1.0 - sa))).astype(F32)).astype(BF16)
        dh_ref[:, D:] = (acc * (yp_ref[...] * (sp * (1.0 - sp))).astype(F32)).astype(BF16)

    row = lambda i, j, k: (i, 0)
    return _mm("out_proj_bwd", dz1b, w_out, "nt", (tm, D, D),
               [((S, D), BF16, (tm, D), row), ((S, D), BF16, (tm, D), row),
                ((S, W), BF16, (tm, 2 * D), lambda i, j, k: (i, 1))], epi,
               extras=[(h, (tm, 2 * D), lambda i, j, k: (i, 1)), (y_attn, (tm, D), row), (y_pool, (tm, D), row)],
               after=after)


def _branch_attn_bwd(dy_attn, w_ba, o_attn, l_tot, after=None):
    S, D = dy_attn.shape
    aw = w_ba.shape[0]
    n_heads = aw // HEAD_DIM
    tm = min(512, S)
    dils = [d for d in DILATIONS if d > 1]

    def epi(acc, ex, out, i, j):
        do_ref, st_ref = out[0], out[1]
        do = acc.astype(BF16)
        do_ref[...] = do
        o = ex[0][...].astype(F32)
        stats = ex[1][...]
        for hd in range(n_heads):
            sl = slice(hd * HEAD_DIM, (hd + 1) * HEAD_DIM)
            stats = _put_column(stats, n_heads + hd, jnp.sum(acc[:, sl] * o[:, sl], axis=-1, keepdims=True))
        st_ref[...] = stats
        for di, d in enumerate(dils):
            perm = _perm_matrix(d, True)
            g = PERM_ROWS // d
            for part in range(tm // PERM_ROWS):
                rows = slice(part * PERM_ROWS, (part + 1) * PERM_ROWS)
                to = slice(part * g, (part + 1) * g)
                out[2 + di][:, to, :] = _permute_rows(perm, do[rows]).astype(BF16).reshape(d, g, aw)
                out[2 + len(dils) + di][:, to, :] = _permute_rows(perm, stats[rows]).reshape(d, g, HEAD_DIM)

    row = lambda i, j, k: (i, 0)
    rm = lambda i, j, k: (0, i, 0)
    res = _mm("branch_attn_bwd", dy_attn, w_ba, "nt", (tm, aw, D),
              [((S, aw), BF16, (tm, aw), row), ((S, HEAD_DIM), F32, (tm, HEAD_DIM), row)]
              + [((d, S // d, aw), BF16, (d, tm // d, aw), rm) for d in dils]
              + [((d, S // d, HEAD_DIM), F32, (d, tm // d, HEAD_DIM), rm) for d in dils], epi,
              extras=[(o_attn, (tm, aw), row), (l_tot, (tm, HEAD_DIM), row)], after=after)
    n = len(dils)
    return ({1: res[0][None], **dict(zip(dils, res[2:2 + n]))}, {1: res[1][None], **dict(zip(dils, res[2 + n:]))})


def _branch_pool_bwd(dy_pool, w_bp, y_pre, pool_scale):
    S, D = dy_pool.shape
    pw = w_bp.shape[0]
    tm = min(512, S)

    def epi(acc, ex, out, i, j):
        y_ref, sc_ref = ex
        dyp_ref, dsc_ref = out
        dyp_ref[...] = (acc * sc_ref[...]).astype(BF16)
        _accumulate_rows(dsc_ref, jnp.sum(acc * y_ref[...].astype(F32), axis=0, keepdims=True), i)

    row = lambda i, j, k: (i, 0)
    return _mm("branch_pool_bwd", dy_pool, w_bp, "nt", (tm, pw, D),
               [((S, pw), BF16, (tm, pw), row), _row_sum_out(pw)], epi,
               extras=[(y_pre, (tm, pw), row), (pool_scale, (1, pw), lambda i, j, k: (0, 0))],
               sequential=True)


def _pool_bwd(dh, dy_pre, p, w_pool, pw, u_col_block):
    S, W = dh.shape
    n_groups = len(POOL_WINDOWS)
    gw = pw // n_groups
    tm = min(512, S)
    n_tiles = S // tm
    halo_per_tile = tm // POOL_HALO
    n_halo_blocks = S // POOL_HALO

    def body(dh_in_ref, dyc_ref, dyh_ref, p_ref, w_ref, dh_ref, dw_ref):
        del dh_in_ref
        i = pl.program_id(0)
        t_cur = i * tm + lax.broadcasted_iota(jnp.int32, (tm, 1), 0)
        t_halo = (i + 1) * tm + lax.broadcasted_iota(jnp.int32, (POOL_HALO, 1), 0)
        for g, w in enumerate(POOL_WINDOWS):
            sl = slice(g * gw, (g + 1) * gw)
            wg = w_ref[g]
            dyc = dyc_ref[:, sl]
            dyh = jnp.where(i < n_tiles - 1, dyh_ref[:, sl], jnp.zeros((POOL_HALO, gw), BF16))
            dp_cur = _dot_nt(dyc, wg)
            dp_halo = _dot_nt(dyh, wg)
            dpc_cur = (dp_cur / jnp.minimum(t_cur + 1, w).astype(F32)).astype(BF16)
            dpc_halo = (dp_halo / jnp.minimum(t_halo + 1, w).astype(F32)).astype(BF16)
            du = (_dot_nn(_band(tm, tm, w, 0, True), dpc_cur)
                  + _dot_nn(_band(tm, POOL_HALO, w, -tm, True), dpc_halo) - dp_cur)
            dh_ref[:, sl] = du.astype(BF16)
            dw = _dot_tn(p_ref[:, sl], dyc)

            @pl.when(i == 0)
            def _():
                dw_ref[g] = dw

            @pl.when(i > 0)
            def _():
                dw_ref[g] += dw

    row = pl.BlockSpec((tm, pw), lambda i: (i, 0))
    dh_new, dw_pool = pl.pallas_call(
        body, name="pool_bwd", grid=(n_tiles,),
        in_specs=[pl.BlockSpec(memory_space=pl.ANY), row,
                  pl.BlockSpec((POOL_HALO, pw), lambda i: (jnp.minimum((i + 1) * halo_per_tile, n_halo_blocks - 1), 0)),
                  row, pl.BlockSpec((n_groups, gw, gw), lambda i: (0, 0, 0))],
        out_specs=[pl.BlockSpec((tm, pw), lambda i: (i, u_col_block)),
                   pl.BlockSpec((n_groups, gw, gw), lambda i: (0, 0, 0))],
        out_shape=[jax.ShapeDtypeStruct((S, W), BF16), jax.ShapeDtypeStruct((n_groups, gw, gw), F32)],
        input_output_aliases={0: 0},
        compiler_params=pltpu.CompilerParams(dimension_semantics=("arbitrary",),
                                             vmem_limit_bytes=_vmem_limit(24 << 20)),
    )(dh, dy_pre, dy_pre, p, w_pool)
    return dh_new, dw_pool


def _attn_bwd(qkv, d_out, stats, d, aw):
    _, rows, _ = qkv.shape
    n_heads = aw // HEAD_DIM
    nb = rows // SUB_BLOCK
    n_blocks = d * nb

    def body(q_ref, kp_ref, kc_ref, vp_ref, vc_ref, do_ref, st_ref, dq_ref, dk_ref, dv_ref,
             carry_k, carry_v, s_buf, dp_buf, p_buf, ds_buf):
        step = pl.program_id(0)

        @pl.when(step == 0)
        def _():
            carry_k[...] = jnp.zeros_like(carry_k)
            carry_v[...] = jnp.zeros_like(carry_v)

        @pl.when(step < n_blocks)
        def _():
            mask = _attn_mask(step % nb)
            st = st_ref[...]
            for hd in range(n_heads):
                sl = slice(hd * HEAD_DIM, (hd + 1) * HEAD_DIM)
                s_buf[hd] = _dot_nt(q_ref[:, sl], _both_blocks(kp_ref, kc_ref, sl))
                dp_buf[hd] = _dot_nt(do_ref[:, sl], _both_blocks(vp_ref, vc_ref, sl))
            for hd in range(n_heads):
                lt, dl = st[:, hd:hd + 1], st[:, n_heads + hd:n_heads + hd + 1]
                p = jnp.where(mask, jnp.exp(jnp.where(mask, s_buf[hd] * ATTN_SCALE - lt, NEG_BIG)), 0.0)
                p_buf[hd] = p.astype(BF16)
                ds_buf[hd] = (p * (dp_buf[hd] - dl) * ATTN_SCALE).astype(BF16)
            for hd in range(n_heads):
                sl = slice(hd * HEAD_DIM, (hd + 1) * HEAD_DIM)
                dq_ref[:, sl] = _dot_nn(ds_buf[hd], _both_blocks(kp_ref, kc_ref, sl)).astype(BF16)
                dk_both = _dot_tn(ds_buf[hd], q_ref[:, sl])
                dv_both = _dot_tn(p_buf[hd], do_ref[:, sl])
                dk_ref[:, sl] = (carry_k[:, sl] + dk_both[:SUB_BLOCK]).astype(BF16)
                dv_ref[:, sl] = (carry_v[:, sl] + dv_both[:SUB_BLOCK]).astype(BF16)
                carry_k[:, sl] = dk_both[SUB_BLOCK:]
                carry_v[:, sl] = dv_both[SUB_BLOCK:]

        @pl.when(step == n_blocks)
        def _():
            dk_ref[...] = carry_k[...].astype(BF16)
            dv_ref[...] = carry_v[...].astype(BF16)

    def cur(step):
        return jnp.minimum(step, n_blocks - 1)

    def qkv_spec(col, prev):
        if prev:
            return pl.BlockSpec((SUB_BLOCK, aw), lambda s: (jnp.maximum(cur(s) - 1, 0), col))
        return pl.BlockSpec((SUB_BLOCK, aw), lambda s: (cur(s), col))

    def at_cur(w):
        return pl.BlockSpec((SUB_BLOCK, w), lambda s: (cur(s), 0))

    finished = pl.BlockSpec((SUB_BLOCK, aw), lambda s: (jnp.maximum(s - 1, 0), 0))
    pair = (n_heads, SUB_BLOCK, 2 * SUB_BLOCK)
    flat = lambda t: t.reshape(d * rows, t.shape[-1])
    qkv2 = flat(qkv)
    outs = pl.pallas_call(
        body, name=f"attn_bwd_d{d}", grid=(n_blocks + 1,),
        in_specs=[qkv_spec(0, False), qkv_spec(1, True), qkv_spec(1, False), qkv_spec(2, True), qkv_spec(2, False),
                  at_cur(aw), at_cur(HEAD_DIM)],
        out_specs=[at_cur(aw), finished, finished],
        out_shape=[jax.ShapeDtypeStruct((d * rows, aw), BF16)] * 3,
        scratch_shapes=[pltpu.VMEM((SUB_BLOCK, aw), F32), pltpu.VMEM((SUB_BLOCK, aw), F32),
                        pltpu.VMEM(pair, F32), pltpu.VMEM(pair, F32), pltpu.VMEM(pair, BF16), pltpu.VMEM(pair, BF16)],
        compiler_params=pltpu.CompilerParams(dimension_semantics=("arbitrary",),
                                             vmem_limit_bytes=_vmem_limit(24 << 20)),
    )(qkv2, qkv2, qkv2, qkv2, qkv2, flat(d_out), flat(stats))
    return [t.reshape(d, rows, aw) for t in outs]


def _attn_bwd_finish(dh, per_pattern, cos2, sin_bwd, aw):
    S, W = dh.shape
    n_heads = aw // HEAD_DIM
    n_pat = len(DILATIONS)

    def body(*refs):
        grad_refs = refs[1:1 + 3 * n_pat]
        cos_ref, sin_ref = refs[1 + 3 * n_pat], refs[2 + 3 * n_pat]
        out_ref = refs[3 + 3 * n_pat]
        perms = {d: _perm_matrix(d, False) for d in DILATIONS if d > 1}
        totals = []
        for which in range(3):
            tot = None
            for pi, d in enumerate(DILATIONS):
                g = grad_refs[which * n_pat + pi][...].reshape(PERM_ROWS, aw)
                g = _permute_rows(perms[d], g) if d > 1 else g.astype(F32)
                tot = g if tot is None else tot + g
            totals.append(tot)
        dq, dk, dv = totals
        c, s = cos_ref[...], sin_ref[...]
        for hd in range(n_heads):
            sl = slice(hd * HEAD_DIM, (hd + 1) * HEAD_DIM)
            out_ref[:, sl] = _rope_apply(dq[:, sl], c, s).astype(BF16)
            out_ref[:, aw + hd * HEAD_DIM:aw + (hd + 1) * HEAD_DIM] = _rope_apply(dk[:, sl], c, s).astype(BF16)
        out_ref[:, 2 * aw:] = dv.astype(BF16)

    grads = [pp[which] for which in range(3) for pp in per_pattern]
    rope_spec = pl.BlockSpec((PERM_ROWS, HEAD_DIM), lambda i: (i, 0))
    return pl.pallas_call(
        body, name="attn_bwd_finish", grid=(S // PERM_ROWS,),
        in_specs=([pl.BlockSpec(memory_space=pl.ANY)] + [_rm_block(d, aw) for d in DILATIONS] * 3
                  + [rope_spec, rope_spec]),
        out_specs=pl.BlockSpec((PERM_ROWS, 3 * aw), lambda i: (i, 0)),
        out_shape=jax.ShapeDtypeStruct((S, W), BF16),
        input_output_aliases={0: 0},
        compiler_params=pltpu.CompilerParams(dimension_semantics=("parallel",),
                                             vmem_limit_bytes=_vmem_limit(32 << 20)),
    )(dh, *grads, cos2, sin_bwd)


def _my_place():
    x, y, c = lax.axis_index("x"), lax.axis_index("y"), lax.axis_index("c")
    return x, y, c


def _flat(px, py, pc):
    return 4 * px + 2 * py + pc


def _shard_slice(ref, axis, idx, size):
    start = pl.multiple_of(idx * size, size)
    ix = [slice(None)] * len(ref.shape)
    ix[axis] = pl.ds(start, size)
    return ref.at[tuple(ix)]


_HBM_SPEC = pl.BlockSpec(memory_space=pltpu.HBM)
_SEM_SPEC = pl.BlockSpec(memory_space=pltpu.SEMAPHORE)
_ANY_SPEC = pl.BlockSpec(memory_space=pl.ANY)
_N_PEER = N_DEV - 1
SIBLING, SAME_CORE_NEIGHBOURS, OTHER_CORE_NEIGHBOURS, DIAGONAL = (1,), (2, 4), (3, 5), (6, 7)
PEER_ORDER = SIBLING + SAME_CORE_NEIGHBOURS + OTHER_CORE_NEIGHBOURS + DIAGONAL


def _peer_of(x, y, c, r):
    return (x ^ ((r >> 2) & 1), y ^ ((r >> 1) & 1), c ^ (r & 1))


class _Exchange:
    def __init__(self, name, part, slot):
        self.name, self.part, self.slot = name, part, slot

    def _copy(self, w, r, src, land, send_sems, recv_sems, sending):
        x, y, c = _my_place()
        peer = _peer_of(x, y, c, r)
        return pltpu.make_async_remote_copy(
            src_ref=self.part(w, src, _flat(*peer)),
            dst_ref=self.slot(w, land, _flat(x, y, c) if sending else _flat(*peer)),
            send_sem=send_sems.at[w * _N_PEER + r - 1], recv_sem=recv_sems.at[w * _N_PEER + r - 1],
            device_id=peer, device_id_type=MESH)

    def start(self, srcs, lands, after=None):
        n = len(srcs)
        n_after = 0 if after is None else 1

        def body(*refs):
            src, land = refs[:n], refs[n:2 * n]
            outs = refs[2 * n + n_after:]
            send_sems, recv_sems, local_sems, token = outs[0], outs[1], outs[2], outs[3 + 2 * n]
            for w in range(n):
                self._own_copy(w, src[w], land[w], local_sems).start()
                for r in PEER_ORDER:
                    self._copy(w, r, src[w], land[w], send_sems, recv_sems, True).start()
            token[...] = jnp.zeros_like(token)

        sems = pltpu.SemaphoreType.DMA((n * _N_PEER,))
        outs = pl.pallas_call(
            body, name=self.name + "_start",
            out_shape=(sems, sems, pltpu.SemaphoreType.DMA((n,)),
                       *[pltpu.HBM(t.shape, t.dtype) for t in list(srcs) + list(lands)],
                       jax.ShapeDtypeStruct((8, 128), F32)),
            in_specs=[_HBM_SPEC] * (2 * n) + [_ANY_SPEC] * n_after,
            out_specs=(_SEM_SPEC, _SEM_SPEC, _SEM_SPEC, *[_HBM_SPEC] * (2 * n), pl.BlockSpec(memory_space=pltpu.VMEM)),
            input_output_aliases={i: 3 + i for i in range(2 * n)},
            compiler_params=pltpu.CompilerParams(has_side_effects=pltpu.SideEffectType.DATAFLOW_SIDE_EFFECTING),
        )(*[pltpu.with_memory_space_constraint(t, pltpu.HBM) for t in list(srcs) + list(lands)],
          *([after] if n_after else []))
        return outs[0], outs[1], outs[2], outs[3:3 + n], outs[3 + n:3 + 2 * n], outs[3 + 2 * n]

    def _own_copy(self, w, src, land, local_sems):
        me = _flat(*_my_place())
        return pltpu.make_async_copy(self.part(w, src, me), self.slot(w, land, me), local_sems.at[w])

    def wait(self, started, after, peers=PEER_ORDER, own=True, tag=""):
        send_sems, recv_sems, local_sems, srcs, lands, token = started
        n = len(srcs)

        def body(*refs):
            src, land = refs[:n], refs[n:2 * n]
            s_sems, r_sems, l_sems = refs[2 * n], refs[2 * n + 1], refs[2 * n + 2]
            for w in range(n):
                if own:
                    self._own_copy(w, src[w], land[w], l_sems).wait()
                for r in peers:
                    cp = self._copy(w, r, src[w], land[w], s_sems, r_sems, False)
                    cp.wait_send()
                    cp.wait_recv()

        outs = pl.pallas_call(
            body, name=self.name + "_wait" + tag,
            out_shape=[pltpu.HBM(t.shape, t.dtype) for t in list(srcs) + list(lands)],
            in_specs=[_HBM_SPEC] * (2 * n) + [_SEM_SPEC, _SEM_SPEC, _SEM_SPEC, _ANY_SPEC],
            out_specs=[_HBM_SPEC] * (2 * n),
            input_output_aliases={i: i for i in range(2 * n)},
            compiler_params=pltpu.CompilerParams(has_side_effects=pltpu.SideEffectType.DATAFLOW_SIDE_EFFECTING),
        )(*srcs, *lands, send_sems, recv_sems, local_sems, after)
        return outs[n:], (send_sems, recv_sems, local_sems, outs[:n], outs[n:], token)


DIRECT_PEERS = (1, 2, 4, 6)
FORWARDED = (3, 5, 7)


class _TwoLevelGather:
    def __init__(self, name, axes, sizes):
        self.name, self.axes, self.sizes = name, axes, sizes

    def _place(self, w, land, dev):
        return _shard_slice(land, self.axes[w], dev, self.sizes[w])

    def _direct(self, w, r, src, land, sems, sending):
        x, y, c = _my_place()
        peer = _peer_of(x, y, c, r)
        k = w * len(DIRECT_PEERS) + DIRECT_PEERS.index(r)
        return pltpu.make_async_remote_copy(
            src_ref=src, dst_ref=self._place(w, land, _flat(x, y, c) if sending else _flat(*peer)),
            send_sem=sems[0].at[k], recv_sem=sems[1].at[k], device_id=peer, device_id_type=MESH)

    def _passed_on(self, w, f, land, sems, sending):
        x, y, c = _my_place()
        owner = _flat(*_peer_of(x, y, c, (f ^ 1) if sending else f))
        slot = self._place(w, land, owner)
        k = w * len(FORWARDED) + FORWARDED.index(f)
        return pltpu.make_async_remote_copy(
            src_ref=slot, dst_ref=slot, send_sem=sems[2].at[k], recv_sem=sems[3].at[k],
            device_id=(x, y, 1 - c), device_id_type=MESH)

    def _own(self, w, src, land, sems):
        return pltpu.make_async_copy(src, self._place(w, land, _flat(*_my_place())), sems[4].at[w])

    def _call(self, suffix, body, sems, srcs, lands, after, make_sems):
        n = len(srcs)
        n_after = 0 if after is None else 1
        bufs = list(srcs) + list(lands)

        def wrapped(*refs):
            ins = refs[:2 * n]
            rest = refs[2 * n + (n_after if make_sems else 0):]
            body(ins[:n], ins[n:], rest[:5], rest[-1] if make_sems else None)

        buf_shapes = [pltpu.HBM(t.shape, t.dtype) for t in bufs]
        if make_sems:
            sem_types = [pltpu.SemaphoreType.DMA((n * len(DIRECT_PEERS),))] * 2 \
                + [pltpu.SemaphoreType.DMA((n * len(FORWARDED),))] * 2 + [pltpu.SemaphoreType.DMA((n,))]
            outs = pl.pallas_call(
                wrapped, name=self.name + suffix,
                out_shape=(*sem_types, *buf_shapes, jax.ShapeDtypeStruct((8, 128), F32)),
                in_specs=[_HBM_SPEC] * (2 * n) + [_ANY_SPEC] * n_after,
                out_specs=(*[_SEM_SPEC] * 5, *[_HBM_SPEC] * (2 * n), pl.BlockSpec(memory_space=pltpu.VMEM)),
                input_output_aliases={i: 5 + i for i in range(2 * n)},
                compiler_params=pltpu.CompilerParams(has_side_effects=pltpu.SideEffectType.DATAFLOW_SIDE_EFFECTING),
            )(*[pltpu.with_memory_space_constraint(t, pltpu.HBM) for t in bufs], *([after] if n_after else []))
            return tuple(outs[:5]), outs[5:5 + n], outs[5 + n:5 + 2 * n], outs[5 + 2 * n]
        outs = pl.pallas_call(
            wrapped, name=self.name + suffix,
            out_shape=buf_shapes,
            in_specs=[_HBM_SPEC] * (2 * n) + [_SEM_SPEC] * 5 + [_ANY_SPEC] * n_after,
            out_specs=[_HBM_SPEC] * (2 * n),
            input_output_aliases={i: i for i in range(2 * n)},
            compiler_params=pltpu.CompilerParams(has_side_effects=pltpu.SideEffectType.DATAFLOW_SIDE_EFFECTING),
        )(*bufs, *sems, *([after] if n_after else []))
        return sems, outs[:n], outs[n:], None

    def start(self, srcs, lands, after=None):
        n = len(srcs)

        def body(src, land, sems, token):
            for w in range(n):
                self._own(w, src[w], land[w], sems).start()
                for r in DIRECT_PEERS:
                    self._direct(w, r, src[w], land[w], sems, True).start()
            token[...] = jnp.zeros_like(token)

        return self._call("_start", body, None, srcs, lands, after, True)

    def forward(self, state, after, which, tag=""):
        sems, srcs, lands, token = state
        n = len(srcs)

        def body(src, land, sem_refs, _):
            for w in range(n):
                for r in which:
                    self._direct(w, r, src[w], land[w], sem_refs, False).wait_recv()
                    self._passed_on(w, r | 1, land[w], sem_refs, True).start()

        sems, srcs, lands, _ = self._call("_forward" + tag, body, sems, srcs, lands, after, False)
        return sems, srcs, lands, token

    def wait(self, state, after, direct=(), passed_on=(), sends=False, tag=""):
        sems, srcs, lands, token = state
        n = len(srcs)

        def body(src, land, sem_refs, _):
            for w in range(n):
                for r in direct:
                    self._direct(w, r, src[w], land[w], sem_refs, False).wait_recv()
                for f in passed_on:
                    self._passed_on(w, f, land[w], sem_refs, False).wait_recv()
                if sends:
                    self._own(w, src[w], land[w], sem_refs).wait()
                    for r in DIRECT_PEERS:
                        self._direct(w, r, src[w], land[w], sem_refs, True).wait_send()
                    for f in FORWARDED:
                        self._passed_on(w, f, land[w], sem_refs, True).wait_send()

        sems, srcs, lands, _ = self._call("_wait" + tag, body, sems, srcs, lands, after, False)
        return lands, (sems, srcs, lands, token)


def _scatter_exchange(name, axes, shard_sizes):
    def part(w, src, dev):
        return src if axes[w] is None else _shard_slice(src, axes[w], dev, shard_sizes[w])
    return _Exchange(name, part, lambda w, land, dev: land.at[dev])


def _adamw(name, partials, w, m, v):
    R, C = w.shape
    tr = R
    while tr * C * 4 > (1 << 20) and tr % 16 == 0:
        tr //= 2

    def body(p_ref, w_ref, m_ref, v_ref, g_ref, d_ref, nm_ref, nv_ref):
        g = p_ref[0].astype(F32)
        for jdev in range(1, N_DEV):
            g = g + p_ref[jdev].astype(F32)
        nm = ADAM_B1 * m_ref[...] + (1.0 - ADAM_B1) * g
        nv = ADAM_B2 * v_ref[...] + (1.0 - ADAM_B2) * (g * g)
        m_hat = nm / (1.0 - ADAM_B1 ** ADAM_STEP)
        v_hat = nv / (1.0 - ADAM_B2 ** ADAM_STEP)
        g_ref[...] = g
        d_ref[...] = -ADAM_LR * (m_hat / (jnp.sqrt(v_hat) + ADAM_EPS) + ADAM_WD * w_ref[...])
        nm_ref[...] = nm
        nv_ref[...] = nv

    spec = pl.BlockSpec((tr, C), lambda i: (i, 0))
    return pl.pallas_call(
        body, name=name, grid=(R // tr,),
        in_specs=[pl.BlockSpec((N_DEV, tr, C), lambda i: (0, i, 0)), spec, spec, spec],
        out_specs=[spec] * 4,
        out_shape=[jax.ShapeDtypeStruct((R, C), F32)] * 4,
        compiler_params=pltpu.CompilerParams(dimension_semantics=("parallel",),
                                             vmem_limit_bytes=_vmem_limit(24 << 20)),
    )(partials, w, m, v)


def _local_step(x, cos2, sin_fwd, sin_bwd, project_in, mix_weights, ffn_weights, pool_scale, g_mix, b_mix, g_ff, b_ff,
                target, send):
    S, D = x.shape
    aw = pw = D // 2
    u_col_block = 3
    gate_col0 = 4 * aw

    xb = x.astype(BF16)
    h, w_in = project_in(xb)
    dilated = [d for d in DILATIONS if d > 1]
    qkv = {1: h[None], **dict(zip(dilated, _to_residue_major("qkv_to_rm", h, 0, 3 * aw)))}
    fwd = [_attn_fwd(qkv[d], d, aw) for d in DILATIONS]
    o_attn, l_tot = _attn_combine([f[0] for f in fwd], [f[1] for f in fwd], aw)
    w_pool, w_ba, w_bp, w_out = mix_weights(o_attn)
    p, y_pre, pm = _pool_fwd(h, w_pool, pool_scale, pw, u_col_block)
    y_attn = _branch_attn(o_attn, w_ba)
    y_pool, merged = _branch_pool_merge(pm, w_bp, h, y_attn, gate_col0)
    w1, w2 = ffn_weights(merged)
    x1, x1b, xhat1, rstd1 = _out_proj_ln(merged, w_out, x, g_mix, b_mix)
    a = _ffn_up(x1b, w1)
    z2 = _residual_matmul("ffn_down", a, w2, "nn", x1)
    dz2, dz2b, dg_ff, db_ff, loss = _ln_loss_bwd(z2, g_ff, b_ff, target)

    tok = send("ff2", [_grad_weight("grad_w_ff2", a, dz2b)])
    dh1 = _ffn_down_bwd(dz2b, w2, a, after=tok)
    tok = send("ff1", [_grad_weight("grad_w_ff1", x1b, dh1)])
    dy1 = _residual_matmul("ffn_up_bwd", dh1, w1, "nt", dz2, after=tok)
    dz1, dz1b, dg_mix, db_mix = _ln_bwd(dy1, xhat1, rstd1, g_mix)
    tok = send("out", [_grad_weight("grad_w_out", merged, dz1b)])
    dy_attn, dy_pool, dh = _out_proj_bwd(dz1b, w_out, h, y_attn, y_pool, gate_col0, after=tok)
    tok = send("branch", [_grad_weight("grad_w_branch_attn", o_attn, dy_attn),
                          _grad_weight("grad_w_branch_pool", pm, dy_pool)])
    d_outs, statss = _branch_attn_bwd(dy_attn, w_ba, o_attn, l_tot, after=tok)
    dy_pre, d_scale = _branch_pool_bwd(dy_pool, w_bp, y_pre, pool_scale)
    dh, dw_pool = _pool_bwd(dh, dy_pre, p, w_pool, pw, u_col_block)
    per_pattern = [_attn_bwd(qkv[d], d_outs[d], statss[d], d, aw) for d in DILATIONS]
    dh = _attn_bwd_finish(dh, per_pattern, cos2, sin_bwd, aw)
    small = jnp.concatenate((d_scale, dg_mix, db_mix, dg_ff, db_ff), axis=-1)
    tok = send("in", [_grad_weight("grad_w_in", xb, dh), dw_pool.astype(BF16),
                      small.reshape(small.shape[-1] // HEAD_DIM, HEAD_DIM)])
    grad_x = _residual_matmul("in_proj_bwd", dh, w_in, "nt", dz1, after=tok)
    return loss, grad_x


def _rope_tables(positions):
    half = HEAD_DIM // 2
    inv_freq = ROPE_THETA ** (-jnp.arange(half, dtype=F32) / half)
    ang = positions.astype(F32)[:, None] * inv_freq
    cos, sin = jnp.cos(ang), jnp.sin(ang)
    cos2 = jnp.concatenate([cos, cos], axis=-1)
    sin_fwd = jnp.concatenate([-sin, sin], axis=-1)
    return cos2, sin_fwd, -sin_fwd


def kernel(x, positions, w_in, w_pool, pool_scale, w_branch_attn, w_branch_pool, w_out, ln_mix_g, ln_mix_b, w_ff1, w_ff2, ln_ff_g, ln_ff_b, loss_target, m_w_in, m_w_pool, m_pool_scale, m_w_branch_attn, m_w_branch_pool, m_w_out, m_ln_mix_g, m_ln_mix_b, m_w_ff1, m_w_ff2, m_ln_ff_g, m_ln_ff_b, v_w_in, v_w_pool, v_pool_scale, v_w_branch_attn, v_w_branch_pool, v_w_out, v_ln_mix_g, v_ln_mix_b, v_w_ff1, v_w_ff2, v_ln_ff_g, v_ln_ff_b):
    big_w = (w_in[0], w_pool[0], w_branch_attn[0], w_branch_pool[0], w_out[0], w_ff1[0], w_ff2[0])
    big_m = (m_w_in[0], m_w_pool[0], m_w_branch_attn[0], m_w_branch_pool[0], m_w_out[0], m_w_ff1[0], m_w_ff2[0])
    big_v = (v_w_in[0], v_w_pool[0], v_w_branch_attn[0], v_w_branch_pool[0], v_w_out[0], v_w_ff1[0], v_w_ff2[0])
    shard_axes = (1, 1, 1, 1, 0, 1, 0)
    small_w = (pool_scale, ln_mix_g, ln_mix_b, ln_ff_g, ln_ff_b)
    small_m = (m_pool_scale, m_ln_mix_g, m_ln_mix_b, m_ln_ff_g, m_ln_ff_b)
    small_v = (v_pool_scale, v_ln_mix_g, v_ln_mix_b, v_ln_ff_g, v_ln_ff_b)

    names = ("w_in", "w_pool", "w_branch_attn", "w_branch_pool", "w_out", "w_ff1", "w_ff2")
    axis_of = dict(zip(names, shard_axes))
    shard_of = dict(zip(names, [w.astype(BF16) for w in big_w]))

    def full_buffer(n):
        s, ax = shard_of[n], axis_of[n]
        full = list(s.shape)
        full[ax] *= N_DEV
        return lax.empty(tuple(full), s.dtype)

    def gather_group(tag, group, after):
        ex = _TwoLevelGather(tag, [axis_of[n] for n in group], [shard_of[n].shape[axis_of[n]] for n in group])
        return ex, ex.start([shard_of[n] for n in group], [full_buffer(n) for n in group], after)

    in_ex, in_state = gather_group("gather_in", ("w_in",), None)
    mix_ex, mix_state = gather_group("gather_mix", ("w_pool", "w_branch_attn", "w_branch_pool", "w_out"),
                                     in_state[-1])
    ffn_ex, ffn_state = gather_group("gather_ffn", ("w_ff1", "w_ff2"), mix_state[-1])
    states = {"mix": mix_state, "ffn": ffn_state}
    me = 4 * lax.axis_index("x") + 2 * lax.axis_index("y") + lax.axis_index("c")
    block_cols = shard_of["w_in"].shape[1]
    neighbours, diagonal = (2, 4), (6,)

    def project_in(xb):
        n_rope_blocks = 2 * (x.shape[-1] // 2) // block_cols

        def piece(tag, w, blocks, h, **kw):
            return _in_proj_piece("in_proj_" + tag, xb, w, jnp.stack(blocks).astype(jnp.int32), cos2, sin_fwd, h,
                                  block_cols, n_rope_blocks, **kw)

        h = piece("own", shard_of["w_in"], [me], None, own_shard=True, after=ffn_state[-1])
        (w_in_land,), state = in_ex.wait(in_state, h, direct=(1,), tag="_sibling")
        h = piece("sibling", w_in_land, [me ^ 1], h)
        state = in_ex.forward(state, h, neighbours, tag="_neighbours")
        h = piece("neighbours", state[2][0], [me ^ r for r in neighbours], h)
        state = in_ex.forward(state, h, diagonal, tag="_diagonal")
        h = piece("diagonal", state[2][0], [me ^ r for r in diagonal], h)
        states["mix"] = mix_ex.forward(states["mix"], h, neighbours + diagonal)
        (w_in_land,), _ = in_ex.wait(state, h, passed_on=FORWARDED, sends=True, tag="_passed_on")
        h = piece("passed_on", w_in_land, [me ^ f for f in FORWARDED], h)
        return h, w_in_land

    def mix_weights(after):
        states["ffn"] = ffn_ex.forward(states["ffn"], after, neighbours + diagonal)
        return mix_ex.wait(states["mix"], after, direct=(1,), passed_on=FORWARDED, sends=True)[0]

    def ffn_weights(after):
        return ffn_ex.wait(states["ffn"], after, direct=(1,), passed_on=FORWARDED, sends=True)[0]

    groups = {"ff2": ("w_ff2",), "ff1": ("w_ff1",), "out": ("w_out",),
              "branch": ("w_branch_attn", "w_branch_pool"), "in": ("w_in", "w_pool", "small")}
    sent = {}

    def send(key, grads_):
        axes = [axis_of.get(n) for n in groups[key]]
        sizes = [None if ax is None else g.shape[ax] // N_DEV for g, ax in zip(grads_, axes)]
        lands = []
        for g, ax, size in zip(grads_, axes, sizes):
            shard = list(g.shape)
            if ax is not None:
                shard[ax] = size
            lands.append(lax.empty((N_DEV, *shard), g.dtype))
        ex = _scatter_exchange("scatter_" + key, axes, sizes)
        sent[key] = (ex, ex.start(list(grads_), lands))
        return sent[key][1][-1]

    cos2, sin_fwd, sin_bwd = _rope_tables(positions[0])
    loss, grad_x = _local_step(
        x[0], cos2, sin_fwd, sin_bwd, project_in, mix_weights, ffn_weights, pool_scale, ln_mix_g, ln_mix_b,
        ln_ff_g, ln_ff_b, loss_target[0], send)

    state = dict(zip(names, zip(big_w, big_m, big_v)))
    n_small = sum(w.shape[-1] for w in small_w)
    small_2d = (n_small // HEAD_DIM, HEAD_DIM)
    state["small"] = tuple(jnp.concatenate(t, axis=-1).reshape(small_2d) for t in (small_w, small_m, small_v))
    grads, deltas, new_ms, new_vs = {}, {}, {}, {}
    after = grad_x
    for key in ("ff2", "ff1", "out", "branch", "in"):
        ex, started = sent[key]
        for n, part in zip(groups[key], ex.wait(started, after)[0]):
            w, m, v = state[n]
            r2 = (-1, w.shape[-1])
            w2d = w.reshape(r2)
            res = _adamw("adamw_" + n, part.reshape((N_DEV,) + w2d.shape), w2d, m.reshape(r2), v.reshape(r2))
            after = res[0]
            if n == "small":
                small_out = [t.reshape(1, n_small) for t in res]
            else:
                grads[n], deltas[n], new_ms[n], new_vs[n] = (t.reshape((1,) + w.shape) for t in res)
    small_names = ("pool_scale", "ln_mix_g", "ln_mix_b", "ln_ff_g", "ln_ff_b")
    off = 0
    for n, w in zip(small_names, small_w):
        width = w.shape[-1]
        grads[n], deltas[n], new_ms[n], new_vs[n] = (t[:, off:off + width] for t in small_out)
        off += width

    order = ("w_in", "w_pool", "pool_scale", "w_branch_attn", "w_branch_pool", "w_out", "ln_mix_g", "ln_mix_b",
             "w_ff1", "w_ff2", "ln_ff_g", "ln_ff_b")
    total_loss = lax.psum(loss[0, 0], ("x", "y", "c"))
    return (total_loss, grad_x[None], *[grads[n] for n in order], *[deltas[n] for n in order],
            *[new_ms[n] for n in order], *[new_vs[n] for n in order])
```

```python
import functools

import jax
import jax.numpy as jnp
from jax import lax
from jax.experimental import pallas as pl
from jax.experimental.pallas import tpu as pltpu

F32 = jnp.float32
BF16 = jnp.bfloat16

N_DEV = 8
HEAD_DIM = 128
SUB_BLOCK = 128
DILATIONS = (1, 4, 16)
POOL_WINDOWS = (2, 4, 8, 16)
POOL_HALO = 128
PERM_ROWS = 256
K_TILE = 1024
LN_EPS = 1e-5
DEEPNORM_ALPHA = 2.0 ** 0.25
ROPE_THETA = 10000.0
ATTN_SCALE = HEAD_DIM ** -0.5
ADAM_LR, ADAM_B1, ADAM_B2, ADAM_EPS, ADAM_WD, ADAM_STEP = 0.001, 0.9, 0.999, 1e-08, 0.01, 10
NEG_BIG = -1e30
VMEM_CAP_V7X = 64 * 1024 * 1024
MESH = pl.DeviceIdType.MESH


def _vmem_limit(est_bytes):
    return int(min(max(est_bytes * 5 // 4 + (4 << 20), 16 << 20), VMEM_CAP_V7X - (6 << 20)))


def _nbytes(shape, dtype):
    n = 1
    for s in shape:
        n *= s
    return n * jnp.dtype(dtype).itemsize


def _mm(name, a, b, form, tiles, outs, epi, extras=(), sequential=False, after=None):
    tm, tn, tk = tiles
    if form == "nn":
        (M, K), (K2, N) = a.shape, b.shape
    elif form == "nt":
        (M, K), (N, K2) = a.shape, b.shape
    else:
        (K, M), (K2, N) = a.shape, b.shape
    assert K == K2, (name, a.shape, b.shape)
    tm, tn, tk = min(tm, M), min(tn, N), min(tk, K)
    assert M % tm == 0 and N % tn == 0 and K % tk == 0, (name, M, N, K, tm, tn, tk)
    grid = (M // tm, N // tn, K // tk)
    nk = grid[2]
    if form == "nn":
        a_spec = pl.BlockSpec((tm, tk), lambda i, j, k: (i, k))
        b_spec = pl.BlockSpec((tk, tn), lambda i, j, k: (k, j))
        contract = ((1,), (0,))
    elif form == "nt":
        a_spec = pl.BlockSpec((tm, tk), lambda i, j, k: (i, k))
        b_spec = pl.BlockSpec((tn, tk), lambda i, j, k: (j, k))
        contract = ((1,), (1,))
    else:
        a_spec = pl.BlockSpec((tk, tm), lambda i, j, k: (k, i))
        b_spec = pl.BlockSpec((tk, tn), lambda i, j, k: (k, j))
        contract = ((0,), (0,))
    n_ex, n_out = len(extras), len(outs)
    n_after = 0 if after is None else 1

    def body(a_ref, b_ref, *rest):
        ex_refs = rest[:n_ex]
        rest = rest[n_ex + n_after:]
        out_refs = rest[:n_out]
        i, j, k = pl.program_id(0), pl.program_id(1), pl.program_id(2)

        def prod():
            return lax.dot_general(a_ref[...].astype(BF16), b_ref[...].astype(BF16),
                                   (contract, ((), ())), preferred_element_type=F32)

        if nk == 1:
            epi(prod(), ex_refs, out_refs, i, j)
        else:
            acc = rest[n_out]

            @pl.when(k == 0)
            def _():
                acc[...] = prod()

            @pl.when(jnp.logical_and(k > 0, k < nk - 1))
            def _():
                acc[...] += prod()

            @pl.when(k == nk - 1)
            def _():
                epi(acc[...] + prod(), ex_refs, out_refs, i, j)

    est = 2 * (_nbytes(a_spec.block_shape, a.dtype) + _nbytes(b_spec.block_shape, b.dtype))
    est += sum(2 * _nbytes(bs, arr.dtype) for arr, bs, _ in extras)
    est += sum(2 * _nbytes(bs, dt) for _, dt, bs, _ in outs)
    est += 4 * tm * tn * 4
    sem = ("arbitrary",) * 3 if sequential else ("parallel", "parallel", "arbitrary")
    return pl.pallas_call(
        body, name=name, grid=grid,
        in_specs=([a_spec, b_spec] + [pl.BlockSpec(bs, im) for _, bs, im in extras]
                  + [pl.BlockSpec(memory_space=pl.ANY)] * n_after),
        out_specs=[pl.BlockSpec(bs, im) for _, _, bs, im in outs],
        out_shape=[jax.ShapeDtypeStruct(sh, dt) for sh, dt, _, _ in outs],
        scratch_shapes=[pltpu.VMEM((tm, tn), F32)] if nk > 1 else [],
        compiler_params=pltpu.CompilerParams(dimension_semantics=sem, vmem_limit_bytes=_vmem_limit(est)),
    )(a, b, *[arr for arr, _, _ in extras], *([after] if n_after else []))


def _tile_out(shape, dtype, tm, tn):
    return (shape, dtype, (tm, tn), lambda i, j, k: (i, j))


def _row_sum_out(width):
    return ((1, width), F32, (1, width), lambda i, j, k: (0, 0))


def _accumulate_rows(ref, value, i):
    @pl.when(i == 0)
    def _():
        ref[...] = value

    @pl.when(i > 0)
    def _():
        ref[...] += value


def _layer_norm_bwd(dy, xhat, rstd, g):
    dxh = dy * g
    m1 = jnp.mean(dxh, axis=-1, keepdims=True)
    m2 = jnp.mean(dxh * xhat, axis=-1, keepdims=True)
    return rstd * (dxh - m1 - xhat * m2)


def _rope_apply(t, cos2, sin_signed):
    return t * cos2 + pltpu.roll(t, HEAD_DIM // 2, axis=1) * sin_signed


def _in_proj_piece(name, xb, w_in, col_blocks, cos2, sin_fwd, h_so_far, block_cols, n_rope_blocks, own_shard=False,
                   after=None):
    S, D = xb.shape
    W = w_in.shape[1] * (N_DEV if own_shard else 1)
    tm = min(1024, S)
    n_blocks = col_blocks.shape[0]

    def body(cols_ref, x_ref, w_ref, cos_ref, sin_ref, *rest):
        h_ref = rest[-1]
        j = pl.program_id(1)

        @pl.when(cols_ref[j] < n_rope_blocks)
        def _():
            acc = _dot_nn(x_ref[...], w_ref[...])
            c, s = cos_ref[...], sin_ref[...]
            for hd in range(block_cols // HEAD_DIM):
                sl = slice(hd * HEAD_DIM, (hd + 1) * HEAD_DIM)
                h_ref[:, sl] = _rope_apply(acc[:, sl], c, s).astype(BF16)

        @pl.when(cols_ref[j] >= n_rope_blocks)
        def _():
            h_ref[...] = _dot_nn(x_ref[...], w_ref[...]).astype(BF16)

    row = pl.BlockSpec((tm, HEAD_DIM), lambda i, j, cols: (i, 0))
    carried = ([] if h_so_far is None else [h_so_far]) + ([] if after is None else [after])
    est = 2 * (tm * D * 2 + D * block_cols * 2 + tm * block_cols * 2 + 2 * tm * HEAD_DIM * 4) + 3 * tm * block_cols * 4
    return pl.pallas_call(
        body, name=name,
        grid_spec=pltpu.PrefetchScalarGridSpec(
            num_scalar_prefetch=1, grid=(S // tm, n_blocks),
            in_specs=[pl.BlockSpec((tm, D), lambda i, j, cols: (i, 0)),
                      pl.BlockSpec((D, block_cols), lambda i, j, cols: (0, 0 if own_shard else cols[j])), row, row]
                     + [pl.BlockSpec(memory_space=pl.ANY)] * len(carried),
            out_specs=pl.BlockSpec((tm, block_cols), lambda i, j, cols: (i, cols[j]))),
        out_shape=jax.ShapeDtypeStruct((S, W), BF16),
        input_output_aliases={} if h_so_far is None else {5: 0},
        compiler_params=pltpu.CompilerParams(dimension_semantics=("parallel", "arbitrary"),
                                             vmem_limit_bytes=_vmem_limit(est)),
    )(col_blocks, xb, w_in, cos2, sin_fwd, *carried)


def _attn_mask(mb):
    qi = lax.broadcasted_iota(jnp.int32, (SUB_BLOCK, 2 * SUB_BLOCK), 0)
    kj = lax.broadcasted_iota(jnp.int32, (SUB_BLOCK, 2 * SUB_BLOCK), 1)
    prev = jnp.logical_and(jnp.logical_and(kj < SUB_BLOCK, kj >= qi), mb > 0)
    cur = jnp.logical_and(kj >= SUB_BLOCK, kj - SUB_BLOCK <= qi)
    return jnp.logical_or(prev, cur)


def _both_blocks(prev_ref, cur_ref, sl):
    return jnp.concatenate([prev_ref[:, sl], cur_ref[:, sl]], axis=0)


def _dot_nt(a, b):
    return lax.dot_general(a, b, (((1,), (1,)), ((), ())), preferred_element_type=F32)


def _dot_tn(a, b):
    return lax.dot_general(a, b, (((0,), (0,)), ((), ())), preferred_element_type=F32)


def _dot_nn(a, b):
    return lax.dot_general(a, b, (((1,), (0,)), ((), ())), preferred_element_type=F32)


def _perm_matrix(d, to_residue_major):
    g = PERM_ROWS // d
    i = lax.broadcasted_iota(jnp.int32, (PERM_ROWS, PERM_ROWS), 0)
    j = lax.broadcasted_iota(jnp.int32, (PERM_ROWS, PERM_ROWS), 1)
    if to_residue_major:
        hit = j == (i % g) * d + i // g
    else:
        hit = j == (i % d) * g + i // d
    return hit.astype(BF16)


def _permute_rows(perm, x):
    if x.dtype == BF16:
        return _dot_nn(perm, x)
    hi = x.astype(BF16)
    r1 = x - hi.astype(F32)
    mid = r1.astype(BF16)
    lo = (r1 - mid.astype(F32)).astype(BF16)
    return (_dot_nn(perm, hi) + _dot_nn(perm, mid)) + _dot_nn(perm, lo)


def _rm_block(d, width):
    return pl.BlockSpec((d, PERM_ROWS // d, width), lambda i: (0, i, 0))


def _to_residue_major(name, x, col_block, width):
    S = x.shape[0]
    dils = [d for d in DILATIONS if d > 1]
    chunk = min(width, 1024)

    def body(x_ref, *out_refs):
        for d, o_ref in zip(dils, out_refs):
            perm = _perm_matrix(d, True)
            for c0 in range(0, width, chunk):
                cw = min(chunk, width - c0)
                y = _permute_rows(perm, x_ref[:, c0:c0 + cw])
                o_ref[:, :, c0:c0 + cw] = y.astype(x.dtype).reshape(d, PERM_ROWS // d, cw)

    return pl.pallas_call(
        body, name=name, grid=(S // PERM_ROWS,),
        in_specs=[pl.BlockSpec((PERM_ROWS, width), lambda i: (i, col_block))],
        out_specs=[_rm_block(d, width) for d in dils],
        out_shape=[jax.ShapeDtypeStruct((d, S // d, width), x.dtype) for d in dils],
        compiler_params=pltpu.CompilerParams(dimension_semantics=("parallel",),
                                             vmem_limit_bytes=_vmem_limit(32 << 20)),
    )(x)


def _put_column(tile, col, value):
    lane = lax.broadcasted_iota(jnp.int32, tile.shape, 1)
    return jnp.where(lane == col, value, tile)


def _attn_fwd(qkv, d, aw):
    _, rows, _ = qkv.shape
    n_heads = aw // HEAD_DIM
    nb = rows // SUB_BLOCK

    def body(q_ref, kc_ref, vc_ref, o_ref, lse_ref, kp_ref, vp_ref, s_buf, p_buf):
        step = pl.program_id(0)

        @pl.when(step == 0)
        def _():
            kp_ref[...] = jnp.zeros_like(kp_ref)
            vp_ref[...] = jnp.zeros_like(vp_ref)

        mask = _attn_mask(step % nb)
        for hd in range(n_heads):
            sl = slice(hd * HEAD_DIM, (hd + 1) * HEAD_DIM)
            s_buf[hd] = _dot_nt(q_ref[:, sl], _both_blocks(kp_ref, kc_ref, sl))
        lse_tile = jnp.zeros((SUB_BLOCK, HEAD_DIM), F32)
        inv_tile = jnp.zeros((SUB_BLOCK, HEAD_DIM), F32)
        for hd in range(n_heads):
            s = jnp.where(mask, s_buf[hd] * ATTN_SCALE, NEG_BIG)
            m = jnp.max(s, axis=-1, keepdims=True)
            p = jnp.exp(s - m)
            l = jnp.sum(p, axis=-1, keepdims=True)
            p_buf[hd] = p.astype(BF16)
            lse_tile = _put_column(lse_tile, hd, m + jnp.log(l))
            inv_tile = _put_column(inv_tile, hd, 1.0 / l)
        lse_ref[...] = lse_tile
        for hd in range(n_heads):
            sl = slice(hd * HEAD_DIM, (hd + 1) * HEAD_DIM)
            o = _dot_nn(p_buf[hd], _both_blocks(vp_ref, vc_ref, sl))
            o_ref[:, sl] = (o * inv_tile[:, hd:hd + 1]).astype(BF16)
        kp_ref[...] = kc_ref[...]
        vp_ref[...] = vc_ref[...]

    def block(col, width):
        return pl.BlockSpec((SUB_BLOCK, width), lambda s: (s, col))

    qkv2 = qkv.reshape(d * rows, qkv.shape[-1])
    o, lse = pl.pallas_call(
        body, name=f"attn_fwd_d{d}", grid=(d * nb,),
        in_specs=[block(0, aw), block(1, aw), block(2, aw)],
        out_specs=[block(0, aw), block(0, HEAD_DIM)],
        out_shape=[jax.ShapeDtypeStruct((d * rows, aw), BF16), jax.ShapeDtypeStruct((d * rows, HEAD_DIM), F32)],
        scratch_shapes=[pltpu.VMEM((SUB_BLOCK, aw), BF16), pltpu.VMEM((SUB_BLOCK, aw), BF16),
                        pltpu.VMEM((n_heads, SUB_BLOCK, 2 * SUB_BLOCK), F32),
                        pltpu.VMEM((n_heads, SUB_BLOCK, 2 * SUB_BLOCK), BF16)],
        compiler_params=pltpu.CompilerParams(dimension_semantics=("arbitrary",),
                                             vmem_limit_bytes=_vmem_limit(16 << 20)),
    )(qkv2, qkv2, qkv2)
    return o.reshape(d, rows, aw), lse.reshape(d, rows, HEAD_DIM)


def _attn_combine(outs, lses, aw):
    S = outs[0].shape[1]
    n_heads = aw // HEAD_DIM
    n_pat = len(DILATIONS)

    def body(*refs):
        o_refs, l_refs = refs[:n_pat], refs[n_pat:2 * n_pat]
        o_ref, lt_ref = refs[2 * n_pat], refs[2 * n_pat + 1]
        o_nat, l_nat = [], []
        for d, o_r, l_r in zip(DILATIONS, o_refs, l_refs):
            o_p = o_r[...].reshape(PERM_ROWS, aw)
            l_p = l_r[...].reshape(PERM_ROWS, HEAD_DIM)
            if d > 1:
                perm = _perm_matrix(d, False)
                o_p, l_p = _permute_rows(perm, o_p), _permute_rows(perm, l_p)
            o_nat.append(o_p)
            l_nat.append(l_p)
        mx = functools.reduce(jnp.maximum, l_nat)
        es = [jnp.exp(l_p - mx) for l_p in l_nat]
        den = functools.reduce(jnp.add, es)
        lt_ref[...] = mx + jnp.log(den)
        ws = [e / den for e in es]
        for hd in range(n_heads):
            sl = slice(hd * HEAD_DIM, (hd + 1) * HEAD_DIM)
            o = ws[0][:, hd:hd + 1] * o_nat[0][:, sl]
            for pi in range(1, n_pat):
                o = o + ws[pi][:, hd:hd + 1] * o_nat[pi][:, sl]
            o_ref[:, sl] = o.astype(BF16)

    return pl.pallas_call(
        body, name="attn_combine", grid=(S // PERM_ROWS,),
        in_specs=[_rm_block(d, aw) for d in DILATIONS] + [_rm_block(d, HEAD_DIM) for d in DILATIONS],
        out_specs=[pl.BlockSpec((PERM_ROWS, aw), lambda i: (i, 0)), pl.BlockSpec((PERM_ROWS, HEAD_DIM), lambda i: (i, 0))],
        out_shape=[jax.ShapeDtypeStruct((S, aw), BF16), jax.ShapeDtypeStruct((S, HEAD_DIM), F32)],
        compiler_params=pltpu.CompilerParams(dimension_semantics=("parallel",),
                                             vmem_limit_bytes=_vmem_limit(40 << 20)),
    )(*outs, *lses)


def _band(tm, width, w, row_offset, transpose):
    t = lax.broadcasted_iota(jnp.int32, (tm, width), 0)
    u = lax.broadcasted_iota(jnp.int32, (tm, width), 1)
    dist = (u - t - row_offset) if transpose else (t + row_offset - u)
    return jnp.logical_and(dist >= 0, dist < w).astype(BF16)


def _pool_fwd(h, w_pool, pool_scale, pw, u_col_block):
    S, W = h.shape
    n_groups = len(POOL_WINDOWS)
    gw = pw // n_groups
    tm = min(512, S)
    halo_per_tile = tm // POOL_HALO

    def body(uc_ref, uh_ref, w_ref, sc_ref, p_ref, y_ref, pm_ref):
        i = pl.program_id(0)
        t_abs = i * tm + lax.broadcasted_iota(jnp.int32, (tm, 1), 0)
        for g, w in enumerate(POOL_WINDOWS):
            sl = slice(g * gw, (g + 1) * gw)
            uc = uc_ref[:, sl]
            uh = jnp.where(i > 0, uh_ref[:, sl], jnp.zeros((POOL_HALO, gw), BF16))
            ssum = _dot_nn(_band(tm, tm, w, 0, False), uc) + _dot_nn(_band(tm, POOL_HALO, w, POOL_HALO, False), uh)
            cnt = jnp.minimum(t_abs + 1, w).astype(F32)
            p = (ssum / cnt - uc.astype(F32)).astype(BF16)
            y = _dot_nn(p, w_ref[g])
            p_ref[:, sl] = p
            y_ref[:, sl] = y.astype(BF16)
            pm_ref[:, sl] = (y * sc_ref[:, sl]).astype(BF16)

    row = pl.BlockSpec((tm, pw), lambda i: (i, 0))
    return pl.pallas_call(
        body, name="pool_fwd", grid=(S // tm,),
        in_specs=[pl.BlockSpec((tm, pw), lambda i: (i, u_col_block)),
                  pl.BlockSpec((POOL_HALO, pw), lambda i: (jnp.maximum(i * halo_per_tile - 1, 0), u_col_block)),
                  pl.BlockSpec((n_groups, gw, gw), lambda i: (0, 0, 0)),
                  pl.BlockSpec((1, pw), lambda i: (0, 0))],
        out_specs=[row, row, row],
        out_shape=[jax.ShapeDtypeStruct((S, pw), BF16)] * 3,
        compiler_params=pltpu.CompilerParams(dimension_semantics=("parallel",),
                                             vmem_limit_bytes=_vmem_limit(24 << 20)),
    )(h, h, w_pool, pool_scale)


def _branch_attn(o_attn, w_ba):
    S, _ = o_attn.shape
    D = w_ba.shape[1]
    tm, tn = min(1024, S), D

    def epi(acc, ex, out, i, j):
        out[0][...] = acc.astype(BF16)

    (y,) = _mm("branch_attn", o_attn, w_ba, "nn", (tm, tn, 1024), [_tile_out((S, D), BF16, tm, tn)], epi)
    return y


def _branch_pool_merge(pm, w_bp, h, y_attn, gate_col0):
    S, _ = pm.shape
    D = w_bp.shape[1]
    tm, tn = min(512, S), D
    ga0, gp0 = gate_col0 // tn, (gate_col0 + D) // tn

    def epi(acc, ex, out, i, j):
        ga_ref, gp_ref, ya_ref = ex
        yp_ref, mg_ref = out
        yp = acc.astype(BF16)
        yp_ref[...] = yp
        mg = (jax.nn.sigmoid(ga_ref[...]).astype(F32) * ya_ref[...].astype(F32)
              + jax.nn.sigmoid(gp_ref[...]).astype(F32) * acc)
        mg_ref[...] = mg.astype(BF16)

    y_pool, merged = _mm(
        "branch_pool_merge", pm, w_bp, "nn", (tm, tn, 1024),
        [_tile_out((S, D), BF16, tm, tn), _tile_out((S, D), BF16, tm, tn)], epi,
        extras=[(h, (tm, tn), lambda i, j, k: (i, ga0 + j)), (h, (tm, tn), lambda i, j, k: (i, gp0 + j)),
                (y_attn, (tm, tn), lambda i, j, k: (i, j))])
    return y_pool, merged


def _layer_norm_rows(z, g, b):
    mu = jnp.mean(z, axis=-1, keepdims=True)
    zc = z - mu
    var = jnp.mean(zc * zc, axis=-1, keepdims=True)
    rstd = lax.rsqrt(var + LN_EPS)
    xhat = zc * rstd
    return xhat * g + b, xhat, rstd


def _out_proj_ln(merged, w_out, x, g, b):
    S, D = x.shape
    tm = min(256, S)

    def epi(acc, ex, out, i, j):
        x_ref, g_ref, b_ref = ex
        x1_ref, x1b_ref, xh_ref, rs_ref = out
        y, xhat, rstd = _layer_norm_rows(DEEPNORM_ALPHA * x_ref[...] + acc, g_ref[...], b_ref[...])
        x1_ref[...] = y
        x1b_ref[...] = y.astype(BF16)
        xh_ref[...] = xhat
        rs_ref[...] = jnp.broadcast_to(rstd, (tm, HEAD_DIM))

    row = lambda i, j, k: (i, 0)
    vec = lambda i, j, k: (0, 0)
    return _mm("out_proj_ln", merged, w_out, "nn", (tm, D, D),
               [((S, D), F32, (tm, D), row), ((S, D), BF16, (tm, D), row), ((S, D), F32, (tm, D), row),
                ((S, HEAD_DIM), F32, (tm, HEAD_DIM), row)], epi,
               extras=[(x, (tm, D), row), (g, (1, D), vec), (b, (1, D), vec)])


def _ffn_up(x1b, w1):
    S, D = x1b.shape
    F = w1.shape[1]
    tm, tn = min(1024, S), min(2048, F)

    def epi(acc, ex, out, i, j):
        r = jnp.maximum(acc, 0.0)
        out[0][...] = (r * r).astype(BF16)

    (a,) = _mm("ffn_up", x1b, w1, "nn", (tm, tn, 2 * K_TILE), [_tile_out((S, F), BF16, tm, tn)], epi)
    return a


def _residual_matmul(name, a, w, form, resid, after=None):
    S, D = resid.shape
    tm, tn = min(1024, S), min(2048, D)

    def epi(acc, ex, out, i, j):
        out[0][...] = DEEPNORM_ALPHA * ex[0][...] + acc

    (z,) = _mm(name, a, w, form, (tm, tn, K_TILE), [_tile_out((S, D), F32, tm, tn)], epi,
               extras=[(resid, (tm, tn), lambda i, j, k: (i, j))], after=after)
    return z


def _row_kernel(name, body, row_inputs, vec_inputs, row_outputs, sum_widths, tr):
    S = row_inputs[0].shape[0]
    row = lambda w: pl.BlockSpec((tr, w), lambda i: (i, 0))
    vec = lambda w: pl.BlockSpec((1, w), lambda i: (0, 0))

    def wrapped(*refs):
        body(pl.program_id(0), *refs)

    return pl.pallas_call(
        wrapped, name=name, grid=(S // tr,),
        in_specs=[row(t.shape[1]) for t in row_inputs] + [vec(t.shape[1]) for t in vec_inputs],
        out_specs=[row(w) for w, _ in row_outputs] + [vec(w) for w in sum_widths],
        out_shape=([jax.ShapeDtypeStruct((S, w), dt) for w, dt in row_outputs]
                   + [jax.ShapeDtypeStruct((1, w), F32) for w in sum_widths]),
        compiler_params=pltpu.CompilerParams(dimension_semantics=("arbitrary",),
                                             vmem_limit_bytes=_vmem_limit(40 << 20)),
    )(*row_inputs, *vec_inputs)


def _ln_loss_bwd(z2, g, b, target):
    S, D = z2.shape

    def body(i, z_ref, t_ref, g_ref, b_ref, dz_ref, dzb_ref, dg_ref, db_ref, loss_ref):
        gv = g_ref[...]
        y, xhat, rstd = _layer_norm_rows(z_ref[...], gv, b_ref[...])
        err = y - t_ref[...]
        loss = 0.5 * jnp.sum(jnp.mean(err * err, axis=-1, keepdims=True), axis=0, keepdims=True)
        dy = err * (1.0 / D)
        dz = _layer_norm_bwd(dy, xhat, rstd, gv)
        dz_ref[...] = dz
        dzb_ref[...] = dz.astype(BF16)
        _accumulate_rows(dg_ref, jnp.sum(dy * xhat, axis=0, keepdims=True), i)
        _accumulate_rows(db_ref, jnp.sum(dy, axis=0, keepdims=True), i)
        _accumulate_rows(loss_ref, jnp.broadcast_to(loss, (1, HEAD_DIM)), i)

    return _row_kernel("ln_loss_bwd", body, [z2, target], [g, b], [(D, F32), (D, BF16)], [D, D, HEAD_DIM],
                       min(256, S))


def _ln_bwd(dy, xhat, rstd, g):
    S, D = dy.shape

    def body(i, dy_ref, xh_ref, rs_ref, g_ref, dz_ref, dzb_ref, dg_ref, db_ref):
        dyv, xhat_v = dy_ref[...], xh_ref[...]
        dz = _layer_norm_bwd(dyv, xhat_v, rs_ref[:, :1], g_ref[...])
        dz_ref[...] = dz
        dzb_ref[...] = dz.astype(BF16)
        _accumulate_rows(dg_ref, jnp.sum(dyv * xhat_v, axis=0, keepdims=True), i)
        _accumulate_rows(db_ref, jnp.sum(dyv, axis=0, keepdims=True), i)

    return _row_kernel("ln_bwd", body, [dy, xhat, rstd], [g], [(D, F32), (D, BF16)], [D, D], min(256, S))


def _grad_weight(name, act, cot):
    M, N = act.shape[1], cot.shape[1]
    tm, tn = min(1024, M), min(2048, N)

    def epi(acc, ex, out, i, j):
        out[0][...] = acc.astype(BF16)

    (g,) = _mm(name, act, cot, "tn", (tm, tn, 2 * K_TILE), [_tile_out((M, N), BF16, tm, tn)], epi)
    return g


def _ffn_down_bwd(dz2b, w2, a, after=None):
    S, D = dz2b.shape
    F = w2.shape[0]
    tm, tn = min(1024, S), min(2048, F)

    def epi(acc, ex, out, i, j):
        out[0][...] = (acc * (2.0 * jnp.sqrt(ex[0][...])).astype(F32)).astype(BF16)

    (dh1,) = _mm("ffn_down_bwd", dz2b, w2, "nt", (tm, tn, 2 * K_TILE), [_tile_out((S, F), BF16, tm, tn)], epi,
                 extras=[(a, (tm, tn), lambda i, j, k: (i, j))], after=after)
    return dh1


def _out_proj_bwd(dz1b, w_out, h, y_attn, y_pool, gate_col0, after=None):
    S, D = dz1b.shape
    W = h.shape[1]
    tm = min(256, S)
    assert gate_col0 == 2 * D and W == 4 * D

    def epi(acc, ex, out, i, j):
        gates_ref, ya_ref, yp_ref = ex
        dya_ref, dyp_ref, dh_ref = out
        sa = jax.nn.sigmoid(gates_ref[:, :D])
        sp = jax.nn.sigmoid(gates_ref[:, D:])
        dya_ref[...] = (acc * sa.astype(F32)).astype(BF16)
        dyp_ref[...] = (acc * sp.astype(F32)).astype(BF16)
        dh_ref[:, :D] = (acc * (ya_ref[...] * (sa * (1.0 - sa))).astype(F32)).astype(BF16)
        dh_ref[:, D:] = (acc * (yp_ref[...] * (sp * (1.0 - sp))).astype(F32)).astype(BF16)

    row = lambda i, j, k: (i, 0)
    return _mm("out_proj_bwd", dz1b, w_out, "nt", (tm, D, D),
               [((S, D), BF16, (tm, D), row), ((S, D), BF16, (tm, D), row),
                ((S, W), BF16, (tm, 2 * D), lambda i, j, k: (i, 1))], epi,
               extras=[(h, (tm, 2 * D), lambda i, j, k: (i, 1)), (y_attn, (tm, D), row), (y_pool, (tm, D), row)],
               after=after)


def _branch_attn_bwd(dy_attn, w_ba, o_attn, l_tot, after=None):
    S, D = dy_attn.shape
    aw = w_ba.shape[0]
    n_heads = aw // HEAD_DIM
    tm = min(512, S)
    dils = [d for d in DILATIONS if d > 1]

    def epi(acc, ex, out, i, j):
        do_ref, st_ref = out[0], out[1]
        do = acc.astype(BF16)
        do_ref[...] = do
        o = ex[0][...].astype(F32)
        stats = ex[1][...]
        for hd in range(n_heads):
            sl = slice(hd * HEAD_DIM, (hd + 1) * HEAD_DIM)
            stats = _put_column(stats, n_heads + hd, jnp.sum(acc[:, sl] * o[:, sl], axis=-1, keepdims=True))
        st_ref[...] = stats
        for di, d in enumerate(dils):
            perm = _perm_matrix(d, True)
            g = PERM_ROWS // d
            for part in range(tm // PERM_ROWS):
                rows = slice(part * PERM_ROWS, (part + 1) * PERM_ROWS)
                to = slice(part * g, (part + 1) * g)
                out[2 + di][:, to, :] = _permute_rows(perm, do[rows]).astype(BF16).reshape(d, g, aw)
                out[2 + len(dils) + di][:, to, :] = _permute_rows(perm, stats[rows]).reshape(d, g, HEAD_DIM)

    row = lambda i, j, k: (i, 0)
    rm = lambda i, j, k: (0, i, 0)
    res = _mm("branch_attn_bwd", dy_attn, w_ba, "nt", (tm, aw, D),
              [((S, aw), BF16, (tm, aw), row), ((S, HEAD_DIM), F32, (tm, HEAD_DIM), row)]
              + [((d, S // d, aw), BF16, (d, tm // d, aw), rm) for d in dils]
              + [((d, S // d, HEAD_DIM), F32, (d, tm // d, HEAD_DIM), rm) for d in dils], epi,
              extras=[(o_attn, (tm, aw), row), (l_tot, (tm, HEAD_DIM), row)], after=after)
    n = len(dils)
    return ({1: res[0][None], **dict(zip(dils, res[2:2 + n]))}, {1: res[1][None], **dict(zip(dils, res[2 + n:]))})


def _branch_pool_bwd(dy_pool, w_bp, y_pre, pool_scale):
    S, D = dy_pool.shape
    pw = w_bp.shape[0]
    tm = min(512, S)

    def epi(acc, ex, out, i, j):
        y_ref, sc_ref = ex
        dyp_ref, dsc_ref = out
        dyp_ref[...] = (acc * sc_ref[...]).astype(BF16)
        _accumulate_rows(dsc_ref, jnp.sum(acc * y_ref[...].astype(F32), axis=0, keepdims=True), i)

    row = lambda i, j, k: (i, 0)
    return _mm("branch_pool_bwd", dy_pool, w_bp, "nt", (tm, pw, D),
               [((S, pw), BF16, (tm, pw), row), _row_sum_out(pw)], epi,
               extras=[(y_pre, (tm, pw), row), (pool_scale, (1, pw), lambda i, j, k: (0, 0))],
               sequential=True)


def _pool_bwd(dh, dy_pre, p, w_pool, pw, u_col_block):
    S, W = dh.shape
    n_groups = len(POOL_WINDOWS)
    gw = pw // n_groups
    tm = min(512, S)
    n_tiles = S // tm
    halo_per_tile = tm // POOL_HALO
    n_halo_blocks = S // POOL_HALO

    def body(dh_in_ref, dyc_ref, dyh_ref, p_ref, w_ref, dh_ref, dw_ref):
        del dh_in_ref
        i = pl.program_id(0)
        t_cur = i * tm + lax.broadcasted_iota(jnp.int32, (tm, 1), 0)
        t_halo = (i + 1) * tm + lax.broadcasted_iota(jnp.int32, (POOL_HALO, 1), 0)
        for g, w in enumerate(POOL_WINDOWS):
            sl = slice(g * gw, (g + 1) * gw)
            wg = w_ref[g]
            dyc = dyc_ref[:, sl]
            dyh = jnp.where(i < n_tiles - 1, dyh_ref[:, sl], jnp.zeros((POOL_HALO, gw), BF16))
            dp_cur = _dot_nt(dyc, wg)
            dp_halo = _dot_nt(dyh, wg)
            dpc_cur = (dp_cur / jnp.minimum(t_cur + 1, w).astype(F32)).astype(BF16)
            dpc_halo = (dp_halo / jnp.minimum(t_halo + 1, w).astype(F32)).astype(BF16)
            du = (_dot_nn(_band(tm, tm, w, 0, True), dpc_cur)
                  + _dot_nn(_band(tm, POOL_HALO, w, -tm, True), dpc_halo) - dp_cur)
            dh_ref[:, sl] = du.astype(BF16)
            dw = _dot_tn(p_ref[:, sl], dyc)

            @pl.when(i == 0)
            def _():
                dw_ref[g] = dw

            @pl.when(i > 0)
            def _():
                dw_ref[g] += dw

    row = pl.BlockSpec((tm, pw), lambda i: (i, 0))
    dh_new, dw_pool = pl.pallas_call(
        body, name="pool_bwd", grid=(n_tiles,),
        in_specs=[pl.BlockSpec(memory_space=pl.ANY), row,
                  pl.BlockSpec((POOL_HALO, pw), lambda i: (jnp.minimum((i + 1) * halo_per_tile, n_halo_blocks - 1), 0)),
                  row, pl.BlockSpec((n_groups, gw, gw), lambda i: (0, 0, 0))],
        out_specs=[pl.BlockSpec((tm, pw), lambda i: (i, u_col_block)),
                   pl.BlockSpec((n_groups, gw, gw), lambda i: (0, 0, 0))],
        out_shape=[jax.ShapeDtypeStruct((S, W), BF16), jax.ShapeDtypeStruct((n_groups, gw, gw), F32)],
        input_output_aliases={0: 0},
        compiler_params=pltpu.CompilerParams(dimension_semantics=("arbitrary",),
                                             vmem_limit_bytes=_vmem_limit(24 << 20)),
    )(dh, dy_pre, dy_pre, p, w_pool)
    return dh_new, dw_pool


def _attn_bwd(qkv, d_out, stats, d, aw):
    _, rows, _ = qkv.shape
    n_heads = aw // HEAD_DIM
    nb = rows // SUB_BLOCK
    n_blocks = d * nb

    def body(q_ref, kp_ref, kc_ref, vp_ref, vc_ref, do_ref, st_ref, dq_ref, dk_ref, dv_ref,
             carry_k, carry_v, s_buf, dp_buf, p_buf, ds_buf):
        step = pl.program_id(0)

        @pl.when(step == 0)
        def _():
            carry_k[...] = jnp.zeros_like(carry_k)
            carry_v[...] = jnp.zeros_like(carry_v)

        @pl.when(step < n_blocks)
        def _():
            mask = _attn_mask(step % nb)
            st = st_ref[...]
            for hd in range(n_heads):
                sl = slice(hd * HEAD_DIM, (hd + 1) * HEAD_DIM)
                s_buf[hd] = _dot_nt(q_ref[:, sl], _both_blocks(kp_ref, kc_ref, sl))
                dp_buf[hd] = _dot_nt(do_ref[:, sl], _both_blocks(vp_ref, vc_ref, sl))
            for hd in range(n_heads):
                lt, dl = st[:, hd:hd + 1], st[:, n_heads + hd:n_heads + hd + 1]
                p = jnp.where(mask, jnp.exp(jnp.where(mask, s_buf[hd] * ATTN_SCALE - lt, NEG_BIG)), 0.0)
                p_buf[hd] = p.astype(BF16)
                ds_buf[hd] = (p * (dp_buf[hd] - dl) * ATTN_SCALE).astype(BF16)
            for hd in range(n_heads):
                sl = slice(hd * HEAD_DIM, (hd + 1) * HEAD_DIM)
                dq_ref[:, sl] = _dot_nn(ds_buf[hd], _both_blocks(kp_ref, kc_ref, sl)).astype(BF16)
                dk_both = _dot_tn(ds_buf[hd], q_ref[:, sl])
                dv_both = _dot_tn(p_buf[hd], do_ref[:, sl])
                dk_ref[:, sl] = (carry_k[:, sl] + dk_both[:SUB_BLOCK]).astype(BF16)
                dv_ref[:, sl] = (carry_v[:, sl] + dv_both[:SUB_BLOCK]).astype(BF16)
                carry_k[:, sl] = dk_both[SUB_BLOCK:]
                carry_v[:, sl] = dv_both[SUB_BLOCK:]

        @pl.when(step == n_blocks)
        def _():
            dk_ref[...] = carry_k[...].astype(BF16)
            dv_ref[...] = carry_v[...].astype(BF16)

    def cur(step):
        return jnp.minimum(step, n_blocks - 1)

    def qkv_spec(col, prev):
        if prev:
            return pl.BlockSpec((SUB_BLOCK, aw), lambda s: (jnp.maximum(cur(s) - 1, 0), col))
        return pl.BlockSpec((SUB_BLOCK, aw), lambda s: (cur(s), col))

    def at_cur(w):
        return pl.BlockSpec((SUB_BLOCK, w), lambda s: (cur(s), 0))

    finished = pl.BlockSpec((SUB_BLOCK, aw), lambda s: (jnp.maximum(s - 1, 0), 0))
    pair = (n_heads, SUB_BLOCK, 2 * SUB_BLOCK)
    flat = lambda t: t.reshape(d * rows, t.shape[-1])
    qkv2 = flat(qkv)
    outs = pl.pallas_call(
        body, name=f"attn_bwd_d{d}", grid=(n_blocks + 1,),
        in_specs=[qkv_spec(0, False), qkv_spec(1, True), qkv_spec(1, False), qkv_spec(2, True), qkv_spec(2, False),
                  at_cur(aw), at_cur(HEAD_DIM)],
        out_specs=[at_cur(aw), finished, finished],
        out_shape=[jax.ShapeDtypeStruct((d * rows, aw), BF16)] * 3,
        scratch_shapes=[pltpu.VMEM((SUB_BLOCK, aw), F32), pltpu.VMEM((SUB_BLOCK, aw), F32),
                        pltpu.VMEM(pair, F32), pltpu.VMEM(pair, F32), pltpu.VMEM(pair, BF16), pltpu.VMEM(pair, BF16)],
        compiler_params=pltpu.CompilerParams(dimension_semantics=("arbitrary",),
                                             vmem_limit_bytes=_vmem_limit(24 << 20)),
    )(qkv2, qkv2, qkv2, qkv2, qkv2, flat(d_out), flat(stats))
    return [t.reshape(d, rows, aw) for t in outs]


def _attn_bwd_finish(dh, per_pattern, cos2, sin_bwd, aw):
    S, W = dh.shape
    n_heads = aw // HEAD_DIM
    n_pat = len(DILATIONS)

    def body(*refs):
        grad_refs = refs[1:1 + 3 * n_pat]
        cos_ref, sin_ref = refs[1 + 3 * n_pat], refs[2 + 3 * n_pat]
        out_ref = refs[3 + 3 * n_pat]
        perms = {d: _perm_matrix(d, False) for d in DILATIONS if d > 1}
        totals = []
        for which in range(3):
            tot = None
            for pi, d in enumerate(DILATIONS):
                g = grad_refs[which * n_pat + pi][...].reshape(PERM_ROWS, aw)
                g = _permute_rows(perms[d], g) if d > 1 else g.astype(F32)
                tot = g if tot is None else tot + g
            totals.append(tot)
        dq, dk, dv = totals
        c, s = cos_ref[...], sin_ref[...]
        for hd in range(n_heads):
            sl = slice(hd * HEAD_DIM, (hd + 1) * HEAD_DIM)
            out_ref[:, sl] = _rope_apply(dq[:, sl], c, s).astype(BF16)
            out_ref[:, aw + hd * HEAD_DIM:aw + (hd + 1) * HEAD_DIM] = _rope_apply(dk[:, sl], c, s).astype(BF16)
        out_ref[:, 2 * aw:] = dv.astype(BF16)

    grads = [pp[which] for which in range(3) for pp in per_pattern]
    rope_spec = pl.BlockSpec((PERM_ROWS, HEAD_DIM), lambda i: (i, 0))
    return pl.pallas_call(
        body, name="attn_bwd_finish", grid=(S // PERM_ROWS,),
        in_specs=([pl.BlockSpec(memory_space=pl.ANY)] + [_rm_block(d, aw) for d in DILATIONS] * 3
                  + [rope_spec, rope_spec]),
        out_specs=pl.BlockSpec((PERM_ROWS, 3 * aw), lambda i: (i, 0)),
        out_shape=jax.ShapeDtypeStruct((S, W), BF16),
        input_output_aliases={0: 0},
        compiler_params=pltpu.CompilerParams(dimension_semantics=("parallel",),
                                             vmem_limit_bytes=_vmem_limit(32 << 20)),
    )(dh, *grads, cos2, sin_bwd)


def _my_place():
    x, y, c = lax.axis_index("x"), lax.axis_index("y"), lax.axis_index("c")
    return x, y, c


def _flat(px, py, pc):
    return 4 * px + 2 * py + pc


def _shard_slice(ref, axis, idx, size):
    start = pl.multiple_of(idx * size, size)
    ix = [slice(None)] * len(ref.shape)
    ix[axis] = pl.ds(start, size)
    return ref.at[tuple(ix)]


_HBM_SPEC = pl.BlockSpec(memory_space=pltpu.HBM)
_SEM_SPEC = pl.BlockSpec(memory_space=pltpu.SEMAPHORE)
_ANY_SPEC = pl.BlockSpec(memory_space=pl.ANY)
_N_PEER = N_DEV - 1
SIBLING, SAME_CORE_NEIGHBOURS, OTHER_CORE_NEIGHBOURS, DIAGONAL = (1,), (2, 4), (3, 5), (6, 7)
PEER_ORDER = SIBLING + SAME_CORE_NEIGHBOURS + OTHER_CORE_NEIGHBOURS + DIAGONAL


def _peer_of(x, y, c, r):
    return (x ^ ((r >> 2) & 1), y ^ ((r >> 1) & 1), c ^ (r & 1))


class _Exchange:
    def __init__(self, name, part, slot):
        self.name, self.part, self.slot = name, part, slot

    def _copy(self, w, r, src, land, send_sems, recv_sems, sending):
        x, y, c = _my_place()
        peer = _peer_of(x, y, c, r)
        return pltpu.make_async_remote_copy(
            src_ref=self.part(w, src, _flat(*peer)),
            dst_ref=self.slot(w, land, _flat(x, y, c) if sending else _flat(*peer)),
            send_sem=send_sems.at[w * _N_PEER + r - 1], recv_sem=recv_sems.at[w * _N_PEER + r - 1],
            device_id=peer, device_id_type=MESH)

    def start(self, srcs, lands, after=None):
        n = len(srcs)
        n_after = 0 if after is None else 1

        def body(*refs):
            src, land = refs[:n], refs[n:2 * n]
            outs = refs[2 * n + n_after:]
            send_sems, recv_sems, local_sems, token = outs[0], outs[1], outs[2], outs[3 + 2 * n]
            for w in range(n):
                self._own_copy(w, src[w], land[w], local_sems).start()
                for r in PEER_ORDER:
                    self._copy(w, r, src[w], land[w], send_sems, recv_sems, True).start()
            token[...] = jnp.zeros_like(token)

        sems = pltpu.SemaphoreType.DMA((n * _N_PEER,))
        outs = pl.pallas_call(
            body, name=self.name + "_start",
            out_shape=(sems, sems, pltpu.SemaphoreType.DMA((n,)),
                       *[pltpu.HBM(t.shape, t.dtype) for t in list(srcs) + list(lands)],
                       jax.ShapeDtypeStruct((8, 128), F32)),
            in_specs=[_HBM_SPEC] * (2 * n) + [_ANY_SPEC] * n_after,
            out_specs=(_SEM_SPEC, _SEM_SPEC, _SEM_SPEC, *[_HBM_SPEC] * (2 * n), pl.BlockSpec(memory_space=pltpu.VMEM)),
            input_output_aliases={i: 3 + i for i in range(2 * n)},
            compiler_params=pltpu.CompilerParams(has_side_effects=pltpu.SideEffectType.DATAFLOW_SIDE_EFFECTING),
        )(*[pltpu.with_memory_space_constraint(t, pltpu.HBM) for t in list(srcs) + list(lands)],
          *([after] if n_after else []))
        return outs[0], outs[1], outs[2], outs[3:3 + n], outs[3 + n:3 + 2 * n], outs[3 + 2 * n]

    def _own_copy(self, w, src, land, local_sems):
        me = _flat(*_my_place())
        return pltpu.make_async_copy(self.part(w, src, me), self.slot(w, land, me), local_sems.at[w])

    def wait(self, started, after, peers=PEER_ORDER, own=True, tag=""):
        send_sems, recv_sems, local_sems, srcs, lands, token = started
        n = len(srcs)

        def body(*refs):
            src, land = refs[:n], refs[n:2 * n]
            s_sems, r_sems, l_sems = refs[2 * n], refs[2 * n + 1], refs[2 * n + 2]
            for w in range(n):
                if own:
                    self._own_copy(w, src[w], land[w], l_sems).wait()
                for r in peers:
                    cp = self._copy(w, r, src[w], land[w], s_sems, r_sems, False)
                    cp.wait_send()
                    cp.wait_recv()

        outs = pl.pallas_call(
            body, name=self.name + "_wait" + tag,
            out_shape=[pltpu.HBM(t.shape, t.dtype) for t in list(srcs) + list(lands)],
            in_specs=[_HBM_SPEC] * (2 * n) + [_SEM_SPEC, _SEM_SPEC, _SEM_SPEC, _ANY_SPEC],
            out_specs=[_HBM_SPEC] * (2 * n),
            input_output_aliases={i: i for i in range(2 * n)},
            compiler_params=pltpu.CompilerParams(has_side_effects=pltpu.SideEffectType.DATAFLOW_SIDE_EFFECTING),
        )(*srcs, *lands, send_sems, recv_sems, local_sems, after)
        return outs[n:], (send_sems, recv_sems, local_sems, outs[:n], outs[n:], token)


DIRECT_PEERS = (1, 2, 4, 6)
FORWARDED = (3, 5, 7)


class _TwoLevelGather:
    def __init__(self, name, axes, sizes):
        self.name, self.axes, self.sizes = name, axes, sizes

    def _place(self, w, land, dev):
        return _shard_slice(land, self.axes[w], dev, self.sizes[w])

    def _direct(self, w, r, src, land, sems, sending):
        x, y, c = _my_place()
        peer = _peer_of(x, y, c, r)
        k = w * len(DIRECT_PEERS) + DIRECT_PEERS.index(r)
        return pltpu.make_async_remote_copy(
            src_ref=src, dst_ref=self._place(w, land, _flat(x, y, c) if sending else _flat(*peer)),
            send_sem=sems[0].at[k], recv_sem=sems[1].at[k], device_id=peer, device_id_type=MESH)

    def _passed_on(self, w, f, land, sems, sending):
        x, y, c = _my_place()
        owner = _flat(*_peer_of(x, y, c, (f ^ 1) if sending else f))
        slot = self._place(w, land, owner)
        k = w * len(FORWARDED) + FORWARDED.index(f)
        return pltpu.make_async_remote_copy(
            src_ref=slot, dst_ref=slot, send_sem=sems[2].at[k], recv_sem=sems[3].at[k],
            device_id=(x, y, 1 - c), device_id_type=MESH)

    def _own(self, w, src, land, sems):
        return pltpu.make_async_copy(src, self._place(w, land, _flat(*_my_place())), sems[4].at[w])

    def _call(self, suffix, body, sems, srcs, lands, after, make_sems):
        n = len(srcs)
        n_after = 0 if after is None else 1
        bufs = list(srcs) + list(lands)

        def wrapped(*refs):
            ins = refs[:2 * n]
            rest = refs[2 * n + (n_after if make_sems else 0):]
            body(ins[:n], ins[n:], rest[:5], rest[-1] if make_sems else None)

        buf_shapes = [pltpu.HBM(t.shape, t.dtype) for t in bufs]
        if make_sems:
            sem_types = [pltpu.SemaphoreType.DMA((n * len(DIRECT_PEERS),))] * 2 \
                + [pltpu.SemaphoreType.DMA((n * len(FORWARDED),))] * 2 + [pltpu.SemaphoreType.DMA((n,))]
            outs = pl.pallas_call(
                wrapped, name=self.name + suffix,
                out_shape=(*sem_types, *buf_shapes, jax.ShapeDtypeStruct((8, 128), F32)),
                in_specs=[_HBM_SPEC] * (2 * n) + [_ANY_SPEC] * n_after,
                out_specs=(*[_SEM_SPEC] * 5, *[_HBM_SPEC] * (2 * n), pl.BlockSpec(memory_space=pltpu.VMEM)),
                input_output_aliases={i: 5 + i for i in range(2 * n)},
                compiler_params=pltpu.CompilerParams(has_side_effects=pltpu.SideEffectType.DATAFLOW_SIDE_EFFECTING),
            )(*[pltpu.with_memory_space_constraint(t, pltpu.HBM) for t in bufs], *([after] if n_after else []))
            return tuple(outs[:5]), outs[5:5 + n], outs[5 + n:5 + 2 * n], outs[5 + 2 * n]
        outs = pl.pallas_call(
            wrapped, name=self.name + suffix,
            out_shape=buf_shapes,
            in_specs=[_HBM_SPEC] * (2 * n) + [_SEM_SPEC] * 5 + [_ANY_SPEC] * n_after,
            out_specs=[_HBM_SPEC] * (2 * n),
            input_output_aliases={i: i for i in range(2 * n)},
            compiler_params=pltpu.CompilerParams(has_side_effects=pltpu.SideEffectType.DATAFLOW_SIDE_EFFECTING),
        )(*bufs, *sems, *([after] if n_after else []))
        return sems, outs[:n], outs[n:], None

    def start(self, srcs, lands, after=None):
        n = len(srcs)

        def body(src, land, sems, token):
            for w in range(n):
                self._own(w, src[w], land[w], sems).start()
                for r in DIRECT_PEERS:
                    self._direct(w, r, src[w], land[w], sems, True).start()
            token[...] = jnp.zeros_like(token)

        return self._call("_start", body, None, srcs, lands, after, True)

    def forward(self, state, after, which, tag=""):
        sems, srcs, lands, token = state
        n = len(srcs)

        def body(src, land, sem_refs, _):
            for w in range(n):
                for r in which:
                    self._direct(w, r, src[w], land[w], sem_refs, False).wait_recv()
                    self._passed_on(w, r | 1, land[w], sem_refs, True).start()

        sems, srcs, lands, _ = self._call("_forward" + tag, body, sems, srcs, lands, after, False)
        return sems, srcs, lands, token

    def wait(self, state, after, direct=(), passed_on=(), sends=False, tag=""):
        sems, srcs, lands, token = state
        n = len(srcs)

        def body(src, land, sem_refs, _):
            for w in range(n):
                for r in direct:
                    self._direct(w, r, src[w], land[w], sem_refs, False).wait_recv()
                for f in passed_on:
                    self._passed_on(w, f, land[w], sem_refs, False).wait_recv()
                if sends:
                    self._own(w, src[w], land[w], sem_refs).wait()
                    for r in DIRECT_PEERS:
                        self._direct(w, r, src[w], land[w], sem_refs, True).wait_send()
                    for f in FORWARDED:
                        self._passed_on(w, f, land[w], sem_refs, True).wait_send()

        sems, srcs, lands, _ = self._call("_wait" + tag, body, sems, srcs, lands, after, False)
        return lands, (sems, srcs, lands, token)


def _scatter_exchange(name, axes, shard_sizes):
    def part(w, src, dev):
        return src if axes[w] is None else _shard_slice(src, axes[w], dev, shard_sizes[w])
    return _Exchange(name, part, lambda w, land, dev: land.at[dev])


def _adamw(name, partials, w, m, v):
    R, C = w.shape
    tr = R
    while tr * C * 4 > (1 << 20) and tr % 16 == 0:
        tr //= 2

    def body(p_ref, w_ref, m_ref, v_ref, g_ref, d_ref, nm_ref, nv_ref):
        g = p_ref[0].astype(F32)
        for jdev in range(1, N_DEV):
            g = g + p_ref[jdev].astype(F32)
        nm = ADAM_B1 * m_ref[...] + (1.0 - ADAM_B1) * g
        nv = ADAM_B2 * v_ref[...] + (1.0 - ADAM_B2) * (g * g)
        m_hat = nm / (1.0 - ADAM_B1 ** ADAM_STEP)
        v_hat = nv / (1.0 - ADAM_B2 ** ADAM_STEP)
        g_ref[...] = g
        d_ref[...] = -ADAM_LR * (m_hat / (jnp.sqrt(v_hat) + ADAM_EPS) + ADAM_WD * w_ref[...])
        nm_ref[...] = nm
        nv_ref[...] = nv

    spec = pl.BlockSpec((tr, C), lambda i: (i, 0))
    return pl.pallas_call(
        body, name=name, grid=(R // tr,),
        in_specs=[pl.BlockSpec((N_DEV, tr, C), lambda i: (0, i, 0)), spec, spec, spec],
        out_specs=[spec] * 4,
        out_shape=[jax.ShapeDtypeStruct((R, C), F32)] * 4,
        compiler_params=pltpu.CompilerParams(dimension_semantics=("parallel",),
                                             vmem_limit_bytes=_vmem_limit(24 << 20)),
    )(partials, w, m, v)


def _local_step(x, cos2, sin_fwd, sin_bwd, project_in, mix_weights, ffn_weights, pool_scale, g_mix, b_mix, g_ff, b_ff,
                target, send):
    S, D = x.shape
    aw = pw = D // 2
    u_col_block = 3
    gate_col0 = 4 * aw

    xb = x.astype(BF16)
    h, w_in = project_in(xb)
    dilated = [d for d in DILATIONS if d > 1]
    qkv = {1: h[None], **dict(zip(dilated, _to_residue_major("qkv_to_rm", h, 0, 3 * aw)))}
    fwd = [_attn_fwd(qkv[d], d, aw) for d in DILATIONS]
    o_attn, l_tot = _attn_combine([f[0] for f in fwd], [f[1] for f in fwd], aw)
    w_pool, w_ba, w_bp, w_out = mix_weights(o_attn)
    p, y_pre, pm = _pool_fwd(h, w_pool, pool_scale, pw, u_col_block)
    y_attn = _branch_attn(o_attn, w_ba)
    y_pool, merged = _branch_pool_merge(pm, w_bp, h, y_attn, gate_col0)
    w1, w2 = ffn_weights(merged)
    x1, x1b, xhat1, rstd1 = _out_proj_ln(merged, w_out, x, g_mix, b_mix)
    a = _ffn_up(x1b, w1)
    z2 = _residual_matmul("ffn_down", a, w2, "nn", x1)
    dz2, dz2b, dg_ff, db_ff, loss = _ln_loss_bwd(z2, g_ff, b_ff, target)

    tok = send("ff2", [_grad_weight("grad_w_ff2", a, dz2b)])
    dh1 = _ffn_down_bwd(dz2b, w2, a, after=tok)
    tok = send("ff1", [_grad_weight("grad_w_ff1", x1b, dh1)])
    dy1 = _residual_matmul("ffn_up_bwd", dh1, w1, "nt", dz2, after=tok)
    dz1, dz1b, dg_mix, db_mix = _ln_bwd(dy1, xhat1, rstd1, g_mix)
    tok = send("out", [_grad_weight("grad_w_out", merged, dz1b)])
    dy_attn, dy_pool, dh = _out_proj_bwd(dz1b, w_out, h, y_attn, y_pool, gate_col0, after=tok)
    tok = send("branch", [_grad_weight("grad_w_branch_attn", o_attn, dy_attn),
                          _grad_weight("grad_w_branch_pool", pm, dy_pool)])
    d_outs, statss = _branch_attn_bwd(dy_attn, w_ba, o_attn, l_tot, after=tok)
    dy_pre, d_scale = _branch_pool_bwd(dy_pool, w_bp, y_pre, pool_scale)
    dh, dw_pool = _pool_bwd(dh, dy_pre, p, w_pool, pw, u_col_block)
    per_pattern = [_attn_bwd(qkv[d], d_outs[d], statss[d], d, aw) for d in DILATIONS]
    dh = _attn_bwd_finish(dh, per_pattern, cos2, sin_bwd, aw)
    small = jnp.concatenate((d_scale, dg_mix, db_mix, dg_ff, db_ff), axis=-1)
    tok = send("in", [_grad_weight("grad_w_in", xb, dh), dw_pool.astype(BF16),
                      small.reshape(small.shape[-1] // HEAD_DIM, HEAD_DIM)])
    grad_x = _residual_matmul("in_proj_bwd", dh, w_in, "nt", dz1, after=tok)
    return loss, grad_x


def _rope_tables(positions):
    half = HEAD_DIM // 2
    inv_freq = ROPE_THETA ** (-jnp.arange(half, dtype=F32) / half)
    ang = positions.astype(F32)[:, None] * inv_freq
    cos, sin = jnp.cos(ang), jnp.sin(ang)
    cos2 = jnp.concatenate([cos, cos], axis=-1)
    sin_fwd = jnp.concatenate([-sin, sin], axis=-1)
    return cos2, sin_fwd, -sin_fwd


def kernel(x, positions, w_in, w_pool, pool_scale, w_branch_attn, w_branch_pool, w_out, ln_mix_g, ln_mix_b, w_ff1, w_ff2, ln_ff_g, ln_ff_b, loss_target, m_w_in, m_w_pool, m_pool_scale, m_w_branch_attn, m_w_branch_pool, m_w_out, m_ln_mix_g, m_ln_mix_b, m_w_ff1, m_w_ff2, m_ln_ff_g, m_ln_ff_b, v_w_in, v_w_pool, v_pool_scale, v_w_branch_attn, v_w_branch_pool, v_w_out, v_ln_mix_g, v_ln_mix_b, v_w_ff1, v_w_ff2, v_ln_ff_g, v_ln_ff_b):
    big_w = (w_in[0], w_pool[0], w_branch_attn[0], w_branch_pool[0], w_out[0], w_ff1[0], w_ff2[0])
    big_m = (m_w_in[0], m_w_pool[0], m_w_branch_attn[0], m_w_branch_pool[0], m_w_out[0], m_w_ff1[0], m_w_ff2[0])
    big_v = (v_w_in[0], v_w_pool[0], v_w_branch_attn[0], v_w_branch_pool[0], v_w_out[0], v_w_ff1[0], v_w_ff2[0])
    shard_axes = (1, 1, 1, 1, 0, 1, 0)
    small_w = (pool_scale, ln_mix_g, ln_mix_b, ln_ff_g, ln_ff_b)
    small_m = (m_pool_scale, m_ln_mix_g, m_ln_mix_b, m_ln_ff_g, m_ln_ff_b)
    small_v = (v_pool_scale, v_ln_mix_g, v_ln_mix_b, v_ln_ff_g, v_ln_ff_b)

    names = ("w_in", "w_pool", "w_branch_attn", "w_branch_pool", "w_out", "w_ff1", "w_ff2")
    axis_of = dict(zip(names, shard_axes))
    shard_of = dict(zip(names, [w.astype(BF16) for w in big_w]))

    def full_buffer(n):
        s, ax = shard_of[n], axis_of[n]
        full = list(s.shape)
        full[ax] *= N_DEV
        return lax.empty(tuple(full), s.dtype)

    def gather_group(tag, group, after):
        ex = _TwoLevelGather(tag, [axis_of[n] for n in group], [shard_of[n].shape[axis_of[n]] for n in group])
        return ex, ex.start([shard_of[n] for n in group], [full_buffer(n) for n in group], after)

    in_ex, in_state = gather_group("gather_in", ("w_in",), None)
    mix_ex, mix_state = gather_group("gather_mix", ("w_pool", "w_branch_attn", "w_branch_pool", "w_out"),
                                     in_state[-1])
    ffn_ex, ffn_state = gather_group("gather_ffn", ("w_ff1", "w_ff2"), mix_state[-1])
    states = {"mix": mix_state, "ffn": ffn_state}
    me = 4 * lax.axis_index("x") + 2 * lax.axis_index("y") + lax.axis_index("c")
    block_cols = shard_of["w_in"].shape[1]
    neighbours, diagonal = (2, 4), (6,)

    def project_in(xb):
        n_rope_blocks = 2 * (x.shape[-1] // 2) // block_cols

        def piece(tag, w, blocks, h, **kw):
            return _in_proj_piece("in_proj_" + tag, xb, w, jnp.stack(blocks).astype(jnp.int32), cos2, sin_fwd, h,
                                  block_cols, n_rope_blocks, **kw)

        h = piece("own", shard_of["w_in"], [me], None, own_shard=True, after=ffn_state[-1])
        (w_in_land,), state = in_ex.wait(in_state, h, direct=(1,), tag="_sibling")
        h = piece("sibling", w_in_land, [me ^ 1], h)
        state = in_ex.forward(state, h, neighbours, tag="_neighbours")
        h = piece("neighbours", state[2][0], [me ^ r for r in neighbours], h)
        state = in_ex.forward(state, h, diagonal, tag="_diagonal")
        h = piece("diagonal", state[2][0], [me ^ r for r in diagonal], h)
        states["mix"] = mix_ex.forward(states["mix"], h, neighbours + diagonal)
        (w_in_land,), _ = in_ex.wait(state, h, passed_on=FORWARDED, sends=True, tag="_passed_on")
        h = piece("passed_on", w_in_land, [me ^ f for f in FORWARDED], h)
        return h, w_in_land

    def mix_weights(after):
        states["ffn"] = ffn_ex.forward(states["ffn"], after, neighbours + diagonal)
        return mix_ex.wait(states["mix"], after, direct=(1,), passed_on=FORWARDED, sends=True)[0]

    def ffn_weights(after):
        return ffn_ex.wait(states["ffn"], after, direct=(1,), passed_on=FORWARDED, sends=True)[0]

    groups = {"ff2": ("w_ff2",), "ff1": ("w_ff1",), "out": ("w_out",),
              "branch": ("w_branch_attn", "w_branch_pool"), "in": ("w_in", "w_pool", "small")}
    sent = {}

    def send(key, grads_):
        axes = [axis_of.get(n) for n in groups[key]]
        sizes = [None if ax is None else g.shape[ax] // N_DEV for g, ax in zip(grads_, axes)]
        lands = []
        for g, ax, size in zip(grads_, axes, sizes):
            shard = list(g.shape)
            if ax is not None:
                shard[ax] = size
            lands.append(lax.empty((N_DEV, *shard), g.dtype))
        ex = _scatter_exchange("scatter_" + key, axes, sizes)
        sent[key] = (ex, ex.start(list(grads_), lands))
        return sent[key][1][-1]

    cos2, sin_fwd, sin_bwd = _rope_tables(positions[0])
    loss, grad_x = _local_step(
        x[0], cos2, sin_fwd, sin_bwd, project_in, mix_weights, ffn_weights, pool_scale, ln_mix_g, ln_mix_b,
        ln_ff_g, ln_ff_b, loss_target[0], send)

    state = dict(zip(names, zip(big_w, big_m, big_v)))
    n_small = sum(w.shape[-1] for w in small_w)
    small_2d = (n_small // HEAD_DIM, HEAD_DIM)
    state["small"] = tuple(jnp.concatenate(t, axis=-1).reshape(small_2d) for t in (small_w, small_m, small_v))
    grads, deltas, new_ms, new_vs = {}, {}, {}, {}
    after = grad_x
    for key in ("ff2", "ff1", "out", "branch", "in"):
        ex, started = sent[key]
        for n, part in zip(groups[key], ex.wait(started, after)[0]):
            w, m, v = state[n]
            r2 = (-1, w.shape[-1])
            w2d = w.reshape(r2)
            res = _adamw("adamw_" + n, part.reshape((N_DEV,) + w2d.shape), w2d, m.reshape(r2), v.reshape(r2))
            after = res[0]
            if n == "small":
                small_out = [t.reshape(1, n_small) for t in res]
            else:
                grads[n], deltas[n], new_ms[n], new_vs[n] = (t.reshape((1,) + w.shape) for t in res)
    small_names = ("pool_scale", "ln_mix_g", "ln_mix_b", "ln_ff_g", "ln_ff_b")
    off = 0
    for n, w in zip(small_names, small_w):
        width = w.shape[-1]
        grads[n], deltas[n], new_ms[n], new_vs[n] = (t[:, off:off + width] for t in small_out)
        off += width

    order = ("w_in", "w_pool", "pool_scale", "w_branch_attn", "w_branch_pool", "w_out", "ln_mix_g", "ln_mix_b",
             "w_ff1", "w_ff2", "ln_ff_g", "ln_ff_b")
    total_loss = lax.psum(loss[0, 0], ("x", "y", "c"))
    return (total_loss, grad_x[None], *[grads[n] for n in order], *[deltas[n] for n in order],
            *[new_ms[n] for n in order], *[new_vs[n] for n in order])
```

```python
import functools

import jax
import jax.numpy as jnp
from jax import lax
from jax.experimental import pallas as pl
from jax.experimental.pallas import tpu as pltpu

F32 = jnp.float32
BF16 = jnp.bfloat16

N_DEV = 8
HEAD_DIM = 128
SUB_BLOCK = 128
DILATIONS = (1, 4, 16)
POOL_WINDOWS = (2, 4, 8, 16)
POOL_HALO = 128
PERM_ROWS = 256
K_TILE = 1024
LN_EPS = 1e-5
DEEPNORM_ALPHA = 2.0 ** 0.25
ROPE_THETA = 10000.0
ATTN_SCALE = HEAD_DIM ** -0.5
ADAM_LR, ADAM_B1, ADAM_B2, ADAM_EPS, ADAM_WD, ADAM_STEP = 0.001, 0.9, 0.999, 1e-08, 0.01, 10
NEG_BIG = -1e30
VMEM_CAP_V7X = 64 * 1024 * 1024
MESH = pl.DeviceIdType.MESH


def _vmem_limit(est_bytes):
    return int(min(max(est_bytes * 5 // 4 + (4 << 20), 16 << 20), VMEM_CAP_V7X - (6 << 20)))


def _nbytes(shape, dtype):
    n = 1
    for s in shape:
        n *= s
    return n * jnp.dtype(dtype).itemsize


def _mm(name, a, b, form, tiles, outs, epi, extras=(), sequential=False, after=None):
    tm, tn, tk = tiles
    if form == "nn":
        (M, K), (K2, N) = a.shape, b.shape
    elif form == "nt":
        (M, K), (N, K2) = a.shape, b.shape
    else:
        (K, M), (K2, N) = a.shape, b.shape
    assert K == K2, (name, a.shape, b.shape)
    tm, tn, tk = min(tm, M), min(tn, N), min(tk, K)
    assert M % tm == 0 and N % tn == 0 and K % tk == 0, (name, M, N, K, tm, tn, tk)
    grid = (M // tm, N // tn, K // tk)
    nk = grid[2]
    if form == "nn":
        a_spec = pl.BlockSpec((tm, tk), lambda i, j, k: (i, k))
        b_spec = pl.BlockSpec((tk, tn), lambda i, j, k: (k, j))
        contract = ((1,), (0,))
    elif form == "nt":
        a_spec = pl.BlockSpec((tm, tk), lambda i, j, k: (i, k))
        b_spec = pl.BlockSpec((tn, tk), lambda i, j, k: (j, k))
        contract = ((1,), (1,))
    else:
        a_spec = pl.BlockSpec((tk, tm), lambda i, j, k: (k, i))
        b_spec = pl.BlockSpec((tk, tn), lambda i, j, k: (k, j))
        contract = ((0,), (0,))
    n_ex, n_out = len(extras), len(outs)
    n_after = 0 if after is None else 1

    def body(a_ref, b_ref, *rest):
        ex_refs = rest[:n_ex]
        rest = rest[n_ex + n_after:]
        out_refs = rest[:n_out]
        i, j, k = pl.program_id(0), pl.program_id(1), pl.program_id(2)

        def prod():
            return lax.dot_general(a_ref[...].astype(BF16), b_ref[...].astype(BF16),
                                   (contract, ((), ())), preferred_element_type=F32)

        if nk == 1:
            epi(prod(), ex_refs, out_refs, i, j)
        else:
            acc = rest[n_out]

            @pl.when(k == 0)
            def _():
                acc[...] = prod()

            @pl.when(jnp.logical_and(k > 0, k < nk - 1))
            def _():
                acc[...] += prod()

            @pl.when(k == nk - 1)
            def _():
                epi(acc[...] + prod(), ex_refs, out_refs, i, j)

    est = 2 * (_nbytes(a_spec.block_shape, a.dtype) + _nbytes(b_spec.block_shape, b.dtype))
    est += sum(2 * _nbytes(bs, arr.dtype) for arr, bs, _ in extras)
    est += sum(2 * _nbytes(bs, dt) for _, dt, bs, _ in outs)
    est += 4 * tm * tn * 4
    sem = ("arbitrary",) * 3 if sequential else ("parallel", "parallel", "arbitrary")
    return pl.pallas_call(
        body, name=name, grid=grid,
        in_specs=([a_spec, b_spec] + [pl.BlockSpec(bs, im) for _, bs, im in extras]
                  + [pl.BlockSpec(memory_space=pl.ANY)] * n_after),
        out_specs=[pl.BlockSpec(bs, im) for _, _, bs, im in outs],
        out_shape=[jax.ShapeDtypeStruct(sh, dt) for sh, dt, _, _ in outs],
        scratch_shapes=[pltpu.VMEM((tm, tn), F32)] if nk > 1 else [],
        compiler_params=pltpu.CompilerParams(dimension_semantics=sem, vmem_limit_bytes=_vmem_limit(est)),
    )(a, b, *[arr for arr, _, _ in extras], *([after] if n_after else []))


def _tile_out(shape, dtype, tm, tn):
    return (shape, dtype, (tm, tn), lambda i, j, k: (i, j))


def _row_sum_out(width):
    return ((1, width), F32, (1, width), lambda i, j, k: (0, 0))


def _accumulate_rows(ref, value, i):
    @pl.when(i == 0)
    def _():
        ref[...] = value

    @pl.when(i > 0)
    def _():
        ref[...] += value


def _layer_norm_bwd(dy, xhat, rstd, g):
    dxh = dy * g
    m1 = jnp.mean(dxh, axis=-1, keepdims=True)
    m2 = jnp.mean(dxh * xhat, axis=-1, keepdims=True)
    return rstd * (dxh - m1 - xhat * m2)


def _rope_apply(t, cos2, sin_signed):
    return t * cos2 + pltpu.roll(t, HEAD_DIM // 2, axis=1) * sin_signed


def _in_proj_piece(name, xb, w_in, col_blocks, cos2, sin_fwd, h_so_far, block_cols, n_rope_blocks, own_shard=False,
                   after=None):
    S, D = xb.shape
    W = w_in.shape[1] * (N_DEV if own_shard else 1)
    tm = min(1024, S)
    n_blocks = col_blocks.shape[0]

    def body(cols_ref, x_ref, w_ref, cos_ref, sin_ref, *rest):
        h_ref = rest[-1]
        j = pl.program_id(1)

        @pl.when(cols_ref[j] < n_rope_blocks)
        def _():
            acc = _dot_nn(x_ref[...], w_ref[...])
            c, s = cos_ref[...], sin_ref[...]
            for hd in range(block_cols // HEAD_DIM):
                sl = slice(hd * HEAD_DIM, (hd + 1) * HEAD_DIM)
                h_ref[:, sl] = _rope_apply(acc[:, sl], c, s).astype(BF16)

        @pl.when(cols_ref[j] >= n_rope_blocks)
        def _():
            h_ref[...] = _dot_nn(x_ref[...], w_ref[...]).astype(BF16)

    row = pl.BlockSpec((tm, HEAD_DIM), lambda i, j, cols: (i, 0))
    carried = ([] if h_so_far is None else [h_so_far]) + ([] if after is None else [after])
    est = 2 * (tm * D * 2 + D * block_cols * 2 + tm * block_cols * 2 + 2 * tm * HEAD_DIM * 4) + 3 * tm * block_cols * 4
    return pl.pallas_call(
        body, name=name,
        grid_spec=pltpu.PrefetchScalarGridSpec(
            num_scalar_prefetch=1, grid=(S // tm, n_blocks),
            in_specs=[pl.BlockSpec((tm, D), lambda i, j, cols: (i, 0)),
                      pl.BlockSpec((D, block_cols), lambda i, j, cols: (0, 0 if own_shard else cols[j])), row, row]
                     + [pl.BlockSpec(memory_space=pl.ANY)] * len(carried),
            out_specs=pl.BlockSpec((tm, block_cols), lambda i, j, cols: (i, cols[j]))),
        out_shape=jax.ShapeDtypeStruct((S, W), BF16),
        input_output_aliases={} if h_so_far is None else {5: 0},
        compiler_params=pltpu.CompilerParams(dimension_semantics=("parallel", "arbitrary"),
                                             vmem_limit_bytes=_vmem_limit(est)),
    )(col_blocks, xb, w_in, cos2, sin_fwd, *carried)


def _attn_mask(mb):
    qi = lax.broadcasted_iota(jnp.int32, (SUB_BLOCK, 2 * SUB_BLOCK), 0)
    kj = lax.broadcasted_iota(jnp.int32, (SUB_BLOCK, 2 * SUB_BLOCK), 1)
    prev = jnp.logical_and(jnp.logical_and(kj < SUB_BLOCK, kj >= qi), mb > 0)
    cur = jnp.logical_and(kj >= SUB_BLOCK, kj - SUB_BLOCK <= qi)
    return jnp.logical_or(prev, cur)


def _both_blocks(prev_ref, cur_ref, sl):
    return jnp.concatenate([prev_ref[:, sl], cur_ref[:, sl]], axis=0)


def _dot_nt(a, b):
    return lax.dot_general(a, b, (((1,), (1,)), ((), ())), preferred_element_type=F32)


def _dot_tn(a, b):
    return lax.dot_general(a, b, (((0,), (0,)), ((), ())), preferred_element_type=F32)


def _dot_nn(a, b):
    return lax.dot_general(a, b, (((1,), (0,)), ((), ())), preferred_element_type=F32)


def _perm_matrix(d, to_residue_major):
    g = PERM_ROWS // d
    i = lax.broadcasted_iota(jnp.int32, (PERM_ROWS, PERM_ROWS), 0)
    j = lax.broadcasted_iota(jnp.int32, (PERM_ROWS, PERM_ROWS), 1)
    if to_residue_major:
        hit = j == (i % g) * d + i // g
    else:
        hit = j == (i % d) * g + i // d
    return hit.astype(BF16)


def _permute_rows(perm, x):
    if x.dtype == BF16:
        return _dot_nn(perm, x)
    hi = x.astype(BF16)
    r1 = x - hi.astype(F32)
    mid = r1.astype(BF16)
    lo = (r1 - mid.astype(F32)).astype(BF16)
    return (_dot_nn(perm, hi) + _dot_nn(perm, mid)) + _dot_nn(perm, lo)


def _rm_block(d, width):
    return pl.BlockSpec((d, PERM_ROWS // d, width), lambda i: (0, i, 0))


def _to_residue_major(name, x, col_block, width):
    S = x.shape[0]
    dils = [d for d in DILATIONS if d > 1]
    chunk = min(width, 1024)
    tiles_per_step = 2
    rows = tiles_per_step * PERM_ROWS

    def body(x_ref, *out_refs):
        for d, o_ref in zip(dils, out_refs):
            perm = _perm_matrix(d, True)
            g = PERM_ROWS // d
            for part in range(tiles_per_step):
                for c0 in range(0, width, chunk):
                    cw = min(chunk, width - c0)
                    y = _permute_rows(perm, x_ref[part * PERM_ROWS:(part + 1) * PERM_ROWS, c0:c0 + cw])
                    o_ref[:, part * g:(part + 1) * g, c0:c0 + cw] = y.astype(x.dtype).reshape(d, g, cw)

    return pl.pallas_call(
        body, name=name, grid=(S // rows,),
        in_specs=[pl.BlockSpec((rows, width), lambda i: (i, col_block))],
        out_specs=[pl.BlockSpec((d, rows // d, width), lambda i: (0, i, 0)) for d in dils],
        out_shape=[jax.ShapeDtypeStruct((d, S // d, width), x.dtype) for d in dils],
        compiler_params=pltpu.CompilerParams(dimension_semantics=("parallel",),
                                             vmem_limit_bytes=_vmem_limit(32 << 20)),
    )(x)


def _put_column(tile, col, value):
    lane = lax.broadcasted_iota(jnp.int32, tile.shape, 1)
    return jnp.where(lane == col, value, tile)


def _attn_fwd(qkv, d, aw):
    _, rows, _ = qkv.shape
    n_heads = aw // HEAD_DIM
    nb = rows // SUB_BLOCK

    def body(q_ref, kc_ref, vc_ref, o_ref, lse_ref, kp_ref, vp_ref, s_buf, p_buf):
        step = pl.program_id(0)

        @pl.when(step == 0)
        def _():
            kp_ref[...] = jnp.zeros_like(kp_ref)
            vp_ref[...] = jnp.zeros_like(vp_ref)

        mask = _attn_mask(step % nb)
        for hd in range(n_heads):
            sl = slice(hd * HEAD_DIM, (hd + 1) * HEAD_DIM)
            s_buf[hd] = _dot_nt(q_ref[:, sl], _both_blocks(kp_ref, kc_ref, sl))
        lse_tile = jnp.zeros((SUB_BLOCK, HEAD_DIM), F32)
        inv_tile = jnp.zeros((SUB_BLOCK, HEAD_DIM), F32)
        for hd in range(n_heads):
            s = jnp.where(mask, s_buf[hd] * ATTN_SCALE, NEG_BIG)
            m = jnp.max(s, axis=-1, keepdims=True)
            p = jnp.exp(s - m)
            l = jnp.sum(p, axis=-1, keepdims=True)
            p_buf[hd] = p.astype(BF16)
            lse_tile = _put_column(lse_tile, hd, m + jnp.log(l))
            inv_tile = _put_column(inv_tile, hd, 1.0 / l)
        lse_ref[...] = lse_tile
        for hd in range(n_heads):
            sl = slice(hd * HEAD_DIM, (hd + 1) * HEAD_DIM)
            o = _dot_nn(p_buf[hd], _both_blocks(vp_ref, vc_ref, sl))
            o_ref[:, sl] = (o * inv_tile[:, hd:hd + 1]).astype(BF16)
        kp_ref[...] = kc_ref[...]
        vp_ref[...] = vc_ref[...]

    def block(col, width):
        return pl.BlockSpec((SUB_BLOCK, width), lambda s: (s, col))

    qkv2 = qkv.reshape(d * rows, qkv.shape[-1])
    o, lse = pl.pallas_call(
        body, name=f"attn_fwd_d{d}", grid=(d * nb,),
        in_specs=[block(0, aw), block(1, aw), block(2, aw)],
        out_specs=[block(0, aw), block(0, HEAD_DIM)],
        out_shape=[jax.ShapeDtypeStruct((d * rows, aw), BF16), jax.ShapeDtypeStruct((d * rows, HEAD_DIM), F32)],
        scratch_shapes=[pltpu.VMEM((SUB_BLOCK, aw), BF16), pltpu.VMEM((SUB_BLOCK, aw), BF16),
                        pltpu.VMEM((n_heads, SUB_BLOCK, 2 * SUB_BLOCK), F32),
                        pltpu.VMEM((n_heads, SUB_BLOCK, 2 * SUB_BLOCK), BF16)],
        compiler_params=pltpu.CompilerParams(dimension_semantics=("arbitrary",),
                                             vmem_limit_bytes=_vmem_limit(16 << 20)),
    )(qkv2, qkv2, qkv2)
    return o.reshape(d, rows, aw), lse.reshape(d, rows, HEAD_DIM)


def _attn_combine(outs, lses, aw):
    S = outs[0].shape[1]
    n_heads = aw // HEAD_DIM
    n_pat = len(DILATIONS)

    def body(*refs):
        o_refs, l_refs = refs[:n_pat], refs[n_pat:2 * n_pat]
        o_ref, lt_ref = refs[2 * n_pat], refs[2 * n_pat + 1]
        o_nat, l_nat = [], []
        for d, o_r, l_r in zip(DILATIONS, o_refs, l_refs):
            o_p = o_r[...].reshape(PERM_ROWS, aw)
            l_p = l_r[...].reshape(PERM_ROWS, HEAD_DIM)
            if d > 1:
                perm = _perm_matrix(d, False)
                o_p, l_p = _permute_rows(perm, o_p), _permute_rows(perm, l_p)
            o_nat.append(o_p)
            l_nat.append(l_p)
        mx = functools.reduce(jnp.maximum, l_nat)
        es = [jnp.exp(l_p - mx) for l_p in l_nat]
        den = functools.reduce(jnp.add, es)
        lt_ref[...] = mx + jnp.log(den)
        ws = [e / den for e in es]
        for hd in range(n_heads):
            sl = slice(hd * HEAD_DIM, (hd + 1) * HEAD_DIM)
            o = ws[0][:, hd:hd + 1] * o_nat[0][:, sl]
            for pi in range(1, n_pat):
                o = o + ws[pi][:, hd:hd + 1] * o_nat[pi][:, sl]
            o_ref[:, sl] = o.astype(BF16)

    return pl.pallas_call(
        body, name="attn_combine", grid=(S // PERM_ROWS,),
        in_specs=[_rm_block(d, aw) for d in DILATIONS] + [_rm_block(d, HEAD_DIM) for d in DILATIONS],
        out_specs=[pl.BlockSpec((PERM_ROWS, aw), lambda i: (i, 0)), pl.BlockSpec((PERM_ROWS, HEAD_DIM), lambda i: (i, 0))],
        out_shape=[jax.ShapeDtypeStruct((S, aw), BF16), jax.ShapeDtypeStruct((S, HEAD_DIM), F32)],
        compiler_params=pltpu.CompilerParams(dimension_semantics=("parallel",),
                                             vmem_limit_bytes=_vmem_limit(40 << 20)),
    )(*outs, *lses)


def _band(tm, width, w, row_offset, transpose):
    t = lax.broadcasted_iota(jnp.int32, (tm, width), 0)
    u = lax.broadcasted_iota(jnp.int32, (tm, width), 1)
    dist = (u - t - row_offset) if transpose else (t + row_offset - u)
    return jnp.logical_and(dist >= 0, dist < w).astype(BF16)


def _pool_fwd(h, w_pool, pool_scale, pw, u_col_block):
    S, W = h.shape
    n_groups = len(POOL_WINDOWS)
    gw = pw // n_groups
    tm = min(512, S)
    halo_per_tile = tm // POOL_HALO

    def body(uc_ref, uh_ref, w_ref, sc_ref, p_ref, y_ref, pm_ref):
        i = pl.program_id(0)
        t_abs = i * tm + lax.broadcasted_iota(jnp.int32, (tm, 1), 0)
        for g, w in enumerate(POOL_WINDOWS):
            sl = slice(g * gw, (g + 1) * gw)
            uc = uc_ref[:, sl]
            uh = jnp.where(i > 0, uh_ref[:, sl], jnp.zeros((POOL_HALO, gw), BF16))
            ssum = _dot_nn(_band(tm, tm, w, 0, False), uc) + _dot_nn(_band(tm, POOL_HALO, w, POOL_HALO, False), uh)
            cnt = jnp.minimum(t_abs + 1, w).astype(F32)
            p = (ssum / cnt - uc.astype(F32)).astype(BF16)
            y = _dot_nn(p, w_ref[g])
            p_ref[:, sl] = p
            y_ref[:, sl] = y.astype(BF16)
            pm_ref[:, sl] = (y * sc_ref[:, sl]).astype(BF16)

    row = pl.BlockSpec((tm, pw), lambda i: (i, 0))
    return pl.pallas_call(
        body, name="pool_fwd", grid=(S // tm,),
        in_specs=[pl.BlockSpec((tm, pw), lambda i: (i, u_col_block)),
                  pl.BlockSpec((POOL_HALO, pw), lambda i: (jnp.maximum(i * halo_per_tile - 1, 0), u_col_block)),
                  pl.BlockSpec((n_groups, gw, gw), lambda i: (0, 0, 0)),
                  pl.BlockSpec((1, pw), lambda i: (0, 0))],
        out_specs=[row, row, row],
        out_shape=[jax.ShapeDtypeStruct((S, pw), BF16)] * 3,
        compiler_params=pltpu.CompilerParams(dimension_semantics=("parallel",),
                                             vmem_limit_bytes=_vmem_limit(24 << 20)),
    )(h, h, w_pool, pool_scale)


def _branch_attn(o_attn, w_ba):
    S, _ = o_attn.shape
    D = w_ba.shape[1]
    tm, tn = min(1024, S), D

    def epi(acc, ex, out, i, j):
        out[0][...] = acc.astype(BF16)

    (y,) = _mm("branch_attn", o_attn, w_ba, "nn", (tm, tn, 1024), [_tile_out((S, D), BF16, tm, tn)], epi)
    return y


def _branch_pool_merge(pm, w_bp, h, y_attn, gate_col0):
    S, _ = pm.shape
    D = w_bp.shape[1]
    tm, tn = min(512, S), D
    ga0, gp0 = gate_col0 // tn, (gate_col0 + D) // tn

    def epi(acc, ex, out, i, j):
        ga_ref, gp_ref, ya_ref = ex
        yp_ref, mg_ref = out
        yp = acc.astype(BF16)
        yp_ref[...] = yp
        mg = (jax.nn.sigmoid(ga_ref[...]).astype(F32) * ya_ref[...].astype(F32)
              + jax.nn.sigmoid(gp_ref[...]).astype(F32) * acc)
        mg_ref[...] = mg.astype(BF16)

    y_pool, merged = _mm(
        "branch_pool_merge", pm, w_bp, "nn", (tm, tn, 1024),
        [_tile_out((S, D), BF16, tm, tn), _tile_out((S, D), BF16, tm, tn)], epi,
        extras=[(h, (tm, tn), lambda i, j, k: (i, ga0 + j)), (h, (tm, tn), lambda i, j, k: (i, gp0 + j)),
                (y_attn, (tm, tn), lambda i, j, k: (i, j))])
    return y_pool, merged


def _layer_norm_rows(z, g, b):
    mu = jnp.mean(z, axis=-1, keepdims=True)
    zc = z - mu
    var = jnp.mean(zc * zc, axis=-1, keepdims=True)
    rstd = lax.rsqrt(var + LN_EPS)
    xhat = zc * rstd
    return xhat * g + b, xhat, rstd


def _out_proj_ln(merged, w_out, x, g, b):
    S, D = x.shape
    tm = min(256, S)

    def epi(acc, ex, out, i, j):
        x_ref, g_ref, b_ref = ex
        x1_ref, x1b_ref, xh_ref, rs_ref = out
        y, xhat, rstd = _layer_norm_rows(DEEPNORM_ALPHA * x_ref[...] + acc, g_ref[...], b_ref[...])
        x1_ref[...] = y
        x1b_ref[...] = y.astype(BF16)
        xh_ref[...] = xhat
        rs_ref[...] = jnp.broadcast_to(rstd, (tm, HEAD_DIM))

    row = lambda i, j, k: (i, 0)
    vec = lambda i, j, k: (0, 0)
    return _mm("out_proj_ln", merged, w_out, "nn", (tm, D, D),
               [((S, D), F32, (tm, D), row), ((S, D), BF16, (tm, D), row), ((S, D), F32, (tm, D), row),
                ((S, HEAD_DIM), F32, (tm, HEAD_DIM), row)], epi,
               extras=[(x, (tm, D), row), (g, (1, D), vec), (b, (1, D), vec)])


def _ffn_up(x1b, w1):
    S, D = x1b.shape
    F = w1.shape[1]
    tm, tn = min(1024, S), min(2048, F)

    def epi(acc, ex, out, i, j):
        r = jnp.maximum(acc, 0.0)
        out[0][...] = (r * r).astype(BF16)

    (a,) = _mm("ffn_up", x1b, w1, "nn", (tm, tn, 2 * K_TILE), [_tile_out((S, F), BF16, tm, tn)], epi)
    return a


def _residual_matmul(name, a, w, form, resid, after=None):
    S, D = resid.shape
    tm, tn = min(1024, S), min(2048, D)

    def epi(acc, ex, out, i, j):
        out[0][...] = DEEPNORM_ALPHA * ex[0][...] + acc

    (z,) = _mm(name, a, w, form, (tm, tn, K_TILE), [_tile_out((S, D), F32, tm, tn)], epi,
               extras=[(resid, (tm, tn), lambda i, j, k: (i, j))], after=after)
    return z


def _row_kernel(name, body, row_inputs, vec_inputs, row_outputs, sum_widths, tr):
    S = row_inputs[0].shape[0]
    row = lambda w: pl.BlockSpec((tr, w), lambda i: (i, 0))
    vec = lambda w: pl.BlockSpec((1, w), lambda i: (0, 0))

    def wrapped(*refs):
        body(pl.program_id(0), *refs)

    return pl.pallas_call(
        wrapped, name=name, grid=(S // tr,),
        in_specs=[row(t.shape[1]) for t in row_inputs] + [vec(t.shape[1]) for t in vec_inputs],
        out_specs=[row(w) for w, _ in row_outputs] + [vec(w) for w in sum_widths],
        out_shape=([jax.ShapeDtypeStruct((S, w), dt) for w, dt in row_outputs]
                   + [jax.ShapeDtypeStruct((1, w), F32) for w in sum_widths]),
        compiler_params=pltpu.CompilerParams(dimension_semantics=("arbitrary",),
                                             vmem_limit_bytes=_vmem_limit(40 << 20)),
    )(*row_inputs, *vec_inputs)


def _ln_loss_bwd(z2, g, b, target):
    S, D = z2.shape

    def body(i, z_ref, t_ref, g_ref, b_ref, dz_ref, dzb_ref, dg_ref, db_ref, loss_ref):
        gv = g_ref[...]
        y, xhat, rstd = _layer_norm_rows(z_ref[...], gv, b_ref[...])
        err = y - t_ref[...]
        loss = 0.5 * jnp.sum(jnp.mean(err * err, axis=-1, keepdims=True), axis=0, keepdims=True)
        dy = err * (1.0 / D)
        dz = _layer_norm_bwd(dy, xhat, rstd, gv)
        dz_ref[...] = dz
        dzb_ref[...] = dz.astype(BF16)
        _accumulate_rows(dg_ref, jnp.sum(dy * xhat, axis=0, keepdims=True), i)
        _accumulate_rows(db_ref, jnp.sum(dy, axis=0, keepdims=True), i)
        _accumulate_rows(loss_ref, jnp.broadcast_to(loss, (1, HEAD_DIM)), i)

    return _row_kernel("ln_loss_bwd", body, [z2, target], [g, b], [(D, F32), (D, BF16)], [D, D, HEAD_DIM],
                       min(256, S))


def _ln_bwd(dy, xhat, rstd, g):
    S, D = dy.shape

    def body(i, dy_ref, xh_ref, rs_ref, g_ref, dz_ref, dzb_ref, dg_ref, db_ref):
        dyv, xhat_v = dy_ref[...], xh_ref[...]
        dz = _layer_norm_bwd(dyv, xhat_v, rs_ref[:, :1], g_ref[...])
        dz_ref[...] = dz
        dzb_ref[...] = dz.astype(BF16)
        _accumulate_rows(dg_ref, jnp.sum(dyv * xhat_v, axis=0, keepdims=True), i)
        _accumulate_rows(db_ref, jnp.sum(dyv, axis=0, keepdims=True), i)

    return _row_kernel("ln_bwd", body, [dy, xhat, rstd], [g], [(D, F32), (D, BF16)], [D, D], min(256, S))


def _grad_weight(name, act, cot):
    M, N = act.shape[1], cot.shape[1]
    tm, tn = min(1024, M), min(2048, N)

    def epi(acc, ex, out, i, j):
        out[0][...] = acc.astype(BF16)

    (g,) = _mm(name, act, cot, "tn", (tm, tn, 2 * K_TILE), [_tile_out((M, N), BF16, tm, tn)], epi)
    return g


def _ffn_down_bwd(dz2b, w2, a, after=None):
    S, D = dz2b.shape
    F = w2.shape[0]
    tm, tn = min(1024, S), min(2048, F)

    def epi(acc, ex, out, i, j):
        out[0][...] = (acc * (2.0 * jnp.sqrt(ex[0][...])).astype(F32)).astype(BF16)

    (dh1,) = _mm("ffn_down_bwd", dz2b, w2, "nt", (tm, tn, 2 * K_TILE), [_tile_out((S, F), BF16, tm, tn)], epi,
                 extras=[(a, (tm, tn), lambda i, j, k: (i, j))], after=after)
    return dh1


def _out_proj_bwd(dz1b, w_out, h, y_attn, y_pool, gate_col0, after=None):
    S, D = dz1b.shape
    W = h.shape[1]
    tm = min(256, S)
    assert gate_col0 == 2 * D and W == 4 * D

    def epi(acc, ex, out, i, j):
        gates_ref, ya_ref, yp_ref = ex
        dya_ref, dyp_ref, dh_ref = out
        sa = jax.nn.sigmoid(gates_ref[:, :D])
        sp = jax.nn.sigmoid(gates_ref[:, D:])
        dya_ref[...] = (acc * sa.astype(F32)).astype(BF16)
        dyp_ref[...] = (acc * sp.astype(F32)).astype(BF16)
        dh_ref[:, :D] = (acc * (ya_ref[...] * (sa * (1.0 - sa))).astype(F32)).astype(BF16)
        dh_ref[:, D:] = (acc * (yp_ref[...] * (sp * (1.0 - sp))).astype(F32)).astype(BF16)

    row = lambda i, j, k: (i, 0)
    return _mm("out_proj_bwd", dz1b, w_out, "nt", (tm, D, D),
               [((S, D), BF16, (tm, D), row), ((S, D), BF16, (tm, D), row),
                ((S, W), BF16, (tm, 2 * D), lambda i, j, k: (i, 1))], epi,
               extras=[(h, (tm, 2 * D), lambda i, j, k: (i, 1)), (y_attn, (tm, D), row), (y_pool, (tm, D), row)],
               after=after)


def _branch_attn_bwd(dy_attn, w_ba, o_attn, l_tot, after=None):
    S, D = dy_attn.shape
    aw = w_ba.shape[0]
    n_heads = aw // HEAD_DIM
    tm = min(512, S)
    dils = [d for d in DILATIONS if d > 1]

    def epi(acc, ex, out, i, j):
        do_ref, st_ref = out[0], out[1]
        do = acc.astype(BF16)
        do_ref[...] = do
        o = ex[0][...].astype(F32)
        stats = ex[1][...]
        for hd in range(n_heads):
            sl = slice(hd * HEAD_DIM, (hd + 1) * HEAD_DIM)
            stats = _put_column(stats, n_heads + hd, jnp.sum(acc[:, sl] * o[:, sl], axis=-1, keepdims=True))
        st_ref[...] = stats
        for di, d in enumerate(dils):
            perm = _perm_matrix(d, True)
            g = PERM_ROWS // d
            for part in range(tm // PERM_ROWS):
                rows = slice(part * PERM_ROWS, (part + 1) * PERM_ROWS)
                to = slice(part * g, (part + 1) * g)
                out[2 + di][:, to, :] = _permute_rows(perm, do[rows]).astype(BF16).reshape(d, g, aw)
                out[2 + len(dils) + di][:, to, :] = _permute_rows(perm, stats[rows]).reshape(d, g, HEAD_DIM)

    row = lambda i, j, k: (i, 0)
    rm = lambda i, j, k: (0, i, 0)
    res = _mm("branch_attn_bwd", dy_attn, w_ba, "nt", (tm, aw, D),
              [((S, aw), BF16, (tm, aw), row), ((S, HEAD_DIM), F32, (tm, HEAD_DIM), row)]
              + [((d, S // d, aw), BF16, (d, tm // d, aw), rm) for d in dils]
              + [((d, S // d, HEAD_DIM), F32, (d, tm // d, HEAD_DIM), rm) for d in dils], epi,
              extras=[(o_attn, (tm, aw), row), (l_tot, (tm, HEAD_DIM), row)], after=after)
    n = len(dils)
    return ({1: res[0][None], **dict(zip(dils, res[2:2 + n]))}, {1: res[1][None], **dict(zip(dils, res[2 + n:]))})


def _branch_pool_bwd(dy_pool, w_bp, y_pre, pool_scale):
    S, D = dy_pool.shape
    pw = w_bp.shape[0]
    tm = min(512, S)

    def epi(acc, ex, out, i, j):
        y_ref, sc_ref = ex
        dyp_ref, dsc_ref = out
        dyp_ref[...] = (acc * sc_ref[...]).astype(BF16)
        _accumulate_rows(dsc_ref, jnp.sum(acc * y_ref[...].astype(F32), axis=0, keepdims=True), i)

    row = lambda i, j, k: (i, 0)
    return _mm("branch_pool_bwd", dy_pool, w_bp, "nt", (tm, pw, D),
               [((S, pw), BF16, (tm, pw), row), _row_sum_out(pw)], epi,
               extras=[(y_pre, (tm, pw), row), (pool_scale, (1, pw), lambda i, j, k: (0, 0))],
               sequential=True)


def _pool_bwd(dh, dy_pre, p, w_pool, pw, u_col_block):
    S, W = dh.shape
    n_groups = len(POOL_WINDOWS)
    gw = pw // n_groups
    tm = min(512, S)
    n_tiles = S // tm
    halo_per_tile = tm // POOL_HALO
    n_halo_blocks = S // POOL_HALO

    def body(dh_in_ref, dyc_ref, dyh_ref, p_ref, w_ref, dh_ref, dw_ref):
        del dh_in_ref
        i = pl.program_id(0)
        t_cur = i * tm + lax.broadcasted_iota(jnp.int32, (tm, 1), 0)
        t_halo = (i + 1) * tm + lax.broadcasted_iota(jnp.int32, (POOL_HALO, 1), 0)
        for g, w in enumerate(POOL_WINDOWS):
            sl = slice(g * gw, (g + 1) * gw)
            wg = w_ref[g]
            dyc = dyc_ref[:, sl]
            dyh = jnp.where(i < n_tiles - 1, dyh_ref[:, sl], jnp.zeros((POOL_HALO, gw), BF16))
            dp_cur = _dot_nt(dyc, wg)
            dp_halo = _dot_nt(dyh, wg)
            dpc_cur = (dp_cur / jnp.minimum(t_cur + 1, w).astype(F32)).astype(BF16)
            dpc_halo = (dp_halo / jnp.minimum(t_halo + 1, w).astype(F32)).astype(BF16)
            du = (_dot_nn(_band(tm, tm, w, 0, True), dpc_cur)
                  + _dot_nn(_band(tm, POOL_HALO, w, -tm, True), dpc_halo) - dp_cur)
            dh_ref[:, sl] = du.astype(BF16)
            dw = _dot_tn(p_ref[:, sl], dyc)

            @pl.when(i == 0)
            def _():
                dw_ref[g] = dw

            @pl.when(i > 0)
            def _():
                dw_ref[g] += dw

    row = pl.BlockSpec((tm, pw), lambda i: (i, 0))
    dh_new, dw_pool = pl.pallas_call(
        body, name="pool_bwd", grid=(n_tiles,),
        in_specs=[pl.BlockSpec(memory_space=pl.ANY), row,
                  pl.BlockSpec((POOL_HALO, pw), lambda i: (jnp.minimum((i + 1) * halo_per_tile, n_halo_blocks - 1), 0)),
                  row, pl.BlockSpec((n_groups, gw, gw), lambda i: (0, 0, 0))],
        out_specs=[pl.BlockSpec((tm, pw), lambda i: (i, u_col_block)),
                   pl.BlockSpec((n_groups, gw, gw), lambda i: (0, 0, 0))],
        out_shape=[jax.ShapeDtypeStruct((S, W), BF16), jax.ShapeDtypeStruct((n_groups, gw, gw), F32)],
        input_output_aliases={0: 0},
        compiler_params=pltpu.CompilerParams(dimension_semantics=("arbitrary",),
                                             vmem_limit_bytes=_vmem_limit(24 << 20)),
    )(dh, dy_pre, dy_pre, p, w_pool)
    return dh_new, dw_pool


def _attn_bwd(qkv, d_out, stats, d, aw):
    _, rows, _ = qkv.shape
    n_heads = aw // HEAD_DIM
    nb = rows // SUB_BLOCK
    n_blocks = d * nb

    def body(q_ref, kp_ref, kc_ref, vp_ref, vc_ref, do_ref, st_ref, dq_ref, dk_ref, dv_ref,
             carry_k, carry_v, s_buf, dp_buf, p_buf, ds_buf):
        step = pl.program_id(0)

        @pl.when(step == 0)
        def _():
            carry_k[...] = jnp.zeros_like(carry_k)
            carry_v[...] = jnp.zeros_like(carry_v)

        @pl.when(step < n_blocks)
        def _():
            mask = _attn_mask(step % nb)
            st = st_ref[...]
            for hd in range(n_heads):
                sl = slice(hd * HEAD_DIM, (hd + 1) * HEAD_DIM)
                s_buf[hd] = _dot_nt(q_ref[:, sl], _both_blocks(kp_ref, kc_ref, sl))
                dp_buf[hd] = _dot_nt(do_ref[:, sl], _both_blocks(vp_ref, vc_ref, sl))
            for hd in range(n_heads):
                lt, dl = st[:, hd:hd + 1], st[:, n_heads + hd:n_heads + hd + 1]
                p = jnp.where(mask, jnp.exp(jnp.where(mask, s_buf[hd] * ATTN_SCALE - lt, NEG_BIG)), 0.0)
                p_buf[hd] = p.astype(BF16)
                ds_buf[hd] = (p * (dp_buf[hd] - dl) * ATTN_SCALE).astype(BF16)
            for hd in range(n_heads):
                sl = slice(hd * HEAD_DIM, (hd + 1) * HEAD_DIM)
                dq_ref[:, sl] = _dot_nn(ds_buf[hd], _both_blocks(kp_ref, kc_ref, sl)).astype(BF16)
                dk_both = _dot_tn(ds_buf[hd], q_ref[:, sl])
                dv_both = _dot_tn(p_buf[hd], do_ref[:, sl])
                dk_ref[:, sl] = (carry_k[:, sl] + dk_both[:SUB_BLOCK]).astype(BF16)
                dv_ref[:, sl] = (carry_v[:, sl] + dv_both[:SUB_BLOCK]).astype(BF16)
                carry_k[:, sl] = dk_both[SUB_BLOCK:]
                carry_v[:, sl] = dv_both[SUB_BLOCK:]

        @pl.when(step == n_blocks)
        def _():
            dk_ref[...] = carry_k[...].astype(BF16)
            dv_ref[...] = carry_v[...].astype(BF16)

    def cur(step):
        return jnp.minimum(step, n_blocks - 1)

    def qkv_spec(col, prev):
        if prev:
            return pl.BlockSpec((SUB_BLOCK, aw), lambda s: (jnp.maximum(cur(s) - 1, 0), col))
        return pl.BlockSpec((SUB_BLOCK, aw), lambda s: (cur(s), col))

    def at_cur(w):
        return pl.BlockSpec((SUB_BLOCK, w), lambda s: (cur(s), 0))

    finished = pl.BlockSpec((SUB_BLOCK, aw), lambda s: (jnp.maximum(s - 1, 0), 0))
    pair = (n_heads, SUB_BLOCK, 2 * SUB_BLOCK)
    flat = lambda t: t.reshape(d * rows, t.shape[-1])
    qkv2 = flat(qkv)
    outs = pl.pallas_call(
        body, name=f"attn_bwd_d{d}", grid=(n_blocks + 1,),
        in_specs=[qkv_spec(0, False), qkv_spec(1, True), qkv_spec(1, False), qkv_spec(2, True), qkv_spec(2, False),
                  at_cur(aw), at_cur(HEAD_DIM)],
        out_specs=[at_cur(aw), finished, finished],
        out_shape=[jax.ShapeDtypeStruct((d * rows, aw), BF16)] * 3,
        scratch_shapes=[pltpu.VMEM((SUB_BLOCK, aw), F32), pltpu.VMEM((SUB_BLOCK, aw), F32),
                        pltpu.VMEM(pair, F32), pltpu.VMEM(pair, F32), pltpu.VMEM(pair, BF16), pltpu.VMEM(pair, BF16)],
        compiler_params=pltpu.CompilerParams(dimension_semantics=("arbitrary",),
                                             vmem_limit_bytes=_vmem_limit(24 << 20)),
    )(qkv2, qkv2, qkv2, qkv2, qkv2, flat(d_out), flat(stats))
    return [t.reshape(d, rows, aw) for t in outs]


def _attn_bwd_finish(dh, per_pattern, cos2, sin_bwd, aw):
    S, W = dh.shape
    n_heads = aw // HEAD_DIM
    n_pat = len(DILATIONS)

    def body(*refs):
        grad_refs = refs[1:1 + 3 * n_pat]
        cos_ref, sin_ref = refs[1 + 3 * n_pat], refs[2 + 3 * n_pat]
        out_ref = refs[3 + 3 * n_pat]
        perms = {d: _perm_matrix(d, False) for d in DILATIONS if d > 1}
        totals = []
        for which in range(3):
            tot = None
            for pi, d in enumerate(DILATIONS):
                g = grad_refs[which * n_pat + pi][...].reshape(PERM_ROWS, aw)
                g = _permute_rows(perms[d], g) if d > 1 else g.astype(F32)
                tot = g if tot is None else tot + g
            totals.append(tot)
        dq, dk, dv = totals
        c, s = cos_ref[...], sin_ref[...]
        for hd in range(n_heads):
            sl = slice(hd * HEAD_DIM, (hd + 1) * HEAD_DIM)
            out_ref[:, sl] = _rope_apply(dq[:, sl], c, s).astype(BF16)
            out_ref[:, aw + hd * HEAD_DIM:aw + (hd + 1) * HEAD_DIM] = _rope_apply(dk[:, sl], c, s).astype(BF16)
        out_ref[:, 2 * aw:] = dv.astype(BF16)

    grads = [pp[which] for which in range(3) for pp in per_pattern]
    rope_spec = pl.BlockSpec((PERM_ROWS, HEAD_DIM), lambda i: (i, 0))
    return pl.pallas_call(
        body, name="attn_bwd_finish", grid=(S // PERM_ROWS,),
        in_specs=([pl.BlockSpec(memory_space=pl.ANY)] + [_rm_block(d, aw) for d in DILATIONS] * 3
                  + [rope_spec, rope_spec]),
        out_specs=pl.BlockSpec((PERM_ROWS, 3 * aw), lambda i: (i, 0)),
        out_shape=jax.ShapeDtypeStruct((S, W), BF16),
        input_output_aliases={0: 0},
        compiler_params=pltpu.CompilerParams(dimension_semantics=("parallel",),
                                             vmem_limit_bytes=_vmem_limit(32 << 20)),
    )(dh, *grads, cos2, sin_bwd)


def _my_place():
    x, y, c = lax.axis_index("x"), lax.axis_index("y"), lax.axis_index("c")
    return x, y, c


def _flat(px, py, pc):
    return 4 * px + 2 * py + pc


def _shard_slice(ref, axis, idx, size):
    start = pl.multiple_of(idx * size, size)
    ix = [slice(None)] * len(ref.shape)
    ix[axis] = pl.ds(start, size)
    return ref.at[tuple(ix)]


_HBM_SPEC = pl.BlockSpec(memory_space=pltpu.HBM)
_SEM_SPEC = pl.BlockSpec(memory_space=pltpu.SEMAPHORE)
_ANY_SPEC = pl.BlockSpec(memory_space=pl.ANY)
_N_PEER = N_DEV - 1
SIBLING, SAME_CORE_NEIGHBOURS, OTHER_CORE_NEIGHBOURS, DIAGONAL = (1,), (2, 4), (3, 5), (6, 7)
PEER_ORDER = SIBLING + SAME_CORE_NEIGHBOURS + OTHER_CORE_NEIGHBOURS + DIAGONAL


def _peer_of(x, y, c, r):
    return (x ^ ((r >> 2) & 1), y ^ ((r >> 1) & 1), c ^ (r & 1))


class _Exchange:
    def __init__(self, name, part, slot):
        self.name, self.part, self.slot = name, part, slot

    def _copy(self, w, r, src, land, send_sems, recv_sems, sending):
        x, y, c = _my_place()
        peer = _peer_of(x, y, c, r)
        return pltpu.make_async_remote_copy(
            src_ref=self.part(w, src, _flat(*peer)),
            dst_ref=self.slot(w, land, _flat(x, y, c) if sending else _flat(*peer)),
            send_sem=send_sems.at[w * _N_PEER + r - 1], recv_sem=recv_sems.at[w * _N_PEER + r - 1],
            device_id=peer, device_id_type=MESH)

    def start(self, srcs, lands, after=None):
        n = len(srcs)
        n_after = 0 if after is None else 1

        def body(*refs):
            src, land = refs[:n], refs[n:2 * n]
            outs = refs[2 * n + n_after:]
            send_sems, recv_sems, local_sems, token = outs[0], outs[1], outs[2], outs[3 + 2 * n]
            for w in range(n):
                self._own_copy(w, src[w], land[w], local_sems).start()
                for r in PEER_ORDER:
                    self._copy(w, r, src[w], land[w], send_sems, recv_sems, True).start()
            token[...] = jnp.zeros_like(token)

        sems = pltpu.SemaphoreType.DMA((n * _N_PEER,))
        outs = pl.pallas_call(
            body, name=self.name + "_start",
            out_shape=(sems, sems, pltpu.SemaphoreType.DMA((n,)),
                       *[pltpu.HBM(t.shape, t.dtype) for t in list(srcs) + list(lands)],
                       jax.ShapeDtypeStruct((8, 128), F32)),
            in_specs=[_HBM_SPEC] * (2 * n) + [_ANY_SPEC] * n_after,
            out_specs=(_SEM_SPEC, _SEM_SPEC, _SEM_SPEC, *[_HBM_SPEC] * (2 * n), pl.BlockSpec(memory_space=pltpu.VMEM)),
            input_output_aliases={i: 3 + i for i in range(2 * n)},
            compiler_params=pltpu.CompilerParams(has_side_effects=pltpu.SideEffectType.DATAFLOW_SIDE_EFFECTING),
        )(*[pltpu.with_memory_space_constraint(t, pltpu.HBM) for t in list(srcs) + list(lands)],
          *([after] if n_after else []))
        return outs[0], outs[1], outs[2], outs[3:3 + n], outs[3 + n:3 + 2 * n], outs[3 + 2 * n]

    def _own_copy(self, w, src, land, local_sems):
        me = _flat(*_my_place())
        return pltpu.make_async_copy(self.part(w, src, me), self.slot(w, land, me), local_sems.at[w])

    def wait(self, started, after, peers=PEER_ORDER, own=True, tag=""):
        send_sems, recv_sems, local_sems, srcs, lands, token = started
        n = len(srcs)

        def body(*refs):
            src, land = refs[:n], refs[n:2 * n]
            s_sems, r_sems, l_sems = refs[2 * n], refs[2 * n + 1], refs[2 * n + 2]
            for w in range(n):
                if own:
                    self._own_copy(w, src[w], land[w], l_sems).wait()
                for r in peers:
                    cp = self._copy(w, r, src[w], land[w], s_sems, r_sems, False)
                    cp.wait_send()
                    cp.wait_recv()

        outs = pl.pallas_call(
            body, name=self.name + "_wait" + tag,
            out_shape=[pltpu.HBM(t.shape, t.dtype) for t in list(srcs) + list(lands)],
            in_specs=[_HBM_SPEC] * (2 * n) + [_SEM_SPEC, _SEM_SPEC, _SEM_SPEC, _ANY_SPEC],
            out_specs=[_HBM_SPEC] * (2 * n),
            input_output_aliases={i: i for i in range(2 * n)},
            compiler_params=pltpu.CompilerParams(has_side_effects=pltpu.SideEffectType.DATAFLOW_SIDE_EFFECTING),
        )(*srcs, *lands, send_sems, recv_sems, local_sems, after)
        return outs[n:], (send_sems, recv_sems, local_sems, outs[:n], outs[n:], token)


DIRECT_PEERS = (1, 2, 4, 6)
FORWARDED = (3, 5, 7)


class _TwoLevelGather:
    def __init__(self, name, axes, sizes):
        self.name, self.axes, self.sizes = name, axes, sizes

    def _place(self, w, land, dev):
        return _shard_slice(land, self.axes[w], dev, self.sizes[w])

    def _direct(self, w, r, src, land, sems, sending):
        x, y, c = _my_place()
        peer = _peer_of(x, y, c, r)
        k = w * len(DIRECT_PEERS) + DIRECT_PEERS.index(r)
        return pltpu.make_async_remote_copy(
            src_ref=src, dst_ref=self._place(w, land, _flat(x, y, c) if sending else _flat(*peer)),
            send_sem=sems[0].at[k], recv_sem=sems[1].at[k], device_id=peer, device_id_type=MESH)

    def _passed_on(self, w, f, land, sems, sending):
        x, y, c = _my_place()
        owner = _flat(*_peer_of(x, y, c, (f ^ 1) if sending else f))
        slot = self._place(w, land, owner)
        k = w * len(FORWARDED) + FORWARDED.index(f)
        return pltpu.make_async_remote_copy(
            src_ref=slot, dst_ref=slot, send_sem=sems[2].at[k], recv_sem=sems[3].at[k],
            device_id=(x, y, 1 - c), device_id_type=MESH)

    def _own(self, w, src, land, sems):
        return pltpu.make_async_copy(src, self._place(w, land, _flat(*_my_place())), sems[4].at[w])

    def _call(self, suffix, body, sems, srcs, lands, after, make_sems):
        n = len(srcs)
        n_after = 0 if after is None else 1
        bufs = list(srcs) + list(lands)

        def wrapped(*refs):
            ins = refs[:2 * n]
            rest = refs[2 * n + (n_after if make_sems else 0):]
            body(ins[:n], ins[n:], rest[:5], rest[-1] if make_sems else None)

        buf_shapes = [pltpu.HBM(t.shape, t.dtype) for t in bufs]
        if make_sems:
            sem_types = [pltpu.SemaphoreType.DMA((n * len(DIRECT_PEERS),))] * 2 \
                + [pltpu.SemaphoreType.DMA((n * len(FORWARDED),))] * 2 + [pltpu.SemaphoreType.DMA((n,))]
            outs = pl.pallas_call(
                wrapped, name=self.name + suffix,
                out_shape=(*sem_types, *buf_shapes, jax.ShapeDtypeStruct((8, 128), F32)),
                in_specs=[_HBM_SPEC] * (2 * n) + [_ANY_SPEC] * n_after,
                out_specs=(*[_SEM_SPEC] * 5, *[_HBM_SPEC] * (2 * n), pl.BlockSpec(memory_space=pltpu.VMEM)),
                input_output_aliases={i: 5 + i for i in range(2 * n)},
                compiler_params=pltpu.CompilerParams(has_side_effects=pltpu.SideEffectType.DATAFLOW_SIDE_EFFECTING),
            )(*[pltpu.with_memory_space_constraint(t, pltpu.HBM) for t in bufs], *([after] if n_after else []))
            return tuple(outs[:5]), outs[5:5 + n], outs[5 + n:5 + 2 * n], outs[5 + 2 * n]
        outs = pl.pallas_call(
            wrapped, name=self.name + suffix,
            out_shape=buf_shapes,
            in_specs=[_HBM_SPEC] * (2 * n) + [_SEM_SPEC] * 5 + [_ANY_SPEC] * n_after,
            out_specs=[_HBM_SPEC] * (2 * n),
            input_output_aliases={i: i for i in range(2 * n)},
            compiler_params=pltpu.CompilerParams(has_side_effects=pltpu.SideEffectType.DATAFLOW_SIDE_EFFECTING),
        )(*bufs, *sems, *([after] if n_after else []))
        return sems, outs[:n], outs[n:], None

    def start(self, srcs, lands, after=None):
        n = len(srcs)

        def body(src, land, sems, token):
            for w in range(n):
                self._own(w, src[w], land[w], sems).start()
                for r in DIRECT_PEERS:
                    self._direct(w, r, src[w], land[w], sems, True).start()
            token[...] = jnp.zeros_like(token)

        return self._call("_start", body, None, srcs, lands, after, True)

    def forward(self, state, after, which, tag=""):
        sems, srcs, lands, token = state
        n = len(srcs)

        def body(src, land, sem_refs, _):
            for w in range(n):
                for r in which:
                    self._direct(w, r, src[w], land[w], sem_refs, False).wait_recv()
                    self._passed_on(w, r | 1, land[w], sem_refs, True).start()

        sems, srcs, lands, _ = self._call("_forward" + tag, body, sems, srcs, lands, after, False)
        return sems, srcs, lands, token

    def wait(self, state, after, direct=(), passed_on=(), sends=False, tag=""):
        sems, srcs, lands, token = state
        n = len(srcs)

        def body(src, land, sem_refs, _):
            for w in range(n):
                for r in direct:
                    self._direct(w, r, src[w], land[w], sem_refs, False).wait_recv()
                for f in passed_on:
                    self._passed_on(w, f, land[w], sem_refs, False).wait_recv()
                if sends:
                    self._own(w, src[w], land[w], sem_refs).wait()
                    for r in DIRECT_PEERS:
                        self._direct(w, r, src[w], land[w], sem_refs, True).wait_send()
                    for f in FORWARDED:
                        self._passed_on(w, f, land[w], sem_refs, True).wait_send()

        sems, srcs, lands, _ = self._call("_wait" + tag, body, sems, srcs, lands, after, False)
        return lands, (sems, srcs, lands, token)


def _scatter_exchange(name, axes, shard_sizes):
    def part(w, src, dev):
        return src if axes[w] is None else _shard_slice(src, axes[w], dev, shard_sizes[w])
    return _Exchange(name, part, lambda w, land, dev: land.at[dev])


def _adamw(name, partials, w, m, v):
    R, C = w.shape
    tr = R
    while tr * C * 4 > (1 << 20) and tr % 16 == 0:
        tr //= 2

    def body(p_ref, w_ref, m_ref, v_ref, g_ref, d_ref, nm_ref, nv_ref):
        g = p_ref[0].astype(F32)
        for jdev in range(1, N_DEV):
            g = g + p_ref[jdev].astype(F32)
        nm = ADAM_B1 * m_ref[...] + (1.0 - ADAM_B1) * g
        nv = ADAM_B2 * v_ref[...] + (1.0 - ADAM_B2) * (g * g)
        m_hat = nm / (1.0 - ADAM_B1 ** ADAM_STEP)
        v_hat = nv / (1.0 - ADAM_B2 ** ADAM_STEP)
        g_ref[...] = g
        d_ref[...] = -ADAM_LR * (m_hat / (jnp.sqrt(v_hat) + ADAM_EPS) + ADAM_WD * w_ref[...])
        nm_ref[...] = nm
        nv_ref[...] = nv

    spec = pl.BlockSpec((tr, C), lambda i: (i, 0))
    return pl.pallas_call(
        body, name=name, grid=(R // tr,),
        in_specs=[pl.BlockSpec((N_DEV, tr, C), lambda i: (0, i, 0)), spec, spec, spec],
        out_specs=[spec] * 4,
        out_shape=[jax.ShapeDtypeStruct((R, C), F32)] * 4,
        compiler_params=pltpu.CompilerParams(dimension_semantics=("parallel",),
                                             vmem_limit_bytes=_vmem_limit(24 << 20)),
    )(partials, w, m, v)


def _local_step(x, cos2, sin_fwd, sin_bwd, project_in, mix_weights, ffn_weights, pool_scale, g_mix, b_mix, g_ff, b_ff,
                target, send):
    S, D = x.shape
    aw = pw = D // 2
    u_col_block = 3
    gate_col0 = 4 * aw

    xb = x.astype(BF16)
    h, w_in = project_in(xb)
    dilated = [d for d in DILATIONS if d > 1]
    qkv = {1: h[None], **dict(zip(dilated, _to_residue_major("qkv_to_rm", h, 0, 3 * aw)))}
    fwd = [_attn_fwd(qkv[d], d, aw) for d in DILATIONS]
    o_attn, l_tot = _attn_combine([f[0] for f in fwd], [f[1] for f in fwd], aw)
    w_pool, w_ba, w_bp, w_out = mix_weights(o_attn)
    p, y_pre, pm = _pool_fwd(h, w_pool, pool_scale, pw, u_col_block)
    y_attn = _branch_attn(o_attn, w_ba)
    y_pool, merged = _branch_pool_merge(pm, w_bp, h, y_attn, gate_col0)
    w1, w2 = ffn_weights(merged)
    x1, x1b, xhat1, rstd1 = _out_proj_ln(merged, w_out, x, g_mix, b_mix)
    a = _ffn_up(x1b, w1)
    z2 = _residual_matmul("ffn_down", a, w2, "nn", x1)
    dz2, dz2b, dg_ff, db_ff, loss = _ln_loss_bwd(z2, g_ff, b_ff, target)

    tok = send("ff2", [_grad_weight("grad_w_ff2", a, dz2b)])
    dh1 = _ffn_down_bwd(dz2b, w2, a, after=tok)
    tok = send("ff1", [_grad_weight("grad_w_ff1", x1b, dh1)])
    dy1 = _residual_matmul("ffn_up_bwd", dh1, w1, "nt", dz2, after=tok)
    dz1, dz1b, dg_mix, db_mix = _ln_bwd(dy1, xhat1, rstd1, g_mix)
    tok = send("out", [_grad_weight("grad_w_out", merged, dz1b)])
    dy_attn, dy_pool, dh = _out_proj_bwd(dz1b, w_out, h, y_attn, y_pool, gate_col0, after=tok)
    tok = send("branch", [_grad_weight("grad_w_branch_attn", o_attn, dy_attn),
                          _grad_weight("grad_w_branch_pool", pm, dy_pool)])
    d_outs, statss = _branch_attn_bwd(dy_attn, w_ba, o_attn, l_tot, after=tok)
    dy_pre, d_scale = _branch_pool_bwd(dy_pool, w_bp, y_pre, pool_scale)
    dh, dw_pool = _pool_bwd(dh, dy_pre, p, w_pool, pw, u_col_block)
    per_pattern = [_attn_bwd(qkv[d], d_outs[d], statss[d], d, aw) for d in DILATIONS]
    dh = _attn_bwd_finish(dh, per_pattern, cos2, sin_bwd, aw)
    small = jnp.concatenate((d_scale, dg_mix, db_mix, dg_ff, db_ff), axis=-1)
    tok = send("in", [_grad_weight("grad_w_in", xb, dh), dw_pool.astype(BF16),
                      small.reshape(small.shape[-1] // HEAD_DIM, HEAD_DIM)])
    grad_x = _residual_matmul("in_proj_bwd", dh, w_in, "nt", dz1, after=tok)
    return loss, grad_x


def _rope_tables(positions):
    half = HEAD_DIM // 2
    inv_freq = ROPE_THETA ** (-jnp.arange(half, dtype=F32) / half)
    ang = positions.astype(F32)[:, None] * inv_freq
    cos, sin = jnp.cos(ang), jnp.sin(ang)
    cos2 = jnp.concatenate([cos, cos], axis=-1)
    sin_fwd = jnp.concatenate([-sin, sin], axis=-1)
    return cos2, sin_fwd, -sin_fwd


def kernel(x, positions, w_in, w_pool, pool_scale, w_branch_attn, w_branch_pool, w_out, ln_mix_g, ln_mix_b, w_ff1, w_ff2, ln_ff_g, ln_ff_b, loss_target, m_w_in, m_w_pool, m_pool_scale, m_w_branch_attn, m_w_branch_pool, m_w_out, m_ln_mix_g, m_ln_mix_b, m_w_ff1, m_w_ff2, m_ln_ff_g, m_ln_ff_b, v_w_in, v_w_pool, v_pool_scale, v_w_branch_attn, v_w_branch_pool, v_w_out, v_ln_mix_g, v_ln_mix_b, v_w_ff1, v_w_ff2, v_ln_ff_g, v_ln_ff_b):
    big_w = (w_in[0], w_pool[0], w_branch_attn[0], w_branch_pool[0], w_out[0], w_ff1[0], w_ff2[0])
    big_m = (m_w_in[0], m_w_pool[0], m_w_branch_attn[0], m_w_branch_pool[0], m_w_out[0], m_w_ff1[0], m_w_ff2[0])
    big_v = (v_w_in[0], v_w_pool[0], v_w_branch_attn[0], v_w_branch_pool[0], v_w_out[0], v_w_ff1[0], v_w_ff2[0])
    shard_axes = (1, 1, 1, 1, 0, 1, 0)
    small_w = (pool_scale, ln_mix_g, ln_mix_b, ln_ff_g, ln_ff_b)
    small_m = (m_pool_scale, m_ln_mix_g, m_ln_mix_b, m_ln_ff_g, m_ln_ff_b)
    small_v = (v_pool_scale, v_ln_mix_g, v_ln_mix_b, v_ln_ff_g, v_ln_ff_b)

    names = ("w_in", "w_pool", "w_branch_attn", "w_branch_pool", "w_out", "w_ff1", "w_ff2")
    axis_of = dict(zip(names, shard_axes))
    shard_of = dict(zip(names, [w.astype(BF16) for w in big_w]))

    def full_buffer(n):
        s, ax = shard_of[n], axis_of[n]
        full = list(s.shape)
        full[ax] *= N_DEV
        return lax.empty(tuple(full), s.dtype)

    def gather_group(tag, group, after):
        ex = _TwoLevelGather(tag, [axis_of[n] for n in group], [shard_of[n].shape[axis_of[n]] for n in group])
        return ex, ex.start([shard_of[n] for n in group], [full_buffer(n) for n in group], after)

    in_ex, in_state = gather_group("gather_in", ("w_in",), None)
    mix_ex, mix_state = gather_group("gather_mix", ("w_pool", "w_branch_attn", "w_branch_pool", "w_out"),
                                     in_state[-1])
    ffn_ex, ffn_state = gather_group("gather_ffn", ("w_ff1", "w_ff2"), mix_state[-1])
    states = {"mix": mix_state, "ffn": ffn_state}
    me = 4 * lax.axis_index("x") + 2 * lax.axis_index("y") + lax.axis_index("c")
    block_cols = shard_of["w_in"].shape[1]
    neighbours, diagonal = (2, 4), (6,)

    def project_in(xb):
        n_rope_blocks = 2 * (x.shape[-1] // 2) // block_cols

        def piece(tag, w, blocks, h, **kw):
            return _in_proj_piece("in_proj_" + tag, xb, w, jnp.stack(blocks).astype(jnp.int32), cos2, sin_fwd, h,
                                  block_cols, n_rope_blocks, **kw)

        h = piece("own", shard_of["w_in"], [me], None, own_shard=True, after=ffn_state[-1])
        (w_in_land,), state = in_ex.wait(in_state, h, direct=(1,), tag="_sibling")
        h = piece("sibling", w_in_land, [me ^ 1], h)
        state = in_ex.forward(state, h, neighbours, tag="_neighbours")
        h = piece("neighbours", state[2][0], [me ^ r for r in neighbours], h)
        state = in_ex.forward(state, h, diagonal, tag="_diagonal")
        h = piece("diagonal", state[2][0], [me ^ r for r in diagonal], h)
        states["mix"] = mix_ex.forward(states["mix"], h, neighbours + diagonal)
        (w_in_land,), _ = in_ex.wait(state, h, passed_on=FORWARDED, sends=True, tag="_passed_on")
        h = piece("passed_on", w_in_land, [me ^ f for f in FORWARDED], h)
        return h, w_in_land

    def mix_weights(after):
        states["ffn"] = ffn_ex.forward(states["ffn"], after, neighbours + diagonal)
        return mix_ex.wait(states["mix"], after, direct=(1,), passed_on=FORWARDED, sends=True)[0]

    def ffn_weights(after):
        return ffn_ex.wait(states["ffn"], after, direct=(1,), passed_on=FORWARDED, sends=True)[0]

    groups = {"ff2": ("w_ff2",), "ff1": ("w_ff1",), "out": ("w_out",),
              "branch": ("w_branch_attn", "w_branch_pool"), "in": ("w_in", "w_pool", "small")}
    sent = {}

    def send(key, grads_):
        axes = [axis_of.get(n) for n in groups[key]]
        sizes = [None if ax is None else g.shape[ax] // N_DEV for g, ax in zip(grads_, axes)]
        lands = []
        for g, ax, size in zip(grads_, axes, sizes):
            shard = list(g.shape)
            if ax is not None:
                shard[ax] = size
            lands.append(lax.empty((N_DEV, *shard), g.dtype))
        ex = _scatter_exchange("scatter_" + key, axes, sizes)
        sent[key] = (ex, ex.start(list(grads_), lands))
        return sent[key][1][-1]

    cos2, sin_fwd, sin_bwd = _rope_tables(positions[0])
    loss, grad_x = _local_step(
        x[0], cos2, sin_fwd, sin_bwd, project_in, mix_weights, ffn_weights, pool_scale, ln_mix_g, ln_mix_b,
        ln_ff_g, ln_ff_b, loss_target[0], send)

    state = dict(zip(names, zip(big_w, big_m, big_v)))
    n_small = sum(w.shape[-1] for w in small_w)
    small_2d = (n_small // HEAD_DIM, HEAD_DIM)
    state["small"] = tuple(jnp.concatenate(t, axis=-1).reshape(small_2d) for t in (small_w, small_m, small_v))
    grads, deltas, new_ms, new_vs = {}, {}, {}, {}
    after = grad_x
    for key in ("ff2", "ff1", "out", "branch", "in"):
        ex, started = sent[key]
        for n, part in zip(groups[key], ex.wait(started, after)[0]):
            w, m, v = state[n]
            r2 = (-1, w.shape[-1])
            w2d = w.reshape(r2)
            res = _adamw("adamw_" + n, part.reshape((N_DEV,) + w2d.shape), w2d, m.reshape(r2), v.reshape(r2))
            after = res[0]
            if n == "small":
                small_out = [t.reshape(1, n_small) for t in res]
            else:
                grads[n], deltas[n], new_ms[n], new_vs[n] = (t.reshape((1,) + w.shape) for t in res)
    small_names = ("pool_scale", "ln_mix_g", "ln_mix_b", "ln_ff_g", "ln_ff_b")
    off = 0
    for n, w in zip(small_names, small_w):
        width = w.shape[-1]
        grads[n], deltas[n], new_ms[n], new_vs[n] = (t[:, off:off + width] for t in small_out)
        off += width

    order = ("w_in", "w_pool", "pool_scale", "w_branch_attn", "w_branch_pool", "w_out", "ln_mix_g", "ln_mix_b",
             "w_ff1", "w_ff2", "ln_ff_g", "ln_ff_b")
    total_loss = lax.psum(loss[0, 0], ("x", "y", "c"))
    return (total_loss, grad_x[None], *[grads[n] for n in order], *[deltas[n] for n in order],
            *[new_ms[n] for n in order], *[new_vs[n] for n in order])
```

```python
import functools

import jax
import jax.numpy as jnp
from jax import lax
from jax.experimental import pallas as pl
from jax.experimental.pallas import tpu as pltpu

F32 = jnp.float32
BF16 = jnp.bfloat16

N_DEV = 8
HEAD_DIM = 128
SUB_BLOCK = 128
DILATIONS = (1, 4, 16)
POOL_WINDOWS = (2, 4, 8, 16)
POOL_HALO = 128
PERM_ROWS = 256
K_TILE = 1024
ROW_CHUNK = 32
LN_EPS = 1e-5
DEEPNORM_ALPHA = 2.0 ** 0.25
ROPE_THETA = 10000.0
ATTN_SCALE = HEAD_DIM ** -0.5
ADAM_LR, ADAM_B1, ADAM_B2, ADAM_EPS, ADAM_WD, ADAM_STEP = 0.001, 0.9, 0.999, 1e-08, 0.01, 10
NEG_BIG = -1e30
VMEM_CAP_V7X = 64 * 1024 * 1024
MESH = pl.DeviceIdType.MESH


def _vmem_limit(est_bytes):
    return int(min(max(est_bytes * 5 // 4 + (4 << 20), 16 << 20), VMEM_CAP_V7X - (6 << 20)))


def _nbytes(shape, dtype):
    n = 1
    for s in shape:
        n *= s
    return n * jnp.dtype(dtype).itemsize


def _mm(name, a, b, form, tiles, outs, epi, extras=(), sequential=False, after=None):
    tm, tn, tk = tiles
    if form == "nn":
        (M, K), (K2, N) = a.shape, b.shape
    elif form == "nt":
        (M, K), (N, K2) = a.shape, b.shape
    else:
        (K, M), (K2, N) = a.shape, b.shape
    assert K == K2, (name, a.shape, b.shape)
    tm, tn, tk = min(tm, M), min(tn, N), min(tk, K)
    assert M % tm == 0 and N % tn == 0 and K % tk == 0, (name, M, N, K, tm, tn, tk)
    grid = (M // tm, N // tn, K // tk)
    nk = grid[2]
    if form == "nn":
        a_spec = pl.BlockSpec((tm, tk), lambda i, j, k: (i, k))
        b_spec = pl.BlockSpec((tk, tn), lambda i, j, k: (k, j))
        contract = ((1,), (0,))
    elif form == "nt":
        a_spec = pl.BlockSpec((tm, tk), lambda i, j, k: (i, k))
        b_spec = pl.BlockSpec((tn, tk), lambda i, j, k: (j, k))
        contract = ((1,), (1,))
    else:
        a_spec = pl.BlockSpec((tk, tm), lambda i, j, k: (k, i))
        b_spec = pl.BlockSpec((tk, tn), lambda i, j, k: (k, j))
        contract = ((0,), (0,))
    n_ex, n_out = len(extras), len(outs)
    n_after = 0 if after is None else 1

    def body(a_ref, b_ref, *rest):
        ex_refs = rest[:n_ex]
        rest = rest[n_ex + n_after:]
        out_refs = rest[:n_out]
        i, j, k = pl.program_id(0), pl.program_id(1), pl.program_id(2)

        def prod():
            return lax.dot_general(a_ref[...].astype(BF16), b_ref[...].astype(BF16),
                                   (contract, ((), ())), preferred_element_type=F32)

        if nk == 1:
            epi(prod(), ex_refs, out_refs, i, j)
        else:
            acc = rest[n_out]

            @pl.when(k == 0)
            def _():
                acc[...] = prod()

            @pl.when(jnp.logical_and(k > 0, k < nk - 1))
            def _():
                acc[...] += prod()

            @pl.when(k == nk - 1)
            def _():
                epi(acc[...] + prod(), ex_refs, out_refs, i, j)

    est = 2 * (_nbytes(a_spec.block_shape, a.dtype) + _nbytes(b_spec.block_shape, b.dtype))
    est += sum(2 * _nbytes(bs, arr.dtype) for arr, bs, _ in extras)
    est += sum(2 * _nbytes(bs, dt) for _, dt, bs, _ in outs)
    est += 4 * tm * tn * 4
    sem = ("arbitrary",) * 3 if sequential else ("parallel", "parallel", "arbitrary")
    return pl.pallas_call(
        body, name=name, grid=grid,
        in_specs=([a_spec, b_spec] + [pl.BlockSpec(bs, im) for _, bs, im in extras]
                  + [pl.BlockSpec(memory_space=pl.ANY)] * n_after),
        out_specs=[pl.BlockSpec(bs, im) for _, _, bs, im in outs],
        out_shape=[jax.ShapeDtypeStruct(sh, dt) for sh, dt, _, _ in outs],
        scratch_shapes=[pltpu.VMEM((tm, tn), F32)] if nk > 1 else [],
        compiler_params=pltpu.CompilerParams(dimension_semantics=sem, vmem_limit_bytes=_vmem_limit(est)),
    )(a, b, *[arr for arr, _, _ in extras], *([after] if n_after else []))


def _tile_out(shape, dtype, tm, tn):
    return (shape, dtype, (tm, tn), lambda i, j, k: (i, j))


def _row_sum_out(width):
    return ((1, width), F32, (1, width), lambda i, j, k: (0, 0))


def _accumulate_rows(ref, value, i):
    @pl.when(i == 0)
    def _():
        ref[...] = value

    @pl.when(i > 0)
    def _():
        ref[...] += value


def _layer_norm_bwd(dy, xhat, rstd, g):
    dxh = dy * g
    m1 = jnp.mean(dxh, axis=-1, keepdims=True)
    m2 = jnp.mean(dxh * xhat, axis=-1, keepdims=True)
    return rstd * (dxh - m1 - xhat * m2)


def _rope_apply(t, cos2, sin_signed):
    return t * cos2 + pltpu.roll(t, HEAD_DIM // 2, axis=1) * sin_signed


def _in_proj_piece(name, xb, w_in, col_blocks, cos2, sin_fwd, h_so_far, block_cols, n_rope_blocks, own_shard=False,
                   after=None):
    S, D = xb.shape
    W = w_in.shape[1] * (N_DEV if own_shard else 1)
    tm = min(1024, S)
    n_blocks = col_blocks.shape[0]

    def body(cols_ref, x_ref, w_ref, cos_ref, sin_ref, *rest):
        h_ref = rest[-1]
        j = pl.program_id(1)

        @pl.when(cols_ref[j] < n_rope_blocks)
        def _():
            acc = _dot_nn(x_ref[...], w_ref[...])
            c, s = cos_ref[...], sin_ref[...]
            for hd in range(block_cols // HEAD_DIM):
                sl = slice(hd * HEAD_DIM, (hd + 1) * HEAD_DIM)
                h_ref[:, sl] = _rope_apply(acc[:, sl], c, s).astype(BF16)

        @pl.when(cols_ref[j] >= n_rope_blocks)
        def _():
            h_ref[...] = _dot_nn(x_ref[...], w_ref[...]).astype(BF16)

    row = pl.BlockSpec((tm, HEAD_DIM), lambda i, j, cols: (i, 0))
    carried = ([] if h_so_far is None else [h_so_far]) + ([] if after is None else [after])
    est = 2 * (tm * D * 2 + D * block_cols * 2 + tm * block_cols * 2 + 2 * tm * HEAD_DIM * 4) + 3 * tm * block_cols * 4
    return pl.pallas_call(
        body, name=name,
        grid_spec=pltpu.PrefetchScalarGridSpec(
            num_scalar_prefetch=1, grid=(S // tm, n_blocks),
            in_specs=[pl.BlockSpec((tm, D), lambda i, j, cols: (i, 0)),
                      pl.BlockSpec((D, block_cols), lambda i, j, cols: (0, 0 if own_shard else cols[j])), row, row]
                     + [pl.BlockSpec(memory_space=pl.ANY)] * len(carried),
            out_specs=pl.BlockSpec((tm, block_cols), lambda i, j, cols: (i, cols[j]))),
        out_shape=jax.ShapeDtypeStruct((S, W), BF16),
        input_output_aliases={} if h_so_far is None else {5: 0},
        compiler_params=pltpu.CompilerParams(dimension_semantics=("parallel", "arbitrary"),
                                             vmem_limit_bytes=_vmem_limit(est)),
    )(col_blocks, xb, w_in, cos2, sin_fwd, *carried)


def _attn_mask(mb):
    qi = lax.broadcasted_iota(jnp.int32, (SUB_BLOCK, 2 * SUB_BLOCK), 0)
    kj = lax.broadcasted_iota(jnp.int32, (SUB_BLOCK, 2 * SUB_BLOCK), 1)
    prev = jnp.logical_and(jnp.logical_and(kj < SUB_BLOCK, kj >= qi), mb > 0)
    cur = jnp.logical_and(kj >= SUB_BLOCK, kj - SUB_BLOCK <= qi)
    return jnp.logical_or(prev, cur)


def _both_blocks(prev_ref, cur_ref, sl):
    return jnp.concatenate([prev_ref[:, sl], cur_ref[:, sl]], axis=0)


def _dot_nt(a, b):
    return lax.dot_general(a, b, (((1,), (1,)), ((), ())), preferred_element_type=F32)


def _dot_tn(a, b):
    return lax.dot_general(a, b, (((0,), (0,)), ((), ())), preferred_element_type=F32)


def _dot_nn(a, b):
    return lax.dot_general(a, b, (((1,), (0,)), ((), ())), preferred_element_type=F32)


def _perm_matrix(d, to_residue_major):
    g = PERM_ROWS // d
    i = lax.broadcasted_iota(jnp.int32, (PERM_ROWS, PERM_ROWS), 0)
    j = lax.broadcasted_iota(jnp.int32, (PERM_ROWS, PERM_ROWS), 1)
    if to_residue_major:
        hit = j == (i % g) * d + i // g
    else:
        hit = j == (i % d) * g + i // d
    return hit.astype(BF16)


def _permute_rows(perm, x):
    if x.dtype == BF16:
        return _dot_nn(perm, x)
    hi = x.astype(BF16)
    r1 = x - hi.astype(F32)
    mid = r1.astype(BF16)
    lo = (r1 - mid.astype(F32)).astype(BF16)
    return (_dot_nn(perm, hi) + _dot_nn(perm, mid)) + _dot_nn(perm, lo)


def _rm_block(d, width):
    return pl.BlockSpec((d, PERM_ROWS // d, width), lambda i: (0, i, 0))


def _to_residue_major(name, x, col_block, width):
    S = x.shape[0]
    dils = [d for d in DILATIONS if d > 1]
    chunk = min(width, 1024)
    tiles_per_step = 2
    rows = tiles_per_step * PERM_ROWS

    def body(x_ref, *out_refs):
        for d, o_ref in zip(dils, out_refs):
            perm = _perm_matrix(d, True)
            g = PERM_ROWS // d
            for part in range(tiles_per_step):
                for c0 in range(0, width, chunk):
                    cw = min(chunk, width - c0)
                    y = _permute_rows(perm, x_ref[part * PERM_ROWS:(part + 1) * PERM_ROWS, c0:c0 + cw])
                    o_ref[:, part * g:(part + 1) * g, c0:c0 + cw] = y.astype(x.dtype).reshape(d, g, cw)

    return pl.pallas_call(
        body, name=name, grid=(S // rows,),
        in_specs=[pl.BlockSpec((rows, width), lambda i: (i, col_block))],
        out_specs=[pl.BlockSpec((d, rows // d, width), lambda i: (0, i, 0)) for d in dils],
        out_shape=[jax.ShapeDtypeStruct((d, S // d, width), x.dtype) for d in dils],
        compiler_params=pltpu.CompilerParams(dimension_semantics=("parallel",),
                                             vmem_limit_bytes=_vmem_limit(32 << 20)),
    )(x)


def _put_column(tile, col, value):
    lane = lax.broadcasted_iota(jnp.int32, tile.shape, 1)
    return jnp.where(lane == col, value, tile)


def _attn_fwd(qkv, d, aw):
    _, rows, _ = qkv.shape
    n_heads = aw // HEAD_DIM
    nb = rows // SUB_BLOCK

    def body(q_ref, kc_ref, vc_ref, o_ref, lse_ref, kp_ref, vp_ref, s_buf, p_buf):
        step = pl.program_id(0)

        @pl.when(step == 0)
        def _():
            kp_ref[...] = jnp.zeros_like(kp_ref)
            vp_ref[...] = jnp.zeros_like(vp_ref)

        mask = _attn_mask(step % nb)
        for hd in range(n_heads):
            sl = slice(hd * HEAD_DIM, (hd + 1) * HEAD_DIM)
            s_buf[hd] = _dot_nt(q_ref[:, sl], _both_blocks(kp_ref, kc_ref, sl))
        lse_tile = jnp.zeros((SUB_BLOCK, HEAD_DIM), F32)
        inv_tile = jnp.zeros((SUB_BLOCK, HEAD_DIM), F32)
        for hd in range(n_heads):
            s = jnp.where(mask, s_buf[hd] * ATTN_SCALE, NEG_BIG)
            m = jnp.max(s, axis=-1, keepdims=True)
            p = jnp.exp(s - m)
            l = jnp.sum(p, axis=-1, keepdims=True)
            p_buf[hd] = p.astype(BF16)
            lse_tile = _put_column(lse_tile, hd, m + jnp.log(l))
            inv_tile = _put_column(inv_tile, hd, 1.0 / l)
        lse_ref[...] = lse_tile
        for hd in range(n_heads):
            sl = slice(hd * HEAD_DIM, (hd + 1) * HEAD_DIM)
            o = _dot_nn(p_buf[hd], _both_blocks(vp_ref, vc_ref, sl))
            o_ref[:, sl] = (o * inv_tile[:, hd:hd + 1]).astype(BF16)
        kp_ref[...] = kc_ref[...]
        vp_ref[...] = vc_ref[...]

    def block(col, width):
        return pl.BlockSpec((SUB_BLOCK, width), lambda s: (s, col))

    qkv2 = qkv.reshape(d * rows, qkv.shape[-1])
    o, lse = pl.pallas_call(
        body, name=f"attn_fwd_d{d}", grid=(d * nb,),
        in_specs=[block(0, aw), block(1, aw), block(2, aw)],
        out_specs=[block(0, aw), block(0, HEAD_DIM)],
        out_shape=[jax.ShapeDtypeStruct((d * rows, aw), BF16), jax.ShapeDtypeStruct((d * rows, HEAD_DIM), F32)],
        scratch_shapes=[pltpu.VMEM((SUB_BLOCK, aw), BF16), pltpu.VMEM((SUB_BLOCK, aw), BF16),
                        pltpu.VMEM((n_heads, SUB_BLOCK, 2 * SUB_BLOCK), F32),
                        pltpu.VMEM((n_heads, SUB_BLOCK, 2 * SUB_BLOCK), BF16)],
        compiler_params=pltpu.CompilerParams(dimension_semantics=("arbitrary",),
                                             vmem_limit_bytes=_vmem_limit(16 << 20)),
    )(qkv2, qkv2, qkv2)
    return o.reshape(d, rows, aw), lse.reshape(d, rows, HEAD_DIM)


def _attn_combine(outs, lses, aw):
    S = outs[0].shape[1]
    n_heads = aw // HEAD_DIM
    n_pat = len(DILATIONS)

    def body(*refs):
        o_refs, l_refs = refs[:n_pat], refs[n_pat:2 * n_pat]
        o_ref, lt_ref = refs[2 * n_pat], refs[2 * n_pat + 1]
        o_nat, l_nat = [], []
        for d, o_r, l_r in zip(DILATIONS, o_refs, l_refs):
            o_p = o_r[...].reshape(PERM_ROWS, aw)
            l_p = l_r[...].reshape(PERM_ROWS, HEAD_DIM)
            if d > 1:
                perm = _perm_matrix(d, False)
                o_p, l_p = _permute_rows(perm, o_p), _permute_rows(perm, l_p)
            o_nat.append(o_p)
            l_nat.append(l_p)
        mx = functools.reduce(jnp.maximum, l_nat)
        es = [jnp.exp(l_p - mx) for l_p in l_nat]
        den = functools.reduce(jnp.add, es)
        lt_ref[...] = mx + jnp.log(den)
        ws = [e / den for e in es]
        for hd in range(n_heads):
            sl = slice(hd * HEAD_DIM, (hd + 1) * HEAD_DIM)
            o = ws[0][:, hd:hd + 1] * o_nat[0][:, sl]
            for pi in range(1, n_pat):
                o = o + ws[pi][:, hd:hd + 1] * o_nat[pi][:, sl]
            o_ref[:, sl] = o.astype(BF16)

    return pl.pallas_call(
        body, name="attn_combine", grid=(S // PERM_ROWS,),
        in_specs=[_rm_block(d, aw) for d in DILATIONS] + [_rm_block(d, HEAD_DIM) for d in DILATIONS],
        out_specs=[pl.BlockSpec((PERM_ROWS, aw), lambda i: (i, 0)), pl.BlockSpec((PERM_ROWS, HEAD_DIM), lambda i: (i, 0))],
        out_shape=[jax.ShapeDtypeStruct((S, aw), BF16), jax.ShapeDtypeStruct((S, HEAD_DIM), F32)],
        compiler_params=pltpu.CompilerParams(dimension_semantics=("parallel",),
                                             vmem_limit_bytes=_vmem_limit(40 << 20)),
    )(*outs, *lses)


def _band(tm, width, w, row_offset, transpose):
    t = lax.broadcasted_iota(jnp.int32, (tm, width), 0)
    u = lax.broadcasted_iota(jnp.int32, (tm, width), 1)
    dist = (u - t - row_offset) if transpose else (t + row_offset - u)
    return jnp.logical_and(dist >= 0, dist < w).astype(BF16)


def _pool_fwd(h, w_pool, pool_scale, pw, u_col_block):
    S, W = h.shape
    n_groups = len(POOL_WINDOWS)
    gw = pw // n_groups
    tm = min(512, S)
    halo_per_tile = tm // POOL_HALO

    def body(uc_ref, uh_ref, w_ref, sc_ref, p_ref, y_ref, pm_ref):
        i = pl.program_id(0)
        t_abs = i * tm + lax.broadcasted_iota(jnp.int32, (tm, 1), 0)
        for g, w in enumerate(POOL_WINDOWS):
            sl = slice(g * gw, (g + 1) * gw)
            uc = uc_ref[:, sl]
            uh = jnp.where(i > 0, uh_ref[:, sl], jnp.zeros((POOL_HALO, gw), BF16))
            ssum = _dot_nn(_band(tm, tm, w, 0, False), uc) + _dot_nn(_band(tm, POOL_HALO, w, POOL_HALO, False), uh)
            cnt = jnp.minimum(t_abs + 1, w).astype(F32)
            p = (ssum / cnt - uc.astype(F32)).astype(BF16)
            y = _dot_nn(p, w_ref[g])
            p_ref[:, sl] = p
            y_ref[:, sl] = y.astype(BF16)
            pm_ref[:, sl] = (y * sc_ref[:, sl]).astype(BF16)

    row = pl.BlockSpec((tm, pw), lambda i: (i, 0))
    return pl.pallas_call(
        body, name="pool_fwd", grid=(S // tm,),
        in_specs=[pl.BlockSpec((tm, pw), lambda i: (i, u_col_block)),
                  pl.BlockSpec((POOL_HALO, pw), lambda i: (jnp.maximum(i * halo_per_tile - 1, 0), u_col_block)),
                  pl.BlockSpec((n_groups, gw, gw), lambda i: (0, 0, 0)),
                  pl.BlockSpec((1, pw), lambda i: (0, 0))],
        out_specs=[row, row, row],
        out_shape=[jax.ShapeDtypeStruct((S, pw), BF16)] * 3,
        compiler_params=pltpu.CompilerParams(dimension_semantics=("parallel",),
                                             vmem_limit_bytes=_vmem_limit(24 << 20)),
    )(h, h, w_pool, pool_scale)


def _branch_attn(o_attn, w_ba):
    S, _ = o_attn.shape
    D = w_ba.shape[1]
    tm, tn = min(1024, S), D

    def epi(acc, ex, out, i, j):
        out[0][...] = acc.astype(BF16)

    (y,) = _mm("branch_attn", o_attn, w_ba, "nn", (tm, tn, 1024), [_tile_out((S, D), BF16, tm, tn)], epi)
    return y


def _branch_pool_merge(pm, w_bp, h, y_attn, gate_col0):
    S, _ = pm.shape
    D = w_bp.shape[1]
    tm, tn = min(512, S), D
    ga0, gp0 = gate_col0 // tn, (gate_col0 + D) // tn

    def epi(acc, ex, out, i, j):
        ga_ref, gp_ref, ya_ref = ex
        yp_ref, mg_ref = out
        yp = acc.astype(BF16)
        yp_ref[...] = yp
        mg = (jax.nn.sigmoid(ga_ref[...]).astype(F32) * ya_ref[...].astype(F32)
              + jax.nn.sigmoid(gp_ref[...]).astype(F32) * acc)
        mg_ref[...] = mg.astype(BF16)

    y_pool, merged = _mm(
        "branch_pool_merge", pm, w_bp, "nn", (tm, tn, 1024),
        [_tile_out((S, D), BF16, tm, tn), _tile_out((S, D), BF16, tm, tn)], epi,
        extras=[(h, (tm, tn), lambda i, j, k: (i, ga0 + j)), (h, (tm, tn), lambda i, j, k: (i, gp0 + j)),
                (y_attn, (tm, tn), lambda i, j, k: (i, j))])
    return y_pool, merged


def _layer_norm_rows(z, g, b):
    mu = jnp.mean(z, axis=-1, keepdims=True)
    zc = z - mu
    var = jnp.mean(zc * zc, axis=-1, keepdims=True)
    rstd = lax.rsqrt(var + LN_EPS)
    xhat = zc * rstd
    return xhat * g + b, xhat, rstd


def _out_proj_ln(merged, w_out, x, g, b):
    S, D = x.shape
    tm = min(256, S)

    def epi(acc, ex, out, i, j):
        x_ref, g_ref, b_ref = ex
        x1_ref, x1b_ref, xh_ref, rs_ref = out
        y, xhat, rstd = _layer_norm_rows(DEEPNORM_ALPHA * x_ref[...] + acc, g_ref[...], b_ref[...])
        x1_ref[...] = y
        x1b_ref[...] = y.astype(BF16)
        xh_ref[...] = xhat
        rs_ref[...] = jnp.broadcast_to(rstd, (tm, HEAD_DIM))

    row = lambda i, j, k: (i, 0)
    vec = lambda i, j, k: (0, 0)
    return _mm("out_proj_ln", merged, w_out, "nn", (tm, D, D),
               [((S, D), F32, (tm, D), row), ((S, D), BF16, (tm, D), row), ((S, D), F32, (tm, D), row),
                ((S, HEAD_DIM), F32, (tm, HEAD_DIM), row)], epi,
               extras=[(x, (tm, D), row), (g, (1, D), vec), (b, (1, D), vec)])


def _ffn_up(x1b, w1):
    S, D = x1b.shape
    F = w1.shape[1]
    tm, tn = min(1024, S), min(2048, F)

    def epi(acc, ex, out, i, j):
        r = jnp.maximum(acc, 0.0)
        out[0][...] = (r * r).astype(BF16)

    (a,) = _mm("ffn_up", x1b, w1, "nn", (tm, tn, 2 * K_TILE), [_tile_out((S, F), BF16, tm, tn)], epi)
    return a


def _residual_matmul(name, a, w, form, resid, after=None):
    S, D = resid.shape
    tm, tn = min(1024, S), min(2048, D)

    def epi(acc, ex, out, i, j):
        out[0][...] = DEEPNORM_ALPHA * ex[0][...] + acc

    (z,) = _mm(name, a, w, form, (tm, tn, K_TILE), [_tile_out((S, D), F32, tm, tn)], epi,
               extras=[(resid, (tm, tn), lambda i, j, k: (i, j))], after=after)
    return z


def _row_kernel(name, body, row_inputs, vec_inputs, row_outputs, sum_widths, tr):
    S = row_inputs[0].shape[0]
    row = lambda w: pl.BlockSpec((tr, w), lambda i: (i, 0))
    vec = lambda w: pl.BlockSpec((1, w), lambda i: (0, 0))

    def wrapped(*refs):
        body(pl.program_id(0), *refs)

    return pl.pallas_call(
        wrapped, name=name, grid=(S // tr,),
        in_specs=[row(t.shape[1]) for t in row_inputs] + [vec(t.shape[1]) for t in vec_inputs],
        out_specs=[row(w) for w, _ in row_outputs] + [vec(w) for w in sum_widths],
        out_shape=([jax.ShapeDtypeStruct((S, w), dt) for w, dt in row_outputs]
                   + [jax.ShapeDtypeStruct((1, w), F32) for w in sum_widths]),
        compiler_params=pltpu.CompilerParams(dimension_semantics=("arbitrary",),
                                             vmem_limit_bytes=_vmem_limit(40 << 20)),
    )(*row_inputs, *vec_inputs)


def _ln_loss_bwd(z2, g, b, target):
    S, D = z2.shape

    tr = min(256, S)
    chunk = min(ROW_CHUNK, tr)

    def body(i, z_ref, t_ref, g_ref, b_ref, dz_ref, dzb_ref, dg_ref, db_ref, loss_ref):
        gv, bv = g_ref[...], b_ref[...]
        dg = db = loss = None
        for c0 in range(0, tr, chunk):
            rows = pl.ds(c0, chunk)
            y, xhat, rstd = _layer_norm_rows(z_ref[rows, :], gv, bv)
            err = y - t_ref[rows, :]
            part = 0.5 * jnp.sum(jnp.mean(err * err, axis=-1, keepdims=True), axis=0, keepdims=True)
            dy = err * (1.0 / D)
            dz = _layer_norm_bwd(dy, xhat, rstd, gv)
            dz_ref[rows, :] = dz
            dzb_ref[rows, :] = dz.astype(BF16)
            dg_c, db_c = jnp.sum(dy * xhat, axis=0, keepdims=True), jnp.sum(dy, axis=0, keepdims=True)
            dg, db, loss = (dg_c, db_c, part) if dg is None else (dg + dg_c, db + db_c, loss + part)
        _accumulate_rows(dg_ref, dg, i)
        _accumulate_rows(db_ref, db, i)
        _accumulate_rows(loss_ref, jnp.broadcast_to(loss, (1, HEAD_DIM)), i)

    return _row_kernel("ln_loss_bwd", body, [z2, target], [g, b], [(D, F32), (D, BF16)], [D, D, HEAD_DIM], tr)


def _ln_bwd(dy, xhat, rstd, g):
    S, D = dy.shape

    def body(i, dy_ref, xh_ref, rs_ref, g_ref, dz_ref, dzb_ref, dg_ref, db_ref):
        dyv, xhat_v = dy_ref[...], xh_ref[...]
        dz = _layer_norm_bwd(dyv, xhat_v, rs_ref[:, :1], g_ref[...])
        dz_ref[...] = dz
        dzb_ref[...] = dz.astype(BF16)
        _accumulate_rows(dg_ref, jnp.sum(dyv * xhat_v, axis=0, keepdims=True), i)
        _accumulate_rows(db_ref, jnp.sum(dyv, axis=0, keepdims=True), i)

    return _row_kernel("ln_bwd", body, [dy, xhat, rstd], [g], [(D, F32), (D, BF16)], [D, D], min(256, S))


def _grad_weight(name, act, cot):
    M, N = act.shape[1], cot.shape[1]
    tm, tn = min(1024, M), min(2048, N)

    def epi(acc, ex, out, i, j):
        out[0][...] = acc.astype(BF16)

    (g,) = _mm(name, act, cot, "tn", (tm, tn, 2 * K_TILE), [_tile_out((M, N), BF16, tm, tn)], epi)
    return g


def _ffn_down_bwd(dz2b, w2, a, after=None):
    S, D = dz2b.shape
    F = w2.shape[0]
    tm, tn = min(1024, S), min(2048, F)

    def epi(acc, ex, out, i, j):
        out[0][...] = (acc * (2.0 * jnp.sqrt(ex[0][...])).astype(F32)).astype(BF16)

    (dh1,) = _mm("ffn_down_bwd", dz2b, w2, "nt", (tm, tn, 2 * K_TILE), [_tile_out((S, F), BF16, tm, tn)], epi,
                 extras=[(a, (tm, tn), lambda i, j, k: (i, j))], after=after)
    return dh1


def _out_proj_bwd(dz1b, w_out, h, y_attn, y_pool, gate_col0, after=None):
    S, D = dz1b.shape
    W = h.shape[1]
    tm = min(256, S)
    assert gate_col0 == 2 * D and W == 4 * D

    def epi(acc, ex, out, i, j):
        gates_ref, ya_ref, yp_ref = ex
        dya_ref, dyp_ref, dh_ref = out
        sa = jax.nn.sigmoid(gates_ref[:, :D])
        sp = jax.nn.sigmoid(gates_ref[:, D:])
        dya_ref[...] = (acc * sa.astype(F32)).astype(BF16)
        dyp_ref[...] = (acc * sp.astype(F32)).astype(BF16)
        dh_ref[:, :D] = (acc * (ya_ref[...] * (sa * (1.0 - sa))).astype(F32)).astype(BF16)
        dh_ref[:, D:] = (acc * (yp_ref[...] * (sp * (1.0 - sp))).astype(F32)).astype(BF16)

    row = lambda i, j, k: (i, 0)
    return _mm("out_proj_bwd", dz1b, w_out, "nt", (tm, D, D),
               [((S, D), BF16, (tm, D), row), ((S, D), BF16, (tm, D), row),
                ((S, W), BF16, (tm, 2 * D), lambda i, j, k: (i, 1))], epi,
               extras=[(h, (tm, 2 * D), lambda i, j, k: (i, 1)), (y_attn, (tm, D), row), (y_pool, (tm, D), row)],
               after=after)


def _branch_attn_bwd(dy_attn, w_ba, o_attn, l_tot, after=None):
    S, D = dy_attn.shape
    aw = w_ba.shape[0]
    n_heads = aw // HEAD_DIM
    tm = min(512, S)
    dils = [d for d in DILATIONS if d > 1]

    def epi(acc, ex, out, i, j):
        do_ref, st_ref = out[0], out[1]
        do = acc.astype(BF16)
        do_ref[...] = do
        o = ex[0][...].astype(F32)
        stats = ex[1][...]
        for hd in range(n_heads):
            sl = slice(hd * HEAD_DIM, (hd + 1) * HEAD_DIM)
            stats = _put_column(stats, n_heads + hd, jnp.sum(acc[:, sl] * o[:, sl], axis=-1, keepdims=True))
        st_ref[...] = stats
        for di, d in enumerate(dils):
            perm = _perm_matrix(d, True)
            g = PERM_ROWS // d
            for part in range(tm // PERM_ROWS):
                rows = slice(part * PERM_ROWS, (part + 1) * PERM_ROWS)
                to = slice(part * g, (part + 1) * g)
                out[2 + di][:, to, :] = _permute_rows(perm, do[rows]).astype(BF16).reshape(d, g, aw)
                out[2 + len(dils) + di][:, to, :] = _permute_rows(perm, stats[rows]).reshape(d, g, HEAD_DIM)

    row = lambda i, j, k: (i, 0)
    rm = lambda i, j, k: (0, i, 0)
    res = _mm("branch_attn_bwd", dy_attn, w_ba, "nt", (tm, aw, D),
              [((S, aw), BF16, (tm, aw), row), ((S, HEAD_DIM), F32, (tm, HEAD_DIM), row)]
              + [((d, S // d, aw), BF16, (d, tm // d, aw), rm) for d in dils]
              + [((d, S // d, HEAD_DIM), F32, (d, tm // d, HEAD_DIM), rm) for d in dils], epi,
              extras=[(o_attn, (tm, aw), row), (l_tot, (tm, HEAD_DIM), row)], after=after)
    n = len(dils)
    return ({1: res[0][None], **dict(zip(dils, res[2:2 + n]))}, {1: res[1][None], **dict(zip(dils, res[2 + n:]))})


def _branch_pool_bwd(dy_pool, w_bp, y_pre, pool_scale):
    S, D = dy_pool.shape
    pw = w_bp.shape[0]
    tm = min(512, S)

    def epi(acc, ex, out, i, j):
        y_ref, sc_ref = ex
        dyp_ref, dsc_ref = out
        dyp_ref[...] = (acc * sc_ref[...]).astype(BF16)
        _accumulate_rows(dsc_ref, jnp.sum(acc * y_ref[...].astype(F32), axis=0, keepdims=True), i)

    row = lambda i, j, k: (i, 0)
    return _mm("branch_pool_bwd", dy_pool, w_bp, "nt", (tm, pw, D),
               [((S, pw), BF16, (tm, pw), row), _row_sum_out(pw)], epi,
               extras=[(y_pre, (tm, pw), row), (pool_scale, (1, pw), lambda i, j, k: (0, 0))],
               sequential=True)


def _pool_bwd(dh, dy_pre, p, w_pool, pw, u_col_block):
    S, W = dh.shape
    n_groups = len(POOL_WINDOWS)
    gw = pw // n_groups
    tm = min(512, S)
    n_tiles = S // tm
    halo_per_tile = tm // POOL_HALO
    n_halo_blocks = S // POOL_HALO

    def body(dh_in_ref, dyc_ref, dyh_ref, p_ref, w_ref, dh_ref, dw_ref):
        del dh_in_ref
        i = pl.program_id(0)
        t_cur = i * tm + lax.broadcasted_iota(jnp.int32, (tm, 1), 0)
        t_halo = (i + 1) * tm + lax.broadcasted_iota(jnp.int32, (POOL_HALO, 1), 0)
        for g, w in enumerate(POOL_WINDOWS):
            sl = slice(g * gw, (g + 1) * gw)
            wg = w_ref[g]
            dyc = dyc_ref[:, sl]
            dyh = jnp.where(i < n_tiles - 1, dyh_ref[:, sl], jnp.zeros((POOL_HALO, gw), BF16))
            dp_cur = _dot_nt(dyc, wg)
            dp_halo = _dot_nt(dyh, wg)
            dpc_cur = (dp_cur / jnp.minimum(t_cur + 1, w).astype(F32)).astype(BF16)
            dpc_halo = (dp_halo / jnp.minimum(t_halo + 1, w).astype(F32)).astype(BF16)
            du = (_dot_nn(_band(tm, tm, w, 0, True), dpc_cur)
                  + _dot_nn(_band(tm, POOL_HALO, w, -tm, True), dpc_halo) - dp_cur)
            dh_ref[:, sl] = du.astype(BF16)
            dw = _dot_tn(p_ref[:, sl], dyc)

            @pl.when(i == 0)
            def _():
                dw_ref[g] = dw

            @pl.when(i > 0)
            def _():
                dw_ref[g] += dw

    row = pl.BlockSpec((tm, pw), lambda i: (i, 0))
    dh_new, dw_pool = pl.pallas_call(
        body, name="pool_bwd", grid=(n_tiles,),
        in_specs=[pl.BlockSpec(memory_space=pl.ANY), row,
                  pl.BlockSpec((POOL_HALO, pw), lambda i: (jnp.minimum((i + 1) * halo_per_tile, n_halo_blocks - 1), 0)),
                  row, pl.BlockSpec((n_groups, gw, gw), lambda i: (0, 0, 0))],
        out_specs=[pl.BlockSpec((tm, pw), lambda i: (i, u_col_block)),
                   pl.BlockSpec((n_groups, gw, gw), lambda i: (0, 0, 0))],
        out_shape=[jax.ShapeDtypeStruct((S, W), BF16), jax.ShapeDtypeStruct((n_groups, gw, gw), F32)],
        input_output_aliases={0: 0},
        compiler_params=pltpu.CompilerParams(dimension_semantics=("arbitrary",),
                                             vmem_limit_bytes=_vmem_limit(24 << 20)),
    )(dh, dy_pre, dy_pre, p, w_pool)
    return dh_new, dw_pool


def _attn_bwd(qkv, d_out, stats, d, aw):
    _, rows, _ = qkv.shape
    n_heads = aw // HEAD_DIM
    nb = rows // SUB_BLOCK
    n_blocks = d * nb

    def body(q_ref, kp_ref, kc_ref, vp_ref, vc_ref, do_ref, st_ref, dq_ref, dk_ref, dv_ref,
             carry_k, carry_v, s_buf, dp_buf, p_buf, ds_buf):
        step = pl.program_id(0)

        @pl.when(step == 0)
        def _():
            carry_k[...] = jnp.zeros_like(carry_k)
            carry_v[...] = jnp.zeros_like(carry_v)

        @pl.when(step < n_blocks)
        def _():
            mask = _attn_mask(step % nb)
            st = st_ref[...]
            for hd in range(n_heads):
                sl = slice(hd * HEAD_DIM, (hd + 1) * HEAD_DIM)
                s_buf[hd] = _dot_nt(q_ref[:, sl], _both_blocks(kp_ref, kc_ref, sl))
                dp_buf[hd] = _dot_nt(do_ref[:, sl], _both_blocks(vp_ref, vc_ref, sl))
            for hd in range(n_heads):
                lt, dl = st[:, hd:hd + 1], st[:, n_heads + hd:n_heads + hd + 1]
                p = jnp.where(mask, jnp.exp(jnp.where(mask, s_buf[hd] * ATTN_SCALE - lt, NEG_BIG)), 0.0)
                p_buf[hd] = p.astype(BF16)
                ds_buf[hd] = (p * (dp_buf[hd] - dl) * ATTN_SCALE).astype(BF16)
            for hd in range(n_heads):
                sl = slice(hd * HEAD_DIM, (hd + 1) * HEAD_DIM)
                dq_ref[:, sl] = _dot_nn(ds_buf[hd], _both_blocks(kp_ref, kc_ref, sl)).astype(BF16)
                dk_both = _dot_tn(ds_buf[hd], q_ref[:, sl])
                dv_both = _dot_tn(p_buf[hd], do_ref[:, sl])
                dk_ref[:, sl] = (carry_k[:, sl] + dk_both[:SUB_BLOCK]).astype(BF16)
                dv_ref[:, sl] = (carry_v[:, sl] + dv_both[:SUB_BLOCK]).astype(BF16)
                carry_k[:, sl] = dk_both[SUB_BLOCK:]
                carry_v[:, sl] = dv_both[SUB_BLOCK:]

        @pl.when(step == n_blocks)
        def _():
            dk_ref[...] = carry_k[...].astype(BF16)
            dv_ref[...] = carry_v[...].astype(BF16)

    def cur(step):
        return jnp.minimum(step, n_blocks - 1)

    def qkv_spec(col, prev):
        if prev:
            return pl.BlockSpec((SUB_BLOCK, aw), lambda s: (jnp.maximum(cur(s) - 1, 0), col))
        return pl.BlockSpec((SUB_BLOCK, aw), lambda s: (cur(s), col))

    def at_cur(w):
        return pl.BlockSpec((SUB_BLOCK, w), lambda s: (cur(s), 0))

    finished = pl.BlockSpec((SUB_BLOCK, aw), lambda s: (jnp.maximum(s - 1, 0), 0))
    pair = (n_heads, SUB_BLOCK, 2 * SUB_BLOCK)
    flat = lambda t: t.reshape(d * rows, t.shape[-1])
    qkv2 = flat(qkv)
    outs = pl.pallas_call(
        body, name=f"attn_bwd_d{d}", grid=(n_blocks + 1,),
        in_specs=[qkv_spec(0, False), qkv_spec(1, True), qkv_spec(1, False), qkv_spec(2, True), qkv_spec(2, False),
                  at_cur(aw), at_cur(HEAD_DIM)],
        out_specs=[at_cur(aw), finished, finished],
        out_shape=[jax.ShapeDtypeStruct((d * rows, aw), BF16)] * 3,
        scratch_shapes=[pltpu.VMEM((SUB_BLOCK, aw), F32), pltpu.VMEM((SUB_BLOCK, aw), F32),
                        pltpu.VMEM(pair, F32), pltpu.VMEM(pair, F32), pltpu.VMEM(pair, BF16), pltpu.VMEM(pair, BF16)],
        compiler_params=pltpu.CompilerParams(dimension_semantics=("arbitrary",),
                                             vmem_limit_bytes=_vmem_limit(24 << 20)),
    )(qkv2, qkv2, qkv2, qkv2, qkv2, flat(d_out), flat(stats))
    return [t.reshape(d, rows, aw) for t in outs]


def _attn_bwd_finish(dh, per_pattern, cos2, sin_bwd, aw):
    S, W = dh.shape
    n_heads = aw // HEAD_DIM
    n_pat = len(DILATIONS)

    def body(*refs):
        grad_refs = refs[1:1 + 3 * n_pat]
        cos_ref, sin_ref = refs[1 + 3 * n_pat], refs[2 + 3 * n_pat]
        out_ref = refs[3 + 3 * n_pat]
        perms = {d: _perm_matrix(d, False) for d in DILATIONS if d > 1}
        totals = []
        for which in range(3):
            tot = None
            for pi, d in enumerate(DILATIONS):
                g = grad_refs[which * n_pat + pi][...].reshape(PERM_ROWS, aw)
                g = _permute_rows(perms[d], g) if d > 1 else g.astype(F32)
                tot = g if tot is None else tot + g
            totals.append(tot)
        dq, dk, dv = totals
        c, s = cos_ref[...], sin_ref[...]
        for hd in range(n_heads):
            sl = slice(hd * HEAD_DIM, (hd + 1) * HEAD_DIM)
            out_ref[:, sl] = _rope_apply(dq[:, sl], c, s).astype(BF16)
            out_ref[:, aw + hd * HEAD_DIM:aw + (hd + 1) * HEAD_DIM] = _rope_apply(dk[:, sl], c, s).astype(BF16)
        out_ref[:, 2 * aw:] = dv.astype(BF16)

    grads = [pp[which] for which in range(3) for pp in per_pattern]
    rope_spec = pl.BlockSpec((PERM_ROWS, HEAD_DIM), lambda i: (i, 0))
    return pl.pallas_call(
        body, name="attn_bwd_finish", grid=(S // PERM_ROWS,),
        in_specs=([pl.BlockSpec(memory_space=pl.ANY)] + [_rm_block(d, aw) for d in DILATIONS] * 3
                  + [rope_spec, rope_spec]),
        out_specs=pl.BlockSpec((PERM_ROWS, 3 * aw), lambda i: (i, 0)),
        out_shape=jax.ShapeDtypeStruct((S, W), BF16),
        input_output_aliases={0: 0},
        compiler_params=pltpu.CompilerParams(dimension_semantics=("parallel",),
                                             vmem_limit_bytes=_vmem_limit(32 << 20)),
    )(dh, *grads, cos2, sin_bwd)


def _my_place():
    x, y, c = lax.axis_index("x"), lax.axis_index("y"), lax.axis_index("c")
    return x, y, c


def _flat(px, py, pc):
    return 4 * px + 2 * py + pc


def _shard_slice(ref, axis, idx, size):
    start = pl.multiple_of(idx * size, size)
    ix = [slice(None)] * len(ref.shape)
    ix[axis] = pl.ds(start, size)
    return ref.at[tuple(ix)]


_HBM_SPEC = pl.BlockSpec(memory_space=pltpu.HBM)
_SEM_SPEC = pl.BlockSpec(memory_space=pltpu.SEMAPHORE)
_ANY_SPEC = pl.BlockSpec(memory_space=pl.ANY)
_N_PEER = N_DEV - 1
SIBLING, SAME_CORE_NEIGHBOURS, OTHER_CORE_NEIGHBOURS, DIAGONAL = (1,), (2, 4), (3, 5), (6, 7)
PEER_ORDER = SIBLING + SAME_CORE_NEIGHBOURS + OTHER_CORE_NEIGHBOURS + DIAGONAL


def _peer_of(x, y, c, r):
    return (x ^ ((r >> 2) & 1), y ^ ((r >> 1) & 1), c ^ (r & 1))


class _Exchange:
    def __init__(self, name, part, slot):
        self.name, self.part, self.slot = name, part, slot

    def _copy(self, w, r, src, land, send_sems, recv_sems, sending):
        x, y, c = _my_place()
        peer = _peer_of(x, y, c, r)
        return pltpu.make_async_remote_copy(
            src_ref=self.part(w, src, _flat(*peer)),
            dst_ref=self.slot(w, land, _flat(x, y, c) if sending else _flat(*peer)),
            send_sem=send_sems.at[w * _N_PEER + r - 1], recv_sem=recv_sems.at[w * _N_PEER + r - 1],
            device_id=peer, device_id_type=MESH)

    def start(self, srcs, lands, after=None):
        n = len(srcs)
        n_after = 0 if after is None else 1

        def body(*refs):
            src, land = refs[:n], refs[n:2 * n]
            outs = refs[2 * n + n_after:]
            send_sems, recv_sems, local_sems, token = outs[0], outs[1], outs[2], outs[3 + 2 * n]
            for w in range(n):
                self._own_copy(w, src[w], land[w], local_sems).start()
                for r in PEER_ORDER:
                    self._copy(w, r, src[w], land[w], send_sems, recv_sems, True).start()
            token[...] = jnp.zeros_like(token)

        sems = pltpu.SemaphoreType.DMA((n * _N_PEER,))
        outs = pl.pallas_call(
            body, name=self.name + "_start",
            out_shape=(sems, sems, pltpu.SemaphoreType.DMA((n,)),
                       *[pltpu.HBM(t.shape, t.dtype) for t in list(srcs) + list(lands)],
                       jax.ShapeDtypeStruct((8, 128), F32)),
            in_specs=[_HBM_SPEC] * (2 * n) + [_ANY_SPEC] * n_after,
            out_specs=(_SEM_SPEC, _SEM_SPEC, _SEM_SPEC, *[_HBM_SPEC] * (2 * n), pl.BlockSpec(memory_space=pltpu.VMEM)),
            input_output_aliases={i: 3 + i for i in range(2 * n)},
            compiler_params=pltpu.CompilerParams(has_side_effects=pltpu.SideEffectType.DATAFLOW_SIDE_EFFECTING),
        )(*[pltpu.with_memory_space_constraint(t, pltpu.HBM) for t in list(srcs) + list(lands)],
          *([after] if n_after else []))
        return outs[0], outs[1], outs[2], outs[3:3 + n], outs[3 + n:3 + 2 * n], outs[3 + 2 * n]

    def _own_copy(self, w, src, land, local_sems):
        me = _flat(*_my_place())
        return pltpu.make_async_copy(self.part(w, src, me), self.slot(w, land, me), local_sems.at[w])

    def wait(self, started, after, peers=PEER_ORDER, own=True, tag=""):
        send_sems, recv_sems, local_sems, srcs, lands, token = started
        n = len(srcs)

        def body(*refs):
            src, land = refs[:n], refs[n:2 * n]
            s_sems, r_sems, l_sems = refs[2 * n], refs[2 * n + 1], refs[2 * n + 2]
            for w in range(n):
                if own:
                    self._own_copy(w, src[w], land[w], l_sems).wait()
                for r in peers:
                    cp = self._copy(w, r, src[w], land[w], s_sems, r_sems, False)
                    cp.wait_send()
                    cp.wait_recv()

        outs = pl.pallas_call(
            body, name=self.name + "_wait" + tag,
            out_shape=[pltpu.HBM(t.shape, t.dtype) for t in list(srcs) + list(lands)],
            in_specs=[_HBM_SPEC] * (2 * n) + [_SEM_SPEC, _SEM_SPEC, _SEM_SPEC, _ANY_SPEC],
            out_specs=[_HBM_SPEC] * (2 * n),
            input_output_aliases={i: i for i in range(2 * n)},
            compiler_params=pltpu.CompilerParams(has_side_effects=pltpu.SideEffectType.DATAFLOW_SIDE_EFFECTING),
        )(*srcs, *lands, send_sems, recv_sems, local_sems, after)
        return outs[n:], (send_sems, recv_sems, local_sems, outs[:n], outs[n:], token)


DIRECT_PEERS = (1, 2, 4, 6)
FORWARDED = (3, 5, 7)


class _TwoLevelGather:
    def __init__(self, name, axes, sizes):
        self.name, self.axes, self.sizes = name, axes, sizes

    def _place(self, w, land, dev):
        return _shard_slice(land, self.axes[w], dev, self.sizes[w])

    def _direct(self, w, r, src, land, sems, sending):
        x, y, c = _my_place()
        peer = _peer_of(x, y, c, r)
        k = w * len(DIRECT_PEERS) + DIRECT_PEERS.index(r)
        return pltpu.make_async_remote_copy(
            src_ref=src, dst_ref=self._place(w, land, _flat(x, y, c) if sending else _flat(*peer)),
            send_sem=sems[0].at[k], recv_sem=sems[1].at[k], device_id=peer, device_id_type=MESH)

    def _passed_on(self, w, f, land, sems, sending):
        x, y, c = _my_place()
        owner = _flat(*_peer_of(x, y, c, (f ^ 1) if sending else f))
        slot = self._place(w, land, owner)
        k = w * len(FORWARDED) + FORWARDED.index(f)
        return pltpu.make_async_remote_copy(
            src_ref=slot, dst_ref=slot, send_sem=sems[2].at[k], recv_sem=sems[3].at[k],
            device_id=(x, y, 1 - c), device_id_type=MESH)

    def _own(self, w, src, land, sems):
        return pltpu.make_async_copy(src, self._place(w, land, _flat(*_my_place())), sems[4].at[w])

    def _call(self, suffix, body, sems, srcs, lands, after, make_sems):
        n = len(srcs)
        n_after = 0 if after is None else 1
        bufs = list(srcs) + list(lands)

        def wrapped(*refs):
            ins = refs[:2 * n]
            rest = refs[2 * n + (n_after if make_sems else 0):]
            body(ins[:n], ins[n:], rest[:5], rest[-1] if make_sems else None)

        buf_shapes = [pltpu.HBM(t.shape, t.dtype) for t in bufs]
        if make_sems:
            sem_types = [pltpu.SemaphoreType.DMA((n * len(DIRECT_PEERS),))] * 2 \
                + [pltpu.SemaphoreType.DMA((n * len(FORWARDED),))] * 2 + [pltpu.SemaphoreType.DMA((n,))]
            outs = pl.pallas_call(
                wrapped, name=self.name + suffix,
                out_shape=(*sem_types, *buf_shapes, jax.ShapeDtypeStruct((8, 128), F32)),
                in_specs=[_HBM_SPEC] * (2 * n) + [_ANY_SPEC] * n_after,
                out_specs=(*[_SEM_SPEC] * 5, *[_HBM_SPEC] * (2 * n), pl.BlockSpec(memory_space=pltpu.VMEM)),
                input_output_aliases={i: 5 + i for i in range(2 * n)},
                compiler_params=pltpu.CompilerParams(has_side_effects=pltpu.SideEffectType.DATAFLOW_SIDE_EFFECTING),
            )(*[pltpu.with_memory_space_constraint(t, pltpu.HBM) for t in bufs], *([after] if n_after else []))
            return tuple(outs[:5]), outs[5:5 + n], outs[5 + n:5 + 2 * n], outs[5 + 2 * n]
        outs = pl.pallas_call(
            wrapped, name=self.name + suffix,
            out_shape=buf_shapes,
            in_specs=[_HBM_SPEC] * (2 * n) + [_SEM_SPEC] * 5 + [_ANY_SPEC] * n_after,
            out_specs=[_HBM_SPEC] * (2 * n),
            input_output_aliases={i: i for i in range(2 * n)},
            compiler_params=pltpu.CompilerParams(has_side_effects=pltpu.SideEffectType.DATAFLOW_SIDE_EFFECTING),
        )(*bufs, *sems, *([after] if n_after else []))
        return sems, outs[:n], outs[n:], None

    def start(self, srcs, lands, after=None):
        n = len(srcs)

        def body(src, land, sems, token):
            for w in range(n):
                self._own(w, src[w], land[w], sems).start()
                for r in DIRECT_PEERS:
                    self._direct(w, r, src[w], land[w], sems, True).start()
            token[...] = jnp.zeros_like(token)

        return self._call("_start", body, None, srcs, lands, after, True)

    def forward(self, state, after, which, tag=""):
        sems, srcs, lands, token = state
        n = len(srcs)

        def body(src, land, sem_refs, _):
            for w in range(n):
                for r in which:
                    self._direct(w, r, src[w], land[w], sem_refs, False).wait_recv()
                    self._passed_on(w, r | 1, land[w], sem_refs, True).start()

        sems, srcs, lands, _ = self._call("_forward" + tag, body, sems, srcs, lands, after, False)
        return sems, srcs, lands, token

    def wait(self, state, after, direct=(), passed_on=(), sends=False, tag=""):
        sems, srcs, lands, token = state
        n = len(srcs)

        def body(src, land, sem_refs, _):
            for w in range(n):
                for r in direct:
                    self._direct(w, r, src[w], land[w], sem_refs, False).wait_recv()
                for f in passed_on:
                    self._passed_on(w, f, land[w], sem_refs, False).wait_recv()
                if sends:
                    self._own(w, src[w], land[w], sem_refs).wait()
                    for r in DIRECT_PEERS:
                        self._direct(w, r, src[w], land[w], sem_refs, True).wait_send()
                    for f in FORWARDED:
                        self._passed_on(w, f, land[w], sem_refs, True).wait_send()

        sems, srcs, lands, _ = self._call("_wait" + tag, body, sems, srcs, lands, after, False)
        return lands, (sems, srcs, lands, token)


def _scatter_exchange(name, axes, shard_sizes):
    def part(w, src, dev):
        return src if axes[w] is None else _shard_slice(src, axes[w], dev, shard_sizes[w])
    return _Exchange(name, part, lambda w, land, dev: land.at[dev])


def _adamw(name, partials, w, m, v):
    R, C = w.shape
    tr = R
    while tr * C * 4 > (1 << 20) and tr % 16 == 0:
        tr //= 2

    def body(p_ref, w_ref, m_ref, v_ref, g_ref, d_ref, nm_ref, nv_ref):
        g = p_ref[0].astype(F32)
        for jdev in range(1, N_DEV):
            g = g + p_ref[jdev].astype(F32)
        nm = ADAM_B1 * m_ref[...] + (1.0 - ADAM_B1) * g
        nv = ADAM_B2 * v_ref[...] + (1.0 - ADAM_B2) * (g * g)
        m_hat = nm / (1.0 - ADAM_B1 ** ADAM_STEP)
        v_hat = nv / (1.0 - ADAM_B2 ** ADAM_STEP)
        g_ref[...] = g
        d_ref[...] = -ADAM_LR * (m_hat / (jnp.sqrt(v_hat) + ADAM_EPS) + ADAM_WD * w_ref[...])
        nm_ref[...] = nm
        nv_ref[...] = nv

    spec = pl.BlockSpec((tr, C), lambda i: (i, 0))
    return pl.pallas_call(
        body, name=name, grid=(R // tr,),
        in_specs=[pl.BlockSpec((N_DEV, tr, C), lambda i: (0, i, 0)), spec, spec, spec],
        out_specs=[spec] * 4,
        out_shape=[jax.ShapeDtypeStruct((R, C), F32)] * 4,
        compiler_params=pltpu.CompilerParams(dimension_semantics=("parallel",),
                                             vmem_limit_bytes=_vmem_limit(24 << 20)),
    )(partials, w, m, v)


def _local_step(x, cos2, sin_fwd, sin_bwd, project_in, mix_weights, ffn_weights, pool_scale, g_mix, b_mix, g_ff, b_ff,
                target, send):
    S, D = x.shape
    aw = pw = D // 2
    u_col_block = 3
    gate_col0 = 4 * aw

    xb = x.astype(BF16)
    h, w_in = project_in(xb)
    dilated = [d for d in DILATIONS if d > 1]
    qkv = {1: h[None], **dict(zip(dilated, _to_residue_major("qkv_to_rm", h, 0, 3 * aw)))}
    fwd = [_attn_fwd(qkv[d], d, aw) for d in DILATIONS]
    o_attn, l_tot = _attn_combine([f[0] for f in fwd], [f[1] for f in fwd], aw)
    w_pool, w_ba, w_bp, w_out = mix_weights(o_attn)
    p, y_pre, pm = _pool_fwd(h, w_pool, pool_scale, pw, u_col_block)
    y_attn = _branch_attn(o_attn, w_ba)
    y_pool, merged = _branch_pool_merge(pm, w_bp, h, y_attn, gate_col0)
    w1, w2 = ffn_weights(merged)
    x1, x1b, xhat1, rstd1 = _out_proj_ln(merged, w_out, x, g_mix, b_mix)
    a = _ffn_up(x1b, w1)
    z2 = _residual_matmul("ffn_down", a, w2, "nn", x1)
    dz2, dz2b, dg_ff, db_ff, loss = _ln_loss_bwd(z2, g_ff, b_ff, target)

    tok = send("ff2", [_grad_weight("grad_w_ff2", a, dz2b)])
    dh1 = _ffn_down_bwd(dz2b, w2, a, after=tok)
    tok = send("ff1", [_grad_weight("grad_w_ff1", x1b, dh1)])
    dy1 = _residual_matmul("ffn_up_bwd", dh1, w1, "nt", dz2, after=tok)
    dz1, dz1b, dg_mix, db_mix = _ln_bwd(dy1, xhat1, rstd1, g_mix)
    tok = send("out", [_grad_weight("grad_w_out", merged, dz1b)])
    dy_attn, dy_pool, dh = _out_proj_bwd(dz1b, w_out, h, y_attn, y_pool, gate_col0, after=tok)
    tok = send("branch", [_grad_weight("grad_w_branch_attn", o_attn, dy_attn),
                          _grad_weight("grad_w_branch_pool", pm, dy_pool)])
    d_outs, statss = _branch_attn_bwd(dy_attn, w_ba, o_attn, l_tot, after=tok)
    dy_pre, d_scale = _branch_pool_bwd(dy_pool, w_bp, y_pre, pool_scale)
    dh, dw_pool = _pool_bwd(dh, dy_pre, p, w_pool, pw, u_col_block)
    per_pattern = [_attn_bwd(qkv[d], d_outs[d], statss[d], d, aw) for d in DILATIONS]
    dh = _attn_bwd_finish(dh, per_pattern, cos2, sin_bwd, aw)
    small = jnp.concatenate((d_scale, dg_mix, db_mix, dg_ff, db_ff), axis=-1)
    tok = send("in", [_grad_weight("grad_w_in", xb, dh), dw_pool.astype(BF16),
                      small.reshape(small.shape[-1] // HEAD_DIM, HEAD_DIM)])
    grad_x = _residual_matmul("in_proj_bwd", dh, w_in, "nt", dz1, after=tok)
    return loss, grad_x


def _rope_tables(positions):
    half = HEAD_DIM // 2
    inv_freq = ROPE_THETA ** (-jnp.arange(half, dtype=F32) / half)
    ang = positions.astype(F32)[:, None] * inv_freq
    cos, sin = jnp.cos(ang), jnp.sin(ang)
    cos2 = jnp.concatenate([cos, cos], axis=-1)
    sin_fwd = jnp.concatenate([-sin, sin], axis=-1)
    return cos2, sin_fwd, -sin_fwd


def kernel(x, positions, w_in, w_pool, pool_scale, w_branch_attn, w_branch_pool, w_out, ln_mix_g, ln_mix_b, w_ff1, w_ff2, ln_ff_g, ln_ff_b, loss_target, m_w_in, m_w_pool, m_pool_scale, m_w_branch_attn, m_w_branch_pool, m_w_out, m_ln_mix_g, m_ln_mix_b, m_w_ff1, m_w_ff2, m_ln_ff_g, m_ln_ff_b, v_w_in, v_w_pool, v_pool_scale, v_w_branch_attn, v_w_branch_pool, v_w_out, v_ln_mix_g, v_ln_mix_b, v_w_ff1, v_w_ff2, v_ln_ff_g, v_ln_ff_b):
    big_w = (w_in[0], w_pool[0], w_branch_attn[0], w_branch_pool[0], w_out[0], w_ff1[0], w_ff2[0])
    big_m = (m_w_in[0], m_w_pool[0], m_w_branch_attn[0], m_w_branch_pool[0], m_w_out[0], m_w_ff1[0], m_w_ff2[0])
    big_v = (v_w_in[0], v_w_pool[0], v_w_branch_attn[0], v_w_branch_pool[0], v_w_out[0], v_w_ff1[0], v_w_ff2[0])
    shard_axes = (1, 1, 1, 1, 0, 1, 0)
    small_w = (pool_scale, ln_mix_g, ln_mix_b, ln_ff_g, ln_ff_b)
    small_m = (m_pool_scale, m_ln_mix_g, m_ln_mix_b, m_ln_ff_g, m_ln_ff_b)
    small_v = (v_pool_scale, v_ln_mix_g, v_ln_mix_b, v_ln_ff_g, v_ln_ff_b)

    names = ("w_in", "w_pool", "w_branch_attn", "w_branch_pool", "w_out", "w_ff1", "w_ff2")
    axis_of = dict(zip(names, shard_axes))
    shard_of = dict(zip(names, [w.astype(BF16) for w in big_w]))

    def full_buffer(n):
        s, ax = shard_of[n], axis_of[n]
        full = list(s.shape)
        full[ax] *= N_DEV
        return lax.empty(tuple(full), s.dtype)

    def gather_group(tag, group, after):
        ex = _TwoLevelGather(tag, [axis_of[n] for n in group], [shard_of[n].shape[axis_of[n]] for n in group])
        return ex, ex.start([shard_of[n] for n in group], [full_buffer(n) for n in group], after)

    in_ex, in_state = gather_group("gather_in", ("w_in",), None)
    mix_ex, mix_state = gather_group("gather_mix", ("w_pool", "w_branch_attn", "w_branch_pool", "w_out"),
                                     in_state[-1])
    ffn_ex, ffn_state = gather_group("gather_ffn", ("w_ff1", "w_ff2"), mix_state[-1])
    states = {"mix": mix_state, "ffn": ffn_state}
    me = 4 * lax.axis_index("x") + 2 * lax.axis_index("y") + lax.axis_index("c")
    block_cols = shard_of["w_in"].shape[1]
    neighbours, diagonal = (2, 4), (6,)

    def project_in(xb):
        n_rope_blocks = 2 * (x.shape[-1] // 2) // block_cols

        def piece(tag, w, blocks, h, **kw):
            return _in_proj_piece("in_proj_" + tag, xb, w, jnp.stack(blocks).astype(jnp.int32), cos2, sin_fwd, h,
                                  block_cols, n_rope_blocks, **kw)

        h = piece("own", shard_of["w_in"], [me], None, own_shard=True, after=ffn_state[-1])
        (w_in_land,), state = in_ex.wait(in_state, h, direct=(1,), tag="_sibling")
        h = piece("sibling", w_in_land, [me ^ 1], h)
        state = in_ex.forward(state, h, neighbours, tag="_neighbours")
        h = piece("neighbours", state[2][0], [me ^ r for r in neighbours], h)
        state = in_ex.forward(state, h, diagonal, tag="_diagonal")
        h = piece("diagonal", state[2][0], [me ^ r for r in diagonal], h)
        states["mix"] = mix_ex.forward(states["mix"], h, neighbours + diagonal)
        (w_in_land,), _ = in_ex.wait(state, h, passed_on=FORWARDED, sends=True, tag="_passed_on")
        h = piece("passed_on", w_in_land, [me ^ f for f in FORWARDED], h)
        return h, w_in_land

    def mix_weights(after):
        states["ffn"] = ffn_ex.forward(states["ffn"], after, neighbours + diagonal)
        return mix_ex.wait(states["mix"], after, direct=(1,), passed_on=FORWARDED, sends=True)[0]

    def ffn_weights(after):
        return ffn_ex.wait(states["ffn"], after, direct=(1,), passed_on=FORWARDED, sends=True)[0]

    groups = {"ff2": ("w_ff2",), "ff1": ("w_ff1",), "out": ("w_out",),
              "branch": ("w_branch_attn", "w_branch_pool"), "in": ("w_in", "w_pool", "small")}
    sent = {}

    def send(key, grads_):
        axes = [axis_of.get(n) for n in groups[key]]
        sizes = [None if ax is None else g.shape[ax] // N_DEV for g, ax in zip(grads_, axes)]
        lands = []
        for g, ax, size in zip(grads_, axes, sizes):
            shard = list(g.shape)
            if ax is not None:
                shard[ax] = size
            lands.append(lax.empty((N_DEV, *shard), g.dtype))
        ex = _scatter_exchange("scatter_" + key, axes, sizes)
        sent[key] = (ex, ex.start(list(grads_), lands))
        return sent[key][1][-1]

    cos2, sin_fwd, sin_bwd = _rope_tables(positions[0])
    loss, grad_x = _local_step(
        x[0], cos2, sin_fwd, sin_bwd, project_in, mix_weights, ffn_weights, pool_scale, ln_mix_g, ln_mix_b,
        ln_ff_g, ln_ff_b, loss_target[0], send)

    state = dict(zip(names, zip(big_w, big_m, big_v)))
    n_small = sum(w.shape[-1] for w in small_w)
    small_2d = (n_small // HEAD_DIM, HEAD_DIM)
    state["small"] = tuple(jnp.concatenate(t, axis=-1).reshape(small_2d) for t in (small_w, small_m, small_v))
    grads, deltas, new_ms, new_vs = {}, {}, {}, {}
    after = grad_x
    for key in ("ff2", "ff1", "out", "branch", "in"):
        ex, started = sent[key]
        for n, part in zip(groups[key], ex.wait(started, after)[0]):
            w, m, v = state[n]
            r2 = (-1, w.shape[-1])
            w2d = w.reshape(r2)
            res = _adamw("adamw_" + n, part.reshape((N_DEV,) + w2d.shape), w2d, m.reshape(r2), v.reshape(r2))
            after = res[0]
            if n == "small":
                small_out = [t.reshape(1, n_small) for t in res]
            else:
                grads[n], deltas[n], new_ms[n], new_vs[n] = (t.reshape((1,) + w.shape) for t in res)
    small_names = ("pool_scale", "ln_mix_g", "ln_mix_b", "ln_ff_g", "ln_ff_b")
    off = 0
    for n, w in zip(small_names, small_w):
        width = w.shape[-1]
        grads[n], deltas[n], new_ms[n], new_vs[n] = (t[:, off:off + width] for t in small_out)
        off += width

    order = ("w_in", "w_pool", "pool_scale", "w_branch_attn", "w_branch_pool", "w_out", "ln_mix_g", "ln_mix_b",
             "w_ff1", "w_ff2", "ln_ff_g", "ln_ff_b")
    total_loss = lax.psum(loss[0, 0], ("x", "y", "c"))
    return (total_loss, grad_x[None], *[grads[n] for n in order], *[deltas[n] for n in order],
            *[new_ms[n] for n in order], *[new_vs[n] for n in order])
```
